```python
import math
import jax, jax.numpy as jnp
from jax import lax
import numpy as np


D_MODEL = 1024
BATCH = 8
SEQ = 4096
DEPTH = 1

N_HEADS_SWA = 8
N_KV_HEADS_SWA = 2
N_HEADS_FOX = 8
HEAD_DIM = 64
WINDOW = 128
BLOCK = 128
NUM_BUCKETS = 32
MAX_DISTANCE = 128
N_GROUPS = 4
EXPERTS_PER_GROUP = 8
TOP_K = 2
D_FF_EXPERT = 256
FORGET_BIAS_INIT = 2.0
EPS = 1e-6
NEG_INF = -1e30

Q_A = N_HEADS_SWA * HEAD_DIM
KV_A = N_KV_HEADS_SWA * HEAD_DIM
W_B = N_HEADS_FOX * HEAD_DIM
IN_SPLITS = (Q_A, KV_A, KV_A, W_B, W_B, W_B, N_HEADS_FOX, 2 * D_MODEL)
IN_COLS = Q_A + 2 * KV_A + 3 * W_B + N_HEADS_FOX + 2 * D_MODEL

kernel_name = "hybrid_swa_sink_fox_hmoe_adaln_block"


def _rmsnorm(x, g):
    x32 = x.astype(jnp.float32)
    y = x32 * lax.rsqrt(jnp.mean(x32 * x32, axis=-1, keepdims=True) + EPS)
    return (y * g.astype(jnp.float32)).astype(x.dtype)


def _modulate(h, shift, scale):
    return h * (1 + scale[:, None, :]) + shift[:, None, :]


def _t5_bucket(dist):
    n = jnp.maximum(dist, 0)
    max_exact = NUM_BUCKETS // 2
    nf = jnp.maximum(n, 1).astype(jnp.float32)
    large = max_exact + (jnp.log(nf / max_exact) / math.log(MAX_DISTANCE / max_exact)
                         * (NUM_BUCKETS - max_exact)).astype(jnp.int32)
    large = jnp.minimum(large, NUM_BUCKETS - 1)
    return jnp.where(n < max_exact, n, large)


def _sliding_window_attention(q, k, v, sinks, rel_table):
    b, s, _ = q.shape
    nb = s // BLOCK
    grp = N_HEADS_SWA // N_KV_HEADS_SWA
    qb = q.reshape(b, nb, BLOCK, N_KV_HEADS_SWA, grp, HEAD_DIM)
    k = k.reshape(b, s, N_KV_HEADS_SWA, HEAD_DIM)
    v = v.reshape(b, s, N_KV_HEADS_SWA, HEAD_DIM)
    pad = ((0, 0), (BLOCK, 0), (0, 0), (0, 0))
    kp = jnp.pad(k, pad)
    vp = jnp.pad(v, pad)
    shp = (b, nb, BLOCK, N_KV_HEADS_SWA, HEAD_DIM)
    kb = jnp.concatenate([kp[:, :s].reshape(shp), kp[:, BLOCK:].reshape(shp)], axis=2)
    vb = jnp.concatenate([vp[:, :s].reshape(shp), vp[:, BLOCK:].reshape(shp)], axis=2)
    scores = jnp.einsum('bnqhgd,bnkhd->bnhgqk', qb, kb).astype(jnp.float32) * (HEAD_DIM ** -0.5)
    qi = jnp.arange(BLOCK)[:, None]
    kj = jnp.arange(2 * BLOCK)[None, :]
    dist = qi - kj + BLOCK
    band = (dist >= 0) & (dist < WINDOW)
    key_pos = jnp.arange(nb)[:, None, None] * BLOCK - BLOCK + kj[None]
    valid = band[None] & (key_pos >= 0)
    bias = rel_table[_t5_bucket(dist)].astype(jnp.float32)
    bias = jnp.transpose(bias, (2, 0, 1)).reshape(N_KV_HEADS_SWA, grp, BLOCK, 2 * BLOCK)
    scores = jnp.where(valid[None, :, None, None], scores + bias, NEG_INF)
    sink = jnp.broadcast_to(sinks.astype(jnp.float32).reshape(N_KV_HEADS_SWA, grp, 1, 1),
                            scores.shape[:-1] + (1,))
    probs = jax.nn.softmax(jnp.concatenate([scores, sink], axis=-1), axis=-1)[..., :-1]
    out = jnp.einsum('bnhgqk,bnkhd->bnqhgd', probs.astype(v.dtype), vb)
    return out.reshape(b, s, Q_A)


def _forgetting_attention(q, k, v, f_logit, b_forget):
    b, s, _ = q.shape
    nb = s // BLOCK
    q = q.reshape(b, s, N_HEADS_FOX, HEAD_DIM)
    k = k.reshape(b, s, N_HEADS_FOX, HEAD_DIM)
    v = v.reshape(b, s, N_HEADS_FOX, HEAD_DIM)
    log_f = jax.nn.log_sigmoid(f_logit.astype(jnp.float32) + b_forget.astype(jnp.float32))
    cum = jnp.cumsum(log_f, axis=1)
    cum_k = jnp.transpose(cum, (0, 2, 1))[:, :, None, :]
    qb = q.reshape(b, nb, BLOCK, N_HEADS_FOX, HEAD_DIM).transpose(1, 0, 2, 3, 4)
    cqb = cum.reshape(b, nb, BLOCK, N_HEADS_FOX).transpose(1, 0, 3, 2)
    kpos = jnp.arange(s)
    scale = HEAD_DIM ** -0.5

    def block(args):
        idx, qblk, cq = args
        sc = jnp.einsum('bqhd,bkhd->bhqk', qblk, k).astype(jnp.float32) * scale
        sc = sc + cq[..., None] - cum_k
        qpos = idx * BLOCK + jnp.arange(BLOCK)
        causal = kpos[None, :] <= qpos[:, None]
        p = jax.nn.softmax(jnp.where(causal, sc, NEG_INF), axis=-1)
        return jnp.einsum('bhqk,bkhd->bqhd', p.astype(v.dtype), v)

    out = lax.map(block, (jnp.arange(nb), qb, cqb))
    return out.transpose(1, 0, 2, 3, 4).reshape(b, s, W_B)


def _hierarchical_moe(h, w_rg, b_rg, w_re, b_re, w_g, w_u, w_d):
    bsz, s, d = h.shape
    t = h.reshape(-1, d)
    g_logits = (t @ w_rg + b_rg).astype(jnp.float32)
    g_prob = jax.nn.softmax(g_logits, axis=-1)
    gp, gi = lax.top_k(g_prob, 1)
    e_logits = (jnp.einsum('nd,gde->nge', t, w_re) + b_re).astype(jnp.float32)
    e_sel = jnp.take_along_axis(e_logits, gi[:, :, None], axis=1)[:, 0]
    ev, ei = lax.top_k(e_sel, TOP_K)
    ew = jax.nn.softmax(ev, axis=-1) * gp
    within = jnp.sum(jax.nn.one_hot(ei, EXPERTS_PER_GROUP, dtype=jnp.float32) * ew[..., None], axis=1)
    combine = jax.nn.one_hot(gi[:, 0], N_GROUPS, dtype=jnp.float32)[:, :, None] * within[:, None, :]
    y = jnp.zeros_like(t)
    for g in range(N_GROUPS):
        a = jnp.einsum('nd,edf->nef', t, w_g[g])
        u = jnp.einsum('nd,edf->nef', t, w_u[g])
        hid = jax.nn.silu(a) * u * combine[:, g, :, None].astype(t.dtype)
        y = y + jnp.einsum('nef,efd->nd', hid, w_d[g])
    return y.reshape(bsz, s, d)


def setup_inputs(seed: int = 0) -> dict:
    key = jax.random.key(seed)
    ks = jax.random.split(key, 22)

    def nrm(k, shape, scale):
        return jax.random.normal(k, shape, jnp.float32) * scale

    D, G, E, F = D_MODEL, N_GROUPS, EXPERTS_PER_GROUP, D_FF_EXPERT
    return {
        "x": nrm(ks[0], (BATCH, SEQ, D), 1.0),
        "c": nrm(ks[1], (BATCH, D), 1.0),
        "w_ada": nrm(ks[2], (DEPTH, D, 6 * D), 0.5 * D ** -0.5),
        "b_ada": nrm(ks[3], (DEPTH, 6 * D), 0.02),
        "g_norm_mix": 1.0 + nrm(ks[4], (DEPTH, D), 0.05),
        "g_norm_ffn": 1.0 + nrm(ks[5], (DEPTH, D), 0.05),
        "w_in": nrm(ks[6], (DEPTH, D, IN_COLS), D ** -0.5),
        "sinks": nrm(ks[7], (DEPTH, N_HEADS_SWA), 0.5),
        "b_forget": FORGET_BIAS_INIT + nrm(ks[8], (DEPTH, N_HEADS_FOX), 0.5),
        "w_proj_swa": nrm(ks[9], (DEPTH, Q_A, D), Q_A ** -0.5),
        "w_proj_fox": nrm(ks[10], (DEPTH, W_B, D), W_B ** -0.5),
        "w_out": nrm(ks[11], (DEPTH, D, D), D ** -0.5),
        "rel_bias_table": nrm(ks[12], (NUM_BUCKETS, N_HEADS_SWA), 0.5),
        "w_router_group": nrm(ks[13], (DEPTH, D, G), D ** -0.5),
        "b_router_group": nrm(ks[14], (DEPTH, G), 0.01),
        "w_router_expert": nrm(ks[15], (DEPTH, G, D, E), D ** -0.5),
        "b_router_expert": nrm(ks[16], (DEPTH, G, E), 0.01),
        "w_gate_exp": nrm(ks[17], (DEPTH, G, E, D, F), D ** -0.5),
        "w_up_exp": nrm(ks[18], (DEPTH, G, E, D, F), D ** -0.5),
        "w_down_exp": nrm(ks[19], (DEPTH, G, E, F, D), F ** -0.5),
        "g_final": 1.0 + nrm(ks[20], (D,), 0.05),
    }


def reference(x, c, w_ada, b_ada, g_norm_mix, g_norm_ffn, w_in, sinks, b_forget,
              w_proj_swa, w_proj_fox, w_out, rel_bias_table, w_router_group, b_router_group,
              w_router_expert, b_router_expert, w_gate_exp, w_up_exp, w_down_exp, g_final):
    offsets = []
    acc = 0
    for n in IN_SPLITS[:-1]:
        acc += n
        offsets.append(acc)
    c_act = jax.nn.silu(c)
    for l in range(DEPTH):
        mod = c_act @ w_ada[l] + b_ada[l]
        shift_m, scale_m, gate_m, shift_f, scale_f, gate_f = jnp.split(mod, 6, axis=-1)

        h = _modulate(_rmsnorm(x, g_norm_mix[l]), shift_m, scale_m)
        proj = h @ w_in[l]
        qa, ka, va, qb, kb, vb, f_logit, gates = jnp.split(proj, offsets, axis=-1)
        o_a = _sliding_window_attention(qa, ka, va, sinks[l], rel_bias_table)
        o_b = _forgetting_attention(qb, kb, vb, f_logit, b_forget[l])
        gate_a = jax.nn.sigmoid(gates[..., :D_MODEL])
        gate_b = jax.nn.sigmoid(gates[..., D_MODEL:])
        merged = gate_a * (o_a @ w_proj_swa[l]) + gate_b * (o_b @ w_proj_fox[l])
        x = x + gate_m[:, None, :] * (merged @ w_out[l])

        h2 = _modulate(_rmsnorm(x, g_norm_ffn[l]), shift_f, scale_f)
        y = _hierarchical_moe(h2, w_router_group[l], b_router_group[l], w_router_expert[l],
                              b_router_expert[l], w_gate_exp[l], w_up_exp[l], w_down_exp[l])
        x = x + gate_f[:, None, :] * y
    return _rmsnorm(x, g_final)
```

```python
import functools
import math

import numpy as np
import jax
import jax.numpy as jnp
from jax import lax
from jax.experimental import pallas as pl
from jax.experimental.pallas import tpu as pltpu

F32 = jnp.float32
BF16 = jnp.bfloat16
HIGHEST = lax.Precision.HIGHEST

D_MODEL = 1024
BATCH = 8
SEQ = 4096
N_TOK = BATCH * SEQ
N_HEADS_SWA = 8
N_KV_HEADS_SWA = 2
N_HEADS_FOX = 8
HEAD_DIM = 64
WINDOW = 128
BLOCK = 128
NUM_BUCKETS = 32
MAX_DISTANCE = 128
N_GROUPS = 4
EXPERTS_PER_GROUP = 8
N_EXPERTS = N_GROUPS * EXPERTS_PER_GROUP
D_FF_EXPERT = 256
EPS = 1e-6
NEG_INF = -1e30

Q_A = N_HEADS_SWA * HEAD_DIM
KV_A = N_KV_HEADS_SWA * HEAD_DIM
W_B = N_HEADS_FOX * HEAD_DIM
LANES = 128
QK_SCALE = HEAD_DIM ** -0.5

TM_IN = 512
TM_POST = 512
TQ_FOX = 256
TK_FOX = TQ_FOX
TM_EXP = 256
TM_ROW = 512
P_ROWS = 2 * N_TOK + N_EXPERTS * TM_EXP
VMEM_LIMIT = 56 * 1024 * 1024

DECAY_LANES = 6


def _cparams(n_axes):
    return pltpu.CompilerParams(dimension_semantics=("arbitrary",) * n_axes,
                                vmem_limit_bytes=VMEM_LIMIT)


def _ada_kernel(c_ref, w_ref, b_ref, o_ref):
    c = c_ref[...]
    ca = c * jax.nn.sigmoid(c)
    o_ref[...] = jnp.dot(ca.astype(BF16), w_ref[...].astype(BF16),
                         preferred_element_type=F32) + b_ref[...]


def _ada(c16, w_ada, b_ada):
    n_out = w_ada.shape[1]
    blk = 1024
    return pl.pallas_call(
        _ada_kernel,
        out_shape=jax.ShapeDtypeStruct((16, n_out), F32),
        grid=(n_out // blk,),
        in_specs=[pl.BlockSpec((16, D_MODEL), lambda j: (0, 0)),
                  pl.BlockSpec((D_MODEL, blk), lambda j: (0, j)),
                  pl.BlockSpec((1, blk), lambda j: (0, j))],
        out_specs=pl.BlockSpec((16, blk), lambda j: (0, j)),
        compiler_params=_cparams(1),
        name="ada",
    )(c16, w_ada, b_ada)


def _inproj_kernel(x_ref, sc_ref, sh_ref, g_ref, wm_ref, wf_ref, wg_ref,
                   qa_ref, kd_ref, vd_ref, qb_ref, kb_ref, vb_ref, f_ref, gt_ref):
    x = x_ref[...]
    rs = lax.rsqrt(jnp.mean(x * x, axis=-1, keepdims=True) + EPS)
    a = g_ref[...] * (1.0 + sc_ref[...])
    h = (x * rs * a + sh_ref[...]).astype(BF16)

    def mm(w):
        return jnp.dot(h, w, preferred_element_type=F32)

    qa_ref[...] = (mm(wm_ref[:, 0:512]) * QK_SCALE).astype(BF16)
    kd_ref[...] = mm(wm_ref[:, 512:768]).astype(BF16)
    vd_ref[...] = mm(wm_ref[:, 768:1024]).astype(BF16)
    qb_ref[...] = (mm(wm_ref[:, 1024:1536]) * QK_SCALE).astype(BF16)
    kb_ref[...] = mm(wm_ref[:, 1536:2048]).astype(BF16)
    vb_ref[...] = mm(wm_ref[:, 2048:2560]).astype(BF16)
    f_ref[...] = mm(wf_ref[...])
    gt_ref[...] = mm(wg_ref[...]).astype(BF16)


def _inproj(x2, scale_m, shift_m, g_mix, w_main, w_f, w_g):
    tm = TM_IN
    tpb = SEQ // tm
    row = lambda i: (i, 0)
    per_b = lambda i: (i // tpb, 0, 0)
    const = lambda i: (0, 0)
    outs = [(Q_A, BF16), (2 * KV_A, BF16), (2 * KV_A, BF16), (W_B, BF16), (W_B, BF16), (W_B, BF16),
            (LANES, F32), (2 * D_MODEL, BF16)]
    return pl.pallas_call(
        _inproj_kernel,
        out_shape=[jax.ShapeDtypeStruct((N_TOK, w), dt) for w, dt in outs],
        grid=(N_TOK // tm,),
        in_specs=[pl.BlockSpec((tm, D_MODEL), row),
                  pl.BlockSpec((None, 1, D_MODEL), per_b),
                  pl.BlockSpec((None, 1, D_MODEL), per_b),
                  pl.BlockSpec((1, D_MODEL), const),
                  pl.BlockSpec(w_main.shape, const),
                  pl.BlockSpec(w_f.shape, const),
                  pl.BlockSpec(w_g.shape, const)],
        out_specs=[pl.BlockSpec((tm, w), row) for w, _ in outs],
        compiler_params=_cparams(1),
        name="inproj",
    )(x2, scale_m, shift_m, g_mix, w_main, w_f, w_g)


def _log_sigmoid(x):
    return jnp.minimum(x, 0.0) - jnp.log1p(jnp.exp(-jnp.abs(x)))


def _cum_kernel(f_ref, b_ref, jm_ref, qa_ref, ka_ref):
    r = lax.broadcasted_iota(jnp.int32, (LANES, LANES), 0)
    c = lax.broadcasted_iota(jnp.int32, (LANES, LANES), 1)
    expand = ((c >= DECAY_LANES * r) & (c < DECAY_LANES * r + DECAY_LANES)
              & (r < N_HEADS_FOX)).astype(F32)
    tril = (r >= c).astype(F32)
    lf = _log_sigmoid(f_ref[...] + b_ref[...])
    lfe = jnp.dot(lf, expand, precision=HIGHEST, preferred_element_type=F32)
    jm = jm_ref[...]
    carry = jnp.zeros((1, LANES), F32)
    for blk in range(SEQ // LANES):
        rows = slice(blk * LANES, (blk + 1) * LANES)
        cb = jnp.dot(tril, lfe[rows], precision=HIGHEST, preferred_element_type=F32) + carry
        carry = cb[LANES - 1:LANES]
        hi = cb.astype(BF16).astype(F32)
        r1 = cb - hi
        mid = r1.astype(BF16).astype(F32)
        lo = (r1 - mid).astype(BF16).astype(F32)
        one = jnp.ones_like(cb)
        zero = jnp.zeros_like(cb)
        qa = jnp.where(jm == 0, hi, jnp.where(jm == 1, mid, jnp.where(jm == 2, lo,
                       jnp.where(jm < DECAY_LANES, one, zero))))
        ka = jnp.where(jm == 3, -hi, jnp.where(jm == 4, -mid, jnp.where(jm == 5, -lo,
                       jnp.where(jm < 3, one, zero))))
        qa_ref[rows, :] = qa.astype(BF16)
        ka_ref[rows, :] = ka.astype(BF16)


def _cum(f_pad, b_pad, jmod):
    return pl.pallas_call(
        _cum_kernel,
        out_shape=[jax.ShapeDtypeStruct((BATCH, SEQ, LANES), BF16)] * 2,
        grid=(BATCH,),
        in_specs=[pl.BlockSpec((SEQ, LANES), lambda b: (b, 0)),
                  pl.BlockSpec((1, LANES), lambda b: (0, 0)),
                  pl.BlockSpec((1, LANES), lambda b: (0, 0))],
        out_specs=[pl.BlockSpec((None, SEQ, LANES), lambda b: (b, 0, 0))] * 2,
        compiler_params=_cparams(1),
        name="cum",
    )(f_pad, b_pad, jmod)


def _swa_kernel(sink_ref, q_ref, kc_ref, kp_ref, vc_ref, vp_ref, bias_ref, o_ref):
    i = pl.program_id(1)
    lane = lax.broadcasted_iota(jnp.int32, (BLOCK, LANES), 1)
    lo = lane < HEAD_DIM
    col = lax.broadcasted_iota(jnp.int32, (BLOCK, 2 * BLOCK), 1)
    dead = jnp.logical_and(i == 0, col < BLOCK)
    kk = jnp.concatenate([kp_ref[...], kc_ref[...]], axis=0)
    vv = jnp.concatenate([vp_ref[...], vc_ref[...]], axis=0)
    grp = N_HEADS_SWA // N_KV_HEADS_SWA
    for g in range(N_KV_HEADS_SWA):
        parts = []
        for t in range(2):
            qt = q_ref[:, (2 * g + t) * LANES:(2 * g + t + 1) * LANES]
            zero = jnp.zeros_like(qt)
            parts.append(jnp.where(lo, qt, zero))
            parts.append(jnp.where(lo, zero, qt))
        q4 = jnp.concatenate(parts, axis=0)
        s = lax.dot_general(q4, kk[:, g * LANES:(g + 1) * LANES], (((1,), (1,)), ((), ())),
                            preferred_element_type=F32)
        s = s + bias_ref[g]
        outs = []
        for hh in range(grp):
            sink = sink_ref[g * grp + hh]
            sl = s[hh * BLOCK:(hh + 1) * BLOCK]
            sl = jnp.where(dead, NEG_INF, sl)
            m = jnp.maximum(jnp.max(sl, axis=-1, keepdims=True), sink)
            p = jnp.exp(sl - m)
            den = jnp.sum(p, axis=-1, keepdims=True) + jnp.exp(sink - m)
            o = jnp.dot(p.astype(BF16), vv[:, g * LANES:(g + 1) * LANES],
                        preferred_element_type=F32)
            outs.append(o / den)
        o_ref[:, (2 * g) * LANES:(2 * g + 1) * LANES] = jnp.where(lo, outs[0], outs[1]).astype(BF16)
        o_ref[:, (2 * g + 1) * LANES:(2 * g + 2) * LANES] = jnp.where(lo, outs[2], outs[3]).astype(BF16)


def _swa(sinks, qa, kdup, vdup, bias):
    nb = SEQ // BLOCK
    cur = lambda b, i, s: (b * nb + i, 0)
    prev = lambda b, i, s: (b * nb + jnp.maximum(i - 1, 0), 0)
    grid_spec = pltpu.PrefetchScalarGridSpec(
        num_scalar_prefetch=1,
        grid=(BATCH, nb),
        in_specs=[pl.BlockSpec((BLOCK, Q_A), cur),
                  pl.BlockSpec((BLOCK, 2 * KV_A), cur),
                  pl.BlockSpec((BLOCK, 2 * KV_A), prev),
                  pl.BlockSpec((BLOCK, 2 * KV_A), cur),
                  pl.BlockSpec((BLOCK, 2 * KV_A), prev),
                  pl.BlockSpec(bias.shape, lambda b, i, s: (0, 0, 0))],
        out_specs=pl.BlockSpec((BLOCK, Q_A), cur))
    return pl.pallas_call(
        _swa_kernel,
        out_shape=jax.ShapeDtypeStruct((N_TOK, Q_A), BF16),
        grid_spec=grid_spec,
        compiler_params=_cparams(2),
        name="swa",
    )(sinks, qa, kdup, kdup, vdup, vdup, bias)


def _fox_kernel(q_ref, k_ref, v_ref, qa_ref, ka_ref, o_ref, kaug, q2, m_sc, l_sc, acc_sc):
    tq, tk = TQ_FOX, TK_FOX
    t = pl.program_id(1)
    i = pl.program_id(2)

    @pl.when(i == 0)
    def _():
        kaug[:, 0:LANES] = k_ref[...]
        kaug[:, LANES:2 * LANES] = ka_ref[...]

    lane = lax.broadcasted_iota(jnp.int32, (tq, LANES), 1)
    lo = lane < HEAD_DIM
    base = 2 * DECAY_LANES * t
    own0 = (lane >= base) & (lane < base + DECAY_LANES)
    own1 = (lane >= base + DECAY_LANES) & (lane < base + 2 * DECAY_LANES)
    q = q_ref[...]
    qa = qa_ref[...]
    zero = jnp.zeros_like(q)
    q2[0:tq, 0:LANES] = jnp.where(lo, q, zero)
    q2[0:tq, LANES:2 * LANES] = jnp.where(own0, qa, zero)
    q2[tq:2 * tq, 0:LANES] = jnp.where(lo, zero, q)
    q2[tq:2 * tq, LANES:2 * LANES] = jnp.where(own1, qa, zero)
    m_sc[...] = jnp.full(m_sc.shape, NEG_INF, F32)
    l_sc[...] = jnp.zeros(l_sc.shape, F32)
    acc_sc[...] = jnp.zeros(acc_sc.shape, F32)

    def update(s, ks):
        m_prev = m_sc[...]
        m_new = jnp.maximum(m_prev, jnp.max(s, axis=-1, keepdims=True))
        alpha = jnp.exp(m_prev - m_new)
        p = jnp.exp(s - m_new)
        l_sc[...] = alpha * l_sc[...] + jnp.sum(p, axis=-1, keepdims=True)
        acc_sc[...] = alpha * acc_sc[...] + jnp.dot(p.astype(BF16), v_ref[pl.ds(ks, tk), :],
                                                    preferred_element_type=F32)
        m_sc[...] = m_new

    def scores(ks):
        return lax.dot_general(q2[...], kaug[pl.ds(ks, tk), :], (((1,), (1,)), ((), ())),
                               preferred_element_type=F32)

    def step(j, carry):
        ks = pl.multiple_of(j * tk, tk)
        update(scores(ks), ks)
        return carry

    lax.fori_loop(0, i, step, 0)

    ks = pl.multiple_of(i * tk, tk)
    s = scores(ks)
    rr = lax.broadcasted_iota(jnp.int32, (2 * tq, tk), 0)
    cc = lax.broadcasted_iota(jnp.int32, (2 * tq, tk), 1)
    rr = jnp.where(rr >= tq, rr - tq, rr)
    update(jnp.where(cc <= rr, s, NEG_INF), ks)

    o = acc_sc[...] / l_sc[...]
    o_ref[...] = jnp.where(lo, o[0:tq], o[tq:2 * tq]).astype(BF16)


def _fox(qb, kb, vb, qa, ka):
    tq = TQ_FOX
    nq = SEQ // tq
    n_pairs = N_HEADS_FOX // 2
    qmap = lambda b, t, i: (b * nq + i, t)
    kmap = lambda b, t, i: (b, t)
    return pl.pallas_call(
        _fox_kernel,
        out_shape=jax.ShapeDtypeStruct((N_TOK, W_B), BF16),
        grid=(BATCH, n_pairs, nq),
        in_specs=[pl.BlockSpec((tq, LANES), qmap),
                  pl.BlockSpec((SEQ, LANES), kmap),
                  pl.BlockSpec((SEQ, LANES), kmap),
                  pl.BlockSpec((None, tq, LANES), lambda b, t, i: (b, i, 0)),
                  pl.BlockSpec((None, SEQ, LANES), lambda b, t, i: (b, 0, 0))],
        out_specs=pl.BlockSpec((tq, LANES), qmap),
        scratch_shapes=[pltpu.VMEM((SEQ, 2 * LANES), BF16),
                        pltpu.VMEM((2 * tq, 2 * LANES), BF16),
                        pltpu.VMEM((2 * tq, 1), F32),
                        pltpu.VMEM((2 * tq, 1), F32),
                        pltpu.VMEM((2 * tq, LANES), F32)],
        compiler_params=_cparams(3),
        name="fox",
    )(qb, kb, vb, qa, ka)


def _post_kernel(x_ref, oa_ref, ob_ref, gt_ref, gm_ref, sc_ref, sh_ref, g_ref,
                 wa_ref, wb_ref, wo_ref, wrh_ref, wrl_ref, br_ref,
                 x1_ref, h2_ref, rc_ref, cnt_ref, carry):
    tm = TM_POST
    step = pl.program_id(0)

    @pl.when(step == 0)
    def _():
        carry[...] = jnp.zeros(carry.shape, F32)

    pa = jnp.dot(oa_ref[...], wa_ref[...], preferred_element_type=F32)
    pb = jnp.dot(ob_ref[...], wb_ref[...], preferred_element_type=F32)
    ga = jax.nn.sigmoid(gt_ref[:, 0:D_MODEL].astype(F32))
    gb = jax.nn.sigmoid(gt_ref[:, D_MODEL:2 * D_MODEL].astype(F32))
    merged = (ga * pa + gb * pb).astype(BF16)
    y = jnp.dot(merged, wo_ref[...], preferred_element_type=F32)
    x1 = x_ref[...] + gm_ref[...] * y
    x1_ref[...] = x1

    rs = lax.rsqrt(jnp.mean(x1 * x1, axis=-1, keepdims=True) + EPS)
    a = g_ref[...] * (1.0 + sc_ref[...])
    h2 = x1 * rs * a + sh_ref[...]
    h2_ref[...] = h2

    hh = h2.astype(BF16)
    hl = (h2 - hh.astype(F32)).astype(BF16)
    logits = (jnp.dot(hh, wrh_ref[...], preferred_element_type=F32)
              + jnp.dot(hl, wrh_ref[...], preferred_element_type=F32)
              + jnp.dot(hh, wrl_ref[...], preferred_element_type=F32)
              + br_ref[...])

    lane = lax.broadcasted_iota(jnp.int32, (tm, LANES), 1).astype(F32)
    big = float(LANES)
    gl = jnp.where(lane < N_GROUPS, logits, -jnp.inf)
    gmax = jnp.max(gl, axis=-1, keepdims=True)
    gi = jnp.min(jnp.where(gl == gmax, lane, big), axis=-1, keepdims=True)
    gsum = jnp.sum(jnp.exp(gl - gmax), axis=-1, keepdims=True)
    gp = 1.0 / gsum
    e_lo = N_GROUPS + EXPERTS_PER_GROUP * gi
    el = jnp.where((lane >= e_lo) & (lane < e_lo + EXPERTS_PER_GROUP), logits, -jnp.inf)
    v1 = jnp.max(el, axis=-1, keepdims=True)
    i1 = jnp.min(jnp.where(el == v1, lane, big), axis=-1, keepdims=True)
    el2 = jnp.where(lane == i1, -jnp.inf, el)
    v2 = jnp.max(el2, axis=-1, keepdims=True)
    i2 = jnp.min(jnp.where(el2 == v2, lane, big), axis=-1, keepdims=True)
    e21 = jnp.exp(v2 - v1)
    w1 = gp / (1.0 + e21)
    w2 = gp * e21 / (1.0 + e21)
    e1 = i1 - N_GROUPS
    e2 = i2 - N_GROUPS

    oh = jnp.where((lane == e1) | (lane == e2), 1.0, 0.0)
    rr = lax.broadcasted_iota(jnp.int32, (tm, tm), 0)
    cc = lax.broadcasted_iota(jnp.int32, (tm, tm), 1)
    strict = jnp.where(rr > cc, 1.0, 0.0).astype(BF16)
    before = jnp.dot(strict, oh.astype(BF16), preferred_element_type=F32) + carry[...]
    rank1 = jnp.sum(jnp.where(lane == e1, before, 0.0), axis=-1, keepdims=True)
    rank2 = jnp.sum(jnp.where(lane == e2, before, 0.0), axis=-1, keepdims=True)
    total = carry[...] + jnp.sum(oh, axis=0, keepdims=True)
    carry[...] = total
    cnt_ref[...] = jnp.broadcast_to(total, cnt_ref.shape)

    rc = jnp.where(lane == 0, e1, jnp.where(lane == 1, e2, jnp.where(lane == 2, rank1,
         jnp.where(lane == 3, rank2, jnp.where(lane == 4, w1, jnp.where(lane == 5, w2, 0.0))))))
    rc_ref[...] = rc


def _post(x2, oa, ob, gates, gate_m, scale_f, shift_f, g_ffn, wa, wb, wo, wr_hi, wr_lo, b_r):
    tm = TM_POST
    tpb = SEQ // tm
    row = lambda i: (i, 0)
    per_b = lambda i: (i // tpb, 0, 0)
    const = lambda i: (0, 0)
    return pl.pallas_call(
        _post_kernel,
        out_shape=[jax.ShapeDtypeStruct((N_TOK, D_MODEL), F32),
                   jax.ShapeDtypeStruct((N_TOK, D_MODEL), F32),
                   jax.ShapeDtypeStruct((N_TOK, LANES), F32),
                   jax.ShapeDtypeStruct((8, LANES), F32)],
        grid=(N_TOK // tm,),
        in_specs=[pl.BlockSpec((tm, D_MODEL), row),
                  pl.BlockSpec((tm, Q_A), row),
                  pl.BlockSpec((tm, W_B), row),
                  pl.BlockSpec((tm, 2 * D_MODEL), row),
                  pl.BlockSpec((None, 1, D_MODEL), per_b),
                  pl.BlockSpec((None, 1, D_MODEL), per_b),
                  pl.BlockSpec((None, 1, D_MODEL), per_b),
                  pl.BlockSpec((1, D_MODEL), const),
                  pl.BlockSpec(wa.shape, const),
                  pl.BlockSpec(wb.shape, const),
                  pl.BlockSpec(wo.shape, const),
                  pl.BlockSpec(wr_hi.shape, const),
                  pl.BlockSpec(wr_lo.shape, const),
                  pl.BlockSpec((1, LANES), const)],
        out_specs=[pl.BlockSpec((tm, D_MODEL), row),
                   pl.BlockSpec((tm, D_MODEL), row),
                   pl.BlockSpec((tm, LANES), row),
                   pl.BlockSpec((8, LANES), const)],
        scratch_shapes=[pltpu.VMEM((1, LANES), F32)],
        compiler_params=_cparams(1),
        name="post",
    )(x2, oa, ob, gates, gate_m, scale_f, shift_f, g_ffn, wa, wb, wo, wr_hi, wr_lo, b_r)


def _row_index(idx, slot, n):
    flat = slot * TM_ROW + n
    return idx[flat // LANES, flat % LANES]


def _dispatch_kernel(pos_hbm, h_ref, xs_in, xs_hbm, idx, isem, sem):
    del xs_in
    i = pl.program_id(0)
    cp = pltpu.make_async_copy(pos_hbm.at[i], idx, isem)
    cp.start()
    cp.wait()

    def body(n, carry):
        for slot in range(2):
            p = _row_index(idx, slot, n)
            pltpu.make_async_copy(h_ref.at[pl.ds(n, 1)], xs_hbm.at[pl.ds(p, 1)], sem).start()
        return carry

    lax.fori_loop(0, TM_ROW, body, 0)
    for slot in range(2):
        pltpu.make_async_copy(h_ref, xs_hbm.at[pl.ds(0, TM_ROW)], sem).wait()


def _dispatch(pos_tiles, h2, xs_init):
    return pl.pallas_call(
        _dispatch_kernel,
        out_shape=jax.ShapeDtypeStruct(xs_init.shape, xs_init.dtype),
        grid=(N_TOK // TM_ROW,),
        in_specs=[pl.BlockSpec(memory_space=pl.ANY),
                  pl.BlockSpec((TM_ROW, D_MODEL), lambda i: (i, 0)),
                  pl.BlockSpec(memory_space=pl.ANY)],
        out_specs=pl.BlockSpec(memory_space=pl.ANY),
        scratch_shapes=[pltpu.SMEM((8, LANES), jnp.int32),
                        pltpu.SemaphoreType.DMA,
                        pltpu.SemaphoreType.DMA],
        input_output_aliases={2: 0},
        compiler_params=_cparams(1),
        name="dispatch",
    )(pos_tiles, h2, xs_init)


def _experts_kernel(te_ref, nv_ref, xs_ref, wg_ref, wu_ref, wd_ref, ys_ref):
    r = pl.program_id(0)

    @pl.when(r < nv_ref[0])
    def _():
        x = xs_ref[...].astype(BF16)
        a = jnp.dot(x, wg_ref[...].astype(BF16), preferred_element_type=F32)
        u = jnp.dot(x, wu_ref[...].astype(BF16), preferred_element_type=F32)
        hid = (a * jax.nn.sigmoid(a) * u).astype(BF16)
        ys_ref[...] = jnp.dot(hid, wd_ref[...].astype(BF16), preferred_element_type=F32)

    @pl.when(r >= nv_ref[0])
    def _():
        ys_ref[...] = jnp.zeros(ys_ref.shape, F32)


def _experts(tile_expert, n_valid, xs, wg, wu, wd):
    tm = TM_EXP
    rows = lambda r, te, nv: (jnp.minimum(r, nv[0] - 1), 0)
    rows_out = lambda r, te, nv: (r, 0)
    wmap = lambda r, te, nv: (te[r], 0, 0)
    grid_spec = pltpu.PrefetchScalarGridSpec(
        num_scalar_prefetch=2,
        grid=(P_ROWS // tm,),
        in_specs=[pl.BlockSpec((tm, D_MODEL), rows),
                  pl.BlockSpec((None, D_MODEL, D_FF_EXPERT), wmap),
                  pl.BlockSpec((None, D_MODEL, D_FF_EXPERT), wmap),
                  pl.BlockSpec((None, D_FF_EXPERT, D_MODEL), wmap)],
        out_specs=pl.BlockSpec((tm, D_MODEL), rows_out))
    return pl.pallas_call(
        _experts_kernel,
        out_shape=jax.ShapeDtypeStruct((P_ROWS, D_MODEL), F32),
        grid_spec=grid_spec,
        compiler_params=_cparams(1),
        name="experts",
    )(tile_expert, n_valid, xs, wg, wu, wd)


def _combine_kernel(pos_hbm, x1_ref, rc_ref, gf_ref, gfin_ref, ys_hbm, o_ref,
                    idx, buf_a, buf_b, isem, sems):
    i = pl.program_id(0)
    cp = pltpu.make_async_copy(pos_hbm.at[i], idx, isem)
    cp.start()
    cp.wait()
    bufs = (buf_a, buf_b)

    def body(n, carry):
        for slot in range(2):
            p = _row_index(idx, slot, n)
            pltpu.make_async_copy(ys_hbm.at[pl.ds(p, 1)], bufs[slot].at[pl.ds(n, 1)],
                                  sems.at[slot]).start()
        return carry

    lax.fori_loop(0, TM_ROW, body, 0)
    for slot in range(2):
        pltpu.make_async_copy(ys_hbm.at[pl.ds(0, TM_ROW)], bufs[slot], sems.at[slot]).wait()

    w1 = rc_ref[:, 4:5]
    w2 = rc_ref[:, 5:6]
    y = w1 * buf_a[...] + w2 * buf_b[...]
    xf = x1_ref[...] + gf_ref[...] * y
    rs = lax.rsqrt(jnp.mean(xf * xf, axis=-1, keepdims=True) + EPS)
    o_ref[...] = xf * rs * gfin_ref[...]


def _combine(pos_tiles, x1, rcol, gate_f, g_final, ys):
    tm = TM_ROW
    tpb = SEQ // tm
    row = lambda i: (i, 0)
    return pl.pallas_call(
        _combine_kernel,
        out_shape=jax.ShapeDtypeStruct((N_TOK, D_MODEL), F32),
        grid=(N_TOK // tm,),
        in_specs=[pl.BlockSpec(memory_space=pl.ANY),
                  pl.BlockSpec((tm, D_MODEL), row),
                  pl.BlockSpec((tm, LANES), row),
                  pl.BlockSpec((None, 1, D_MODEL), lambda i: (i // tpb, 0, 0)),
                  pl.BlockSpec((1, D_MODEL), lambda i: (0, 0)),
                  pl.BlockSpec(memory_space=pl.ANY)],
        out_specs=pl.BlockSpec((tm, D_MODEL), row),
        scratch_shapes=[pltpu.SMEM((8, LANES), jnp.int32),
                        pltpu.VMEM((tm, D_MODEL), F32),
                        pltpu.VMEM((tm, D_MODEL), F32),
                        pltpu.SemaphoreType.DMA,
                        pltpu.SemaphoreType.DMA((2,))],
        compiler_params=_cparams(1),
        name="combine",
    )(pos_tiles, x1, rcol, gate_f, g_final, ys)


def _t5_bucket_np():
    qi = np.arange(BLOCK)[:, None]
    kj = np.arange(2 * BLOCK)[None, :]
    dist = qi - kj + BLOCK
    n = np.maximum(dist, 0)
    max_exact = NUM_BUCKETS // 2
    nf = np.maximum(n, 1).astype(np.float32)
    large = max_exact + (np.log(nf / np.float32(max_exact)) / np.float32(math.log(MAX_DISTANCE / max_exact))
                         * np.float32(NUM_BUCKETS - max_exact)).astype(np.int32)
    large = np.minimum(large, NUM_BUCKETS - 1)
    bucket = np.where(n < max_exact, n, large)
    band = (dist >= 0) & (dist < WINDOW)
    return bucket.astype(np.int32), band


def kernel(x, c, w_ada, b_ada, g_norm_mix, g_norm_ffn, w_in, sinks, b_forget, w_proj_swa, w_proj_fox,
           w_out, rel_bias_table, w_router_group, b_router_group, w_router_expert, b_router_expert,
           w_gate_exp, w_up_exp, w_down_exp, g_final):
    l = 0
    x2 = x.reshape(N_TOK, D_MODEL)

    c16 = jnp.concatenate([c, jnp.zeros_like(c)], axis=0)
    mod = _ada(c16, w_ada[l], b_ada[l][None, :])[:BATCH]
    shift_m, scale_m, gate_m, shift_f, scale_f, gate_f = [
        m.reshape(BATCH, 1, D_MODEL) for m in jnp.split(mod, 6, axis=-1)]

    w = w_in[l]
    o_ka, o_va, o_qb = Q_A, Q_A + KV_A, Q_A + 2 * KV_A
    o_kb, o_vb, o_f = o_qb + W_B, o_qb + 2 * W_B, o_qb + 3 * W_B
    o_g = o_f + N_HEADS_FOX

    def dup(cols):
        heads = [cols[:, h * HEAD_DIM:(h + 1) * HEAD_DIM] for h in range(N_KV_HEADS_SWA)]
        return jnp.concatenate([hd for hd in heads for _ in range(2)], axis=1)

    w_main = jnp.concatenate([w[:, :Q_A], dup(w[:, o_ka:o_va]), dup(w[:, o_va:o_qb]),
                              w[:, o_qb:o_f]], axis=1).astype(BF16)
    w_f = jnp.pad(w[:, o_f:o_g], ((0, 0), (0, LANES - N_HEADS_FOX))).astype(BF16)
    w_g = w[:, o_g:].astype(BF16)
    qa, kdup, vdup, qb, kb, vb, f_pad, gates = _inproj(
        x2, scale_m, shift_m, g_norm_mix[l][None, :], w_main, w_f, w_g)

    b_pad = jnp.pad(b_forget[l], (0, LANES - N_HEADS_FOX))[None, :]
    lanes = np.arange(LANES)
    jmod = jnp.asarray(np.where(lanes < DECAY_LANES * N_HEADS_FOX, lanes % DECAY_LANES, 7)[None, :].astype(np.int32))
    dq, dk = _cum(f_pad, b_pad, jmod)

    bucket, band = _t5_bucket_np()
    bias = jnp.transpose(rel_bias_table[bucket].astype(F32), (2, 0, 1))
    bias = jnp.where(band[None], bias, NEG_INF).reshape(N_KV_HEADS_SWA, -1, 2 * BLOCK)
    o_a = _swa(sinks[l].astype(F32), qa, kdup, vdup, bias)

    o_b = _fox(qb, kb, vb, dq, dk)

    w_r = jnp.concatenate([w_router_group[l]] + [w_router_expert[l][g] for g in range(N_GROUPS)], axis=1)
    w_r = jnp.pad(w_r, ((0, 0), (0, LANES - w_r.shape[1])))
    wr_hi = w_r.astype(BF16)
    wr_lo = (w_r - wr_hi.astype(F32)).astype(BF16)
    b_r = jnp.concatenate([b_router_group[l], b_router_expert[l].reshape(-1)])
    b_r = jnp.pad(b_r, (0, LANES - b_r.shape[0]))[None, :]
    x1, h2, rcol, cnt = _post(x2, o_a, o_b, gates, gate_m, scale_f, shift_f, g_norm_ffn[l][None, :],
                              w_proj_swa[l].astype(BF16), w_proj_fox[l].astype(BF16), w_out[l].astype(BF16),
                              wr_hi, wr_lo, b_r)

    counts = cnt[0, :N_EXPERTS].astype(jnp.int32)
    padded = ((counts + TM_EXP - 1) // TM_EXP) * TM_EXP
    ends = jnp.cumsum(padded)
    offs = ends - padded
    e12 = rcol[:, 0:2].astype(jnp.int32)
    r12 = rcol[:, 2:4].astype(jnp.int32)
    pos = jnp.take(offs, e12) + r12
    pos_tiles = pos.reshape(N_TOK // TM_ROW, TM_ROW, 2).transpose(0, 2, 1).reshape(N_TOK // TM_ROW, 8, LANES)
    n_tiles = P_ROWS // TM_EXP
    tile_start = jnp.arange(n_tiles, dtype=jnp.int32) * TM_EXP
    tile_expert = jnp.minimum(jnp.searchsorted(ends, tile_start, side="right"), N_EXPERTS - 1).astype(jnp.int32)
    n_valid = (ends[-1:] // TM_EXP).astype(jnp.int32)

    xs = _dispatch(pos_tiles, h2, jnp.zeros((P_ROWS, D_MODEL), F32))
    ys = _experts(tile_expert, n_valid, xs,
                  w_gate_exp[l].reshape(N_EXPERTS, D_MODEL, D_FF_EXPERT),
                  w_up_exp[l].reshape(N_EXPERTS, D_MODEL, D_FF_EXPERT),
                  w_down_exp[l].reshape(N_EXPERTS, D_FF_EXPERT, D_MODEL))
    out = _combine(pos_tiles, x1, rcol, gate_f, g_final[None, :], ys)
    return out.reshape(BATCH, SEQ, D_MODEL)
```

```python
import functools
import math

import numpy as np
import jax
import jax.numpy as jnp
from jax import lax
from jax.experimental import pallas as pl
from jax.experimental.pallas import tpu as pltpu

F32 = jnp.float32
BF16 = jnp.bfloat16
HIGHEST = lax.Precision.HIGHEST

D_MODEL = 1024
BATCH = 8
SEQ = 4096
N_TOK = BATCH * SEQ
N_HEADS_SWA = 8
N_KV_HEADS_SWA = 2
N_HEADS_FOX = 8
HEAD_DIM = 64
WINDOW = 128
BLOCK = 128
NUM_BUCKETS = 32
MAX_DISTANCE = 128
N_GROUPS = 4
EXPERTS_PER_GROUP = 8
N_EXPERTS = N_GROUPS * EXPERTS_PER_GROUP
D_FF_EXPERT = 256
EPS = 1e-6
NEG_INF = -1e30

Q_A = N_HEADS_SWA * HEAD_DIM
KV_A = N_KV_HEADS_SWA * HEAD_DIM
W_B = N_HEADS_FOX * HEAD_DIM
LANES = 128
QK_SCALE = HEAD_DIM ** -0.5

TM_IN = 512
TM_POST = 512
TQ_FOX = 512
TK_FOX = TQ_FOX
TM_EXP = 256
TM_ROW = 512
P_ROWS = 2 * N_TOK
N_ITEMS = P_ROWS // TM_EXP + N_EXPERTS - 1
VMEM_LIMIT = 56 * 1024 * 1024

DECAY_LANES = 6


def _cparams(n_axes):
    return pltpu.CompilerParams(dimension_semantics=("arbitrary",) * n_axes,
                                vmem_limit_bytes=VMEM_LIMIT)


def _ada_kernel(c_ref, w_ref, b_ref, o_ref):
    c = c_ref[...]
    ca = c * jax.nn.sigmoid(c)
    o_ref[...] = jnp.dot(ca.astype(BF16), w_ref[...].astype(BF16),
                         preferred_element_type=F32) + b_ref[...]


def _ada(c16, w_ada, b_ada):
    n_out = w_ada.shape[1]
    blk = 1024
    return pl.pallas_call(
        _ada_kernel,
        out_shape=jax.ShapeDtypeStruct((16, n_out), F32),
        grid=(n_out // blk,),
        in_specs=[pl.BlockSpec((16, D_MODEL), lambda j: (0, 0)),
                  pl.BlockSpec((D_MODEL, blk), lambda j: (0, j)),
                  pl.BlockSpec((1, blk), lambda j: (0, j))],
        out_specs=pl.BlockSpec((16, blk), lambda j: (0, j)),
        compiler_params=_cparams(1),
        name="ada",
    )(c16, w_ada, b_ada)


def _inproj_kernel(x_ref, sc_ref, sh_ref, g_ref, wm_ref, wf_ref, wg_ref,
                   qa_ref, kd_ref, vd_ref, qb_ref, kb_ref, vb_ref, f_ref, gt_ref):
    x = x_ref[...]
    rs = lax.rsqrt(jnp.mean(x * x, axis=-1, keepdims=True) + EPS)
    a = g_ref[...] * (1.0 + sc_ref[...])
    h = (x * rs * a + sh_ref[...]).astype(BF16)

    def mm(w):
        return jnp.dot(h, w, preferred_element_type=F32)

    qa_ref[...] = (mm(wm_ref[:, 0:512]) * QK_SCALE).astype(BF16)
    kd_ref[...] = mm(wm_ref[:, 512:768]).astype(BF16)
    vd_ref[...] = mm(wm_ref[:, 768:1024]).astype(BF16)
    qb_ref[...] = (mm(wm_ref[:, 1024:1536]) * QK_SCALE).astype(BF16)
    kb_ref[...] = mm(wm_ref[:, 1536:2048]).astype(BF16)
    vb_ref[...] = mm(wm_ref[:, 2048:2560]).astype(BF16)
    f_ref[...] = mm(wf_ref[...])
    gt_ref[...] = mm(wg_ref[...]).astype(BF16)


def _inproj(x2, scale_m, shift_m, g_mix, w_main, w_f, w_g):
    tm = TM_IN
    tpb = SEQ // tm
    row = lambda i: (i, 0)
    per_b = lambda i: (i // tpb, 0, 0)
    const = lambda i: (0, 0)
    outs = [(Q_A, BF16), (2 * KV_A, BF16), (2 * KV_A, BF16), (W_B, BF16), (W_B, BF16), (W_B, BF16),
            (LANES, F32), (2 * D_MODEL, BF16)]
    return pl.pallas_call(
        _inproj_kernel,
        out_shape=[jax.ShapeDtypeStruct((N_TOK, w), dt) for w, dt in outs],
        grid=(N_TOK // tm,),
        in_specs=[pl.BlockSpec((tm, D_MODEL), row),
                  pl.BlockSpec((None, 1, D_MODEL), per_b),
                  pl.BlockSpec((None, 1, D_MODEL), per_b),
                  pl.BlockSpec((1, D_MODEL), const),
                  pl.BlockSpec(w_main.shape, const),
                  pl.BlockSpec(w_f.shape, const),
                  pl.BlockSpec(w_g.shape, const)],
        out_specs=[pl.BlockSpec((tm, w), row) for w, _ in outs],
        compiler_params=_cparams(1),
        name="inproj",
    )(x2, scale_m, shift_m, g_mix, w_main, w_f, w_g)


def _log_sigmoid(x):
    return jnp.minimum(x, 0.0) - jnp.log1p(jnp.exp(-jnp.abs(x)))


def _cum_kernel(f_ref, b_ref, jm_ref, qa_ref, ka_ref):
    r = lax.broadcasted_iota(jnp.int32, (LANES, LANES), 0)
    c = lax.broadcasted_iota(jnp.int32, (LANES, LANES), 1)
    expand = ((c >= DECAY_LANES * r) & (c < DECAY_LANES * r + DECAY_LANES)
              & (r < N_HEADS_FOX)).astype(F32)
    tril = (r >= c).astype(F32)
    lf = _log_sigmoid(f_ref[...] + b_ref[...])
    lfe = jnp.dot(lf, expand, precision=HIGHEST, preferred_element_type=F32)
    jm = jm_ref[...]
    carry = jnp.zeros((1, LANES), F32)
    for blk in range(SEQ // LANES):
        rows = slice(blk * LANES, (blk + 1) * LANES)
        cb = jnp.dot(tril, lfe[rows], precision=HIGHEST, preferred_element_type=F32) + carry
        carry = cb[LANES - 1:LANES]
        hi = cb.astype(BF16).astype(F32)
        r1 = cb - hi
        mid = r1.astype(BF16).astype(F32)
        lo = (r1 - mid).astype(BF16).astype(F32)
        one = jnp.ones_like(cb)
        zero = jnp.zeros_like(cb)
        qa = jnp.where(jm == 0, hi, jnp.where(jm == 1, mid, jnp.where(jm == 2, lo,
                       jnp.where(jm < DECAY_LANES, one, zero))))
        ka = jnp.where(jm == 3, -hi, jnp.where(jm == 4, -mid, jnp.where(jm == 5, -lo,
                       jnp.where(jm < 3, one, zero))))
        qa_ref[rows, :] = qa.astype(BF16)
        ka_ref[rows, :] = ka.astype(BF16)


def _cum(f_pad, b_pad, jmod):
    return pl.pallas_call(
        _cum_kernel,
        out_shape=[jax.ShapeDtypeStruct((BATCH, SEQ, LANES), BF16)] * 2,
        grid=(BATCH,),
        in_specs=[pl.BlockSpec((SEQ, LANES), lambda b: (b, 0)),
                  pl.BlockSpec((1, LANES), lambda b: (0, 0)),
                  pl.BlockSpec((1, LANES), lambda b: (0, 0))],
        out_specs=[pl.BlockSpec((None, SEQ, LANES), lambda b: (b, 0, 0))] * 2,
        compiler_params=_cparams(1),
        name="cum",
    )(f_pad, b_pad, jmod)


def _swa_kernel(sink_ref, q_ref, kc_ref, kp_ref, vc_ref, vp_ref, bias_ref, o_ref):
    i = pl.program_id(1)
    lane = lax.broadcasted_iota(jnp.int32, (BLOCK, LANES), 1)
    lo = lane < HEAD_DIM
    col = lax.broadcasted_iota(jnp.int32, (BLOCK, 2 * BLOCK), 1)
    dead = jnp.logical_and(i == 0, col < BLOCK)
    kk = jnp.concatenate([kp_ref[...], kc_ref[...]], axis=0)
    vv = jnp.concatenate([vp_ref[...], vc_ref[...]], axis=0)
    grp = N_HEADS_SWA // N_KV_HEADS_SWA
    for g in range(N_KV_HEADS_SWA):
        parts = []
        for t in range(2):
            qt = q_ref[:, (2 * g + t) * LANES:(2 * g + t + 1) * LANES]
            zero = jnp.zeros_like(qt)
            parts.append(jnp.where(lo, qt, zero))
            parts.append(jnp.where(lo, zero, qt))
        q4 = jnp.concatenate(parts, axis=0)
        s = lax.dot_general(q4, kk[:, g * LANES:(g + 1) * LANES], (((1,), (1,)), ((), ())),
                            preferred_element_type=F32)
        s = s + bias_ref[g]
        outs = []
        for hh in range(grp):
            sink = sink_ref[g * grp + hh]
            sl = s[hh * BLOCK:(hh + 1) * BLOCK]
            sl = jnp.where(dead, NEG_INF, sl)
            m = jnp.maximum(jnp.max(sl, axis=-1, keepdims=True), sink)
            p = jnp.exp(sl - m)
            den = jnp.sum(p, axis=-1, keepdims=True) + jnp.exp(sink - m)
            o = jnp.dot(p.astype(BF16), vv[:, g * LANES:(g + 1) * LANES],
                        preferred_element_type=F32)
            outs.append(o / den)
        o_ref[:, (2 * g) * LANES:(2 * g + 1) * LANES] = jnp.where(lo, outs[0], outs[1]).astype(BF16)
        o_ref[:, (2 * g + 1) * LANES:(2 * g + 2) * LANES] = jnp.where(lo, outs[2], outs[3]).astype(BF16)


def _swa(sinks, qa, kdup, vdup, bias):
    nb = SEQ // BLOCK
    cur = lambda b, i, s: (b * nb + i, 0)
    prev = lambda b, i, s: (b * nb + jnp.maximum(i - 1, 0), 0)
    grid_spec = pltpu.PrefetchScalarGridSpec(
        num_scalar_prefetch=1,
        grid=(BATCH, nb),
        in_specs=[pl.BlockSpec((BLOCK, Q_A), cur),
                  pl.BlockSpec((BLOCK, 2 * KV_A), cur),
                  pl.BlockSpec((BLOCK, 2 * KV_A), prev),
                  pl.BlockSpec((BLOCK, 2 * KV_A), cur),
                  pl.BlockSpec((BLOCK, 2 * KV_A), prev),
                  pl.BlockSpec(bias.shape, lambda b, i, s: (0, 0, 0))],
        out_specs=pl.BlockSpec((BLOCK, Q_A), cur))
    return pl.pallas_call(
        _swa_kernel,
        out_shape=jax.ShapeDtypeStruct((N_TOK, Q_A), BF16),
        grid_spec=grid_spec,
        compiler_params=_cparams(2),
        name="swa",
    )(sinks, qa, kdup, kdup, vdup, vdup, bias)


def _fox_kernel(q_ref, k_ref, v_ref, qa_ref, ka_ref, o_ref, kaug, vaug, q2, m_sc, acc_sc):
    tq, tk = TQ_FOX, TK_FOX
    t = pl.program_id(1)
    i = pl.program_id(2)

    @pl.when(i == 0)
    def _():
        kaug[:, 0:LANES] = k_ref[...]
        kaug[:, LANES:2 * LANES] = ka_ref[...]
        vaug[:, 0:LANES] = v_ref[...]
        vaug[:, LANES:2 * LANES] = jnp.ones((SEQ, LANES), BF16)

    lane = lax.broadcasted_iota(jnp.int32, (tq, LANES), 1)
    lo = lane < HEAD_DIM
    base = 2 * DECAY_LANES * t
    own = [(lane >= base + h * DECAY_LANES) & (lane < base + (h + 1) * DECAY_LANES) for h in range(2)]
    q = q_ref[...]
    qa = qa_ref[...]
    zero = jnp.zeros_like(q)
    q2[0, :, 0:LANES] = jnp.where(lo, q, zero)
    q2[1, :, 0:LANES] = jnp.where(lo, zero, q)
    for h in range(2):
        q2[h, :, LANES:2 * LANES] = jnp.where(own[h], qa, zero)
    m_sc[...] = jnp.full(m_sc.shape, NEG_INF, F32)
    acc_sc[...] = jnp.zeros(acc_sc.shape, F32)

    def chain_step(h, ks, mask):
        s = lax.dot_general(q2[h], kaug[pl.ds(ks, tk), :], (((1,), (1,)), ((), ())),
                            preferred_element_type=F32)
        if mask is not None:
            s = jnp.where(mask, s, NEG_INF)
        m_prev = m_sc[h]
        m_new = jnp.maximum(m_prev, jnp.max(s, axis=-1, keepdims=True))
        alpha = jnp.exp(m_prev - m_new)
        p = jnp.exp(s - jnp.concatenate([m_new] * (tk // LANES), axis=1))
        pv = jnp.dot(p.astype(BF16), vaug[pl.ds(ks, tk), :], preferred_element_type=F32)
        acc_sc[h] = jnp.concatenate([alpha, alpha], axis=1) * acc_sc[h] + pv
        m_sc[h] = m_new

    def step(j, carry):
        ks = pl.multiple_of(j * tk, tk)
        for h in range(2):
            chain_step(h, ks, None)
        return carry

    lax.fori_loop(0, i, step, 0)

    ks = pl.multiple_of(i * tk, tk)
    rr = lax.broadcasted_iota(jnp.int32, (tq, tk), 0)
    cc = lax.broadcasted_iota(jnp.int32, (tq, tk), 1)
    for h in range(2):
        chain_step(h, ks, cc <= rr)

    outs = [acc_sc[h, :, 0:LANES] / acc_sc[h, :, LANES:2 * LANES] for h in range(2)]
    o_ref[...] = jnp.where(lo, outs[0], outs[1]).astype(BF16)


def _fox(qb, kb, vb, qa, ka):
    tq = TQ_FOX
    nq = SEQ // tq
    n_pairs = N_HEADS_FOX // 2
    qmap = lambda b, t, i: (b * nq + i, t)
    kmap = lambda b, t, i: (b, t)
    return pl.pallas_call(
        _fox_kernel,
        out_shape=jax.ShapeDtypeStruct((N_TOK, W_B), BF16),
        grid=(BATCH, n_pairs, nq),
        in_specs=[pl.BlockSpec((tq, LANES), qmap),
                  pl.BlockSpec((SEQ, LANES), kmap),
                  pl.BlockSpec((SEQ, LANES), kmap),
                  pl.BlockSpec((None, tq, LANES), lambda b, t, i: (b, i, 0)),
                  pl.BlockSpec((None, SEQ, LANES), lambda b, t, i: (b, 0, 0))],
        out_specs=pl.BlockSpec((tq, LANES), qmap),
        scratch_shapes=[pltpu.VMEM((SEQ, 2 * LANES), BF16),
                        pltpu.VMEM((SEQ, 2 * LANES), BF16),
                        pltpu.VMEM((2, tq, 2 * LANES), BF16),
                        pltpu.VMEM((2, tq, LANES), F32),
                        pltpu.VMEM((2, tq, 2 * LANES), F32)],
        compiler_params=_cparams(3),
        name="fox",
    )(qb, kb, vb, qa, ka)


def _post_kernel(x_ref, oa_ref, ob_ref, gt_ref, gm_ref, sc_ref, sh_ref, g_ref,
                 wa_ref, wb_ref, wo_ref, wrh_ref, wrl_ref, br_ref,
                 x1_ref, h2_ref, rc_ref, cnt_ref, carry):
    tm = TM_POST
    step = pl.program_id(0)

    @pl.when(step == 0)
    def _():
        carry[...] = jnp.zeros(carry.shape, F32)

    pa = jnp.dot(oa_ref[...], wa_ref[...], preferred_element_type=F32)
    pb = jnp.dot(ob_ref[...], wb_ref[...], preferred_element_type=F32)
    ga = jax.nn.sigmoid(gt_ref[:, 0:D_MODEL].astype(F32))
    gb = jax.nn.sigmoid(gt_ref[:, D_MODEL:2 * D_MODEL].astype(F32))
    merged = (ga * pa + gb * pb).astype(BF16)
    y = jnp.dot(merged, wo_ref[...], preferred_element_type=F32)
    x1 = x_ref[...] + gm_ref[...] * y
    x1_ref[...] = x1

    rs = lax.rsqrt(jnp.mean(x1 * x1, axis=-1, keepdims=True) + EPS)
    a = g_ref[...] * (1.0 + sc_ref[...])
    h2 = x1 * rs * a + sh_ref[...]
    h2_ref[...] = h2

    hh = h2.astype(BF16)
    hl = (h2 - hh.astype(F32)).astype(BF16)
    logits = (jnp.dot(hh, wrh_ref[...], preferred_element_type=F32)
              + jnp.dot(hl, wrh_ref[...], preferred_element_type=F32)
              + jnp.dot(hh, wrl_ref[...], preferred_element_type=F32)
              + br_ref[...])

    lane = lax.broadcasted_iota(jnp.int32, (tm, LANES), 1).astype(F32)
    big = float(LANES)
    gl = jnp.where(lane < N_GROUPS, logits, -jnp.inf)
    gmax = jnp.max(gl, axis=-1, keepdims=True)
    gi = jnp.min(jnp.where(gl == gmax, lane, big), axis=-1, keepdims=True)
    gsum = jnp.sum(jnp.exp(gl - gmax), axis=-1, keepdims=True)
    gp = 1.0 / gsum
    e_lo = N_GROUPS + EXPERTS_PER_GROUP * gi
    el = jnp.where((lane >= e_lo) & (lane < e_lo + EXPERTS_PER_GROUP), logits, -jnp.inf)
    v1 = jnp.max(el, axis=-1, keepdims=True)
    i1 = jnp.min(jnp.where(el == v1, lane, big), axis=-1, keepdims=True)
    el2 = jnp.where(lane == i1, -jnp.inf, el)
    v2 = jnp.max(el2, axis=-1, keepdims=True)
    i2 = jnp.min(jnp.where(el2 == v2, lane, big), axis=-1, keepdims=True)
    e21 = jnp.exp(v2 - v1)
    w1 = gp / (1.0 + e21)
    w2 = gp * e21 / (1.0 + e21)
    e1 = i1 - N_GROUPS
    e2 = i2 - N_GROUPS

    oh = jnp.where((lane == e1) | (lane == e2), 1.0, 0.0)
    rr = lax.broadcasted_iota(jnp.int32, (tm, tm), 0)
    cc = lax.broadcasted_iota(jnp.int32, (tm, tm), 1)
    strict = jnp.where(rr > cc, 1.0, 0.0).astype(BF16)
    before = jnp.dot(strict, oh.astype(BF16), preferred_element_type=F32) + carry[...]
    rank1 = jnp.sum(jnp.where(lane == e1, before, 0.0), axis=-1, keepdims=True)
    rank2 = jnp.sum(jnp.where(lane == e2, before, 0.0), axis=-1, keepdims=True)
    total = carry[...] + jnp.sum(oh, axis=0, keepdims=True)
    carry[...] = total
    cnt_ref[...] = jnp.broadcast_to(total, cnt_ref.shape)

    rc = jnp.where(lane == 0, e1, jnp.where(lane == 1, e2, jnp.where(lane == 2, rank1,
         jnp.where(lane == 3, rank2, jnp.where(lane == 4, w1, jnp.where(lane == 5, w2, 0.0))))))
    rc_ref[...] = rc


def _post(x2, oa, ob, gates, gate_m, scale_f, shift_f, g_ffn, wa, wb, wo, wr_hi, wr_lo, b_r):
    tm = TM_POST
    tpb = SEQ // tm
    row = lambda i: (i, 0)
    per_b = lambda i: (i // tpb, 0, 0)
    const = lambda i: (0, 0)
    return pl.pallas_call(
        _post_kernel,
        out_shape=[jax.ShapeDtypeStruct((N_TOK, D_MODEL), F32),
                   jax.ShapeDtypeStruct((N_TOK, D_MODEL), F32),
                   jax.ShapeDtypeStruct((N_TOK, LANES), F32),
                   jax.ShapeDtypeStruct((8, LANES), F32)],
        grid=(N_TOK // tm,),
        in_specs=[pl.BlockSpec((tm, D_MODEL), row),
                  pl.BlockSpec((tm, Q_A), row),
                  pl.BlockSpec((tm, W_B), row),
                  pl.BlockSpec((tm, 2 * D_MODEL), row),
                  pl.BlockSpec((None, 1, D_MODEL), per_b),
                  pl.BlockSpec((None, 1, D_MODEL), per_b),
                  pl.BlockSpec((None, 1, D_MODEL), per_b),
                  pl.BlockSpec((1, D_MODEL), const),
                  pl.BlockSpec(wa.shape, const),
                  pl.BlockSpec(wb.shape, const),
                  pl.BlockSpec(wo.shape, const),
                  pl.BlockSpec(wr_hi.shape, const),
                  pl.BlockSpec(wr_lo.shape, const),
                  pl.BlockSpec((1, LANES), const)],
        out_specs=[pl.BlockSpec((tm, D_MODEL), row),
                   pl.BlockSpec((tm, D_MODEL), row),
                   pl.BlockSpec((tm, LANES), row),
                   pl.BlockSpec((8, LANES), const)],
        scratch_shapes=[pltpu.VMEM((1, LANES), F32)],
        compiler_params=_cparams(1),
        name="post",
    )(x2, oa, ob, gates, gate_m, scale_f, shift_f, g_ffn, wa, wb, wo, wr_hi, wr_lo, b_r)


def _row_index(idx, slot, n):
    return idx[slot * (TM_ROW // LANES) + lax.shift_right_logical(n, LANES.bit_length() - 1),
               jnp.bitwise_and(n, LANES - 1)]


def _dispatch_kernel(pos_hbm, h_ref, xs_hbm, idx, isem, sem):
    i = pl.program_id(0)
    cp = pltpu.make_async_copy(pos_hbm.at[i], idx, isem)
    cp.start()
    cp.wait()

    def body(n, carry):
        for slot in range(2):
            p = _row_index(idx, slot, n)
            pltpu.make_async_copy(h_ref.at[pl.ds(n, 1)], xs_hbm.at[pl.ds(p, 1)], sem).start()
        return carry

    lax.fori_loop(0, TM_ROW, body, 0, unroll=8)
    for slot in range(2):
        pltpu.make_async_copy(h_ref, xs_hbm.at[pl.ds(0, TM_ROW)], sem).wait()


def _dispatch(pos_tiles, h2):
    return pl.pallas_call(
        _dispatch_kernel,
        out_shape=jax.ShapeDtypeStruct((P_ROWS, D_MODEL), F32),
        grid=(N_TOK // TM_ROW,),
        in_specs=[pl.BlockSpec(memory_space=pl.ANY),
                  pl.BlockSpec((TM_ROW, D_MODEL), lambda i: (i, 0))],
        out_specs=pl.BlockSpec(memory_space=pl.ANY),
        scratch_shapes=[pltpu.SMEM((8, LANES), jnp.int32),
                        pltpu.SemaphoreType.DMA,
                        pltpu.SemaphoreType.DMA],
        compiler_params=_cparams(1),
        name="dispatch",
    )(pos_tiles, h2)


def _experts_kernel(wt_ref, we_ref, st_ref, en_ref, ni_ref, xs_ref, wg_ref, wu_ref, wd_ref, ys_ref):
    tm = TM_EXP
    w = pl.program_id(0)

    @pl.when(w < ni_ref[0])
    def _():
        tile = wt_ref[w]
        e = we_ref[w]
        first = jnp.logical_or(w == 0, wt_ref[jnp.maximum(w - 1, 0)] != tile)
        x = xs_ref[...].astype(BF16)
        a = jnp.dot(x, wg_ref[...].astype(BF16), preferred_element_type=F32)
        u = jnp.dot(x, wu_ref[...].astype(BF16), preferred_element_type=F32)
        hid = (a * jax.nn.sigmoid(a) * u).astype(BF16)
        y = jnp.dot(hid, wd_ref[...].astype(BF16), preferred_element_type=F32)
        row = tile * tm + lax.broadcasted_iota(jnp.int32, (tm, LANES), 0)
        mine = (row >= st_ref[e]) & (row < en_ref[e])

        @pl.when(first)
        def _():
            for c in range(D_MODEL // LANES):
                cols = slice(c * LANES, (c + 1) * LANES)
                ys_ref[:, cols] = jnp.where(mine, y[:, cols], 0.0)

        @pl.when(jnp.logical_not(first))
        def _():
            for c in range(D_MODEL // LANES):
                cols = slice(c * LANES, (c + 1) * LANES)
                ys_ref[:, cols] = jnp.where(mine, y[:, cols], ys_ref[:, cols])


def _experts(item_tile, item_expert, starts, ends, n_items, xs, wg, wu, wd):
    tm = TM_EXP
    rows = lambda w, wt, we, st, en, ni: (wt[w], 0)
    wmap = lambda w, wt, we, st, en, ni: (we[w], 0, 0)
    grid_spec = pltpu.PrefetchScalarGridSpec(
        num_scalar_prefetch=5,
        grid=(N_ITEMS,),
        in_specs=[pl.BlockSpec((tm, D_MODEL), rows),
                  pl.BlockSpec((None, D_MODEL, D_FF_EXPERT), wmap),
                  pl.BlockSpec((None, D_MODEL, D_FF_EXPERT), wmap),
                  pl.BlockSpec((None, D_FF_EXPERT, D_MODEL), wmap)],
        out_specs=pl.BlockSpec((tm, D_MODEL), rows))
    return pl.pallas_call(
        _experts_kernel,
        out_shape=jax.ShapeDtypeStruct((P_ROWS, D_MODEL), F32),
        grid_spec=grid_spec,
        compiler_params=_cparams(1),
        name="experts",
    )(item_tile, item_expert, starts, ends, n_items, xs, wg, wu, wd)


def _combine_kernel(pos_hbm, x1_ref, rc_ref, gf_ref, gfin_ref, ys_hbm, o_ref,
                    idx, buf_a, buf_b, isem, sems):
    i = pl.program_id(0)
    cp = pltpu.make_async_copy(pos_hbm.at[i], idx, isem)
    cp.start()
    cp.wait()
    bufs = (buf_a, buf_b)

    def body(n, carry):
        for slot in range(2):
            p = _row_index(idx, slot, n)
            pltpu.make_async_copy(ys_hbm.at[pl.ds(p, 1)], bufs[slot].at[pl.ds(n, 1)],
                                  sems.at[slot]).start()
        return carry

    lax.fori_loop(0, TM_ROW, body, 0, unroll=8)
    for slot in range(2):
        pltpu.make_async_copy(ys_hbm.at[pl.ds(0, TM_ROW)], bufs[slot], sems.at[slot]).wait()

    w1 = rc_ref[:, 4:5]
    w2 = rc_ref[:, 5:6]
    y = w1 * buf_a[...] + w2 * buf_b[...]
    xf = x1_ref[...] + gf_ref[...] * y
    rs = lax.rsqrt(jnp.mean(xf * xf, axis=-1, keepdims=True) + EPS)
    o_ref[...] = xf * rs * gfin_ref[...]


def _combine(pos_tiles, x1, rcol, gate_f, g_final, ys):
    tm = TM_ROW
    tpb = SEQ // tm
    row = lambda i: (i, 0)
    return pl.pallas_call(
        _combine_kernel,
        out_shape=jax.ShapeDtypeStruct((N_TOK, D_MODEL), F32),
        grid=(N_TOK // tm,),
        in_specs=[pl.BlockSpec(memory_space=pl.ANY),
                  pl.BlockSpec((tm, D_MODEL), row),
                  pl.BlockSpec((tm, LANES), row),
                  pl.BlockSpec((None, 1, D_MODEL), lambda i: (i // tpb, 0, 0)),
                  pl.BlockSpec((1, D_MODEL), lambda i: (0, 0)),
                  pl.BlockSpec(memory_space=pl.ANY)],
        out_specs=pl.BlockSpec((tm, D_MODEL), row),
        scratch_shapes=[pltpu.SMEM((8, LANES), jnp.int32),
                        pltpu.VMEM((tm, D_MODEL), F32),
                        pltpu.VMEM((tm, D_MODEL), F32),
                        pltpu.SemaphoreType.DMA,
                        pltpu.SemaphoreType.DMA((2,))],
        compiler_params=_cparams(1),
        name="combine",
    )(pos_tiles, x1, rcol, gate_f, g_final, ys)


def _t5_bucket_np():
    qi = np.arange(BLOCK)[:, None]
    kj = np.arange(2 * BLOCK)[None, :]
    dist = qi - kj + BLOCK
    n = np.maximum(dist, 0)
    max_exact = NUM_BUCKETS // 2
    nf = np.maximum(n, 1).astype(np.float32)
    large = max_exact + (np.log(nf / np.float32(max_exact)) / np.float32(math.log(MAX_DISTANCE / max_exact))
                         * np.float32(NUM_BUCKETS - max_exact)).astype(np.int32)
    large = np.minimum(large, NUM_BUCKETS - 1)
    bucket = np.where(n < max_exact, n, large)
    band = (dist >= 0) & (dist < WINDOW)
    return bucket.astype(np.int32), band


def kernel(x, c, w_ada, b_ada, g_norm_mix, g_norm_ffn, w_in, sinks, b_forget, w_proj_swa, w_proj_fox,
           w_out, rel_bias_table, w_router_group, b_router_group, w_router_expert, b_router_expert,
           w_gate_exp, w_up_exp, w_down_exp, g_final):
    l = 0
    x2 = x.reshape(N_TOK, D_MODEL)

    c16 = jnp.concatenate([c, jnp.zeros_like(c)], axis=0)
    mod = _ada(c16, w_ada[l], b_ada[l][None, :])[:BATCH]
    shift_m, scale_m, gate_m, shift_f, scale_f, gate_f = [
        m.reshape(BATCH, 1, D_MODEL) for m in jnp.split(mod, 6, axis=-1)]

    w = w_in[l]
    o_ka, o_va, o_qb = Q_A, Q_A + KV_A, Q_A + 2 * KV_A
    o_kb, o_vb, o_f = o_qb + W_B, o_qb + 2 * W_B, o_qb + 3 * W_B
    o_g = o_f + N_HEADS_FOX

    def dup(cols):
        heads = [cols[:, h * HEAD_DIM:(h + 1) * HEAD_DIM] for h in range(N_KV_HEADS_SWA)]
        return jnp.concatenate([hd for hd in heads for _ in range(2)], axis=1)

    w_main = jnp.concatenate([w[:, :Q_A], dup(w[:, o_ka:o_va]), dup(w[:, o_va:o_qb]),
                              w[:, o_qb:o_f]], axis=1).astype(BF16)
    w_f = jnp.pad(w[:, o_f:o_g], ((0, 0), (0, LANES - N_HEADS_FOX))).astype(BF16)
    w_g = w[:, o_g:].astype(BF16)
    qa, kdup, vdup, qb, kb, vb, f_pad, gates = _inproj(
        x2, scale_m, shift_m, g_norm_mix[l][None, :], w_main, w_f, w_g)

    b_pad = jnp.pad(b_forget[l], (0, LANES - N_HEADS_FOX))[None, :]
    lanes = np.arange(LANES)
    jmod = jnp.asarray(np.where(lanes < DECAY_LANES * N_HEADS_FOX, lanes % DECAY_LANES, 7)[None, :].astype(np.int32))
    dq, dk = _cum(f_pad, b_pad, jmod)

    bucket, band = _t5_bucket_np()
    bias = jnp.transpose(rel_bias_table[bucket].astype(F32), (2, 0, 1))
    bias = jnp.where(band[None], bias, NEG_INF).reshape(N_KV_HEADS_SWA, -1, 2 * BLOCK)
    o_a = _swa(sinks[l].astype(F32), qa, kdup, vdup, bias)

    o_b = _fox(qb, kb, vb, dq, dk)

    w_r = jnp.concatenate([w_router_group[l]] + [w_router_expert[l][g] for g in range(N_GROUPS)], axis=1)
    w_r = jnp.pad(w_r, ((0, 0), (0, LANES - w_r.shape[1])))
    wr_hi = w_r.astype(BF16)
    wr_lo = (w_r - wr_hi.astype(F32)).astype(BF16)
    b_r = jnp.concatenate([b_router_group[l], b_router_expert[l].reshape(-1)])
    b_r = jnp.pad(b_r, (0, LANES - b_r.shape[0]))[None, :]
    x1, h2, rcol, cnt = _post(x2, o_a, o_b, gates, gate_m, scale_f, shift_f, g_norm_ffn[l][None, :],
                              w_proj_swa[l].astype(BF16), w_proj_fox[l].astype(BF16), w_out[l].astype(BF16),
                              wr_hi, wr_lo, b_r)

    i32 = jnp.int32
    counts = cnt[0, :N_EXPERTS].astype(i32)
    ends = jnp.cumsum(counts).astype(i32)
    starts = ends - counts
    e12 = rcol[:, 0:2].astype(i32)
    r12 = rcol[:, 2:4].astype(i32)
    expert_ids = jnp.arange(N_EXPERTS, dtype=i32)
    pos = jnp.sum(jnp.where(e12[..., None] == expert_ids, starts, 0), axis=-1) + r12
    pos_tiles = pos.reshape(N_TOK // TM_ROW, TM_ROW, 2).transpose(0, 2, 1).reshape(N_TOK // TM_ROW, 8, LANES)
    first_tile = starts // TM_EXP
    last_tile = (ends - 1) // TM_EXP
    per_expert = jnp.where(counts > 0, last_tile - first_tile + 1, 0)
    item_end = jnp.cumsum(per_expert).astype(i32)
    item_start = item_end - per_expert
    n_items = item_end[-1:]
    w = jnp.minimum(jnp.arange(N_ITEMS, dtype=i32), n_items - 1)
    item_expert = jnp.sum((item_end[None, :] <= w[:, None]).astype(i32), axis=1)
    sel = item_expert[:, None] == expert_ids
    item_tile = jnp.sum(jnp.where(sel, first_tile - item_start, 0), axis=1) + w

    xs = _dispatch(pos_tiles, h2)
    ys = _experts(item_tile.astype(i32), item_expert.astype(i32), starts, ends, n_items, xs,
                  w_gate_exp[l].reshape(N_EXPERTS, D_MODEL, D_FF_EXPERT),
                  w_up_exp[l].reshape(N_EXPERTS, D_MODEL, D_FF_EXPERT),
                  w_down_exp[l].reshape(N_EXPERTS, D_FF_EXPERT, D_MODEL))
    out = _combine(pos_tiles, x1, rcol, gate_f, g_final[None, :], ys)
    return out.reshape(BATCH, SEQ, D_MODEL)
```

```python
import functools
import math

import numpy as np
import jax
import jax.numpy as jnp
from jax import lax
from jax.experimental import pallas as pl
from jax.experimental.pallas import tpu as pltpu

F32 = jnp.float32
BF16 = jnp.bfloat16
HIGHEST = lax.Precision.HIGHEST

D_MODEL = 1024
BATCH = 8
SEQ = 4096
N_TOK = BATCH * SEQ
N_HEADS_SWA = 8
N_KV_HEADS_SWA = 2
N_HEADS_FOX = 8
HEAD_DIM = 64
WINDOW = 128
BLOCK = 128
NUM_BUCKETS = 32
MAX_DISTANCE = 128
N_GROUPS = 4
EXPERTS_PER_GROUP = 8
N_EXPERTS = N_GROUPS * EXPERTS_PER_GROUP
D_FF_EXPERT = 256
EPS = 1e-6
NEG_INF = -1e30

Q_A = N_HEADS_SWA * HEAD_DIM
KV_A = N_KV_HEADS_SWA * HEAD_DIM
W_B = N_HEADS_FOX * HEAD_DIM
LANES = 128
QK_SCALE = HEAD_DIM ** -0.5

TM_IN = 512
TM_POST = 512
TQ_FOX = 512
TK_FOX = TQ_FOX
TM_EXP = 256
TM_ROW = 512
P_ROWS = 2 * N_TOK
N_ITEMS = P_ROWS // TM_EXP + N_EXPERTS - 1
VMEM_LIMIT = 56 * 1024 * 1024

DECAY_LANES = 6
PRUNE_MARGIN = 110.0


def _cparams(n_axes):
    return pltpu.CompilerParams(dimension_semantics=("arbitrary",) * n_axes,
                                vmem_limit_bytes=VMEM_LIMIT)


def _ada_kernel(c_ref, w_ref, b_ref, o_ref):
    c = c_ref[...]
    ca = c * jax.nn.sigmoid(c)
    o_ref[...] = jnp.dot(ca.astype(BF16), w_ref[...].astype(BF16),
                         preferred_element_type=F32) + b_ref[...]


def _ada(c16, w_ada, b_ada):
    n_out = w_ada.shape[1]
    blk = 1024
    return pl.pallas_call(
        _ada_kernel,
        out_shape=jax.ShapeDtypeStruct((16, n_out), F32),
        grid=(n_out // blk,),
        in_specs=[pl.BlockSpec((16, D_MODEL), lambda j: (0, 0)),
                  pl.BlockSpec((D_MODEL, blk), lambda j: (0, j)),
                  pl.BlockSpec((1, blk), lambda j: (0, j))],
        out_specs=pl.BlockSpec((16, blk), lambda j: (0, j)),
        compiler_params=_cparams(1),
        name="ada",
    )(c16, w_ada, b_ada)


def _inproj_kernel(x_ref, sc_ref, sh_ref, g_ref, wm_ref, wf_ref, wg_ref, ind_ref,
                   qa_ref, kd_ref, vd_ref, qb_ref, kb_ref, vb_ref, f_ref, gt_ref, nrm_ref):
    x = x_ref[...]
    rs = lax.rsqrt(jnp.mean(x * x, axis=-1, keepdims=True) + EPS)
    a = g_ref[...] * (1.0 + sc_ref[...])
    h = (x * rs * a + sh_ref[...]).astype(BF16)

    def mm(w):
        return jnp.dot(h, w, preferred_element_type=F32)

    qa_ref[...] = (mm(wm_ref[:, 0:512]) * QK_SCALE).astype(BF16)
    kd_ref[...] = mm(wm_ref[:, 512:768]).astype(BF16)
    vd_ref[...] = mm(wm_ref[:, 768:1024]).astype(BF16)
    qb = (mm(wm_ref[:, 1024:1536]) * QK_SCALE).astype(BF16)
    kb = mm(wm_ref[:, 1536:2048]).astype(BF16)
    qb_ref[...] = qb
    kb_ref[...] = kb
    vb_ref[...] = mm(wm_ref[:, 2048:2560]).astype(BF16)
    sq = jnp.concatenate([qb, kb], axis=1).astype(F32)
    seg = jnp.dot((sq * sq).astype(BF16), ind_ref[...], preferred_element_type=F32)
    nrm_ref[...] = jnp.broadcast_to(jnp.max(seg, axis=0, keepdims=True), nrm_ref.shape)
    f_ref[...] = mm(wf_ref[...])
    gt_ref[...] = mm(wg_ref[...]).astype(BF16)


def _inproj(x2, scale_m, shift_m, g_mix, w_main, w_f, w_g):
    tm = TM_IN
    tpb = SEQ // tm
    row = lambda i: (i, 0)
    per_b = lambda i: (i // tpb, 0, 0)
    const = lambda i: (0, 0)
    outs = [(Q_A, BF16), (2 * KV_A, BF16), (2 * KV_A, BF16), (W_B, BF16), (W_B, BF16), (W_B, BF16),
            (LANES, F32), (2 * D_MODEL, BF16)]
    ind_np = np.zeros((2 * W_B, LANES), np.float32)
    ind_np[np.arange(2 * W_B), np.arange(2 * W_B) // HEAD_DIM] = 1.0
    ind = jnp.asarray(ind_np, dtype=BF16)
    n_steps = N_TOK // tm
    return pl.pallas_call(
        _inproj_kernel,
        out_shape=[jax.ShapeDtypeStruct((N_TOK, w), dt) for w, dt in outs]
        + [jax.ShapeDtypeStruct((n_steps * 8, LANES), F32)],
        grid=(n_steps,),
        in_specs=[pl.BlockSpec((tm, D_MODEL), row),
                  pl.BlockSpec((None, 1, D_MODEL), per_b),
                  pl.BlockSpec((None, 1, D_MODEL), per_b),
                  pl.BlockSpec((1, D_MODEL), const),
                  pl.BlockSpec(w_main.shape, const),
                  pl.BlockSpec(w_f.shape, const),
                  pl.BlockSpec(w_g.shape, const),
                  pl.BlockSpec(ind.shape, const)],
        out_specs=[pl.BlockSpec((tm, w), row) for w, _ in outs] + [pl.BlockSpec((8, LANES), row)],
        compiler_params=_cparams(1),
        name="inproj",
    )(x2, scale_m, shift_m, g_mix, w_main, w_f, w_g, ind)


def _log_sigmoid(x):
    return jnp.minimum(x, 0.0) - jnp.log1p(jnp.exp(-jnp.abs(x)))


def _cum_kernel(f_ref, b_ref, jm_ref, qa_ref, ka_ref, fb_ref):
    r = lax.broadcasted_iota(jnp.int32, (LANES, LANES), 0)
    c = lax.broadcasted_iota(jnp.int32, (LANES, LANES), 1)
    expand = ((c >= DECAY_LANES * r) & (c < DECAY_LANES * r + DECAY_LANES)
              & (r < N_HEADS_FOX)).astype(F32)
    tril = (r >= c).astype(F32)
    lf = _log_sigmoid(f_ref[...] + b_ref[...])
    lfe = jnp.dot(lf, expand, precision=HIGHEST, preferred_element_type=F32)
    jm = jm_ref[...]
    carry = jnp.zeros((1, LANES), F32)
    for blk in range(SEQ // LANES):
        rows = slice(blk * LANES, (blk + 1) * LANES)
        cb = jnp.dot(tril, lfe[rows], precision=HIGHEST, preferred_element_type=F32) + carry
        carry = cb[LANES - 1:LANES]
        hi = cb.astype(BF16).astype(F32)
        r1 = cb - hi
        mid = r1.astype(BF16).astype(F32)
        lo = (r1 - mid).astype(BF16).astype(F32)
        one = jnp.ones_like(cb)
        zero = jnp.zeros_like(cb)
        qa = jnp.where(jm == 0, hi, jnp.where(jm == 1, mid, jnp.where(jm == 2, lo,
                       jnp.where(jm < DECAY_LANES, one, zero))))
        ka = jnp.where(jm == 3, -hi, jnp.where(jm == 4, -mid, jnp.where(jm == 5, -lo,
                       jnp.where(jm < 3, one, zero))))
        qa_ref[rows, :] = qa.astype(BF16)
        ka_ref[rows, :] = ka.astype(BF16)
        blocks_per_tile = TQ_FOX // LANES
        tile = blk // blocks_per_tile
        if blk % blocks_per_tile == 0:
            fb_ref[2 * tile:2 * tile + 1, :] = cb[0:1]
        if blk % blocks_per_tile == blocks_per_tile - 1:
            fb_ref[2 * tile + 1:2 * tile + 2, :] = carry


def _cum(f_pad, b_pad, jmod):
    n_tiles = SEQ // TQ_FOX
    return pl.pallas_call(
        _cum_kernel,
        out_shape=[jax.ShapeDtypeStruct((BATCH, SEQ, LANES), BF16)] * 2
        + [jax.ShapeDtypeStruct((BATCH, 2 * n_tiles, LANES), F32)],
        grid=(BATCH,),
        in_specs=[pl.BlockSpec((SEQ, LANES), lambda b: (b, 0)),
                  pl.BlockSpec((1, LANES), lambda b: (0, 0)),
                  pl.BlockSpec((1, LANES), lambda b: (0, 0))],
        out_specs=[pl.BlockSpec((None, SEQ, LANES), lambda b: (b, 0, 0))] * 2
        + [pl.BlockSpec((None, 2 * n_tiles, LANES), lambda b: (b, 0, 0))],
        compiler_params=_cparams(1),
        name="cum",
    )(f_pad, b_pad, jmod)


def _swa_kernel(sink_ref, q_ref, kc_ref, kp_ref, vc_ref, vp_ref, bias_ref, o_ref):
    lane = lax.broadcasted_iota(jnp.int32, (BLOCK, LANES), 1)
    lo = lane < HEAD_DIM
    kk =jnp.concatenate([kp_ref[...], kc_ref[...]], axis=0)
    vv = jnp.concatenate([vp_ref[...], vc_ref[...]], axis=0)
    grp = N_HEADS_SWA // N_KV_HEADS_SWA
    for g in range(N_KV_HEADS_SWA):
        parts = []
        for t in range(2):
            qt = q_ref[:, (2 * g + t) * LANES:(2 * g + t + 1) * LANES]
            zero = jnp.zeros_like(qt)
            parts.append(jnp.where(lo, qt, zero))
            parts.append(jnp.where(lo, zero, qt))
        q4 = jnp.concatenate(parts, axis=0)
        s = lax.dot_general(q4, kk[:, g * LANES:(g + 1) * LANES], (((1,), (1,)), ((), ())),
                            preferred_element_type=F32)
        s = s + bias_ref[g]
        outs = []
        for hh in range(grp):
            sink = sink_ref[g * grp + hh]
            sl = s[hh * BLOCK:(hh + 1) * BLOCK]
            m =jnp.maximum(jnp.max(sl, axis=-1, keepdims=True), sink)
            p = jnp.exp(sl - m)
            den = jnp.sum(p, axis=-1, keepdims=True) + jnp.exp(sink - m)
            o = jnp.dot(p.astype(BF16), vv[:, g * LANES:(g + 1) * LANES],
                        preferred_element_type=F32)
            outs.append(o / den)
        o_ref[:, (2 * g) * LANES:(2 * g + 1) * LANES] = jnp.where(lo, outs[0], outs[1]).astype(BF16)
        o_ref[:, (2 * g + 1) * LANES:(2 * g + 2) * LANES] = jnp.where(lo, outs[2], outs[3]).astype(BF16)


def _swa(sinks, qa, kdup, vdup, bias):
    nb = SEQ // BLOCK
    cur = lambda b, i, s: (b * nb + i, 0)
    prev = lambda b, i, s: (b * nb + jnp.maximum(i - 1, 0), 0)
    grid_spec = pltpu.PrefetchScalarGridSpec(
        num_scalar_prefetch=1,
        grid=(BATCH, nb),
        in_specs=[pl.BlockSpec((BLOCK, Q_A), cur),
                  pl.BlockSpec((BLOCK, 2 * KV_A), cur),
                  pl.BlockSpec((BLOCK, 2 * KV_A), prev),
                  pl.BlockSpec((BLOCK, 2 * KV_A), cur),
                  pl.BlockSpec((BLOCK, 2 * KV_A), prev),
                  pl.BlockSpec((None,) + bias.shape[1:], lambda b, i, s: (jnp.minimum(i, 1), 0, 0, 0))],
        out_specs=pl.BlockSpec((BLOCK, Q_A), cur))
    return pl.pallas_call(
        _swa_kernel,
        out_shape=jax.ShapeDtypeStruct((N_TOK, Q_A), BF16),
        grid_spec=grid_spec,
        compiler_params=_cparams(2),
        name="swa",
    )(sinks, qa, kdup, kdup, vdup, vdup, bias)


def _fox_kernel(js_ref, q_ref, k_ref, v_ref, qa_ref, ka_ref, o_ref, kaug, vaug, q2, m_sc, acc_sc):
    tq, tk = TQ_FOX, TK_FOX
    b = pl.program_id(0)
    t = pl.program_id(1)
    i = pl.program_id(2)
    j_start = js_ref[(b * pl.num_programs(1) + t) * pl.num_programs(2) + i]

    @pl.when(i == 0)
    def _():
        kaug[:, 0:LANES] = k_ref[...]
        kaug[:, LANES:2 * LANES] = ka_ref[...]
        vaug[:, 0:LANES] = v_ref[...]
        vaug[:, LANES:2 * LANES] = jnp.ones((SEQ, LANES), BF16)

    lane = lax.broadcasted_iota(jnp.int32, (tq, LANES), 1)
    lo = lane < HEAD_DIM
    base = 2 * DECAY_LANES * t
    own = [(lane >= base + h * DECAY_LANES) & (lane < base + (h + 1) * DECAY_LANES) for h in range(2)]
    q = q_ref[...]
    qa = qa_ref[...]
    zero = jnp.zeros_like(q)
    q2[0, :, 0:LANES] = jnp.where(lo, q, zero)
    q2[1, :, 0:LANES] = jnp.where(lo, zero, q)
    for h in range(2):
        q2[h, :, LANES:2 * LANES] = jnp.where(own[h], qa, zero)
    m_sc[...] = jnp.full(m_sc.shape, NEG_INF, F32)
    acc_sc[...] = jnp.zeros(acc_sc.shape, F32)

    def chain_step(h, ks, mask):
        s = lax.dot_general(q2[h], kaug[pl.ds(ks, tk), :], (((1,), (1,)), ((), ())),
                            preferred_element_type=F32)
        if mask is not None:
            s = jnp.where(mask, s, NEG_INF)
        m_prev = m_sc[h]
        m_new = jnp.maximum(m_prev, jnp.max(s, axis=-1, keepdims=True))
        alpha = jnp.exp(m_prev - m_new)
        p = jnp.exp(s - jnp.concatenate([m_new] * (tk // LANES), axis=1))
        pv = jnp.dot(p.astype(BF16), vaug[pl.ds(ks, tk), :], preferred_element_type=F32)
        acc_sc[h] = jnp.concatenate([alpha, alpha], axis=1) * acc_sc[h] + pv
        m_sc[h] = m_new

    def step(j, carry):
        ks = pl.multiple_of(j * tk, tk)
        for h in range(2):
            chain_step(h, ks, None)
        return carry

    lax.fori_loop(j_start, i, step, 0)

    ks = pl.multiple_of(i * tk, tk)
    rr = lax.broadcasted_iota(jnp.int32, (tq, tk), 0)
    cc = lax.broadcasted_iota(jnp.int32, (tq, tk), 1)
    for h in range(2):
        chain_step(h, ks, cc <= rr)

    outs = [acc_sc[h, :, 0:LANES] / acc_sc[h, :, LANES:2 * LANES] for h in range(2)]
    o_ref[...] = jnp.where(lo, outs[0], outs[1]).astype(BF16)


def _fox(j_start, qb, kb, vb, qa, ka):
    tq = TQ_FOX
    nq = SEQ // tq
    n_pairs = N_HEADS_FOX // 2
    qmap = lambda b, t, i, js: (b * nq + i, t)
    kmap = lambda b, t, i, js: (b, t)
    grid_spec = pltpu.PrefetchScalarGridSpec(
        num_scalar_prefetch=1,
        grid=(BATCH, n_pairs, nq),
        in_specs=[pl.BlockSpec((tq, LANES), qmap),
                  pl.BlockSpec((SEQ, LANES), kmap),
                  pl.BlockSpec((SEQ, LANES), kmap),
                  pl.BlockSpec((None, tq, LANES), lambda b, t, i, js: (b, i, 0)),
                  pl.BlockSpec((None, SEQ, LANES), lambda b, t, i, js: (b, 0, 0))],
        out_specs=pl.BlockSpec((tq, LANES), qmap),
        scratch_shapes=[pltpu.VMEM((SEQ, 2 * LANES), BF16),
                        pltpu.VMEM((SEQ, 2 * LANES), BF16),
                        pltpu.VMEM((2, tq, 2 * LANES), BF16),
                        pltpu.VMEM((2, tq, LANES), F32),
                        pltpu.VMEM((2, tq, 2 * LANES), F32)])
    return pl.pallas_call(
        _fox_kernel,
        out_shape=jax.ShapeDtypeStruct((N_TOK, W_B), BF16),
        grid_spec=grid_spec,
        compiler_params=_cparams(3),
        name="fox",
    )(j_start, qb, kb, vb, qa, ka)


def _fox_first_tiles(nrm, fb):
    n_tiles = SEQ // TQ_FOX
    nr = nrm.reshape(BATCH, n_tiles, 8, LANES)[:, :, 0, :] * 1.02
    qn = jnp.sqrt(nr[..., 0:N_HEADS_FOX])
    kn = jnp.sqrt(nr[..., N_HEADS_FOX:2 * N_HEADS_FOX])
    f_first = fb[:, 0::2, 0:DECAY_LANES * N_HEADS_FOX:DECAY_LANES]
    f_last = fb[:, 1::2, 0:DECAY_LANES * N_HEADS_FOX:DECAY_LANES]
    kn_prefix = lax.cummax(kn, axis=1)
    upper = qn[:, :, None, :] * kn_prefix[:, None, :, :] + f_first[:, :, None, :] - f_last[:, None, :, :]
    row_max_low = -(qn * kn)[:, :, None, :]
    ii = jnp.arange(n_tiles)[None, :, None, None]
    jj = jnp.arange(n_tiles)[None, None, :, None]
    skip = (upper < row_max_low - PRUNE_MARGIN) & (jj < ii)
    skip = jnp.all(skip.reshape(BATCH, n_tiles, n_tiles, N_HEADS_FOX // 2, 2), axis=-1)
    first = jnp.sum(jnp.cumprod(skip.astype(jnp.int32), axis=2), axis=2)
    return jnp.transpose(first, (0, 2, 1)).reshape(-1).astype(jnp.int32)


def _post_kernel(x_ref, oa_ref, ob_ref, gt_ref, gm_ref, sc_ref, sh_ref, g_ref,
                 wa_ref, wb_ref, wo_ref, wrh_ref, wrl_ref, br_ref,
                 x1_ref, h2_ref, rc_ref, cnt_ref, carry):
    tm = TM_POST
    step = pl.program_id(0)

    @pl.when(step == 0)
    def _():
        carry[...] = jnp.zeros(carry.shape, F32)

    pa = jnp.dot(oa_ref[...], wa_ref[...], preferred_element_type=F32)
    pb = jnp.dot(ob_ref[...], wb_ref[...], preferred_element_type=F32)
    ga = jax.nn.sigmoid(gt_ref[:, 0:D_MODEL].astype(F32))
    gb = jax.nn.sigmoid(gt_ref[:, D_MODEL:2 * D_MODEL].astype(F32))
    merged = (ga * pa + gb * pb).astype(BF16)
    y = jnp.dot(merged, wo_ref[...], preferred_element_type=F32)
    x1 = x_ref[...] + gm_ref[...] * y
    x1_ref[...] = x1

    rs = lax.rsqrt(jnp.mean(x1 * x1, axis=-1, keepdims=True) + EPS)
    a = g_ref[...] * (1.0 + sc_ref[...])
    h2 = x1 * rs * a + sh_ref[...]
    h2_ref[...] = h2

    hh = h2.astype(BF16)
    hl = (h2 - hh.astype(F32)).astype(BF16)
    logits = (jnp.dot(hh, wrh_ref[...], preferred_element_type=F32)
              + jnp.dot(hl, wrh_ref[...], preferred_element_type=F32)
              + jnp.dot(hh, wrl_ref[...], preferred_element_type=F32)
              + br_ref[...])

    lane = lax.broadcasted_iota(jnp.int32, (tm, LANES), 1).astype(F32)
    big = float(LANES)
    gl = jnp.where(lane < N_GROUPS, logits, -jnp.inf)
    gmax = jnp.max(gl, axis=-1, keepdims=True)
    gi = jnp.min(jnp.where(gl == gmax, lane, big), axis=-1, keepdims=True)
    gsum = jnp.sum(jnp.exp(gl - gmax), axis=-1, keepdims=True)
    gp = 1.0 / gsum
    e_lo = N_GROUPS + EXPERTS_PER_GROUP * gi
    el = jnp.where((lane >= e_lo) & (lane < e_lo + EXPERTS_PER_GROUP), logits, -jnp.inf)
    v1 = jnp.max(el, axis=-1, keepdims=True)
    i1 = jnp.min(jnp.where(el == v1, lane, big), axis=-1, keepdims=True)
    el2 = jnp.where(lane == i1, -jnp.inf, el)
    v2 = jnp.max(el2, axis=-1, keepdims=True)
    i2 = jnp.min(jnp.where(el2 == v2, lane, big), axis=-1, keepdims=True)
    e21 = jnp.exp(v2 - v1)
    w1 = gp / (1.0 + e21)
    w2 = gp * e21 / (1.0 + e21)
    e1 = i1 - N_GROUPS
    e2 = i2 - N_GROUPS

    oh = jnp.where((lane == e1) | (lane == e2), 1.0, 0.0)
    rr = lax.broadcasted_iota(jnp.int32, (tm, tm), 0)
    cc = lax.broadcasted_iota(jnp.int32, (tm, tm), 1)
    strict = jnp.where(rr > cc, 1.0, 0.0).astype(BF16)
    before = jnp.dot(strict, oh.astype(BF16), preferred_element_type=F32) + carry[...]
    rank1 = jnp.sum(jnp.where(lane == e1, before, 0.0), axis=-1, keepdims=True)
    rank2 = jnp.sum(jnp.where(lane == e2, before, 0.0), axis=-1, keepdims=True)
    total = carry[...] + jnp.sum(oh, axis=0, keepdims=True)
    carry[...] = total
    cnt_ref[...] = jnp.broadcast_to(total, cnt_ref.shape)

    rc = jnp.where(lane == 0, e1, jnp.where(lane == 1, e2, jnp.where(lane == 2, rank1,
         jnp.where(lane == 3, rank2, jnp.where(lane == 4, w1, jnp.where(lane == 5, w2, 0.0))))))
    rc_ref[...] = rc


def _post(x2, oa, ob, gates, gate_m, scale_f, shift_f, g_ffn, wa, wb, wo, wr_hi, wr_lo, b_r):
    tm = TM_POST
    tpb = SEQ // tm
    row = lambda i: (i, 0)
    per_b = lambda i: (i // tpb, 0, 0)
    const = lambda i: (0, 0)
    return pl.pallas_call(
        _post_kernel,
        out_shape=[jax.ShapeDtypeStruct((N_TOK, D_MODEL), F32),
                   jax.ShapeDtypeStruct((N_TOK, D_MODEL), F32),
                   jax.ShapeDtypeStruct((N_TOK, LANES), F32),
                   jax.ShapeDtypeStruct((8, LANES), F32)],
        grid=(N_TOK // tm,),
        in_specs=[pl.BlockSpec((tm, D_MODEL), row),
                  pl.BlockSpec((tm, Q_A), row),
                  pl.BlockSpec((tm, W_B), row),
                  pl.BlockSpec((tm, 2 * D_MODEL), row),
                  pl.BlockSpec((None, 1, D_MODEL), per_b),
                  pl.BlockSpec((None, 1, D_MODEL), per_b),
                  pl.BlockSpec((None, 1, D_MODEL), per_b),
                  pl.BlockSpec((1, D_MODEL), const),
                  pl.BlockSpec(wa.shape, const),
                  pl.BlockSpec(wb.shape, const),
                  pl.BlockSpec(wo.shape, const),
                  pl.BlockSpec(wr_hi.shape, const),
                  pl.BlockSpec(wr_lo.shape, const),
                  pl.BlockSpec((1, LANES), const)],
        out_specs=[pl.BlockSpec((tm, D_MODEL), row),
                   pl.BlockSpec((tm, D_MODEL), row),
                   pl.BlockSpec((tm, LANES), row),
                   pl.BlockSpec((8, LANES), const)],
        scratch_shapes=[pltpu.VMEM((1, LANES), F32)],
        compiler_params=_cparams(1),
        name="post",
    )(x2, oa, ob, gates, gate_m, scale_f, shift_f, g_ffn, wa, wb, wo, wr_hi, wr_lo, b_r)


def _row_index(idx, slot, n):
    return idx[slot * (TM_ROW // LANES) + lax.shift_right_logical(n, LANES.bit_length() - 1),
               jnp.bitwise_and(n, LANES - 1)]


def _dispatch_kernel(pos_hbm, h_ref, xs_hbm, idx, isem, sem):
    i = pl.program_id(0)
    cp = pltpu.make_async_copy(pos_hbm.at[i], idx, isem)
    cp.start()
    cp.wait()

    def body(n, carry):
        for slot in range(2):
            p = _row_index(idx, slot, n)
            pltpu.make_async_copy(h_ref.at[pl.ds(n, 1)], xs_hbm.at[pl.ds(p, 1)], sem).start()
        return carry

    lax.fori_loop(0, TM_ROW, body, 0, unroll=8)
    for slot in range(2):
        pltpu.make_async_copy(h_ref, xs_hbm.at[pl.ds(0, TM_ROW)], sem).wait()


def _dispatch(pos_tiles, h2):
    return pl.pallas_call(
        _dispatch_kernel,
        out_shape=jax.ShapeDtypeStruct((P_ROWS, D_MODEL), F32),
        grid=(N_TOK // TM_ROW,),
        in_specs=[pl.BlockSpec(memory_space=pl.ANY),
                  pl.BlockSpec((TM_ROW, D_MODEL), lambda i: (i, 0))],
        out_specs=pl.BlockSpec(memory_space=pl.ANY),
        scratch_shapes=[pltpu.SMEM((8, LANES), jnp.int32),
                        pltpu.SemaphoreType.DMA,
                        pltpu.SemaphoreType.DMA],
        compiler_params=_cparams(1),
        name="dispatch",
    )(pos_tiles, h2)


def _experts_kernel(wt_ref, we_ref, st_ref, en_ref, ni_ref, xs_ref, wg_ref, wu_ref, wd_ref, ys_ref):
    tm = TM_EXP
    w = pl.program_id(0)

    @pl.when(w < ni_ref[0])
    def _():
        tile = wt_ref[w]
        e = we_ref[w]
        first = jnp.logical_or(w == 0, wt_ref[jnp.maximum(w - 1, 0)] != tile)
        x = xs_ref[...].astype(BF16)
        a = jnp.dot(x, wg_ref[...].astype(BF16), preferred_element_type=F32)
        u = jnp.dot(x, wu_ref[...].astype(BF16), preferred_element_type=F32)
        hid = (a * jax.nn.sigmoid(a) * u).astype(BF16)
        y = jnp.dot(hid, wd_ref[...].astype(BF16), preferred_element_type=F32)
        row = tile * tm + lax.broadcasted_iota(jnp.int32, (tm, LANES), 0)
        mine = (row >= st_ref[e]) & (row < en_ref[e])

        @pl.when(first)
        def _():
            for c in range(D_MODEL // LANES):
                cols = slice(c * LANES, (c + 1) * LANES)
                ys_ref[:, cols] = jnp.where(mine, y[:, cols], 0.0)

        @pl.when(jnp.logical_not(first))
        def _():
            for c in range(D_MODEL // LANES):
                cols = slice(c * LANES, (c + 1) * LANES)
                ys_ref[:, cols] = jnp.where(mine, y[:, cols], ys_ref[:, cols])


def _experts(item_tile, item_expert, starts, ends, n_items, xs, wg, wu, wd):
    tm = TM_EXP
    rows = lambda w, wt, we, st, en, ni: (wt[w], 0)
    wmap = lambda w, wt, we, st, en, ni: (we[w], 0, 0)
    grid_spec = pltpu.PrefetchScalarGridSpec(
        num_scalar_prefetch=5,
        grid=(N_ITEMS,),
        in_specs=[pl.BlockSpec((tm, D_MODEL), rows),
                  pl.BlockSpec((None, D_MODEL, D_FF_EXPERT), wmap),
                  pl.BlockSpec((None, D_MODEL, D_FF_EXPERT), wmap),
                  pl.BlockSpec((None, D_FF_EXPERT, D_MODEL), wmap)],
        out_specs=pl.BlockSpec((tm, D_MODEL), rows))
    return pl.pallas_call(
        _experts_kernel,
        out_shape=jax.ShapeDtypeStruct((P_ROWS, D_MODEL), F32),
        grid_spec=grid_spec,
        compiler_params=_cparams(1),
        name="experts",
    )(item_tile, item_expert, starts, ends, n_items, xs, wg, wu, wd)


def _combine_kernel(pos_hbm, x1_ref, rc_ref, gf_ref, gfin_ref, ys_hbm, o_ref,
                    idx, buf_a, buf_b, isem, sems):
    i = pl.program_id(0)
    cp = pltpu.make_async_copy(pos_hbm.at[i], idx, isem)
    cp.start()
    cp.wait()
    bufs = (buf_a, buf_b)

    def body(n, carry):
        for slot in range(2):
            p = _row_index(idx, slot, n)
            pltpu.make_async_copy(ys_hbm.at[pl.ds(p, 1)], bufs[slot].at[pl.ds(n, 1)],
                                  sems.at[slot]).start()
        return carry

    lax.fori_loop(0, TM_ROW, body, 0, unroll=8)
    for slot in range(2):
        pltpu.make_async_copy(ys_hbm.at[pl.ds(0, TM_ROW)], bufs[slot], sems.at[slot]).wait()

    w1 = rc_ref[:, 4:5]
    w2 = rc_ref[:, 5:6]
    y = w1 * buf_a[...] + w2 * buf_b[...]
    xf = x1_ref[...] + gf_ref[...] * y
    rs = lax.rsqrt(jnp.mean(xf * xf, axis=-1, keepdims=True) + EPS)
    o_ref[...] = xf * rs * gfin_ref[...]


def _combine(pos_tiles, x1, rcol, gate_f, g_final, ys):
    tm = TM_ROW
    tpb = SEQ // tm
    row = lambda i: (i, 0)
    return pl.pallas_call(
        _combine_kernel,
        out_shape=jax.ShapeDtypeStruct((N_TOK, D_MODEL), F32),
        grid=(N_TOK // tm,),
        in_specs=[pl.BlockSpec(memory_space=pl.ANY),
                  pl.BlockSpec((tm, D_MODEL), row),
                  pl.BlockSpec((tm, LANES), row),
                  pl.BlockSpec((None, 1, D_MODEL), lambda i: (i // tpb, 0, 0)),
                  pl.BlockSpec((1, D_MODEL), lambda i: (0, 0)),
                  pl.BlockSpec(memory_space=pl.ANY)],
        out_specs=pl.BlockSpec((tm, D_MODEL), row),
        scratch_shapes=[pltpu.SMEM((8, LANES), jnp.int32),
                        pltpu.VMEM((tm, D_MODEL), F32),
                        pltpu.VMEM((tm, D_MODEL), F32),
                        pltpu.SemaphoreType.DMA,
                        pltpu.SemaphoreType.DMA((2,))],
        compiler_params=_cparams(1),
        name="combine",
    )(pos_tiles, x1, rcol, gate_f, g_final, ys)


def _t5_bucket_np():
    qi = np.arange(BLOCK)[:, None]
    kj = np.arange(2 * BLOCK)[None, :]
    dist = qi - kj + BLOCK
    n = np.maximum(dist, 0)
    max_exact = NUM_BUCKETS // 2
    nf = np.maximum(n, 1).astype(np.float32)
    large = max_exact + (np.log(nf / np.float32(max_exact)) / np.float32(math.log(MAX_DISTANCE / max_exact))
                         * np.float32(NUM_BUCKETS - max_exact)).astype(np.int32)
    large = np.minimum(large, NUM_BUCKETS - 1)
    bucket = np.where(n < max_exact, n, large)
    band = (dist >= 0) & (dist < WINDOW)
    return bucket.astype(np.int32), band


def kernel(x, c, w_ada, b_ada, g_norm_mix, g_norm_ffn, w_in, sinks, b_forget, w_proj_swa, w_proj_fox,
           w_out, rel_bias_table, w_router_group, b_router_group, w_router_expert, b_router_expert,
           w_gate_exp, w_up_exp, w_down_exp, g_final):
    l = 0
    x2 = x.reshape(N_TOK, D_MODEL)

    c16 = jnp.concatenate([c, jnp.zeros_like(c)], axis=0)
    mod = _ada(c16, w_ada[l], b_ada[l][None, :])[:BATCH]
    shift_m, scale_m, gate_m, shift_f, scale_f, gate_f = [
        m.reshape(BATCH, 1, D_MODEL) for m in jnp.split(mod, 6, axis=-1)]

    w = w_in[l]
    o_ka, o_va, o_qb = Q_A, Q_A + KV_A, Q_A + 2 * KV_A
    o_kb, o_vb, o_f = o_qb + W_B, o_qb + 2 * W_B, o_qb + 3 * W_B
    o_g = o_f + N_HEADS_FOX

    def dup(cols):
        heads = [cols[:, h * HEAD_DIM:(h + 1) * HEAD_DIM] for h in range(N_KV_HEADS_SWA)]
        return jnp.concatenate([hd for hd in heads for _ in range(2)], axis=1)

    w_main = jnp.concatenate([w[:, :Q_A], dup(w[:, o_ka:o_va]), dup(w[:, o_va:o_qb]),
                              w[:, o_qb:o_f]], axis=1).astype(BF16)
    w_f = jnp.pad(w[:, o_f:o_g], ((0, 0), (0, LANES - N_HEADS_FOX))).astype(BF16)
    w_g = w[:, o_g:].astype(BF16)
    qa, kdup, vdup, qb, kb, vb, f_pad, gates, nrm = _inproj(
        x2, scale_m, shift_m, g_norm_mix[l][None, :], w_main, w_f, w_g)

    b_pad = jnp.pad(b_forget[l], (0, LANES - N_HEADS_FOX))[None, :]
    lanes = np.arange(LANES)
    jmod = jnp.asarray(np.where(lanes < DECAY_LANES * N_HEADS_FOX, lanes % DECAY_LANES, 7)[None, :].astype(np.int32))
    dq, dk, fb = _cum(f_pad, b_pad, jmod)

    bucket, band = _t5_bucket_np()
    onehot = jnp.asarray(bucket[None] == np.arange(NUM_BUCKETS)[:, None, None], dtype=F32)
    bias = jnp.einsum("bh,bqk->hqk", rel_bias_table.astype(F32), onehot, precision=HIGHEST)
    bias = jnp.where(band[None], bias, NEG_INF)
    first = np.arange(2 * BLOCK)[None, None, :] < BLOCK
    bias = jnp.stack([jnp.where(first, NEG_INF, bias), bias]).reshape(2, N_KV_HEADS_SWA, -1, 2 * BLOCK)
    o_a = _swa(sinks[l].astype(F32), qa, kdup, vdup, bias)

    o_b = _fox(_fox_first_tiles(nrm, fb), qb, kb, vb, dq, dk)

    w_r = jnp.concatenate([w_router_group[l]] + [w_router_expert[l][g] for g in range(N_GROUPS)], axis=1)
    w_r = jnp.pad(w_r, ((0, 0), (0, LANES - w_r.shape[1])))
    wr_hi = w_r.astype(BF16)
    wr_lo = (w_r - wr_hi.astype(F32)).astype(BF16)
    b_r = jnp.concatenate([b_router_group[l], b_router_expert[l].reshape(-1)])
    b_r = jnp.pad(b_r, (0, LANES - b_r.shape[0]))[None, :]
    x1, h2, rcol, cnt = _post(x2, o_a, o_b, gates, gate_m, scale_f, shift_f, g_norm_ffn[l][None, :],
                              w_proj_swa[l].astype(BF16), w_proj_fox[l].astype(BF16), w_out[l].astype(BF16),
                              wr_hi, wr_lo, b_r)

    i32 = jnp.int32
    counts = cnt[0, :N_EXPERTS].astype(i32)
    ends = jnp.cumsum(counts).astype(i32)
    starts = ends - counts
    e12 = rcol[:, 0:2].astype(i32)
    r12 = rcol[:, 2:4].astype(i32)
    expert_ids = jnp.arange(N_EXPERTS, dtype=i32)
    pos = jnp.sum(jnp.where(e12[..., None] == expert_ids, starts, 0), axis=-1) + r12
    pos_tiles = pos.reshape(N_TOK // TM_ROW, TM_ROW, 2).transpose(0, 2, 1).reshape(N_TOK // TM_ROW, 8, LANES)
    first_tile = starts // TM_EXP
    last_tile = (ends - 1) // TM_EXP
    per_expert = jnp.where(counts > 0, last_tile - first_tile + 1, 0)
    item_end = jnp.cumsum(per_expert).astype(i32)
    item_start = item_end - per_expert
    n_items = item_end[-1:]
    w = jnp.minimum(jnp.arange(N_ITEMS, dtype=i32), n_items - 1)
    item_expert = jnp.sum((item_end[None, :] <= w[:, None]).astype(i32), axis=1)
    sel = item_expert[:, None] == expert_ids
    item_tile = jnp.sum(jnp.where(sel, first_tile - item_start, 0), axis=1) + w

    xs = _dispatch(pos_tiles, h2)
    ys = _experts(item_tile.astype(i32), item_expert.astype(i32), starts, ends, n_items, xs,
                  w_gate_exp[l].reshape(N_EXPERTS, D_MODEL, D_FF_EXPERT),
                  w_up_exp[l].reshape(N_EXPERTS, D_MODEL, D_FF_EXPERT),
                  w_down_exp[l].reshape(N_EXPERTS, D_FF_EXPERT, D_MODEL))
    out = _combine(pos_tiles, x1, rcol, gate_f, g_final[None, :], ys)
    return out.reshape(BATCH, SEQ, D_MODEL)
```

```python
import functools
import math

import numpy as np
import jax
import jax.numpy as jnp
from jax import lax
from jax.experimental import pallas as pl
from jax.experimental.pallas import tpu as pltpu

F32 = jnp.float32
BF16 = jnp.bfloat16
HIGHEST = lax.Precision.HIGHEST

D_MODEL = 1024
BATCH = 8
SEQ = 4096
N_TOK = BATCH * SEQ
N_HEADS_SWA = 8
N_KV_HEADS_SWA = 2
N_HEADS_FOX = 8
HEAD_DIM = 64
WINDOW = 128
BLOCK = 128
NUM_BUCKETS = 32
MAX_DISTANCE = 128
N_GROUPS = 4
EXPERTS_PER_GROUP = 8
N_EXPERTS = N_GROUPS * EXPERTS_PER_GROUP
D_FF_EXPERT = 256
EPS = 1e-6
NEG_INF = -1e30

Q_A = N_HEADS_SWA * HEAD_DIM
KV_A = N_KV_HEADS_SWA * HEAD_DIM
W_B = N_HEADS_FOX * HEAD_DIM
LANES = 128
QK_SCALE = HEAD_DIM ** -0.5

TM_IN = 512
TM_POST = 512
TQ_FOX = 512
TK_FOX = TQ_FOX
TM_EXP = 512
TM_ROW = 512
UNIT = 16
XY_UNITS = 2 * TM_POST // UNIT + N_EXPERTS
XY_ROWS = XY_UNITS * UNIT
XY_COLS = D_MODEL + LANES
UNITS_PER_TILE = TM_EXP // UNIT
N_EXP_TILES = (N_TOK // TM_POST) * XY_UNITS // UNITS_PER_TILE + N_EXPERTS
VMEM_LIMIT = 56 * 1024 * 1024

DECAY_LANES = 6
PRUNE_MARGIN = 110.0


def _cparams(n_axes):
    return pltpu.CompilerParams(dimension_semantics=("arbitrary",) * n_axes,
                                vmem_limit_bytes=VMEM_LIMIT)


def _ada_kernel(c_ref, w_ref, b_ref, o_ref):
    c = c_ref[...]
    ca = c * jax.nn.sigmoid(c)
    o_ref[...] = jnp.dot(ca.astype(BF16), w_ref[...].astype(BF16),
                         preferred_element_type=F32) + b_ref[...]


def _ada(c16, w_ada, b_ada):
    n_out = w_ada.shape[1]
    blk = 1024
    return pl.pallas_call(
        _ada_kernel,
        out_shape=jax.ShapeDtypeStruct((16, n_out), F32),
        grid=(n_out // blk,),
        in_specs=[pl.BlockSpec((16, D_MODEL), lambda j: (0, 0)),
                  pl.BlockSpec((D_MODEL, blk), lambda j: (0, j)),
                  pl.BlockSpec((1, blk), lambda j: (0, j))],
        out_specs=pl.BlockSpec((16, blk), lambda j: (0, j)),
        compiler_params=_cparams(1),
        name="ada",
    )(c16, w_ada, b_ada)


def _inproj_kernel(x_ref, sc_ref, sh_ref, g_ref, wm_ref, wf_ref, wg_ref, ind_ref,
                   qa_ref, kd_ref, vd_ref, qb_ref, kb_ref, vb_ref, f_ref, gt_ref, nrm_ref):
    x = x_ref[...]
    rs = lax.rsqrt(jnp.mean(x * x, axis=-1, keepdims=True) + EPS)
    a = g_ref[...] * (1.0 + sc_ref[...])
    h = (x * rs * a + sh_ref[...]).astype(BF16)

    def mm(w):
        return jnp.dot(h, w, preferred_element_type=F32)

    qa_ref[...] = (mm(wm_ref[:, 0:512]) * QK_SCALE).astype(BF16)
    kd_ref[...] = mm(wm_ref[:, 512:768]).astype(BF16)
    vd_ref[...] = mm(wm_ref[:, 768:1024]).astype(BF16)
    qb = (mm(wm_ref[:, 1024:1536]) * QK_SCALE).astype(BF16)
    kb = mm(wm_ref[:, 1536:2048]).astype(BF16)
    qb_ref[...] = qb
    kb_ref[...] = kb
    vb_ref[...] = mm(wm_ref[:, 2048:2560]).astype(BF16)
    sq = jnp.concatenate([qb, kb], axis=1).astype(F32)
    seg = jnp.dot((sq * sq).astype(BF16), ind_ref[...], preferred_element_type=F32)
    nrm_ref[...] = jnp.broadcast_to(jnp.max(seg, axis=0, keepdims=True), nrm_ref.shape)
    f_ref[...] = mm(wf_ref[...])
    gt_ref[...] = mm(wg_ref[...]).astype(BF16)


def _inproj(x2, scale_m, shift_m, g_mix, w_main, w_f, w_g):
    tm = TM_IN
    tpb = SEQ // tm
    row = lambda i: (i, 0)
    per_b = lambda i: (i // tpb, 0, 0)
    const = lambda i: (0, 0)
    outs = [(Q_A, BF16), (2 * KV_A, BF16), (2 * KV_A, BF16), (W_B, BF16), (W_B, BF16), (W_B, BF16),
            (LANES, F32), (2 * D_MODEL, BF16)]
    ind_np = np.zeros((2 * W_B, LANES), np.float32)
    ind_np[np.arange(2 * W_B), np.arange(2 * W_B) // HEAD_DIM] = 1.0
    ind = jnp.asarray(ind_np, dtype=BF16)
    n_steps = N_TOK // tm
    return pl.pallas_call(
        _inproj_kernel,
        out_shape=[jax.ShapeDtypeStruct((N_TOK, w), dt) for w, dt in outs]
        + [jax.ShapeDtypeStruct((n_steps * 8, LANES), F32)],
        grid=(n_steps,),
        in_specs=[pl.BlockSpec((tm, D_MODEL), row),
                  pl.BlockSpec((None, 1, D_MODEL), per_b),
                  pl.BlockSpec((None, 1, D_MODEL), per_b),
                  pl.BlockSpec((1, D_MODEL), const),
                  pl.BlockSpec(w_main.shape, const),
                  pl.BlockSpec(w_f.shape, const),
                  pl.BlockSpec(w_g.shape, const),
                  pl.BlockSpec(ind.shape, const)],
        out_specs=[pl.BlockSpec((tm, w), row) for w, _ in outs] + [pl.BlockSpec((8, LANES), row)],
        compiler_params=_cparams(1),
        name="inproj",
    )(x2, scale_m, shift_m, g_mix, w_main, w_f, w_g, ind)


def _log_sigmoid(x):
    return jnp.minimum(x, 0.0) - jnp.log1p(jnp.exp(-jnp.abs(x)))


def _cum_kernel(f_ref, b_ref, jm_ref, qa_ref, ka_ref, fb_ref):
    r = lax.broadcasted_iota(jnp.int32, (LANES, LANES), 0)
    c = lax.broadcasted_iota(jnp.int32, (LANES, LANES), 1)
    expand = ((c >= DECAY_LANES * r) & (c < DECAY_LANES * r + DECAY_LANES)
              & (r < N_HEADS_FOX)).astype(F32)
    tril = (r >= c).astype(F32)
    lf = _log_sigmoid(f_ref[...] + b_ref[...])
    lfe = jnp.dot(lf, expand, precision=HIGHEST, preferred_element_type=F32)
    jm = jm_ref[...]
    carry = jnp.zeros((1, LANES), F32)
    for blk in range(SEQ // LANES):
        rows = slice(blk * LANES, (blk + 1) * LANES)
        cb = jnp.dot(tril, lfe[rows], precision=HIGHEST, preferred_element_type=F32) + carry
        carry = cb[LANES - 1:LANES]
        hi = cb.astype(BF16).astype(F32)
        r1 = cb - hi
        mid = r1.astype(BF16).astype(F32)
        lo = (r1 - mid).astype(BF16).astype(F32)
        one = jnp.ones_like(cb)
        zero = jnp.zeros_like(cb)
        qa = jnp.where(jm == 0, hi, jnp.where(jm == 1, mid, jnp.where(jm == 2, lo,
                       jnp.where(jm < DECAY_LANES, one, zero))))
        ka = jnp.where(jm == 3, -hi, jnp.where(jm == 4, -mid, jnp.where(jm == 5, -lo,
                       jnp.where(jm < 3, one, zero))))
        qa_ref[rows, :] = qa.astype(BF16)
        ka_ref[rows, :] = ka.astype(BF16)
        blocks_per_tile = TQ_FOX // LANES
        tile = blk // blocks_per_tile
        if blk % blocks_per_tile == 0:
            fb_ref[2 * tile:2 * tile + 1, :] = cb[0:1]
        if blk % blocks_per_tile == blocks_per_tile - 1:
            fb_ref[2 * tile + 1:2 * tile + 2, :] = carry


def _cum(f_pad, b_pad, jmod):
    n_tiles = SEQ // TQ_FOX
    return pl.pallas_call(
        _cum_kernel,
        out_shape=[jax.ShapeDtypeStruct((BATCH, SEQ, LANES), BF16)] * 2
        + [jax.ShapeDtypeStruct((BATCH, 2 * n_tiles, LANES), F32)],
        grid=(BATCH,),
        in_specs=[pl.BlockSpec((SEQ, LANES), lambda b: (b, 0)),
                  pl.BlockSpec((1, LANES), lambda b: (0, 0)),
                  pl.BlockSpec((1, LANES), lambda b: (0, 0))],
        out_specs=[pl.BlockSpec((None, SEQ, LANES), lambda b: (b, 0, 0))] * 2
        + [pl.BlockSpec((None, 2 * n_tiles, LANES), lambda b: (b, 0, 0))],
        compiler_params=_cparams(1),
        name="cum",
    )(f_pad, b_pad, jmod)


def _swa_kernel(sink_ref, q_ref, kc_ref, kp_ref, vc_ref, vp_ref, bias_ref, o_ref):
    lane = lax.broadcasted_iota(jnp.int32, (BLOCK, LANES), 1)
    lo = lane < HEAD_DIM
    kk =jnp.concatenate([kp_ref[...], kc_ref[...]], axis=0)
    vv = jnp.concatenate([vp_ref[...], vc_ref[...]], axis=0)
    grp = N_HEADS_SWA // N_KV_HEADS_SWA
    for g in range(N_KV_HEADS_SWA):
        parts = []
        for t in range(2):
            qt = q_ref[:, (2 * g + t) * LANES:(2 * g + t + 1) * LANES]
            zero = jnp.zeros_like(qt)
            parts.append(jnp.where(lo, qt, zero))
            parts.append(jnp.where(lo, zero, qt))
        q4 = jnp.concatenate(parts, axis=0)
        s = lax.dot_general(q4, kk[:, g * LANES:(g + 1) * LANES], (((1,), (1,)), ((), ())),
                            preferred_element_type=F32)
        s = s + bias_ref[g]
        outs = []
        for hh in range(grp):
            sink = sink_ref[g * grp + hh]
            sl = s[hh * BLOCK:(hh + 1) * BLOCK]
            m =jnp.maximum(jnp.max(sl, axis=-1, keepdims=True), sink)
            p = jnp.exp(sl - m)
            den = jnp.sum(p, axis=-1, keepdims=True) + jnp.exp(sink - m)
            o = jnp.dot(p.astype(BF16), vv[:, g * LANES:(g + 1) * LANES],
                        preferred_element_type=F32)
            outs.append(o / den)
        o_ref[:, (2 * g) * LANES:(2 * g + 1) * LANES] = jnp.where(lo, outs[0], outs[1]).astype(BF16)
        o_ref[:, (2 * g + 1) * LANES:(2 * g + 2) * LANES] = jnp.where(lo, outs[2], outs[3]).astype(BF16)


def _swa(sinks, qa, kdup, vdup, bias):
    nb = SEQ // BLOCK
    cur = lambda b, i, s: (b * nb + i, 0)
    prev = lambda b, i, s: (b * nb + jnp.maximum(i - 1, 0), 0)
    grid_spec = pltpu.PrefetchScalarGridSpec(
        num_scalar_prefetch=1,
        grid=(BATCH, nb),
        in_specs=[pl.BlockSpec((BLOCK, Q_A), cur),
                  pl.BlockSpec((BLOCK, 2 * KV_A), cur),
                  pl.BlockSpec((BLOCK, 2 * KV_A), prev),
                  pl.BlockSpec((BLOCK, 2 * KV_A), cur),
                  pl.BlockSpec((BLOCK, 2 * KV_A), prev),
                  pl.BlockSpec((None,) + bias.shape[1:], lambda b, i, s: (jnp.minimum(i, 1), 0, 0, 0))],
        out_specs=pl.BlockSpec((BLOCK, Q_A), cur))
    return pl.pallas_call(
        _swa_kernel,
        out_shape=jax.ShapeDtypeStruct((N_TOK, Q_A), BF16),
        grid_spec=grid_spec,
        compiler_params=_cparams(2),
        name="swa",
    )(sinks, qa, kdup, kdup, vdup, vdup, bias)


def _fox_kernel(js_ref, q_ref, k_ref, v_ref, qa_ref, ka_ref, o_ref, kaug, vaug, q2, m_sc, acc_sc):
    tq, tk = TQ_FOX, TK_FOX
    b = pl.program_id(0)
    t = pl.program_id(1)
    i = pl.program_id(2)
    j_start = js_ref[(b * pl.num_programs(1) + t) * pl.num_programs(2) + i]

    @pl.when(i == 0)
    def _():
        kaug[:, 0:LANES] = k_ref[...]
        kaug[:, LANES:2 * LANES] = ka_ref[...]
        vaug[:, 0:LANES] = v_ref[...]
        vaug[:, LANES:2 * LANES] = jnp.ones((SEQ, LANES), BF16)

    lane = lax.broadcasted_iota(jnp.int32, (tq, LANES), 1)
    lo = lane < HEAD_DIM
    base = 2 * DECAY_LANES * t
    own = [(lane >= base + h * DECAY_LANES) & (lane < base + (h + 1) * DECAY_LANES) for h in range(2)]
    q = q_ref[...]
    qa = qa_ref[...]
    zero = jnp.zeros_like(q)
    q2[0, :, 0:LANES] = jnp.where(lo, q, zero)
    q2[1, :, 0:LANES] = jnp.where(lo, zero, q)
    for h in range(2):
        q2[h, :, LANES:2 * LANES] = jnp.where(own[h], qa, zero)
    m_sc[...] = jnp.full(m_sc.shape, NEG_INF, F32)
    acc_sc[...] = jnp.zeros(acc_sc.shape, F32)

    def chain_step(h, ks, mask):
        s = lax.dot_general(q2[h], kaug[pl.ds(ks, tk), :], (((1,), (1,)), ((), ())),
                            preferred_element_type=F32)
        if mask is not None:
            s = jnp.where(mask, s, NEG_INF)
        m_prev = m_sc[h]
        m_new = jnp.maximum(m_prev, jnp.max(s, axis=-1, keepdims=True))
        alpha = jnp.exp(m_prev - m_new)
        p = jnp.exp(s - jnp.concatenate([m_new] * (tk // LANES), axis=1))
        pv = jnp.dot(p.astype(BF16), vaug[pl.ds(ks, tk), :], preferred_element_type=F32)
        acc_sc[h] = jnp.concatenate([alpha, alpha], axis=1) * acc_sc[h] + pv
        m_sc[h] = m_new

    def step(j, carry):
        ks = pl.multiple_of(j * tk, tk)
        for h in range(2):
            chain_step(h, ks, None)
        return carry

    lax.fori_loop(j_start, i, step, 0)

    ks = pl.multiple_of(i * tk, tk)
    rr = lax.broadcasted_iota(jnp.int32, (tq, tk), 0)
    cc = lax.broadcasted_iota(jnp.int32, (tq, tk), 1)
    for h in range(2):
        chain_step(h, ks, cc <= rr)

    outs = [acc_sc[h, :, 0:LANES] / acc_sc[h, :, LANES:2 * LANES] for h in range(2)]
    o_ref[...] = jnp.where(lo, outs[0], outs[1]).astype(BF16)


def _fox(j_start, qb, kb, vb, qa, ka):
    tq = TQ_FOX
    nq = SEQ // tq
    n_pairs = N_HEADS_FOX // 2
    qmap = lambda b, t, i, js: (b * nq + i, t)
    kmap = lambda b, t, i, js: (b, t)
    grid_spec = pltpu.PrefetchScalarGridSpec(
        num_scalar_prefetch=1,
        grid=(BATCH, n_pairs, nq),
        in_specs=[pl.BlockSpec((tq, LANES), qmap),
                  pl.BlockSpec((SEQ, LANES), kmap),
                  pl.BlockSpec((SEQ, LANES), kmap),
                  pl.BlockSpec((None, tq, LANES), lambda b, t, i, js: (b, i, 0)),
                  pl.BlockSpec((None, SEQ, LANES), lambda b, t, i, js: (b, 0, 0))],
        out_specs=pl.BlockSpec((tq, LANES), qmap),
        scratch_shapes=[pltpu.VMEM((SEQ, 2 * LANES), BF16),
                        pltpu.VMEM((SEQ, 2 * LANES), BF16),
                        pltpu.VMEM((2, tq, 2 * LANES), BF16),
                        pltpu.VMEM((2, tq, LANES), F32),
                        pltpu.VMEM((2, tq, 2 * LANES), F32)])
    return pl.pallas_call(
        _fox_kernel,
        out_shape=jax.ShapeDtypeStruct((N_TOK, W_B), BF16),
        grid_spec=grid_spec,
        compiler_params=_cparams(3),
        name="fox",
    )(j_start, qb, kb, vb, qa, ka)


def _fox_first_tiles(nrm, fb):
    n_tiles = SEQ // TQ_FOX
    nr = nrm.reshape(BATCH, n_tiles, 8, LANES)[:, :, 0, :] * 1.02
    qn = jnp.sqrt(nr[..., 0:N_HEADS_FOX])
    kn = jnp.sqrt(nr[..., N_HEADS_FOX:2 * N_HEADS_FOX])
    f_first = fb[:, 0::2, 0:DECAY_LANES * N_HEADS_FOX:DECAY_LANES]
    f_last = fb[:, 1::2, 0:DECAY_LANES * N_HEADS_FOX:DECAY_LANES]
    kn_prefix = lax.cummax(kn, axis=1)
    upper = qn[:, :, None, :] * kn_prefix[:, None, :, :] + f_first[:, :, None, :] - f_last[:, None, :, :]
    row_max_low = -(qn * kn)[:, :, None, :]
    ii = jnp.arange(n_tiles)[None, :, None, None]
    jj = jnp.arange(n_tiles)[None, None, :, None]
    skip = (upper < row_max_low - PRUNE_MARGIN) & (jj < ii)
    skip = jnp.all(skip.reshape(BATCH, n_tiles, n_tiles, N_HEADS_FOX // 2, 2), axis=-1)
    first = jnp.sum(jnp.cumprod(skip.astype(jnp.int32), axis=2), axis=2)
    return jnp.transpose(first, (0, 2, 1)).reshape(-1).astype(jnp.int32)


def _post_kernel(x_ref, oa_ref, ob_ref, gt_ref, gm_ref, sc_ref, sh_ref, g_ref,
                 wa_ref, wb_ref, wo_ref, wrh_ref, wrl_ref, br_ref,
                 x1_ref, xy_ref, rc_ref, cu_ref):
    tm = TM_POST
    pa =jnp.dot(oa_ref[...], wa_ref[...], preferred_element_type=F32)
    pb = jnp.dot(ob_ref[...], wb_ref[...], preferred_element_type=F32)
    ga = jax.nn.sigmoid(gt_ref[:, 0:D_MODEL].astype(F32))
    gb = jax.nn.sigmoid(gt_ref[:, D_MODEL:2 * D_MODEL].astype(F32))
    merged = (ga * pa + gb * pb).astype(BF16)
    y = jnp.dot(merged, wo_ref[...], preferred_element_type=F32)
    x1 = x_ref[...] + gm_ref[...] * y
    x1_ref[...] = x1

    rs = lax.rsqrt(jnp.mean(x1 * x1, axis=-1, keepdims=True) + EPS)
    a = g_ref[...] * (1.0 + sc_ref[...])
    h2 = x1 * rs * a + sh_ref[...]

    hh = h2.astype(BF16)
    hl = (h2 - hh.astype(F32)).astype(BF16)
    logits = (jnp.dot(hh, wrh_ref[...], preferred_element_type=F32)
              + jnp.dot(hl, wrh_ref[...], preferred_element_type=F32)
              + jnp.dot(hh, wrl_ref[...], preferred_element_type=F32)
              + br_ref[...])

    lane = lax.broadcasted_iota(jnp.int32, (tm, LANES), 1).astype(F32)
    big = float(LANES)
    gl = jnp.where(lane < N_GROUPS, logits, -jnp.inf)
    gmax = jnp.max(gl, axis=-1, keepdims=True)
    gi = jnp.min(jnp.where(gl == gmax, lane, big), axis=-1, keepdims=True)
    gsum = jnp.sum(jnp.exp(gl - gmax), axis=-1, keepdims=True)
    gp = 1.0 / gsum
    e_lo = N_GROUPS + EXPERTS_PER_GROUP * gi
    el = jnp.where((lane >= e_lo) & (lane < e_lo + EXPERTS_PER_GROUP), logits, -jnp.inf)
    v1 = jnp.max(el, axis=-1, keepdims=True)
    i1 = jnp.min(jnp.where(el == v1, lane, big), axis=-1, keepdims=True)
    el2 = jnp.where(lane == i1, -jnp.inf, el)
    v2 = jnp.max(el2, axis=-1, keepdims=True)
    i2 = jnp.min(jnp.where(el2 == v2, lane, big), axis=-1, keepdims=True)
    e21 = jnp.exp(v2 - v1)
    w1 = gp / (1.0 + e21)
    w2 = gp * e21 / (1.0 + e21)
    e1 = i1 - N_GROUPS
    e2 = i2 - N_GROUPS

    oh = jnp.where((lane == e1) | (lane == e2), 1.0, 0.0)
    cnt_u = jnp.floor((jnp.sum(oh, axis=0, keepdims=True) + (UNIT - 1)) * (1.0 / UNIT))
    r128 = lax.broadcasted_iota(jnp.int32, (LANES, LANES), 0)
    c128 = lax.broadcasted_iota(jnp.int32, (LANES, LANES), 1)
    before_lane = jnp.where(r128 < c128, 1.0, 0.0).astype(BF16)
    loc_u = jnp.dot(jnp.broadcast_to(cnt_u, (8, LANES)).astype(BF16), before_lane,
                    preferred_element_type=F32)
    rr = lax.broadcasted_iota(jnp.int32, (tm, tm), 0)
    cc = lax.broadcasted_iota(jnp.int32, (tm, tm), 1)
    strict = jnp.where(rr > cc, 1.0, 0.0).astype(BF16)
    pos_e = jnp.dot(strict, oh.astype(BF16), preferred_element_type=F32) + loc_u[0:1] * UNIT
    lp1 = jnp.sum(jnp.where(lane == e1, pos_e, 0.0), axis=-1, keepdims=True)
    lp2 = jnp.sum(jnp.where(lane == e2, pos_e, 0.0), axis=-1, keepdims=True)

    eye = rr == cc
    ones8 = jnp.ones((8, tm), BF16)

    def to_row(col):
        hi = jnp.floor(col * (1.0 / 64.0))
        lo_ = col - 64.0 * hi
        d_hi = jnp.where(eye, hi, 0.0).astype(BF16)
        d_lo = jnp.where(eye, lo_, 0.0).astype(BF16)
        row8 = (64.0 * jnp.dot(ones8, d_hi, preferred_element_type=F32)
                + jnp.dot(ones8, d_lo, preferred_element_type=F32))
        return row8[0:1]

    srow = lax.broadcasted_iota(jnp.int32, (XY_ROWS, tm), 0).astype(F32)
    pm1 = jnp.where(srow == to_row(lp1), 1.0, 0.0).astype(BF16)
    pm2 = jnp.where(srow == to_row(lp2), 1.0, 0.0).astype(BF16)
    xy_ref[:, 0:D_MODEL] = jnp.dot(pm1 + pm2, hh, preferred_element_type=F32).astype(BF16)
    w1h = w1.astype(BF16).astype(F32)
    w2h = w2.astype(BF16).astype(F32)
    a1 = jnp.where(lane == 4, w1h, jnp.where(lane == 6, w1 - w1h, 0.0)).astype(BF16)
    a2 = jnp.where(lane == 5, w2h, jnp.where(lane == 7, w2 - w2h, 0.0)).astype(BF16)
    wl = jnp.dot(pm1, a1, preferred_element_type=F32) + jnp.dot(pm2, a2, preferred_element_type=F32)
    xy_ref[:, D_MODEL:D_MODEL + LANES] = wl.astype(BF16)

    cu_ref[...] = jnp.broadcast_to(cnt_u, cu_ref.shape)
    rc_ref[...] = jnp.where(lane == 0, lp1, jnp.where(lane == 1, lp2, 0.0))


def _post(x2, oa, ob, gates, gate_m, scale_f, shift_f, g_ffn, wa, wb, wo, wr_hi, wr_lo, b_r):
    tm = TM_POST
    tpb = SEQ // tm
    row = lambda i: (i, 0)
    per_b = lambda i: (i // tpb, 0, 0)
    const = lambda i: (0, 0)
    n_steps = N_TOK // tm
    return pl.pallas_call(
        _post_kernel,
        out_shape=[jax.ShapeDtypeStruct((N_TOK, D_MODEL), F32),
                   jax.ShapeDtypeStruct((n_steps * XY_ROWS, XY_COLS), BF16),
                   jax.ShapeDtypeStruct((N_TOK, LANES), F32),
                   jax.ShapeDtypeStruct((n_steps * 8, LANES), F32)],
        grid=(n_steps,),
        in_specs=[pl.BlockSpec((tm, D_MODEL), row),
                  pl.BlockSpec((tm, Q_A), row),
                  pl.BlockSpec((tm, W_B), row),
                  pl.BlockSpec((tm, 2 * D_MODEL), row),
                  pl.BlockSpec((None, 1, D_MODEL), per_b),
                  pl.BlockSpec((None, 1, D_MODEL), per_b),
                  pl.BlockSpec((None, 1, D_MODEL), per_b),
                  pl.BlockSpec((1, D_MODEL), const),
                  pl.BlockSpec(wa.shape, const),
                  pl.BlockSpec(wb.shape, const),
                  pl.BlockSpec(wo.shape, const),
                  pl.BlockSpec(wr_hi.shape, const),
                  pl.BlockSpec(wr_lo.shape, const),
                  pl.BlockSpec((1, LANES), const)],
        out_specs=[pl.BlockSpec((tm, D_MODEL), row),
                   pl.BlockSpec((XY_ROWS, XY_COLS), row),
                   pl.BlockSpec((tm, LANES), row),
                   pl.BlockSpec((8, LANES), row)],
        compiler_params=_cparams(1),
        name="post",
    )(x2, oa, ob, gates, gate_m, scale_f, shift_f, g_ffn, wa, wb, wo, wr_hi, wr_lo, b_r)


def _experts_kernel(te_ref, nv_ref, ur_ref, xy_in, wg_ref, wu_ref, wd_ref, xy_out,
                    xbuf, ybuf, gsem, ssem):
    del xy_in
    r = pl.program_id(0)
    last = pl.num_programs(0) - 1
    slot = lax.rem(r, 2)

    def unit_row(step, s):
        return pl.multiple_of(ur_ref[step * UNITS_PER_TILE + s], UNIT)

    def gather(step, sl, s):
        return pltpu.make_async_copy(xy_out.at[pl.ds(unit_row(step, s), UNIT), :],
                                     xbuf.at[sl, pl.ds(pl.multiple_of(s * UNIT, UNIT), UNIT), :],
                                     gsem.at[sl])

    def scatter(step, sl, s):
        return pltpu.make_async_copy(ybuf.at[sl, pl.ds(pl.multiple_of(s * UNIT, UNIT), UNIT), :],
                                     xy_out.at[pl.ds(unit_row(step, s), UNIT), pl.ds(0, D_MODEL)],
                                     ssem.at[sl])

    def for_units(step, fn):
        def body(s, carry):
            fn(s)
            return carry
        lax.fori_loop(0, nv_ref[step], body, 0)

    @pl.when(r == 0)
    def _():
        xbuf[...] = jnp.zeros(xbuf.shape, BF16)
        for_units(0, lambda s: gather(0, 0, s).start())

    @pl.when(r < last)
    def _():
        for_units(r + 1, lambda s: gather(r + 1, 1 - slot, s).start())

    for_units(r, lambda s: gather(r, slot, s).wait())

    @pl.when(r >= 2)
    def _():
        for_units(r - 2, lambda s: scatter(r - 2, slot, s).wait())

    @pl.when(nv_ref[r] > 0)
    def _():
        x = xbuf[slot, :, 0:D_MODEL]
        wrow = jnp.sum(xbuf[slot, :, D_MODEL:XY_COLS].astype(F32), axis=-1, keepdims=True)
        a = jnp.dot(x, wg_ref[...].astype(BF16), preferred_element_type=F32)
        u = jnp.dot(x, wu_ref[...].astype(BF16), preferred_element_type=F32)
        hid = (a * jax.nn.sigmoid(a) * u * wrow).astype(BF16)
        ybuf[slot] = jnp.dot(hid, wd_ref[...].astype(BF16), preferred_element_type=F32).astype(BF16)

    for_units(r, lambda s: scatter(r, slot, s).start())

    @pl.when(r == last)
    def _():
        for_units(r, lambda s: scatter(r, slot, s).wait())

        @pl.when(r >= 1)
        def _():
            for_units(r - 1, lambda s: scatter(r - 1, 1 - slot, s).wait())


def _experts(tile_expert, n_valid, unit_rows, xy, wg, wu, wd):
    wmap = lambda r, te, nv, ur: (te[r], 0, 0)
    grid_spec = pltpu.PrefetchScalarGridSpec(
        num_scalar_prefetch=3,
        grid=(N_EXP_TILES,),
        in_specs=[pl.BlockSpec(memory_space=pl.ANY),
                  pl.BlockSpec((None, D_MODEL, D_FF_EXPERT), wmap),
                  pl.BlockSpec((None, D_MODEL, D_FF_EXPERT), wmap),
                  pl.BlockSpec((None, D_FF_EXPERT, D_MODEL), wmap)],
        out_specs=pl.BlockSpec(memory_space=pl.ANY),
        scratch_shapes=[pltpu.VMEM((2, TM_EXP, XY_COLS), BF16),
                        pltpu.VMEM((2, TM_EXP, D_MODEL), BF16),
                        pltpu.SemaphoreType.DMA((2,)),
                        pltpu.SemaphoreType.DMA((2,))])
    return pl.pallas_call(
        _experts_kernel,
        out_shape=jax.ShapeDtypeStruct(xy.shape, xy.dtype),
        grid_spec=grid_spec,
        input_output_aliases={3: 0},
        compiler_params=_cparams(1),
        name="experts",
    )(tile_expert, n_valid, unit_rows, xy, wg, wu, wd)


def _combine_kernel(x1_ref, rc_ref, gf_ref, gfin_ref, y_ref, o_ref):
    lp1 = rc_ref[:, 0:1]
    lp2 = rc_ref[:, 1:2]
    scol = lax.broadcasted_iota(jnp.int32, (TM_ROW, XY_ROWS), 1).astype(F32)
    pick = jnp.where((scol == lp1) | (scol == lp2), 1.0, 0.0).astype(BF16)
    y = jnp.dot(pick, y_ref[...], preferred_element_type=F32)
    xf = x1_ref[...] + gf_ref[...] * y
    rs = lax.rsqrt(jnp.mean(xf * xf, axis=-1, keepdims=True) + EPS)
    o_ref[...] = xf * rs * gfin_ref[...]


def _combine(x1, rcol, gate_f, g_final, xy):
    tm = TM_ROW
    tpb = SEQ // tm
    row = lambda i: (i, 0)
    return pl.pallas_call(
        _combine_kernel,
        out_shape=jax.ShapeDtypeStruct((N_TOK, D_MODEL), F32),
        grid=(N_TOK // tm,),
        in_specs=[pl.BlockSpec((tm, D_MODEL), row),
                  pl.BlockSpec((tm, LANES), row),
                  pl.BlockSpec((None, 1, D_MODEL), lambda i: (i // tpb, 0, 0)),
                  pl.BlockSpec((1, D_MODEL), lambda i: (0, 0)),
                  pl.BlockSpec((XY_ROWS, D_MODEL), row)],
        out_specs=pl.BlockSpec((tm, D_MODEL), row),
        compiler_params=_cparams(1),
        name="combine",
    )(x1, rcol, gate_f, g_final, xy)


def _t5_bucket_np():
    qi = np.arange(BLOCK)[:, None]
    kj = np.arange(2 * BLOCK)[None, :]
    dist = qi - kj + BLOCK
    n = np.maximum(dist, 0)
    max_exact = NUM_BUCKETS // 2
    nf = np.maximum(n, 1).astype(np.float32)
    large = max_exact + (np.log(nf / np.float32(max_exact)) / np.float32(math.log(MAX_DISTANCE / max_exact))
                         * np.float32(NUM_BUCKETS - max_exact)).astype(np.int32)
    large = np.minimum(large, NUM_BUCKETS - 1)
    bucket = np.where(n < max_exact, n, large)
    band = (dist >= 0) & (dist < WINDOW)
    return bucket.astype(np.int32), band


def kernel(x, c, w_ada, b_ada, g_norm_mix, g_norm_ffn, w_in, sinks, b_forget, w_proj_swa, w_proj_fox,
           w_out, rel_bias_table, w_router_group, b_router_group, w_router_expert, b_router_expert,
           w_gate_exp, w_up_exp, w_down_exp, g_final):
    l = 0
    x2 = x.reshape(N_TOK, D_MODEL)

    c16 = jnp.concatenate([c, jnp.zeros_like(c)], axis=0)
    mod = _ada(c16, w_ada[l], b_ada[l][None, :])[:BATCH]
    shift_m, scale_m, gate_m, shift_f, scale_f, gate_f = [
        m.reshape(BATCH, 1, D_MODEL) for m in jnp.split(mod, 6, axis=-1)]

    w = w_in[l]
    o_ka, o_va, o_qb = Q_A, Q_A + KV_A, Q_A + 2 * KV_A
    o_kb, o_vb, o_f = o_qb + W_B, o_qb + 2 * W_B, o_qb + 3 * W_B
    o_g = o_f + N_HEADS_FOX

    def dup(cols):
        heads = [cols[:, h * HEAD_DIM:(h + 1) * HEAD_DIM] for h in range(N_KV_HEADS_SWA)]
        return jnp.concatenate([hd for hd in heads for _ in range(2)], axis=1)

    w_main = jnp.concatenate([w[:, :Q_A], dup(w[:, o_ka:o_va]), dup(w[:, o_va:o_qb]),
                              w[:, o_qb:o_f]], axis=1).astype(BF16)
    w_f = jnp.pad(w[:, o_f:o_g], ((0, 0), (0, LANES - N_HEADS_FOX))).astype(BF16)
    w_g = w[:, o_g:].astype(BF16)
    qa, kdup, vdup, qb, kb, vb, f_pad, gates, nrm = _inproj(
        x2, scale_m, shift_m, g_norm_mix[l][None, :], w_main, w_f, w_g)

    b_pad = jnp.pad(b_forget[l], (0, LANES - N_HEADS_FOX))[None, :]
    lanes = np.arange(LANES)
    jmod = jnp.asarray(np.where(lanes < DECAY_LANES * N_HEADS_FOX, lanes % DECAY_LANES, 7)[None, :].astype(np.int32))
    dq, dk, fb = _cum(f_pad, b_pad, jmod)

    bucket, band = _t5_bucket_np()
    onehot = jnp.asarray(bucket[None] == np.arange(NUM_BUCKETS)[:, None, None], dtype=F32)
    bias = jnp.einsum("bh,bqk->hqk", rel_bias_table.astype(F32), onehot, precision=HIGHEST)
    bias = jnp.where(band[None], bias, NEG_INF)
    first = np.arange(2 * BLOCK)[None, None, :] < BLOCK
    bias = jnp.stack([jnp.where(first, NEG_INF, bias), bias]).reshape(2, N_KV_HEADS_SWA, -1, 2 * BLOCK)
    o_a = _swa(sinks[l].astype(F32), qa, kdup, vdup, bias)

    o_b = _fox(_fox_first_tiles(nrm, fb), qb, kb, vb, dq, dk)

    w_r = jnp.concatenate([w_router_group[l]] + [w_router_expert[l][g] for g in range(N_GROUPS)], axis=1)
    w_r = jnp.pad(w_r, ((0, 0), (0, LANES - w_r.shape[1])))
    wr_hi = w_r.astype(BF16)
    wr_lo = (w_r - wr_hi.astype(F32)).astype(BF16)
    b_r = jnp.concatenate([b_router_group[l], b_router_expert[l].reshape(-1)])
    b_r = jnp.pad(b_r, (0, LANES - b_r.shape[0]))[None, :]
    x1, xy, rcol, cu = _post(x2, o_a, o_b, gates, gate_m, scale_f, shift_f, g_norm_ffn[l][None, :],
                             w_proj_swa[l].astype(BF16), w_proj_fox[l].astype(BF16), w_out[l].astype(BF16),
                             wr_hi, wr_lo, b_r)

    i32 = jnp.int32
    n_tok_tiles = N_TOK // TM_POST
    cu = cu.reshape(n_tok_tiles, 8, LANES)[:, 0, :N_EXPERTS].astype(i32)
    loc_u = jnp.cumsum(cu, axis=1) - cu
    cend = jnp.cumsum(cu, axis=0)
    cstart = cend - cu
    tot_u = cend[-1]
    tiles_e = (tot_u + UNITS_PER_TILE - 1) // UNITS_PER_TILE
    tile_end = jnp.cumsum(tiles_e)
    tile_start = tile_end - tiles_e
    r = jnp.arange(N_EXP_TILES, dtype=i32)
    tile_expert = jnp.minimum(jnp.sum((tile_end[None, :] <= r[:, None]).astype(i32), axis=1), N_EXPERTS - 1)
    sel_e = tile_expert[:, None] == jnp.arange(N_EXPERTS, dtype=i32)[None, :]
    tw = r - jnp.sum(jnp.where(sel_e, tile_start[None, :], 0), axis=1)
    n_valid = jnp.clip(jnp.sum(jnp.where(sel_e, tot_u[None, :], 0), axis=1) - tw * UNITS_PER_TILE,
                       0, UNITS_PER_TILE)
    n_valid = jnp.where(r < tile_end[-1], n_valid, 0).astype(i32)
    q = tw[:, None] * UNITS_PER_TILE + jnp.arange(UNITS_PER_TILE, dtype=i32)[None, :]

    def of_expert(tab):
        return jnp.sum(jnp.where(sel_e[:, None, :], tab[None, :, :], 0), axis=2)

    cend_r, cstart_r, loc_r = of_expert(cend), of_expert(cstart), of_expert(loc_u)
    src_tile = jnp.minimum(jnp.sum((cend_r[:, None, :] <= q[:, :, None]).astype(i32), axis=2), n_tok_tiles - 1)
    sel_t = src_tile[:, :, None] == jnp.arange(n_tok_tiles, dtype=i32)[None, None, :]
    k = (q - jnp.sum(jnp.where(sel_t, cstart_r[:, None, :], 0), axis=2)
         + jnp.sum(jnp.where(sel_t, loc_r[:, None, :], 0), axis=2))
    valid = jnp.arange(UNITS_PER_TILE, dtype=i32)[None, :] < n_valid[:, None]
    unit_rows = jnp.where(valid, src_tile * XY_ROWS + k * UNIT, 0).reshape(-1).astype(i32)

    xy = _experts(tile_expert.astype(i32), n_valid, unit_rows, xy,
                  w_gate_exp[l].reshape(N_EXPERTS, D_MODEL, D_FF_EXPERT),
                  w_up_exp[l].reshape(N_EXPERTS, D_MODEL, D_FF_EXPERT),
                  w_down_exp[l].reshape(N_EXPERTS, D_FF_EXPERT, D_MODEL))
    out = _combine(x1, rcol, gate_f, g_final[None, :], xy)
    return out.reshape(BATCH, SEQ, D_MODEL)
```

```python
import functools
import math

import numpy as np
import jax
import jax.numpy as jnp
from jax import lax
from jax.experimental import pallas as pl
from jax.experimental.pallas import tpu as pltpu

F32 = jnp.float32
BF16 = jnp.bfloat16
HIGHEST = lax.Precision.HIGHEST

D_MODEL = 1024
BATCH = 8
SEQ = 4096
N_TOK = BATCH * SEQ
N_HEADS_SWA = 8
N_KV_HEADS_SWA = 2
N_HEADS_FOX = 8
HEAD_DIM = 64
WINDOW = 128
BLOCK = 128
NUM_BUCKETS = 32
MAX_DISTANCE = 128
N_GROUPS = 4
EXPERTS_PER_GROUP = 8
N_EXPERTS = N_GROUPS * EXPERTS_PER_GROUP
D_FF_EXPERT = 256
EPS = 1e-6
NEG_INF = -1e30

Q_A = N_HEADS_SWA * HEAD_DIM
KV_A = N_KV_HEADS_SWA * HEAD_DIM
W_B = N_HEADS_FOX * HEAD_DIM
LANES = 128
QK_SCALE = HEAD_DIM ** -0.5

TM_IN = 512
TM_POST = 512
TQ_FOX = 512
TK_FOX = TQ_FOX
TM_EXP = 512
TM_ROW = 512
UNIT = 16
XY_UNITS = 2 * TM_POST // UNIT + N_EXPERTS
XY_ROWS = XY_UNITS * UNIT
XY_COLS = D_MODEL + LANES
UNITS_PER_TILE = TM_EXP // UNIT
N_TOK_TILES = N_TOK // TM_POST
N_EXP_TILES = N_TOK_TILES * XY_UNITS // UNITS_PER_TILE + N_EXPERTS
PAD_UNITS_PER_EXPERT = UNITS_PER_TILE - 1
PAD_BLOCKS = -(-(N_EXPERTS * PAD_UNITS_PER_EXPERT * UNIT) // XY_ROWS)
PAD_BASE_ROW = N_TOK_TILES * XY_ROWS
W1_LANES, W2_LANES, E1_LANE, E2_LANE = (4, 6), (5, 7), 8, 9
VMEM_LIMIT = 56 * 1024 * 1024

DECAY_LANES = 6
PRUNE_MARGIN = 110.0


def _cparams(n_axes):
    return pltpu.CompilerParams(dimension_semantics=("arbitrary",) * n_axes,
                                vmem_limit_bytes=VMEM_LIMIT)


def _ada_kernel(c_ref, w_ref, b_ref, o_ref):
    c = c_ref[...]
    ca = c * jax.nn.sigmoid(c)
    o_ref[...] = jnp.dot(ca.astype(BF16), w_ref[...].astype(BF16),
                         preferred_element_type=F32) + b_ref[...]


def _ada(c16, w_ada, b_ada):
    n_out = w_ada.shape[1]
    blk = 1024
    return pl.pallas_call(
        _ada_kernel,
        out_shape=jax.ShapeDtypeStruct((16, n_out), F32),
        grid=(n_out // blk,),
        in_specs=[pl.BlockSpec((16, D_MODEL), lambda j: (0, 0)),
                  pl.BlockSpec((D_MODEL, blk), lambda j: (0, j)),
                  pl.BlockSpec((1, blk), lambda j: (0, j))],
        out_specs=pl.BlockSpec((16, blk), lambda j: (0, j)),
        compiler_params=_cparams(1),
        name="ada",
    )(c16, w_ada, b_ada)


def _inproj_kernel(x_ref, sc_ref, sh_ref, g_ref, wm_ref, wf_ref, wg_ref, ind_ref,
                   qa_ref, kd_ref, vd_ref, qb_ref, kb_ref, vb_ref, f_ref, gt_ref, nrm_ref):
    x = x_ref[...]
    rs = lax.rsqrt(jnp.mean(x * x, axis=-1, keepdims=True) + EPS)
    a = g_ref[...] * (1.0 + sc_ref[...])
    h = (x * rs * a + sh_ref[...]).astype(BF16)

    def mm(w):
        return jnp.dot(h, w, preferred_element_type=F32)

    qa_ref[...] = (mm(wm_ref[:, 0:512]) * QK_SCALE).astype(BF16)
    kd_ref[...] = mm(wm_ref[:, 512:768]).astype(BF16)
    vd_ref[...] = mm(wm_ref[:, 768:1024]).astype(BF16)
    qb = (mm(wm_ref[:, 1024:1536]) * QK_SCALE).astype(BF16)
    kb = mm(wm_ref[:, 1536:2048]).astype(BF16)
    qb_ref[...] = qb
    kb_ref[...] = kb
    vb_ref[...] = mm(wm_ref[:, 2048:2560]).astype(BF16)
    sq = jnp.concatenate([qb, kb], axis=1).astype(F32)
    seg = jnp.dot((sq * sq).astype(BF16), ind_ref[...], preferred_element_type=F32)
    nrm_ref[...] = jnp.broadcast_to(jnp.max(seg, axis=0, keepdims=True), nrm_ref.shape)
    f_ref[...] = mm(wf_ref[...])
    gt_ref[...] = mm(wg_ref[...]).astype(BF16)


def _inproj(x2, scale_m, shift_m, g_mix, w_main, w_f, w_g):
    tm = TM_IN
    tpb = SEQ // tm
    row = lambda i: (i, 0)
    per_b = lambda i: (i // tpb, 0, 0)
    const = lambda i: (0, 0)
    outs = [(Q_A, BF16), (2 * KV_A, BF16), (2 * KV_A, BF16), (W_B, BF16), (W_B, BF16), (W_B, BF16),
            (LANES, F32), (2 * D_MODEL, BF16)]
    ind_np = np.zeros((2 * W_B, LANES), np.float32)
    ind_np[np.arange(2 * W_B), np.arange(2 * W_B) // HEAD_DIM] = 1.0
    ind = jnp.asarray(ind_np, dtype=BF16)
    n_steps = N_TOK // tm
    return pl.pallas_call(
        _inproj_kernel,
        out_shape=[jax.ShapeDtypeStruct((N_TOK, w), dt) for w, dt in outs]
        + [jax.ShapeDtypeStruct((n_steps * 8, LANES), F32)],
        grid=(n_steps,),
        in_specs=[pl.BlockSpec((tm, D_MODEL), row),
                  pl.BlockSpec((None, 1, D_MODEL), per_b),
                  pl.BlockSpec((None, 1, D_MODEL), per_b),
                  pl.BlockSpec((1, D_MODEL), const),
                  pl.BlockSpec(w_main.shape, const),
                  pl.BlockSpec(w_f.shape, const),
                  pl.BlockSpec(w_g.shape, const),
                  pl.BlockSpec(ind.shape, const)],
        out_specs=[pl.BlockSpec((tm, w), row) for w, _ in outs] + [pl.BlockSpec((8, LANES), row)],
        compiler_params=_cparams(1),
        name="inproj",
    )(x2, scale_m, shift_m, g_mix, w_main, w_f, w_g, ind)


def _log_sigmoid(x):
    return jnp.minimum(x, 0.0) - jnp.log1p(jnp.exp(-jnp.abs(x)))


def _cum_kernel(f_ref, b_ref, jm_ref, qa_ref, ka_ref, fb_ref):
    r = lax.broadcasted_iota(jnp.int32, (LANES, LANES), 0)
    c = lax.broadcasted_iota(jnp.int32, (LANES, LANES), 1)
    expand = ((c >= DECAY_LANES * r) & (c < DECAY_LANES * r + DECAY_LANES)
              & (r < N_HEADS_FOX)).astype(F32)
    tril = (r >= c).astype(F32)
    lf = _log_sigmoid(f_ref[...] + b_ref[...])
    lfe = jnp.dot(lf, expand, precision=HIGHEST, preferred_element_type=F32)
    jm = jm_ref[...]
    carry = jnp.zeros((1, LANES), F32)
    for blk in range(SEQ // LANES):
        rows = slice(blk * LANES, (blk + 1) * LANES)
        cb = jnp.dot(tril, lfe[rows], precision=HIGHEST, preferred_element_type=F32) + carry
        carry = cb[LANES - 1:LANES]
        hi = cb.astype(BF16).astype(F32)
        r1 = cb - hi
        mid = r1.astype(BF16).astype(F32)
        lo = (r1 - mid).astype(BF16).astype(F32)
        one = jnp.ones_like(cb)
        zero = jnp.zeros_like(cb)
        qa = jnp.where(jm == 0, hi, jnp.where(jm == 1, mid, jnp.where(jm == 2, lo,
                       jnp.where(jm < DECAY_LANES, one, zero))))
        ka = jnp.where(jm == 3, -hi, jnp.where(jm == 4, -mid, jnp.where(jm == 5, -lo,
                       jnp.where(jm < 3, one, zero))))
        qa_ref[rows, :] = qa.astype(BF16)
        ka_ref[rows, :] = ka.astype(BF16)
        blocks_per_tile = TQ_FOX // LANES
        tile = blk // blocks_per_tile
        if blk % blocks_per_tile == 0:
            fb_ref[2 * tile:2 * tile + 1, :] = cb[0:1]
        if blk % blocks_per_tile == blocks_per_tile - 1:
            fb_ref[2 * tile + 1:2 * tile + 2, :] = carry


def _cum(f_pad, b_pad, jmod):
    n_tiles = SEQ // TQ_FOX
    return pl.pallas_call(
        _cum_kernel,
        out_shape=[jax.ShapeDtypeStruct((BATCH, SEQ, LANES), BF16)] * 2
        + [jax.ShapeDtypeStruct((BATCH, 2 * n_tiles, LANES), F32)],
        grid=(BATCH,),
        in_specs=[pl.BlockSpec((SEQ, LANES), lambda b: (b, 0)),
                  pl.BlockSpec((1, LANES), lambda b: (0, 0)),
                  pl.BlockSpec((1, LANES), lambda b: (0, 0))],
        out_specs=[pl.BlockSpec((None, SEQ, LANES), lambda b: (b, 0, 0))] * 2
        + [pl.BlockSpec((None, 2 * n_tiles, LANES), lambda b: (b, 0, 0))],
        compiler_params=_cparams(1),
        name="cum",
    )(f_pad, b_pad, jmod)


def _swa_kernel(sink_ref, q_ref, kc_ref, kp_ref, vc_ref, vp_ref, bias_ref, o_ref):
    lane = lax.broadcasted_iota(jnp.int32, (BLOCK, LANES), 1)
    lo = lane < HEAD_DIM
    kk =jnp.concatenate([kp_ref[...], kc_ref[...]], axis=0)
    vv = jnp.concatenate([vp_ref[...], vc_ref[...]], axis=0)
    grp = N_HEADS_SWA // N_KV_HEADS_SWA
    for g in range(N_KV_HEADS_SWA):
        parts = []
        for t in range(2):
            qt = q_ref[:, (2 * g + t) * LANES:(2 * g + t + 1) * LANES]
            zero = jnp.zeros_like(qt)
            parts.append(jnp.where(lo, qt, zero))
            parts.append(jnp.where(lo, zero, qt))
        q4 = jnp.concatenate(parts, axis=0)
        s = lax.dot_general(q4, kk[:, g * LANES:(g + 1) * LANES], (((1,), (1,)), ((), ())),
                            preferred_element_type=F32)
        s = s + bias_ref[g]
        outs = []
        for hh in range(grp):
            sink = sink_ref[g * grp + hh]
            sl = s[hh * BLOCK:(hh + 1) * BLOCK]
            m =jnp.maximum(jnp.max(sl, axis=-1, keepdims=True), sink)
            p = jnp.exp(sl - m)
            den = jnp.sum(p, axis=-1, keepdims=True) + jnp.exp(sink - m)
            o = jnp.dot(p.astype(BF16), vv[:, g * LANES:(g + 1) * LANES],
                        preferred_element_type=F32)
            outs.append(o / den)
        o_ref[:, (2 * g) * LANES:(2 * g + 1) * LANES] = jnp.where(lo, outs[0], outs[1]).astype(BF16)
        o_ref[:, (2 * g + 1) * LANES:(2 * g + 2) * LANES] = jnp.where(lo, outs[2], outs[3]).astype(BF16)


def _swa(sinks, qa, kdup, vdup, bias):
    nb = SEQ // BLOCK
    cur = lambda b, i, s: (b * nb + i, 0)
    prev = lambda b, i, s: (b * nb + jnp.maximum(i - 1, 0), 0)
    grid_spec = pltpu.PrefetchScalarGridSpec(
        num_scalar_prefetch=1,
        grid=(BATCH, nb),
        in_specs=[pl.BlockSpec((BLOCK, Q_A), cur),
                  pl.BlockSpec((BLOCK, 2 * KV_A), cur),
                  pl.BlockSpec((BLOCK, 2 * KV_A), prev),
                  pl.BlockSpec((BLOCK, 2 * KV_A), cur),
                  pl.BlockSpec((BLOCK, 2 * KV_A), prev),
                  pl.BlockSpec((None,) + bias.shape[1:], lambda b, i, s: (jnp.minimum(i, 1), 0, 0, 0))],
        out_specs=pl.BlockSpec((BLOCK, Q_A), cur))
    return pl.pallas_call(
        _swa_kernel,
        out_shape=jax.ShapeDtypeStruct((N_TOK, Q_A), BF16),
        grid_spec=grid_spec,
        compiler_params=_cparams(2),
        name="swa",
    )(sinks, qa, kdup, kdup, vdup, vdup, bias)


def _fox_kernel(js_ref, q_ref, k_ref, v_ref, qa_ref, ka_ref, o_ref, kaug, vaug, q2, m_sc, acc_sc):
    tq, tk = TQ_FOX, TK_FOX
    b = pl.program_id(0)
    t = pl.program_id(1)
    i = pl.program_id(2)
    j_start = js_ref[(b * pl.num_programs(1) + t) * pl.num_programs(2) + i]

    @pl.when(i == 0)
    def _():
        kaug[:, 0:LANES] = k_ref[...]
        kaug[:, LANES:2 * LANES] = ka_ref[...]
        vaug[:, 0:LANES] = v_ref[...]
        vaug[:, LANES:2 * LANES] = jnp.ones((SEQ, LANES), BF16)

    lane = lax.broadcasted_iota(jnp.int32, (tq, LANES), 1)
    lo = lane < HEAD_DIM
    base = 2 * DECAY_LANES * t
    own = [(lane >= base + h * DECAY_LANES) & (lane < base + (h + 1) * DECAY_LANES) for h in range(2)]
    q = q_ref[...]
    qa = qa_ref[...]
    zero = jnp.zeros_like(q)
    q2[0, :, 0:LANES] = jnp.where(lo, q, zero)
    q2[1, :, 0:LANES] = jnp.where(lo, zero, q)
    for h in range(2):
        q2[h, :, LANES:2 * LANES] = jnp.where(own[h], qa, zero)
    m_sc[...] = jnp.full(m_sc.shape, NEG_INF, F32)
    acc_sc[...] = jnp.zeros(acc_sc.shape, F32)

    def scores(h, ks):
        return lax.dot_general(q2[h], kaug[pl.ds(ks, tk), :], (((1,), (1,)), ((), ())),
                               preferred_element_type=F32)

    def consume(h, s, ks, mask):
        if mask is not None:
            s = jnp.where(mask, s, NEG_INF)
        m_prev = m_sc[h]
        m_new = jnp.maximum(m_prev, jnp.max(s, axis=-1, keepdims=True))
        alpha = jnp.exp(m_prev - m_new)
        p = jnp.exp(s - jnp.concatenate([m_new] * (tk // LANES), axis=1))
        pv = jnp.dot(p.astype(BF16), vaug[pl.ds(ks, tk), :], preferred_element_type=F32)
        acc_sc[h] = jnp.concatenate([alpha, alpha], axis=1) * acc_sc[h] + pv
        m_sc[h] = m_new

    def both_chains(ks, mask):
        ss = [scores(h, ks) for h in range(2)]
        for h in range(2):
            consume(h, ss[h], ks, mask)

    def step(j, carry):
        both_chains(pl.multiple_of(j * tk, tk), None)
        return carry

    lax.fori_loop(j_start, i, step, 0)

    rr = lax.broadcasted_iota(jnp.int32, (tq, tk), 0)
    cc = lax.broadcasted_iota(jnp.int32, (tq, tk), 1)
    both_chains(pl.multiple_of(i * tk, tk), cc <= rr)

    outs = [acc_sc[h, :, 0:LANES] / acc_sc[h, :, LANES:2 * LANES] for h in range(2)]
    o_ref[...] = jnp.where(lo, outs[0], outs[1]).astype(BF16)


def _fox(j_start, qb, kb, vb, qa, ka):
    tq = TQ_FOX
    nq = SEQ // tq
    n_pairs = N_HEADS_FOX // 2
    qmap = lambda b, t, i, js: (b * nq + i, t)
    kmap = lambda b, t, i, js: (b, t)
    grid_spec = pltpu.PrefetchScalarGridSpec(
        num_scalar_prefetch=1,
        grid=(BATCH, n_pairs, nq),
        in_specs=[pl.BlockSpec((tq, LANES), qmap),
                  pl.BlockSpec((SEQ, LANES), kmap),
                  pl.BlockSpec((SEQ, LANES), kmap),
                  pl.BlockSpec((None, tq, LANES), lambda b, t, i, js: (b, i, 0)),
                  pl.BlockSpec((None, SEQ, LANES), lambda b, t, i, js: (b, 0, 0))],
        out_specs=pl.BlockSpec((tq, LANES), qmap),
        scratch_shapes=[pltpu.VMEM((SEQ, 2 * LANES), BF16),
                        pltpu.VMEM((SEQ, 2 * LANES), BF16),
                        pltpu.VMEM((2, tq, 2 * LANES), BF16),
                        pltpu.VMEM((2, tq, LANES), F32),
                        pltpu.VMEM((2, tq, 2 * LANES), F32)])
    return pl.pallas_call(
        _fox_kernel,
        out_shape=jax.ShapeDtypeStruct((N_TOK, W_B), BF16),
        grid_spec=grid_spec,
        compiler_params=_cparams(3),
        name="fox",
    )(j_start, qb, kb, vb, qa, ka)


def _fox_first_tiles(nrm, fb):
    n_tiles = SEQ // TQ_FOX
    nr = nrm.reshape(BATCH, n_tiles, 8, LANES)[:, :, 0, :] * 1.02
    qn = jnp.sqrt(nr[..., 0:N_HEADS_FOX])
    kn = jnp.sqrt(nr[..., N_HEADS_FOX:2 * N_HEADS_FOX])
    f_first = fb[:, 0::2, 0:DECAY_LANES * N_HEADS_FOX:DECAY_LANES]
    f_last = fb[:, 1::2, 0:DECAY_LANES * N_HEADS_FOX:DECAY_LANES]
    kn_prefix = lax.cummax(kn, axis=1)
    upper = qn[:, :, None, :] * kn_prefix[:, None, :, :] + f_first[:, :, None, :] - f_last[:, None, :, :]
    row_max_low = -(qn * kn)[:, :, None, :]
    ii = jnp.arange(n_tiles)[None, :, None, None]
    jj = jnp.arange(n_tiles)[None, None, :, None]
    skip = (upper < row_max_low - PRUNE_MARGIN) & (jj < ii)
    skip = jnp.all(skip.reshape(BATCH, n_tiles, n_tiles, N_HEADS_FOX // 2, 2), axis=-1)
    first = jnp.sum(jnp.cumprod(skip.astype(jnp.int32), axis=2), axis=2)
    return jnp.transpose(first, (0, 2, 1)).reshape(-1).astype(jnp.int32)


def _post_kernel(*refs):
    xy_ref = refs[-3]
    step = pl.program_id(0)

    @pl.when(step < N_TOK_TILES)
    def _():
        _post_body(*refs)

    @pl.when(step >= N_TOK_TILES)
    def _():
        xy_ref[...] = jnp.zeros(xy_ref.shape, BF16)


def _post_body(x_ref, oa_ref, ob_ref, gt_ref, gm_ref, sc_ref, sh_ref, g_ref,
               wa_ref, wb_ref, wo_ref, wr2_ref, br_ref,
               x1_ref, xy_ref, rc_ref, cu_ref):
    tm = TM_POST
    pa = jnp.dot(oa_ref[...], wa_ref[...], preferred_element_type=F32)
    pb = jnp.dot(ob_ref[...], wb_ref[...], preferred_element_type=F32)
    ga = jax.nn.sigmoid(gt_ref[:, 0:D_MODEL].astype(F32))
    gb = jax.nn.sigmoid(gt_ref[:, D_MODEL:2 * D_MODEL].astype(F32))
    merged = (ga * pa + gb * pb).astype(BF16)
    y = jnp.dot(merged, wo_ref[...], preferred_element_type=F32)
    x1 = x_ref[...] + gm_ref[...] * y
    x1_ref[...] = x1

    rs = lax.rsqrt(jnp.mean(x1 * x1, axis=-1, keepdims=True) + EPS)
    a = g_ref[...] * (1.0 + sc_ref[...])
    h2 = x1 * rs * a + sh_ref[...]

    hh = h2.astype(BF16)
    hl = (h2 - hh.astype(F32)).astype(BF16)
    hi_both = jnp.dot(hh, wr2_ref[...], preferred_element_type=F32)
    logits = (hi_both[:, 0:LANES] + hi_both[:, LANES:2 * LANES]
              + jnp.dot(hl, wr2_ref[:, 0:LANES], preferred_element_type=F32)
              + br_ref[...])

    lane = lax.broadcasted_iota(jnp.int32, (tm, LANES), 1).astype(F32)
    big = float(LANES)
    gl = jnp.where(lane < N_GROUPS, logits, -jnp.inf)
    gmax = jnp.max(gl, axis=-1, keepdims=True)
    gi = jnp.min(jnp.where(gl == gmax, lane, big), axis=-1, keepdims=True)
    gsum = jnp.sum(jnp.exp(gl - gmax), axis=-1, keepdims=True)
    gp = 1.0 / gsum
    e_lo = N_GROUPS + EXPERTS_PER_GROUP * gi
    el = jnp.where((lane >= e_lo) & (lane < e_lo + EXPERTS_PER_GROUP), logits, -jnp.inf)
    v1 = jnp.max(el, axis=-1, keepdims=True)
    i1 = jnp.min(jnp.where(el == v1, lane, big), axis=-1, keepdims=True)
    el2 = jnp.where(lane == i1, -jnp.inf, el)
    v2 = jnp.max(el2, axis=-1, keepdims=True)
    i2 = jnp.min(jnp.where(el2 == v2, lane, big), axis=-1, keepdims=True)
    e21 = jnp.exp(v2 - v1)
    w1 = gp / (1.0 + e21)
    w2 = gp * e21 / (1.0 + e21)
    e1 = i1 - N_GROUPS
    e2 = i2 - N_GROUPS

    oh = jnp.where((lane == e1) | (lane == e2), 1.0, 0.0)
    cnt_u = jnp.floor((jnp.sum(oh, axis=0, keepdims=True) + (UNIT - 1)) * (1.0 / UNIT))
    r128 = lax.broadcasted_iota(jnp.int32, (LANES, LANES), 0)
    c128 = lax.broadcasted_iota(jnp.int32, (LANES, LANES), 1)
    before_lane = jnp.where(r128 < c128, 1.0, 0.0).astype(BF16)
    loc_u = jnp.dot(jnp.broadcast_to(cnt_u, (8, LANES)).astype(BF16), before_lane,
                    preferred_element_type=F32)
    rr = lax.broadcasted_iota(jnp.int32, (tm, tm), 0)
    cc = lax.broadcasted_iota(jnp.int32, (tm, tm), 1)
    strict = jnp.where(rr > cc, 1.0, 0.0).astype(BF16)
    pos_e = jnp.dot(strict, oh.astype(BF16), preferred_element_type=F32) + loc_u[0:1] * UNIT
    lp1 = jnp.sum(jnp.where(lane == e1, pos_e, 0.0), axis=-1, keepdims=True)
    lp2 = jnp.sum(jnp.where(lane == e2, pos_e, 0.0), axis=-1, keepdims=True)

    eye = rr == cc
    ones8 = jnp.ones((8, tm), BF16)

    def to_row(col):
        hi = jnp.floor(col * (1.0 / 64.0))
        lo_ = col - 64.0 * hi
        d_hi = jnp.where(eye, hi, 0.0).astype(BF16)
        d_lo = jnp.where(eye, lo_, 0.0).astype(BF16)
        row8 = (64.0 * jnp.dot(ones8, d_hi, preferred_element_type=F32)
                + jnp.dot(ones8, d_lo, preferred_element_type=F32))
        return row8[0:1]

    srow = lax.broadcasted_iota(jnp.int32, (XY_ROWS, tm), 0).astype(F32)
    pm1 = jnp.where(srow == to_row(lp1), 1.0, 0.0).astype(BF16)
    pm2 = jnp.where(srow == to_row(lp2), 1.0, 0.0).astype(BF16)
    w1h = w1.astype(BF16).astype(F32)
    w2h = w2.astype(BF16).astype(F32)
    side = jnp.where(lane == W1_LANES[0], w1h, jnp.where(lane == W1_LANES[1], w1 - w1h,
           jnp.where(lane == W2_LANES[0], w2h, jnp.where(lane == W2_LANES[1], w2 - w2h,
           jnp.where(lane == E1_LANE, e1, jnp.where(lane == E2_LANE, e2, 0.0))))))
    tok = jnp.concatenate([hh, side.astype(BF16)], axis=1)
    xy_ref[...] = jnp.dot(pm1 + pm2, tok, preferred_element_type=F32).astype(BF16)

    cu_ref[...] = jnp.broadcast_to(cnt_u, cu_ref.shape)
    rc_ref[...] = jnp.where(lane == 0, lp1, jnp.where(lane == 1, lp2, 0.0))


def _post(x2, oa, ob, gates, gate_m, scale_f, shift_f, g_ffn, wa, wb, wo, wr2, b_r):
    tm = TM_POST
    tpb = SEQ // tm
    n_steps = N_TOK_TILES
    row = lambda i: (jnp.minimum(i, n_steps - 1), 0)
    per_b = lambda i: (jnp.minimum(i, n_steps - 1) // tpb, 0, 0)
    const = lambda i: (0, 0)
    return pl.pallas_call(
        _post_kernel,
        out_shape=[jax.ShapeDtypeStruct((N_TOK, D_MODEL), F32),
                   jax.ShapeDtypeStruct(((n_steps + PAD_BLOCKS) * XY_ROWS, XY_COLS), BF16),
                   jax.ShapeDtypeStruct((N_TOK, LANES), F32),
                   jax.ShapeDtypeStruct((n_steps * 8, LANES), F32)],
        grid=(n_steps + PAD_BLOCKS,),
        in_specs=[pl.BlockSpec((tm, D_MODEL), row),
                  pl.BlockSpec((tm, Q_A), row),
                  pl.BlockSpec((tm, W_B), row),
                  pl.BlockSpec((tm, 2 * D_MODEL), row),
                  pl.BlockSpec((None, 1, D_MODEL), per_b),
                  pl.BlockSpec((None, 1, D_MODEL), per_b),
                  pl.BlockSpec((None, 1, D_MODEL), per_b),
                  pl.BlockSpec((1, D_MODEL), const),
                  pl.BlockSpec(wa.shape, const),
                  pl.BlockSpec(wb.shape, const),
                  pl.BlockSpec(wo.shape, const),
                  pl.BlockSpec(wr2.shape, const),
                  pl.BlockSpec((1, LANES), const)],
        out_specs=[pl.BlockSpec((tm, D_MODEL), row),
                   pl.BlockSpec((XY_ROWS, XY_COLS), lambda i: (i, 0)),
                   pl.BlockSpec((tm, LANES), row),
                   pl.BlockSpec((8, LANES), row)],
        compiler_params=_cparams(1),
        name="post",
    )(x2, oa, ob, gates, gate_m, scale_f, shift_f, g_ffn, wa, wb, wo, wr2, b_r)


def _experts_kernel(te_ref, nu_ref, ur_ref, xy_in, wg_ref, wu_ref, wd_ref, xy_out,
                    xbuf, ybuf, gsem, ssem):
    del xy_in
    r = pl.program_id(0)
    last = pl.num_programs(0) - 1
    n_used = nu_ref[0]
    slot = lax.rem(r, 2)

    def unit_row(step, s):
        return pl.multiple_of(ur_ref[step * UNITS_PER_TILE + s], UNIT)

    def start_gathers(step, sl):
        for s in range(UNITS_PER_TILE):
            pltpu.make_async_copy(xy_out.at[pl.ds(unit_row(step, s), UNIT), :],
                                  xbuf.at[sl, pl.ds(s * UNIT, UNIT), :], gsem.at[sl]).start()

    def wait_gathers(sl):
        pltpu.make_async_copy(xy_out.at[pl.ds(0, TM_EXP), :], xbuf.at[sl], gsem.at[sl]).wait()

    def start_scatters(step, sl):
        for s in range(UNITS_PER_TILE):
            pltpu.make_async_copy(ybuf.at[sl, pl.ds(s * UNIT, UNIT), :],
                                  xy_out.at[pl.ds(unit_row(step, s), UNIT), pl.ds(0, D_MODEL)],
                                  ssem.at[sl]).start()

    def wait_scatters(sl):
        pltpu.make_async_copy(ybuf.at[sl], xy_out.at[pl.ds(0, TM_EXP), pl.ds(0, D_MODEL)], ssem.at[sl]).wait()

    @pl.when(r == 0)
    def _():
        start_gathers(0, 0)

    @pl.when(jnp.logical_and(r < n_used, r >= 2))
    def _():
        wait_scatters(slot)

    @pl.when(r < n_used)
    def _():
        wait_gathers(slot)
        start_gathers(jnp.minimum(r + 1, last), 1 - slot)
        x = xbuf[slot, :, 0:D_MODEL]
        side = xbuf[slot, :, D_MODEL:XY_COLS].astype(F32)
        lane = lax.broadcasted_iota(jnp.int32, side.shape, 1)

        def lanes_sum(a, b):
            return jnp.sum(jnp.where((lane == a) | (lane == b), side, 0.0), axis=-1, keepdims=True)

        is_slot1 = lanes_sum(E1_LANE, E1_LANE) == te_ref[r].astype(F32)
        wrow = jnp.where(is_slot1, lanes_sum(*W1_LANES), lanes_sum(*W2_LANES))
        a = jnp.dot(x, wg_ref[...].astype(BF16), preferred_element_type=F32)
        u = jnp.dot(x, wu_ref[...].astype(BF16), preferred_element_type=F32)
        hid = (a * jax.nn.sigmoid(a) * u * wrow).astype(BF16)
        ybuf[slot] = jnp.dot(hid, wd_ref[...].astype(BF16), preferred_element_type=F32).astype(BF16)
        start_scatters(r, slot)

    @pl.when(r == n_used - 1)
    def _():
        wait_gathers(1 - slot)
        wait_scatters(slot)

        @pl.when(r >= 1)
        def _():
            wait_scatters(1 - slot)


def _experts(tile_expert, n_used, unit_rows, xy, wg, wu, wd):
    wmap = lambda r, te, nv, ur: (te[r], 0, 0)
    grid_spec = pltpu.PrefetchScalarGridSpec(
        num_scalar_prefetch=3,
        grid=(N_EXP_TILES,),
        in_specs=[pl.BlockSpec(memory_space=pl.ANY),
                  pl.BlockSpec((None, D_MODEL, D_FF_EXPERT), wmap),
                  pl.BlockSpec((None, D_MODEL, D_FF_EXPERT), wmap),
                  pl.BlockSpec((None, D_FF_EXPERT, D_MODEL), wmap)],
        out_specs=pl.BlockSpec(memory_space=pl.ANY),
        scratch_shapes=[pltpu.VMEM((2, TM_EXP, XY_COLS), BF16),
                        pltpu.VMEM((2, TM_EXP, D_MODEL), BF16),
                        pltpu.SemaphoreType.DMA((2,)),
                        pltpu.SemaphoreType.DMA((2,))])
    return pl.pallas_call(
        _experts_kernel,
        out_shape=jax.ShapeDtypeStruct(xy.shape, xy.dtype),
        grid_spec=grid_spec,
        input_output_aliases={3: 0},
        compiler_params=_cparams(1),
        name="experts",
    )(tile_expert, n_used, unit_rows, xy, wg, wu, wd)


def _combine_kernel(x1_ref, rc_ref, gf_ref, gfin_ref, y_ref, o_ref):
    lp1 = rc_ref[:, 0:1]
    lp2 = rc_ref[:, 1:2]
    scol = lax.broadcasted_iota(jnp.int32, (TM_ROW, XY_ROWS), 1).astype(F32)
    pick = jnp.where((scol == lp1) | (scol == lp2), 1.0, 0.0).astype(BF16)
    y = jnp.dot(pick, y_ref[...], preferred_element_type=F32)
    xf = x1_ref[...] + gf_ref[...] * y
    rs = lax.rsqrt(jnp.mean(xf * xf, axis=-1, keepdims=True) + EPS)
    o_ref[...] = xf * rs * gfin_ref[...]


def _combine(x1, rcol, gate_f, g_final, xy):
    tm = TM_ROW
    tpb = SEQ // tm
    row = lambda i: (i, 0)
    return pl.pallas_call(
        _combine_kernel,
        out_shape=jax.ShapeDtypeStruct((N_TOK, D_MODEL), F32),
        grid=(N_TOK // tm,),
        in_specs=[pl.BlockSpec((tm, D_MODEL), row),
                  pl.BlockSpec((tm, LANES), row),
                  pl.BlockSpec((None, 1, D_MODEL), lambda i: (i // tpb, 0, 0)),
                  pl.BlockSpec((1, D_MODEL), lambda i: (0, 0)),
                  pl.BlockSpec((XY_ROWS, D_MODEL), row)],
        out_specs=pl.BlockSpec((tm, D_MODEL), row),
        compiler_params=_cparams(1),
        name="combine",
    )(x1, rcol, gate_f, g_final, xy)


def _t5_bucket_np():
    qi = np.arange(BLOCK)[:, None]
    kj = np.arange(2 * BLOCK)[None, :]
    dist = qi - kj + BLOCK
    n = np.maximum(dist, 0)
    max_exact = NUM_BUCKETS // 2
    nf = np.maximum(n, 1).astype(np.float32)
    large = max_exact + (np.log(nf / np.float32(max_exact)) / np.float32(math.log(MAX_DISTANCE / max_exact))
                         * np.float32(NUM_BUCKETS - max_exact)).astype(np.int32)
    large = np.minimum(large, NUM_BUCKETS - 1)
    bucket = np.where(n < max_exact, n, large)
    band = (dist >= 0) & (dist < WINDOW)
    return bucket.astype(np.int32), band


def kernel(x, c, w_ada, b_ada, g_norm_mix, g_norm_ffn, w_in, sinks, b_forget, w_proj_swa, w_proj_fox,
           w_out, rel_bias_table, w_router_group, b_router_group, w_router_expert, b_router_expert,
           w_gate_exp, w_up_exp, w_down_exp, g_final):
    l = 0
    x2 = x.reshape(N_TOK, D_MODEL)

    c16 = jnp.concatenate([c, jnp.zeros_like(c)], axis=0)
    mod = _ada(c16, w_ada[l], b_ada[l][None, :])[:BATCH]
    shift_m, scale_m, gate_m, shift_f, scale_f, gate_f = [
        m.reshape(BATCH, 1, D_MODEL) for m in jnp.split(mod, 6, axis=-1)]

    w = w_in[l]
    o_ka, o_va, o_qb = Q_A, Q_A + KV_A, Q_A + 2 * KV_A
    o_kb, o_vb, o_f = o_qb + W_B, o_qb + 2 * W_B, o_qb + 3 * W_B
    o_g = o_f + N_HEADS_FOX

    def dup(cols):
        heads = [cols[:, h * HEAD_DIM:(h + 1) * HEAD_DIM] for h in range(N_KV_HEADS_SWA)]
        return jnp.concatenate([hd for hd in heads for _ in range(2)], axis=1)

    w_main = jnp.concatenate([w[:, :Q_A], dup(w[:, o_ka:o_va]), dup(w[:, o_va:o_qb]),
                              w[:, o_qb:o_f]], axis=1).astype(BF16)
    w_f = jnp.pad(w[:, o_f:o_g], ((0, 0), (0, LANES - N_HEADS_FOX))).astype(BF16)
    w_g = w[:, o_g:].astype(BF16)
    qa, kdup, vdup, qb, kb, vb, f_pad, gates, nrm = _inproj(
        x2, scale_m, shift_m, g_norm_mix[l][None, :], w_main, w_f, w_g)

    b_pad = jnp.pad(b_forget[l], (0, LANES - N_HEADS_FOX))[None, :]
    lanes = np.arange(LANES)
    jmod = jnp.asarray(np.where(lanes < DECAY_LANES * N_HEADS_FOX, lanes % DECAY_LANES, 7)[None, :].astype(np.int32))
    dq, dk, fb = _cum(f_pad, b_pad, jmod)

    bucket, band = _t5_bucket_np()
    onehot = jnp.asarray(bucket[None] == np.arange(NUM_BUCKETS)[:, None, None], dtype=F32)
    bias = jnp.einsum("bh,bqk->hqk", rel_bias_table.astype(F32), onehot, precision=HIGHEST)
    bias = jnp.where(band[None], bias, NEG_INF)
    first = np.arange(2 * BLOCK)[None, None, :] < BLOCK
    bias = jnp.stack([jnp.where(first, NEG_INF, bias), bias]).reshape(2, N_KV_HEADS_SWA, -1, 2 * BLOCK)
    o_a = _swa(sinks[l].astype(F32), qa, kdup, vdup, bias)

    o_b = _fox(_fox_first_tiles(nrm, fb), qb, kb, vb, dq, dk)

    w_r = jnp.concatenate([w_router_group[l]] + [w_router_expert[l][g] for g in range(N_GROUPS)], axis=1)
    w_r = jnp.pad(w_r, ((0, 0), (0, LANES - w_r.shape[1])))
    wr_hi = w_r.astype(BF16)
    wr_lo = (w_r - wr_hi.astype(F32)).astype(BF16)
    wr2 = jnp.concatenate([wr_hi, wr_lo], axis=1)
    b_r = jnp.concatenate([b_router_group[l], b_router_expert[l].reshape(-1)])
    b_r = jnp.pad(b_r, (0, LANES - b_r.shape[0]))[None, :]
    x1, xy, rcol, cu = _post(x2, o_a, o_b, gates, gate_m, scale_f, shift_f, g_norm_ffn[l][None, :],
                             w_proj_swa[l].astype(BF16), w_proj_fox[l].astype(BF16), w_out[l].astype(BF16),
                             wr2, b_r)

    i32 = jnp.int32
    n_tok_tiles = N_TOK // TM_POST
    cu = cu.reshape(n_tok_tiles, 8, LANES)[:, 0, :N_EXPERTS].astype(i32)
    loc_u = jnp.cumsum(cu, axis=1) - cu
    cend = jnp.cumsum(cu, axis=0)
    cstart = cend - cu
    tot_u = cend[-1]
    tiles_e = (tot_u + UNITS_PER_TILE - 1) // UNITS_PER_TILE
    tile_end = jnp.cumsum(tiles_e)
    tile_start = tile_end - tiles_e
    r = jnp.arange(N_EXP_TILES, dtype=i32)
    tile_expert = jnp.minimum(jnp.sum((tile_end[None, :] <= r[:, None]).astype(i32), axis=1), N_EXPERTS - 1)
    sel_e = tile_expert[:, None] == jnp.arange(N_EXPERTS, dtype=i32)[None, :]
    tw = r - jnp.sum(jnp.where(sel_e, tile_start[None, :], 0), axis=1)
    tot_r = jnp.sum(jnp.where(sel_e, tot_u[None, :], 0), axis=1)
    n_used = tile_end[-1:].astype(i32)
    q = tw[:, None] * UNITS_PER_TILE + jnp.arange(UNITS_PER_TILE, dtype=i32)[None, :]

    def of_expert(tab):
        return jnp.sum(jnp.where(sel_e[:, None, :], tab[None, :, :], 0), axis=2)

    cend_r, cstart_r, loc_r = of_expert(cend), of_expert(cstart), of_expert(loc_u)
    src_tile = jnp.minimum(jnp.sum((cend_r[:, None, :] <= q[:, :, None]).astype(i32), axis=2), n_tok_tiles - 1)
    sel_t = src_tile[:, :, None] == jnp.arange(n_tok_tiles, dtype=i32)[None, None, :]
    k = (q - jnp.sum(jnp.where(sel_t, cstart_r[:, None, :], 0), axis=2)
         + jnp.sum(jnp.where(sel_t, loc_r[:, None, :], 0), axis=2))
    real_rows = src_tile * XY_ROWS + k * UNIT
    pad_rows = PAD_BASE_ROW + (tile_expert[:, None] * PAD_UNITS_PER_EXPERT + (q - tot_r[:, None])) * UNIT
    idle_row = PAD_BASE_ROW + N_EXPERTS * PAD_UNITS_PER_EXPERT * UNIT
    unit_rows = jnp.where(q < tot_r[:, None], real_rows, pad_rows)
    unit_rows = jnp.where((r < n_used)[:, None], unit_rows, idle_row).reshape(-1).astype(i32)

    xy = _experts(tile_expert.astype(i32), n_used, unit_rows, xy,
                  w_gate_exp[l].reshape(N_EXPERTS, D_MODEL, D_FF_EXPERT),
                  w_up_exp[l].reshape(N_EXPERTS, D_MODEL, D_FF_EXPERT),
                  w_down_exp[l].reshape(N_EXPERTS, D_FF_EXPERT, D_MODEL))
    out = _combine(x1, rcol, gate_f, g_final[None, :], xy)
    return out.reshape(BATCH, SEQ, D_MODEL)
```

```python
import functools
import math

import numpy as np
import jax
import jax.numpy as jnp
from jax import lax
from jax.experimental import pallas as pl
from jax.experimental.pallas import tpu as pltpu

F32 = jnp.float32
BF16 = jnp.bfloat16
HIGHEST = lax.Precision.HIGHEST

D_MODEL = 1024
BATCH = 8
SEQ = 4096
N_TOK = BATCH * SEQ
N_HEADS_SWA = 8
N_KV_HEADS_SWA = 2
N_HEADS_FOX = 8
HEAD_DIM = 64
WINDOW = 128
BLOCK = 128
NUM_BUCKETS = 32
MAX_DISTANCE = 128
N_GROUPS = 4
EXPERTS_PER_GROUP = 8
N_EXPERTS = N_GROUPS * EXPERTS_PER_GROUP
D_FF_EXPERT = 256
EPS = 1e-6
NEG_INF = -1e30

Q_A = N_HEADS_SWA * HEAD_DIM
KV_A = N_KV_HEADS_SWA * HEAD_DIM
W_B = N_HEADS_FOX * HEAD_DIM
LANES = 128
QK_SCALE = HEAD_DIM ** -0.5

TM_IN = 512
TM_POST = 512
TQ_FOX = 512
TK_FOX = TQ_FOX
SWA_BLOCKS = 4
TM_EXP = 512
TM_ROW = 512
UNIT = 16
XY_UNITS = 2 * TM_POST // UNIT + N_EXPERTS
XY_ROWS = XY_UNITS * UNIT
XY_COLS = D_MODEL + LANES
UNITS_PER_TILE = TM_EXP // UNIT
N_TOK_TILES = N_TOK // TM_POST
N_EXP_TILES = N_TOK_TILES * XY_UNITS // UNITS_PER_TILE + N_EXPERTS
PAD_UNITS_PER_EXPERT = UNITS_PER_TILE - 1
PAD_BLOCKS = -(-(N_EXPERTS * PAD_UNITS_PER_EXPERT * UNIT) // XY_ROWS)
PAD_BASE_ROW = N_TOK_TILES * XY_ROWS
W1_LANES, W2_LANES, E1_LANE, E2_LANE = (4, 6), (5, 7), 8, 9
VMEM_LIMIT = 56 * 1024 * 1024

DECAY_LANES = 6
PRUNE_MARGIN = 110.0


def _cparams(n_axes):
    return pltpu.CompilerParams(dimension_semantics=("arbitrary",) * n_axes,
                                vmem_limit_bytes=VMEM_LIMIT)


def _ada_kernel(c_ref, w_ref, b_ref, o_ref):
    c = c_ref[...]
    ca = c * jax.nn.sigmoid(c)
    o_ref[...] = jnp.dot(ca.astype(BF16), w_ref[...].astype(BF16),
                         preferred_element_type=F32) + b_ref[...]


def _ada(c16, w_ada, b_ada):
    n_out = w_ada.shape[1]
    blk = 1024
    return pl.pallas_call(
        _ada_kernel,
        out_shape=jax.ShapeDtypeStruct((16, n_out), F32),
        grid=(n_out // blk,),
        in_specs=[pl.BlockSpec((16, D_MODEL), lambda j: (0, 0)),
                  pl.BlockSpec((D_MODEL, blk), lambda j: (0, j)),
                  pl.BlockSpec((1, blk), lambda j: (0, j))],
        out_specs=pl.BlockSpec((16, blk), lambda j: (0, j)),
        compiler_params=_cparams(1),
        name="ada",
    )(c16, w_ada, b_ada)


def _inproj_kernel(x_ref, sc_ref, sh_ref, g_ref, wm_ref, wf_ref, wg_ref, ind_ref,
                   qa_ref, kd_ref, vd_ref, qb_ref, kb_ref, vb_ref, f_ref, gt_ref, nrm_ref):
    x = x_ref[...]
    rs = lax.rsqrt(jnp.mean(x * x, axis=-1, keepdims=True) + EPS)
    a = g_ref[...] * (1.0 + sc_ref[...])
    h = (x * rs * a + sh_ref[...]).astype(BF16)

    def mm(w):
        return jnp.dot(h, w, preferred_element_type=F32)

    qa_ref[...] = (mm(wm_ref[:, 0:512]) * QK_SCALE).astype(BF16)
    kd_ref[...] = mm(wm_ref[:, 512:768]).astype(BF16)
    vd_ref[...] = mm(wm_ref[:, 768:1024]).astype(BF16)
    qb = (mm(wm_ref[:, 1024:1536]) * QK_SCALE).astype(BF16)
    kb = mm(wm_ref[:, 1536:2048]).astype(BF16)
    qb_ref[...] = qb
    kb_ref[...] = kb
    vb_ref[...] = mm(wm_ref[:, 2048:2560]).astype(BF16)
    sq = jnp.concatenate([qb, kb], axis=1).astype(F32)
    seg = jnp.dot((sq * sq).astype(BF16), ind_ref[...], preferred_element_type=F32)
    nrm_ref[...] = jnp.broadcast_to(jnp.max(seg, axis=0, keepdims=True), nrm_ref.shape)
    f_ref[...] = mm(wf_ref[...])
    gt_ref[...] = mm(wg_ref[...]).astype(BF16)


def _inproj(x2, scale_m, shift_m, g_mix, w_main, w_f, w_g):
    tm = TM_IN
    tpb = SEQ // tm
    row = lambda i: (i, 0)
    per_b = lambda i: (i // tpb, 0, 0)
    const = lambda i: (0, 0)
    outs = [(Q_A, BF16), (2 * KV_A, BF16), (2 * KV_A, BF16), (W_B, BF16), (W_B, BF16), (W_B, BF16),
            (LANES, F32), (2 * D_MODEL, BF16)]
    ind_np = np.zeros((2 * W_B, LANES), np.float32)
    ind_np[np.arange(2 * W_B), np.arange(2 * W_B) // HEAD_DIM] = 1.0
    ind = jnp.asarray(ind_np, dtype=BF16)
    n_steps = N_TOK // tm
    return pl.pallas_call(
        _inproj_kernel,
        out_shape=[jax.ShapeDtypeStruct((N_TOK, w), dt) for w, dt in outs]
        + [jax.ShapeDtypeStruct((n_steps * 8, LANES), F32)],
        grid=(n_steps,),
        in_specs=[pl.BlockSpec((tm, D_MODEL), row),
                  pl.BlockSpec((None, 1, D_MODEL), per_b),
                  pl.BlockSpec((None, 1, D_MODEL), per_b),
                  pl.BlockSpec((1, D_MODEL), const),
                  pl.BlockSpec(w_main.shape, const),
                  pl.BlockSpec(w_f.shape, const),
                  pl.BlockSpec(w_g.shape, const),
                  pl.BlockSpec(ind.shape, const)],
        out_specs=[pl.BlockSpec((tm, w), row) for w, _ in outs] + [pl.BlockSpec((8, LANES), row)],
        compiler_params=_cparams(1),
        name="inproj",
    )(x2, scale_m, shift_m, g_mix, w_main, w_f, w_g, ind)


def _log_sigmoid(x):
    return jnp.minimum(x, 0.0) - jnp.log1p(jnp.exp(-jnp.abs(x)))


def _cum_kernel(f_ref, b_ref, jm_ref, qa_ref, ka_ref, fb_ref):
    r = lax.broadcasted_iota(jnp.int32, (LANES, LANES), 0)
    c = lax.broadcasted_iota(jnp.int32, (LANES, LANES), 1)
    expand = ((c >= DECAY_LANES * r) & (c < DECAY_LANES * r + DECAY_LANES)
              & (r < N_HEADS_FOX)).astype(F32)
    tril = (r >= c).astype(F32)
    lf = _log_sigmoid(f_ref[...] + b_ref[...])
    lfe = jnp.dot(lf, expand, precision=HIGHEST, preferred_element_type=F32)
    jm = jm_ref[...]
    carry = jnp.zeros((1, LANES), F32)
    for blk in range(SEQ // LANES):
        rows = slice(blk * LANES, (blk + 1) * LANES)
        cb = jnp.dot(tril, lfe[rows], precision=HIGHEST, preferred_element_type=F32) + carry
        carry = cb[LANES - 1:LANES]
        hi = cb.astype(BF16).astype(F32)
        r1 = cb - hi
        mid = r1.astype(BF16).astype(F32)
        lo = (r1 - mid).astype(BF16).astype(F32)
        one = jnp.ones_like(cb)
        zero = jnp.zeros_like(cb)
        qa = jnp.where(jm == 0, hi, jnp.where(jm == 1, mid, jnp.where(jm == 2, lo,
                       jnp.where(jm < DECAY_LANES, one, zero))))
        ka = jnp.where(jm == 3, -hi, jnp.where(jm == 4, -mid, jnp.where(jm == 5, -lo,
                       jnp.where(jm < 3, one, zero))))
        qa_ref[rows, :] = qa.astype(BF16)
        ka_ref[rows, :] = ka.astype(BF16)
        blocks_per_tile = TQ_FOX // LANES
        tile = blk // blocks_per_tile
        if blk % blocks_per_tile == 0:
            fb_ref[2 * tile:2 * tile + 1, :] = cb[0:1]
        if blk % blocks_per_tile == blocks_per_tile - 1:
            fb_ref[2 * tile + 1:2 * tile + 2, :] = carry


def _cum(f_pad, b_pad, jmod):
    n_tiles = SEQ // TQ_FOX
    return pl.pallas_call(
        _cum_kernel,
        out_shape=[jax.ShapeDtypeStruct((BATCH, SEQ, LANES), BF16)] * 2
        + [jax.ShapeDtypeStruct((BATCH, 2 * n_tiles, LANES), F32)],
        grid=(BATCH,),
        in_specs=[pl.BlockSpec((SEQ, LANES), lambda b: (b, 0)),
                  pl.BlockSpec((1, LANES), lambda b: (0, 0)),
                  pl.BlockSpec((1, LANES), lambda b: (0, 0))],
        out_specs=[pl.BlockSpec((None, SEQ, LANES), lambda b: (b, 0, 0))] * 2
        + [pl.BlockSpec((None, 2 * n_tiles, LANES), lambda b: (b, 0, 0))],
        compiler_params=_cparams(1),
        name="cum",
    )(f_pad, b_pad, jmod)


def _swa_block(sink_cols, q, kk, vv, bias_ref, lo):
    tiles = []
    for g in range(N_KV_HEADS_SWA):
        parts = []
        for t in range(2):
            qt = q[:, (2 * g + t) * LANES:(2 * g + t + 1) * LANES]
            zero = jnp.zeros_like(qt)
            parts.append(jnp.where(lo, qt, zero))
            parts.append(jnp.where(lo, zero, qt))
        q4 = jnp.concatenate(parts, axis=0)
        s = lax.dot_general(q4, kk[:, g * LANES:(g + 1) * LANES], (((1,), (1,)), ((), ())),
                            preferred_element_type=F32)
        s = s + bias_ref[g]
        sink = sink_cols[g]
        m = jnp.maximum(jnp.max(s, axis=-1, keepdims=True), sink)
        p = jnp.exp(s - m)
        den = jnp.sum(p, axis=-1, keepdims=True) + jnp.exp(sink - m)
        o = jnp.dot(p.astype(BF16), vv[:, g * LANES:(g + 1) * LANES],
                    preferred_element_type=F32) / den
        tiles.append(jnp.where(lo, o[0:BLOCK], o[BLOCK:2 * BLOCK]))
        tiles.append(jnp.where(lo, o[2 * BLOCK:3 * BLOCK], o[3 * BLOCK:4 * BLOCK]))
    return tiles


def _swa_kernel(sink_ref, q_ref, kc_ref, kp_ref, vc_ref, vp_ref, bias_first_ref, bias_ref, o_ref):
    lane = lax.broadcasted_iota(jnp.int32, (BLOCK, LANES), 1)
    lo = lane < HEAD_DIM
    grp = N_HEADS_SWA // N_KV_HEADS_SWA
    row = lax.broadcasted_iota(jnp.int32, (grp * BLOCK, 1), 0)
    sink_cols = []
    for g in range(N_KV_HEADS_SWA):
        col = jnp.full((grp * BLOCK, 1), sink_ref[g * grp + grp - 1], F32)
        for hh in range(grp - 2, -1, -1):
            col = jnp.where(row < (hh + 1) * BLOCK, sink_ref[g * grp + hh], col)
        sink_cols.append(col)
    for blk in range(SWA_BLOCKS):
        rows = slice(blk * BLOCK, (blk + 1) * BLOCK)
        if blk == 0:
            kk = jnp.concatenate([kp_ref[...], kc_ref[rows, :]], axis=0)
            vv = jnp.concatenate([vp_ref[...], vc_ref[rows, :]], axis=0)
            bias = bias_first_ref
        else:
            prev_rows = slice((blk - 1) * BLOCK, (blk + 1) * BLOCK)
            kk = kc_ref[prev_rows, :]
            vv = vc_ref[prev_rows, :]
            bias = bias_ref
        tiles = _swa_block(sink_cols, q_ref[rows, :], kk, vv, bias, lo)
        for c, tile in enumerate(tiles):
            o_ref[rows, c * LANES:(c + 1) * LANES] = tile.astype(BF16)


def _swa(sinks, qa, kdup, vdup, bias):
    nb = SEQ // BLOCK
    ns = nb // SWA_BLOCKS
    cur = lambda b, i, s: (b * ns + i, 0)
    prev = lambda b, i, s: (b * nb + jnp.maximum(SWA_BLOCKS * i - 1, 0), 0)
    grid_spec = pltpu.PrefetchScalarGridSpec(
        num_scalar_prefetch=1,
        grid=(BATCH, ns),
        in_specs=[pl.BlockSpec((SWA_BLOCKS * BLOCK, Q_A), cur),
                  pl.BlockSpec((SWA_BLOCKS * BLOCK, 2 * KV_A), cur),
                  pl.BlockSpec((BLOCK, 2 * KV_A), prev),
                  pl.BlockSpec((SWA_BLOCKS * BLOCK, 2 * KV_A), cur),
                  pl.BlockSpec((BLOCK, 2 * KV_A), prev),
                  pl.BlockSpec((None,) + bias.shape[1:], lambda b, i, s: (jnp.minimum(i, 1), 0, 0, 0)),
                  pl.BlockSpec((None,) + bias.shape[1:], lambda b, i, s: (1, 0, 0, 0))],
        out_specs=pl.BlockSpec((SWA_BLOCKS * BLOCK, Q_A), cur))
    return pl.pallas_call(
        _swa_kernel,
        out_shape=jax.ShapeDtypeStruct((N_TOK, Q_A), BF16),
        grid_spec=grid_spec,
        compiler_params=_cparams(2),
        name="swa",
    )(sinks, qa, kdup, kdup, vdup, vdup, bias, bias)


def _fox_kernel(js_ref, q_ref, k_ref, v_ref, qa_ref, ka_ref, o_ref,
                kaug, vaug, q2, m_sc, acc_sc, s_a, s_b):
    tq, tk = TQ_FOX, TK_FOX
    b = pl.program_id(0)
    t = pl.program_id(1)
    i = pl.program_id(2)
    j_start = js_ref[(b * pl.num_programs(1) + t) * pl.num_programs(2) + i]

    @pl.when(i == 0)
    def _():
        kaug[:, 0:LANES] = k_ref[...]
        kaug[:, LANES:2 * LANES] = ka_ref[...]
        vaug[:, 0:LANES] = v_ref[...]
        vaug[:, LANES:2 * LANES] = jnp.ones((SEQ, LANES), BF16)

    lane = lax.broadcasted_iota(jnp.int32, (tq, LANES), 1)
    lo = lane < HEAD_DIM
    base = 2 * DECAY_LANES * t
    own = [(lane >= base + h * DECAY_LANES) & (lane < base + (h + 1) * DECAY_LANES) for h in range(2)]
    q = q_ref[...]
    qa = qa_ref[...]
    zero = jnp.zeros_like(q)
    q2[0, :, 0:LANES] = jnp.where(lo, q, zero)
    q2[1, :, 0:LANES] = jnp.where(lo, zero, q)
    for h in range(2):
        q2[h, :, LANES:2 * LANES] = jnp.where(own[h], qa, zero)
    m_sc[...] = jnp.full(m_sc.shape, NEG_INF, F32)
    acc_sc[...] = jnp.zeros(acc_sc.shape, F32)

    def scores(h, ks):
        return lax.dot_general(q2[h], kaug[pl.ds(ks, tk), :], (((1,), (1,)), ((), ())),
                               preferred_element_type=F32)

    def consume(h, s, ks, mask):
        if mask is not None:
            s = jnp.where(mask, s, NEG_INF)
        m_prev = m_sc[h]
        m_new = jnp.maximum(m_prev, jnp.max(s, axis=-1, keepdims=True))
        alpha = jnp.exp(m_prev - m_new)
        p = jnp.exp(s - jnp.concatenate([m_new] * (tk // LANES), axis=1))
        pv = jnp.dot(p.astype(BF16), vaug[pl.ds(ks, tk), :], preferred_element_type=F32)
        acc_sc[h] = jnp.concatenate([alpha, alpha], axis=1) * acc_sc[h] + pv
        m_sc[h] = m_new

    def key_start(j):
        return pl.multiple_of(j * tk, tk)

    def scores_into(buf, j):
        for h in range(2):
            buf[h] = scores(h, key_start(j))

    def consume_from(buf, j, mask):
        for h in range(2):
            consume(h, buf[h], key_start(j), mask)

    rr = lax.broadcasted_iota(jnp.int32, (tq, tk), 0)
    cc = lax.broadcasted_iota(jnp.int32, (tq, tk), 1)
    causal = cc <= rr
    n_full = i - j_start

    scores_into(s_a, j_start)

    def pair(p, carry):
        j = j_start + 2 * p
        scores_into(s_b, j + 1)
        consume_from(s_a, j, None)
        scores_into(s_a, j + 2)
        consume_from(s_b, j + 1, None)
        return carry

    lax.fori_loop(0, n_full // 2, pair, 0)
    odd = lax.rem(n_full, 2) == 1

    @pl.when(odd)
    def _():
        scores_into(s_b, i)
        consume_from(s_a, i - 1, None)
        consume_from(s_b, i, causal)

    @pl.when(jnp.logical_not(odd))
    def _():
        consume_from(s_a, i, causal)

    outs = [acc_sc[h, :, 0:LANES] / acc_sc[h, :, LANES:2 * LANES] for h in range(2)]
    o_ref[...] = jnp.where(lo, outs[0], outs[1]).astype(BF16)


def _fox(j_start, qb, kb, vb, qa, ka):
    tq = TQ_FOX
    nq = SEQ // tq
    n_pairs = N_HEADS_FOX // 2
    qmap = lambda b, t, i, js: (b * nq + i, t)
    kmap = lambda b, t, i, js: (b, t)
    grid_spec = pltpu.PrefetchScalarGridSpec(
        num_scalar_prefetch=1,
        grid=(BATCH, n_pairs, nq),
        in_specs=[pl.BlockSpec((tq, LANES), qmap),
                  pl.BlockSpec((SEQ, LANES), kmap),
                  pl.BlockSpec((SEQ, LANES), kmap),
                  pl.BlockSpec((None, tq, LANES), lambda b, t, i, js: (b, i, 0)),
                  pl.BlockSpec((None, SEQ, LANES), lambda b, t, i, js: (b, 0, 0))],
        out_specs=pl.BlockSpec((tq, LANES), qmap),
        scratch_shapes=[pltpu.VMEM((SEQ, 2 * LANES), BF16),
                        pltpu.VMEM((SEQ, 2 * LANES), BF16),
                        pltpu.VMEM((2, tq, 2 * LANES), BF16),
                        pltpu.VMEM((2, tq, LANES), F32),
                        pltpu.VMEM((2, tq, 2 * LANES), F32),
                        pltpu.VMEM((2, tq, TK_FOX), F32),
                        pltpu.VMEM((2, tq, TK_FOX), F32)])
    return pl.pallas_call(
        _fox_kernel,
        out_shape=jax.ShapeDtypeStruct((N_TOK, W_B), BF16),
        grid_spec=grid_spec,
        compiler_params=_cparams(3),
        name="fox",
    )(j_start, qb, kb, vb, qa, ka)


def _fox_first_tiles(nrm, fb):
    n_tiles = SEQ // TQ_FOX
    nr = nrm.reshape(BATCH, n_tiles, 8, LANES)[:, :, 0, :] * 1.02
    qn = jnp.sqrt(nr[..., 0:N_HEADS_FOX])
    kn = jnp.sqrt(nr[..., N_HEADS_FOX:2 * N_HEADS_FOX])
    f_first = fb[:, 0::2, 0:DECAY_LANES * N_HEADS_FOX:DECAY_LANES]
    f_last = fb[:, 1::2, 0:DECAY_LANES * N_HEADS_FOX:DECAY_LANES]
    kn_prefix = lax.cummax(kn, axis=1)
    upper = qn[:, :, None, :] * kn_prefix[:, None, :, :] + f_first[:, :, None, :] - f_last[:, None, :, :]
    row_max_low = -(qn * kn)[:, :, None, :]
    ii = jnp.arange(n_tiles)[None, :, None, None]
    jj = jnp.arange(n_tiles)[None, None, :, None]
    skip = (upper < row_max_low - PRUNE_MARGIN) & (jj < ii)
    skip = jnp.all(skip.reshape(BATCH, n_tiles, n_tiles, N_HEADS_FOX // 2, 2), axis=-1)
    first = jnp.sum(jnp.cumprod(skip.astype(jnp.int32), axis=2), axis=2)
    return jnp.transpose(first, (0, 2, 1)).reshape(-1).astype(jnp.int32)


def _post_kernel(*refs):
    xy_ref = refs[-3]
    step = pl.program_id(0)

    @pl.when(step < N_TOK_TILES)
    def _():
        _post_body(*refs)

    @pl.when(step >= N_TOK_TILES)
    def _():
        xy_ref[...] = jnp.zeros(xy_ref.shape, BF16)


def _post_body(x_ref, oa_ref, ob_ref, gt_ref, gm_ref, sc_ref, sh_ref, g_ref,
               wa_ref, wb_ref, wo_ref, wr2_ref, br_ref,
               x1_ref, xy_ref, rc_ref, cu_ref):
    tm = TM_POST
    pa = jnp.dot(oa_ref[...], wa_ref[...], preferred_element_type=F32)
    pb = jnp.dot(ob_ref[...], wb_ref[...], preferred_element_type=F32)
    ga = jax.nn.sigmoid(gt_ref[:, 0:D_MODEL].astype(F32))
    gb = jax.nn.sigmoid(gt_ref[:, D_MODEL:2 * D_MODEL].astype(F32))
    merged = (ga * pa + gb * pb).astype(BF16)
    y = jnp.dot(merged, wo_ref[...], preferred_element_type=F32)
    x1 = x_ref[...] + gm_ref[...] * y
    x1_ref[...] = x1

    rs = lax.rsqrt(jnp.mean(x1 * x1, axis=-1, keepdims=True) + EPS)
    a = g_ref[...] * (1.0 + sc_ref[...])
    h2 = x1 * rs * a + sh_ref[...]

    hh = h2.astype(BF16)
    hl = (h2 - hh.astype(F32)).astype(BF16)
    hi_both = jnp.dot(hh, wr2_ref[...], preferred_element_type=F32)
    logits = (hi_both[:, 0:LANES] + hi_both[:, LANES:2 * LANES]
              + jnp.dot(hl, wr2_ref[:, 0:LANES], preferred_element_type=F32)
              + br_ref[...])

    lane = lax.broadcasted_iota(jnp.int32, (tm, LANES), 1).astype(F32)
    big = float(LANES)
    gl = jnp.where(lane < N_GROUPS, logits, -jnp.inf)
    gmax = jnp.max(gl, axis=-1, keepdims=True)
    gi = jnp.min(jnp.where(gl == gmax, lane, big), axis=-1, keepdims=True)
    gsum = jnp.sum(jnp.exp(gl - gmax), axis=-1, keepdims=True)
    gp = 1.0 / gsum
    e_lo = N_GROUPS + EXPERTS_PER_GROUP * gi
    el = jnp.where((lane >= e_lo) & (lane < e_lo + EXPERTS_PER_GROUP), logits, -jnp.inf)
    v1 = jnp.max(el, axis=-1, keepdims=True)
    i1 = jnp.min(jnp.where(el == v1, lane, big), axis=-1, keepdims=True)
    el2 = jnp.where(lane == i1, -jnp.inf, el)
    v2 = jnp.max(el2, axis=-1, keepdims=True)
    i2 = jnp.min(jnp.where(el2 == v2, lane, big), axis=-1, keepdims=True)
    e21 = jnp.exp(v2 - v1)
    w1 = gp / (1.0 + e21)
    w2 = gp * e21 / (1.0 + e21)
    e1 = i1 - N_GROUPS
    e2 = i2 - N_GROUPS

    oh = jnp.where((lane == e1) | (lane == e2), 1.0, 0.0)
    cnt_u = jnp.floor((jnp.sum(oh, axis=0, keepdims=True) + (UNIT - 1)) * (1.0 / UNIT))
    r128 = lax.broadcasted_iota(jnp.int32, (LANES, LANES), 0)
    c128 = lax.broadcasted_iota(jnp.int32, (LANES, LANES), 1)
    before_lane = jnp.where(r128 < c128, 1.0, 0.0).astype(BF16)
    loc_u = jnp.dot(jnp.broadcast_to(cnt_u, (8, LANES)).astype(BF16), before_lane,
                    preferred_element_type=F32)
    rr = lax.broadcasted_iota(jnp.int32, (tm, tm), 0)
    cc = lax.broadcasted_iota(jnp.int32, (tm, tm), 1)
    strict = jnp.where(rr > cc, 1.0, 0.0).astype(BF16)
    pos_e = jnp.dot(strict, oh.astype(BF16), preferred_element_type=F32) + loc_u[0:1] * UNIT
    lp1 = jnp.sum(jnp.where(lane == e1, pos_e, 0.0), axis=-1, keepdims=True)
    lp2 = jnp.sum(jnp.where(lane == e2, pos_e, 0.0), axis=-1, keepdims=True)

    eye = rr == cc
    ones8 = jnp.ones((8, tm), BF16)

    def to_row(col):
        hi = jnp.floor(col * (1.0 / 64.0))
        lo_ = col - 64.0 * hi
        d_hi = jnp.where(eye, hi, 0.0).astype(BF16)
        d_lo = jnp.where(eye, lo_, 0.0).astype(BF16)
        row8 = (64.0 * jnp.dot(ones8, d_hi, preferred_element_type=F32)
                + jnp.dot(ones8, d_lo, preferred_element_type=F32))
        return row8[0:1]

    srow = lax.broadcasted_iota(jnp.int32, (XY_ROWS, tm), 0).astype(F32)
    pm1 = jnp.where(srow == to_row(lp1), 1.0, 0.0).astype(BF16)
    pm2 = jnp.where(srow == to_row(lp2), 1.0, 0.0).astype(BF16)
    w1h = w1.astype(BF16).astype(F32)
    w2h = w2.astype(BF16).astype(F32)
    side = jnp.where(lane == W1_LANES[0], w1h, jnp.where(lane == W1_LANES[1], w1 - w1h,
           jnp.where(lane == W2_LANES[0], w2h, jnp.where(lane == W2_LANES[1], w2 - w2h,
           jnp.where(lane == E1_LANE, e1, jnp.where(lane == E2_LANE, e2, 0.0))))))
    tok = jnp.concatenate([hh, side.astype(BF16)], axis=1)
    xy_ref[...] = jnp.dot(pm1 + pm2, tok, preferred_element_type=F32).astype(BF16)

    cu_ref[...] = jnp.broadcast_to(cnt_u, cu_ref.shape)
    rc_ref[...] = jnp.where(lane == 0, lp1, jnp.where(lane == 1, lp2, 0.0))


def _post(x2, oa, ob, gates, gate_m, scale_f, shift_f, g_ffn, wa, wb, wo, wr2, b_r):
    tm = TM_POST
    tpb = SEQ // tm
    n_steps = N_TOK_TILES
    row = lambda i: (jnp.minimum(i, n_steps - 1), 0)
    per_b = lambda i: (jnp.minimum(i, n_steps - 1) // tpb, 0, 0)
    const = lambda i: (0, 0)
    return pl.pallas_call(
        _post_kernel,
        out_shape=[jax.ShapeDtypeStruct((N_TOK, D_MODEL), F32),
                   jax.ShapeDtypeStruct(((n_steps + PAD_BLOCKS) * XY_ROWS, XY_COLS), BF16),
                   jax.ShapeDtypeStruct((N_TOK, LANES), F32),
                   jax.ShapeDtypeStruct((n_steps * 8, LANES), F32)],
        grid=(n_steps + PAD_BLOCKS,),
        in_specs=[pl.BlockSpec((tm, D_MODEL), row),
                  pl.BlockSpec((tm, Q_A), row),
                  pl.BlockSpec((tm, W_B), row),
                  pl.BlockSpec((tm, 2 * D_MODEL), row),
                  pl.BlockSpec((None, 1, D_MODEL), per_b),
                  pl.BlockSpec((None, 1, D_MODEL), per_b),
                  pl.BlockSpec((None, 1, D_MODEL), per_b),
                  pl.BlockSpec((1, D_MODEL), const),
                  pl.BlockSpec(wa.shape, const),
                  pl.BlockSpec(wb.shape, const),
                  pl.BlockSpec(wo.shape, const),
                  pl.BlockSpec(wr2.shape, const),
                  pl.BlockSpec((1, LANES), const)],
        out_specs=[pl.BlockSpec((tm, D_MODEL), row),
                   pl.BlockSpec((XY_ROWS, XY_COLS), lambda i: (i, 0)),
                   pl.BlockSpec((tm, LANES), row),
                   pl.BlockSpec((8, LANES), row)],
        compiler_params=_cparams(1),
        name="post",
    )(x2, oa, ob, gates, gate_m, scale_f, shift_f, g_ffn, wa, wb, wo, wr2, b_r)


def _experts_kernel(te_ref, nu_ref, ur_ref, xy_in, wg_ref, wu_ref, wd_ref, xy_out,
                    xbuf, ybuf, gsem, ssem):
    del xy_in
    r = pl.program_id(0)
    last = pl.num_programs(0) - 1
    n_used = nu_ref[0]
    slot = lax.rem(r, 2)

    def unit_row(step, s):
        return pl.multiple_of(ur_ref[step * UNITS_PER_TILE + s], UNIT)

    def start_gathers(step, sl):
        for s in range(UNITS_PER_TILE):
            pltpu.make_async_copy(xy_out.at[pl.ds(unit_row(step, s), UNIT), :],
                                  xbuf.at[sl, pl.ds(s * UNIT, UNIT), :], gsem.at[sl]).start()

    def wait_gathers(sl):
        pltpu.make_async_copy(xy_out.at[pl.ds(0, TM_EXP), :], xbuf.at[sl], gsem.at[sl]).wait()

    def start_scatters(step, sl):
        for s in range(UNITS_PER_TILE):
            pltpu.make_async_copy(ybuf.at[sl, pl.ds(s * UNIT, UNIT), :],
                                  xy_out.at[pl.ds(unit_row(step, s), UNIT), pl.ds(0, D_MODEL)],
                                  ssem.at[sl]).start()

    def wait_scatters(sl):
        pltpu.make_async_copy(ybuf.at[sl], xy_out.at[pl.ds(0, TM_EXP), pl.ds(0, D_MODEL)], ssem.at[sl]).wait()

    @pl.when(r == 0)
    def _():
        start_gathers(0, 0)

    @pl.when(jnp.logical_and(r < n_used, r >= 2))
    def _():
        wait_scatters(slot)

    @pl.when(r < n_used)
    def _():
        wait_gathers(slot)
        start_gathers(jnp.minimum(r + 1, last), 1 - slot)
        x = xbuf[slot, :, 0:D_MODEL]
        side = xbuf[slot, :, D_MODEL:XY_COLS].astype(F32)
        lane = lax.broadcasted_iota(jnp.int32, side.shape, 1)

        def lanes_sum(a, b):
            return jnp.sum(jnp.where((lane == a) | (lane == b), side, 0.0), axis=-1, keepdims=True)

        is_slot1 = lanes_sum(E1_LANE, E1_LANE) == te_ref[r].astype(F32)
        wrow = jnp.where(is_slot1, lanes_sum(*W1_LANES), lanes_sum(*W2_LANES))
        a = jnp.dot(x, wg_ref[...].astype(BF16), preferred_element_type=F32)
        u = jnp.dot(x, wu_ref[...].astype(BF16), preferred_element_type=F32)
        hid = (a * jax.nn.sigmoid(a) * u * wrow).astype(BF16)
        ybuf[slot] = jnp.dot(hid, wd_ref[...].astype(BF16), preferred_element_type=F32).astype(BF16)
        start_scatters(r, slot)

    @pl.when(r == n_used - 1)
    def _():
        wait_gathers(1 - slot)
        wait_scatters(slot)

        @pl.when(r >= 1)
        def _():
            wait_scatters(1 - slot)


def _experts(tile_expert, n_used, unit_rows, xy, wg, wu, wd):
    wmap = lambda r, te, nv, ur: (te[r], 0, 0)
    grid_spec = pltpu.PrefetchScalarGridSpec(
        num_scalar_prefetch=3,
        grid=(N_EXP_TILES,),
        in_specs=[pl.BlockSpec(memory_space=pl.ANY),
                  pl.BlockSpec((None, D_MODEL, D_FF_EXPERT), wmap),
                  pl.BlockSpec((None, D_MODEL, D_FF_EXPERT), wmap),
                  pl.BlockSpec((None, D_FF_EXPERT, D_MODEL), wmap)],
        out_specs=pl.BlockSpec(memory_space=pl.ANY),
        scratch_shapes=[pltpu.VMEM((2, TM_EXP, XY_COLS), BF16),
                        pltpu.VMEM((2, TM_EXP, D_MODEL), BF16),
                        pltpu.SemaphoreType.DMA((2,)),
                        pltpu.SemaphoreType.DMA((2,))])
    return pl.pallas_call(
        _experts_kernel,
        out_shape=jax.ShapeDtypeStruct(xy.shape, xy.dtype),
        grid_spec=grid_spec,
        input_output_aliases={3: 0},
        compiler_params=_cparams(1),
        name="experts",
    )(tile_expert, n_used, unit_rows, xy, wg, wu, wd)


def _combine_kernel(x1_ref, rc_ref, gf_ref, gfin_ref, y_ref, o_ref):
    lp1 = rc_ref[:, 0:1]
    lp2 = rc_ref[:, 1:2]
    scol = lax.broadcasted_iota(jnp.int32, (TM_ROW, XY_ROWS), 1).astype(F32)
    pick = jnp.where((scol == lp1) | (scol == lp2), 1.0, 0.0).astype(BF16)
    y = jnp.dot(pick, y_ref[...], preferred_element_type=F32)
    xf = x1_ref[...] + gf_ref[...] * y
    rs = lax.rsqrt(jnp.mean(xf * xf, axis=-1, keepdims=True) + EPS)
    o_ref[...] = xf * rs * gfin_ref[...]


def _combine(x1, rcol, gate_f, g_final, xy):
    tm = TM_ROW
    tpb = SEQ // tm
    row = lambda i: (i, 0)
    return pl.pallas_call(
        _combine_kernel,
        out_shape=jax.ShapeDtypeStruct((N_TOK, D_MODEL), F32),
        grid=(N_TOK // tm,),
        in_specs=[pl.BlockSpec((tm, D_MODEL), row),
                  pl.BlockSpec((tm, LANES), row),
                  pl.BlockSpec((None, 1, D_MODEL), lambda i: (i // tpb, 0, 0)),
                  pl.BlockSpec((1, D_MODEL), lambda i: (0, 0)),
                  pl.BlockSpec((XY_ROWS, D_MODEL), row)],
        out_specs=pl.BlockSpec((tm, D_MODEL), row),
        compiler_params=_cparams(1),
        name="combine",
    )(x1, rcol, gate_f, g_final, xy)


def _t5_bucket_np():
    qi = np.arange(BLOCK)[:, None]
    kj = np.arange(2 * BLOCK)[None, :]
    dist = qi - kj + BLOCK
    n = np.maximum(dist, 0)
    max_exact = NUM_BUCKETS // 2
    nf = np.maximum(n, 1).astype(np.float32)
    large = max_exact + (np.log(nf / np.float32(max_exact)) / np.float32(math.log(MAX_DISTANCE / max_exact))
                         * np.float32(NUM_BUCKETS - max_exact)).astype(np.int32)
    large = np.minimum(large, NUM_BUCKETS - 1)
    bucket = np.where(n < max_exact, n, large)
    band = (dist >= 0) & (dist < WINDOW)
    return bucket.astype(np.int32), band


def kernel(x, c, w_ada, b_ada, g_norm_mix, g_norm_ffn, w_in, sinks, b_forget, w_proj_swa, w_proj_fox,
           w_out, rel_bias_table, w_router_group, b_router_group, w_router_expert, b_router_expert,
           w_gate_exp, w_up_exp, w_down_exp, g_final):
    l = 0
    x2 = x.reshape(N_TOK, D_MODEL)

    c16 = jnp.concatenate([c, jnp.zeros_like(c)], axis=0)
    mod = _ada(c16, w_ada[l], b_ada[l][None, :])[:BATCH]
    shift_m, scale_m, gate_m, shift_f, scale_f, gate_f = [
        m.reshape(BATCH, 1, D_MODEL) for m in jnp.split(mod, 6, axis=-1)]

    w = w_in[l]
    o_ka, o_va, o_qb = Q_A, Q_A + KV_A, Q_A + 2 * KV_A
    o_kb, o_vb, o_f = o_qb + W_B, o_qb + 2 * W_B, o_qb + 3 * W_B
    o_g = o_f + N_HEADS_FOX

    def dup(cols):
        heads = [cols[:, h * HEAD_DIM:(h + 1) * HEAD_DIM] for h in range(N_KV_HEADS_SWA)]
        return jnp.concatenate([hd for hd in heads for _ in range(2)], axis=1)

    w_main = jnp.concatenate([w[:, :Q_A], dup(w[:, o_ka:o_va]), dup(w[:, o_va:o_qb]),
                              w[:, o_qb:o_f]], axis=1).astype(BF16)
    w_f = jnp.pad(w[:, o_f:o_g], ((0, 0), (0, LANES - N_HEADS_FOX))).astype(BF16)
    w_g = w[:, o_g:].astype(BF16)
    qa, kdup, vdup, qb, kb, vb, f_pad, gates, nrm = _inproj(
        x2, scale_m, shift_m, g_norm_mix[l][None, :], w_main, w_f, w_g)

    b_pad = jnp.pad(b_forget[l], (0, LANES - N_HEADS_FOX))[None, :]
    lanes = np.arange(LANES)
    jmod = jnp.asarray(np.where(lanes < DECAY_LANES * N_HEADS_FOX, lanes % DECAY_LANES, 7)[None, :].astype(np.int32))
    dq, dk, fb = _cum(f_pad, b_pad, jmod)

    bucket, band = _t5_bucket_np()
    onehot = jnp.asarray(bucket[None] == np.arange(NUM_BUCKETS)[:, None, None], dtype=F32)
    bias = jnp.einsum("bh,bqk->hqk", rel_bias_table.astype(F32), onehot, precision=HIGHEST)
    bias = jnp.where(band[None], bias, NEG_INF)
    first = np.arange(2 * BLOCK)[None, None, :] < BLOCK
    bias = jnp.stack([jnp.where(first, NEG_INF, bias), bias]).reshape(2, N_KV_HEADS_SWA, -1, 2 * BLOCK)
    o_a = _swa(sinks[l].astype(F32), qa, kdup, vdup, bias)

    o_b = _fox(_fox_first_tiles(nrm, fb), qb, kb, vb, dq, dk)

    w_r = jnp.concatenate([w_router_group[l]] + [w_router_expert[l][g] for g in range(N_GROUPS)], axis=1)
    w_r = jnp.pad(w_r, ((0, 0), (0, LANES - w_r.shape[1])))
    wr_hi = w_r.astype(BF16)
    wr_lo = (w_r - wr_hi.astype(F32)).astype(BF16)
    wr2 = jnp.concatenate([wr_hi, wr_lo], axis=1)
    b_r = jnp.concatenate([b_router_group[l], b_router_expert[l].reshape(-1)])
    b_r = jnp.pad(b_r, (0, LANES - b_r.shape[0]))[None, :]
    x1, xy, rcol, cu = _post(x2, o_a, o_b, gates, gate_m, scale_f, shift_f, g_norm_ffn[l][None, :],
                             w_proj_swa[l].astype(BF16), w_proj_fox[l].astype(BF16), w_out[l].astype(BF16),
                             wr2, b_r)

    i32 = jnp.int32
    n_tok_tiles = N_TOK // TM_POST
    cu = cu.reshape(n_tok_tiles, 8, LANES)[:, 0, :N_EXPERTS].astype(i32)
    loc_u = jnp.cumsum(cu, axis=1) - cu
    cend = jnp.cumsum(cu, axis=0)
    cstart = cend - cu
    tot_u = cend[-1]
    tiles_e = (tot_u + UNITS_PER_TILE - 1) // UNITS_PER_TILE
    tile_end = jnp.cumsum(tiles_e)
    tile_start = tile_end - tiles_e
    r = jnp.arange(N_EXP_TILES, dtype=i32)
    tile_expert = jnp.minimum(jnp.sum((tile_end[None, :] <= r[:, None]).astype(i32), axis=1), N_EXPERTS - 1)
    sel_e = tile_expert[:, None] == jnp.arange(N_EXPERTS, dtype=i32)[None, :]
    tw = r - jnp.sum(jnp.where(sel_e, tile_start[None, :], 0), axis=1)
    tot_r = jnp.sum(jnp.where(sel_e, tot_u[None, :], 0), axis=1)
    n_used = tile_end[-1:].astype(i32)
    q = tw[:, None] * UNITS_PER_TILE + jnp.arange(UNITS_PER_TILE, dtype=i32)[None, :]

    def of_expert(tab):
        return jnp.sum(jnp.where(sel_e[:, None, :], tab[None, :, :], 0), axis=2)

    cend_r, cstart_r, loc_r = of_expert(cend), of_expert(cstart), of_expert(loc_u)
    src_tile = jnp.minimum(jnp.sum((cend_r[:, None, :] <= q[:, :, None]).astype(i32), axis=2), n_tok_tiles - 1)
    sel_t = src_tile[:, :, None] == jnp.arange(n_tok_tiles, dtype=i32)[None, None, :]
    k = (q - jnp.sum(jnp.where(sel_t, cstart_r[:, None, :], 0), axis=2)
         + jnp.sum(jnp.where(sel_t, loc_r[:, None, :], 0), axis=2))
    real_rows = src_tile * XY_ROWS + k * UNIT
    pad_rows = PAD_BASE_ROW + (tile_expert[:, None] * PAD_UNITS_PER_EXPERT + (q - tot_r[:, None])) * UNIT
    idle_row = PAD_BASE_ROW + N_EXPERTS * PAD_UNITS_PER_EXPERT * UNIT
    unit_rows = jnp.where(q < tot_r[:, None], real_rows, pad_rows)
    unit_rows = jnp.where((r < n_used)[:, None], unit_rows, idle_row).reshape(-1).astype(i32)

    xy = _experts(tile_expert.astype(i32), n_used, unit_rows, xy,
                  w_gate_exp[l].reshape(N_EXPERTS, D_MODEL, D_FF_EXPERT),
                  w_up_exp[l].reshape(N_EXPERTS, D_MODEL, D_FF_EXPERT),
                  w_down_exp[l].reshape(N_EXPERTS, D_FF_EXPERT, D_MODEL))
    out = _combine(x1, rcol, gate_f, g_final[None, :], xy)
    return out.reshape(BATCH, SEQ, D_MODEL)
```

```python
import functools
import math

import numpy as np
import jax
import jax.numpy as jnp
from jax import lax
from jax.experimental import pallas as pl
from jax.experimental.pallas import tpu as pltpu

F32 = jnp.float32
BF16 = jnp.bfloat16
HIGHEST = lax.Precision.HIGHEST

D_MODEL = 1024
BATCH = 8
SEQ = 4096
N_TOK = BATCH * SEQ
N_HEADS_SWA = 8
N_KV_HEADS_SWA = 2
N_HEADS_FOX = 8
HEAD_DIM = 64
WINDOW = 128
BLOCK = 128
NUM_BUCKETS = 32
MAX_DISTANCE = 128
N_GROUPS = 4
EXPERTS_PER_GROUP = 8
N_EXPERTS = N_GROUPS * EXPERTS_PER_GROUP
D_FF_EXPERT = 256
EPS = 1e-6
NEG_INF = -1e30

Q_A = N_HEADS_SWA * HEAD_DIM
KV_A = N_KV_HEADS_SWA * HEAD_DIM
W_B = N_HEADS_FOX * HEAD_DIM
LANES = 128
QK_SCALE = HEAD_DIM ** -0.5

TM_IN = 512
TM_POST = 512
TQ_FOX = 512
TK_FOX = TQ_FOX
SWA_BLOCKS = 4
TM_EXP = 512
TM_ROW = 512
UNIT = 16
XY_UNITS = 2 * TM_POST // UNIT + N_EXPERTS
XY_ROWS = XY_UNITS * UNIT
XY_COLS = D_MODEL + LANES
UNITS_PER_TILE = TM_EXP // UNIT
N_TOK_TILES = N_TOK // TM_POST
N_EXP_TILES = N_TOK_TILES * XY_UNITS // UNITS_PER_TILE + N_EXPERTS
PAD_UNITS_PER_EXPERT = UNITS_PER_TILE - 1
PAD_BLOCKS = -(-(N_EXPERTS * PAD_UNITS_PER_EXPERT * UNIT) // XY_ROWS)
PAD_BASE_ROW = N_TOK_TILES * XY_ROWS
W1_LANES, W2_LANES, E1_LANE, E2_LANE = (4, 6), (5, 7), 8, 9
VMEM_LIMIT = 56 * 1024 * 1024

DECAY_LANES = 6
PRUNE_MARGIN = 110.0


def _cparams(n_axes):
    return pltpu.CompilerParams(dimension_semantics=("arbitrary",) * n_axes,
                                vmem_limit_bytes=VMEM_LIMIT)


def _ada_kernel(c_ref, w_ref, b_ref, o_ref):
    c = c_ref[...]
    ca = c * jax.nn.sigmoid(c)
    o_ref[...] = jnp.dot(ca.astype(BF16), w_ref[...].astype(BF16),
                         preferred_element_type=F32) + b_ref[...]


def _ada(c16, w_ada, b_ada):
    n_out = w_ada.shape[1]
    blk = 1024
    return pl.pallas_call(
        _ada_kernel,
        out_shape=jax.ShapeDtypeStruct((16, n_out), F32),
        grid=(n_out // blk,),
        in_specs=[pl.BlockSpec((16, D_MODEL), lambda j: (0, 0)),
                  pl.BlockSpec((D_MODEL, blk), lambda j: (0, j)),
                  pl.BlockSpec((1, blk), lambda j: (0, j))],
        out_specs=pl.BlockSpec((16, blk), lambda j: (0, j)),
        compiler_params=_cparams(1),
        name="ada",
    )(c16, w_ada, b_ada)


def _inproj_kernel(x_ref, sc_ref, sh_ref, g_ref, wm_ref, wf_ref, wg_ref, ind_ref,
                   qa_ref, kd_ref, vd_ref, qb_ref, kb_ref, vb_ref, f_ref, gt_ref, nrm_ref):
    x = x_ref[...]
    rs = lax.rsqrt(jnp.mean(x * x, axis=-1, keepdims=True) + EPS)
    a = g_ref[...] * (1.0 + sc_ref[...])
    h = (x * rs * a + sh_ref[...]).astype(BF16)

    def mm(w):
        return jnp.dot(h, w, preferred_element_type=F32)

    qa_ref[...] = (mm(wm_ref[:, 0:512]) * QK_SCALE).astype(BF16)
    kd_ref[...] = mm(wm_ref[:, 512:768]).astype(BF16)
    vd_ref[...] = mm(wm_ref[:, 768:1024]).astype(BF16)
    qb = (mm(wm_ref[:, 1024:1536]) * QK_SCALE).astype(BF16)
    kb = mm(wm_ref[:, 1536:2048]).astype(BF16)
    qb_ref[...] = qb
    kb_ref[...] = kb
    vb_ref[...] = mm(wm_ref[:, 2048:2560]).astype(BF16)
    sq = jnp.concatenate([qb, kb], axis=1).astype(F32)
    seg = jnp.dot((sq * sq).astype(BF16), ind_ref[...], preferred_element_type=F32)
    nrm_ref[...] = jnp.broadcast_to(jnp.max(seg, axis=0, keepdims=True), nrm_ref.shape)
    f_ref[...] = mm(wf_ref[...])
    gt_ref[...] = mm(wg_ref[...]).astype(BF16)


def _inproj(x2, scale_m, shift_m, g_mix, w_main, w_f, w_g):
    tm = TM_IN
    tpb = SEQ // tm
    row = lambda i: (i, 0)
    per_b = lambda i: (i // tpb, 0, 0)
    const = lambda i: (0, 0)
    outs = [(Q_A, BF16), (2 * KV_A, BF16), (2 * KV_A, BF16), (W_B, BF16), (W_B, BF16), (W_B, BF16),
            (LANES, F32), (2 * D_MODEL, BF16)]
    ind_np = np.zeros((2 * W_B, LANES), np.float32)
    ind_np[np.arange(2 * W_B), np.arange(2 * W_B) // HEAD_DIM] = 1.0
    ind = jnp.asarray(ind_np, dtype=BF16)
    n_steps = N_TOK // tm
    return pl.pallas_call(
        _inproj_kernel,
        out_shape=[jax.ShapeDtypeStruct((N_TOK, w), dt) for w, dt in outs]
        + [jax.ShapeDtypeStruct((n_steps * 8, LANES), F32)],
        grid=(n_steps,),
        in_specs=[pl.BlockSpec((tm, D_MODEL), row),
                  pl.BlockSpec((None, 1, D_MODEL), per_b),
                  pl.BlockSpec((None, 1, D_MODEL), per_b),
                  pl.BlockSpec((1, D_MODEL), const),
                  pl.BlockSpec(w_main.shape, const),
                  pl.BlockSpec(w_f.shape, const),
                  pl.BlockSpec(w_g.shape, const),
                  pl.BlockSpec(ind.shape, const)],
        out_specs=[pl.BlockSpec((tm, w), row) for w, _ in outs] + [pl.BlockSpec((8, LANES), row)],
        compiler_params=_cparams(1),
        name="inproj",
    )(x2, scale_m, shift_m, g_mix, w_main, w_f, w_g, ind)


def _log_sigmoid(x):
    return jnp.minimum(x, 0.0) - jnp.log1p(jnp.exp(-jnp.abs(x)))


def _cum_kernel(f_ref, b_ref, jm_ref, qa_ref, ka_ref, fb_ref):
    r = lax.broadcasted_iota(jnp.int32, (LANES, LANES), 0)
    c = lax.broadcasted_iota(jnp.int32, (LANES, LANES), 1)
    expand = ((c >= DECAY_LANES * r) & (c < DECAY_LANES * r + DECAY_LANES)
              & (r < N_HEADS_FOX)).astype(F32)
    tril = (r >= c).astype(F32)
    lf = _log_sigmoid(f_ref[...] + b_ref[...])
    lfe = jnp.dot(lf, expand, precision=HIGHEST, preferred_element_type=F32)
    jm = jm_ref[...]
    carry = jnp.zeros((1, LANES), F32)
    for blk in range(SEQ // LANES):
        rows = slice(blk * LANES, (blk + 1) * LANES)
        cb = jnp.dot(tril, lfe[rows], precision=HIGHEST, preferred_element_type=F32) + carry
        carry = cb[LANES - 1:LANES]
        hi = cb.astype(BF16).astype(F32)
        r1 = cb - hi
        mid = r1.astype(BF16).astype(F32)
        lo = (r1 - mid).astype(BF16).astype(F32)
        one = jnp.ones_like(cb)
        zero = jnp.zeros_like(cb)
        qa = jnp.where(jm == 0, hi, jnp.where(jm == 1, mid, jnp.where(jm == 2, lo,
                       jnp.where(jm < DECAY_LANES, one, zero))))
        ka = jnp.where(jm == 3, -hi, jnp.where(jm == 4, -mid, jnp.where(jm == 5, -lo,
                       jnp.where(jm < 3, one, zero))))
        qa_ref[rows, :] = qa.astype(BF16)
        ka_ref[rows, :] = ka.astype(BF16)
        blocks_per_tile = TQ_FOX // LANES
        tile = blk // blocks_per_tile
        if blk % blocks_per_tile == 0:
            fb_ref[2 * tile:2 * tile + 1, :] = cb[0:1]
        if blk % blocks_per_tile == blocks_per_tile - 1:
            fb_ref[2 * tile + 1:2 * tile + 2, :] = carry


def _cum(f_pad, b_pad, jmod):
    n_tiles = SEQ // TQ_FOX
    return pl.pallas_call(
        _cum_kernel,
        out_shape=[jax.ShapeDtypeStruct((BATCH, SEQ, LANES), BF16)] * 2
        + [jax.ShapeDtypeStruct((BATCH, 2 * n_tiles, LANES), F32)],
        grid=(BATCH,),
        in_specs=[pl.BlockSpec((SEQ, LANES), lambda b: (b, 0)),
                  pl.BlockSpec((1, LANES), lambda b: (0, 0)),
                  pl.BlockSpec((1, LANES), lambda b: (0, 0))],
        out_specs=[pl.BlockSpec((None, SEQ, LANES), lambda b: (b, 0, 0))] * 2
        + [pl.BlockSpec((None, 2 * n_tiles, LANES), lambda b: (b, 0, 0))],
        compiler_params=_cparams(1),
        name="cum",
    )(f_pad, b_pad, jmod)


def _swa_block(sink_cols, q, kk, vv, bias_ref, lo):
    tiles = []
    for g in range(N_KV_HEADS_SWA):
        parts = []
        for t in range(2):
            qt = q[:, (2 * g + t) * LANES:(2 * g + t + 1) * LANES]
            zero = jnp.zeros_like(qt)
            parts.append(jnp.where(lo, qt, zero))
            parts.append(jnp.where(lo, zero, qt))
        q4 = jnp.concatenate(parts, axis=0)
        s = lax.dot_general(q4, kk[:, g * LANES:(g + 1) * LANES], (((1,), (1,)), ((), ())),
                            preferred_element_type=F32)
        s = s + bias_ref[g]
        sink = sink_cols[g]
        m = jnp.maximum(jnp.max(s, axis=-1, keepdims=True), sink)
        p = jnp.exp(s - m)
        den = jnp.sum(p, axis=-1, keepdims=True) + jnp.exp(sink - m)
        o = jnp.dot(p.astype(BF16), vv[:, g * LANES:(g + 1) * LANES],
                    preferred_element_type=F32) / den
        tiles.append(jnp.where(lo, o[0:BLOCK], o[BLOCK:2 * BLOCK]))
        tiles.append(jnp.where(lo, o[2 * BLOCK:3 * BLOCK], o[3 * BLOCK:4 * BLOCK]))
    return tiles


def _swa_kernel(sink_ref, q_ref, kc_ref, kp_ref, vc_ref, vp_ref, bias_first_ref, bias_ref, o_ref):
    lane = lax.broadcasted_iota(jnp.int32, (BLOCK, LANES), 1)
    lo = lane < HEAD_DIM
    grp = N_HEADS_SWA // N_KV_HEADS_SWA
    row = lax.broadcasted_iota(jnp.int32, (grp * BLOCK, 1), 0)
    sink_cols = []
    for g in range(N_KV_HEADS_SWA):
        col = jnp.full((grp * BLOCK, 1), sink_ref[g * grp + grp - 1], F32)
        for hh in range(grp - 2, -1, -1):
            col = jnp.where(row < (hh + 1) * BLOCK, sink_ref[g * grp + hh], col)
        sink_cols.append(col)
    for blk in range(SWA_BLOCKS):
        rows = slice(blk * BLOCK, (blk + 1) * BLOCK)
        if blk == 0:
            kk = jnp.concatenate([kp_ref[...], kc_ref[rows, :]], axis=0)
            vv = jnp.concatenate([vp_ref[...], vc_ref[rows, :]], axis=0)
            bias = bias_first_ref
        else:
            prev_rows = slice((blk - 1) * BLOCK, (blk + 1) * BLOCK)
            kk = kc_ref[prev_rows, :]
            vv = vc_ref[prev_rows, :]
            bias = bias_ref
        tiles = _swa_block(sink_cols, q_ref[rows, :], kk, vv, bias, lo)
        for c, tile in enumerate(tiles):
            o_ref[rows, c * LANES:(c + 1) * LANES] = tile.astype(BF16)


def _swa(sinks, qa, kdup, vdup, bias):
    nb = SEQ // BLOCK
    ns = nb // SWA_BLOCKS
    cur = lambda b, i, s: (b * ns + i, 0)
    prev = lambda b, i, s: (b * nb + jnp.maximum(SWA_BLOCKS * i - 1, 0), 0)
    grid_spec = pltpu.PrefetchScalarGridSpec(
        num_scalar_prefetch=1,
        grid=(BATCH, ns),
        in_specs=[pl.BlockSpec((SWA_BLOCKS * BLOCK, Q_A), cur),
                  pl.BlockSpec((SWA_BLOCKS * BLOCK, 2 * KV_A), cur),
                  pl.BlockSpec((BLOCK, 2 * KV_A), prev),
                  pl.BlockSpec((SWA_BLOCKS * BLOCK, 2 * KV_A), cur),
                  pl.BlockSpec((BLOCK, 2 * KV_A), prev),
                  pl.BlockSpec((None,) + bias.shape[1:], lambda b, i, s: (jnp.minimum(i, 1), 0, 0, 0)),
                  pl.BlockSpec((None,) + bias.shape[1:], lambda b, i, s: (1, 0, 0, 0))],
        out_specs=pl.BlockSpec((SWA_BLOCKS * BLOCK, Q_A), cur))
    return pl.pallas_call(
        _swa_kernel,
        out_shape=jax.ShapeDtypeStruct((N_TOK, Q_A), BF16),
        grid_spec=grid_spec,
        compiler_params=_cparams(2),
        name="swa",
    )(sinks, qa, kdup, kdup, vdup, vdup, bias, bias)


def _fox_kernel(js_ref, q_ref, k_ref, v_ref, qa_ref, ka_ref, o_ref,
                kaug, vaug, q2, m_sc, acc_sc, s_a, s_b):
    tq, tk = TQ_FOX, TK_FOX
    b = pl.program_id(0)
    t = pl.program_id(1)
    i = pl.program_id(2)
    j_start = js_ref[(b * pl.num_programs(1) + t) * pl.num_programs(2) + i]

    @pl.when(i == 0)
    def _():
        kaug[:, 0:LANES] = k_ref[...]
        kaug[:, LANES:2 * LANES] = ka_ref[...]
        vaug[:, 0:LANES] = v_ref[...]
        vaug[:, LANES:2 * LANES] = jnp.ones((SEQ, LANES), BF16)

    lane = lax.broadcasted_iota(jnp.int32, (tq, LANES), 1)
    lo = lane < HEAD_DIM
    base = 2 * DECAY_LANES * t
    own = [(lane >= base + h * DECAY_LANES) & (lane < base + (h + 1) * DECAY_LANES) for h in range(2)]
    q = q_ref[...]
    qa = qa_ref[...]
    zero = jnp.zeros_like(q)
    q2[0, :, 0:LANES] = jnp.where(lo, q, zero)
    q2[1, :, 0:LANES] = jnp.where(lo, zero, q)
    for h in range(2):
        q2[h, :, LANES:2 * LANES] = jnp.where(own[h], qa, zero)
    m_sc[...] = jnp.full(m_sc.shape, NEG_INF, F32)
    acc_sc[...] = jnp.zeros(acc_sc.shape, F32)

    def scores(h, ks):
        return lax.dot_general(q2[h], kaug[pl.ds(ks, tk), :], (((1,), (1,)), ((), ())),
                               preferred_element_type=F32)

    def consume(h, s, ks, mask):
        if mask is not None:
            s = jnp.where(mask, s, NEG_INF)
        m_prev = m_sc[h]
        m_new = jnp.maximum(m_prev, jnp.max(s, axis=-1, keepdims=True))
        alpha = jnp.exp(m_prev - m_new)
        p = jnp.exp(s - jnp.concatenate([m_new] * (tk // LANES), axis=1))
        pv = jnp.dot(p.astype(BF16), vaug[pl.ds(ks, tk), :], preferred_element_type=F32)
        acc_sc[h] = jnp.concatenate([alpha, alpha], axis=1) * acc_sc[h] + pv
        m_sc[h] = m_new

    def key_start(j):
        return pl.multiple_of(j * tk, tk)

    def scores_into(buf, j):
        for h in range(2):
            buf[h] = scores(h, key_start(j))

    def consume_from(buf, j, mask):
        for h in range(2):
            consume(h, buf[h], key_start(j), mask)

    rr = lax.broadcasted_iota(jnp.int32, (tq, tk), 0)
    cc = lax.broadcasted_iota(jnp.int32, (tq, tk), 1)
    causal = cc <= rr
    n_full = i - j_start

    scores_into(s_a, j_start)

    def pair(p, carry):
        j = j_start + 2 * p
        scores_into(s_b, j + 1)
        consume_from(s_a, j, None)
        scores_into(s_a, j + 2)
        consume_from(s_b, j + 1, None)
        return carry

    lax.fori_loop(0, n_full // 2, pair, 0)
    odd = lax.rem(n_full, 2) == 1

    @pl.when(odd)
    def _():
        scores_into(s_b, i)
        consume_from(s_a, i - 1, None)
        consume_from(s_b, i, causal)

    @pl.when(jnp.logical_not(odd))
    def _():
        consume_from(s_a, i, causal)

    outs = [acc_sc[h, :, 0:LANES] / acc_sc[h, :, LANES:2 * LANES] for h in range(2)]
    o_ref[...] = jnp.where(lo, outs[0], outs[1]).astype(BF16)


def _fox(j_start, qb, kb, vb, qa, ka):
    tq = TQ_FOX
    nq = SEQ // tq
    n_pairs = N_HEADS_FOX // 2
    qmap = lambda b, t, i, js: (b * nq + i, t)
    kmap = lambda b, t, i, js: (b, t)
    grid_spec = pltpu.PrefetchScalarGridSpec(
        num_scalar_prefetch=1,
        grid=(BATCH, n_pairs, nq),
        in_specs=[pl.BlockSpec((tq, LANES), qmap),
                  pl.BlockSpec((SEQ, LANES), kmap),
                  pl.BlockSpec((SEQ, LANES), kmap),
                  pl.BlockSpec((None, tq, LANES), lambda b, t, i, js: (b, i, 0)),
                  pl.BlockSpec((None, SEQ, LANES), lambda b, t, i, js: (b, 0, 0))],
        out_specs=pl.BlockSpec((tq, LANES), qmap),
        scratch_shapes=[pltpu.VMEM((SEQ, 2 * LANES), BF16),
                        pltpu.VMEM((SEQ, 2 * LANES), BF16),
                        pltpu.VMEM((2, tq, 2 * LANES), BF16),
                        pltpu.VMEM((2, tq, LANES), F32),
                        pltpu.VMEM((2, tq, 2 * LANES), F32),
                        pltpu.VMEM((2, tq, TK_FOX), F32),
                        pltpu.VMEM((2, tq, TK_FOX), F32)])
    return pl.pallas_call(
        _fox_kernel,
        out_shape=jax.ShapeDtypeStruct((N_TOK, W_B), BF16),
        grid_spec=grid_spec,
        compiler_params=_cparams(3),
        name="fox",
    )(j_start, qb, kb, vb, qa, ka)


def _fox_first_tiles(nrm, fb):
    n_tiles = SEQ // TQ_FOX
    nr = nrm.reshape(BATCH, n_tiles, 8, LANES)[:, :, 0, :] * 1.02
    qn = jnp.sqrt(nr[..., 0:N_HEADS_FOX])
    kn = jnp.sqrt(nr[..., N_HEADS_FOX:2 * N_HEADS_FOX])
    f_first = fb[:, 0::2, 0:DECAY_LANES * N_HEADS_FOX:DECAY_LANES]
    f_last = fb[:, 1::2, 0:DECAY_LANES * N_HEADS_FOX:DECAY_LANES]
    kn_prefix = lax.cummax(kn, axis=1)
    upper = qn[:, :, None, :] * kn_prefix[:, None, :, :] + f_first[:, :, None, :] - f_last[:, None, :, :]
    row_max_low = -(qn * kn)[:, :, None, :]
    ii = jnp.arange(n_tiles)[None, :, None, None]
    jj = jnp.arange(n_tiles)[None, None, :, None]
    skip = (upper < row_max_low - PRUNE_MARGIN) & (jj < ii)
    skip = jnp.all(skip.reshape(BATCH, n_tiles, n_tiles, N_HEADS_FOX // 2, 2), axis=-1)
    first = jnp.sum(jnp.cumprod(skip.astype(jnp.int32), axis=2), axis=2)
    return jnp.transpose(first, (0, 2, 1)).reshape(-1).astype(jnp.int32)


def _post_kernel(x_ref, oa_ref, ob_ref, gt_ref, gm_ref, sc_ref, sh_ref, g_ref,
                 wa_ref, wb_ref, wo_ref, wr2_ref, br_ref,
                 x1_ref, xy_ref, rc_ref, cu_ref, hh_prev, lg_prev):
    step = pl.program_id(0)

    @pl.when(step == 0)
    def _():
        hh_prev[...] = jnp.zeros(hh_prev.shape, BF16)
        lg_prev[...] = jnp.zeros(lg_prev.shape, F32)

    @pl.when(step <= N_TOK_TILES)
    def _():
        hh_p = hh_prev[...]
        lg_p = lg_prev[...]
        hh, logits = _post_mix(x_ref, oa_ref, ob_ref, gt_ref, gm_ref, sc_ref, sh_ref, g_ref,
                               wa_ref, wb_ref, wo_ref, wr2_ref, br_ref, x1_ref)
        _post_route(hh_p, lg_p, xy_ref, rc_ref, cu_ref)
        hh_prev[...] = hh
        lg_prev[...] = logits

    @pl.when(step > N_TOK_TILES)
    def _():
        xy_ref[...] = jnp.zeros(xy_ref.shape, BF16)


def _post_mix(x_ref, oa_ref, ob_ref, gt_ref, gm_ref, sc_ref, sh_ref, g_ref,
              wa_ref, wb_ref, wo_ref, wr2_ref, br_ref, x1_ref):
    pa = jnp.dot(oa_ref[...], wa_ref[...], preferred_element_type=F32)
    pb = jnp.dot(ob_ref[...], wb_ref[...], preferred_element_type=F32)
    ga = jax.nn.sigmoid(gt_ref[:, 0:D_MODEL].astype(F32))
    gb = jax.nn.sigmoid(gt_ref[:, D_MODEL:2 * D_MODEL].astype(F32))
    merged = (ga * pa + gb * pb).astype(BF16)
    y = jnp.dot(merged, wo_ref[...], preferred_element_type=F32)
    x1 = x_ref[...] + gm_ref[...] * y
    x1_ref[...] = x1

    rs = lax.rsqrt(jnp.mean(x1 * x1, axis=-1, keepdims=True) + EPS)
    a = g_ref[...] * (1.0 + sc_ref[...])
    h2 = x1 * rs * a + sh_ref[...]

    hh = h2.astype(BF16)
    hl = (h2 - hh.astype(F32)).astype(BF16)
    hi_both = jnp.dot(hh, wr2_ref[...], preferred_element_type=F32)
    logits = (hi_both[:, 0:LANES] + hi_both[:, LANES:2 * LANES]
              + jnp.dot(hl, wr2_ref[:, 0:LANES], preferred_element_type=F32)
              + br_ref[...])
    return hh, logits


def _post_route(hh, logits, xy_ref, rc_ref, cu_ref):
    tm = TM_POST
    lane = lax.broadcasted_iota(jnp.int32, (tm, LANES), 1).astype(F32)
    big = float(LANES)
    gl = jnp.where(lane < N_GROUPS, logits, -jnp.inf)
    gmax = jnp.max(gl, axis=-1, keepdims=True)
    gi = jnp.min(jnp.where(gl == gmax, lane, big), axis=-1, keepdims=True)
    gsum = jnp.sum(jnp.exp(gl - gmax), axis=-1, keepdims=True)
    gp = 1.0 / gsum
    e_lo = N_GROUPS + EXPERTS_PER_GROUP * gi
    el = jnp.where((lane >= e_lo) & (lane < e_lo + EXPERTS_PER_GROUP), logits, -jnp.inf)
    v1 = jnp.max(el, axis=-1, keepdims=True)
    i1 = jnp.min(jnp.where(el == v1, lane, big), axis=-1, keepdims=True)
    el2 = jnp.where(lane == i1, -jnp.inf, el)
    v2 = jnp.max(el2, axis=-1, keepdims=True)
    i2 = jnp.min(jnp.where(el2 == v2, lane, big), axis=-1, keepdims=True)
    e21 = jnp.exp(v2 - v1)
    w1 = gp / (1.0 + e21)
    w2 = gp * e21 / (1.0 + e21)
    e1 = i1 - N_GROUPS
    e2 = i2 - N_GROUPS

    oh = jnp.where((lane == e1) | (lane == e2), 1.0, 0.0)
    cnt_u = jnp.floor((jnp.sum(oh, axis=0, keepdims=True) + (UNIT - 1)) * (1.0 / UNIT))
    r128 = lax.broadcasted_iota(jnp.int32, (LANES, LANES), 0)
    c128 = lax.broadcasted_iota(jnp.int32, (LANES, LANES), 1)
    before_lane = jnp.where(r128 < c128, 1.0, 0.0).astype(BF16)
    loc_u = jnp.dot(jnp.broadcast_to(cnt_u, (8, LANES)).astype(BF16), before_lane,
                    preferred_element_type=F32)
    rr = lax.broadcasted_iota(jnp.int32, (tm, tm), 0)
    cc = lax.broadcasted_iota(jnp.int32, (tm, tm), 1)
    strict = jnp.where(rr > cc, 1.0, 0.0).astype(BF16)
    pos_e = jnp.dot(strict, oh.astype(BF16), preferred_element_type=F32) + loc_u[0:1] * UNIT
    lp1 = jnp.sum(jnp.where(lane == e1, pos_e, 0.0), axis=-1, keepdims=True)
    lp2 = jnp.sum(jnp.where(lane == e2, pos_e, 0.0), axis=-1, keepdims=True)

    eye = rr == cc
    ones8 = jnp.ones((8, tm), BF16)

    def to_row(col):
        hi = jnp.floor(col * (1.0 / 64.0))
        lo_ = col - 64.0 * hi
        d_hi = jnp.where(eye, hi, 0.0).astype(BF16)
        d_lo = jnp.where(eye, lo_, 0.0).astype(BF16)
        row8 = (64.0 * jnp.dot(ones8, d_hi, preferred_element_type=F32)
                + jnp.dot(ones8, d_lo, preferred_element_type=F32))
        return row8[0:1]

    srow = lax.broadcasted_iota(jnp.int32, (XY_ROWS, tm), 0).astype(F32)
    pm1 = jnp.where(srow == to_row(lp1), 1.0, 0.0).astype(BF16)
    pm2 = jnp.where(srow == to_row(lp2), 1.0, 0.0).astype(BF16)
    w1h = w1.astype(BF16).astype(F32)
    w2h = w2.astype(BF16).astype(F32)
    side = jnp.where(lane == W1_LANES[0], w1h, jnp.where(lane == W1_LANES[1], w1 - w1h,
           jnp.where(lane == W2_LANES[0], w2h, jnp.where(lane == W2_LANES[1], w2 - w2h,
           jnp.where(lane == E1_LANE, e1, jnp.where(lane == E2_LANE, e2, 0.0))))))
    tok = jnp.concatenate([hh, side.astype(BF16)], axis=1)
    xy_ref[...] = jnp.dot(pm1 + pm2, tok, preferred_element_type=F32).astype(BF16)

    cu_ref[...] = jnp.broadcast_to(cnt_u, cu_ref.shape)
    rc_ref[...] = jnp.where(lane == 0, lp1, jnp.where(lane == 1, lp2, 0.0))


def _post(x2, oa, ob, gates, gate_m, scale_f, shift_f, g_ffn, wa, wb, wo, wr2, b_r):
    tm = TM_POST
    tpb = SEQ // tm
    n_steps = N_TOK_TILES
    row = lambda i: (jnp.minimum(i, n_steps - 1), 0)
    per_b = lambda i: (jnp.minimum(i, n_steps - 1) // tpb, 0, 0)
    routed = lambda i: (jnp.clip(i - 1, 0, n_steps - 1), 0)
    const = lambda i: (0, 0)
    return pl.pallas_call(
        _post_kernel,
        out_shape=[jax.ShapeDtypeStruct((N_TOK, D_MODEL), F32),
                   jax.ShapeDtypeStruct(((n_steps + PAD_BLOCKS) * XY_ROWS, XY_COLS), BF16),
                   jax.ShapeDtypeStruct((N_TOK, LANES), F32),
                   jax.ShapeDtypeStruct((n_steps * 8, LANES), F32)],
        grid=(n_steps + 1 + PAD_BLOCKS,),
        in_specs=[pl.BlockSpec((tm, D_MODEL), row),
                  pl.BlockSpec((tm, Q_A), row),
                  pl.BlockSpec((tm, W_B), row),
                  pl.BlockSpec((tm, 2 * D_MODEL), row),
                  pl.BlockSpec((None, 1, D_MODEL), per_b),
                  pl.BlockSpec((None, 1, D_MODEL), per_b),
                  pl.BlockSpec((None, 1, D_MODEL), per_b),
                  pl.BlockSpec((1, D_MODEL), const),
                  pl.BlockSpec(wa.shape, const),
                  pl.BlockSpec(wb.shape, const),
                  pl.BlockSpec(wo.shape, const),
                  pl.BlockSpec(wr2.shape, const),
                  pl.BlockSpec((1, LANES), const)],
        out_specs=[pl.BlockSpec((tm, D_MODEL), row),
                   pl.BlockSpec((XY_ROWS, XY_COLS), lambda i: (jnp.maximum(i - 1, 0), 0)),
                   pl.BlockSpec((tm, LANES), routed),
                   pl.BlockSpec((8, LANES), routed)],
        scratch_shapes=[pltpu.VMEM((tm, D_MODEL), BF16),
                        pltpu.VMEM((tm, LANES), F32)],
        compiler_params=_cparams(1),
        name="post",
    )(x2, oa, ob, gates, gate_m, scale_f, shift_f, g_ffn, wa, wb, wo, wr2, b_r)


def _experts_kernel(te_ref, nu_ref, ur_ref, xy_in, wg_ref, wu_ref, wd_ref, xy_out,
                    xbuf, ybuf, wg_s, wu_s, wd_s, gsem, ssem):
    del xy_in
    r = pl.program_id(0)
    last = pl.num_programs(0) - 1
    n_used = nu_ref[0]
    slot = lax.rem(r, 2)

    def unit_row(step, s):
        return pl.multiple_of(ur_ref[step * UNITS_PER_TILE + s], UNIT)

    def start_gathers(step, sl):
        for s in range(UNITS_PER_TILE):
            pltpu.make_async_copy(xy_out.at[pl.ds(unit_row(step, s), UNIT), :],
                                  xbuf.at[sl, pl.ds(s * UNIT, UNIT), :], gsem.at[sl]).start()

    def wait_gathers(sl):
        pltpu.make_async_copy(xy_out.at[pl.ds(0, TM_EXP), :], xbuf.at[sl], gsem.at[sl]).wait()

    def start_scatters(step, sl):
        for s in range(UNITS_PER_TILE):
            pltpu.make_async_copy(ybuf.at[sl, pl.ds(s * UNIT, UNIT), :],
                                  xy_out.at[pl.ds(unit_row(step, s), UNIT), pl.ds(0, D_MODEL)],
                                  ssem.at[sl]).start()

    def wait_scatters(sl):
        pltpu.make_async_copy(ybuf.at[sl], xy_out.at[pl.ds(0, TM_EXP), pl.ds(0, D_MODEL)], ssem.at[sl]).wait()

    @pl.when(r == 0)
    def _():
        start_gathers(0, 0)

    @pl.when(jnp.logical_and(r < n_used, r >= 2))
    def _():
        wait_scatters(slot)

    @pl.when(jnp.logical_and(r < n_used,
                             jnp.logical_or(r == 0, te_ref[r] != te_ref[jnp.maximum(r - 1, 0)])))
    def _():
        wg_s[...] = wg_ref[...].astype(BF16)
        wu_s[...] = wu_ref[...].astype(BF16)
        wd_s[...] = wd_ref[...].astype(BF16)

    @pl.when(r < n_used)
    def _():
        wait_gathers(slot)
        start_gathers(jnp.minimum(r + 1, last), 1 - slot)
        x = xbuf[slot, :, 0:D_MODEL]
        side = xbuf[slot, :, D_MODEL:XY_COLS].astype(F32)
        lane = lax.broadcasted_iota(jnp.int32, side.shape, 1)

        def lanes_sum(a, b):
            return jnp.sum(jnp.where((lane == a) | (lane == b), side, 0.0), axis=-1, keepdims=True)

        is_slot1 = lanes_sum(E1_LANE, E1_LANE) == te_ref[r].astype(F32)
        wrow = jnp.where(is_slot1, lanes_sum(*W1_LANES), lanes_sum(*W2_LANES))
        a = jnp.dot(x, wg_s[...], preferred_element_type=F32)
        u = jnp.dot(x, wu_s[...], preferred_element_type=F32)
        hid = (a * jax.nn.sigmoid(a) * u * wrow).astype(BF16)
        ybuf[slot] = jnp.dot(hid, wd_s[...], preferred_element_type=F32).astype(BF16)
        start_scatters(r, slot)

    @pl.when(r == n_used - 1)
    def _():
        wait_gathers(1 - slot)
        wait_scatters(slot)

        @pl.when(r >= 1)
        def _():
            wait_scatters(1 - slot)


def _experts(tile_expert, n_used, unit_rows, xy, wg, wu, wd):
    wmap = lambda r, te, nv, ur: (te[r], 0, 0)
    grid_spec = pltpu.PrefetchScalarGridSpec(
        num_scalar_prefetch=3,
        grid=(N_EXP_TILES,),
        in_specs=[pl.BlockSpec(memory_space=pl.ANY),
                  pl.BlockSpec((None, D_MODEL, D_FF_EXPERT), wmap),
                  pl.BlockSpec((None, D_MODEL, D_FF_EXPERT), wmap),
                  pl.BlockSpec((None, D_FF_EXPERT, D_MODEL), wmap)],
        out_specs=pl.BlockSpec(memory_space=pl.ANY),
        scratch_shapes=[pltpu.VMEM((2, TM_EXP, XY_COLS), BF16),
                        pltpu.VMEM((2, TM_EXP, D_MODEL), BF16),
                        pltpu.VMEM((D_MODEL, D_FF_EXPERT), BF16),
                        pltpu.VMEM((D_MODEL, D_FF_EXPERT), BF16),
                        pltpu.VMEM((D_FF_EXPERT, D_MODEL), BF16),
                        pltpu.SemaphoreType.DMA((2,)),
                        pltpu.SemaphoreType.DMA((2,))])
    return pl.pallas_call(
        _experts_kernel,
        out_shape=jax.ShapeDtypeStruct(xy.shape, xy.dtype),
        grid_spec=grid_spec,
        input_output_aliases={3: 0},
        compiler_params=_cparams(1),
        name="experts",
    )(tile_expert, n_used, unit_rows, xy, wg, wu, wd)


def _combine_kernel(x1_ref, rc_ref, gf_ref, gfin_ref, y_ref, o_ref):
    lp1 = rc_ref[:, 0:1]
    lp2 = rc_ref[:, 1:2]
    scol = lax.broadcasted_iota(jnp.int32, (TM_ROW, XY_ROWS), 1).astype(F32)
    pick = jnp.where((scol == lp1) | (scol == lp2), 1.0, 0.0).astype(BF16)
    y = jnp.dot(pick, y_ref[...], preferred_element_type=F32)
    xf = x1_ref[...] + gf_ref[...] * y
    rs = lax.rsqrt(jnp.mean(xf * xf, axis=-1, keepdims=True) + EPS)
    o_ref[...] = xf * rs * gfin_ref[...]


def _combine(x1, rcol, gate_f, g_final, xy):
    tm = TM_ROW
    tpb = SEQ // tm
    row = lambda i: (i, 0)
    return pl.pallas_call(
        _combine_kernel,
        out_shape=jax.ShapeDtypeStruct((N_TOK, D_MODEL), F32),
        grid=(N_TOK // tm,),
        in_specs=[pl.BlockSpec((tm, D_MODEL), row),
                  pl.BlockSpec((tm, LANES), row),
                  pl.BlockSpec((None, 1, D_MODEL), lambda i: (i // tpb, 0, 0)),
                  pl.BlockSpec((1, D_MODEL), lambda i: (0, 0)),
                  pl.BlockSpec((XY_ROWS, D_MODEL), row)],
        out_specs=pl.BlockSpec((tm, D_MODEL), row),
        compiler_params=_cparams(1),
        name="combine",
    )(x1, rcol, gate_f, g_final, xy)


def _t5_bucket_np():
    qi = np.arange(BLOCK)[:, None]
    kj = np.arange(2 * BLOCK)[None, :]
    dist = qi - kj + BLOCK
    n = np.maximum(dist, 0)
    max_exact = NUM_BUCKETS // 2
    nf = np.maximum(n, 1).astype(np.float32)
    large = max_exact + (np.log(nf / np.float32(max_exact)) / np.float32(math.log(MAX_DISTANCE / max_exact))
                         * np.float32(NUM_BUCKETS - max_exact)).astype(np.int32)
    large = np.minimum(large, NUM_BUCKETS - 1)
    bucket = np.where(n < max_exact, n, large)
    band = (dist >= 0) & (dist < WINDOW)
    return bucket.astype(np.int32), band


def kernel(x, c, w_ada, b_ada, g_norm_mix, g_norm_ffn, w_in, sinks, b_forget, w_proj_swa, w_proj_fox,
           w_out, rel_bias_table, w_router_group, b_router_group, w_router_expert, b_router_expert,
           w_gate_exp, w_up_exp, w_down_exp, g_final):
    l = 0
    x2 = x.reshape(N_TOK, D_MODEL)

    c16 = jnp.concatenate([c, jnp.zeros_like(c)], axis=0)
    mod = _ada(c16, w_ada[l], b_ada[l][None, :])[:BATCH]
    shift_m, scale_m, gate_m, shift_f, scale_f, gate_f = [
        m.reshape(BATCH, 1, D_MODEL) for m in jnp.split(mod, 6, axis=-1)]

    w = w_in[l]
    o_ka, o_va, o_qb = Q_A, Q_A + KV_A, Q_A + 2 * KV_A
    o_kb, o_vb, o_f = o_qb + W_B, o_qb + 2 * W_B, o_qb + 3 * W_B
    o_g = o_f + N_HEADS_FOX

    def dup(cols):
        heads = [cols[:, h * HEAD_DIM:(h + 1) * HEAD_DIM] for h in range(N_KV_HEADS_SWA)]
        return jnp.concatenate([hd for hd in heads for _ in range(2)], axis=1)

    w_main = jnp.concatenate([w[:, :Q_A], dup(w[:, o_ka:o_va]), dup(w[:, o_va:o_qb]),
                              w[:, o_qb:o_f]], axis=1).astype(BF16)
    w_f = jnp.pad(w[:, o_f:o_g], ((0, 0), (0, LANES - N_HEADS_FOX))).astype(BF16)
    w_g = w[:, o_g:].astype(BF16)
    qa, kdup, vdup, qb, kb, vb, f_pad, gates, nrm = _inproj(
        x2, scale_m, shift_m, g_norm_mix[l][None, :], w_main, w_f, w_g)

    b_pad = jnp.pad(b_forget[l], (0, LANES - N_HEADS_FOX))[None, :]
    lanes = np.arange(LANES)
    jmod = jnp.asarray(np.where(lanes < DECAY_LANES * N_HEADS_FOX, lanes % DECAY_LANES, 7)[None, :].astype(np.int32))
    dq, dk, fb = _cum(f_pad, b_pad, jmod)

    bucket, band = _t5_bucket_np()
    onehot = jnp.asarray(bucket[None] == np.arange(NUM_BUCKETS)[:, None, None], dtype=F32)
    bias = jnp.einsum("bh,bqk->hqk", rel_bias_table.astype(F32), onehot, precision=HIGHEST)
    bias = jnp.where(band[None], bias, NEG_INF)
    first = np.arange(2 * BLOCK)[None, None, :] < BLOCK
    bias = jnp.stack([jnp.where(first, NEG_INF, bias), bias]).reshape(2, N_KV_HEADS_SWA, -1, 2 * BLOCK)
    o_a = _swa(sinks[l].astype(F32), qa, kdup, vdup, bias)

    o_b = _fox(_fox_first_tiles(nrm, fb), qb, kb, vb, dq, dk)

    w_r = jnp.concatenate([w_router_group[l]] + [w_router_expert[l][g] for g in range(N_GROUPS)], axis=1)
    w_r = jnp.pad(w_r, ((0, 0), (0, LANES - w_r.shape[1])))
    wr_hi = w_r.astype(BF16)
    wr_lo = (w_r - wr_hi.astype(F32)).astype(BF16)
    wr2 = jnp.concatenate([wr_hi, wr_lo], axis=1)
    b_r = jnp.concatenate([b_router_group[l], b_router_expert[l].reshape(-1)])
    b_r = jnp.pad(b_r, (0, LANES - b_r.shape[0]))[None, :]
    x1, xy, rcol, cu = _post(x2, o_a, o_b, gates, gate_m, scale_f, shift_f, g_norm_ffn[l][None, :],
                             w_proj_swa[l].astype(BF16), w_proj_fox[l].astype(BF16), w_out[l].astype(BF16),
                             wr2, b_r)

    i32 = jnp.int32
    n_tok_tiles = N_TOK // TM_POST
    cu = cu.reshape(n_tok_tiles, 8, LANES)[:, 0, :N_EXPERTS].astype(i32)
    loc_u = jnp.cumsum(cu, axis=1) - cu
    cend = jnp.cumsum(cu, axis=0)
    cstart = cend - cu
    tot_u = cend[-1]
    tiles_e = (tot_u + UNITS_PER_TILE - 1) // UNITS_PER_TILE
    tile_end = jnp.cumsum(tiles_e)
    tile_start = tile_end - tiles_e
    r = jnp.arange(N_EXP_TILES, dtype=i32)
    tile_expert = jnp.minimum(jnp.sum((tile_end[None, :] <= r[:, None]).astype(i32), axis=1), N_EXPERTS - 1)
    sel_e = tile_expert[:, None] == jnp.arange(N_EXPERTS, dtype=i32)[None, :]
    tw = r - jnp.sum(jnp.where(sel_e, tile_start[None, :], 0), axis=1)
    tot_r = jnp.sum(jnp.where(sel_e, tot_u[None, :], 0), axis=1)
    n_used = tile_end[-1:].astype(i32)
    q = tw[:, None] * UNITS_PER_TILE + jnp.arange(UNITS_PER_TILE, dtype=i32)[None, :]

    def of_expert(tab):
        return jnp.sum(jnp.where(sel_e[:, None, :], tab[None, :, :], 0), axis=2)

    cend_r, cstart_r, loc_r = of_expert(cend), of_expert(cstart), of_expert(loc_u)
    src_tile = jnp.minimum(jnp.sum((cend_r[:, None, :] <= q[:, :, None]).astype(i32), axis=2), n_tok_tiles - 1)
    sel_t = src_tile[:, :, None] == jnp.arange(n_tok_tiles, dtype=i32)[None, None, :]
    k = (q - jnp.sum(jnp.where(sel_t, cstart_r[:, None, :], 0), axis=2)
         + jnp.sum(jnp.where(sel_t, loc_r[:, None, :], 0), axis=2))
    real_rows = src_tile * XY_ROWS + k * UNIT
    pad_rows = PAD_BASE_ROW + (tile_expert[:, None] * PAD_UNITS_PER_EXPERT + (q - tot_r[:, None])) * UNIT
    idle_row = PAD_BASE_ROW + N_EXPERTS * PAD_UNITS_PER_EXPERT * UNIT
    unit_rows = jnp.where(q < tot_r[:, None], real_rows, pad_rows)
    unit_rows = jnp.where((r < n_used)[:, None], unit_rows, idle_row).reshape(-1).astype(i32)

    xy = _experts(tile_expert.astype(i32), n_used, unit_rows, xy,
                  w_gate_exp[l].reshape(N_EXPERTS, D_MODEL, D_FF_EXPERT),
                  w_up_exp[l].reshape(N_EXPERTS, D_MODEL, D_FF_EXPERT),
                  w_down_exp[l].reshape(N_EXPERTS, D_FF_EXPERT, D_MODEL))
    out = _combine(x1, rcol, gate_f, g_final[None, :], xy)
    return out.reshape(BATCH, SEQ, D_MODEL)
```

```python
import functools
import math

import numpy as np
import jax
import jax.numpy as jnp
from jax import lax
from jax.experimental import pallas as pl
from jax.experimental.pallas import tpu as pltpu

F32 = jnp.float32
BF16 = jnp.bfloat16
HIGHEST = lax.Precision.HIGHEST

D_MODEL = 1024
BATCH = 8
SEQ = 4096
N_TOK = BATCH * SEQ
N_HEADS_SWA = 8
N_KV_HEADS_SWA = 2
N_HEADS_FOX = 8
HEAD_DIM = 64
WINDOW = 128
BLOCK = 128
NUM_BUCKETS = 32
MAX_DISTANCE = 128
N_GROUPS = 4
EXPERTS_PER_GROUP = 8
N_EXPERTS = N_GROUPS * EXPERTS_PER_GROUP
D_FF_EXPERT = 256
EPS = 1e-6
NEG_INF = -1e30

Q_A = N_HEADS_SWA * HEAD_DIM
KV_A = N_KV_HEADS_SWA * HEAD_DIM
W_B = N_HEADS_FOX * HEAD_DIM
LANES = 128
QK_SCALE = HEAD_DIM ** -0.5

TM_IN = 512
TM_POST = 512
TQ_FOX = 512
TK_FOX = TQ_FOX
SWA_BLOCKS = 4
TM_EXP = 512
TM_ROW = 512
UNIT = 16
XY_UNITS = 2 * TM_POST // UNIT + N_EXPERTS
XY_ROWS = XY_UNITS * UNIT
XY_COLS = D_MODEL + LANES
UNITS_PER_TILE = TM_EXP // UNIT
N_TOK_TILES = N_TOK // TM_POST
N_EXP_TILES = N_TOK_TILES * XY_UNITS // UNITS_PER_TILE + N_EXPERTS
PAD_UNITS_PER_EXPERT = UNITS_PER_TILE - 1
PAD_BLOCKS = -(-(N_EXPERTS * PAD_UNITS_PER_EXPERT * UNIT) // XY_ROWS)
PAD_BASE_ROW = N_TOK_TILES * XY_ROWS
W1_LANES, W2_LANES, E1_LANE, E2_LANE = (4, 6), (5, 7), 8, 9
VMEM_LIMIT = 56 * 1024 * 1024

DECAY_LANES = 6
PRUNE_MARGIN = 110.0


def _cparams(n_axes):
    return pltpu.CompilerParams(dimension_semantics=("arbitrary",) * n_axes,
                                vmem_limit_bytes=VMEM_LIMIT)


def _ada_kernel(c_ref, w_ref, b_ref, o_ref):
    c = c_ref[...]
    ca = c * jax.nn.sigmoid(c)
    o_ref[...] = jnp.dot(ca.astype(BF16), w_ref[...].astype(BF16),
                         preferred_element_type=F32) + b_ref[...]


def _ada(c16, w_ada, b_ada):
    n_out = w_ada.shape[1]
    blk = 1024
    return pl.pallas_call(
        _ada_kernel,
        out_shape=jax.ShapeDtypeStruct((16, n_out), F32),
        grid=(n_out // blk,),
        in_specs=[pl.BlockSpec((16, D_MODEL), lambda j: (0, 0)),
                  pl.BlockSpec((D_MODEL, blk), lambda j: (0, j)),
                  pl.BlockSpec((1, blk), lambda j: (0, j))],
        out_specs=pl.BlockSpec((16, blk), lambda j: (0, j)),
        compiler_params=_cparams(1),
        name="ada",
    )(c16, w_ada, b_ada)


def _inproj_kernel(x_ref, sc_ref, sh_ref, g_ref, wm_ref, wf_ref, wg_ref, ind_ref,
                   qa_ref, kd_ref, vd_ref, qb_ref, kb_ref, vb_ref, f_ref, gt_ref, nrm_ref):
    x = x_ref[...]
    rs = lax.rsqrt(jnp.mean(x * x, axis=-1, keepdims=True) + EPS)
    a = g_ref[...] * (1.0 + sc_ref[...])
    h = (x * rs * a + sh_ref[...]).astype(BF16)

    def mm(w):
        return jnp.dot(h, w, preferred_element_type=F32)

    qa_ref[...] = (mm(wm_ref[:, 0:512]) * QK_SCALE).astype(BF16)
    kd_ref[...] = mm(wm_ref[:, 512:768]).astype(BF16)
    vd_ref[...] = mm(wm_ref[:, 768:1024]).astype(BF16)
    qb = (mm(wm_ref[:, 1024:1536]) * QK_SCALE).astype(BF16)
    kb = mm(wm_ref[:, 1536:2048]).astype(BF16)
    qb_ref[...] = qb
    kb_ref[...] = kb
    vb_ref[...] = mm(wm_ref[:, 2048:2560]).astype(BF16)
    sq = jnp.concatenate([qb, kb], axis=1).astype(F32)
    seg = jnp.dot((sq * sq).astype(BF16), ind_ref[...], preferred_element_type=F32)
    nrm_ref[...] = jnp.broadcast_to(jnp.max(seg, axis=0, keepdims=True), nrm_ref.shape)
    f_ref[...] = mm(wf_ref[...])
    gt_ref[...] = mm(wg_ref[...]).astype(BF16)


def _inproj(x2, scale_m, shift_m, g_mix, w_main, w_f, w_g):
    tm = TM_IN
    tpb = SEQ // tm
    row = lambda i: (i, 0)
    per_b = lambda i: (i // tpb, 0, 0)
    const = lambda i: (0, 0)
    outs = [(Q_A, BF16), (2 * KV_A, BF16), (2 * KV_A, BF16), (W_B, BF16), (W_B, BF16), (W_B, BF16),
            (LANES, F32), (2 * D_MODEL, BF16)]
    ind_np = np.zeros((2 * W_B, LANES), np.float32)
    ind_np[np.arange(2 * W_B), np.arange(2 * W_B) // HEAD_DIM] = 1.0
    ind = jnp.asarray(ind_np, dtype=BF16)
    n_steps = N_TOK // tm
    return pl.pallas_call(
        _inproj_kernel,
        out_shape=[jax.ShapeDtypeStruct((N_TOK, w), dt) for w, dt in outs]
        + [jax.ShapeDtypeStruct((n_steps * 8, LANES), F32)],
        grid=(n_steps,),
        in_specs=[pl.BlockSpec((tm, D_MODEL), row),
                  pl.BlockSpec((None, 1, D_MODEL), per_b),
                  pl.BlockSpec((None, 1, D_MODEL), per_b),
                  pl.BlockSpec((1, D_MODEL), const),
                  pl.BlockSpec(w_main.shape, const),
                  pl.BlockSpec(w_f.shape, const),
                  pl.BlockSpec(w_g.shape, const),
                  pl.BlockSpec(ind.shape, const)],
        out_specs=[pl.BlockSpec((tm, w), row) for w, _ in outs] + [pl.BlockSpec((8, LANES), row)],
        compiler_params=_cparams(1),
        name="inproj",
    )(x2, scale_m, shift_m, g_mix, w_main, w_f, w_g, ind)


def _log_sigmoid(x):
    return jnp.minimum(x, 0.0) - jnp.log1p(jnp.exp(-jnp.abs(x)))


def _cum_kernel(f_ref, b_ref, jm_ref, qa_ref, ka_ref, fb_ref):
    r = lax.broadcasted_iota(jnp.int32, (LANES, LANES), 0)
    c = lax.broadcasted_iota(jnp.int32, (LANES, LANES), 1)
    expand = ((c >= DECAY_LANES * r) & (c < DECAY_LANES * r + DECAY_LANES)
              & (r < N_HEADS_FOX)).astype(F32)
    tril = (r >= c).astype(F32)
    lf = _log_sigmoid(f_ref[...] + b_ref[...])
    lfe = jnp.dot(lf, expand, precision=HIGHEST, preferred_element_type=F32)
    jm = jm_ref[...]
    carry = jnp.zeros((1, LANES), F32)
    for blk in range(SEQ // LANES):
        rows = slice(blk * LANES, (blk + 1) * LANES)
        cb = jnp.dot(tril, lfe[rows], precision=HIGHEST, preferred_element_type=F32) + carry
        carry = cb[LANES - 1:LANES]
        hi = cb.astype(BF16).astype(F32)
        r1 = cb - hi
        mid = r1.astype(BF16).astype(F32)
        lo = (r1 - mid).astype(BF16).astype(F32)
        one = jnp.ones_like(cb)
        zero = jnp.zeros_like(cb)
        qa = jnp.where(jm == 0, hi, jnp.where(jm == 1, mid, jnp.where(jm == 2, lo,
                       jnp.where(jm < DECAY_LANES, one, zero))))
        ka = jnp.where(jm == 3, -hi, jnp.where(jm == 4, -mid, jnp.where(jm == 5, -lo,
                       jnp.where(jm < 3, one, zero))))
        qa_ref[rows, :] = qa.astype(BF16)
        ka_ref[rows, :] = ka.astype(BF16)
        blocks_per_tile = TQ_FOX // LANES
        tile = blk // blocks_per_tile
        if blk % blocks_per_tile == 0:
            fb_ref[2 * tile:2 * tile + 1, :] = cb[0:1]
        if blk % blocks_per_tile == blocks_per_tile - 1:
            fb_ref[2 * tile + 1:2 * tile + 2, :] = carry


def _cum(f_pad, b_pad, jmod):
    n_tiles = SEQ // TQ_FOX
    return pl.pallas_call(
        _cum_kernel,
        out_shape=[jax.ShapeDtypeStruct((BATCH, SEQ, LANES), BF16)] * 2
        + [jax.ShapeDtypeStruct((BATCH, 2 * n_tiles, LANES), F32)],
        grid=(BATCH,),
        in_specs=[pl.BlockSpec((SEQ, LANES), lambda b: (b, 0)),
                  pl.BlockSpec((1, LANES), lambda b: (0, 0)),
                  pl.BlockSpec((1, LANES), lambda b: (0, 0))],
        out_specs=[pl.BlockSpec((None, SEQ, LANES), lambda b: (b, 0, 0))] * 2
        + [pl.BlockSpec((None, 2 * n_tiles, LANES), lambda b: (b, 0, 0))],
        compiler_params=_cparams(1),
        name="cum",
    )(f_pad, b_pad, jmod)


def _swa_block(sink_cols, q, kk, vv, bias_ref, lo):
    tiles = []
    for g in range(N_KV_HEADS_SWA):
        parts = []
        for t in range(2):
            qt = q[:, (2 * g + t) * LANES:(2 * g + t + 1) * LANES]
            zero = jnp.zeros_like(qt)
            parts.append(jnp.where(lo, qt, zero))
            parts.append(jnp.where(lo, zero, qt))
        q4 = jnp.concatenate(parts, axis=0)
        s = lax.dot_general(q4, kk[:, g * LANES:(g + 1) * LANES], (((1,), (1,)), ((), ())),
                            preferred_element_type=F32)
        s = s + bias_ref[g]
        sink = sink_cols[g]
        m = jnp.maximum(jnp.max(s, axis=-1, keepdims=True), sink)
        p = jnp.exp(s - m)
        den = jnp.sum(p, axis=-1, keepdims=True) + jnp.exp(sink - m)
        o = jnp.dot(p.astype(BF16), vv[:, g * LANES:(g + 1) * LANES],
                    preferred_element_type=F32) / den
        tiles.append(jnp.where(lo, o[0:BLOCK], o[BLOCK:2 * BLOCK]))
        tiles.append(jnp.where(lo, o[2 * BLOCK:3 * BLOCK], o[3 * BLOCK:4 * BLOCK]))
    return tiles


def _swa_kernel(sink_ref, q_ref, kc_ref, kp_ref, vc_ref, vp_ref, bias_first_ref, bias_ref, o_ref):
    lane = lax.broadcasted_iota(jnp.int32, (BLOCK, LANES), 1)
    lo = lane < HEAD_DIM
    grp = N_HEADS_SWA // N_KV_HEADS_SWA
    row = lax.broadcasted_iota(jnp.int32, (grp * BLOCK, 1), 0)
    sink_cols = []
    for g in range(N_KV_HEADS_SWA):
        col = jnp.full((grp * BLOCK, 1), sink_ref[g * grp + grp - 1], F32)
        for hh in range(grp - 2, -1, -1):
            col = jnp.where(row < (hh + 1) * BLOCK, sink_ref[g * grp + hh], col)
        sink_cols.append(col)
    for blk in range(SWA_BLOCKS):
        rows = slice(blk * BLOCK, (blk + 1) * BLOCK)
        if blk == 0:
            kk = jnp.concatenate([kp_ref[...], kc_ref[rows, :]], axis=0)
            vv = jnp.concatenate([vp_ref[...], vc_ref[rows, :]], axis=0)
            bias = bias_first_ref
        else:
            prev_rows = slice((blk - 1) * BLOCK, (blk + 1) * BLOCK)
            kk = kc_ref[prev_rows, :]
            vv = vc_ref[prev_rows, :]
            bias = bias_ref
        tiles = _swa_block(sink_cols, q_ref[rows, :], kk, vv, bias, lo)
        for c, tile in enumerate(tiles):
            o_ref[rows, c * LANES:(c + 1) * LANES] = tile.astype(BF16)


def _swa(sinks, qa, kdup, vdup, bias):
    nb = SEQ // BLOCK
    ns = nb // SWA_BLOCKS
    cur = lambda b, i, s: (b * ns + i, 0)
    prev = lambda b, i, s: (b * nb + jnp.maximum(SWA_BLOCKS * i - 1, 0), 0)
    grid_spec = pltpu.PrefetchScalarGridSpec(
        num_scalar_prefetch=1,
        grid=(BATCH, ns),
        in_specs=[pl.BlockSpec((SWA_BLOCKS * BLOCK, Q_A), cur),
                  pl.BlockSpec((SWA_BLOCKS * BLOCK, 2 * KV_A), cur),
                  pl.BlockSpec((BLOCK, 2 * KV_A), prev),
                  pl.BlockSpec((SWA_BLOCKS * BLOCK, 2 * KV_A), cur),
                  pl.BlockSpec((BLOCK, 2 * KV_A), prev),
                  pl.BlockSpec((None,) + bias.shape[1:], lambda b, i, s: (jnp.minimum(i, 1), 0, 0, 0)),
                  pl.BlockSpec((None,) + bias.shape[1:], lambda b, i, s: (1, 0, 0, 0))],
        out_specs=pl.BlockSpec((SWA_BLOCKS * BLOCK, Q_A), cur))
    return pl.pallas_call(
        _swa_kernel,
        out_shape=jax.ShapeDtypeStruct((N_TOK, Q_A), BF16),
        grid_spec=grid_spec,
        compiler_params=_cparams(2),
        name="swa",
    )(sinks, qa, kdup, kdup, vdup, vdup, bias, bias)


def _fox_kernel(js_ref, q_ref, k_ref, v_ref, qa_ref, ka_ref, o_ref,
                kaug, vaug, q2, m_sc, acc_sc, s_a, s_b):
    tq, tk = TQ_FOX, TK_FOX
    b = pl.program_id(0)
    t = pl.program_id(1)
    i = pl.program_id(2)
    j_start = js_ref[(b * pl.num_programs(1) + t) * pl.num_programs(2) + i]

    @pl.when(i == 0)
    def _():
        kaug[:, 0:LANES] = k_ref[...]
        kaug[:, LANES:2 * LANES] = ka_ref[...]
        vaug[:, 0:LANES] = v_ref[...]
        vaug[:, LANES:2 * LANES] = jnp.ones((SEQ, LANES), BF16)

    lane = lax.broadcasted_iota(jnp.int32, (tq, LANES), 1)
    lo = lane < HEAD_DIM
    base = 2 * DECAY_LANES * t
    own = [(lane >= base + h * DECAY_LANES) & (lane < base + (h + 1) * DECAY_LANES) for h in range(2)]
    q = q_ref[...]
    qa = qa_ref[...]
    zero = jnp.zeros_like(q)
    q2[0, :, 0:LANES] = jnp.where(lo, q, zero)
    q2[1, :, 0:LANES] = jnp.where(lo, zero, q)
    for h in range(2):
        q2[h, :, LANES:2 * LANES] = jnp.where(own[h], qa, zero)
    m_sc[...] = jnp.full(m_sc.shape, NEG_INF, F32)
    acc_sc[...] = jnp.zeros(acc_sc.shape, F32)

    def scores(h, ks):
        return lax.dot_general(q2[h], kaug[pl.ds(ks, tk), :], (((1,), (1,)), ((), ())),
                               preferred_element_type=F32)

    def consume(h, s, ks, mask):
        if mask is not None:
            s = jnp.where(mask, s, NEG_INF)
        m_prev = m_sc[h]
        m_new = jnp.maximum(m_prev, jnp.max(s, axis=-1, keepdims=True))
        alpha = jnp.exp(m_prev - m_new)
        p = jnp.exp(s - jnp.concatenate([m_new] * (tk // LANES), axis=1))
        pv = jnp.dot(p.astype(BF16), vaug[pl.ds(ks, tk), :], preferred_element_type=F32)
        acc_sc[h] = jnp.concatenate([alpha, alpha], axis=1) * acc_sc[h] + pv
        m_sc[h] = m_new

    def key_start(j):
        return pl.multiple_of(j * tk, tk)

    def scores_into(buf, j):
        for h in range(2):
            buf[h] = scores(h, key_start(j))

    def consume_from(buf, j, mask):
        for h in range(2):
            consume(h, buf[h], key_start(j), mask)

    rr = lax.broadcasted_iota(jnp.int32, (tq, tk), 0)
    cc = lax.broadcasted_iota(jnp.int32, (tq, tk), 1)
    causal = cc <= rr
    n_full = i - j_start

    scores_into(s_a, j_start)

    def pair(p, carry):
        j = j_start + 2 * p
        scores_into(s_b, j + 1)
        consume_from(s_a, j, None)
        scores_into(s_a, j + 2)
        consume_from(s_b, j + 1, None)
        return carry

    lax.fori_loop(0, n_full // 2, pair, 0)
    odd = lax.rem(n_full, 2) == 1

    @pl.when(odd)
    def _():
        scores_into(s_b, i)
        consume_from(s_a, i - 1, None)
        consume_from(s_b, i, causal)

    @pl.when(jnp.logical_not(odd))
    def _():
        consume_from(s_a, i, causal)

    outs = [acc_sc[h, :, 0:LANES] / acc_sc[h, :, LANES:2 * LANES] for h in range(2)]
    o_ref[...] = jnp.where(lo, outs[0], outs[1]).astype(BF16)


def _fox(j_start, qb, kb, vb, qa, ka):
    tq = TQ_FOX
    nq = SEQ // tq
    n_pairs = N_HEADS_FOX // 2
    qmap = lambda b, t, i, js: (b * nq + i, t)
    kmap = lambda b, t, i, js: (b, t)
    grid_spec = pltpu.PrefetchScalarGridSpec(
        num_scalar_prefetch=1,
        grid=(BATCH, n_pairs, nq),
        in_specs=[pl.BlockSpec((tq, LANES), qmap),
                  pl.BlockSpec((SEQ, LANES), kmap),
                  pl.BlockSpec((SEQ, LANES), kmap),
                  pl.BlockSpec((None, tq, LANES), lambda b, t, i, js: (b, i, 0)),
                  pl.BlockSpec((None, SEQ, LANES), lambda b, t, i, js: (b, 0, 0))],
        out_specs=pl.BlockSpec((tq, LANES), qmap),
        scratch_shapes=[pltpu.VMEM((SEQ, 2 * LANES), BF16),
                        pltpu.VMEM((SEQ, 2 * LANES), BF16),
                        pltpu.VMEM((2, tq, 2 * LANES), BF16),
                        pltpu.VMEM((2, tq, LANES), F32),
                        pltpu.VMEM((2, tq, 2 * LANES), F32),
                        pltpu.VMEM((2, tq, TK_FOX), F32),
                        pltpu.VMEM((2, tq, TK_FOX), F32)])
    return pl.pallas_call(
        _fox_kernel,
        out_shape=jax.ShapeDtypeStruct((N_TOK, W_B), BF16),
        grid_spec=grid_spec,
        compiler_params=_cparams(3),
        name="fox",
    )(j_start, qb, kb, vb, qa, ka)


def _fox_flat_kernel(js_ref, ni_ref, q_ref, k_ref, v_ref, qa_ref, ka_ref, o_ref,
                     kaug, vaug, q2, m_sc, acc_sc, s0, s1, mask_tbl):
    tq, tk = TQ_FOX, TK_FOX
    nq = SEQ // tq
    b = pl.program_id(0)
    t = pl.program_id(1)
    pair = b * pl.num_programs(1) + t
    first_tile = lambda i: js_ref[pair * nq + i]

    kaug[:, 0:LANES] = k_ref[...]
    kaug[:, LANES:2 * LANES] = ka_ref[...]
    vaug[:, 0:LANES] = v_ref[...]
    vaug[:, LANES:2 * LANES] = jnp.ones((SEQ, LANES), BF16)

    lane = lax.broadcasted_iota(jnp.int32, (tq, LANES), 1)
    lo = lane < HEAD_DIM
    base = 2 * DECAY_LANES * t
    own = [(lane >= base + h * DECAY_LANES) & (lane < base + (h + 1) * DECAY_LANES) for h in range(2)]
    for it in range(nq):
        rows = slice(it * tq, (it + 1) * tq)
        q = q_ref[rows, :]
        qa = qa_ref[rows, :]
        zero = jnp.zeros_like(q)
        q2[0, rows, 0:LANES] = jnp.where(lo, q, zero)
        q2[1, rows, 0:LANES] = jnp.where(lo, zero, q)
        for h in range(2):
            q2[h, rows, LANES:2 * LANES] = jnp.where(own[h], qa, zero)
    rr = lax.broadcasted_iota(jnp.int32, (tq, tk), 0)
    cc = lax.broadcasted_iota(jnp.int32, (tq, tk), 1)
    mask_tbl[0] = jnp.zeros((tq, tk), F32)
    mask_tbl[1] = jnp.where(cc <= rr, 0.0, NEG_INF)
    m_sc[...] = jnp.full(m_sc.shape, NEG_INF, F32)
    acc_sc[...] = jnp.zeros(acc_sc.shape, F32)

    def scores(h, i, j):
        return lax.dot_general(q2[h, pl.ds(pl.multiple_of(i * tq, tq), tq), :],
                               kaug[pl.ds(pl.multiple_of(j * tk, tk), tk), :],
                               (((1,), (1,)), ((), ())), preferred_element_type=F32)

    def consume(h, sbuf, i, j):
        s = sbuf[...] + mask_tbl[(i == j).astype(jnp.int32)]
        m_prev = jnp.where(j == first_tile(i), NEG_INF, m_sc[h])
        m_new = jnp.maximum(m_prev, jnp.max(s, axis=-1, keepdims=True))
        alpha = jnp.exp(m_prev - m_new)
        p = jnp.exp(s - jnp.concatenate([m_new] * (tk // LANES), axis=1))
        pv = jnp.dot(p.astype(BF16), vaug[pl.ds(pl.multiple_of(j * tk, tk), tk), :],
                     preferred_element_type=F32)
        acc_sc[h] = jnp.concatenate([alpha, alpha], axis=1) * acc_sc[h] + pv
        m_sc[h] = m_new

    def body(n, carry):
        i, j = carry
        wrap = j == i
        i_next = jnp.minimum(jnp.where(wrap, i + 1, i), nq - 1)
        j_next = jnp.where(wrap, first_tile(i_next), j + 1)
        s1[...] = scores(1, i, j)
        consume(0, s0, i, j)
        s0[...] = scores(0, i_next, j_next)
        consume(1, s1, i, j)

        @pl.when(wrap)
        def _():
            outs = [acc_sc[h, :, 0:LANES] / acc_sc[h, :, LANES:2 * LANES] for h in range(2)]
            o_ref[pl.ds(pl.multiple_of(i * tq, tq), tq), :] = jnp.where(lo, outs[0], outs[1]).astype(BF16)

        return i_next, j_next

    s0[...] = scores(0, 0, 0)
    lax.fori_loop(0, ni_ref[pair], body, (jnp.int32(0), jnp.int32(0)))


def _fox_flat(j_start, n_items, qb, kb, vb, qa, ka):
    tq = TQ_FOX
    n_pairs = N_HEADS_FOX // 2
    kmap = lambda b, t, js, ni: (b, t)
    bmap = lambda b, t, js, ni: (b, 0, 0)
    grid_spec = pltpu.PrefetchScalarGridSpec(
        num_scalar_prefetch=2,
        grid=(BATCH, n_pairs),
        in_specs=[pl.BlockSpec((SEQ, LANES), kmap),
                  pl.BlockSpec((SEQ, LANES), kmap),
                  pl.BlockSpec((SEQ, LANES), kmap),
                  pl.BlockSpec((None, SEQ, LANES), bmap),
                  pl.BlockSpec((None, SEQ, LANES), bmap)],
        out_specs=pl.BlockSpec((SEQ, LANES), kmap),
        scratch_shapes=[pltpu.VMEM((SEQ, 2 * LANES), BF16),
                        pltpu.VMEM((SEQ, 2 * LANES), BF16),
                        pltpu.VMEM((2, SEQ, 2 * LANES), BF16),
                        pltpu.VMEM((2, tq, LANES), F32),
                        pltpu.VMEM((2, tq, 2 * LANES), F32),
                        pltpu.VMEM((tq, TK_FOX), F32),
                        pltpu.VMEM((tq, TK_FOX), F32),
                        pltpu.VMEM((2, tq, TK_FOX), F32)])
    return pl.pallas_call(
        _fox_flat_kernel,
        out_shape=jax.ShapeDtypeStruct((N_TOK, W_B), BF16),
        grid_spec=grid_spec,
        compiler_params=_cparams(2),
        name="fox",
    )(j_start, n_items, qb, kb, vb, qa, ka)


def _fox_first_tiles(nrm, fb):
    n_tiles = SEQ // TQ_FOX
    nr = nrm.reshape(BATCH, n_tiles, 8, LANES)[:, :, 0, :] * 1.02
    qn = jnp.sqrt(nr[..., 0:N_HEADS_FOX])
    kn = jnp.sqrt(nr[..., N_HEADS_FOX:2 * N_HEADS_FOX])
    f_first = fb[:, 0::2, 0:DECAY_LANES * N_HEADS_FOX:DECAY_LANES]
    f_last = fb[:, 1::2, 0:DECAY_LANES * N_HEADS_FOX:DECAY_LANES]
    kn_prefix = lax.cummax(kn, axis=1)
    upper = qn[:, :, None, :] * kn_prefix[:, None, :, :] + f_first[:, :, None, :] - f_last[:, None, :, :]
    row_max_low = -(qn * kn)[:, :, None, :]
    ii = jnp.arange(n_tiles)[None, :, None, None]
    jj = jnp.arange(n_tiles)[None, None, :, None]
    skip = (upper < row_max_low - PRUNE_MARGIN) & (jj < ii)
    skip = jnp.all(skip.reshape(BATCH, n_tiles, n_tiles, N_HEADS_FOX // 2, 2), axis=-1)
    first = jnp.sum(jnp.cumprod(skip.astype(jnp.int32), axis=2), axis=2)
    return jnp.transpose(first, (0, 2, 1)).reshape(-1).astype(jnp.int32)


def _post_kernel(x_ref, oa_ref, ob_ref, gt_ref, gm_ref, sc_ref, sh_ref, g_ref,
                 wa_ref, wb_ref, wo_ref, wr2_ref, br_ref,
                 x1_ref, xy_ref, rc_ref, cu_ref, hh_prev, lg_prev):
    step = pl.program_id(0)

    @pl.when(step == 0)
    def _():
        hh_prev[...] = jnp.zeros(hh_prev.shape, BF16)
        lg_prev[...] = jnp.zeros(lg_prev.shape, F32)

    @pl.when(step <= N_TOK_TILES)
    def _():
        hh_p = hh_prev[...]
        lg_p = lg_prev[...]
        hh, logits = _post_mix(x_ref, oa_ref, ob_ref, gt_ref, gm_ref, sc_ref, sh_ref, g_ref,
                               wa_ref, wb_ref, wo_ref, wr2_ref, br_ref, x1_ref)
        _post_route(hh_p, lg_p, xy_ref, rc_ref, cu_ref)
        hh_prev[...] = hh
        lg_prev[...] = logits

    @pl.when(step > N_TOK_TILES)
    def _():
        xy_ref[...] = jnp.zeros(xy_ref.shape, BF16)


def _post_mix(x_ref, oa_ref, ob_ref, gt_ref, gm_ref, sc_ref, sh_ref, g_ref,
              wa_ref, wb_ref, wo_ref, wr2_ref, br_ref, x1_ref):
    pa = jnp.dot(oa_ref[...], wa_ref[...], preferred_element_type=F32)
    pb = jnp.dot(ob_ref[...], wb_ref[...], preferred_element_type=F32)
    ga = jax.nn.sigmoid(gt_ref[:, 0:D_MODEL].astype(F32))
    gb = jax.nn.sigmoid(gt_ref[:, D_MODEL:2 * D_MODEL].astype(F32))
    merged = (ga * pa + gb * pb).astype(BF16)
    y = jnp.dot(merged, wo_ref[...], preferred_element_type=F32)
    x1 = x_ref[...] + gm_ref[...] * y
    x1_ref[...] = x1

    rs = lax.rsqrt(jnp.mean(x1 * x1, axis=-1, keepdims=True) + EPS)
    a = g_ref[...] * (1.0 + sc_ref[...])
    h2 = x1 * rs * a + sh_ref[...]

    hh = h2.astype(BF16)
    hl = (h2 - hh.astype(F32)).astype(BF16)
    hi_both = jnp.dot(hh, wr2_ref[...], preferred_element_type=F32)
    logits = (hi_both[:, 0:LANES] + hi_both[:, LANES:2 * LANES]
              + jnp.dot(hl, wr2_ref[:, 0:LANES], preferred_element_type=F32)
              + br_ref[...])
    return hh, logits


def _post_route(hh, logits, xy_ref, rc_ref, cu_ref):
    tm = TM_POST
    lane = lax.broadcasted_iota(jnp.int32, (tm, LANES), 1).astype(F32)
    big = float(LANES)
    gl = jnp.where(lane < N_GROUPS, logits, -jnp.inf)
    gmax = jnp.max(gl, axis=-1, keepdims=True)
    gi = jnp.min(jnp.where(gl == gmax, lane, big), axis=-1, keepdims=True)
    gsum = jnp.sum(jnp.exp(gl - gmax), axis=-1, keepdims=True)
    gp = 1.0 / gsum
    e_lo = N_GROUPS + EXPERTS_PER_GROUP * gi
    el = jnp.where((lane >= e_lo) & (lane < e_lo + EXPERTS_PER_GROUP), logits, -jnp.inf)
    v1 = jnp.max(el, axis=-1, keepdims=True)
    i1 = jnp.min(jnp.where(el == v1, lane, big), axis=-1, keepdims=True)
    el2 = jnp.where(lane == i1, -jnp.inf, el)
    v2 = jnp.max(el2, axis=-1, keepdims=True)
    i2 = jnp.min(jnp.where(el2 == v2, lane, big), axis=-1, keepdims=True)
    e21 = jnp.exp(v2 - v1)
    w1 = gp / (1.0 + e21)
    w2 = gp * e21 / (1.0 + e21)
    e1 = i1 - N_GROUPS
    e2 = i2 - N_GROUPS

    oh = jnp.where((lane == e1) | (lane == e2), 1.0, 0.0)
    cnt_u = jnp.floor((jnp.sum(oh, axis=0, keepdims=True) + (UNIT - 1)) * (1.0 / UNIT))
    r128 = lax.broadcasted_iota(jnp.int32, (LANES, LANES), 0)
    c128 = lax.broadcasted_iota(jnp.int32, (LANES, LANES), 1)
    before_lane = jnp.where(r128 < c128, 1.0, 0.0).astype(BF16)
    loc_u = jnp.dot(jnp.broadcast_to(cnt_u, (8, LANES)).astype(BF16), before_lane,
                    preferred_element_type=F32)
    rr = lax.broadcasted_iota(jnp.int32, (tm, tm), 0)
    cc = lax.broadcasted_iota(jnp.int32, (tm, tm), 1)
    strict = jnp.where(rr > cc, 1.0, 0.0).astype(BF16)
    pos_e = jnp.dot(strict, oh.astype(BF16), preferred_element_type=F32) + loc_u[0:1] * UNIT
    lp1 = jnp.sum(jnp.where(lane == e1, pos_e, 0.0), axis=-1, keepdims=True)
    lp2 = jnp.sum(jnp.where(lane == e2, pos_e, 0.0), axis=-1, keepdims=True)

    eye = rr == cc
    ones8 = jnp.ones((8, tm), BF16)

    def to_row(col):
        hi = jnp.floor(col * (1.0 / 64.0))
        lo_ = col - 64.0 * hi
        d_hi = jnp.where(eye, hi, 0.0).astype(BF16)
        d_lo = jnp.where(eye, lo_, 0.0).astype(BF16)
        row8 = (64.0 * jnp.dot(ones8, d_hi, preferred_element_type=F32)
                + jnp.dot(ones8, d_lo, preferred_element_type=F32))
        return row8[0:1]

    srow = lax.broadcasted_iota(jnp.int32, (XY_ROWS, tm), 0).astype(F32)
    pm1 = jnp.where(srow == to_row(lp1), 1.0, 0.0).astype(BF16)
    pm2 = jnp.where(srow == to_row(lp2), 1.0, 0.0).astype(BF16)
    w1h = w1.astype(BF16).astype(F32)
    w2h = w2.astype(BF16).astype(F32)
    side = jnp.where(lane == W1_LANES[0], w1h, jnp.where(lane == W1_LANES[1], w1 - w1h,
           jnp.where(lane == W2_LANES[0], w2h, jnp.where(lane == W2_LANES[1], w2 - w2h,
           jnp.where(lane == E1_LANE, e1, jnp.where(lane == E2_LANE, e2, 0.0))))))
    tok = jnp.concatenate([hh, side.astype(BF16)], axis=1)
    xy_ref[...] = jnp.dot(pm1 + pm2, tok, preferred_element_type=F32).astype(BF16)

    cu_ref[...] = jnp.broadcast_to(cnt_u, cu_ref.shape)
    rc_ref[...] = jnp.where(lane == 0, lp1, jnp.where(lane == 1, lp2, 0.0))


def _post(x2, oa, ob, gates, gate_m, scale_f, shift_f, g_ffn, wa, wb, wo, wr2, b_r):
    tm = TM_POST
    tpb = SEQ // tm
    n_steps = N_TOK_TILES
    row = lambda i: (jnp.minimum(i, n_steps - 1), 0)
    per_b = lambda i: (jnp.minimum(i, n_steps - 1) // tpb, 0, 0)
    routed = lambda i: (jnp.clip(i - 1, 0, n_steps - 1), 0)
    const = lambda i: (0, 0)
    return pl.pallas_call(
        _post_kernel,
        out_shape=[jax.ShapeDtypeStruct((N_TOK, D_MODEL), F32),
                   jax.ShapeDtypeStruct(((n_steps + PAD_BLOCKS) * XY_ROWS, XY_COLS), BF16),
                   jax.ShapeDtypeStruct((N_TOK, LANES), F32),
                   jax.ShapeDtypeStruct((n_steps * 8, LANES), F32)],
        grid=(n_steps + 1 + PAD_BLOCKS,),
        in_specs=[pl.BlockSpec((tm, D_MODEL), row),
                  pl.BlockSpec((tm, Q_A), row),
                  pl.BlockSpec((tm, W_B), row),
                  pl.BlockSpec((tm, 2 * D_MODEL), row),
                  pl.BlockSpec((None, 1, D_MODEL), per_b),
                  pl.BlockSpec((None, 1, D_MODEL), per_b),
                  pl.BlockSpec((None, 1, D_MODEL), per_b),
                  pl.BlockSpec((1, D_MODEL), const),
                  pl.BlockSpec(wa.shape, const),
                  pl.BlockSpec(wb.shape, const),
                  pl.BlockSpec(wo.shape, const),
                  pl.BlockSpec(wr2.shape, const),
                  pl.BlockSpec((1, LANES), const)],
        out_specs=[pl.BlockSpec((tm, D_MODEL), row),
                   pl.BlockSpec((XY_ROWS, XY_COLS), lambda i: (jnp.maximum(i - 1, 0), 0)),
                   pl.BlockSpec((tm, LANES), routed),
                   pl.BlockSpec((8, LANES), routed)],
        scratch_shapes=[pltpu.VMEM((tm, D_MODEL), BF16),
                        pltpu.VMEM((tm, LANES), F32)],
        compiler_params=_cparams(1),
        name="post",
    )(x2, oa, ob, gates, gate_m, scale_f, shift_f, g_ffn, wa, wb, wo, wr2, b_r)


def _experts_kernel(te_ref, nu_ref, ur_ref, xy_in, wg_ref, wu_ref, wd_ref, xy_out,
                    xbuf, ybuf, wg_s, wu_s, wd_s, gsem, ssem):
    del xy_in
    r = pl.program_id(0)
    last = pl.num_programs(0) - 1
    n_used = nu_ref[0]
    slot = lax.rem(r, 2)

    def unit_row(step, s):
        return pl.multiple_of(ur_ref[step * UNITS_PER_TILE + s], UNIT)

    def start_gathers(step, sl):
        for s in range(UNITS_PER_TILE):
            pltpu.make_async_copy(xy_out.at[pl.ds(unit_row(step, s), UNIT), :],
                                  xbuf.at[sl, pl.ds(s * UNIT, UNIT), :], gsem.at[sl]).start()

    def wait_gathers(sl):
        pltpu.make_async_copy(xy_out.at[pl.ds(0, TM_EXP), :], xbuf.at[sl], gsem.at[sl]).wait()

    def start_scatters(step, sl):
        for s in range(UNITS_PER_TILE):
            pltpu.make_async_copy(ybuf.at[sl, pl.ds(s * UNIT, UNIT), :],
                                  xy_out.at[pl.ds(unit_row(step, s), UNIT), pl.ds(0, D_MODEL)],
                                  ssem.at[sl]).start()

    def wait_scatters(sl):
        pltpu.make_async_copy(ybuf.at[sl], xy_out.at[pl.ds(0, TM_EXP), pl.ds(0, D_MODEL)], ssem.at[sl]).wait()

    @pl.when(r == 0)
    def _():
        start_gathers(0, 0)

    @pl.when(jnp.logical_and(r < n_used, r >= 2))
    def _():
        wait_scatters(slot)

    @pl.when(jnp.logical_and(r < n_used,
                             jnp.logical_or(r == 0, te_ref[r] != te_ref[jnp.maximum(r - 1, 0)])))
    def _():
        wg_s[...] = wg_ref[...].astype(BF16)
        wu_s[...] = wu_ref[...].astype(BF16)
        wd_s[...] = wd_ref[...].astype(BF16)

    @pl.when(r < n_used)
    def _():
        wait_gathers(slot)
        start_gathers(jnp.minimum(r + 1, last), 1 - slot)
        x = xbuf[slot, :, 0:D_MODEL]
        side = xbuf[slot, :, D_MODEL:XY_COLS].astype(F32)
        lane = lax.broadcasted_iota(jnp.int32, side.shape, 1)

        def lanes_sum(a, b):
            return jnp.sum(jnp.where((lane == a) | (lane == b), side, 0.0), axis=-1, keepdims=True)

        is_slot1 = lanes_sum(E1_LANE, E1_LANE) == te_ref[r].astype(F32)
        wrow = jnp.where(is_slot1, lanes_sum(*W1_LANES), lanes_sum(*W2_LANES))
        a = jnp.dot(x, wg_s[...], preferred_element_type=F32)
        u = jnp.dot(x, wu_s[...], preferred_element_type=F32)
        hid = (a * jax.nn.sigmoid(a) * u * wrow).astype(BF16)
        ybuf[slot] = jnp.dot(hid, wd_s[...], preferred_element_type=F32).astype(BF16)
        start_scatters(r, slot)

    @pl.when(r == n_used - 1)
    def _():
        wait_gathers(1 - slot)
        wait_scatters(slot)

        @pl.when(r >= 1)
        def _():
            wait_scatters(1 - slot)


def _experts(tile_expert, n_used, unit_rows, xy, wg, wu, wd):
    wmap = lambda r, te, nv, ur: (te[r], 0, 0)
    grid_spec = pltpu.PrefetchScalarGridSpec(
        num_scalar_prefetch=3,
        grid=(N_EXP_TILES,),
        in_specs=[pl.BlockSpec(memory_space=pl.ANY),
                  pl.BlockSpec((None, D_MODEL, D_FF_EXPERT), wmap),
                  pl.BlockSpec((None, D_MODEL, D_FF_EXPERT), wmap),
                  pl.BlockSpec((None, D_FF_EXPERT, D_MODEL), wmap)],
        out_specs=pl.BlockSpec(memory_space=pl.ANY),
        scratch_shapes=[pltpu.VMEM((2, TM_EXP, XY_COLS), BF16),
                        pltpu.VMEM((2, TM_EXP, D_MODEL), BF16),
                        pltpu.VMEM((D_MODEL, D_FF_EXPERT), BF16),
                        pltpu.VMEM((D_MODEL, D_FF_EXPERT), BF16),
                        pltpu.VMEM((D_FF_EXPERT, D_MODEL), BF16),
                        pltpu.SemaphoreType.DMA((2,)),
                        pltpu.SemaphoreType.DMA((2,))])
    return pl.pallas_call(
        _experts_kernel,
        out_shape=jax.ShapeDtypeStruct(xy.shape, xy.dtype),
        grid_spec=grid_spec,
        input_output_aliases={3: 0},
        compiler_params=_cparams(1),
        name="experts",
    )(tile_expert, n_used, unit_rows, xy, wg, wu, wd)


def _combine_kernel(x1_ref, rc_ref, gf_ref, gfin_ref, y_ref, o_ref):
    lp1 = rc_ref[:, 0:1]
    lp2 = rc_ref[:, 1:2]
    scol = lax.broadcasted_iota(jnp.int32, (TM_ROW, XY_ROWS), 1).astype(F32)
    pick = jnp.where((scol == lp1) | (scol == lp2), 1.0, 0.0).astype(BF16)
    y = jnp.dot(pick, y_ref[...], preferred_element_type=F32)
    xf = x1_ref[...] + gf_ref[...] * y
    rs = lax.rsqrt(jnp.mean(xf * xf, axis=-1, keepdims=True) + EPS)
    o_ref[...] = xf * rs * gfin_ref[...]


def _combine(x1, rcol, gate_f, g_final, xy):
    tm = TM_ROW
    tpb = SEQ // tm
    row = lambda i: (i, 0)
    return pl.pallas_call(
        _combine_kernel,
        out_shape=jax.ShapeDtypeStruct((N_TOK, D_MODEL), F32),
        grid=(N_TOK // tm,),
        in_specs=[pl.BlockSpec((tm, D_MODEL), row),
                  pl.BlockSpec((tm, LANES), row),
                  pl.BlockSpec((None, 1, D_MODEL), lambda i: (i // tpb, 0, 0)),
                  pl.BlockSpec((1, D_MODEL), lambda i: (0, 0)),
                  pl.BlockSpec((XY_ROWS, D_MODEL), row)],
        out_specs=pl.BlockSpec((tm, D_MODEL), row),
        compiler_params=_cparams(1),
        name="combine",
    )(x1, rcol, gate_f, g_final, xy)


def _t5_bucket_np():
    qi = np.arange(BLOCK)[:, None]
    kj = np.arange(2 * BLOCK)[None, :]
    dist = qi - kj + BLOCK
    n = np.maximum(dist, 0)
    max_exact = NUM_BUCKETS // 2
    nf = np.maximum(n, 1).astype(np.float32)
    large = max_exact + (np.log(nf / np.float32(max_exact)) / np.float32(math.log(MAX_DISTANCE / max_exact))
                         * np.float32(NUM_BUCKETS - max_exact)).astype(np.int32)
    large = np.minimum(large, NUM_BUCKETS - 1)
    bucket = np.where(n < max_exact, n, large)
    band = (dist >= 0) & (dist < WINDOW)
    return bucket.astype(np.int32), band


def kernel(x, c, w_ada, b_ada, g_norm_mix, g_norm_ffn, w_in, sinks, b_forget, w_proj_swa, w_proj_fox,
           w_out, rel_bias_table, w_router_group, b_router_group, w_router_expert, b_router_expert,
           w_gate_exp, w_up_exp, w_down_exp, g_final):
    l = 0
    x2 = x.reshape(N_TOK, D_MODEL)

    c16 = jnp.concatenate([c, jnp.zeros_like(c)], axis=0)
    mod = _ada(c16, w_ada[l], b_ada[l][None, :])[:BATCH]
    shift_m, scale_m, gate_m, shift_f, scale_f, gate_f = [
        m.reshape(BATCH, 1, D_MODEL) for m in jnp.split(mod, 6, axis=-1)]

    w = w_in[l]
    o_ka, o_va, o_qb = Q_A, Q_A + KV_A, Q_A + 2 * KV_A
    o_kb, o_vb, o_f = o_qb + W_B, o_qb + 2 * W_B, o_qb + 3 * W_B
    o_g = o_f + N_HEADS_FOX

    def dup(cols):
        heads = [cols[:, h * HEAD_DIM:(h + 1) * HEAD_DIM] for h in range(N_KV_HEADS_SWA)]
        return jnp.concatenate([hd for hd in heads for _ in range(2)], axis=1)

    w_main = jnp.concatenate([w[:, :Q_A], dup(w[:, o_ka:o_va]), dup(w[:, o_va:o_qb]),
                              w[:, o_qb:o_f]], axis=1).astype(BF16)
    w_f = jnp.pad(w[:, o_f:o_g], ((0, 0), (0, LANES - N_HEADS_FOX))).astype(BF16)
    w_g = w[:, o_g:].astype(BF16)
    qa, kdup, vdup, qb, kb, vb, f_pad, gates, nrm = _inproj(
        x2, scale_m, shift_m, g_norm_mix[l][None, :], w_main, w_f, w_g)

    b_pad = jnp.pad(b_forget[l], (0, LANES - N_HEADS_FOX))[None, :]
    lanes = np.arange(LANES)
    jmod = jnp.asarray(np.where(lanes < DECAY_LANES * N_HEADS_FOX, lanes % DECAY_LANES, 7)[None, :].astype(np.int32))
    dq, dk, fb = _cum(f_pad, b_pad, jmod)

    bucket, band = _t5_bucket_np()
    onehot = jnp.asarray(bucket[None] == np.arange(NUM_BUCKETS)[:, None, None], dtype=F32)
    bias = jnp.einsum("bh,bqk->hqk", rel_bias_table.astype(F32), onehot, precision=HIGHEST)
    bias = jnp.where(band[None], bias, NEG_INF)
    first = np.arange(2 * BLOCK)[None, None, :] < BLOCK
    bias = jnp.stack([jnp.where(first, NEG_INF, bias), bias]).reshape(2, N_KV_HEADS_SWA, -1, 2 * BLOCK)
    o_a = _swa(sinks[l].astype(F32), qa, kdup, vdup, bias)

    first_tiles = _fox_first_tiles(nrm, fb)
    n_q_tiles = SEQ // TQ_FOX
    n_items = jnp.sum(jnp.arange(n_q_tiles, dtype=jnp.int32)[None, :] + 1
                      - first_tiles.reshape(-1, n_q_tiles), axis=1).astype(jnp.int32)
    o_b = _fox_flat(first_tiles, n_items, qb, kb, vb, dq, dk)

    w_r = jnp.concatenate([w_router_group[l]] + [w_router_expert[l][g] for g in range(N_GROUPS)], axis=1)
    w_r = jnp.pad(w_r, ((0, 0), (0, LANES - w_r.shape[1])))
    wr_hi = w_r.astype(BF16)
    wr_lo = (w_r - wr_hi.astype(F32)).astype(BF16)
    wr2 = jnp.concatenate([wr_hi, wr_lo], axis=1)
    b_r = jnp.concatenate([b_router_group[l], b_router_expert[l].reshape(-1)])
    b_r = jnp.pad(b_r, (0, LANES - b_r.shape[0]))[None, :]
    x1, xy, rcol, cu = _post(x2, o_a, o_b, gates, gate_m, scale_f, shift_f, g_norm_ffn[l][None, :],
                             w_proj_swa[l].astype(BF16), w_proj_fox[l].astype(BF16), w_out[l].astype(BF16),
                             wr2, b_r)

    i32 = jnp.int32
    n_tok_tiles = N_TOK // TM_POST
    cu = cu.reshape(n_tok_tiles, 8, LANES)[:, 0, :N_EXPERTS].astype(i32)
    loc_u = jnp.cumsum(cu, axis=1) - cu
    cend = jnp.cumsum(cu, axis=0)
    cstart = cend - cu
    tot_u = cend[-1]
    tiles_e = (tot_u + UNITS_PER_TILE - 1) // UNITS_PER_TILE
    tile_end = jnp.cumsum(tiles_e)
    tile_start = tile_end - tiles_e
    r = jnp.arange(N_EXP_TILES, dtype=i32)
    tile_expert = jnp.minimum(jnp.sum((tile_end[None, :] <= r[:, None]).astype(i32), axis=1), N_EXPERTS - 1)
    sel_e = tile_expert[:, None] == jnp.arange(N_EXPERTS, dtype=i32)[None, :]
    tw = r - jnp.sum(jnp.where(sel_e, tile_start[None, :], 0), axis=1)
    tot_r = jnp.sum(jnp.where(sel_e, tot_u[None, :], 0), axis=1)
    n_used = tile_end[-1:].astype(i32)
    q = tw[:, None] * UNITS_PER_TILE + jnp.arange(UNITS_PER_TILE, dtype=i32)[None, :]

    def of_expert(tab):
        return jnp.sum(jnp.where(sel_e[:, None, :], tab[None, :, :], 0), axis=2)

    cend_r, cstart_r, loc_r = of_expert(cend), of_expert(cstart), of_expert(loc_u)
    src_tile = jnp.minimum(jnp.sum((cend_r[:, None, :] <= q[:, :, None]).astype(i32), axis=2), n_tok_tiles - 1)
    sel_t = src_tile[:, :, None] == jnp.arange(n_tok_tiles, dtype=i32)[None, None, :]
    k = (q - jnp.sum(jnp.where(sel_t, cstart_r[:, None, :], 0), axis=2)
         + jnp.sum(jnp.where(sel_t, loc_r[:, None, :], 0), axis=2))
    real_rows = src_tile * XY_ROWS + k * UNIT
    pad_rows = PAD_BASE_ROW + (tile_expert[:, None] * PAD_UNITS_PER_EXPERT + (q - tot_r[:, None])) * UNIT
    idle_row = PAD_BASE_ROW + N_EXPERTS * PAD_UNITS_PER_EXPERT * UNIT
    unit_rows = jnp.where(q < tot_r[:, None], real_rows, pad_rows)
    unit_rows = jnp.where((r < n_used)[:, None], unit_rows, idle_row).reshape(-1).astype(i32)

    xy = _experts(tile_expert.astype(i32), n_used, unit_rows, xy,
                  w_gate_exp[l].reshape(N_EXPERTS, D_MODEL, D_FF_EXPERT),
                  w_up_exp[l].reshape(N_EXPERTS, D_MODEL, D_FF_EXPERT),
                  w_down_exp[l].reshape(N_EXPERTS, D_FF_EXPERT, D_MODEL))
    out = _combine(x1, rcol, gate_f, g_final[None, :], xy)
    return out.reshape(BATCH, SEQ, D_MODEL)
```

```python
import functools
import math

import numpy as np
import jax
import jax.numpy as jnp
from jax import lax
from jax.experimental import pallas as pl
from jax.experimental.pallas import tpu as pltpu

F32 = jnp.float32
BF16 = jnp.bfloat16
HIGHEST = lax.Precision.HIGHEST

D_MODEL = 1024
BATCH = 8
SEQ = 4096
N_TOK = BATCH * SEQ
N_HEADS_SWA = 8
N_KV_HEADS_SWA = 2
N_HEADS_FOX = 8
HEAD_DIM = 64
WINDOW = 128
BLOCK = 128
NUM_BUCKETS = 32
MAX_DISTANCE = 128
N_GROUPS = 4
EXPERTS_PER_GROUP = 8
N_EXPERTS = N_GROUPS * EXPERTS_PER_GROUP
D_FF_EXPERT = 256
EPS = 1e-6
NEG_INF = -1e30

Q_A = N_HEADS_SWA * HEAD_DIM
KV_A = N_KV_HEADS_SWA * HEAD_DIM
W_B = N_HEADS_FOX * HEAD_DIM
LANES = 128
QK_SCALE = HEAD_DIM ** -0.5

TM_IN = 512
TM_POST = 512
TQ_FOX = 512
TK_FOX = TQ_FOX
SWA_BLOCKS = 4
TM_EXP = 512
TM_ROW = 512
UNIT = 16
XY_UNITS = 2 * TM_POST // UNIT + N_EXPERTS
XY_ROWS = XY_UNITS * UNIT
XY_COLS = D_MODEL + LANES
UNITS_PER_TILE = TM_EXP // UNIT
N_TOK_TILES = N_TOK // TM_POST
N_EXP_TILES = N_TOK_TILES * XY_UNITS // UNITS_PER_TILE + N_EXPERTS
PAD_UNITS_PER_EXPERT = UNITS_PER_TILE - 1
PAD_BLOCKS = -(-(N_EXPERTS * PAD_UNITS_PER_EXPERT * UNIT) // XY_ROWS)
PAD_BASE_ROW = N_TOK_TILES * XY_ROWS
W1_LANES, W2_LANES, E1_LANE, E2_LANE = (4, 6), (5, 7), 8, 9
VMEM_LIMIT = 56 * 1024 * 1024

DECAY_LANES = 6
PRUNE_MARGIN = 110.0


def _cparams(n_axes):
    return pltpu.CompilerParams(dimension_semantics=("arbitrary",) * n_axes,
                                vmem_limit_bytes=VMEM_LIMIT)


def _ada_kernel(c_ref, w_ref, b_ref, o_ref):
    c = c_ref[...]
    ca = c * jax.nn.sigmoid(c)
    o_ref[...] = jnp.dot(ca.astype(BF16), w_ref[...].astype(BF16),
                         preferred_element_type=F32) + b_ref[...]


def _ada(c16, w_ada, b_ada):
    n_out = w_ada.shape[1]
    blk = 1024
    return pl.pallas_call(
        _ada_kernel,
        out_shape=jax.ShapeDtypeStruct((16, n_out), F32),
        grid=(n_out // blk,),
        in_specs=[pl.BlockSpec((16, D_MODEL), lambda j: (0, 0)),
                  pl.BlockSpec((D_MODEL, blk), lambda j: (0, j)),
                  pl.BlockSpec((1, blk), lambda j: (0, j))],
        out_specs=pl.BlockSpec((16, blk), lambda j: (0, j)),
        compiler_params=_cparams(1),
        name="ada",
    )(c16, w_ada, b_ada)


def _inproj_kernel(x_ref, sc_ref, sh_ref, g_ref, wm_ref, wf_ref, wg_ref, ind_ref,
                   qa_ref, kd_ref, vd_ref, qb_ref, kb_ref, vb_ref, f_ref, gt_ref, nrm_ref):
    x = x_ref[...]
    rs = lax.rsqrt(jnp.mean(x * x, axis=-1, keepdims=True) + EPS)
    a = g_ref[...] * (1.0 + sc_ref[...])
    h = (x * rs * a + sh_ref[...]).astype(BF16)

    def mm(w):
        return jnp.dot(h, w, preferred_element_type=F32)

    qa_ref[...] = (mm(wm_ref[:, 0:512]) * QK_SCALE).astype(BF16)
    kd_ref[...] = mm(wm_ref[:, 512:768]).astype(BF16)
    vd_ref[...] = mm(wm_ref[:, 768:1024]).astype(BF16)
    qb = (mm(wm_ref[:, 1024:1536]) * QK_SCALE).astype(BF16)
    kb = mm(wm_ref[:, 1536:2048]).astype(BF16)
    qb_ref[...] = qb
    kb_ref[...] = kb
    vb_ref[...] = mm(wm_ref[:, 2048:2560]).astype(BF16)
    sq = jnp.concatenate([qb, kb], axis=1).astype(F32)
    seg = jnp.dot((sq * sq).astype(BF16), ind_ref[...], preferred_element_type=F32)
    nrm_ref[...] = jnp.broadcast_to(jnp.max(seg, axis=0, keepdims=True), nrm_ref.shape)
    f_ref[...] = mm(wf_ref[...])
    gt_ref[...] = mm(wg_ref[...]).astype(BF16)


def _inproj(x2, scale_m, shift_m, g_mix, w_main, w_f, w_g):
    tm = TM_IN
    tpb = SEQ // tm
    row = lambda i: (i, 0)
    per_b = lambda i: (i // tpb, 0, 0)
    const = lambda i: (0, 0)
    outs = [(Q_A, BF16), (2 * KV_A, BF16), (2 * KV_A, BF16), (W_B, BF16), (W_B, BF16), (W_B, BF16),
            (LANES, F32), (2 * D_MODEL, BF16)]
    ind_np = np.zeros((2 * W_B, LANES), np.float32)
    ind_np[np.arange(2 * W_B), np.arange(2 * W_B) // HEAD_DIM] = 1.0
    ind = jnp.asarray(ind_np, dtype=BF16)
    n_steps = N_TOK // tm
    return pl.pallas_call(
        _inproj_kernel,
        out_shape=[jax.ShapeDtypeStruct((N_TOK, w), dt) for w, dt in outs]
        + [jax.ShapeDtypeStruct((n_steps * 8, LANES), F32)],
        grid=(n_steps,),
        in_specs=[pl.BlockSpec((tm, D_MODEL), row),
                  pl.BlockSpec((None, 1, D_MODEL), per_b),
                  pl.BlockSpec((None, 1, D_MODEL), per_b),
                  pl.BlockSpec((1, D_MODEL), const),
                  pl.BlockSpec(w_main.shape, const),
                  pl.BlockSpec(w_f.shape, const),
                  pl.BlockSpec(w_g.shape, const),
                  pl.BlockSpec(ind.shape, const)],
        out_specs=[pl.BlockSpec((tm, w), row) for w, _ in outs] + [pl.BlockSpec((8, LANES), row)],
        compiler_params=_cparams(1),
        name="inproj",
    )(x2, scale_m, shift_m, g_mix, w_main, w_f, w_g, ind)


def _log_sigmoid(x):
    return jnp.minimum(x, 0.0) - jnp.log1p(jnp.exp(-jnp.abs(x)))


def _cum_kernel(f_ref, b_ref, jm_ref, qa_ref, ka_ref, fb_ref):
    cum = _log_sigmoid(f_ref[...] + b_ref[...])
    row = lax.broadcasted_iota(jnp.int32, cum.shape, 0)
    k = 1
    while k < SEQ:
        if k < 8:
            shifted = jnp.where(row >= k, pltpu.roll(cum, k, 0), 0.0)
        else:
            shifted = jnp.concatenate([jnp.zeros((k, LANES), F32), cum[:SEQ - k]], axis=0)
        cum = cum + shifted
        k *= 2
    jm = jm_ref[...]
    for blk in range(SEQ // LANES):
        rows = slice(blk * LANES, (blk + 1) * LANES)
        cb = cum[rows]
        carry = cb[LANES - 1:LANES]
        hi = cb.astype(BF16).astype(F32)
        r1 = cb - hi
        mid = r1.astype(BF16).astype(F32)
        lo = (r1 - mid).astype(BF16).astype(F32)
        one = jnp.ones_like(cb)
        zero = jnp.zeros_like(cb)
        qa = jnp.where(jm == 0, hi, jnp.where(jm == 1, mid, jnp.where(jm == 2, lo,
                       jnp.where(jm < DECAY_LANES, one, zero))))
        ka = jnp.where(jm == 3, -hi, jnp.where(jm == 4, -mid, jnp.where(jm == 5, -lo,
                       jnp.where(jm < 3, one, zero))))
        qa_ref[rows, :] = qa.astype(BF16)
        ka_ref[rows, :] = ka.astype(BF16)
        blocks_per_tile = TQ_FOX // LANES
        tile = blk // blocks_per_tile
        if blk % blocks_per_tile == 0:
            fb_ref[2 * tile:2 * tile + 1, :] = cb[0:1]
        if blk % blocks_per_tile == blocks_per_tile - 1:
            fb_ref[2 * tile + 1:2 * tile + 2, :] = carry


def _cum(f_pad, b_pad, jmod):
    n_tiles = SEQ // TQ_FOX
    return pl.pallas_call(
        _cum_kernel,
        out_shape=[jax.ShapeDtypeStruct((BATCH, SEQ, LANES), BF16)] * 2
        + [jax.ShapeDtypeStruct((BATCH, 2 * n_tiles, LANES), F32)],
        grid=(BATCH,),
        in_specs=[pl.BlockSpec((SEQ, LANES), lambda b: (b, 0)),
                  pl.BlockSpec((1, LANES), lambda b: (0, 0)),
                  pl.BlockSpec((1, LANES), lambda b: (0, 0))],
        out_specs=[pl.BlockSpec((None, SEQ, LANES), lambda b: (b, 0, 0))] * 2
        + [pl.BlockSpec((None, 2 * n_tiles, LANES), lambda b: (b, 0, 0))],
        compiler_params=_cparams(1),
        name="cum",
    )(f_pad, b_pad, jmod)


def _swa_block(sink_cols, q, kk, vv, bias_ref, lo):
    tiles = []
    for g in range(N_KV_HEADS_SWA):
        parts = []
        for t in range(2):
            qt = q[:, (2 * g + t) * LANES:(2 * g + t + 1) * LANES]
            zero = jnp.zeros_like(qt)
            parts.append(jnp.where(lo, qt, zero))
            parts.append(jnp.where(lo, zero, qt))
        q4 = jnp.concatenate(parts, axis=0)
        s = lax.dot_general(q4, kk[:, g * LANES:(g + 1) * LANES], (((1,), (1,)), ((), ())),
                            preferred_element_type=F32)
        s = s + bias_ref[g]
        sink = sink_cols[g]
        m = jnp.maximum(jnp.max(s, axis=-1, keepdims=True), sink)
        p = jnp.exp(s - m)
        den = jnp.sum(p, axis=-1, keepdims=True) + jnp.exp(sink - m)
        o = jnp.dot(p.astype(BF16), vv[:, g * LANES:(g + 1) * LANES],
                    preferred_element_type=F32) / den
        tiles.append(jnp.where(lo, o[0:BLOCK], o[BLOCK:2 * BLOCK]))
        tiles.append(jnp.where(lo, o[2 * BLOCK:3 * BLOCK], o[3 * BLOCK:4 * BLOCK]))
    return tiles


def _swa_kernel(sink_ref, q_ref, kc_ref, kp_ref, vc_ref, vp_ref, bias_first_ref, bias_ref, o_ref):
    lane = lax.broadcasted_iota(jnp.int32, (BLOCK, LANES), 1)
    lo = lane < HEAD_DIM
    grp = N_HEADS_SWA // N_KV_HEADS_SWA
    row = lax.broadcasted_iota(jnp.int32, (grp * BLOCK, 1), 0)
    sink_cols = []
    for g in range(N_KV_HEADS_SWA):
        col = jnp.full((grp * BLOCK, 1), sink_ref[g * grp + grp - 1], F32)
        for hh in range(grp - 2, -1, -1):
            col = jnp.where(row < (hh + 1) * BLOCK, sink_ref[g * grp + hh], col)
        sink_cols.append(col)
    for blk in range(SWA_BLOCKS):
        rows = slice(blk * BLOCK, (blk + 1) * BLOCK)
        if blk == 0:
            kk = jnp.concatenate([kp_ref[...], kc_ref[rows, :]], axis=0)
            vv = jnp.concatenate([vp_ref[...], vc_ref[rows, :]], axis=0)
            bias = bias_first_ref
        else:
            prev_rows = slice((blk - 1) * BLOCK, (blk + 1) * BLOCK)
            kk = kc_ref[prev_rows, :]
            vv = vc_ref[prev_rows, :]
            bias = bias_ref
        tiles = _swa_block(sink_cols, q_ref[rows, :], kk, vv, bias, lo)
        for c, tile in enumerate(tiles):
            o_ref[rows, c * LANES:(c + 1) * LANES] = tile.astype(BF16)


def _swa(sinks, qa, kdup, vdup, bias):
    nb = SEQ // BLOCK
    ns = nb // SWA_BLOCKS
    cur = lambda b, i, s: (b * ns + i, 0)
    prev = lambda b, i, s: (b * nb + jnp.maximum(SWA_BLOCKS * i - 1, 0), 0)
    grid_spec = pltpu.PrefetchScalarGridSpec(
        num_scalar_prefetch=1,
        grid=(BATCH, ns),
        in_specs=[pl.BlockSpec((SWA_BLOCKS * BLOCK, Q_A), cur),
                  pl.BlockSpec((SWA_BLOCKS * BLOCK, 2 * KV_A), cur),
                  pl.BlockSpec((BLOCK, 2 * KV_A), prev),
                  pl.BlockSpec((SWA_BLOCKS * BLOCK, 2 * KV_A), cur),
                  pl.BlockSpec((BLOCK, 2 * KV_A), prev),
                  pl.BlockSpec((None,) + bias.shape[1:], lambda b, i, s: (jnp.minimum(i, 1), 0, 0, 0)),
                  pl.BlockSpec((None,) + bias.shape[1:], lambda b, i, s: (1, 0, 0, 0))],
        out_specs=pl.BlockSpec((SWA_BLOCKS * BLOCK, Q_A), cur))
    return pl.pallas_call(
        _swa_kernel,
        out_shape=jax.ShapeDtypeStruct((N_TOK, Q_A), BF16),
        grid_spec=grid_spec,
        compiler_params=_cparams(2),
        name="swa",
    )(sinks, qa, kdup, kdup, vdup, vdup, bias, bias)


def _fox_kernel(js_ref, q_ref, k_ref, v_ref, qa_ref, ka_ref, o_ref,
                kaug, vaug, q2, m_sc, acc_sc, s_a, s_b):
    tq, tk = TQ_FOX, TK_FOX
    b = pl.program_id(0)
    t = pl.program_id(1)
    i = pl.program_id(2)
    j_start = js_ref[(b * pl.num_programs(1) + t) * pl.num_programs(2) + i]

    @pl.when(i == 0)
    def _():
        kaug[:, 0:LANES] = k_ref[...]
        kaug[:, LANES:2 * LANES] = ka_ref[...]
        vaug[:, 0:LANES] = v_ref[...]
        vaug[:, LANES:2 * LANES] = jnp.ones((SEQ, LANES), BF16)

    lane = lax.broadcasted_iota(jnp.int32, (tq, LANES), 1)
    lo = lane < HEAD_DIM
    base = 2 * DECAY_LANES * t
    own = [(lane >= base + h * DECAY_LANES) & (lane < base + (h + 1) * DECAY_LANES) for h in range(2)]
    q = q_ref[...]
    qa = qa_ref[...]
    zero = jnp.zeros_like(q)
    q2[0, :, 0:LANES] = jnp.where(lo, q, zero)
    q2[1, :, 0:LANES] = jnp.where(lo, zero, q)
    for h in range(2):
        q2[h, :, LANES:2 * LANES] = jnp.where(own[h], qa, zero)
    m_sc[...] = jnp.full(m_sc.shape, NEG_INF, F32)
    acc_sc[...] = jnp.zeros(acc_sc.shape, F32)

    def scores(h, ks):
        return lax.dot_general(q2[h], kaug[pl.ds(ks, tk), :], (((1,), (1,)), ((), ())),
                               preferred_element_type=F32)

    def consume(h, s, ks, mask):
        if mask is not None:
            s = jnp.where(mask, s, NEG_INF)
        m_prev = m_sc[h]
        m_new = jnp.maximum(m_prev, jnp.max(s, axis=-1, keepdims=True))
        alpha = jnp.exp(m_prev - m_new)
        p = jnp.exp(s - jnp.concatenate([m_new] * (tk // LANES), axis=1))
        pv = jnp.dot(p.astype(BF16), vaug[pl.ds(ks, tk), :], preferred_element_type=F32)
        acc_sc[h] = jnp.concatenate([alpha, alpha], axis=1) * acc_sc[h] + pv
        m_sc[h] = m_new

    def key_start(j):
        return pl.multiple_of(j * tk, tk)

    def scores_into(buf, j):
        for h in range(2):
            buf[h] = scores(h, key_start(j))

    def consume_from(buf, j, mask):
        for h in range(2):
            consume(h, buf[h], key_start(j), mask)

    rr = lax.broadcasted_iota(jnp.int32, (tq, tk), 0)
    cc = lax.broadcasted_iota(jnp.int32, (tq, tk), 1)
    causal = cc <= rr
    n_full = i - j_start

    scores_into(s_a, j_start)

    def pair(p, carry):
        j = j_start + 2 * p
        scores_into(s_b, j + 1)
        consume_from(s_a, j, None)
        scores_into(s_a, j + 2)
        consume_from(s_b, j + 1, None)
        return carry

    lax.fori_loop(0, n_full // 2, pair, 0)
    odd = lax.rem(n_full, 2) == 1

    @pl.when(odd)
    def _():
        scores_into(s_b, i)
        consume_from(s_a, i - 1, None)
        consume_from(s_b, i, causal)

    @pl.when(jnp.logical_not(odd))
    def _():
        consume_from(s_a, i, causal)

    outs = [acc_sc[h, :, 0:LANES] / acc_sc[h, :, LANES:2 * LANES] for h in range(2)]
    o_ref[...] = jnp.where(lo, outs[0], outs[1]).astype(BF16)


def _fox(j_start, qb, kb, vb, qa, ka):
    tq = TQ_FOX
    nq = SEQ // tq
    n_pairs = N_HEADS_FOX // 2
    qmap = lambda b, t, i, js: (b * nq + i, t)
    kmap = lambda b, t, i, js: (b, t)
    grid_spec = pltpu.PrefetchScalarGridSpec(
        num_scalar_prefetch=1,
        grid=(BATCH, n_pairs, nq),
        in_specs=[pl.BlockSpec((tq, LANES), qmap),
                  pl.BlockSpec((SEQ, LANES), kmap),
                  pl.BlockSpec((SEQ, LANES), kmap),
                  pl.BlockSpec((None, tq, LANES), lambda b, t, i, js: (b, i, 0)),
                  pl.BlockSpec((None, SEQ, LANES), lambda b, t, i, js: (b, 0, 0))],
        out_specs=pl.BlockSpec((tq, LANES), qmap),
        scratch_shapes=[pltpu.VMEM((SEQ, 2 * LANES), BF16),
                        pltpu.VMEM((SEQ, 2 * LANES), BF16),
                        pltpu.VMEM((2, tq, 2 * LANES), BF16),
                        pltpu.VMEM((2, tq, LANES), F32),
                        pltpu.VMEM((2, tq, 2 * LANES), F32),
                        pltpu.VMEM((2, tq, TK_FOX), F32),
                        pltpu.VMEM((2, tq, TK_FOX), F32)])
    return pl.pallas_call(
        _fox_kernel,
        out_shape=jax.ShapeDtypeStruct((N_TOK, W_B), BF16),
        grid_spec=grid_spec,
        compiler_params=_cparams(3),
        name="fox",
    )(j_start, qb, kb, vb, qa, ka)


def _fox_flat_kernel(js_ref, ni_ref, q_ref, k_ref, v_ref, qa_ref, ka_ref, o_ref,
                     kaug, vaug, q2, m_sc, acc_sc, s0, s1, mask_tbl):
    tq, tk = TQ_FOX, TK_FOX
    nq = SEQ // tq
    b = pl.program_id(0)
    t = pl.program_id(1)
    pair = b * pl.num_programs(1) + t
    first_tile = lambda i: js_ref[pair * nq + i]

    kaug[:, 0:LANES] = k_ref[...]
    kaug[:, LANES:2 * LANES] = ka_ref[...]
    vaug[:, 0:LANES] = v_ref[...]
    vaug[:, LANES:2 * LANES] = jnp.ones((SEQ, LANES), BF16)

    lane = lax.broadcasted_iota(jnp.int32, (tq, LANES), 1)
    lo = lane < HEAD_DIM
    base = 2 * DECAY_LANES * t
    own = [(lane >= base + h * DECAY_LANES) & (lane < base + (h + 1) * DECAY_LANES) for h in range(2)]
    for it in range(nq):
        rows = slice(it * tq, (it + 1) * tq)
        q = q_ref[rows, :]
        qa = qa_ref[rows, :]
        zero = jnp.zeros_like(q)
        q2[0, rows, 0:LANES] = jnp.where(lo, q, zero)
        q2[1, rows, 0:LANES] = jnp.where(lo, zero, q)
        for h in range(2):
            q2[h, rows, LANES:2 * LANES] = jnp.where(own[h], qa, zero)
    rr = lax.broadcasted_iota(jnp.int32, (tq, tk), 0)
    cc = lax.broadcasted_iota(jnp.int32, (tq, tk), 1)
    mask_tbl[0] = jnp.zeros((tq, tk), F32)
    mask_tbl[1] = jnp.where(cc <= rr, 0.0, NEG_INF)
    m_sc[...] = jnp.full(m_sc.shape, NEG_INF, F32)
    acc_sc[...] = jnp.zeros(acc_sc.shape, F32)

    def scores(h, i, j):
        return lax.dot_general(q2[h, pl.ds(pl.multiple_of(i * tq, tq), tq), :],
                               kaug[pl.ds(pl.multiple_of(j * tk, tk), tk), :],
                               (((1,), (1,)), ((), ())), preferred_element_type=F32)

    def consume(h, sbuf, i, j):
        s = sbuf[...] + mask_tbl[(i == j).astype(jnp.int32)]
        m_prev = jnp.where(j == first_tile(i), NEG_INF, m_sc[h])
        m_new = jnp.maximum(m_prev, jnp.max(s, axis=-1, keepdims=True))
        alpha = jnp.exp(m_prev - m_new)
        p = jnp.exp(s - jnp.concatenate([m_new] * (tk // LANES), axis=1))
        pv = jnp.dot(p.astype(BF16), vaug[pl.ds(pl.multiple_of(j * tk, tk), tk), :],
                     preferred_element_type=F32)
        acc_sc[h] = jnp.concatenate([alpha, alpha], axis=1) * acc_sc[h] + pv
        m_sc[h] = m_new

    def body(n, carry):
        i, j = carry
        wrap = j == i
        i_next = jnp.minimum(jnp.where(wrap, i + 1, i), nq - 1)
        j_next = jnp.where(wrap, first_tile(i_next), j + 1)
        s1[...] = scores(1, i, j)
        consume(0, s0, i, j)
        s0[...] = scores(0, i_next, j_next)
        consume(1, s1, i, j)

        @pl.when(wrap)
        def _():
            outs = [acc_sc[h, :, 0:LANES] / acc_sc[h, :, LANES:2 * LANES] for h in range(2)]
            o_ref[pl.ds(pl.multiple_of(i * tq, tq), tq), :] = jnp.where(lo, outs[0], outs[1]).astype(BF16)

        return i_next, j_next

    s0[...] = scores(0, 0, 0)
    lax.fori_loop(0, ni_ref[pair], body, (jnp.int32(0), jnp.int32(0)))


def _fox_flat(j_start, n_items, qb, kb, vb, qa, ka):
    tq = TQ_FOX
    n_pairs = N_HEADS_FOX // 2
    kmap = lambda b, t, js, ni: (b, t)
    bmap = lambda b, t, js, ni: (b, 0, 0)
    grid_spec = pltpu.PrefetchScalarGridSpec(
        num_scalar_prefetch=2,
        grid=(BATCH, n_pairs),
        in_specs=[pl.BlockSpec((SEQ, LANES), kmap),
                  pl.BlockSpec((SEQ, LANES), kmap),
                  pl.BlockSpec((SEQ, LANES), kmap),
                  pl.BlockSpec((None, SEQ, LANES), bmap),
                  pl.BlockSpec((None, SEQ, LANES), bmap)],
        out_specs=pl.BlockSpec((SEQ, LANES), kmap),
        scratch_shapes=[pltpu.VMEM((SEQ, 2 * LANES), BF16),
                        pltpu.VMEM((SEQ, 2 * LANES), BF16),
                        pltpu.VMEM((2, SEQ, 2 * LANES), BF16),
                        pltpu.VMEM((2, tq, LANES), F32),
                        pltpu.VMEM((2, tq, 2 * LANES), F32),
                        pltpu.VMEM((tq, TK_FOX), F32),
                        pltpu.VMEM((tq, TK_FOX), F32),
                        pltpu.VMEM((2, tq, TK_FOX), F32)])
    return pl.pallas_call(
        _fox_flat_kernel,
        out_shape=jax.ShapeDtypeStruct((N_TOK, W_B), BF16),
        grid_spec=grid_spec,
        compiler_params=_cparams(2),
        name="fox",
    )(j_start, n_items, qb, kb, vb, qa, ka)


def _fox_first_tiles(nrm, fb):
    n_tiles = SEQ // TQ_FOX
    nr = nrm.reshape(BATCH, n_tiles, 8, LANES)[:, :, 0, :] * 1.02
    qn = jnp.sqrt(nr[..., 0:N_HEADS_FOX])
    kn = jnp.sqrt(nr[..., N_HEADS_FOX:2 * N_HEADS_FOX])
    f_first = fb[:, 0::2, 0:DECAY_LANES * N_HEADS_FOX:DECAY_LANES]
    f_last = fb[:, 1::2, 0:DECAY_LANES * N_HEADS_FOX:DECAY_LANES]
    kn_prefix = lax.cummax(kn, axis=1)
    upper = qn[:, :, None, :] * kn_prefix[:, None, :, :] + f_first[:, :, None, :] - f_last[:, None, :, :]
    row_max_low = -(qn * kn)[:, :, None, :]
    ii = jnp.arange(n_tiles)[None, :, None, None]
    jj = jnp.arange(n_tiles)[None, None, :, None]
    skip = (upper < row_max_low - PRUNE_MARGIN) & (jj < ii)
    skip = jnp.all(skip.reshape(BATCH, n_tiles, n_tiles, N_HEADS_FOX // 2, 2), axis=-1)
    first = jnp.sum(jnp.cumprod(skip.astype(jnp.int32), axis=2), axis=2)
    return jnp.transpose(first, (0, 2, 1)).reshape(-1).astype(jnp.int32)


def _post_kernel(x_ref, oa_ref, ob_ref, gt_ref, gm_ref, sc_ref, sh_ref, g_ref,
                 wa_ref, wb_ref, wo_ref, wr2_ref, br_ref,
                 x1_ref, xy_ref, rc_ref, cu_ref, hh_prev, lg_prev):
    step = pl.program_id(0)

    @pl.when(step == 0)
    def _():
        hh_prev[...] = jnp.zeros(hh_prev.shape, BF16)
        lg_prev[...] = jnp.zeros(lg_prev.shape, F32)

    @pl.when(step <= N_TOK_TILES)
    def _():
        hh_p = hh_prev[...]
        lg_p = lg_prev[...]
        hh, logits = _post_mix(x_ref, oa_ref, ob_ref, gt_ref, gm_ref, sc_ref, sh_ref, g_ref,
                               wa_ref, wb_ref, wo_ref, wr2_ref, br_ref, x1_ref)
        _post_route(hh_p, lg_p, xy_ref, rc_ref, cu_ref)
        hh_prev[...] = hh
        lg_prev[...] = logits

    @pl.when(step > N_TOK_TILES)
    def _():
        xy_ref[...] = jnp.zeros(xy_ref.shape, BF16)


def _post_mix(x_ref, oa_ref, ob_ref, gt_ref, gm_ref, sc_ref, sh_ref, g_ref,
              wa_ref, wb_ref, wo_ref, wr2_ref, br_ref, x1_ref):
    pa = jnp.dot(oa_ref[...], wa_ref[...], preferred_element_type=F32)
    pb = jnp.dot(ob_ref[...], wb_ref[...], preferred_element_type=F32)
    ga = jax.nn.sigmoid(gt_ref[:, 0:D_MODEL].astype(F32))
    gb = jax.nn.sigmoid(gt_ref[:, D_MODEL:2 * D_MODEL].astype(F32))
    merged = (ga * pa + gb * pb).astype(BF16)
    y = jnp.dot(merged, wo_ref[...], preferred_element_type=F32)
    x1 = x_ref[...] + gm_ref[...] * y
    x1_ref[...] = x1

    rs = lax.rsqrt(jnp.mean(x1 * x1, axis=-1, keepdims=True) + EPS)
    a = g_ref[...] * (1.0 + sc_ref[...])
    h2 = x1 * rs * a + sh_ref[...]

    hh = h2.astype(BF16)
    hl = (h2 - hh.astype(F32)).astype(BF16)
    hi_both = jnp.dot(hh, wr2_ref[...], preferred_element_type=F32)
    logits = (hi_both[:, 0:LANES] + hi_both[:, LANES:2 * LANES]
              + jnp.dot(hl, wr2_ref[:, 0:LANES], preferred_element_type=F32)
              + br_ref[...])
    return hh, logits


def _post_route(hh, logits, xy_ref, rc_ref, cu_ref):
    tm = TM_POST
    lane = lax.broadcasted_iota(jnp.int32, (tm, LANES), 1).astype(F32)
    big = float(LANES)
    gl = jnp.where(lane < N_GROUPS, logits, -jnp.inf)
    gmax = jnp.max(gl, axis=-1, keepdims=True)
    gi = jnp.min(jnp.where(gl == gmax, lane, big), axis=-1, keepdims=True)
    gsum = jnp.sum(jnp.exp(gl - gmax), axis=-1, keepdims=True)
    gp = 1.0 / gsum
    e_lo = N_GROUPS + EXPERTS_PER_GROUP * gi
    el = jnp.where((lane >= e_lo) & (lane < e_lo + EXPERTS_PER_GROUP), logits, -jnp.inf)
    v1 = jnp.max(el, axis=-1, keepdims=True)
    i1 = jnp.min(jnp.where(el == v1, lane, big), axis=-1, keepdims=True)
    el2 = jnp.where(lane == i1, -jnp.inf, el)
    v2 = jnp.max(el2, axis=-1, keepdims=True)
    i2 = jnp.min(jnp.where(el2 == v2, lane, big), axis=-1, keepdims=True)
    e21 = jnp.exp(v2 - v1)
    w1 = gp / (1.0 + e21)
    w2 = gp * e21 / (1.0 + e21)
    e1 = i1 - N_GROUPS
    e2 = i2 - N_GROUPS

    oh = jnp.where((lane == e1) | (lane == e2), 1.0, 0.0)
    cnt_u = jnp.floor((jnp.sum(oh, axis=0, keepdims=True) + (UNIT - 1)) * (1.0 / UNIT))
    r128 = lax.broadcasted_iota(jnp.int32, (LANES, LANES), 0)
    c128 = lax.broadcasted_iota(jnp.int32, (LANES, LANES), 1)
    before_lane = jnp.where(r128 < c128, 1.0, 0.0).astype(BF16)
    loc_u = jnp.dot(jnp.broadcast_to(cnt_u, (8, LANES)).astype(BF16), before_lane,
                    preferred_element_type=F32)
    rr = lax.broadcasted_iota(jnp.int32, (tm, tm), 0)
    cc = lax.broadcasted_iota(jnp.int32, (tm, tm), 1)
    strict = jnp.where(rr > cc, 1.0, 0.0).astype(BF16)
    pos_e = jnp.dot(strict, oh.astype(BF16), preferred_element_type=F32) + loc_u[0:1] * UNIT
    lp1 = jnp.sum(jnp.where(lane == e1, pos_e, 0.0), axis=-1, keepdims=True)
    lp2 = jnp.sum(jnp.where(lane == e2, pos_e, 0.0), axis=-1, keepdims=True)

    eye = rr == cc
    ones8 = jnp.ones((8, tm), BF16)

    def to_row(col):
        hi = jnp.floor(col * (1.0 / 64.0))
        lo_ = col - 64.0 * hi
        d_hi = jnp.where(eye, hi, 0.0).astype(BF16)
        d_lo = jnp.where(eye, lo_, 0.0).astype(BF16)
        row8 = (64.0 * jnp.dot(ones8, d_hi, preferred_element_type=F32)
                + jnp.dot(ones8, d_lo, preferred_element_type=F32))
        return row8[0:1]

    srow = lax.broadcasted_iota(jnp.int32, (XY_ROWS, tm), 0).astype(F32)
    pm1 = jnp.where(srow == to_row(lp1), 1.0, 0.0).astype(BF16)
    pm2 = jnp.where(srow == to_row(lp2), 1.0, 0.0).astype(BF16)
    w1h = w1.astype(BF16).astype(F32)
    w2h = w2.astype(BF16).astype(F32)
    side = jnp.where(lane == W1_LANES[0], w1h, jnp.where(lane == W1_LANES[1], w1 - w1h,
           jnp.where(lane == W2_LANES[0], w2h, jnp.where(lane == W2_LANES[1], w2 - w2h,
           jnp.where(lane == E1_LANE, e1, jnp.where(lane == E2_LANE, e2, 0.0))))))
    tok = jnp.concatenate([hh, side.astype(BF16)], axis=1)
    xy_ref[...] = jnp.dot(pm1 + pm2, tok, preferred_element_type=F32).astype(BF16)

    cu_ref[...] = jnp.broadcast_to(cnt_u, cu_ref.shape)
    rc_ref[...] = jnp.where(lane == 0, lp1, jnp.where(lane == 1, lp2, 0.0))


def _post(x2, oa, ob, gates, gate_m, scale_f, shift_f, g_ffn, wa, wb, wo, wr2, b_r):
    tm = TM_POST
    tpb = SEQ // tm
    n_steps = N_TOK_TILES
    row = lambda i: (jnp.minimum(i, n_steps - 1), 0)
    per_b = lambda i: (jnp.minimum(i, n_steps - 1) // tpb, 0, 0)
    routed = lambda i: (jnp.clip(i - 1, 0, n_steps - 1), 0)
    const = lambda i: (0, 0)
    return pl.pallas_call(
        _post_kernel,
        out_shape=[jax.ShapeDtypeStruct((N_TOK, D_MODEL), F32),
                   jax.ShapeDtypeStruct(((n_steps + PAD_BLOCKS) * XY_ROWS, XY_COLS), BF16),
                   jax.ShapeDtypeStruct((N_TOK, LANES), F32),
                   jax.ShapeDtypeStruct((n_steps * 8, LANES), F32)],
        grid=(n_steps + 1 + PAD_BLOCKS,),
        in_specs=[pl.BlockSpec((tm, D_MODEL), row),
                  pl.BlockSpec((tm, Q_A), row),
                  pl.BlockSpec((tm, W_B), row),
                  pl.BlockSpec((tm, 2 * D_MODEL), row),
                  pl.BlockSpec((None, 1, D_MODEL), per_b),
                  pl.BlockSpec((None, 1, D_MODEL), per_b),
                  pl.BlockSpec((None, 1, D_MODEL), per_b),
                  pl.BlockSpec((1, D_MODEL), const),
                  pl.BlockSpec(wa.shape, const),
                  pl.BlockSpec(wb.shape, const),
                  pl.BlockSpec(wo.shape, const),
                  pl.BlockSpec(wr2.shape, const),
                  pl.BlockSpec((1, LANES), const)],
        out_specs=[pl.BlockSpec((tm, D_MODEL), row),
                   pl.BlockSpec((XY_ROWS, XY_COLS), lambda i: (jnp.maximum(i - 1, 0), 0)),
                   pl.BlockSpec((tm, LANES), routed),
                   pl.BlockSpec((8, LANES), routed)],
        scratch_shapes=[pltpu.VMEM((tm, D_MODEL), BF16),
                        pltpu.VMEM((tm, LANES), F32)],
        compiler_params=_cparams(1),
        name="post",
    )(x2, oa, ob, gates, gate_m, scale_f, shift_f, g_ffn, wa, wb, wo, wr2, b_r)


def _experts_kernel(te_ref, nu_ref, ur_ref, ne_ref, ep_ref, xy_in, wg_hbm, wu_hbm, wd_hbm, xy_out,
                    xbuf, ybuf, wg_s, wu_s, wd_s, wg_f, wu_f, wd_f, gsem, ssem, wsem):
    del xy_in
    r = pl.program_id(0)
    last = pl.num_programs(0) - 1
    n_used = nu_ref[0]
    slot = lax.rem(r, 2)

    def unit_row(step, s):
        return pl.multiple_of(ur_ref[step * UNITS_PER_TILE + s], UNIT)

    def start_gathers(step, sl):
        for s in range(UNITS_PER_TILE):
            pltpu.make_async_copy(xy_out.at[pl.ds(unit_row(step, s), UNIT), :],
                                  xbuf.at[sl, pl.ds(s * UNIT, UNIT), :], gsem.at[sl]).start()

    def wait_gathers(sl):
        pltpu.make_async_copy(xy_out.at[pl.ds(0, TM_EXP), :], xbuf.at[sl], gsem.at[sl]).wait()

    def start_scatters(step, sl):
        for s in range(UNITS_PER_TILE):
            pltpu.make_async_copy(ybuf.at[sl, pl.ds(s * UNIT, UNIT), :],
                                  xy_out.at[pl.ds(unit_row(step, s), UNIT), pl.ds(0, D_MODEL)],
                                  ssem.at[sl]).start()

    def wait_scatters(sl):
        pltpu.make_async_copy(ybuf.at[sl], xy_out.at[pl.ds(0, TM_EXP), pl.ds(0, D_MODEL)], ssem.at[sl]).wait()

    @pl.when(r == 0)
    def _():
        start_gathers(0, 0)

    @pl.when(jnp.logical_and(r < n_used, r >= 2))
    def _():
        wait_scatters(slot)

    def weight_copies(e, p):
        return [pltpu.make_async_copy(src.at[e], dst.at[p], wsem.at[p])
                for src, dst in ((wg_hbm, wg_f), (wu_hbm, wu_f), (wd_hbm, wd_f))]

    @pl.when(jnp.logical_and(r < n_used,
                             jnp.logical_or(r == 0, te_ref[r] != te_ref[jnp.maximum(r - 1, 0)])))
    def _():
        e = te_ref[r]
        p = ep_ref[r]

        @pl.when(r == 0)
        def _():
            for cp in weight_copies(e, p):
                cp.start()

        for cp in weight_copies(e, p):
            cp.wait()
        wg_s[...] = wg_f[p].astype(BF16)
        wu_s[...] = wu_f[p].astype(BF16)
        wd_s[...] = wd_f[p].astype(BF16)

        @pl.when(ne_ref[r] >= 0)
        def _():
            for cp in weight_copies(ne_ref[r], 1 - p):
                cp.start()

    @pl.when(r < n_used)
    def _():
        wait_gathers(slot)
        start_gathers(jnp.minimum(r + 1, last), 1 - slot)
        x = xbuf[slot, :, 0:D_MODEL]
        side = xbuf[slot, :, D_MODEL:XY_COLS].astype(F32)
        lane = lax.broadcasted_iota(jnp.int32, side.shape, 1)

        def lanes_sum(a, b):
            return jnp.sum(jnp.where((lane == a) | (lane == b), side, 0.0), axis=-1, keepdims=True)

        is_slot1 = lanes_sum(E1_LANE, E1_LANE) == te_ref[r].astype(F32)
        wrow = jnp.where(is_slot1, lanes_sum(*W1_LANES), lanes_sum(*W2_LANES))
        a = jnp.dot(x, wg_s[...], preferred_element_type=F32)
        u = jnp.dot(x, wu_s[...], preferred_element_type=F32)
        hid = (a * jax.nn.sigmoid(a) * u * wrow).astype(BF16)
        ybuf[slot] = jnp.dot(hid, wd_s[...], preferred_element_type=F32).astype(BF16)
        start_scatters(r, slot)

    @pl.when(r == n_used - 1)
    def _():
        wait_gathers(1 - slot)
        wait_scatters(slot)

        @pl.when(r >= 1)
        def _():
            wait_scatters(1 - slot)


def _experts(tile_expert, n_used, unit_rows, next_expert, expert_parity, xy, wg, wu, wd):
    grid_spec = pltpu.PrefetchScalarGridSpec(
        num_scalar_prefetch=5,
        grid=(N_EXP_TILES,),
        in_specs=[pl.BlockSpec(memory_space=pl.ANY),
                  pl.BlockSpec(memory_space=pl.ANY),
                  pl.BlockSpec(memory_space=pl.ANY),
                  pl.BlockSpec(memory_space=pl.ANY)],
        out_specs=pl.BlockSpec(memory_space=pl.ANY),
        scratch_shapes=[pltpu.VMEM((2, TM_EXP, XY_COLS), BF16),
                        pltpu.VMEM((2, TM_EXP, D_MODEL), BF16),
                        pltpu.VMEM((D_MODEL, D_FF_EXPERT), BF16),
                        pltpu.VMEM((D_MODEL, D_FF_EXPERT), BF16),
                        pltpu.VMEM((D_FF_EXPERT, D_MODEL), BF16),
                        pltpu.VMEM((2, D_MODEL, D_FF_EXPERT), F32),
                        pltpu.VMEM((2, D_MODEL, D_FF_EXPERT), F32),
                        pltpu.VMEM((2, D_FF_EXPERT, D_MODEL), F32),
                        pltpu.SemaphoreType.DMA((2,)),
                        pltpu.SemaphoreType.DMA((2,)),
                        pltpu.SemaphoreType.DMA((2,))])
    return pl.pallas_call(
        _experts_kernel,
        out_shape=jax.ShapeDtypeStruct(xy.shape, xy.dtype),
        grid_spec=grid_spec,
        input_output_aliases={5: 0},
        compiler_params=_cparams(1),
        name="experts",
    )(tile_expert, n_used, unit_rows, next_expert, expert_parity, xy, wg, wu, wd)


def _combine_kernel(x1_ref, rc_ref, gf_ref, gfin_ref, y_ref, o_ref):
    lp1 = rc_ref[:, 0:1]
    lp2 = rc_ref[:, 1:2]
    scol = lax.broadcasted_iota(jnp.int32, (TM_ROW, XY_ROWS), 1).astype(F32)
    pick = jnp.where((scol == lp1) | (scol == lp2), 1.0, 0.0).astype(BF16)
    y = jnp.dot(pick, y_ref[...], preferred_element_type=F32)
    xf = x1_ref[...] + gf_ref[...] * y
    rs = lax.rsqrt(jnp.mean(xf * xf, axis=-1, keepdims=True) + EPS)
    o_ref[...] = xf * rs * gfin_ref[...]


def _combine(x1, rcol, gate_f, g_final, xy):
    tm = TM_ROW
    tpb = SEQ // tm
    row = lambda i: (i, 0)
    return pl.pallas_call(
        _combine_kernel,
        out_shape=jax.ShapeDtypeStruct((N_TOK, D_MODEL), F32),
        grid=(N_TOK // tm,),
        in_specs=[pl.BlockSpec((tm, D_MODEL), row),
                  pl.BlockSpec((tm, LANES), row),
                  pl.BlockSpec((None, 1, D_MODEL), lambda i: (i // tpb, 0, 0)),
                  pl.BlockSpec((1, D_MODEL), lambda i: (0, 0)),
                  pl.BlockSpec((XY_ROWS, D_MODEL), row)],
        out_specs=pl.BlockSpec((tm, D_MODEL), row),
        compiler_params=_cparams(1),
        name="combine",
    )(x1, rcol, gate_f, g_final, xy)


def _t5_bucket_np():
    qi = np.arange(BLOCK)[:, None]
    kj = np.arange(2 * BLOCK)[None, :]
    dist = qi - kj + BLOCK
    n = np.maximum(dist, 0)
    max_exact = NUM_BUCKETS // 2
    nf = np.maximum(n, 1).astype(np.float32)
    large = max_exact + (np.log(nf / np.float32(max_exact)) / np.float32(math.log(MAX_DISTANCE / max_exact))
                         * np.float32(NUM_BUCKETS - max_exact)).astype(np.int32)
    large = np.minimum(large, NUM_BUCKETS - 1)
    bucket = np.where(n < max_exact, n, large)
    band = (dist >= 0) & (dist < WINDOW)
    return bucket.astype(np.int32), band


def kernel(x, c, w_ada, b_ada, g_norm_mix, g_norm_ffn, w_in, sinks, b_forget, w_proj_swa, w_proj_fox,
           w_out, rel_bias_table, w_router_group, b_router_group, w_router_expert, b_router_expert,
           w_gate_exp, w_up_exp, w_down_exp, g_final):
    l = 0
    x2 = x.reshape(N_TOK, D_MODEL)

    c16 = jnp.concatenate([c, jnp.zeros_like(c)], axis=0)
    mod = _ada(c16, w_ada[l], b_ada[l][None, :])[:BATCH]
    shift_m, scale_m, gate_m, shift_f, scale_f, gate_f = [
        m.reshape(BATCH, 1, D_MODEL) for m in jnp.split(mod, 6, axis=-1)]

    w = w_in[l]
    o_ka, o_va, o_qb = Q_A, Q_A + KV_A, Q_A + 2 * KV_A
    o_kb, o_vb, o_f = o_qb + W_B, o_qb + 2 * W_B, o_qb + 3 * W_B
    o_g = o_f + N_HEADS_FOX

    def dup(cols):
        heads = [cols[:, h * HEAD_DIM:(h + 1) * HEAD_DIM] for h in range(N_KV_HEADS_SWA)]
        return jnp.concatenate([hd for hd in heads for _ in range(2)], axis=1)

    w_main = jnp.concatenate([w[:, :Q_A], dup(w[:, o_ka:o_va]), dup(w[:, o_va:o_qb]),
                              w[:, o_qb:o_f]], axis=1).astype(BF16)
    carrier = DECAY_LANES * N_HEADS_FOX
    w_f = jnp.pad(jnp.repeat(w[:, o_f:o_g], DECAY_LANES, axis=1), ((0, 0), (0, LANES - carrier))).astype(BF16)
    w_g = w[:, o_g:].astype(BF16)
    qa, kdup, vdup, qb, kb, vb, f_pad, gates, nrm = _inproj(
        x2, scale_m, shift_m, g_norm_mix[l][None, :], w_main, w_f, w_g)

    b_pad = jnp.pad(jnp.repeat(b_forget[l], DECAY_LANES), (0, LANES - carrier))[None, :]
    lanes = np.arange(LANES)
    jmod = jnp.asarray(np.where(lanes < DECAY_LANES * N_HEADS_FOX, lanes % DECAY_LANES, 7)[None, :].astype(np.int32))
    dq, dk, fb = _cum(f_pad, b_pad, jmod)

    bucket, band = _t5_bucket_np()
    onehot = jnp.asarray(bucket[None] == np.arange(NUM_BUCKETS)[:, None, None], dtype=F32)
    bias = jnp.einsum("bh,bqk->hqk", rel_bias_table.astype(F32), onehot, precision=HIGHEST)
    bias = jnp.where(band[None], bias, NEG_INF)
    first = np.arange(2 * BLOCK)[None, None, :] < BLOCK
    bias = jnp.stack([jnp.where(first, NEG_INF, bias), bias]).reshape(2, N_KV_HEADS_SWA, -1, 2 * BLOCK)
    o_a = _swa(sinks[l].astype(F32), qa, kdup, vdup, bias)

    o_b = _fox(_fox_first_tiles(nrm, fb), qb, kb, vb, dq, dk)

    w_r = jnp.concatenate([w_router_group[l]] + [w_router_expert[l][g] for g in range(N_GROUPS)], axis=1)
    w_r = jnp.pad(w_r, ((0, 0), (0, LANES - w_r.shape[1])))
    wr_hi = w_r.astype(BF16)
    wr_lo = (w_r - wr_hi.astype(F32)).astype(BF16)
    wr2 = jnp.concatenate([wr_hi, wr_lo], axis=1)
    b_r = jnp.concatenate([b_router_group[l], b_router_expert[l].reshape(-1)])
    b_r = jnp.pad(b_r, (0, LANES - b_r.shape[0]))[None, :]
    x1, xy, rcol, cu = _post(x2, o_a, o_b, gates, gate_m, scale_f, shift_f, g_norm_ffn[l][None, :],
                             w_proj_swa[l].astype(BF16), w_proj_fox[l].astype(BF16), w_out[l].astype(BF16),
                             wr2, b_r)

    i32 = jnp.int32
    n_tok_tiles = N_TOK // TM_POST
    cu = cu.reshape(n_tok_tiles, 8, LANES)[:, 0, :N_EXPERTS].astype(i32)
    loc_u = jnp.cumsum(cu, axis=1) - cu
    cend = jnp.cumsum(cu, axis=0)
    cstart = cend - cu
    tot_u = cend[-1]
    tiles_e = (tot_u + UNITS_PER_TILE - 1) // UNITS_PER_TILE
    tile_end = jnp.cumsum(tiles_e)
    tile_start = tile_end - tiles_e
    r = jnp.arange(N_EXP_TILES, dtype=i32)
    tile_expert = jnp.minimum(jnp.sum((tile_end[None, :] <= r[:, None]).astype(i32), axis=1), N_EXPERTS - 1)
    sel_e = tile_expert[:, None] == jnp.arange(N_EXPERTS, dtype=i32)[None, :]
    tw = r - jnp.sum(jnp.where(sel_e, tile_start[None, :], 0), axis=1)
    tot_r = jnp.sum(jnp.where(sel_e, tot_u[None, :], 0), axis=1)
    n_used = tile_end[-1:].astype(i32)
    q = tw[:, None] * UNITS_PER_TILE + jnp.arange(UNITS_PER_TILE, dtype=i32)[None, :]

    def of_expert(tab):
        return jnp.sum(jnp.where(sel_e[:, None, :], tab[None, :, :], 0), axis=2)

    cend_r, cstart_r, loc_r = of_expert(cend), of_expert(cstart), of_expert(loc_u)
    src_tile = jnp.minimum(jnp.sum((cend_r[:, None, :] <= q[:, :, None]).astype(i32), axis=2), n_tok_tiles - 1)
    sel_t = src_tile[:, :, None] == jnp.arange(n_tok_tiles, dtype=i32)[None, None, :]
    k = (q - jnp.sum(jnp.where(sel_t, cstart_r[:, None, :], 0), axis=2)
         + jnp.sum(jnp.where(sel_t, loc_r[:, None, :], 0), axis=2))
    real_rows = src_tile * XY_ROWS + k * UNIT
    pad_rows = PAD_BASE_ROW + (tile_expert[:, None] * PAD_UNITS_PER_EXPERT + (q - tot_r[:, None])) * UNIT
    idle_row = PAD_BASE_ROW + N_EXPERTS * PAD_UNITS_PER_EXPERT * UNIT
    unit_rows = jnp.where(q < tot_r[:, None], real_rows, pad_rows)
    unit_rows = jnp.where((r < n_used)[:, None], unit_rows, idle_row).reshape(-1).astype(i32)

    eid = jnp.arange(N_EXPERTS, dtype=i32)
    used = tiles_e > 0
    later_used = (eid[None, :] > eid[:, None]) & used[None, :]
    next_e = jnp.min(jnp.where(later_used, eid[None, :], N_EXPERTS), axis=1)
    next_e = jnp.where(next_e == N_EXPERTS, -1, next_e)
    parity_e = (jnp.cumsum(used.astype(i32)) - used.astype(i32)) % 2
    next_expert = jnp.sum(jnp.where(sel_e, next_e[None, :], 0), axis=1).astype(i32)
    expert_parity = jnp.sum(jnp.where(sel_e, parity_e[None, :], 0), axis=1).astype(i32)

    xy = _experts(tile_expert.astype(i32), n_used, unit_rows, next_expert, expert_parity, xy,
                  w_gate_exp[l].reshape(N_EXPERTS, D_MODEL, D_FF_EXPERT),
                  w_up_exp[l].reshape(N_EXPERTS, D_MODEL, D_FF_EXPERT),
                  w_down_exp[l].reshape(N_EXPERTS, D_FF_EXPERT, D_MODEL))
    out = _combine(x1, rcol, gate_f, g_final[None, :], xy)
    return out.reshape(BATCH, SEQ, D_MODEL)
```

```python
import functools
import math

import numpy as np
import jax
import jax.numpy as jnp
from jax import lax
from jax.experimental import pallas as pl
from jax.experimental.pallas import tpu as pltpu

F32 = jnp.float32
BF16 = jnp.bfloat16
HIGHEST = lax.Precision.HIGHEST

D_MODEL = 1024
BATCH = 8
SEQ = 4096
N_TOK = BATCH * SEQ
N_HEADS_SWA = 8
N_KV_HEADS_SWA = 2
N_HEADS_FOX = 8
HEAD_DIM = 64
WINDOW = 128
BLOCK = 128
NUM_BUCKETS = 32
MAX_DISTANCE = 128
N_GROUPS = 4
EXPERTS_PER_GROUP = 8
N_EXPERTS = N_GROUPS * EXPERTS_PER_GROUP
D_FF_EXPERT = 256
EPS = 1e-6
NEG_INF = -1e30

Q_A = N_HEADS_SWA * HEAD_DIM
KV_A = N_KV_HEADS_SWA * HEAD_DIM
W_B = N_HEADS_FOX * HEAD_DIM
LANES = 128
QK_SCALE = HEAD_DIM ** -0.5

TM_IN = 512
TM_POST = 512
TQ_FOX = 512
TK_FOX = TQ_FOX
SWA_BLOCKS = 4
TM_EXP = 512
TM_ROW = 512
UNIT = 16
XY_UNITS = 2 * TM_POST // UNIT + N_EXPERTS
XY_ROWS = XY_UNITS * UNIT
XY_COLS = D_MODEL + LANES
UNITS_PER_TILE = TM_EXP // UNIT
N_TOK_TILES = N_TOK // TM_POST
N_EXP_TILES = N_TOK_TILES * XY_UNITS // UNITS_PER_TILE + N_EXPERTS
PAD_UNITS_PER_EXPERT = UNITS_PER_TILE - 1
PAD_BLOCKS = -(-(N_EXPERTS * PAD_UNITS_PER_EXPERT * UNIT) // XY_ROWS)
PAD_BASE_ROW = N_TOK_TILES * XY_ROWS
W1_LANES, W2_LANES, E1_LANE, E2_LANE = (4, 6), (5, 7), 8, 9
VMEM_LIMIT = 56 * 1024 * 1024

DECAY_LANES = 6
PRUNE_MARGIN = 110.0


def _cparams(n_axes):
    return pltpu.CompilerParams(dimension_semantics=("arbitrary",) * n_axes,
                                vmem_limit_bytes=VMEM_LIMIT)


def _ada_kernel(c_ref, w_ref, b_ref, o_ref):
    c = c_ref[...]
    ca = c * jax.nn.sigmoid(c)
    o_ref[...] = jnp.dot(ca.astype(BF16), w_ref[...].astype(BF16),
                         preferred_element_type=F32) + b_ref[...]


def _ada(c16, w_ada, b_ada):
    n_out = w_ada.shape[1]
    blk = 1024
    return pl.pallas_call(
        _ada_kernel,
        out_shape=jax.ShapeDtypeStruct((16, n_out), F32),
        grid=(n_out // blk,),
        in_specs=[pl.BlockSpec((16, D_MODEL), lambda j: (0, 0)),
                  pl.BlockSpec((D_MODEL, blk), lambda j: (0, j)),
                  pl.BlockSpec((1, blk), lambda j: (0, j))],
        out_specs=pl.BlockSpec((16, blk), lambda j: (0, j)),
        compiler_params=_cparams(1),
        name="ada",
    )(c16, w_ada, b_ada)


def _inproj_kernel(x_ref, sc_ref, sh_ref, g_ref, wm_ref, wf_ref, wg_ref, ind_ref,
                   qa_ref, kd_ref, vd_ref, qb_ref, kb_ref, vb_ref, f_ref, gt_ref, nrm_ref):
    x = x_ref[...]
    rs = lax.rsqrt(jnp.mean(x * x, axis=-1, keepdims=True) + EPS)
    a = g_ref[...] * (1.0 + sc_ref[...])
    h = (x * rs * a + sh_ref[...]).astype(BF16)

    def mm(w):
        return jnp.dot(h, w, preferred_element_type=F32)

    qa_ref[...] = (mm(wm_ref[:, 0:512]) * QK_SCALE).astype(BF16)
    kd_ref[...] = mm(wm_ref[:, 512:768]).astype(BF16)
    vd_ref[...] = mm(wm_ref[:, 768:1024]).astype(BF16)
    qb = (mm(wm_ref[:, 1024:1536]) * QK_SCALE).astype(BF16)
    kb = mm(wm_ref[:, 1536:2048]).astype(BF16)
    qb_ref[...] = qb
    kb_ref[...] = kb
    vb_ref[...] = mm(wm_ref[:, 2048:2560]).astype(BF16)
    sq = jnp.concatenate([qb, kb], axis=1).astype(F32)
    seg = jnp.dot((sq * sq).astype(BF16), ind_ref[...], preferred_element_type=F32)
    nrm_ref[...] = jnp.broadcast_to(jnp.max(seg, axis=0, keepdims=True), nrm_ref.shape)
    f_ref[...] = mm(wf_ref[...])
    gt_ref[...] = mm(wg_ref[...]).astype(BF16)


def _inproj(x2, scale_m, shift_m, g_mix, w_main, w_f, w_g):
    tm = TM_IN
    tpb = SEQ // tm
    row = lambda i: (i, 0)
    per_b = lambda i: (i // tpb, 0, 0)
    const = lambda i: (0, 0)
    outs = [(Q_A, BF16), (2 * KV_A, BF16), (2 * KV_A, BF16), (W_B, BF16), (W_B, BF16), (W_B, BF16),
            (LANES, F32), (2 * D_MODEL, BF16)]
    ind_np = np.zeros((2 * W_B, LANES), np.float32)
    ind_np[np.arange(2 * W_B), np.arange(2 * W_B) // HEAD_DIM] = 1.0
    ind = jnp.asarray(ind_np, dtype=BF16)
    n_steps = N_TOK // tm
    return pl.pallas_call(
        _inproj_kernel,
        out_shape=[jax.ShapeDtypeStruct((N_TOK, w), dt) for w, dt in outs]
        + [jax.ShapeDtypeStruct((n_steps * 8, LANES), F32)],
        grid=(n_steps,),
        in_specs=[pl.BlockSpec((tm, D_MODEL), row),
                  pl.BlockSpec((None, 1, D_MODEL), per_b),
                  pl.BlockSpec((None, 1, D_MODEL), per_b),
                  pl.BlockSpec((1, D_MODEL), const),
                  pl.BlockSpec(w_main.shape, const),
                  pl.BlockSpec(w_f.shape, const),
                  pl.BlockSpec(w_g.shape, const),
                  pl.BlockSpec(ind.shape, const)],
        out_specs=[pl.BlockSpec((tm, w), row) for w, _ in outs] + [pl.BlockSpec((8, LANES), row)],
        compiler_params=_cparams(1),
        name="inproj",
    )(x2, scale_m, shift_m, g_mix, w_main, w_f, w_g, ind)


def _log_sigmoid(x):
    return jnp.minimum(x, 0.0) - jnp.log1p(jnp.exp(-jnp.abs(x)))


def _cum_kernel(f_ref, b_ref, jm_ref, qa_ref, ka_ref, fb_ref):
    cum = _log_sigmoid(f_ref[...] + b_ref[...])
    row = lax.broadcasted_iota(jnp.int32, cum.shape, 0)
    k = 1
    while k < SEQ:
        if k < 8:
            shifted = jnp.where(row >= k, pltpu.roll(cum, k, 0), 0.0)
        else:
            shifted = jnp.concatenate([jnp.zeros((k, LANES), F32), cum[:SEQ - k]], axis=0)
        cum = cum + shifted
        k *= 2
    jm = jm_ref[...]
    for blk in range(SEQ // LANES):
        rows = slice(blk * LANES, (blk + 1) * LANES)
        cb = cum[rows]
        carry = cb[LANES - 1:LANES]
        hi = cb.astype(BF16).astype(F32)
        r1 = cb - hi
        mid = r1.astype(BF16).astype(F32)
        lo = (r1 - mid).astype(BF16).astype(F32)
        one = jnp.ones_like(cb)
        zero = jnp.zeros_like(cb)
        qa = jnp.where(jm == 0, hi, jnp.where(jm == 1, mid, jnp.where(jm == 2, lo,
                       jnp.where(jm < DECAY_LANES, one, zero))))
        ka = jnp.where(jm == 3, -hi, jnp.where(jm == 4, -mid, jnp.where(jm == 5, -lo,
                       jnp.where(jm < 3, one, zero))))
        qa_ref[rows, :] = qa.astype(BF16)
        ka_ref[rows, :] = ka.astype(BF16)
        blocks_per_tile = TQ_FOX // LANES
        tile = blk // blocks_per_tile
        if blk % blocks_per_tile == 0:
            fb_ref[2 * tile:2 * tile + 1, :] = cb[0:1]
        if blk % blocks_per_tile == blocks_per_tile - 1:
            fb_ref[2 * tile + 1:2 * tile + 2, :] = carry


def _cum(f_pad, b_pad, jmod):
    n_tiles = SEQ // TQ_FOX
    return pl.pallas_call(
        _cum_kernel,
        out_shape=[jax.ShapeDtypeStruct((BATCH, SEQ, LANES), BF16)] * 2
        + [jax.ShapeDtypeStruct((BATCH, 2 * n_tiles, LANES), F32)],
        grid=(BATCH,),
        in_specs=[pl.BlockSpec((SEQ, LANES), lambda b: (b, 0)),
                  pl.BlockSpec((1, LANES), lambda b: (0, 0)),
                  pl.BlockSpec((1, LANES), lambda b: (0, 0))],
        out_specs=[pl.BlockSpec((None, SEQ, LANES), lambda b: (b, 0, 0))] * 2
        + [pl.BlockSpec((None, 2 * n_tiles, LANES), lambda b: (b, 0, 0))],
        compiler_params=_cparams(1),
        name="cum",
    )(f_pad, b_pad, jmod)


def _swa_block(sink_cols, q, kk, vv, bias_ref, lo):
    tiles = []
    for g in range(N_KV_HEADS_SWA):
        parts = []
        for t in range(2):
            qt = q[:, (2 * g + t) * LANES:(2 * g + t + 1) * LANES]
            zero = jnp.zeros_like(qt)
            parts.append(jnp.where(lo, qt, zero))
            parts.append(jnp.where(lo, zero, qt))
        q4 = jnp.concatenate(parts, axis=0)
        s = lax.dot_general(q4, kk[:, g * LANES:(g + 1) * LANES], (((1,), (1,)), ((), ())),
                            preferred_element_type=F32)
        s = s + bias_ref[g]
        sink = sink_cols[g]
        m = jnp.maximum(jnp.max(s, axis=-1, keepdims=True), sink)
        p = jnp.exp(s - m)
        den = jnp.sum(p, axis=-1, keepdims=True) + jnp.exp(sink - m)
        o = jnp.dot(p.astype(BF16), vv[:, g * LANES:(g + 1) * LANES],
                    preferred_element_type=F32) / den
        tiles.append(jnp.where(lo, o[0:BLOCK], o[BLOCK:2 * BLOCK]))
        tiles.append(jnp.where(lo, o[2 * BLOCK:3 * BLOCK], o[3 * BLOCK:4 * BLOCK]))
    return tiles


def _swa_kernel(sink_ref, q_ref, kc_ref, kp_ref, vc_ref, vp_ref, bias_first_ref, bias_ref, o_ref):
    lane = lax.broadcasted_iota(jnp.int32, (BLOCK, LANES), 1)
    lo = lane < HEAD_DIM
    grp = N_HEADS_SWA // N_KV_HEADS_SWA
    row = lax.broadcasted_iota(jnp.int32, (grp * BLOCK, 1), 0)
    sink_cols = []
    for g in range(N_KV_HEADS_SWA):
        col = jnp.full((grp * BLOCK, 1), sink_ref[g * grp + grp - 1], F32)
        for hh in range(grp - 2, -1, -1):
            col = jnp.where(row < (hh + 1) * BLOCK, sink_ref[g * grp + hh], col)
        sink_cols.append(col)
    for blk in range(SWA_BLOCKS):
        rows = slice(blk * BLOCK, (blk + 1) * BLOCK)
        if blk == 0:
            kk = jnp.concatenate([kp_ref[...], kc_ref[rows, :]], axis=0)
            vv = jnp.concatenate([vp_ref[...], vc_ref[rows, :]], axis=0)
            bias = bias_first_ref
        else:
            prev_rows = slice((blk - 1) * BLOCK, (blk + 1) * BLOCK)
            kk = kc_ref[prev_rows, :]
            vv = vc_ref[prev_rows, :]
            bias = bias_ref
        tiles = _swa_block(sink_cols, q_ref[rows, :], kk, vv, bias, lo)
        for c, tile in enumerate(tiles):
            o_ref[rows, c * LANES:(c + 1) * LANES] = tile.astype(BF16)


def _swa(sinks, qa, kdup, vdup, bias):
    nb = SEQ // BLOCK
    ns = nb // SWA_BLOCKS
    cur = lambda b, i, s: (b * ns + i, 0)
    prev = lambda b, i, s: (b * nb + jnp.maximum(SWA_BLOCKS * i - 1, 0), 0)
    grid_spec = pltpu.PrefetchScalarGridSpec(
        num_scalar_prefetch=1,
        grid=(BATCH, ns),
        in_specs=[pl.BlockSpec((SWA_BLOCKS * BLOCK, Q_A), cur),
                  pl.BlockSpec((SWA_BLOCKS * BLOCK, 2 * KV_A), cur),
                  pl.BlockSpec((BLOCK, 2 * KV_A), prev),
                  pl.BlockSpec((SWA_BLOCKS * BLOCK, 2 * KV_A), cur),
                  pl.BlockSpec((BLOCK, 2 * KV_A), prev),
                  pl.BlockSpec((None,) + bias.shape[1:], lambda b, i, s: (jnp.minimum(i, 1), 0, 0, 0)),
                  pl.BlockSpec((None,) + bias.shape[1:], lambda b, i, s: (1, 0, 0, 0))],
        out_specs=pl.BlockSpec((SWA_BLOCKS * BLOCK, Q_A), cur))
    return pl.pallas_call(
        _swa_kernel,
        out_shape=jax.ShapeDtypeStruct((N_TOK, Q_A), BF16),
        grid_spec=grid_spec,
        compiler_params=_cparams(2),
        name="swa",
    )(sinks, qa, kdup, kdup, vdup, vdup, bias, bias)


def _fox_kernel(js_ref, q_ref, k_ref, v_ref, qa_ref, ka_ref, o_ref,
                kaug, vaug, q2, m_sc, acc_sc, s_a, s_b):
    tq, tk = TQ_FOX, TK_FOX
    b = pl.program_id(0)
    t = pl.program_id(1)
    i = pl.program_id(2)
    j_start = js_ref[(b * pl.num_programs(1) + t) * pl.num_programs(2) + i]

    @pl.when(i == 0)
    def _():
        kaug[:, 0:LANES] = k_ref[...]
        kaug[:, LANES:2 * LANES] = ka_ref[...]
        vaug[:, 0:LANES] = v_ref[...]
        vaug[:, LANES:2 * LANES] = jnp.ones((SEQ, LANES), BF16)

    lane = lax.broadcasted_iota(jnp.int32, (tq, LANES), 1)
    lo = lane < HEAD_DIM
    base = 2 * DECAY_LANES * t
    own = [(lane >= base + h * DECAY_LANES) & (lane < base + (h + 1) * DECAY_LANES) for h in range(2)]
    q = q_ref[...]
    qa = qa_ref[...]
    zero = jnp.zeros_like(q)
    q2[0, :, 0:LANES] = jnp.where(lo, q, zero)
    q2[1, :, 0:LANES] = jnp.where(lo, zero, q)
    for h in range(2):
        q2[h, :, LANES:2 * LANES] = jnp.where(own[h], qa, zero)
    m_sc[...] = jnp.full(m_sc.shape, NEG_INF, F32)
    acc_sc[...] = jnp.zeros(acc_sc.shape, F32)

    def scores(h, ks):
        return lax.dot_general(q2[h], kaug[pl.ds(ks, tk), :], (((1,), (1,)), ((), ())),
                               preferred_element_type=F32)

    def consume(h, s, ks, mask):
        if mask is not None:
            s = jnp.where(mask, s, NEG_INF)
        m_prev = m_sc[h]
        m_new = jnp.maximum(m_prev, jnp.max(s, axis=-1, keepdims=True))
        alpha = jnp.exp(m_prev - m_new)
        p = jnp.exp(s - jnp.concatenate([m_new] * (tk // LANES), axis=1))
        pv = jnp.dot(p.astype(BF16), vaug[pl.ds(ks, tk), :], preferred_element_type=F32)
        acc_sc[h] = jnp.concatenate([alpha, alpha], axis=1) * acc_sc[h] + pv
        m_sc[h] = m_new

    def key_start(j):
        return pl.multiple_of(j * tk, tk)

    def scores_into(buf, j):
        for h in range(2):
            buf[h] = scores(h, key_start(j))

    def consume_from(buf, j, mask):
        for h in range(2):
            consume(h, buf[h], key_start(j), mask)

    rr = lax.broadcasted_iota(jnp.int32, (tq, tk), 0)
    cc = lax.broadcasted_iota(jnp.int32, (tq, tk), 1)
    causal = cc <= rr
    n_full = i - j_start

    scores_into(s_a, j_start)

    def pair(p, carry):
        j = j_start + 2 * p
        scores_into(s_b, j + 1)
        consume_from(s_a, j, None)
        scores_into(s_a, j + 2)
        consume_from(s_b, j + 1, None)
        return carry

    lax.fori_loop(0, n_full // 2, pair, 0)
    odd = lax.rem(n_full, 2) == 1

    @pl.when(odd)
    def _():
        scores_into(s_b, i)
        consume_from(s_a, i - 1, None)
        consume_from(s_b, i, causal)

    @pl.when(jnp.logical_not(odd))
    def _():
        consume_from(s_a, i, causal)

    outs = [acc_sc[h, :, 0:LANES] / acc_sc[h, :, LANES:2 * LANES] for h in range(2)]
    o_ref[...] = jnp.where(lo, outs[0], outs[1]).astype(BF16)


def _fox(j_start, qb, kb, vb, qa, ka):
    tq = TQ_FOX
    nq = SEQ // tq
    n_pairs = N_HEADS_FOX // 2
    qmap = lambda b, t, i, js: (b * nq + i, t)
    kmap = lambda b, t, i, js: (b, t)
    grid_spec = pltpu.PrefetchScalarGridSpec(
        num_scalar_prefetch=1,
        grid=(BATCH, n_pairs, nq),
        in_specs=[pl.BlockSpec((tq, LANES), qmap),
                  pl.BlockSpec((SEQ, LANES), kmap),
                  pl.BlockSpec((SEQ, LANES), kmap),
                  pl.BlockSpec((None, tq, LANES), lambda b, t, i, js: (b, i, 0)),
                  pl.BlockSpec((None, SEQ, LANES), lambda b, t, i, js: (b, 0, 0))],
        out_specs=pl.BlockSpec((tq, LANES), qmap),
        scratch_shapes=[pltpu.VMEM((SEQ, 2 * LANES), BF16),
                        pltpu.VMEM((SEQ, 2 * LANES), BF16),
                        pltpu.VMEM((2, tq, 2 * LANES), BF16),
                        pltpu.VMEM((2, tq, LANES), F32),
                        pltpu.VMEM((2, tq, 2 * LANES), F32),
                        pltpu.VMEM((2, tq, TK_FOX), F32),
                        pltpu.VMEM((2, tq, TK_FOX), F32)])
    return pl.pallas_call(
        _fox_kernel,
        out_shape=jax.ShapeDtypeStruct((N_TOK, W_B), BF16),
        grid_spec=grid_spec,
        compiler_params=_cparams(3),
        name="fox",
    )(j_start, qb, kb, vb, qa, ka)


def _fox_flat_kernel(js_ref, ni_ref, q_ref, k_ref, v_ref, qa_ref, ka_ref, o_ref,
                     kaug, vaug, q2, m_sc, acc_sc, s0, s1, mask_tbl):
    tq, tk = TQ_FOX, TK_FOX
    nq = SEQ // tq
    b = pl.program_id(0)
    t = pl.program_id(1)
    pair = b * pl.num_programs(1) + t
    first_tile = lambda i: js_ref[pair * nq + i]

    kaug[:, 0:LANES] = k_ref[...]
    kaug[:, LANES:2 * LANES] = ka_ref[...]
    vaug[:, 0:LANES] = v_ref[...]
    vaug[:, LANES:2 * LANES] = jnp.ones((SEQ, LANES), BF16)

    lane = lax.broadcasted_iota(jnp.int32, (tq, LANES), 1)
    lo = lane < HEAD_DIM
    base = 2 * DECAY_LANES * t
    own = [(lane >= base + h * DECAY_LANES) & (lane < base + (h + 1) * DECAY_LANES) for h in range(2)]
    for it in range(nq):
        rows = slice(it * tq, (it + 1) * tq)
        q = q_ref[rows, :]
        qa = qa_ref[rows, :]
        zero = jnp.zeros_like(q)
        q2[0, rows, 0:LANES] = jnp.where(lo, q, zero)
        q2[1, rows, 0:LANES] = jnp.where(lo, zero, q)
        for h in range(2):
            q2[h, rows, LANES:2 * LANES] = jnp.where(own[h], qa, zero)
    rr = lax.broadcasted_iota(jnp.int32, (tq, tk), 0)
    cc = lax.broadcasted_iota(jnp.int32, (tq, tk), 1)
    mask_tbl[0] = jnp.zeros((tq, tk), F32)
    mask_tbl[1] = jnp.where(cc <= rr, 0.0, NEG_INF)
    m_sc[...] = jnp.full(m_sc.shape, NEG_INF, F32)
    acc_sc[...] = jnp.zeros(acc_sc.shape, F32)

    def scores(h, i, j):
        return lax.dot_general(q2[h, pl.ds(pl.multiple_of(i * tq, tq), tq), :],
                               kaug[pl.ds(pl.multiple_of(j * tk, tk), tk), :],
                               (((1,), (1,)), ((), ())), preferred_element_type=F32)

    def consume(h, sbuf, i, j):
        s = sbuf[...] + mask_tbl[(i == j).astype(jnp.int32)]
        m_prev = jnp.where(j == first_tile(i), NEG_INF, m_sc[h])
        m_new = jnp.maximum(m_prev, jnp.max(s, axis=-1, keepdims=True))
        alpha = jnp.exp(m_prev - m_new)
        p = jnp.exp(s - jnp.concatenate([m_new] * (tk // LANES), axis=1))
        pv = jnp.dot(p.astype(BF16), vaug[pl.ds(pl.multiple_of(j * tk, tk), tk), :],
                     preferred_element_type=F32)
        acc_sc[h] = jnp.concatenate([alpha, alpha], axis=1) * acc_sc[h] + pv
        m_sc[h] = m_new

    def body(n, carry):
        i, j = carry
        wrap = j == i
        i_next = jnp.minimum(jnp.where(wrap, i + 1, i), nq - 1)
        j_next = jnp.where(wrap, first_tile(i_next), j + 1)
        s1[...] = scores(1, i, j)
        consume(0, s0, i, j)
        s0[...] = scores(0, i_next, j_next)
        consume(1, s1, i, j)

        @pl.when(wrap)
        def _():
            outs = [acc_sc[h, :, 0:LANES] / acc_sc[h, :, LANES:2 * LANES] for h in range(2)]
            o_ref[pl.ds(pl.multiple_of(i * tq, tq), tq), :] = jnp.where(lo, outs[0], outs[1]).astype(BF16)

        return i_next, j_next

    s0[...] = scores(0, 0, 0)
    lax.fori_loop(0, ni_ref[pair], body, (jnp.int32(0), jnp.int32(0)))


def _fox_flat(j_start, n_items, qb, kb, vb, qa, ka):
    tq = TQ_FOX
    n_pairs = N_HEADS_FOX // 2
    kmap = lambda b, t, js, ni: (b, t)
    bmap = lambda b, t, js, ni: (b, 0, 0)
    grid_spec = pltpu.PrefetchScalarGridSpec(
        num_scalar_prefetch=2,
        grid=(BATCH, n_pairs),
        in_specs=[pl.BlockSpec((SEQ, LANES), kmap),
                  pl.BlockSpec((SEQ, LANES), kmap),
                  pl.BlockSpec((SEQ, LANES), kmap),
                  pl.BlockSpec((None, SEQ, LANES), bmap),
                  pl.BlockSpec((None, SEQ, LANES), bmap)],
        out_specs=pl.BlockSpec((SEQ, LANES), kmap),
        scratch_shapes=[pltpu.VMEM((SEQ, 2 * LANES), BF16),
                        pltpu.VMEM((SEQ, 2 * LANES), BF16),
                        pltpu.VMEM((2, SEQ, 2 * LANES), BF16),
                        pltpu.VMEM((2, tq, LANES), F32),
                        pltpu.VMEM((2, tq, 2 * LANES), F32),
                        pltpu.VMEM((tq, TK_FOX), F32),
                        pltpu.VMEM((tq, TK_FOX), F32),
                        pltpu.VMEM((2, tq, TK_FOX), F32)])
    return pl.pallas_call(
        _fox_flat_kernel,
        out_shape=jax.ShapeDtypeStruct((N_TOK, W_B), BF16),
        grid_spec=grid_spec,
        compiler_params=_cparams(2),
        name="fox",
    )(j_start, n_items, qb, kb, vb, qa, ka)


def _fox_first_tiles(nrm, fb):
    n_tiles = SEQ // TQ_FOX
    nr = nrm.reshape(BATCH, n_tiles, 8, LANES)[:, :, 0, :] * 1.02
    qn = jnp.sqrt(nr[..., 0:N_HEADS_FOX])
    kn = jnp.sqrt(nr[..., N_HEADS_FOX:2 * N_HEADS_FOX])
    f_first = fb[:, 0::2, 0:DECAY_LANES * N_HEADS_FOX:DECAY_LANES]
    f_last = fb[:, 1::2, 0:DECAY_LANES * N_HEADS_FOX:DECAY_LANES]
    kn_prefix = lax.cummax(kn, axis=1)
    upper = qn[:, :, None, :] * kn_prefix[:, None, :, :] + f_first[:, :, None, :] - f_last[:, None, :, :]
    row_max_low = -(qn * kn)[:, :, None, :]
    ii = jnp.arange(n_tiles)[None, :, None, None]
    jj = jnp.arange(n_tiles)[None, None, :, None]
    skip = (upper < row_max_low - PRUNE_MARGIN) & (jj < ii)
    skip = jnp.all(skip.reshape(BATCH, n_tiles, n_tiles, N_HEADS_FOX // 2, 2), axis=-1)
    first = jnp.sum(jnp.cumprod(skip.astype(jnp.int32), axis=2), axis=2)
    return jnp.transpose(first, (0, 2, 1)).reshape(-1).astype(jnp.int32)


def _post_kernel(x_ref, oa_ref, ob_ref, gt_ref, gm_ref, sc_ref, sh_ref, g_ref,
                 wa_ref, wb_ref, wo_ref, wr2_ref, br_ref,
                 x1_ref, xy_ref, rc_ref, cu_ref, hh_prev, lg_prev):
    step = pl.program_id(0)

    @pl.when(step == 0)
    def _():
        hh_prev[...] = jnp.zeros(hh_prev.shape, BF16)
        lg_prev[...] = jnp.zeros(lg_prev.shape, F32)

    @pl.when(step <= N_TOK_TILES)
    def _():
        hh_p = hh_prev[...]
        lg_p = lg_prev[...]
        hh, logits = _post_mix(x_ref, oa_ref, ob_ref, gt_ref, gm_ref, sc_ref, sh_ref, g_ref,
                               wa_ref, wb_ref, wo_ref, wr2_ref, br_ref, x1_ref)
        _post_route(hh_p, lg_p, xy_ref, rc_ref, cu_ref)
        hh_prev[...] = hh
        lg_prev[...] = logits

    @pl.when(step > N_TOK_TILES)
    def _():
        xy_ref[...] = jnp.zeros(xy_ref.shape, BF16)


def _post_mix(x_ref, oa_ref, ob_ref, gt_ref, gm_ref, sc_ref, sh_ref, g_ref,
              wa_ref, wb_ref, wo_ref, wr2_ref, br_ref, x1_ref):
    pa = jnp.dot(oa_ref[...], wa_ref[...], preferred_element_type=F32)
    pb = jnp.dot(ob_ref[...], wb_ref[...], preferred_element_type=F32)
    ga = jax.nn.sigmoid(gt_ref[:, 0:D_MODEL].astype(F32))
    gb = jax.nn.sigmoid(gt_ref[:, D_MODEL:2 * D_MODEL].astype(F32))
    merged = (ga * pa + gb * pb).astype(BF16)
    y = jnp.dot(merged, wo_ref[...], preferred_element_type=F32)
    x1 = x_ref[...] + gm_ref[...] * y
    x1_ref[...] = x1

    rs = lax.rsqrt(jnp.mean(x1 * x1, axis=-1, keepdims=True) + EPS)
    a = g_ref[...] * (1.0 + sc_ref[...])
    h2 = x1 * rs * a + sh_ref[...]

    hh = h2.astype(BF16)
    hl = (h2 - hh.astype(F32)).astype(BF16)
    hi_both = jnp.dot(hh, wr2_ref[...], preferred_element_type=F32)
    logits = (hi_both[:, 0:LANES] + hi_both[:, LANES:2 * LANES]
              + jnp.dot(hl, wr2_ref[:, 0:LANES], preferred_element_type=F32)
              + br_ref[...])
    return hh, logits


def _post_route(hh, logits, xy_ref, rc_ref, cu_ref):
    tm = TM_POST
    lane = lax.broadcasted_iota(jnp.int32, (tm, LANES), 1).astype(F32)
    big = float(LANES)
    gl = jnp.where(lane < N_GROUPS, logits, -jnp.inf)
    gmax = jnp.max(gl, axis=-1, keepdims=True)
    gi = jnp.min(jnp.where(gl == gmax, lane, big), axis=-1, keepdims=True)
    gsum = jnp.sum(jnp.exp(gl - gmax), axis=-1, keepdims=True)
    gp = 1.0 / gsum
    e_lo = N_GROUPS + EXPERTS_PER_GROUP * gi
    el = jnp.where((lane >= e_lo) & (lane < e_lo + EXPERTS_PER_GROUP), logits, -jnp.inf)
    v1 = jnp.max(el, axis=-1, keepdims=True)
    i1 = jnp.min(jnp.where(el == v1, lane, big), axis=-1, keepdims=True)
    el2 = jnp.where(lane == i1, -jnp.inf, el)
    v2 = jnp.max(el2, axis=-1, keepdims=True)
    i2 = jnp.min(jnp.where(el2 == v2, lane, big), axis=-1, keepdims=True)
    e21 = jnp.exp(v2 - v1)
    w1 = gp / (1.0 + e21)
    w2 = gp * e21 / (1.0 + e21)
    e1 = i1 - N_GROUPS
    e2 = i2 - N_GROUPS

    oh = jnp.where((lane == e1) | (lane == e2), 1.0, 0.0)
    cnt_u = jnp.floor((jnp.sum(oh, axis=0, keepdims=True) + (UNIT - 1)) * (1.0 / UNIT))
    r128 = lax.broadcasted_iota(jnp.int32, (LANES, LANES), 0)
    c128 = lax.broadcasted_iota(jnp.int32, (LANES, LANES), 1)
    before_lane = jnp.where(r128 < c128, 1.0, 0.0).astype(BF16)
    loc_u = jnp.dot(jnp.broadcast_to(cnt_u, (8, LANES)).astype(BF16), before_lane,
                    preferred_element_type=F32)
    trow = lax.broadcasted_iota(jnp.int32, (tm, LANES), 0)
    seen = oh
    k = 1
    while k < tm:
        if k < 8:
            shifted = jnp.where(trow >= k, pltpu.roll(seen, k, 0), 0.0)
        else:
            shifted = jnp.concatenate([jnp.zeros((k, LANES), F32), seen[:tm - k]], axis=0)
        seen = seen + shifted
        k *= 2
    pos_e = (seen - oh) + loc_u[0:1] * UNIT
    lp1 = jnp.sum(jnp.where(lane == e1, pos_e, 0.0), axis=-1, keepdims=True)
    lp2 = jnp.sum(jnp.where(lane == e2, pos_e, 0.0), axis=-1, keepdims=True)

    def to_row(col):
        return jnp.transpose(jnp.broadcast_to(col, (tm, LANES)))[0:1]

    srow = lax.broadcasted_iota(jnp.int32, (XY_ROWS, tm), 0).astype(F32)
    pm1 = jnp.where(srow == to_row(lp1), 1.0, 0.0).astype(BF16)
    pm2 = jnp.where(srow == to_row(lp2), 1.0, 0.0).astype(BF16)
    w1h = w1.astype(BF16).astype(F32)
    w2h = w2.astype(BF16).astype(F32)
    side = jnp.where(lane == W1_LANES[0], w1h, jnp.where(lane == W1_LANES[1], w1 - w1h,
           jnp.where(lane == W2_LANES[0], w2h, jnp.where(lane == W2_LANES[1], w2 - w2h,
           jnp.where(lane == E1_LANE, e1, jnp.where(lane == E2_LANE, e2, 0.0))))))
    tok = jnp.concatenate([hh, side.astype(BF16)], axis=1)
    xy_ref[...] = jnp.dot(pm1 + pm2, tok, preferred_element_type=F32).astype(BF16)

    cu_ref[...] = jnp.broadcast_to(cnt_u, cu_ref.shape)
    rc_ref[...] = jnp.where(lane == 0, lp1, jnp.where(lane == 1, lp2, 0.0))


def _post(x2, oa, ob, gates, gate_m, scale_f, shift_f, g_ffn, wa, wb, wo, wr2, b_r):
    tm = TM_POST
    tpb = SEQ // tm
    n_steps = N_TOK_TILES
    row = lambda i: (jnp.minimum(i, n_steps - 1), 0)
    per_b = lambda i: (jnp.minimum(i, n_steps - 1) // tpb, 0, 0)
    routed = lambda i: (jnp.clip(i - 1, 0, n_steps - 1), 0)
    const = lambda i: (0, 0)
    return pl.pallas_call(
        _post_kernel,
        out_shape=[jax.ShapeDtypeStruct((N_TOK, D_MODEL), F32),
                   jax.ShapeDtypeStruct(((n_steps + PAD_BLOCKS) * XY_ROWS, XY_COLS), BF16),
                   jax.ShapeDtypeStruct((N_TOK, LANES), F32),
                   jax.ShapeDtypeStruct((n_steps * 8, LANES), F32)],
        grid=(n_steps + 1 + PAD_BLOCKS,),
        in_specs=[pl.BlockSpec((tm, D_MODEL), row),
                  pl.BlockSpec((tm, Q_A), row),
                  pl.BlockSpec((tm, W_B), row),
                  pl.BlockSpec((tm, 2 * D_MODEL), row),
                  pl.BlockSpec((None, 1, D_MODEL), per_b),
                  pl.BlockSpec((None, 1, D_MODEL), per_b),
                  pl.BlockSpec((None, 1, D_MODEL), per_b),
                  pl.BlockSpec((1, D_MODEL), const),
                  pl.BlockSpec(wa.shape, const),
                  pl.BlockSpec(wb.shape, const),
                  pl.BlockSpec(wo.shape, const),
                  pl.BlockSpec(wr2.shape, const),
                  pl.BlockSpec((1, LANES), const)],
        out_specs=[pl.BlockSpec((tm, D_MODEL), row),
                   pl.BlockSpec((XY_ROWS, XY_COLS), lambda i: (jnp.maximum(i - 1, 0), 0)),
                   pl.BlockSpec((tm, LANES), routed),
                   pl.BlockSpec((8, LANES), routed)],
        scratch_shapes=[pltpu.VMEM((tm, D_MODEL), BF16),
                        pltpu.VMEM((tm, LANES), F32)],
        compiler_params=_cparams(1),
        name="post",
    )(x2, oa, ob, gates, gate_m, scale_f, shift_f, g_ffn, wa, wb, wo, wr2, b_r)


def _experts_kernel(te_ref, nu_ref, ur_ref, ne_ref, ep_ref, xy_in, wg_hbm, wu_hbm, wd_hbm, xy_out,
                    xbuf, ybuf, wg_s, wu_s, wd_s, wg_f, wu_f, wd_f, gsem, ssem, wsem):
    del xy_in
    r = pl.program_id(0)
    last = pl.num_programs(0) - 1
    n_used = nu_ref[0]
    slot = lax.rem(r, 2)

    def unit_row(step, s):
        return pl.multiple_of(ur_ref[step * UNITS_PER_TILE + s], UNIT)

    def start_gathers(step, sl):
        for s in range(UNITS_PER_TILE):
            pltpu.make_async_copy(xy_out.at[pl.ds(unit_row(step, s), UNIT), :],
                                  xbuf.at[sl, pl.ds(s * UNIT, UNIT), :], gsem.at[sl]).start()

    def wait_gathers(sl):
        pltpu.make_async_copy(xy_out.at[pl.ds(0, TM_EXP), :], xbuf.at[sl], gsem.at[sl]).wait()

    def start_scatters(step, sl):
        for s in range(UNITS_PER_TILE):
            pltpu.make_async_copy(ybuf.at[sl, pl.ds(s * UNIT, UNIT), :],
                                  xy_out.at[pl.ds(unit_row(step, s), UNIT), pl.ds(0, D_MODEL)],
                                  ssem.at[sl]).start()

    def wait_scatters(sl):
        pltpu.make_async_copy(ybuf.at[sl], xy_out.at[pl.ds(0, TM_EXP), pl.ds(0, D_MODEL)], ssem.at[sl]).wait()

    @pl.when(r == 0)
    def _():
        start_gathers(0, 0)

    @pl.when(jnp.logical_and(r < n_used, r >= 2))
    def _():
        wait_scatters(slot)

    def weight_copies(e, p):
        return [pltpu.make_async_copy(src.at[e], dst.at[p], wsem.at[p])
                for src, dst in ((wg_hbm, wg_f), (wu_hbm, wu_f), (wd_hbm, wd_f))]

    @pl.when(jnp.logical_and(r < n_used,
                             jnp.logical_or(r == 0, te_ref[r] != te_ref[jnp.maximum(r - 1, 0)])))
    def _():
        e = te_ref[r]
        p = ep_ref[r]

        @pl.when(r == 0)
        def _():
            for cp in weight_copies(e, p):
                cp.start()

        for cp in weight_copies(e, p):
            cp.wait()
        wg_s[...] = wg_f[p].astype(BF16)
        wu_s[...] = wu_f[p].astype(BF16)
        wd_s[...] = wd_f[p].astype(BF16)

        @pl.when(ne_ref[r] >= 0)
        def _():
            for cp in weight_copies(ne_ref[r], 1 - p):
                cp.start()

    @pl.when(r < n_used)
    def _():
        wait_gathers(slot)
        start_gathers(jnp.minimum(r + 1, last), 1 - slot)
        x = xbuf[slot, :, 0:D_MODEL]
        side = xbuf[slot, :, D_MODEL:XY_COLS].astype(F32)
        lane = lax.broadcasted_iota(jnp.int32, side.shape, 1)

        def lanes_sum(a, b):
            return jnp.sum(jnp.where((lane == a) | (lane == b), side, 0.0), axis=-1, keepdims=True)

        is_slot1 = lanes_sum(E1_LANE, E1_LANE) == te_ref[r].astype(F32)
        wrow = jnp.where(is_slot1, lanes_sum(*W1_LANES), lanes_sum(*W2_LANES))
        a = jnp.dot(x, wg_s[...], preferred_element_type=F32)
        u = jnp.dot(x, wu_s[...], preferred_element_type=F32)
        hid = (a * jax.nn.sigmoid(a) * u * wrow).astype(BF16)
        ybuf[slot] = jnp.dot(hid, wd_s[...], preferred_element_type=F32).astype(BF16)
        start_scatters(r, slot)

    @pl.when(r == n_used - 1)
    def _():
        wait_gathers(1 - slot)
        wait_scatters(slot)

        @pl.when(r >= 1)
        def _():
            wait_scatters(1 - slot)


def _experts(tile_expert, n_used, unit_rows, next_expert, expert_parity, xy, wg, wu, wd):
    grid_spec = pltpu.PrefetchScalarGridSpec(
        num_scalar_prefetch=5,
        grid=(N_EXP_TILES,),
        in_specs=[pl.BlockSpec(memory_space=pl.ANY),
                  pl.BlockSpec(memory_space=pl.ANY),
                  pl.BlockSpec(memory_space=pl.ANY),
                  pl.BlockSpec(memory_space=pl.ANY)],
        out_specs=pl.BlockSpec(memory_space=pl.ANY),
        scratch_shapes=[pltpu.VMEM((2, TM_EXP, XY_COLS), BF16),
                        pltpu.VMEM((2, TM_EXP, D_MODEL), BF16),
                        pltpu.VMEM((D_MODEL, D_FF_EXPERT), BF16),
                        pltpu.VMEM((D_MODEL, D_FF_EXPERT), BF16),
                        pltpu.VMEM((D_FF_EXPERT, D_MODEL), BF16),
                        pltpu.VMEM((2, D_MODEL, D_FF_EXPERT), F32),
                        pltpu.VMEM((2, D_MODEL, D_FF_EXPERT), F32),
                        pltpu.VMEM((2, D_FF_EXPERT, D_MODEL), F32),
                        pltpu.SemaphoreType.DMA((2,)),
                        pltpu.SemaphoreType.DMA((2,)),
                        pltpu.SemaphoreType.DMA((2,))])
    return pl.pallas_call(
        _experts_kernel,
        out_shape=jax.ShapeDtypeStruct(xy.shape, xy.dtype),
        grid_spec=grid_spec,
        input_output_aliases={5: 0},
        compiler_params=_cparams(1),
        name="experts",
    )(tile_expert, n_used, unit_rows, next_expert, expert_parity, xy, wg, wu, wd)


def _combine_kernel(x1_ref, rc_ref, gf_ref, gfin_ref, y_ref, o_ref):
    lp1 = rc_ref[:, 0:1]
    lp2 = rc_ref[:, 1:2]
    scol = lax.broadcasted_iota(jnp.int32, (TM_ROW, XY_ROWS), 1).astype(F32)
    pick = jnp.where((scol == lp1) | (scol == lp2), 1.0, 0.0).astype(BF16)
    y = jnp.dot(pick, y_ref[...], preferred_element_type=F32)
    xf = x1_ref[...] + gf_ref[...] * y
    rs = lax.rsqrt(jnp.mean(xf * xf, axis=-1, keepdims=True) + EPS)
    o_ref[...] = xf * rs * gfin_ref[...]


def _combine(x1, rcol, gate_f, g_final, xy):
    tm = TM_ROW
    tpb = SEQ // tm
    row = lambda i: (i, 0)
    return pl.pallas_call(
        _combine_kernel,
        out_shape=jax.ShapeDtypeStruct((N_TOK, D_MODEL), F32),
        grid=(N_TOK // tm,),
        in_specs=[pl.BlockSpec((tm, D_MODEL), row),
                  pl.BlockSpec((tm, LANES), row),
                  pl.BlockSpec((None, 1, D_MODEL), lambda i: (i // tpb, 0, 0)),
                  pl.BlockSpec((1, D_MODEL), lambda i: (0, 0)),
                  pl.BlockSpec((XY_ROWS, D_MODEL), row)],
        out_specs=pl.BlockSpec((tm, D_MODEL), row),
        compiler_params=_cparams(1),
        name="combine",
    )(x1, rcol, gate_f, g_final, xy)


def _t5_bucket_np():
    qi = np.arange(BLOCK)[:, None]
    kj = np.arange(2 * BLOCK)[None, :]
    dist = qi - kj + BLOCK
    n = np.maximum(dist, 0)
    max_exact = NUM_BUCKETS // 2
    nf = np.maximum(n, 1).astype(np.float32)
    large = max_exact + (np.log(nf / np.float32(max_exact)) / np.float32(math.log(MAX_DISTANCE / max_exact))
                         * np.float32(NUM_BUCKETS - max_exact)).astype(np.int32)
    large = np.minimum(large, NUM_BUCKETS - 1)
    bucket = np.where(n < max_exact, n, large)
    band = (dist >= 0) & (dist < WINDOW)
    return bucket.astype(np.int32), band


def kernel(x, c, w_ada, b_ada, g_norm_mix, g_norm_ffn, w_in, sinks, b_forget, w_proj_swa, w_proj_fox,
           w_out, rel_bias_table, w_router_group, b_router_group, w_router_expert, b_router_expert,
           w_gate_exp, w_up_exp, w_down_exp, g_final):
    l = 0
    x2 = x.reshape(N_TOK, D_MODEL)

    c16 = jnp.concatenate([c, jnp.zeros_like(c)], axis=0)
    mod = _ada(c16, w_ada[l], b_ada[l][None, :])[:BATCH]
    shift_m, scale_m, gate_m, shift_f, scale_f, gate_f = [
        m.reshape(BATCH, 1, D_MODEL) for m in jnp.split(mod, 6, axis=-1)]

    w = w_in[l]
    o_ka, o_va, o_qb = Q_A, Q_A + KV_A, Q_A + 2 * KV_A
    o_kb, o_vb, o_f = o_qb + W_B, o_qb + 2 * W_B, o_qb + 3 * W_B
    o_g = o_f + N_HEADS_FOX

    def dup(cols):
        heads = [cols[:, h * HEAD_DIM:(h + 1) * HEAD_DIM] for h in range(N_KV_HEADS_SWA)]
        return jnp.concatenate([hd for hd in heads for _ in range(2)], axis=1)

    w_main = jnp.concatenate([w[:, :Q_A], dup(w[:, o_ka:o_va]), dup(w[:, o_va:o_qb]),
                              w[:, o_qb:o_f]], axis=1).astype(BF16)
    carrier = DECAY_LANES * N_HEADS_FOX
    w_f = jnp.pad(jnp.repeat(w[:, o_f:o_g], DECAY_LANES, axis=1), ((0, 0), (0, LANES - carrier))).astype(BF16)
    w_g = w[:, o_g:].astype(BF16)
    qa, kdup, vdup, qb, kb, vb, f_pad, gates, nrm = _inproj(
        x2, scale_m, shift_m, g_norm_mix[l][None, :], w_main, w_f, w_g)

    b_pad = jnp.pad(jnp.repeat(b_forget[l], DECAY_LANES), (0, LANES - carrier))[None, :]
    lanes = np.arange(LANES)
    jmod = jnp.asarray(np.where(lanes < DECAY_LANES * N_HEADS_FOX, lanes % DECAY_LANES, 7)[None, :].astype(np.int32))
    dq, dk, fb = _cum(f_pad, b_pad, jmod)

    bucket, band = _t5_bucket_np()
    onehot = jnp.asarray(bucket[None] == np.arange(NUM_BUCKETS)[:, None, None], dtype=F32)
    bias = jnp.einsum("bh,bqk->hqk", rel_bias_table.astype(F32), onehot, precision=HIGHEST)
    bias = jnp.where(band[None], bias, NEG_INF)
    first = np.arange(2 * BLOCK)[None, None, :] < BLOCK
    bias = jnp.stack([jnp.where(first, NEG_INF, bias), bias]).reshape(2, N_KV_HEADS_SWA, -1, 2 * BLOCK)
    o_a = _swa(sinks[l].astype(F32), qa, kdup, vdup, bias)

    o_b = _fox(_fox_first_tiles(nrm, fb), qb, kb, vb, dq, dk)

    w_r = jnp.concatenate([w_router_group[l]] + [w_router_expert[l][g] for g in range(N_GROUPS)], axis=1)
    w_r = jnp.pad(w_r, ((0, 0), (0, LANES - w_r.shape[1])))
    wr_hi = w_r.astype(BF16)
    wr_lo = (w_r - wr_hi.astype(F32)).astype(BF16)
    wr2 = jnp.concatenate([wr_hi, wr_lo], axis=1)
    b_r = jnp.concatenate([b_router_group[l], b_router_expert[l].reshape(-1)])
    b_r = jnp.pad(b_r, (0, LANES - b_r.shape[0]))[None, :]
    x1, xy, rcol, cu = _post(x2, o_a, o_b, gates, gate_m, scale_f, shift_f, g_norm_ffn[l][None, :],
                             w_proj_swa[l].astype(BF16), w_proj_fox[l].astype(BF16), w_out[l].astype(BF16),
                             wr2, b_r)

    i32 = jnp.int32
    n_tok_tiles = N_TOK // TM_POST
    cu = cu.reshape(n_tok_tiles, 8, LANES)[:, 0, :N_EXPERTS].astype(i32)
    loc_u = jnp.cumsum(cu, axis=1) - cu
    cend = jnp.cumsum(cu, axis=0)
    cstart = cend - cu
    tot_u = cend[-1]
    tiles_e = (tot_u + UNITS_PER_TILE - 1) // UNITS_PER_TILE
    tile_end = jnp.cumsum(tiles_e)
    tile_start = tile_end - tiles_e
    r = jnp.arange(N_EXP_TILES, dtype=i32)
    tile_expert = jnp.minimum(jnp.sum((tile_end[None, :] <= r[:, None]).astype(i32), axis=1), N_EXPERTS - 1)
    sel_e = tile_expert[:, None] == jnp.arange(N_EXPERTS, dtype=i32)[None, :]
    tw = r - jnp.sum(jnp.where(sel_e, tile_start[None, :], 0), axis=1)
    tot_r = jnp.sum(jnp.where(sel_e, tot_u[None, :], 0), axis=1)
    n_used = tile_end[-1:].astype(i32)
    q = tw[:, None] * UNITS_PER_TILE + jnp.arange(UNITS_PER_TILE, dtype=i32)[None, :]

    def of_expert(tab):
        return jnp.sum(jnp.where(sel_e[:, None, :], tab[None, :, :], 0), axis=2)

    cend_r, cstart_r, loc_r = of_expert(cend), of_expert(cstart), of_expert(loc_u)
    src_tile = jnp.minimum(jnp.sum((cend_r[:, None, :] <= q[:, :, None]).astype(i32), axis=2), n_tok_tiles - 1)
    sel_t = src_tile[:, :, None] == jnp.arange(n_tok_tiles, dtype=i32)[None, None, :]
    k = (q - jnp.sum(jnp.where(sel_t, cstart_r[:, None, :], 0), axis=2)
         + jnp.sum(jnp.where(sel_t, loc_r[:, None, :], 0), axis=2))
    real_rows = src_tile * XY_ROWS + k * UNIT
    pad_rows = PAD_BASE_ROW + (tile_expert[:, None] * PAD_UNITS_PER_EXPERT + (q - tot_r[:, None])) * UNIT
    idle_row = PAD_BASE_ROW + N_EXPERTS * PAD_UNITS_PER_EXPERT * UNIT
    unit_rows = jnp.where(q < tot_r[:, None], real_rows, pad_rows)
    unit_rows = jnp.where((r < n_used)[:, None], unit_rows, idle_row).reshape(-1).astype(i32)

    eid = jnp.arange(N_EXPERTS, dtype=i32)
    used = tiles_e > 0
    later_used = (eid[None, :] > eid[:, None]) & used[None, :]
    next_e = jnp.min(jnp.where(later_used, eid[None, :], N_EXPERTS), axis=1)
    next_e = jnp.where(next_e == N_EXPERTS, -1, next_e)
    parity_e = (jnp.cumsum(used.astype(i32)) - used.astype(i32)) % 2
    next_expert = jnp.sum(jnp.where(sel_e, next_e[None, :], 0), axis=1).astype(i32)
    expert_parity = jnp.sum(jnp.where(sel_e, parity_e[None, :], 0), axis=1).astype(i32)

    xy = _experts(tile_expert.astype(i32), n_used, unit_rows, next_expert, expert_parity, xy,
                  w_gate_exp[l].reshape(N_EXPERTS, D_MODEL, D_FF_EXPERT),
                  w_up_exp[l].reshape(N_EXPERTS, D_MODEL, D_FF_EXPERT),
                  w_down_exp[l].reshape(N_EXPERTS, D_FF_EXPERT, D_MODEL))
    out = _combine(x1, rcol, gate_f, g_final[None, :], xy)
    return out.reshape(BATCH, SEQ, D_MODEL)
```

```python
import math

import numpy as np
import jax
import jax.numpy as jnp
from jax import lax
from jax.experimental import pallas as pl
from jax.experimental.pallas import tpu as pltpu

F32 = jnp.float32
BF16 = jnp.bfloat16
HIGHEST = lax.Precision.HIGHEST

D_MODEL = 1024
BATCH = 8
SEQ = 4096
N_TOK = BATCH * SEQ
N_HEADS_SWA = 8
N_KV_HEADS_SWA = 2
N_HEADS_FOX = 8
HEAD_DIM = 64
WINDOW = 128
BLOCK = 128
NUM_BUCKETS = 32
MAX_DISTANCE = 128
N_GROUPS = 4
EXPERTS_PER_GROUP = 8
N_EXPERTS = N_GROUPS * EXPERTS_PER_GROUP
D_FF_EXPERT = 256
EPS = 1e-6
NEG_INF = -1e30

Q_A = N_HEADS_SWA * HEAD_DIM
KV_A = N_KV_HEADS_SWA * HEAD_DIM
W_B = N_HEADS_FOX * HEAD_DIM
LANES = 128
QK_SCALE = HEAD_DIM ** -0.5

TM_IN = 512
TM_POST = 512
TQ_FOX = 512
TK_FOX = TQ_FOX
SWA_BLOCKS = 4
TM_EXP = 512
TM_ROW = 512
UNIT = 16
XY_UNITS = 2 * TM_POST // UNIT + N_EXPERTS
XY_ROWS = XY_UNITS * UNIT
XY_COLS = D_MODEL + LANES
UNITS_PER_TILE = TM_EXP // UNIT
N_TOK_TILES = N_TOK // TM_POST
N_EXP_TILES = N_TOK_TILES * XY_UNITS // UNITS_PER_TILE + N_EXPERTS
PAD_UNITS_PER_EXPERT = UNITS_PER_TILE - 1
PAD_BLOCKS = -(-(N_EXPERTS * PAD_UNITS_PER_EXPERT * UNIT) // XY_ROWS)
PAD_BASE_ROW = N_TOK_TILES * XY_ROWS
W1_LANES, W2_LANES, E1_LANE, E2_LANE = (4, 6), (5, 7), 8, 9
VMEM_LIMIT = 56 * 1024 * 1024

DECAY_LANES = 6
PRUNE_MARGIN = 110.0


def _cparams(n_axes):
    return pltpu.CompilerParams(dimension_semantics=("arbitrary",) * n_axes,
                                vmem_limit_bytes=VMEM_LIMIT)


def _ada_kernel(c_ref, w_ref, b_ref, o_ref):
    c = c_ref[...]
    ca = c * jax.nn.sigmoid(c)
    o_ref[...] = jnp.dot(ca.astype(BF16), w_ref[...].astype(BF16),
                         preferred_element_type=F32) + b_ref[...]


def _ada(c16, w_ada, b_ada):
    n_out = w_ada.shape[1]
    blk = 1024
    return pl.pallas_call(
        _ada_kernel,
        out_shape=jax.ShapeDtypeStruct((16, n_out), F32),
        grid=(n_out // blk,),
        in_specs=[pl.BlockSpec((16, D_MODEL), lambda j: (0, 0)),
                  pl.BlockSpec((D_MODEL, blk), lambda j: (0, j)),
                  pl.BlockSpec((1, blk), lambda j: (0, j))],
        out_specs=pl.BlockSpec((16, blk), lambda j: (0, j)),
        compiler_params=_cparams(1),
        name="ada",
    )(c16, w_ada, b_ada)


def _inproj_kernel(x_ref, sc_ref, sh_ref, g_ref, wm_ref, wg_ref, ind_ref,
                   qa_ref, kd_ref, va_ref, qb_ref, kb_ref, vb_ref, f_ref, gt_ref, nrm_ref):
    x = x_ref[...]
    rs = lax.rsqrt(jnp.mean(x * x, axis=-1, keepdims=True) + EPS)
    a = g_ref[...] * (1.0 + sc_ref[...])
    h = (x * rs * a + sh_ref[...]).astype(BF16)

    def mm(w):
        return jnp.dot(h, w, preferred_element_type=F32)

    qa_ref[...] = (mm(wm_ref[:, 0:512]) * QK_SCALE).astype(BF16)
    kd_ref[...] = mm(wm_ref[:, 512:768]).astype(BF16)
    vf = mm(wm_ref[:, 768:1024])
    f_ref[...] = vf[:, LANES:2 * LANES]
    v = vf[:, 0:LANES]
    vr = pltpu.roll(v, HEAD_DIM, 1)
    lo = lax.broadcasted_iota(jnp.int32, v.shape, 1) < HEAD_DIM
    va_ref[:, 0:LANES] = jnp.where(lo, v, vr).astype(BF16)
    va_ref[:, LANES:2 * LANES] = jnp.where(lo, vr, v).astype(BF16)
    qb =(mm(wm_ref[:, 1024:1536]) * QK_SCALE).astype(BF16)
    kb = mm(wm_ref[:, 1536:2048]).astype(BF16)
    qb_ref[...] = qb
    kb_ref[...] = kb
    vb_ref[...] = mm(wm_ref[:, 2048:2560]).astype(BF16)
    sq = jnp.concatenate([qb, kb], axis=1).astype(F32)
    seg = jnp.dot((sq * sq).astype(BF16), ind_ref[...], preferred_element_type=F32)
    nrm_ref[...] = jnp.broadcast_to(jnp.max(seg, axis=0, keepdims=True), nrm_ref.shape)
    gt_ref[...] = mm(wg_ref[...]).astype(BF16)


def _inproj(x2, scale_m, shift_m, g_mix, w_main, w_g):
    tm = TM_IN
    tpb = SEQ // tm
    row = lambda i: (i, 0)
    per_b = lambda i: (i // tpb, 0, 0)
    const = lambda i: (0, 0)
    outs = [(Q_A, BF16), (2 * KV_A, BF16), (2 * KV_A, BF16), (W_B, BF16), (W_B, BF16), (W_B, BF16),
            (LANES, F32), (2 * D_MODEL, BF16)]
    ind_np = np.zeros((2 * W_B, LANES), np.float32)
    ind_np[np.arange(2 * W_B), np.arange(2 * W_B) // HEAD_DIM] = 1.0
    ind = jnp.asarray(ind_np, dtype=BF16)
    n_steps = N_TOK // tm
    return pl.pallas_call(
        _inproj_kernel,
        out_shape=[jax.ShapeDtypeStruct((N_TOK, w), dt) for w, dt in outs]
        + [jax.ShapeDtypeStruct((n_steps * 8, LANES), F32)],
        grid=(n_steps,),
        in_specs=[pl.BlockSpec((tm, D_MODEL), row),
                  pl.BlockSpec((None, 1, D_MODEL), per_b),
                  pl.BlockSpec((None, 1, D_MODEL), per_b),
                  pl.BlockSpec((1, D_MODEL), const),
                  pl.BlockSpec(w_main.shape, const),
                  pl.BlockSpec(w_g.shape, const),
                  pl.BlockSpec(ind.shape, const)],
        out_specs=[pl.BlockSpec((tm, w), row) for w, _ in outs] + [pl.BlockSpec((8, LANES), row)],
        compiler_params=_cparams(1),
        name="inproj",
    )(x2, scale_m, shift_m, g_mix, w_main, w_g, ind)


def _log_sigmoid(x):
    return jnp.minimum(x, 0.0) - jnp.log1p(jnp.exp(-jnp.abs(x)))


def _cum_kernel(f_ref, b_ref, jm_ref, qa_ref, ka_ref, fb_ref):
    cum = _log_sigmoid(f_ref[...] + b_ref[...])
    row = lax.broadcasted_iota(jnp.int32, cum.shape, 0)
    k = 1
    while k < SEQ:
        if k < 8:
            shifted = jnp.where(row >= k, pltpu.roll(cum, k, 0), 0.0)
        else:
            shifted = jnp.concatenate([jnp.zeros((k, LANES), F32), cum[:SEQ - k]], axis=0)
        cum = cum + shifted
        k *= 2
    jm = jm_ref[...]
    for blk in range(SEQ // LANES):
        rows = slice(blk * LANES, (blk + 1) * LANES)
        cb = cum[rows]
        carry = cb[LANES - 1:LANES]
        hi = cb.astype(BF16).astype(F32)
        r1 = cb - hi
        mid = r1.astype(BF16).astype(F32)
        lo = (r1 - mid).astype(BF16).astype(F32)
        one = jnp.ones_like(cb)
        zero = jnp.zeros_like(cb)
        qa = jnp.where(jm == 0, hi, jnp.where(jm == 1, mid, jnp.where(jm == 2, lo,
                       jnp.where(jm < DECAY_LANES, one, zero))))
        ka = jnp.where(jm == 3, -hi, jnp.where(jm == 4, -mid, jnp.where(jm == 5, -lo,
                       jnp.where(jm < 3, one, zero))))
        qa_ref[rows, :] = qa.astype(BF16)
        ka_ref[rows, :] = ka.astype(BF16)
        blocks_per_tile = TQ_FOX // LANES
        tile = blk // blocks_per_tile
        if blk % blocks_per_tile == 0:
            fb_ref[2 * tile:2 * tile + 1, :] = cb[0:1]
        if blk % blocks_per_tile == blocks_per_tile - 1:
            fb_ref[2 * tile + 1:2 * tile + 2, :] = carry


def _cum(f_pad, b_pad, jmod):
    n_tiles = SEQ // TQ_FOX
    return pl.pallas_call(
        _cum_kernel,
        out_shape=[jax.ShapeDtypeStruct((BATCH, SEQ, LANES), BF16)] * 2
        + [jax.ShapeDtypeStruct((BATCH, 2 * n_tiles, LANES), F32)],
        grid=(BATCH,),
        in_specs=[pl.BlockSpec((SEQ, LANES), lambda b: (b, 0)),
                  pl.BlockSpec((1, LANES), lambda b: (0, 0)),
                  pl.BlockSpec((1, LANES), lambda b: (0, 0))],
        out_specs=[pl.BlockSpec((None, SEQ, LANES), lambda b: (b, 0, 0))] * 2
        + [pl.BlockSpec((None, 2 * n_tiles, LANES), lambda b: (b, 0, 0))],
        compiler_params=_cparams(1),
        name="cum",
    )(f_pad, b_pad, jmod)


def _swa_block(sink_cols, q, kk, vv, bias_ref, lo):
    tiles = []
    for g in range(N_KV_HEADS_SWA):
        parts = []
        for t in range(2):
            qt = q[:, (2 * g + t) * LANES:(2 * g + t + 1) * LANES]
            zero = jnp.zeros_like(qt)
            parts.append(jnp.where(lo, qt, zero))
            parts.append(jnp.where(lo, zero, qt))
        q4 = jnp.concatenate(parts, axis=0)
        s = lax.dot_general(q4, kk[:, g * LANES:(g + 1) * LANES], (((1,), (1,)), ((), ())),
                            preferred_element_type=F32)
        s = s + bias_ref[g]
        sink = sink_cols[g]
        m = jnp.maximum(jnp.max(s, axis=-1, keepdims=True), sink)
        p = jnp.exp(s - m)
        den = jnp.sum(p, axis=-1, keepdims=True) + jnp.exp(sink - m)
        o = jnp.dot(p.astype(BF16), vv[:, g * LANES:(g + 1) * LANES],
                    preferred_element_type=F32) / den
        tiles.append(jnp.where(lo, o[0:BLOCK], o[BLOCK:2 * BLOCK]))
        tiles.append(jnp.where(lo, o[2 * BLOCK:3 * BLOCK], o[3 * BLOCK:4 * BLOCK]))
    return tiles


def _swa_kernel(sink_ref, q_ref, kc_ref, kp_ref, vc_ref, vp_ref, bias_first_ref, bias_ref, o_ref):
    lane = lax.broadcasted_iota(jnp.int32, (BLOCK, LANES), 1)
    lo = lane < HEAD_DIM
    grp = N_HEADS_SWA // N_KV_HEADS_SWA
    row = lax.broadcasted_iota(jnp.int32, (grp * BLOCK, 1), 0)
    sink_cols = []
    for g in range(N_KV_HEADS_SWA):
        col = jnp.full((grp * BLOCK, 1), sink_ref[g * grp + grp - 1], F32)
        for hh in range(grp - 2, -1, -1):
            col = jnp.where(row < (hh + 1) * BLOCK, sink_ref[g * grp + hh], col)
        sink_cols.append(col)
    for blk in range(SWA_BLOCKS):
        rows = slice(blk * BLOCK, (blk + 1) * BLOCK)
        if blk == 0:
            kk = jnp.concatenate([kp_ref[...], kc_ref[rows, :]], axis=0)
            vv = jnp.concatenate([vp_ref[...], vc_ref[rows, :]], axis=0)
            bias = bias_first_ref
        else:
            prev_rows = slice((blk - 1) * BLOCK, (blk + 1) * BLOCK)
            kk = kc_ref[prev_rows, :]
            vv = vc_ref[prev_rows, :]
            bias = bias_ref
        tiles = _swa_block(sink_cols, q_ref[rows, :], kk, vv, bias, lo)
        for c, tile in enumerate(tiles):
            o_ref[rows, c * LANES:(c + 1) * LANES] = tile.astype(BF16)


def _swa(sinks, qa, kdup, va, bias):
    nb = SEQ // BLOCK
    ns = nb // SWA_BLOCKS
    cur = lambda b, i, s: (b * ns + i, 0)
    prev = lambda b, i, s: (b * nb + jnp.maximum(SWA_BLOCKS * i - 1, 0), 0)
    grid_spec = pltpu.PrefetchScalarGridSpec(
        num_scalar_prefetch=1,
        grid=(BATCH, ns),
        in_specs=[pl.BlockSpec((SWA_BLOCKS * BLOCK, Q_A), cur),
                  pl.BlockSpec((SWA_BLOCKS * BLOCK, 2 * KV_A), cur),
                  pl.BlockSpec((BLOCK, 2 * KV_A), prev),
                  pl.BlockSpec((SWA_BLOCKS * BLOCK, 2 * KV_A), cur),
                  pl.BlockSpec((BLOCK, 2 * KV_A), prev),
                  pl.BlockSpec((None,) + bias.shape[1:], lambda b, i, s: (jnp.minimum(i, 1), 0, 0, 0)),
                  pl.BlockSpec((None,) + bias.shape[1:], lambda b, i, s: (1, 0, 0, 0))],
        out_specs=pl.BlockSpec((SWA_BLOCKS * BLOCK, Q_A), cur))
    return pl.pallas_call(
        _swa_kernel,
        out_shape=jax.ShapeDtypeStruct((N_TOK, Q_A), BF16),
        grid_spec=grid_spec,
        compiler_params=_cparams(2),
        name="swa",
    )(sinks, qa, kdup, kdup, va, va, bias, bias)


def _fox_kernel(js_ref, q_ref, k_ref, v_ref, qa_ref, ka_ref, o_ref,
                kaug, vaug, q2, m_sc, acc_sc, s_a, s_b):
    tq, tk = TQ_FOX, TK_FOX
    b = pl.program_id(0)
    t = pl.program_id(1)
    i = pl.program_id(2)
    j_start = js_ref[(b * pl.num_programs(1) + t) * pl.num_programs(2) + i]

    @pl.when(i == 0)
    def _():
        kaug[:, 0:LANES] = k_ref[...]
        kaug[:, LANES:2 * LANES] = ka_ref[...]
        vaug[:, 0:LANES] = v_ref[...]
        vaug[:, LANES:2 * LANES] = jnp.ones((SEQ, LANES), BF16)

    lane = lax.broadcasted_iota(jnp.int32, (tq, LANES), 1)
    lo = lane < HEAD_DIM
    base = 2 * DECAY_LANES * t
    own = [(lane >= base + h * DECAY_LANES) & (lane < base + (h + 1) * DECAY_LANES) for h in range(2)]
    q = q_ref[...]
    qa = qa_ref[...]
    zero = jnp.zeros_like(q)
    q2[0, :, 0:LANES] = jnp.where(lo, q, zero)
    q2[1, :, 0:LANES] = jnp.where(lo, zero, q)
    for h in range(2):
        q2[h, :, LANES:2 * LANES] = jnp.where(own[h], qa, zero)
    m_sc[...] = jnp.full(m_sc.shape, NEG_INF, F32)
    acc_sc[...] = jnp.zeros(acc_sc.shape, F32)

    def scores(h, ks):
        return lax.dot_general(q2[h], kaug[pl.ds(ks, tk), :], (((1,), (1,)), ((), ())),
                               preferred_element_type=F32)

    def consume(h, s, ks, mask):
        if mask is not None:
            s = jnp.where(mask, s, NEG_INF)
        m_prev = m_sc[h]
        m_new = jnp.maximum(m_prev, jnp.max(s, axis=-1, keepdims=True))
        alpha = jnp.exp(m_prev - m_new)
        p = jnp.exp(s - jnp.concatenate([m_new] * (tk // LANES), axis=1))
        pv = jnp.dot(p.astype(BF16), vaug[pl.ds(ks, tk), :], preferred_element_type=F32)
        acc_sc[h] = jnp.concatenate([alpha, alpha], axis=1) * acc_sc[h] + pv
        m_sc[h] = m_new

    def key_start(j):
        return pl.multiple_of(j * tk, tk)

    def scores_into(buf, j):
        for h in range(2):
            buf[h] = scores(h, key_start(j))

    def consume_from(buf, j, mask):
        for h in range(2):
            consume(h, buf[h], key_start(j), mask)

    rr = lax.broadcasted_iota(jnp.int32, (tq, tk), 0)
    cc = lax.broadcasted_iota(jnp.int32, (tq, tk), 1)
    causal = cc <= rr
    n_full = i - j_start

    scores_into(s_a, j_start)

    def pair(p, carry):
        j = j_start + 2 * p
        scores_into(s_b, j + 1)
        consume_from(s_a, j, None)
        scores_into(s_a, j + 2)
        consume_from(s_b, j + 1, None)
        return carry

    lax.fori_loop(0, n_full // 2, pair, 0)
    odd = lax.rem(n_full, 2) == 1

    @pl.when(odd)
    def _():
        scores_into(s_b, i)
        consume_from(s_a, i - 1, None)
        consume_from(s_b, i, causal)

    @pl.when(jnp.logical_not(odd))
    def _():
        consume_from(s_a, i, causal)

    outs = [acc_sc[h, :, 0:LANES] / acc_sc[h, :, LANES:2 * LANES] for h in range(2)]
    o_ref[...] = jnp.where(lo, outs[0], outs[1]).astype(BF16)


def _fox(j_start, qb, kb, vb, qa, ka):
    tq = TQ_FOX
    nq = SEQ // tq
    n_pairs = N_HEADS_FOX // 2
    qmap = lambda b, t, i, js: (b * nq + i, t)
    kmap = lambda b, t, i, js: (b, t)
    grid_spec = pltpu.PrefetchScalarGridSpec(
        num_scalar_prefetch=1,
        grid=(BATCH, n_pairs, nq),
        in_specs=[pl.BlockSpec((tq, LANES), qmap),
                  pl.BlockSpec((SEQ, LANES), kmap),
                  pl.BlockSpec((SEQ, LANES), kmap),
                  pl.BlockSpec((None, tq, LANES), lambda b, t, i, js: (b, i, 0)),
                  pl.BlockSpec((None, SEQ, LANES), lambda b, t, i, js: (b, 0, 0))],
        out_specs=pl.BlockSpec((tq, LANES), qmap),
        scratch_shapes=[pltpu.VMEM((SEQ, 2 * LANES), BF16),
                        pltpu.VMEM((SEQ, 2 * LANES), BF16),
                        pltpu.VMEM((2, tq, 2 * LANES), BF16),
                        pltpu.VMEM((2, tq, LANES), F32),
                        pltpu.VMEM((2, tq, 2 * LANES), F32),
                        pltpu.VMEM((2, tq, TK_FOX), F32),
                        pltpu.VMEM((2, tq, TK_FOX), F32)])
    return pl.pallas_call(
        _fox_kernel,
        out_shape=jax.ShapeDtypeStruct((N_TOK, W_B), BF16),
        grid_spec=grid_spec,
        compiler_params=_cparams(3),
        name="fox",
    )(j_start, qb, kb, vb, qa, ka)


def _fox_first_tiles(nrm, fb):
    n_tiles = SEQ // TQ_FOX
    nr = nrm.reshape(BATCH, n_tiles, 8, LANES)[:, :, 0, :] * 1.02
    qn = jnp.sqrt(nr[..., 0:N_HEADS_FOX])
    kn = jnp.sqrt(nr[..., N_HEADS_FOX:2 * N_HEADS_FOX])
    f_first = fb[:, 0::2, 0:DECAY_LANES * N_HEADS_FOX:DECAY_LANES]
    f_last = fb[:, 1::2, 0:DECAY_LANES * N_HEADS_FOX:DECAY_LANES]
    kn_prefix = lax.cummax(kn, axis=1)
    upper = qn[:, :, None, :] * kn_prefix[:, None, :, :] + f_first[:, :, None, :] - f_last[:, None, :, :]
    row_max_low = -(qn * kn)[:, :, None, :]
    ii = jnp.arange(n_tiles)[None, :, None, None]
    jj = jnp.arange(n_tiles)[None, None, :, None]
    skip = (upper < row_max_low - PRUNE_MARGIN) & (jj < ii)
    skip = jnp.all(skip.reshape(BATCH, n_tiles, n_tiles, N_HEADS_FOX // 2, 2), axis=-1)
    first = jnp.sum(jnp.cumprod(skip.astype(jnp.int32), axis=2), axis=2)
    return jnp.transpose(first, (0, 2, 1)).reshape(-1).astype(jnp.int32)


def _post_kernel(x_ref, oa_ref, ob_ref, gt_ref, gm_ref, sc_ref, sh_ref, g_ref,
                 wa_ref, wb_ref, wo_ref, wr2_ref, br_ref,
                 x1_ref, xy_ref, rc_ref, cu_ref, hh_prev, lg_prev):
    step = pl.program_id(0)

    @pl.when(step == 0)
    def _():
        hh_prev[...] = jnp.zeros(hh_prev.shape, BF16)
        lg_prev[...] = jnp.zeros(lg_prev.shape, F32)

    @pl.when(step <= N_TOK_TILES)
    def _():
        hh_p = hh_prev[...]
        lg_p = lg_prev[...]
        hh, logits = _post_mix(x_ref, oa_ref, ob_ref, gt_ref, gm_ref, sc_ref, sh_ref, g_ref,
                               wa_ref, wb_ref, wo_ref, wr2_ref, br_ref, x1_ref)
        _post_route(hh_p, lg_p, xy_ref, rc_ref, cu_ref)
        hh_prev[...] = hh
        lg_prev[...] = logits

    @pl.when(step > N_TOK_TILES)
    def _():
        xy_ref[...] = jnp.zeros(xy_ref.shape, BF16)


def _post_mix(x_ref, oa_ref, ob_ref, gt_ref, gm_ref, sc_ref, sh_ref, g_ref,
              wa_ref, wb_ref, wo_ref, wr2_ref, br_ref, x1_ref):
    pa = jnp.dot(oa_ref[...], wa_ref[...], preferred_element_type=F32)
    pb = jnp.dot(ob_ref[...], wb_ref[...], preferred_element_type=F32)
    ga = jax.nn.sigmoid(gt_ref[:, 0:D_MODEL].astype(F32))
    gb = jax.nn.sigmoid(gt_ref[:, D_MODEL:2 * D_MODEL].astype(F32))
    merged = (ga * pa + gb * pb).astype(BF16)
    y = jnp.dot(merged, wo_ref[...], preferred_element_type=F32)
    x1 = x_ref[...] + gm_ref[...] * y
    x1_ref[...] = x1

    rs = lax.rsqrt(jnp.mean(x1 * x1, axis=-1, keepdims=True) + EPS)
    a = g_ref[...] * (1.0 + sc_ref[...])
    h2 = x1 * rs * a + sh_ref[...]

    hh = h2.astype(BF16)
    hl = (h2 - hh.astype(F32)).astype(BF16)
    hi_both = jnp.dot(hh, wr2_ref[...], preferred_element_type=F32)
    logits = (hi_both[:, 0:LANES] + hi_both[:, LANES:2 * LANES]
              + jnp.dot(hl, wr2_ref[:, 0:LANES], preferred_element_type=F32)
              + br_ref[...])
    return hh, logits


def _post_route(hh, logits, xy_ref, rc_ref, cu_ref):
    tm = TM_POST
    lane = lax.broadcasted_iota(jnp.int32, (tm, LANES), 1).astype(F32)
    big = float(LANES)
    gl = jnp.where(lane < N_GROUPS, logits, -jnp.inf)
    gmax = jnp.max(gl, axis=-1, keepdims=True)
    gi = jnp.min(jnp.where(gl == gmax, lane, big), axis=-1, keepdims=True)
    gsum = jnp.sum(jnp.exp(gl - gmax), axis=-1, keepdims=True)
    gp = 1.0 / gsum
    e_lo = N_GROUPS + EXPERTS_PER_GROUP * gi
    el = jnp.where((lane >= e_lo) & (lane < e_lo + EXPERTS_PER_GROUP), logits, -jnp.inf)
    v1 = jnp.max(el, axis=-1, keepdims=True)
    i1 = jnp.min(jnp.where(el == v1, lane, big), axis=-1, keepdims=True)
    el2 = jnp.where(lane == i1, -jnp.inf, el)
    v2 = jnp.max(el2, axis=-1, keepdims=True)
    i2 = jnp.min(jnp.where(el2 == v2, lane, big), axis=-1, keepdims=True)
    e21 = jnp.exp(v2 - v1)
    w1 = gp / (1.0 + e21)
    w2 = gp * e21 / (1.0 + e21)
    e1 = i1 - N_GROUPS
    e2 = i2 - N_GROUPS

    oh = jnp.where((lane == e1) | (lane == e2), 1.0, 0.0)
    cnt_u = jnp.floor((jnp.sum(oh, axis=0, keepdims=True) + (UNIT - 1)) * (1.0 / UNIT))
    r128 = lax.broadcasted_iota(jnp.int32, (LANES, LANES), 0)
    c128 = lax.broadcasted_iota(jnp.int32, (LANES, LANES), 1)
    before_lane = jnp.where(r128 < c128, 1.0, 0.0).astype(BF16)
    loc_u = jnp.dot(jnp.broadcast_to(cnt_u, (8, LANES)).astype(BF16), before_lane,
                    preferred_element_type=F32)
    trow = lax.broadcasted_iota(jnp.int32, (tm, LANES), 0)
    seen = oh
    k = 1
    while k < tm:
        if k < 8:
            shifted = jnp.where(trow >= k, pltpu.roll(seen, k, 0), 0.0)
        else:
            shifted = jnp.concatenate([jnp.zeros((k, LANES), F32), seen[:tm - k]], axis=0)
        seen = seen + shifted
        k *= 2
    pos_e = (seen - oh) + loc_u[0:1] * UNIT
    lp1 = jnp.sum(jnp.where(lane == e1, pos_e, 0.0), axis=-1, keepdims=True)
    lp2 = jnp.sum(jnp.where(lane == e2, pos_e, 0.0), axis=-1, keepdims=True)

    def to_row(col):
        return jnp.transpose(jnp.broadcast_to(col, (tm, LANES)))[0:1]

    srow = lax.broadcasted_iota(jnp.int32, (XY_ROWS, tm), 0).astype(F32)
    pm1 = jnp.where(srow == to_row(lp1), 1.0, 0.0).astype(BF16)
    pm2 = jnp.where(srow == to_row(lp2), 1.0, 0.0).astype(BF16)
    w1h = w1.astype(BF16).astype(F32)
    w2h = w2.astype(BF16).astype(F32)
    side = jnp.where(lane == W1_LANES[0], w1h, jnp.where(lane == W1_LANES[1], w1 - w1h,
           jnp.where(lane == W2_LANES[0], w2h, jnp.where(lane == W2_LANES[1], w2 - w2h,
           jnp.where(lane == E1_LANE, e1, jnp.where(lane == E2_LANE, e2, 0.0))))))
    tok = jnp.concatenate([hh, side.astype(BF16)], axis=1)
    xy_ref[...] = jnp.dot(pm1 + pm2, tok, preferred_element_type=F32).astype(BF16)

    cu_ref[...] = jnp.broadcast_to(cnt_u, cu_ref.shape)
    rc_ref[...] = jnp.where(lane == 0, lp1, jnp.where(lane == 1, lp2, 0.0))


def _post(x2, oa, ob, gates, gate_m, scale_f, shift_f, g_ffn, wa, wb, wo, wr2, b_r):
    tm = TM_POST
    tpb = SEQ // tm
    n_steps = N_TOK_TILES
    row = lambda i: (jnp.minimum(i, n_steps - 1), 0)
    per_b = lambda i: (jnp.minimum(i, n_steps - 1) // tpb, 0, 0)
    routed = lambda i: (jnp.clip(i - 1, 0, n_steps - 1), 0)
    const = lambda i: (0, 0)
    return pl.pallas_call(
        _post_kernel,
        out_shape=[jax.ShapeDtypeStruct((N_TOK, D_MODEL), F32),
                   jax.ShapeDtypeStruct(((n_steps + PAD_BLOCKS) * XY_ROWS, XY_COLS), BF16),
                   jax.ShapeDtypeStruct((N_TOK, LANES), F32),
                   jax.ShapeDtypeStruct((n_steps * 8, LANES), F32)],
        grid=(n_steps + 1 + PAD_BLOCKS,),
        in_specs=[pl.BlockSpec((tm, D_MODEL), row),
                  pl.BlockSpec((tm, Q_A), row),
                  pl.BlockSpec((tm, W_B), row),
                  pl.BlockSpec((tm, 2 * D_MODEL), row),
                  pl.BlockSpec((None, 1, D_MODEL), per_b),
                  pl.BlockSpec((None, 1, D_MODEL), per_b),
                  pl.BlockSpec((None, 1, D_MODEL), per_b),
                  pl.BlockSpec((1, D_MODEL), const),
                  pl.BlockSpec(wa.shape, const),
                  pl.BlockSpec(wb.shape, const),
                  pl.BlockSpec(wo.shape, const),
                  pl.BlockSpec(wr2.shape, const),
                  pl.BlockSpec((1, LANES), const)],
        out_specs=[pl.BlockSpec((tm, D_MODEL), row),
                   pl.BlockSpec((XY_ROWS, XY_COLS), lambda i: (jnp.maximum(i - 1, 0), 0)),
                   pl.BlockSpec((tm, LANES), routed),
                   pl.BlockSpec((8, LANES), routed)],
        scratch_shapes=[pltpu.VMEM((tm, D_MODEL), BF16),
                        pltpu.VMEM((tm, LANES), F32)],
        compiler_params=_cparams(1),
        name="post",
    )(x2, oa, ob, gates, gate_m, scale_f, shift_f, g_ffn, wa, wb, wo, wr2, b_r)


def _experts_kernel(te_ref, nu_ref, ur_ref, ne_ref, ep_ref, xy_in, wg_hbm, wu_hbm, wd_hbm, xy_out,
                    xbuf, ybuf, wg_s, wu_s, wd_s, wg_f, wu_f, wd_f, gsem, ssem, wsem):
    del xy_in
    r = pl.program_id(0)
    last = pl.num_programs(0) - 1
    n_used = nu_ref[0]
    slot = lax.rem(r, 2)

    def unit_row(step, s):
        return pl.multiple_of(ur_ref[step * UNITS_PER_TILE + s], UNIT)

    def start_gathers(step, sl):
        for s in range(UNITS_PER_TILE):
            pltpu.make_async_copy(xy_out.at[pl.ds(unit_row(step, s), UNIT), :],
                                  xbuf.at[sl, pl.ds(s * UNIT, UNIT), :], gsem.at[sl]).start()

    def wait_gathers(sl):
        pltpu.make_async_copy(xy_out.at[pl.ds(0, TM_EXP), :], xbuf.at[sl], gsem.at[sl]).wait()

    def start_scatters(step, sl):
        for s in range(UNITS_PER_TILE):
            pltpu.make_async_copy(ybuf.at[sl, pl.ds(s * UNIT, UNIT), :],
                                  xy_out.at[pl.ds(unit_row(step, s), UNIT), pl.ds(0, D_MODEL)],
                                  ssem.at[sl]).start()

    def wait_scatters(sl):
        pltpu.make_async_copy(ybuf.at[sl], xy_out.at[pl.ds(0, TM_EXP), pl.ds(0, D_MODEL)], ssem.at[sl]).wait()

    @pl.when(r == 0)
    def _():
        start_gathers(0, 0)

    @pl.when(jnp.logical_and(r < n_used, r >= 2))
    def _():
        wait_scatters(slot)

    def weight_copies(e, p):
        return [pltpu.make_async_copy(src.at[e], dst.at[p], wsem.at[p])
                for src, dst in ((wg_hbm, wg_f), (wu_hbm, wu_f), (wd_hbm, wd_f))]

    @pl.when(jnp.logical_and(r < n_used,
                             jnp.logical_or(r == 0, te_ref[r] != te_ref[jnp.maximum(r - 1, 0)])))
    def _():
        e = te_ref[r]
        p = ep_ref[r]

        @pl.when(r == 0)
        def _():
            for cp in weight_copies(e, p):
                cp.start()

        for cp in weight_copies(e, p):
            cp.wait()
        wg_s[...] = wg_f[p].astype(BF16)
        wu_s[...] = wu_f[p].astype(BF16)
        wd_s[...] = wd_f[p].astype(BF16)

        @pl.when(ne_ref[r] >= 0)
        def _():
            for cp in weight_copies(ne_ref[r], 1 - p):
                cp.start()

    @pl.when(r < n_used)
    def _():
        wait_gathers(slot)
        start_gathers(jnp.minimum(r + 1, last), 1 - slot)
        x = xbuf[slot, :, 0:D_MODEL]
        side = xbuf[slot, :, D_MODEL:XY_COLS].astype(F32)
        lane = lax.broadcasted_iota(jnp.int32, side.shape, 1)

        def lanes_sum(a, b):
            return jnp.sum(jnp.where((lane == a) | (lane == b), side, 0.0), axis=-1, keepdims=True)

        is_slot1 = lanes_sum(E1_LANE, E1_LANE) == te_ref[r].astype(F32)
        wrow = jnp.where(is_slot1, lanes_sum(*W1_LANES), lanes_sum(*W2_LANES))
        a = jnp.dot(x, wg_s[...], preferred_element_type=F32)
        u = jnp.dot(x, wu_s[...], preferred_element_type=F32)
        hid = (a * jax.nn.sigmoid(a) * u * wrow).astype(BF16)
        ybuf[slot] = jnp.dot(hid, wd_s[...], preferred_element_type=F32).astype(BF16)
        start_scatters(r, slot)

    @pl.when(r == n_used - 1)
    def _():
        wait_gathers(1 - slot)
        wait_scatters(slot)

        @pl.when(r >= 1)
        def _():
            wait_scatters(1 - slot)


def _experts(tile_expert, n_used, unit_rows, next_expert, expert_parity, xy, wg, wu, wd):
    grid_spec = pltpu.PrefetchScalarGridSpec(
        num_scalar_prefetch=5,
        grid=(N_EXP_TILES,),
        in_specs=[pl.BlockSpec(memory_space=pl.ANY),
                  pl.BlockSpec(memory_space=pl.ANY),
                  pl.BlockSpec(memory_space=pl.ANY),
                  pl.BlockSpec(memory_space=pl.ANY)],
        out_specs=pl.BlockSpec(memory_space=pl.ANY),
        scratch_shapes=[pltpu.VMEM((2, TM_EXP, XY_COLS), BF16),
                        pltpu.VMEM((2, TM_EXP, D_MODEL), BF16),
                        pltpu.VMEM((D_MODEL, D_FF_EXPERT), BF16),
                        pltpu.VMEM((D_MODEL, D_FF_EXPERT), BF16),
                        pltpu.VMEM((D_FF_EXPERT, D_MODEL), BF16),
                        pltpu.VMEM((2, D_MODEL, D_FF_EXPERT), F32),
                        pltpu.VMEM((2, D_MODEL, D_FF_EXPERT), F32),
                        pltpu.VMEM((2, D_FF_EXPERT, D_MODEL), F32),
                        pltpu.SemaphoreType.DMA((2,)),
                        pltpu.SemaphoreType.DMA((2,)),
                        pltpu.SemaphoreType.DMA((2,))])
    return pl.pallas_call(
        _experts_kernel,
        out_shape=jax.ShapeDtypeStruct(xy.shape, xy.dtype),
        grid_spec=grid_spec,
        input_output_aliases={5: 0},
        compiler_params=_cparams(1),
        name="experts",
    )(tile_expert, n_used, unit_rows, next_expert, expert_parity, xy, wg, wu, wd)


def _combine_kernel(x1_ref, rc_ref, gf_ref, gfin_ref, y_ref, o_ref):
    lp1 = rc_ref[:, 0:1]
    lp2 = rc_ref[:, 1:2]
    scol = lax.broadcasted_iota(jnp.int32, (TM_ROW, XY_ROWS), 1).astype(F32)
    pick = jnp.where((scol == lp1) | (scol == lp2), 1.0, 0.0).astype(BF16)
    y = jnp.dot(pick, y_ref[...], preferred_element_type=F32)
    xf = x1_ref[...] + gf_ref[...] * y
    rs = lax.rsqrt(jnp.mean(xf * xf, axis=-1, keepdims=True) + EPS)
    o_ref[...] = xf * rs * gfin_ref[...]


def _combine(x1, rcol, gate_f, g_final, xy):
    tm = TM_ROW
    tpb = SEQ // tm
    row = lambda i: (i, 0)
    return pl.pallas_call(
        _combine_kernel,
        out_shape=jax.ShapeDtypeStruct((N_TOK, D_MODEL), F32),
        grid=(N_TOK // tm,),
        in_specs=[pl.BlockSpec((tm, D_MODEL), row),
                  pl.BlockSpec((tm, LANES), row),
                  pl.BlockSpec((None, 1, D_MODEL), lambda i: (i // tpb, 0, 0)),
                  pl.BlockSpec((1, D_MODEL), lambda i: (0, 0)),
                  pl.BlockSpec((XY_ROWS, D_MODEL), row)],
        out_specs=pl.BlockSpec((tm, D_MODEL), row),
        compiler_params=_cparams(1),
        name="combine",
    )(x1, rcol, gate_f, g_final, xy)


def _t5_bucket_np():
    qi = np.arange(BLOCK)[:, None]
    kj = np.arange(2 * BLOCK)[None, :]
    dist = qi - kj + BLOCK
    n = np.maximum(dist, 0)
    max_exact = NUM_BUCKETS // 2
    nf = np.maximum(n, 1).astype(np.float32)
    large = max_exact + (np.log(nf / np.float32(max_exact)) / np.float32(math.log(MAX_DISTANCE / max_exact))
                         * np.float32(NUM_BUCKETS - max_exact)).astype(np.int32)
    large = np.minimum(large, NUM_BUCKETS - 1)
    bucket = np.where(n < max_exact, n, large)
    band = (dist >= 0) & (dist < WINDOW)
    return bucket.astype(np.int32), band


def kernel(x, c, w_ada, b_ada, g_norm_mix, g_norm_ffn, w_in, sinks, b_forget, w_proj_swa, w_proj_fox,
           w_out, rel_bias_table, w_router_group, b_router_group, w_router_expert, b_router_expert,
           w_gate_exp, w_up_exp, w_down_exp, g_final):
    l = 0
    x2 = x.reshape(N_TOK, D_MODEL)

    c16 = jnp.concatenate([c, jnp.zeros_like(c)], axis=0)
    mod = _ada(c16, w_ada[l], b_ada[l][None, :])[:BATCH]
    shift_m, scale_m, gate_m, shift_f, scale_f, gate_f = [
        m.reshape(BATCH, 1, D_MODEL) for m in jnp.split(mod, 6, axis=-1)]

    w = w_in[l]
    o_ka, o_va, o_qb = Q_A, Q_A + KV_A, Q_A + 2 * KV_A
    o_kb, o_vb, o_f = o_qb + W_B, o_qb + 2 * W_B, o_qb + 3 * W_B
    o_g = o_f + N_HEADS_FOX

    def dup(cols):
        heads = [cols[:, h * HEAD_DIM:(h + 1) * HEAD_DIM] for h in range(N_KV_HEADS_SWA)]
        return jnp.concatenate([hd for hd in heads for _ in range(2)], axis=1)

    carrier = DECAY_LANES * N_HEADS_FOX
    w_f = jnp.pad(jnp.repeat(w[:, o_f:o_g], DECAY_LANES, axis=1), ((0, 0), (0, LANES - carrier)))
    w_main = jnp.concatenate([w[:, :Q_A], dup(w[:, o_ka:o_va]), w[:, o_va:o_qb], w_f,
                              w[:, o_qb:o_f]], axis=1).astype(BF16)
    w_g = w[:, o_g:].astype(BF16)
    qa, kdup, va, qb, kb, vb, f_pad, gates, nrm = _inproj(
        x2, scale_m, shift_m, g_norm_mix[l][None, :], w_main, w_g)

    b_pad = jnp.pad(jnp.repeat(b_forget[l], DECAY_LANES), (0, LANES - carrier))[None, :]
    lanes = np.arange(LANES)
    jmod = jnp.asarray(np.where(lanes < DECAY_LANES * N_HEADS_FOX, lanes % DECAY_LANES, 7)[None, :].astype(np.int32))
    dq, dk, fb = _cum(f_pad, b_pad, jmod)

    bucket, band = _t5_bucket_np()
    onehot = jnp.asarray(bucket[None] == np.arange(NUM_BUCKETS)[:, None, None], dtype=F32)
    bias = jnp.einsum("bh,bqk->hqk", rel_bias_table.astype(F32), onehot, precision=HIGHEST)
    bias = jnp.where(band[None], bias, NEG_INF)
    first = np.arange(2 * BLOCK)[None, None, :] < BLOCK
    bias = jnp.stack([jnp.where(first, NEG_INF, bias), bias]).reshape(2, N_KV_HEADS_SWA, -1, 2 * BLOCK)
    o_a = _swa(sinks[l].astype(F32), qa, kdup, va, bias)

    o_b = _fox(_fox_first_tiles(nrm, fb), qb, kb, vb, dq, dk)

    w_r = jnp.concatenate([w_router_group[l]] + [w_router_expert[l][g] for g in range(N_GROUPS)], axis=1)
    w_r = jnp.pad(w_r, ((0, 0), (0, LANES - w_r.shape[1])))
    wr_hi = w_r.astype(BF16)
    wr_lo = (w_r - wr_hi.astype(F32)).astype(BF16)
    wr2 = jnp.concatenate([wr_hi, wr_lo], axis=1)
    b_r = jnp.concatenate([b_router_group[l], b_router_expert[l].reshape(-1)])
    b_r = jnp.pad(b_r, (0, LANES - b_r.shape[0]))[None, :]
    x1, xy, rcol, cu = _post(x2, o_a, o_b, gates, gate_m, scale_f, shift_f, g_norm_ffn[l][None, :],
                             w_proj_swa[l].astype(BF16), w_proj_fox[l].astype(BF16), w_out[l].astype(BF16),
                             wr2, b_r)

    i32 = jnp.int32
    n_tok_tiles = N_TOK // TM_POST
    cu = cu.reshape(n_tok_tiles, 8, LANES)[:, 0, :N_EXPERTS].astype(i32)
    loc_u = jnp.cumsum(cu, axis=1) - cu
    cend = jnp.cumsum(cu, axis=0)
    cstart = cend - cu
    tot_u = cend[-1]
    tiles_e = (tot_u + UNITS_PER_TILE - 1) // UNITS_PER_TILE
    tile_end = jnp.cumsum(tiles_e)
    tile_start = tile_end - tiles_e
    r = jnp.arange(N_EXP_TILES, dtype=i32)
    tile_expert = jnp.minimum(jnp.sum((tile_end[None, :] <= r[:, None]).astype(i32), axis=1), N_EXPERTS - 1)
    sel_e = tile_expert[:, None] == jnp.arange(N_EXPERTS, dtype=i32)[None, :]
    tw = r - jnp.sum(jnp.where(sel_e, tile_start[None, :], 0), axis=1)
    tot_r = jnp.sum(jnp.where(sel_e, tot_u[None, :], 0), axis=1)
    n_used = tile_end[-1:].astype(i32)
    q = tw[:, None] * UNITS_PER_TILE + jnp.arange(UNITS_PER_TILE, dtype=i32)[None, :]

    def of_expert(tab):
        return jnp.sum(jnp.where(sel_e[:, None, :], tab[None, :, :], 0), axis=2)

    cend_r, cstart_r, loc_r = of_expert(cend), of_expert(cstart), of_expert(loc_u)
    src_tile = jnp.minimum(jnp.sum((cend_r[:, None, :] <= q[:, :, None]).astype(i32), axis=2), n_tok_tiles - 1)
    sel_t = src_tile[:, :, None] == jnp.arange(n_tok_tiles, dtype=i32)[None, None, :]
    k = (q - jnp.sum(jnp.where(sel_t, cstart_r[:, None, :], 0), axis=2)
         + jnp.sum(jnp.where(sel_t, loc_r[:, None, :], 0), axis=2))
    real_rows = src_tile * XY_ROWS + k * UNIT
    pad_rows = PAD_BASE_ROW + (tile_expert[:, None] * PAD_UNITS_PER_EXPERT + (q - tot_r[:, None])) * UNIT
    idle_row = PAD_BASE_ROW + N_EXPERTS * PAD_UNITS_PER_EXPERT * UNIT
    unit_rows = jnp.where(q < tot_r[:, None], real_rows, pad_rows)
    unit_rows = jnp.where((r < n_used)[:, None], unit_rows, idle_row).reshape(-1).astype(i32)

    eid = jnp.arange(N_EXPERTS, dtype=i32)
    used = tiles_e > 0
    later_used = (eid[None, :] > eid[:, None]) & used[None, :]
    next_e = jnp.min(jnp.where(later_used, eid[None, :], N_EXPERTS), axis=1)
    next_e = jnp.where(next_e == N_EXPERTS, -1, next_e)
    parity_e = (jnp.cumsum(used.astype(i32)) - used.astype(i32)) % 2
    next_expert = jnp.sum(jnp.where(sel_e, next_e[None, :], 0), axis=1).astype(i32)
    expert_parity = jnp.sum(jnp.where(sel_e, parity_e[None, :], 0), axis=1).astype(i32)

    xy = _experts(tile_expert.astype(i32), n_used, unit_rows, next_expert, expert_parity, xy,
                  w_gate_exp[l].reshape(N_EXPERTS, D_MODEL, D_FF_EXPERT),
                  w_up_exp[l].reshape(N_EXPERTS, D_MODEL, D_FF_EXPERT),
                  w_down_exp[l].reshape(N_EXPERTS, D_FF_EXPERT, D_MODEL))
    out = _combine(x1, rcol, gate_f, g_final[None, :], xy)
    return out.reshape(BATCH, SEQ, D_MODEL)
```

```python
import math

import numpy as np
import jax
import jax.numpy as jnp
from jax import lax
from jax.experimental import pallas as pl
from jax.experimental.pallas import tpu as pltpu

F32 = jnp.float32
BF16 = jnp.bfloat16
HIGHEST = lax.Precision.HIGHEST

D_MODEL = 1024
BATCH = 8
SEQ = 4096
N_TOK = BATCH * SEQ
N_HEADS_SWA = 8
N_KV_HEADS_SWA = 2
N_HEADS_FOX = 8
HEAD_DIM = 64
WINDOW = 128
BLOCK = 128
NUM_BUCKETS = 32
MAX_DISTANCE = 128
N_GROUPS = 4
EXPERTS_PER_GROUP = 8
N_EXPERTS = N_GROUPS * EXPERTS_PER_GROUP
D_FF_EXPERT = 256
EPS = 1e-6
NEG_INF = -1e30

Q_A = N_HEADS_SWA * HEAD_DIM
KV_A = N_KV_HEADS_SWA * HEAD_DIM
W_B = N_HEADS_FOX * HEAD_DIM
LANES = 128
QK_SCALE = HEAD_DIM ** -0.5

TM_IN = 512
TM_POST = 512
TQ_FOX = 512
TK_FOX = TQ_FOX
SWA_BLOCKS = 4
TM_EXP = 512
TM_ROW = 512
UNIT = 16
XY_UNITS = 2 * TM_POST // UNIT + N_EXPERTS
XY_ROWS = XY_UNITS * UNIT
XY_COLS = D_MODEL + LANES
UNITS_PER_TILE = TM_EXP // UNIT
N_TOK_TILES = N_TOK // TM_POST
N_EXP_TILES = N_TOK_TILES * XY_UNITS // UNITS_PER_TILE + N_EXPERTS
PAD_UNITS_PER_EXPERT = UNITS_PER_TILE - 1
PAD_BLOCKS = -(-(N_EXPERTS * PAD_UNITS_PER_EXPERT * UNIT) // XY_ROWS)
PAD_BASE_ROW = N_TOK_TILES * XY_ROWS
W1_LANES, W2_LANES, E1_LANE, E2_LANE = (4, 6), (5, 7), 8, 9
VMEM_LIMIT = 56 * 1024 * 1024

DECAY_LANES = 6
LOG2E = math.log2(math.e)
Q_SCALE_LOG2 = QK_SCALE * LOG2E
PRUNE_MARGIN = 160.0


def _cparams(n_axes):
    return pltpu.CompilerParams(dimension_semantics=("arbitrary",) * n_axes,
                                vmem_limit_bytes=VMEM_LIMIT)


def _ada_kernel(c_ref, w_ref, b_ref, o_ref):
    c = c_ref[...]
    ca = c * jax.nn.sigmoid(c)
    o_ref[...] = jnp.dot(ca.astype(BF16), w_ref[...].astype(BF16),
                         preferred_element_type=F32) + b_ref[...]


def _ada(c16, w_ada, b_ada):
    n_out = w_ada.shape[1]
    blk = 1024
    return pl.pallas_call(
        _ada_kernel,
        out_shape=jax.ShapeDtypeStruct((16, n_out), F32),
        grid=(n_out // blk,),
        in_specs=[pl.BlockSpec((16, D_MODEL), lambda j: (0, 0)),
                  pl.BlockSpec((D_MODEL, blk), lambda j: (0, j)),
                  pl.BlockSpec((1, blk), lambda j: (0, j))],
        out_specs=pl.BlockSpec((16, blk), lambda j: (0, j)),
        compiler_params=_cparams(1),
        name="ada",
    )(c16, w_ada, b_ada)


def _inproj_kernel(x_ref, sc_ref, sh_ref, g_ref, wm_ref, wg_ref, ind_ref,
                   qa_ref, kd_ref, va_ref, qb_ref, kb_ref, vb_ref, f_ref, gt_ref, nrm_ref):
    x = x_ref[...]
    rs = lax.rsqrt(jnp.mean(x * x, axis=-1, keepdims=True) + EPS)
    a = g_ref[...] * (1.0 + sc_ref[...])
    h = (x * rs * a + sh_ref[...]).astype(BF16)

    def mm(w):
        return jnp.dot(h, w, preferred_element_type=F32)

    qa_ref[...] = (mm(wm_ref[:, 0:512]) * Q_SCALE_LOG2).astype(BF16)
    kd_ref[...] = mm(wm_ref[:, 512:768]).astype(BF16)
    vf = mm(wm_ref[:, 768:1024])
    f_ref[...] = vf[:, LANES:2 * LANES]
    v = vf[:, 0:LANES]
    vr = pltpu.roll(v, HEAD_DIM, 1)
    lo = lax.broadcasted_iota(jnp.int32, v.shape, 1) < HEAD_DIM
    va_ref[:, 0:LANES] = jnp.where(lo, v, vr).astype(BF16)
    va_ref[:, LANES:2 * LANES] = jnp.where(lo, vr, v).astype(BF16)
    qb = (mm(wm_ref[:, 1024:1536]) * Q_SCALE_LOG2).astype(BF16)
    kb = mm(wm_ref[:, 1536:2048]).astype(BF16)
    qb_ref[...] = qb
    kb_ref[...] = kb
    vb_ref[...] = mm(wm_ref[:, 2048:2560]).astype(BF16)
    sq = jnp.concatenate([qb, kb], axis=1).astype(F32)
    seg = jnp.dot((sq * sq).astype(BF16), ind_ref[...], preferred_element_type=F32)
    nrm_ref[...] = jnp.broadcast_to(jnp.max(seg, axis=0, keepdims=True), nrm_ref.shape)
    gt_ref[...] = mm(wg_ref[...]).astype(BF16)


def _inproj(x2, scale_m, shift_m, g_mix, w_main, w_g):
    tm = TM_IN
    tpb = SEQ // tm
    row = lambda i: (i, 0)
    per_b = lambda i: (i // tpb, 0, 0)
    const = lambda i: (0, 0)
    outs = [(Q_A, BF16), (2 * KV_A, BF16), (2 * KV_A, BF16), (W_B, BF16), (W_B, BF16), (W_B, BF16),
            (LANES, F32), (2 * D_MODEL, BF16)]
    ind_np = np.zeros((2 * W_B, LANES), np.float32)
    ind_np[np.arange(2 * W_B), np.arange(2 * W_B) // HEAD_DIM] = 1.0
    ind = jnp.asarray(ind_np, dtype=BF16)
    n_steps = N_TOK // tm
    return pl.pallas_call(
        _inproj_kernel,
        out_shape=[jax.ShapeDtypeStruct((N_TOK, w), dt) for w, dt in outs]
        + [jax.ShapeDtypeStruct((n_steps * 8, LANES), F32)],
        grid=(n_steps,),
        in_specs=[pl.BlockSpec((tm, D_MODEL), row),
                  pl.BlockSpec((None, 1, D_MODEL), per_b),
                  pl.BlockSpec((None, 1, D_MODEL), per_b),
                  pl.BlockSpec((1, D_MODEL), const),
                  pl.BlockSpec(w_main.shape, const),
                  pl.BlockSpec(w_g.shape, const),
                  pl.BlockSpec(ind.shape, const)],
        out_specs=[pl.BlockSpec((tm, w), row) for w, _ in outs] + [pl.BlockSpec((8, LANES), row)],
        compiler_params=_cparams(1),
        name="inproj",
    )(x2, scale_m, shift_m, g_mix, w_main, w_g, ind)


def _log_sigmoid(x):
    return jnp.minimum(x, 0.0) - jnp.log1p(jnp.exp(-jnp.abs(x)))


def _cum_kernel(f_ref, b_ref, jm_ref, qa_ref, ka_ref, fb_ref):
    cum = _log_sigmoid(f_ref[...] + b_ref[...]) * LOG2E
    row = lax.broadcasted_iota(jnp.int32, cum.shape, 0)
    k = 1
    while k < SEQ:
        if k < 8:
            shifted = jnp.where(row >= k, pltpu.roll(cum, k, 0), 0.0)
        else:
            shifted = jnp.concatenate([jnp.zeros((k, LANES), F32), cum[:SEQ - k]], axis=0)
        cum = cum + shifted
        k *= 2
    jm = jm_ref[...]
    for blk in range(SEQ // LANES):
        rows = slice(blk * LANES, (blk + 1) * LANES)
        cb = cum[rows]
        carry = cb[LANES - 1:LANES]
        hi = cb.astype(BF16).astype(F32)
        r1 = cb - hi
        mid = r1.astype(BF16).astype(F32)
        lo = (r1 - mid).astype(BF16).astype(F32)
        one = jnp.ones_like(cb)
        zero = jnp.zeros_like(cb)
        qa = jnp.where(jm == 0, hi, jnp.where(jm == 1, mid, jnp.where(jm == 2, lo,
                       jnp.where(jm < DECAY_LANES, one, zero))))
        ka = jnp.where(jm == 3, -hi, jnp.where(jm == 4, -mid, jnp.where(jm == 5, -lo,
                       jnp.where(jm < 3, one, zero))))
        qa_ref[rows, :] = qa.astype(BF16)
        ka_ref[rows, :] = ka.astype(BF16)
        blocks_per_tile = TQ_FOX // LANES
        tile = blk // blocks_per_tile
        if blk % blocks_per_tile == 0:
            fb_ref[2 * tile:2 * tile + 1, :] = cb[0:1]
        if blk % blocks_per_tile == blocks_per_tile - 1:
            fb_ref[2 * tile + 1:2 * tile + 2, :] = carry


def _cum(f_pad, b_pad, jmod):
    n_tiles = SEQ // TQ_FOX
    return pl.pallas_call(
        _cum_kernel,
        out_shape=[jax.ShapeDtypeStruct((BATCH, SEQ, LANES), BF16)] * 2
        + [jax.ShapeDtypeStruct((BATCH, 2 * n_tiles, LANES), F32)],
        grid=(BATCH,),
        in_specs=[pl.BlockSpec((SEQ, LANES), lambda b: (b, 0)),
                  pl.BlockSpec((1, LANES), lambda b: (0, 0)),
                  pl.BlockSpec((1, LANES), lambda b: (0, 0))],
        out_specs=[pl.BlockSpec((None, SEQ, LANES), lambda b: (b, 0, 0))] * 2
        + [pl.BlockSpec((None, 2 * n_tiles, LANES), lambda b: (b, 0, 0))],
        compiler_params=_cparams(1),
        name="cum",
    )(f_pad, b_pad, jmod)


def _swa_block(sink_cols, q, kk, vv, bias_ref, lo):
    tiles = []
    for g in range(N_KV_HEADS_SWA):
        parts = []
        for t in range(2):
            qt = q[:, (2 * g + t) * LANES:(2 * g + t + 1) * LANES]
            zero = jnp.zeros_like(qt)
            parts.append(jnp.where(lo, qt, zero))
            parts.append(jnp.where(lo, zero, qt))
        q4 = jnp.concatenate(parts, axis=0)
        s = lax.dot_general(q4, kk[:, g * LANES:(g + 1) * LANES], (((1,), (1,)), ((), ())),
                            preferred_element_type=F32)
        s = s + bias_ref[g]
        sink = sink_cols[g]
        m = jnp.maximum(jnp.max(s, axis=-1, keepdims=True), sink)
        p = jnp.exp2(s - m)
        den = jnp.sum(p, axis=-1, keepdims=True) + jnp.exp2(sink - m)
        o = jnp.dot(p.astype(BF16), vv[:, g * LANES:(g + 1) * LANES],
                    preferred_element_type=F32) / den
        tiles.append(jnp.where(lo, o[0:BLOCK], o[BLOCK:2 * BLOCK]))
        tiles.append(jnp.where(lo, o[2 * BLOCK:3 * BLOCK], o[3 * BLOCK:4 * BLOCK]))
    return tiles


def _swa_kernel(sink_ref, q_ref, kc_ref, kp_ref, vc_ref, vp_ref, bias_first_ref, bias_ref, o_ref):
    lane = lax.broadcasted_iota(jnp.int32, (BLOCK, LANES), 1)
    lo = lane < HEAD_DIM
    grp = N_HEADS_SWA // N_KV_HEADS_SWA
    row = lax.broadcasted_iota(jnp.int32, (grp * BLOCK, 1), 0)
    sink_cols = []
    for g in range(N_KV_HEADS_SWA):
        col = jnp.full((grp * BLOCK, 1), sink_ref[g * grp + grp - 1], F32)
        for hh in range(grp - 2, -1, -1):
            col = jnp.where(row < (hh + 1) * BLOCK, sink_ref[g * grp + hh], col)
        sink_cols.append(col)
    for blk in range(SWA_BLOCKS):
        rows = slice(blk * BLOCK, (blk + 1) * BLOCK)
        if blk == 0:
            kk = jnp.concatenate([kp_ref[...], kc_ref[rows, :]], axis=0)
            vv = jnp.concatenate([vp_ref[...], vc_ref[rows, :]], axis=0)
            bias = bias_first_ref
        else:
            prev_rows = slice((blk - 1) * BLOCK, (blk + 1) * BLOCK)
            kk = kc_ref[prev_rows, :]
            vv = vc_ref[prev_rows, :]
            bias = bias_ref
        tiles = _swa_block(sink_cols, q_ref[rows, :], kk, vv, bias, lo)
        for c, tile in enumerate(tiles):
            o_ref[rows, c * LANES:(c + 1) * LANES] = tile.astype(BF16)


def _swa(sinks, qa, kdup, va, bias):
    nb = SEQ // BLOCK
    ns = nb // SWA_BLOCKS
    cur = lambda b, i, s: (b * ns + i, 0)
    prev = lambda b, i, s: (b * nb + jnp.maximum(SWA_BLOCKS * i - 1, 0), 0)
    grid_spec = pltpu.PrefetchScalarGridSpec(
        num_scalar_prefetch=1,
        grid=(BATCH, ns),
        in_specs=[pl.BlockSpec((SWA_BLOCKS * BLOCK, Q_A), cur),
                  pl.BlockSpec((SWA_BLOCKS * BLOCK, 2 * KV_A), cur),
                  pl.BlockSpec((BLOCK, 2 * KV_A), prev),
                  pl.BlockSpec((SWA_BLOCKS * BLOCK, 2 * KV_A), cur),
                  pl.BlockSpec((BLOCK, 2 * KV_A), prev),
                  pl.BlockSpec((None,) + bias.shape[1:], lambda b, i, s: (jnp.minimum(i, 1), 0, 0, 0)),
                  pl.BlockSpec((None,) + bias.shape[1:], lambda b, i, s: (1, 0, 0, 0))],
        out_specs=pl.BlockSpec((SWA_BLOCKS * BLOCK, Q_A), cur))
    return pl.pallas_call(
        _swa_kernel,
        out_shape=jax.ShapeDtypeStruct((N_TOK, Q_A), BF16),
        grid_spec=grid_spec,
        compiler_params=_cparams(2),
        name="swa",
    )(sinks, qa, kdup, kdup, va, va, bias, bias)


def _fox_kernel(js_ref, q_ref, k_ref, v_ref, qa_ref, ka_ref, o_ref,
                kaug, vaug, q2, m_sc, acc_sc, s_a, s_b):
    tq, tk = TQ_FOX, TK_FOX
    b = pl.program_id(0)
    t = pl.program_id(1)
    i = pl.program_id(2)
    j_start = js_ref[(b * pl.num_programs(1) + t) * pl.num_programs(2) + i]

    @pl.when(i == 0)
    def _():
        kaug[:, 0:LANES] = k_ref[...]
        kaug[:, LANES:2 * LANES] = ka_ref[...]
        vaug[:, 0:LANES] = v_ref[...]
        vaug[:, LANES:2 * LANES] = jnp.ones((SEQ, LANES), BF16)

    lane = lax.broadcasted_iota(jnp.int32, (tq, LANES), 1)
    lo = lane < HEAD_DIM
    base = 2 * DECAY_LANES * t
    own = [(lane >= base + h * DECAY_LANES) & (lane < base + (h + 1) * DECAY_LANES) for h in range(2)]
    q = q_ref[...]
    qa = qa_ref[...]
    zero = jnp.zeros_like(q)
    q2[0, :, 0:LANES] = jnp.where(lo, q, zero)
    q2[1, :, 0:LANES] = jnp.where(lo, zero, q)
    for h in range(2):
        q2[h, :, LANES:2 * LANES] = jnp.where(own[h], qa, zero)
    m_sc[...] = jnp.full(m_sc.shape, NEG_INF, F32)
    acc_sc[...] = jnp.zeros(acc_sc.shape, F32)

    def scores(h, ks):
        return lax.dot_general(q2[h], kaug[pl.ds(ks, tk), :], (((1,), (1,)), ((), ())),
                               preferred_element_type=F32)

    def consume(h, s, ks, mask):
        if mask is not None:
            s = jnp.where(mask, s, NEG_INF)
        m_prev = m_sc[h]
        m_new = jnp.maximum(m_prev, jnp.max(s, axis=-1, keepdims=True))
        alpha = jnp.exp2(m_prev - m_new)
        p = jnp.exp2(s - jnp.concatenate([m_new] * (tk // LANES), axis=1))
        pv = jnp.dot(p.astype(BF16), vaug[pl.ds(ks, tk), :], preferred_element_type=F32)
        acc_sc[h] = jnp.concatenate([alpha, alpha], axis=1) * acc_sc[h] + pv
        m_sc[h] = m_new

    def key_start(j):
        return pl.multiple_of(j * tk, tk)

    def scores_into(buf, j):
        for h in range(2):
            buf[h] = scores(h, key_start(j))

    def consume_from(buf, j, mask):
        for h in range(2):
            consume(h, buf[h], key_start(j), mask)

    rr = lax.broadcasted_iota(jnp.int32, (tq, tk), 0)
    cc = lax.broadcasted_iota(jnp.int32, (tq, tk), 1)
    causal = cc <= rr
    n_full = i - j_start

    scores_into(s_a, j_start)

    def pair(p, carry):
        j = j_start + 2 * p
        scores_into(s_b, j + 1)
        consume_from(s_a, j, None)
        scores_into(s_a, j + 2)
        consume_from(s_b, j + 1, None)
        return carry

    lax.fori_loop(0, n_full // 2, pair, 0)
    odd = lax.rem(n_full, 2) == 1

    @pl.when(odd)
    def _():
        scores_into(s_b, i)
        consume_from(s_a, i - 1, None)
        consume_from(s_b, i, causal)

    @pl.when(jnp.logical_not(odd))
    def _():
        consume_from(s_a, i, causal)

    outs = [acc_sc[h, :, 0:LANES] / acc_sc[h, :, LANES:2 * LANES] for h in range(2)]
    o_ref[...] = jnp.where(lo, outs[0], outs[1]).astype(BF16)


def _fox(j_start, qb, kb, vb, qa, ka):
    tq = TQ_FOX
    nq = SEQ // tq
    n_pairs = N_HEADS_FOX // 2
    qmap = lambda b, t, i, js: (b * nq + i, t)
    kmap = lambda b, t, i, js: (b, t)
    grid_spec = pltpu.PrefetchScalarGridSpec(
        num_scalar_prefetch=1,
        grid=(BATCH, n_pairs, nq),
        in_specs=[pl.BlockSpec((tq, LANES), qmap),
                  pl.BlockSpec((SEQ, LANES), kmap),
                  pl.BlockSpec((SEQ, LANES), kmap),
                  pl.BlockSpec((None, tq, LANES), lambda b, t, i, js: (b, i, 0)),
                  pl.BlockSpec((None, SEQ, LANES), lambda b, t, i, js: (b, 0, 0))],
        out_specs=pl.BlockSpec((tq, LANES), qmap),
        scratch_shapes=[pltpu.VMEM((SEQ, 2 * LANES), BF16),
                        pltpu.VMEM((SEQ, 2 * LANES), BF16),
                        pltpu.VMEM((2, tq, 2 * LANES), BF16),
                        pltpu.VMEM((2, tq, LANES), F32),
                        pltpu.VMEM((2, tq, 2 * LANES), F32),
                        pltpu.VMEM((2, tq, TK_FOX), F32),
                        pltpu.VMEM((2, tq, TK_FOX), F32)])
    return pl.pallas_call(
        _fox_kernel,
        out_shape=jax.ShapeDtypeStruct((N_TOK, W_B), BF16),
        grid_spec=grid_spec,
        compiler_params=_cparams(3),
        name="fox",
    )(j_start, qb, kb, vb, qa, ka)


def _fox_first_tiles(nrm, fb):
    n_tiles = SEQ // TQ_FOX
    nr = nrm.reshape(BATCH, n_tiles, 8, LANES)[:, :, 0, :] * 1.02
    qn = jnp.sqrt(nr[..., 0:N_HEADS_FOX])
    kn = jnp.sqrt(nr[..., N_HEADS_FOX:2 * N_HEADS_FOX])
    f_first = fb[:, 0::2, 0:DECAY_LANES * N_HEADS_FOX:DECAY_LANES]
    f_last = fb[:, 1::2, 0:DECAY_LANES * N_HEADS_FOX:DECAY_LANES]
    kn_prefix = lax.cummax(kn, axis=1)
    upper = qn[:, :, None, :] * kn_prefix[:, None, :, :] + f_first[:, :, None, :] - f_last[:, None, :, :]
    row_max_low = -(qn * kn)[:, :, None, :]
    ii = jnp.arange(n_tiles)[None, :, None, None]
    jj = jnp.arange(n_tiles)[None, None, :, None]
    skip = (upper < row_max_low - PRUNE_MARGIN) & (jj < ii)
    skip = jnp.all(skip.reshape(BATCH, n_tiles, n_tiles, N_HEADS_FOX // 2, 2), axis=-1)
    first = jnp.sum(jnp.cumprod(skip.astype(jnp.int32), axis=2), axis=2)
    return jnp.transpose(first, (0, 2, 1)).reshape(-1).astype(jnp.int32)


def _post_kernel(x_ref, oa_ref, ob_ref, gt_ref, gm_ref, sc_ref, sh_ref, g_ref,
                 wa_ref, wb_ref, wo_ref, wr2_ref, br_ref,
                 x1_ref, xy_ref, rc_ref, cu_ref, hh_prev, lg_prev):
    step = pl.program_id(0)

    @pl.when(step == 0)
    def _():
        hh_prev[...] = jnp.zeros(hh_prev.shape, BF16)
        lg_prev[...] = jnp.zeros(lg_prev.shape, F32)

    @pl.when(step <= N_TOK_TILES)
    def _():
        hh_p = hh_prev[...]
        lg_p = lg_prev[...]
        hh, logits = _post_mix(x_ref, oa_ref, ob_ref, gt_ref, gm_ref, sc_ref, sh_ref, g_ref,
                               wa_ref, wb_ref, wo_ref, wr2_ref, br_ref, x1_ref)
        _post_route(hh_p, lg_p, xy_ref, rc_ref, cu_ref)
        hh_prev[...] = hh
        lg_prev[...] = logits

    @pl.when(step > N_TOK_TILES)
    def _():
        xy_ref[...] = jnp.zeros(xy_ref.shape, BF16)


def _post_mix(x_ref, oa_ref, ob_ref, gt_ref, gm_ref, sc_ref, sh_ref, g_ref,
              wa_ref, wb_ref, wo_ref, wr2_ref, br_ref, x1_ref):
    pa = jnp.dot(oa_ref[...], wa_ref[...], preferred_element_type=F32)
    pb = jnp.dot(ob_ref[...], wb_ref[...], preferred_element_type=F32)
    ga = jax.nn.sigmoid(gt_ref[:, 0:D_MODEL].astype(F32))
    gb = jax.nn.sigmoid(gt_ref[:, D_MODEL:2 * D_MODEL].astype(F32))
    merged = (ga * pa + gb * pb).astype(BF16)
    y = jnp.dot(merged, wo_ref[...], preferred_element_type=F32)
    x1 = x_ref[...] + gm_ref[...] * y
    x1_ref[...] = x1

    rs = lax.rsqrt(jnp.mean(x1 * x1, axis=-1, keepdims=True) + EPS)
    a = g_ref[...] * (1.0 + sc_ref[...])
    h2 = x1 * rs * a + sh_ref[...]

    hh = h2.astype(BF16)
    hl = (h2 - hh.astype(F32)).astype(BF16)
    hi_both = jnp.dot(hh, wr2_ref[...], preferred_element_type=F32)
    logits = (hi_both[:, 0:LANES] + hi_both[:, LANES:2 * LANES]
              + jnp.dot(hl, wr2_ref[:, 0:LANES], preferred_element_type=F32)
              + br_ref[...])
    return hh, logits


def _post_route(hh, logits, xy_ref, rc_ref, cu_ref):
    tm = TM_POST
    lane = lax.broadcasted_iota(jnp.int32, (tm, LANES), 1).astype(F32)
    big = float(LANES)
    gl = jnp.where(lane < N_GROUPS, logits, -jnp.inf)
    gmax = jnp.max(gl, axis=-1, keepdims=True)
    gi = jnp.min(jnp.where(gl == gmax, lane, big), axis=-1, keepdims=True)
    gsum = jnp.sum(jnp.exp(gl - gmax), axis=-1, keepdims=True)
    gp = 1.0 / gsum
    e_lo = N_GROUPS + EXPERTS_PER_GROUP * gi
    el = jnp.where((lane >= e_lo) & (lane < e_lo + EXPERTS_PER_GROUP), logits, -jnp.inf)
    v1 = jnp.max(el, axis=-1, keepdims=True)
    i1 = jnp.min(jnp.where(el == v1, lane, big), axis=-1, keepdims=True)
    el2 = jnp.where(lane == i1, -jnp.inf, el)
    v2 = jnp.max(el2, axis=-1, keepdims=True)
    i2 = jnp.min(jnp.where(el2 == v2, lane, big), axis=-1, keepdims=True)
    e21 = jnp.exp(v2 - v1)
    w1 = gp / (1.0 + e21)
    w2 = gp * e21 / (1.0 + e21)
    e1 = i1 - N_GROUPS
    e2 = i2 - N_GROUPS

    oh = jnp.where((lane == e1) | (lane == e2), 1.0, 0.0)
    cnt_u = jnp.floor((jnp.sum(oh, axis=0, keepdims=True) + (UNIT - 1)) * (1.0 / UNIT))
    r128 = lax.broadcasted_iota(jnp.int32, (LANES, LANES), 0)
    c128 = lax.broadcasted_iota(jnp.int32, (LANES, LANES), 1)
    before_lane = jnp.where(r128 < c128, 1.0, 0.0).astype(BF16)
    loc_u = jnp.dot(jnp.broadcast_to(cnt_u, (8, LANES)).astype(BF16), before_lane,
                    preferred_element_type=F32)
    trow = lax.broadcasted_iota(jnp.int32, (tm, LANES), 0)
    seen = oh
    k = 1
    while k < tm:
        if k < 8:
            shifted = jnp.where(trow >= k, pltpu.roll(seen, k, 0), 0.0)
        else:
            shifted = jnp.concatenate([jnp.zeros((k, LANES), F32), seen[:tm - k]], axis=0)
        seen = seen + shifted
        k *= 2
    pos_e = (seen - oh) + loc_u[0:1] * UNIT
    lp1 = jnp.sum(jnp.where(lane == e1, pos_e, 0.0), axis=-1, keepdims=True)
    lp2 = jnp.sum(jnp.where(lane == e2, pos_e, 0.0), axis=-1, keepdims=True)

    def to_row(col):
        return jnp.transpose(jnp.broadcast_to(col, (tm, LANES)))[0:1]

    srow = lax.broadcasted_iota(jnp.int32, (XY_ROWS, tm), 0).astype(F32)
    pm1 = jnp.where(srow == to_row(lp1), 1.0, 0.0).astype(BF16)
    pm2 = jnp.where(srow == to_row(lp2), 1.0, 0.0).astype(BF16)
    w1h = w1.astype(BF16).astype(F32)
    w2h = w2.astype(BF16).astype(F32)
    side = jnp.where(lane == W1_LANES[0], w1h, jnp.where(lane == W1_LANES[1], w1 - w1h,
           jnp.where(lane == W2_LANES[0], w2h, jnp.where(lane == W2_LANES[1], w2 - w2h,
           jnp.where(lane == E1_LANE, e1, jnp.where(lane == E2_LANE, e2, 0.0))))))
    tok = jnp.concatenate([hh, side.astype(BF16)], axis=1)
    xy_ref[...] = jnp.dot(pm1 + pm2, tok, preferred_element_type=F32).astype(BF16)

    cu_ref[...] = jnp.broadcast_to(cnt_u, cu_ref.shape)
    rc_ref[...] = jnp.where(lane == 0, lp1, jnp.where(lane == 1, lp2, 0.0))


def _post(x2, oa, ob, gates, gate_m, scale_f, shift_f, g_ffn, wa, wb, wo, wr2, b_r):
    tm = TM_POST
    tpb = SEQ // tm
    n_steps = N_TOK_TILES
    row = lambda i: (jnp.minimum(i, n_steps - 1), 0)
    per_b = lambda i: (jnp.minimum(i, n_steps - 1) // tpb, 0, 0)
    routed = lambda i: (jnp.clip(i - 1, 0, n_steps - 1), 0)
    const = lambda i: (0, 0)
    return pl.pallas_call(
        _post_kernel,
        out_shape=[jax.ShapeDtypeStruct((N_TOK, D_MODEL), F32),
                   jax.ShapeDtypeStruct(((n_steps + PAD_BLOCKS) * XY_ROWS, XY_COLS), BF16),
                   jax.ShapeDtypeStruct((N_TOK, LANES), F32),
                   jax.ShapeDtypeStruct((n_steps * 8, LANES), F32)],
        grid=(n_steps + 1 + PAD_BLOCKS,),
        in_specs=[pl.BlockSpec((tm, D_MODEL), row),
                  pl.BlockSpec((tm, Q_A), row),
                  pl.BlockSpec((tm, W_B), row),
                  pl.BlockSpec((tm, 2 * D_MODEL), row),
                  pl.BlockSpec((None, 1, D_MODEL), per_b),
                  pl.BlockSpec((None, 1, D_MODEL), per_b),
                  pl.BlockSpec((None, 1, D_MODEL), per_b),
                  pl.BlockSpec((1, D_MODEL), const),
                  pl.BlockSpec(wa.shape, const),
                  pl.BlockSpec(wb.shape, const),
                  pl.BlockSpec(wo.shape, const),
                  pl.BlockSpec(wr2.shape, const),
                  pl.BlockSpec((1, LANES), const)],
        out_specs=[pl.BlockSpec((tm, D_MODEL), row),
                   pl.BlockSpec((XY_ROWS, XY_COLS), lambda i: (jnp.maximum(i - 1, 0), 0)),
                   pl.BlockSpec((tm, LANES), routed),
                   pl.BlockSpec((8, LANES), routed)],
        scratch_shapes=[pltpu.VMEM((tm, D_MODEL), BF16),
                        pltpu.VMEM((tm, LANES), F32)],
        compiler_params=_cparams(1),
        name="post",
    )(x2, oa, ob, gates, gate_m, scale_f, shift_f, g_ffn, wa, wb, wo, wr2, b_r)


def _experts_kernel(te_ref, nu_ref, ur_ref, ne_ref, ep_ref, xy_in, wg_hbm, wu_hbm, wd_hbm, xy_out,
                    xbuf, ybuf, wg_s, wu_s, wd_s, wg_f, wu_f, wd_f, gsem, ssem, wsem):
    del xy_in
    r = pl.program_id(0)
    last = pl.num_programs(0) - 1
    n_used = nu_ref[0]
    slot = lax.rem(r, 2)

    def unit_row(step, s):
        return pl.multiple_of(ur_ref[step * UNITS_PER_TILE + s], UNIT)

    def start_gathers(step, sl):
        for s in range(UNITS_PER_TILE):
            pltpu.make_async_copy(xy_out.at[pl.ds(unit_row(step, s), UNIT), :],
                                  xbuf.at[sl, pl.ds(s * UNIT, UNIT), :], gsem.at[sl]).start()

    def wait_gathers(sl):
        pltpu.make_async_copy(xy_out.at[pl.ds(0, TM_EXP), :], xbuf.at[sl], gsem.at[sl]).wait()

    def start_scatters(step, sl):
        for s in range(UNITS_PER_TILE):
            pltpu.make_async_copy(ybuf.at[sl, pl.ds(s * UNIT, UNIT), :],
                                  xy_out.at[pl.ds(unit_row(step, s), UNIT), pl.ds(0, D_MODEL)],
                                  ssem.at[sl]).start()

    def wait_scatters(sl):
        pltpu.make_async_copy(ybuf.at[sl], xy_out.at[pl.ds(0, TM_EXP), pl.ds(0, D_MODEL)], ssem.at[sl]).wait()

    @pl.when(r == 0)
    def _():
        start_gathers(0, 0)

    @pl.when(jnp.logical_and(r < n_used, r >= 2))
    def _():
        wait_scatters(slot)

    def weight_copies(e, p):
        return [pltpu.make_async_copy(src.at[e], dst.at[p], wsem.at[p])
                for src, dst in ((wg_hbm, wg_f), (wu_hbm, wu_f), (wd_hbm, wd_f))]

    @pl.when(jnp.logical_and(r < n_used,
                             jnp.logical_or(r == 0, te_ref[r] != te_ref[jnp.maximum(r - 1, 0)])))
    def _():
        e = te_ref[r]
        p = ep_ref[r]

        @pl.when(r == 0)
        def _():
            for cp in weight_copies(e, p):
                cp.start()

        for cp in weight_copies(e, p):
            cp.wait()
        wg_s[...] = wg_f[p].astype(BF16)
        wu_s[...] = wu_f[p].astype(BF16)
        wd_s[...] = wd_f[p].astype(BF16)

        @pl.when(ne_ref[r] >= 0)
        def _():
            for cp in weight_copies(ne_ref[r], 1 - p):
                cp.start()

    @pl.when(r < n_used)
    def _():
        wait_gathers(slot)
        start_gathers(jnp.minimum(r + 1, last), 1 - slot)
        x = xbuf[slot, :, 0:D_MODEL]
        side = xbuf[slot, :, D_MODEL:XY_COLS].astype(F32)
        lane = lax.broadcasted_iota(jnp.int32, side.shape, 1)

        def lanes_sum(a, b):
            return jnp.sum(jnp.where((lane == a) | (lane == b), side, 0.0), axis=-1, keepdims=True)

        is_slot1 = lanes_sum(E1_LANE, E1_LANE) == te_ref[r].astype(F32)
        wrow = jnp.where(is_slot1, lanes_sum(*W1_LANES), lanes_sum(*W2_LANES))
        a = jnp.dot(x, wg_s[...], preferred_element_type=F32)
        u = jnp.dot(x, wu_s[...], preferred_element_type=F32)
        hid = (a * jax.nn.sigmoid(a) * u * wrow).astype(BF16)
        ybuf[slot] = jnp.dot(hid, wd_s[...], preferred_element_type=F32).astype(BF16)
        start_scatters(r, slot)

    @pl.when(r == n_used - 1)
    def _():
        wait_gathers(1 - slot)
        wait_scatters(slot)

        @pl.when(r >= 1)
        def _():
            wait_scatters(1 - slot)


def _experts(tile_expert, n_used, unit_rows, next_expert, expert_parity, xy, wg, wu, wd):
    grid_spec = pltpu.PrefetchScalarGridSpec(
        num_scalar_prefetch=5,
        grid=(N_EXP_TILES,),
        in_specs=[pl.BlockSpec(memory_space=pl.ANY),
                  pl.BlockSpec(memory_space=pl.ANY),
                  pl.BlockSpec(memory_space=pl.ANY),
                  pl.BlockSpec(memory_space=pl.ANY)],
        out_specs=pl.BlockSpec(memory_space=pl.ANY),
        scratch_shapes=[pltpu.VMEM((2, TM_EXP, XY_COLS), BF16),
                        pltpu.VMEM((2, TM_EXP, D_MODEL), BF16),
                        pltpu.VMEM((D_MODEL, D_FF_EXPERT), BF16),
                        pltpu.VMEM((D_MODEL, D_FF_EXPERT), BF16),
                        pltpu.VMEM((D_FF_EXPERT, D_MODEL), BF16),
                        pltpu.VMEM((2, D_MODEL, D_FF_EXPERT), F32),
                        pltpu.VMEM((2, D_MODEL, D_FF_EXPERT), F32),
                        pltpu.VMEM((2, D_FF_EXPERT, D_MODEL), F32),
                        pltpu.SemaphoreType.DMA((2,)),
                        pltpu.SemaphoreType.DMA((2,)),
                        pltpu.SemaphoreType.DMA((2,))])
    return pl.pallas_call(
        _experts_kernel,
        out_shape=jax.ShapeDtypeStruct(xy.shape, xy.dtype),
        grid_spec=grid_spec,
        input_output_aliases={5: 0},
        compiler_params=_cparams(1),
        name="experts",
    )(tile_expert, n_used, unit_rows, next_expert, expert_parity, xy, wg, wu, wd)


def _combine_kernel(x1_ref, rc_ref, gf_ref, gfin_ref, y_ref, o_ref):
    lp1 = rc_ref[:, 0:1]
    lp2 = rc_ref[:, 1:2]
    scol = lax.broadcasted_iota(jnp.int32, (TM_ROW, XY_ROWS), 1).astype(F32)
    pick = jnp.where((scol == lp1) | (scol == lp2), 1.0, 0.0).astype(BF16)
    y = jnp.dot(pick, y_ref[...], preferred_element_type=F32)
    xf = x1_ref[...] + gf_ref[...] * y
    rs = lax.rsqrt(jnp.mean(xf * xf, axis=-1, keepdims=True) + EPS)
    o_ref[...] = xf * rs * gfin_ref[...]


def _combine(x1, rcol, gate_f, g_final, xy):
    tm = TM_ROW
    tpb = SEQ // tm
    row = lambda i: (i, 0)
    return pl.pallas_call(
        _combine_kernel,
        out_shape=jax.ShapeDtypeStruct((N_TOK, D_MODEL), F32),
        grid=(N_TOK // tm,),
        in_specs=[pl.BlockSpec((tm, D_MODEL), row),
                  pl.BlockSpec((tm, LANES), row),
                  pl.BlockSpec((None, 1, D_MODEL), lambda i: (i // tpb, 0, 0)),
                  pl.BlockSpec((1, D_MODEL), lambda i: (0, 0)),
                  pl.BlockSpec((XY_ROWS, D_MODEL), row)],
        out_specs=pl.BlockSpec((tm, D_MODEL), row),
        compiler_params=_cparams(1),
        name="combine",
    )(x1, rcol, gate_f, g_final, xy)


def _t5_bucket_np():
    qi = np.arange(BLOCK)[:, None]
    kj = np.arange(2 * BLOCK)[None, :]
    dist = qi - kj + BLOCK
    n = np.maximum(dist, 0)
    max_exact = NUM_BUCKETS // 2
    nf = np.maximum(n, 1).astype(np.float32)
    large = max_exact + (np.log(nf / np.float32(max_exact)) / np.float32(math.log(MAX_DISTANCE / max_exact))
                         * np.float32(NUM_BUCKETS - max_exact)).astype(np.int32)
    large = np.minimum(large, NUM_BUCKETS - 1)
    bucket = np.where(n < max_exact, n, large)
    band = (dist >= 0) & (dist < WINDOW)
    return bucket.astype(np.int32), band


def kernel(x, c, w_ada, b_ada, g_norm_mix, g_norm_ffn, w_in, sinks, b_forget, w_proj_swa, w_proj_fox,
           w_out, rel_bias_table, w_router_group, b_router_group, w_router_expert, b_router_expert,
           w_gate_exp, w_up_exp, w_down_exp, g_final):
    l = 0
    x2 = x.reshape(N_TOK, D_MODEL)

    c16 = jnp.concatenate([c, jnp.zeros_like(c)], axis=0)
    mod = _ada(c16, w_ada[l], b_ada[l][None, :])[:BATCH]
    shift_m, scale_m, gate_m, shift_f, scale_f, gate_f = [
        m.reshape(BATCH, 1, D_MODEL) for m in jnp.split(mod, 6, axis=-1)]

    w = w_in[l]
    o_ka, o_va, o_qb = Q_A, Q_A + KV_A, Q_A + 2 * KV_A
    o_kb, o_vb, o_f = o_qb + W_B, o_qb + 2 * W_B, o_qb + 3 * W_B
    o_g = o_f + N_HEADS_FOX

    def dup(cols):
        heads = [cols[:, h * HEAD_DIM:(h + 1) * HEAD_DIM] for h in range(N_KV_HEADS_SWA)]
        return jnp.concatenate([hd for hd in heads for _ in range(2)], axis=1)

    carrier = DECAY_LANES * N_HEADS_FOX
    w_f = jnp.pad(jnp.repeat(w[:, o_f:o_g], DECAY_LANES, axis=1), ((0, 0), (0, LANES - carrier)))
    w_main = jnp.concatenate([w[:, :Q_A], dup(w[:, o_ka:o_va]), w[:, o_va:o_qb], w_f,
                              w[:, o_qb:o_f]], axis=1).astype(BF16)
    w_g = w[:, o_g:].astype(BF16)
    qa, kdup, va, qb, kb, vb, f_pad, gates, nrm = _inproj(
        x2, scale_m, shift_m, g_norm_mix[l][None, :], w_main, w_g)

    b_pad = jnp.pad(jnp.repeat(b_forget[l], DECAY_LANES), (0, LANES - carrier))[None, :]
    lanes = np.arange(LANES)
    jmod = jnp.asarray(np.where(lanes < DECAY_LANES * N_HEADS_FOX, lanes % DECAY_LANES, 7)[None, :].astype(np.int32))
    dq, dk, fb = _cum(f_pad, b_pad, jmod)

    bucket, band = _t5_bucket_np()
    onehot = jnp.asarray(bucket[None] == np.arange(NUM_BUCKETS)[:, None, None], dtype=F32)
    bias = jnp.einsum("bh,bqk->hqk", rel_bias_table.astype(F32), onehot, precision=HIGHEST)
    bias = jnp.where(band[None], bias * LOG2E, NEG_INF)
    first = np.arange(2 * BLOCK)[None, None, :] < BLOCK
    bias = jnp.stack([jnp.where(first, NEG_INF, bias), bias]).reshape(2, N_KV_HEADS_SWA, -1, 2 * BLOCK)
    o_a = _swa(sinks[l].astype(F32) * LOG2E, qa, kdup, va, bias)

    o_b = _fox(_fox_first_tiles(nrm, fb), qb, kb, vb, dq, dk)

    w_r = jnp.concatenate([w_router_group[l]] + [w_router_expert[l][g] for g in range(N_GROUPS)], axis=1)
    w_r = jnp.pad(w_r, ((0, 0), (0, LANES - w_r.shape[1])))
    wr_hi = w_r.astype(BF16)
    wr_lo = (w_r - wr_hi.astype(F32)).astype(BF16)
    wr2 = jnp.concatenate([wr_hi, wr_lo], axis=1)
    b_r = jnp.concatenate([b_router_group[l], b_router_expert[l].reshape(-1)])
    b_r = jnp.pad(b_r, (0, LANES - b_r.shape[0]))[None, :]
    x1, xy, rcol, cu = _post(x2, o_a, o_b, gates, gate_m, scale_f, shift_f, g_norm_ffn[l][None, :],
                             w_proj_swa[l].astype(BF16), w_proj_fox[l].astype(BF16), w_out[l].astype(BF16),
                             wr2, b_r)

    i32 = jnp.int32
    n_tok_tiles = N_TOK // TM_POST
    cu = cu.reshape(n_tok_tiles, 8, LANES)[:, 0, :N_EXPERTS].astype(i32)
    loc_u = jnp.cumsum(cu, axis=1) - cu
    cend = jnp.cumsum(cu, axis=0)
    cstart = cend - cu
    tot_u = cend[-1]
    tiles_e = (tot_u + UNITS_PER_TILE - 1) // UNITS_PER_TILE
    tile_end = jnp.cumsum(tiles_e)
    tile_start = tile_end - tiles_e
    r = jnp.arange(N_EXP_TILES, dtype=i32)
    tile_expert = jnp.minimum(jnp.sum((tile_end[None, :] <= r[:, None]).astype(i32), axis=1), N_EXPERTS - 1)
    sel_e = tile_expert[:, None] == jnp.arange(N_EXPERTS, dtype=i32)[None, :]
    tw = r - jnp.sum(jnp.where(sel_e, tile_start[None, :], 0), axis=1)
    tot_r = jnp.sum(jnp.where(sel_e, tot_u[None, :], 0), axis=1)
    n_used = tile_end[-1:].astype(i32)
    q = tw[:, None] * UNITS_PER_TILE + jnp.arange(UNITS_PER_TILE, dtype=i32)[None, :]

    def of_expert(tab):
        return jnp.sum(jnp.where(sel_e[:, None, :], tab[None, :, :], 0), axis=2)

    cend_r, cstart_r, loc_r = of_expert(cend), of_expert(cstart), of_expert(loc_u)
    src_tile = jnp.minimum(jnp.sum((cend_r[:, None, :] <= q[:, :, None]).astype(i32), axis=2), n_tok_tiles - 1)
    sel_t = src_tile[:, :, None] == jnp.arange(n_tok_tiles, dtype=i32)[None, None, :]
    k = (q - jnp.sum(jnp.where(sel_t, cstart_r[:, None, :], 0), axis=2)
         + jnp.sum(jnp.where(sel_t, loc_r[:, None, :], 0), axis=2))
    real_rows = src_tile * XY_ROWS + k * UNIT
    pad_rows = PAD_BASE_ROW + (tile_expert[:, None] * PAD_UNITS_PER_EXPERT + (q - tot_r[:, None])) * UNIT
    idle_row = PAD_BASE_ROW + N_EXPERTS * PAD_UNITS_PER_EXPERT * UNIT
    unit_rows = jnp.where(q < tot_r[:, None], real_rows, pad_rows)
    unit_rows = jnp.where((r < n_used)[:, None], unit_rows, idle_row).reshape(-1).astype(i32)

    eid = jnp.arange(N_EXPERTS, dtype=i32)
    used = tiles_e > 0
    later_used = (eid[None, :] > eid[:, None]) & used[None, :]
    next_e = jnp.min(jnp.where(later_used, eid[None, :], N_EXPERTS), axis=1)
    next_e = jnp.where(next_e == N_EXPERTS, -1, next_e)
    parity_e = (jnp.cumsum(used.astype(i32)) - used.astype(i32)) % 2
    next_expert = jnp.sum(jnp.where(sel_e, next_e[None, :], 0), axis=1).astype(i32)
    expert_parity = jnp.sum(jnp.where(sel_e, parity_e[None, :], 0), axis=1).astype(i32)

    xy = _experts(tile_expert.astype(i32), n_used, unit_rows, next_expert, expert_parity, xy,
                  w_gate_exp[l].reshape(N_EXPERTS, D_MODEL, D_FF_EXPERT),
                  w_up_exp[l].reshape(N_EXPERTS, D_MODEL, D_FF_EXPERT),
                  w_down_exp[l].reshape(N_EXPERTS, D_FF_EXPERT, D_MODEL))
    out = _combine(x1, rcol, gate_f, g_final[None, :], xy)
    return out.reshape(BATCH, SEQ, D_MODEL)
```

```python
import math

import numpy as np
import jax
import jax.numpy as jnp
from jax import lax
from jax.experimental import pallas as pl
from jax.experimental.pallas import tpu as pltpu

F32 = jnp.float32
BF16 = jnp.bfloat16
HIGHEST = lax.Precision.HIGHEST

D_MODEL = 1024
BATCH = 8
SEQ = 4096
N_TOK = BATCH * SEQ
N_HEADS_SWA = 8
N_KV_HEADS_SWA = 2
N_HEADS_FOX = 8
HEAD_DIM = 64
WINDOW = 128
BLOCK = 128
NUM_BUCKETS = 32
MAX_DISTANCE = 128
N_GROUPS = 4
EXPERTS_PER_GROUP = 8
N_EXPERTS = N_GROUPS * EXPERTS_PER_GROUP
D_FF_EXPERT = 256
EPS = 1e-6
NEG_INF = -1e30

Q_A = N_HEADS_SWA * HEAD_DIM
KV_A = N_KV_HEADS_SWA * HEAD_DIM
W_B = N_HEADS_FOX * HEAD_DIM
LANES = 128
QK_SCALE = HEAD_DIM ** -0.5

TM_IN = 512
TM_POST = 512
TQ_FOX = 512
TK_FOX = TQ_FOX
SWA_BLOCKS = 4
TM_EXP = 512
TM_ROW = 512
UNIT = 16
XY_UNITS = 2 * TM_POST // UNIT + N_EXPERTS
XY_ROWS = XY_UNITS * UNIT
XY_COLS = D_MODEL + LANES
UNITS_PER_TILE = TM_EXP // UNIT
N_TOK_TILES = N_TOK // TM_POST
N_EXP_TILES = N_TOK_TILES * XY_UNITS // UNITS_PER_TILE + N_EXPERTS
PAD_UNITS_PER_EXPERT = UNITS_PER_TILE - 1
PAD_BLOCKS = -(-(N_EXPERTS * PAD_UNITS_PER_EXPERT * UNIT) // XY_ROWS)
PAD_BASE_ROW = N_TOK_TILES * XY_ROWS
W1_LANES, W2_LANES, E1_LANE, E2_LANE = (4, 6), (5, 7), 8, 9
VMEM_LIMIT = 56 * 1024 * 1024

DECAY_LANES = 6
LOG2E = math.log2(math.e)
Q_SCALE_LOG2 = QK_SCALE * LOG2E
PRUNE_MARGIN = 160.0


def _cparams(n_axes):
    return pltpu.CompilerParams(dimension_semantics=("arbitrary",) * n_axes,
                                vmem_limit_bytes=VMEM_LIMIT)


def _ada_kernel(c_ref, w_ref, b_ref, o_ref):
    c = c_ref[...]
    ca = c * jax.nn.sigmoid(c)
    o_ref[...] = jnp.dot(ca.astype(BF16), w_ref[...].astype(BF16),
                         preferred_element_type=F32) + b_ref[...]


def _ada(c16, w_ada, b_ada):
    n_out = w_ada.shape[1]
    blk = 1024
    return pl.pallas_call(
        _ada_kernel,
        out_shape=jax.ShapeDtypeStruct((16, n_out), F32),
        grid=(n_out // blk,),
        in_specs=[pl.BlockSpec((16, D_MODEL), lambda j: (0, 0)),
                  pl.BlockSpec((D_MODEL, blk), lambda j: (0, j)),
                  pl.BlockSpec((1, blk), lambda j: (0, j))],
        out_specs=pl.BlockSpec((16, blk), lambda j: (0, j)),
        compiler_params=_cparams(1),
        name="ada",
    )(c16, w_ada, b_ada)


def _inproj_kernel(x_ref, sc_ref, sh_ref, g_ref, wm_ref, wg_ref, ind_ref,
                   qa_ref, kd_ref, va_ref, qb_ref, kb_ref, vb_ref, f_ref, gt_ref, nrm_ref):
    x = x_ref[...]
    rs = lax.rsqrt(jnp.mean(x * x, axis=-1, keepdims=True) + EPS)
    a = g_ref[...] * (1.0 + sc_ref[...])
    h = (x * rs * a + sh_ref[...]).astype(BF16)

    def mm(w):
        return jnp.dot(h, w, preferred_element_type=F32)

    qa_ref[...] = (mm(wm_ref[:, 0:512]) * Q_SCALE_LOG2).astype(BF16)
    kd_ref[...] = mm(wm_ref[:, 512:768]).astype(BF16)
    vf = mm(wm_ref[:, 768:1024])
    f_ref[...] = vf[:, LANES:2 * LANES]
    v = vf[:, 0:LANES]
    vr = pltpu.roll(v, HEAD_DIM, 1)
    lo = lax.broadcasted_iota(jnp.int32, v.shape, 1) < HEAD_DIM
    va_ref[:, 0:LANES] = jnp.where(lo, v, vr).astype(BF16)
    va_ref[:, LANES:2 * LANES] = jnp.where(lo, vr, v).astype(BF16)
    qb = (mm(wm_ref[:, 1024:1536]) * Q_SCALE_LOG2).astype(BF16)
    kb = mm(wm_ref[:, 1536:2048]).astype(BF16)
    qb_ref[...] = qb
    kb_ref[...] = kb
    vb_ref[...] = mm(wm_ref[:, 2048:2560]).astype(BF16)
    sq = jnp.concatenate([qb, kb], axis=1).astype(F32)
    seg = jnp.dot((sq * sq).astype(BF16), ind_ref[...], preferred_element_type=F32)
    nrm_ref[...] = jnp.broadcast_to(jnp.max(seg, axis=0, keepdims=True), nrm_ref.shape)
    gt_ref[...] = mm(wg_ref[...]).astype(BF16)


def _inproj(x2, scale_m, shift_m, g_mix, w_main, w_g):
    tm = TM_IN
    tpb = SEQ // tm
    row = lambda i: (i, 0)
    per_b = lambda i: (i // tpb, 0, 0)
    const = lambda i: (0, 0)
    outs = [(Q_A, BF16), (2 * KV_A, BF16), (2 * KV_A, BF16), (W_B, BF16), (W_B, BF16), (W_B, BF16),
            (LANES, F32), (2 * D_MODEL, BF16)]
    ind_np = np.zeros((2 * W_B, LANES), np.float32)
    ind_np[np.arange(2 * W_B), np.arange(2 * W_B) // HEAD_DIM] = 1.0
    ind = jnp.asarray(ind_np, dtype=BF16)
    n_steps = N_TOK // tm
    return pl.pallas_call(
        _inproj_kernel,
        out_shape=[jax.ShapeDtypeStruct((N_TOK, w), dt) for w, dt in outs]
        + [jax.ShapeDtypeStruct((n_steps * 8, LANES), F32)],
        grid=(n_steps,),
        in_specs=[pl.BlockSpec((tm, D_MODEL), row),
                  pl.BlockSpec((None, 1, D_MODEL), per_b),
                  pl.BlockSpec((None, 1, D_MODEL), per_b),
                  pl.BlockSpec((1, D_MODEL), const),
                  pl.BlockSpec(w_main.shape, const),
                  pl.BlockSpec(w_g.shape, const),
                  pl.BlockSpec(ind.shape, const)],
        out_specs=[pl.BlockSpec((tm, w), row) for w, _ in outs] + [pl.BlockSpec((8, LANES), row)],
        compiler_params=_cparams(1),
        name="inproj",
    )(x2, scale_m, shift_m, g_mix, w_main, w_g, ind)


def _log_sigmoid(x):
    return jnp.minimum(x, 0.0) - jnp.log1p(jnp.exp(-jnp.abs(x)))


def _cum_kernel(f_ref, b_ref, jm_ref, qa_ref, ka_ref, fb_ref):
    cum = _log_sigmoid(f_ref[...] + b_ref[...]) * LOG2E
    row = lax.broadcasted_iota(jnp.int32, cum.shape, 0)
    k = 1
    while k < SEQ:
        if k < 8:
            shifted = jnp.where(row >= k, pltpu.roll(cum, k, 0), 0.0)
        else:
            shifted = jnp.concatenate([jnp.zeros((k, LANES), F32), cum[:SEQ - k]], axis=0)
        cum = cum + shifted
        k *= 2
    jm = jm_ref[...]
    for blk in range(SEQ // LANES):
        rows = slice(blk * LANES, (blk + 1) * LANES)
        cb = cum[rows]
        carry = cb[LANES - 1:LANES]
        hi = cb.astype(BF16).astype(F32)
        r1 = cb - hi
        mid = r1.astype(BF16).astype(F32)
        lo = (r1 - mid).astype(BF16).astype(F32)
        one = jnp.ones_like(cb)
        zero = jnp.zeros_like(cb)
        qa = jnp.where(jm == 0, hi, jnp.where(jm == 1, mid, jnp.where(jm == 2, lo,
                       jnp.where(jm < DECAY_LANES, one, zero))))
        ka = jnp.where(jm == 3, -hi, jnp.where(jm == 4, -mid, jnp.where(jm == 5, -lo,
                       jnp.where(jm < 3, one, zero))))
        qa_ref[rows, :] = qa.astype(BF16)
        ka_ref[rows, :] = ka.astype(BF16)
        blocks_per_tile = TQ_FOX // LANES
        tile = blk // blocks_per_tile
        if blk % blocks_per_tile == 0:
            fb_ref[2 * tile:2 * tile + 1, :] = cb[0:1]
        if blk % blocks_per_tile == blocks_per_tile - 1:
            fb_ref[2 * tile + 1:2 * tile + 2, :] = carry


def _cum(f_pad, b_pad, jmod):
    n_tiles = SEQ // TQ_FOX
    return pl.pallas_call(
        _cum_kernel,
        out_shape=[jax.ShapeDtypeStruct((BATCH, SEQ, LANES), BF16)] * 2
        + [jax.ShapeDtypeStruct((BATCH, 2 * n_tiles, LANES), F32)],
        grid=(BATCH,),
        in_specs=[pl.BlockSpec((SEQ, LANES), lambda b: (b, 0)),
                  pl.BlockSpec((1, LANES), lambda b: (0, 0)),
                  pl.BlockSpec((1, LANES), lambda b: (0, 0))],
        out_specs=[pl.BlockSpec((None, SEQ, LANES), lambda b: (b, 0, 0))] * 2
        + [pl.BlockSpec((None, 2 * n_tiles, LANES), lambda b: (b, 0, 0))],
        compiler_params=_cparams(1),
        name="cum",
    )(f_pad, b_pad, jmod)


def _swa_block(sink_cols, q, kk, vv, bias_ref, lo):
    tiles = []
    for g in range(N_KV_HEADS_SWA):
        parts = []
        for t in range(2):
            qt = q[:, (2 * g + t) * LANES:(2 * g + t + 1) * LANES]
            zero = jnp.zeros_like(qt)
            parts.append(jnp.where(lo, qt, zero))
            parts.append(jnp.where(lo, zero, qt))
        q4 = jnp.concatenate(parts, axis=0)
        s = lax.dot_general(q4, kk[:, g * LANES:(g + 1) * LANES], (((1,), (1,)), ((), ())),
                            preferred_element_type=F32)
        s = s + bias_ref[g]
        sink = sink_cols[g]
        m = jnp.maximum(jnp.max(s, axis=-1, keepdims=True), sink)
        p = jnp.exp2(s - m)
        den = jnp.sum(p, axis=-1, keepdims=True) + jnp.exp2(sink - m)
        o = jnp.dot(p.astype(BF16), vv[:, g * LANES:(g + 1) * LANES],
                    preferred_element_type=F32) / den
        tiles.append(jnp.where(lo, o[0:BLOCK], o[BLOCK:2 * BLOCK]))
        tiles.append(jnp.where(lo, o[2 * BLOCK:3 * BLOCK], o[3 * BLOCK:4 * BLOCK]))
    return tiles


def _swa_kernel(sink_ref, q_ref, kc_ref, kp_ref, vc_ref, vp_ref, bias_first_ref, bias_ref, o_ref):
    lane = lax.broadcasted_iota(jnp.int32, (BLOCK, LANES), 1)
    lo = lane < HEAD_DIM
    grp = N_HEADS_SWA // N_KV_HEADS_SWA
    row = lax.broadcasted_iota(jnp.int32, (grp * BLOCK, 1), 0)
    sink_cols = []
    for g in range(N_KV_HEADS_SWA):
        col = jnp.full((grp * BLOCK, 1), sink_ref[g * grp + grp - 1], F32)
        for hh in range(grp - 2, -1, -1):
            col = jnp.where(row < (hh + 1) * BLOCK, sink_ref[g * grp + hh], col)
        sink_cols.append(col)
    for blk in range(SWA_BLOCKS):
        rows = slice(blk * BLOCK, (blk + 1) * BLOCK)
        if blk == 0:
            kk = jnp.concatenate([kp_ref[...], kc_ref[rows, :]], axis=0)
            vv = jnp.concatenate([vp_ref[...], vc_ref[rows, :]], axis=0)
            bias = bias_first_ref
        else:
            prev_rows = slice((blk - 1) * BLOCK, (blk + 1) * BLOCK)
            kk = kc_ref[prev_rows, :]
            vv = vc_ref[prev_rows, :]
            bias = bias_ref
        tiles = _swa_block(sink_cols, q_ref[rows, :], kk, vv, bias, lo)
        for c, tile in enumerate(tiles):
            o_ref[rows, c * LANES:(c + 1) * LANES] = tile.astype(BF16)


def _swa(sinks, qa, kdup, va, bias):
    nb = SEQ // BLOCK
    ns = nb // SWA_BLOCKS
    cur = lambda b, i, s: (b * ns + i, 0)
    prev = lambda b, i, s: (b * nb + jnp.maximum(SWA_BLOCKS * i - 1, 0), 0)
    grid_spec = pltpu.PrefetchScalarGridSpec(
        num_scalar_prefetch=1,
        grid=(BATCH, ns),
        in_specs=[pl.BlockSpec((SWA_BLOCKS * BLOCK, Q_A), cur),
                  pl.BlockSpec((SWA_BLOCKS * BLOCK, 2 * KV_A), cur),
                  pl.BlockSpec((BLOCK, 2 * KV_A), prev),
                  pl.BlockSpec((SWA_BLOCKS * BLOCK, 2 * KV_A), cur),
                  pl.BlockSpec((BLOCK, 2 * KV_A), prev),
                  pl.BlockSpec((None,) + bias.shape[1:], lambda b, i, s: (jnp.minimum(i, 1), 0, 0, 0)),
                  pl.BlockSpec((None,) + bias.shape[1:], lambda b, i, s: (1, 0, 0, 0))],
        out_specs=pl.BlockSpec((SWA_BLOCKS * BLOCK, Q_A), cur))
    return pl.pallas_call(
        _swa_kernel,
        out_shape=jax.ShapeDtypeStruct((N_TOK, Q_A), BF16),
        grid_spec=grid_spec,
        compiler_params=_cparams(2),
        name="swa",
    )(sinks, qa, kdup, kdup, va, va, bias, bias)


def _fox_kernel(js_ref, q_ref, k_ref, v_ref, qa_ref, ka_ref, o_ref,
                kaug, vaug, q2, m_sc, acc_sc, s_a, s_b):
    tq, tk = TQ_FOX, TK_FOX
    b = pl.program_id(0)
    t = pl.program_id(1)
    i = pl.program_id(2)
    j_start = js_ref[(b * pl.num_programs(1) + t) * pl.num_programs(2) + i]

    @pl.when(i == 0)
    def _():
        kaug[:, 0:LANES] = k_ref[...]
        kaug[:, LANES:2 * LANES] = ka_ref[...]
        vaug[:, 0:LANES] = v_ref[...]
        vaug[:, LANES:2 * LANES] = jnp.ones((SEQ, LANES), BF16)

    lane = lax.broadcasted_iota(jnp.int32, (tq, LANES), 1)
    lo = lane < HEAD_DIM
    base = 2 * DECAY_LANES * t
    own = [(lane >= base + h * DECAY_LANES) & (lane < base + (h + 1) * DECAY_LANES) for h in range(2)]
    q = q_ref[...]
    qa = qa_ref[...]
    zero = jnp.zeros_like(q)
    q2[0, :, 0:LANES] = jnp.where(lo, q, zero)
    q2[1, :, 0:LANES] = jnp.where(lo, zero, q)
    for h in range(2):
        q2[h, :, LANES:2 * LANES] = jnp.where(own[h], qa, zero)
    m_sc[...] = jnp.full(m_sc.shape, NEG_INF, F32)
    acc_sc[...] = jnp.zeros(acc_sc.shape, F32)

    def scores(h, ks):
        return lax.dot_general(q2[h], kaug[pl.ds(ks, tk), :], (((1,), (1,)), ((), ())),
                               preferred_element_type=F32)

    def consume(h, s, ks, mask):
        if mask is not None:
            s = jnp.where(mask, s, NEG_INF)
        m_prev = m_sc[h]
        m_new = jnp.maximum(m_prev, jnp.max(s, axis=-1, keepdims=True))
        alpha = jnp.exp2(m_prev - m_new)
        p = jnp.exp2(s - jnp.concatenate([m_new] * (tk // LANES), axis=1))
        pv = jnp.dot(p.astype(BF16), vaug[pl.ds(ks, tk), :], preferred_element_type=F32)
        acc_sc[h] = jnp.concatenate([alpha, alpha], axis=1) * acc_sc[h] + pv
        m_sc[h] = m_new

    def key_start(j):
        return pl.multiple_of(j * tk, tk)

    def scores_into(buf, j):
        for h in range(2):
            buf[h] = scores(h, key_start(j))

    def consume_from(buf, j, mask):
        for h in range(2):
            consume(h, buf[h], key_start(j), mask)

    rr = lax.broadcasted_iota(jnp.int32, (tq, tk), 0)
    cc = lax.broadcasted_iota(jnp.int32, (tq, tk), 1)
    causal = cc <= rr
    n_full = i - j_start

    scores_into(s_a, j_start)

    def pair(p, carry):
        j = j_start + 2 * p
        scores_into(s_b, j + 1)
        consume_from(s_a, j, None)
        scores_into(s_a, j + 2)
        consume_from(s_b, j + 1, None)
        return carry

    lax.fori_loop(0, n_full // 2, pair, 0)
    odd = lax.rem(n_full, 2) == 1

    @pl.when(odd)
    def _():
        scores_into(s_b, i)
        consume_from(s_a, i - 1, None)
        consume_from(s_b, i, causal)

    @pl.when(jnp.logical_not(odd))
    def _():
        consume_from(s_a, i, causal)

    outs = [acc_sc[h, :, 0:LANES] / acc_sc[h, :, LANES:2 * LANES] for h in range(2)]
    o_ref[...] = jnp.where(lo, outs[0], outs[1]).astype(BF16)


def _fox(j_start, qb, kb, vb, qa, ka):
    tq = TQ_FOX
    nq = SEQ // tq
    n_pairs = N_HEADS_FOX // 2
    qmap = lambda b, t, i, js: (b * nq + i, t)
    kmap = lambda b, t, i, js: (b, t)
    grid_spec = pltpu.PrefetchScalarGridSpec(
        num_scalar_prefetch=1,
        grid=(BATCH, n_pairs, nq),
        in_specs=[pl.BlockSpec((tq, LANES), qmap),
                  pl.BlockSpec((SEQ, LANES), kmap),
                  pl.BlockSpec((SEQ, LANES), kmap),
                  pl.BlockSpec((None, tq, LANES), lambda b, t, i, js: (b, i, 0)),
                  pl.BlockSpec((None, SEQ, LANES), lambda b, t, i, js: (b, 0, 0))],
        out_specs=pl.BlockSpec((tq, LANES), qmap),
        scratch_shapes=[pltpu.VMEM((SEQ, 2 * LANES), BF16),
                        pltpu.VMEM((SEQ, 2 * LANES), BF16),
                        pltpu.VMEM((2, tq, 2 * LANES), BF16),
                        pltpu.VMEM((2, tq, LANES), F32),
                        pltpu.VMEM((2, tq, 2 * LANES), F32),
                        pltpu.VMEM((2, tq, TK_FOX), F32),
                        pltpu.VMEM((2, tq, TK_FOX), F32)])
    return pl.pallas_call(
        _fox_kernel,
        out_shape=jax.ShapeDtypeStruct((N_TOK, W_B), BF16),
        grid_spec=grid_spec,
        compiler_params=_cparams(3),
        name="fox",
    )(j_start, qb, kb, vb, qa, ka)


def _fox_first_tiles(nrm, fb):
    n_tiles = SEQ // TQ_FOX
    nr = nrm.reshape(BATCH, SEQ // TM_IN, 8, LANES)[:, :, 0, :] * 1.02
    nr = jnp.repeat(nr, TM_IN // TQ_FOX, axis=1)
    qn = jnp.sqrt(nr[..., 0:N_HEADS_FOX])
    kn = jnp.sqrt(nr[..., N_HEADS_FOX:2 * N_HEADS_FOX])
    f_first = fb[:, 0::2, 0:DECAY_LANES * N_HEADS_FOX:DECAY_LANES]
    f_last = fb[:, 1::2, 0:DECAY_LANES * N_HEADS_FOX:DECAY_LANES]
    kn_prefix = lax.cummax(kn, axis=1)
    upper = qn[:, :, None, :] * kn_prefix[:, None, :, :] + f_first[:, :, None, :] - f_last[:, None, :, :]
    row_max_low = -(qn * kn)[:, :, None, :]
    ii = jnp.arange(n_tiles)[None, :, None, None]
    jj = jnp.arange(n_tiles)[None, None, :, None]
    skip = (upper < row_max_low - PRUNE_MARGIN) & (jj < ii)
    skip = jnp.all(skip.reshape(BATCH, n_tiles, n_tiles, N_HEADS_FOX // 2, 2), axis=-1)
    first = jnp.sum(jnp.cumprod(skip.astype(jnp.int32), axis=2), axis=2)
    return jnp.transpose(first, (0, 2, 1)).reshape(-1).astype(jnp.int32)


def _post_kernel(x_ref, oa_ref, ob_ref, gt_ref, gm_ref, sc_ref, sh_ref, g_ref,
                 wa_ref, wb_ref, wo_ref, wr2_ref, br_ref,
                 x1_ref, xy_ref, rc_ref, cu_ref, hh_prev, lg_prev):
    step = pl.program_id(0)

    @pl.when(step == 0)
    def _():
        hh_prev[...] = jnp.zeros(hh_prev.shape, BF16)
        lg_prev[...] = jnp.zeros(lg_prev.shape, F32)

    @pl.when(step <= N_TOK_TILES)
    def _():
        hh_p = hh_prev[...]
        lg_p = lg_prev[...]
        hh, logits = _post_mix(x_ref, oa_ref, ob_ref, gt_ref, gm_ref, sc_ref, sh_ref, g_ref,
                               wa_ref, wb_ref, wo_ref, wr2_ref, br_ref, x1_ref)
        _post_route(hh_p, lg_p, xy_ref, rc_ref, cu_ref)
        hh_prev[...] = hh
        lg_prev[...] = logits

    @pl.when(step > N_TOK_TILES)
    def _():
        xy_ref[...] = jnp.zeros(xy_ref.shape, BF16)


def _post_mix(x_ref, oa_ref, ob_ref, gt_ref, gm_ref, sc_ref, sh_ref, g_ref,
              wa_ref, wb_ref, wo_ref, wr2_ref, br_ref, x1_ref):
    pa = jnp.dot(oa_ref[...], wa_ref[...], preferred_element_type=F32)
    pb = jnp.dot(ob_ref[...], wb_ref[...], preferred_element_type=F32)
    ga = jax.nn.sigmoid(gt_ref[:, 0:D_MODEL].astype(F32))
    gb = jax.nn.sigmoid(gt_ref[:, D_MODEL:2 * D_MODEL].astype(F32))
    merged = (ga * pa + gb * pb).astype(BF16)
    y = jnp.dot(merged, wo_ref[...], preferred_element_type=F32)
    x1 = x_ref[...] + gm_ref[...] * y
    x1_ref[...] = x1

    rs = lax.rsqrt(jnp.mean(x1 * x1, axis=-1, keepdims=True) + EPS)
    a = g_ref[...] * (1.0 + sc_ref[...])
    h2 = x1 * rs * a + sh_ref[...]

    hh = h2.astype(BF16)
    hl = (h2 - hh.astype(F32)).astype(BF16)
    hi_both = jnp.dot(hh, wr2_ref[...], preferred_element_type=F32)
    logits = (hi_both[:, 0:LANES] + hi_both[:, LANES:2 * LANES]
              + jnp.dot(hl, wr2_ref[:, 0:LANES], preferred_element_type=F32)
              + br_ref[...])
    return hh, logits


def _post_route(hh, logits, xy_ref, rc_ref, cu_ref):
    tm = TM_POST
    lane = lax.broadcasted_iota(jnp.int32, (tm, LANES), 1).astype(F32)
    big = float(LANES)
    gl = jnp.where(lane < N_GROUPS, logits, -jnp.inf)
    gmax = jnp.max(gl, axis=-1, keepdims=True)
    gi = jnp.min(jnp.where(gl == gmax, lane, big), axis=-1, keepdims=True)
    gsum = jnp.sum(jnp.exp(gl - gmax), axis=-1, keepdims=True)
    gp = 1.0 / gsum
    e_lo = N_GROUPS + EXPERTS_PER_GROUP * gi
    el = jnp.where((lane >= e_lo) & (lane < e_lo + EXPERTS_PER_GROUP), logits, -jnp.inf)
    v1 = jnp.max(el, axis=-1, keepdims=True)
    i1 = jnp.min(jnp.where(el == v1, lane, big), axis=-1, keepdims=True)
    el2 = jnp.where(lane == i1, -jnp.inf, el)
    v2 = jnp.max(el2, axis=-1, keepdims=True)
    i2 = jnp.min(jnp.where(el2 == v2, lane, big), axis=-1, keepdims=True)
    e21 = jnp.exp(v2 - v1)
    w1 = gp / (1.0 + e21)
    w2 = gp * e21 / (1.0 + e21)
    e1 = i1 - N_GROUPS
    e2 = i2 - N_GROUPS

    oh = jnp.where((lane == e1) | (lane == e2), 1.0, 0.0)
    cnt_u = jnp.floor((jnp.sum(oh, axis=0, keepdims=True) + (UNIT - 1)) * (1.0 / UNIT))
    r128 = lax.broadcasted_iota(jnp.int32, (LANES, LANES), 0)
    c128 = lax.broadcasted_iota(jnp.int32, (LANES, LANES), 1)
    before_lane = jnp.where(r128 < c128, 1.0, 0.0).astype(BF16)
    loc_u = jnp.dot(jnp.broadcast_to(cnt_u, (8, LANES)).astype(BF16), before_lane,
                    preferred_element_type=F32)
    trow = lax.broadcasted_iota(jnp.int32, (tm, LANES), 0)
    seen = oh
    k = 1
    while k < tm:
        if k < 8:
            shifted = jnp.where(trow >= k, pltpu.roll(seen, k, 0), 0.0)
        else:
            shifted = jnp.concatenate([jnp.zeros((k, LANES), F32), seen[:tm - k]], axis=0)
        seen = seen + shifted
        k *= 2
    pos_e = (seen - oh) + loc_u[0:1] * UNIT
    lp1 = jnp.sum(jnp.where(lane == e1, pos_e, 0.0), axis=-1, keepdims=True)
    lp2 = jnp.sum(jnp.where(lane == e2, pos_e, 0.0), axis=-1, keepdims=True)

    def to_row(col):
        return jnp.transpose(jnp.broadcast_to(col, (tm, LANES)))[0:1]

    srow = lax.broadcasted_iota(jnp.int32, (XY_ROWS, tm), 0).astype(F32)
    pm1 = jnp.where(srow == to_row(lp1), 1.0, 0.0).astype(BF16)
    pm2 = jnp.where(srow == to_row(lp2), 1.0, 0.0).astype(BF16)
    w1h = w1.astype(BF16).astype(F32)
    w2h = w2.astype(BF16).astype(F32)
    side = jnp.where(lane == W1_LANES[0], w1h, jnp.where(lane == W1_LANES[1], w1 - w1h,
           jnp.where(lane == W2_LANES[0], w2h, jnp.where(lane == W2_LANES[1], w2 - w2h,
           jnp.where(lane == E1_LANE, e1, jnp.where(lane == E2_LANE, e2, 0.0))))))
    tok = jnp.concatenate([hh, side.astype(BF16)], axis=1)
    xy_ref[...] = jnp.dot(pm1 + pm2, tok, preferred_element_type=F32).astype(BF16)

    cu_ref[...] = jnp.broadcast_to(cnt_u, cu_ref.shape)
    rc_ref[...] = jnp.where(lane == 0, lp1, jnp.where(lane == 1, lp2, 0.0))


def _post(x2, oa, ob, gates, gate_m, scale_f, shift_f, g_ffn, wa, wb, wo, wr2, b_r):
    tm = TM_POST
    tpb = SEQ // tm
    n_steps = N_TOK_TILES
    row = lambda i: (jnp.minimum(i, n_steps - 1), 0)
    per_b = lambda i: (jnp.minimum(i, n_steps - 1) // tpb, 0, 0)
    routed = lambda i: (jnp.clip(i - 1, 0, n_steps - 1), 0)
    const = lambda i: (0, 0)
    return pl.pallas_call(
        _post_kernel,
        out_shape=[jax.ShapeDtypeStruct((N_TOK, D_MODEL), F32),
                   jax.ShapeDtypeStruct(((n_steps + PAD_BLOCKS) * XY_ROWS, XY_COLS), BF16),
                   jax.ShapeDtypeStruct((N_TOK, LANES), F32),
                   jax.ShapeDtypeStruct((n_steps * 8, LANES), F32)],
        grid=(n_steps + 1 + PAD_BLOCKS,),
        in_specs=[pl.BlockSpec((tm, D_MODEL), row),
                  pl.BlockSpec((tm, Q_A), row),
                  pl.BlockSpec((tm, W_B), row),
                  pl.BlockSpec((tm, 2 * D_MODEL), row),
                  pl.BlockSpec((None, 1, D_MODEL), per_b),
                  pl.BlockSpec((None, 1, D_MODEL), per_b),
                  pl.BlockSpec((None, 1, D_MODEL), per_b),
                  pl.BlockSpec((1, D_MODEL), const),
                  pl.BlockSpec(wa.shape, const),
                  pl.BlockSpec(wb.shape, const),
                  pl.BlockSpec(wo.shape, const),
                  pl.BlockSpec(wr2.shape, const),
                  pl.BlockSpec((1, LANES), const)],
        out_specs=[pl.BlockSpec((tm, D_MODEL), row),
                   pl.BlockSpec((XY_ROWS, XY_COLS), lambda i: (jnp.maximum(i - 1, 0), 0)),
                   pl.BlockSpec((tm, LANES), routed),
                   pl.BlockSpec((8, LANES), routed)],
        scratch_shapes=[pltpu.VMEM((tm, D_MODEL), BF16),
                        pltpu.VMEM((tm, LANES), F32)],
        compiler_params=_cparams(1),
        name="post",
    )(x2, oa, ob, gates, gate_m, scale_f, shift_f, g_ffn, wa, wb, wo, wr2, b_r)


def _experts_kernel(te_ref, nu_ref, ur_ref, ne_ref, ep_ref, xy_in, wg_hbm, wu_hbm, wd_hbm, xy_out,
                    xbuf, ybuf, wg_s, wu_s, wd_s, wg_f, wu_f, wd_f, gsem, ssem, wsem):
    del xy_in
    r = pl.program_id(0)
    last = pl.num_programs(0) - 1
    n_used = nu_ref[0]
    slot = lax.rem(r, 2)

    def unit_row(step, s):
        return pl.multiple_of(ur_ref[step * UNITS_PER_TILE + s], UNIT)

    def start_gathers(step, sl):
        for s in range(UNITS_PER_TILE):
            pltpu.make_async_copy(xy_out.at[pl.ds(unit_row(step, s), UNIT), :],
                                  xbuf.at[sl, pl.ds(s * UNIT, UNIT), :], gsem.at[sl]).start()

    def wait_gathers(sl):
        pltpu.make_async_copy(xy_out.at[pl.ds(0, TM_EXP), :], xbuf.at[sl], gsem.at[sl]).wait()

    def start_scatters(step, sl):
        for s in range(UNITS_PER_TILE):
            pltpu.make_async_copy(ybuf.at[sl, pl.ds(s * UNIT, UNIT), :],
                                  xy_out.at[pl.ds(unit_row(step, s), UNIT), pl.ds(0, D_MODEL)],
                                  ssem.at[sl]).start()

    def wait_scatters(sl):
        pltpu.make_async_copy(ybuf.at[sl], xy_out.at[pl.ds(0, TM_EXP), pl.ds(0, D_MODEL)], ssem.at[sl]).wait()

    @pl.when(r == 0)
    def _():
        start_gathers(0, 0)

    @pl.when(jnp.logical_and(r < n_used, r >= 2))
    def _():
        wait_scatters(slot)

    def weight_copies(e, p):
        return [pltpu.make_async_copy(src.at[e], dst.at[p], wsem.at[p])
                for src, dst in ((wg_hbm, wg_f), (wu_hbm, wu_f), (wd_hbm, wd_f))]

    @pl.when(jnp.logical_and(r < n_used,
                             jnp.logical_or(r == 0, te_ref[r] != te_ref[jnp.maximum(r - 1, 0)])))
    def _():
        e = te_ref[r]
        p = ep_ref[r]

        @pl.when(r == 0)
        def _():
            for cp in weight_copies(e, p):
                cp.start()

        for cp in weight_copies(e, p):
            cp.wait()
        wg_s[...] = wg_f[p].astype(BF16)
        wu_s[...] = wu_f[p].astype(BF16)
        wd_s[...] = wd_f[p].astype(BF16)

        @pl.when(ne_ref[r] >= 0)
        def _():
            for cp in weight_copies(ne_ref[r], 1 - p):
                cp.start()

    @pl.when(r < n_used)
    def _():
        wait_gathers(slot)
        start_gathers(jnp.minimum(r + 1, last), 1 - slot)
        x = xbuf[slot, :, 0:D_MODEL]
        side = xbuf[slot, :, D_MODEL:XY_COLS].astype(F32)
        lane = lax.broadcasted_iota(jnp.int32, side.shape, 1)

        def lanes_sum(a, b):
            return jnp.sum(jnp.where((lane == a) | (lane == b), side, 0.0), axis=-1, keepdims=True)

        is_slot1 = lanes_sum(E1_LANE, E1_LANE) == te_ref[r].astype(F32)
        wrow = jnp.where(is_slot1, lanes_sum(*W1_LANES), lanes_sum(*W2_LANES))
        a = jnp.dot(x, wg_s[...], preferred_element_type=F32)
        u = jnp.dot(x, wu_s[...], preferred_element_type=F32)
        hid = (a * jax.nn.sigmoid(a) * u * wrow).astype(BF16)
        ybuf[slot] = jnp.dot(hid, wd_s[...], preferred_element_type=F32).astype(BF16)
        start_scatters(r, slot)

    @pl.when(r == n_used - 1)
    def _():
        wait_gathers(1 - slot)
        wait_scatters(slot)

        @pl.when(r >= 1)
        def _():
            wait_scatters(1 - slot)


def _experts(tile_expert, n_used, unit_rows, next_expert, expert_parity, xy, wg, wu, wd):
    grid_spec = pltpu.PrefetchScalarGridSpec(
        num_scalar_prefetch=5,
        grid=(N_EXP_TILES,),
        in_specs=[pl.BlockSpec(memory_space=pl.ANY),
                  pl.BlockSpec(memory_space=pl.ANY),
                  pl.BlockSpec(memory_space=pl.ANY),
                  pl.BlockSpec(memory_space=pl.ANY)],
        out_specs=pl.BlockSpec(memory_space=pl.ANY),
        scratch_shapes=[pltpu.VMEM((2, TM_EXP, XY_COLS), BF16),
                        pltpu.VMEM((2, TM_EXP, D_MODEL), BF16),
                        pltpu.VMEM((D_MODEL, D_FF_EXPERT), BF16),
                        pltpu.VMEM((D_MODEL, D_FF_EXPERT), BF16),
                        pltpu.VMEM((D_FF_EXPERT, D_MODEL), BF16),
                        pltpu.VMEM((2, D_MODEL, D_FF_EXPERT), F32),
                        pltpu.VMEM((2, D_MODEL, D_FF_EXPERT), F32),
                        pltpu.VMEM((2, D_FF_EXPERT, D_MODEL), F32),
                        pltpu.SemaphoreType.DMA((2,)),
                        pltpu.SemaphoreType.DMA((2,)),
                        pltpu.SemaphoreType.DMA((2,))])
    return pl.pallas_call(
        _experts_kernel,
        out_shape=jax.ShapeDtypeStruct(xy.shape, xy.dtype),
        grid_spec=grid_spec,
        input_output_aliases={5: 0},
        compiler_params=_cparams(1),
        name="experts",
    )(tile_expert, n_used, unit_rows, next_expert, expert_parity, xy, wg, wu, wd)


def _combine_kernel(x1_ref, rc_ref, gf_ref, gfin_ref, y_ref, o_ref):
    lp1 = rc_ref[:, 0:1]
    lp2 = rc_ref[:, 1:2]
    scol = lax.broadcasted_iota(jnp.int32, (TM_ROW, XY_ROWS), 1).astype(F32)
    pick = jnp.where((scol == lp1) | (scol == lp2), 1.0, 0.0).astype(BF16)
    y = jnp.dot(pick, y_ref[...], preferred_element_type=F32)
    xf = x1_ref[...] + gf_ref[...] * y
    rs = lax.rsqrt(jnp.mean(xf * xf, axis=-1, keepdims=True) + EPS)
    o_ref[...] = xf * rs * gfin_ref[...]


def _combine(x1, rcol, gate_f, g_final, xy):
    tm = TM_ROW
    tpb = SEQ // tm
    row = lambda i: (i, 0)
    return pl.pallas_call(
        _combine_kernel,
        out_shape=jax.ShapeDtypeStruct((N_TOK, D_MODEL), F32),
        grid=(N_TOK // tm,),
        in_specs=[pl.BlockSpec((tm, D_MODEL), row),
                  pl.BlockSpec((tm, LANES), row),
                  pl.BlockSpec((None, 1, D_MODEL), lambda i: (i // tpb, 0, 0)),
                  pl.BlockSpec((1, D_MODEL), lambda i: (0, 0)),
                  pl.BlockSpec((XY_ROWS, D_MODEL), row)],
        out_specs=pl.BlockSpec((tm, D_MODEL), row),
        compiler_params=_cparams(1),
        name="combine",
    )(x1, rcol, gate_f, g_final, xy)


def _t5_bucket_np():
    qi = np.arange(BLOCK)[:, None]
    kj = np.arange(2 * BLOCK)[None, :]
    dist = qi - kj + BLOCK
    n = np.maximum(dist, 0)
    max_exact = NUM_BUCKETS // 2
    nf = np.maximum(n, 1).astype(np.float32)
    large = max_exact + (np.log(nf / np.float32(max_exact)) / np.float32(math.log(MAX_DISTANCE / max_exact))
                         * np.float32(NUM_BUCKETS - max_exact)).astype(np.int32)
    large = np.minimum(large, NUM_BUCKETS - 1)
    bucket = np.where(n < max_exact, n, large)
    band = (dist >= 0) & (dist < WINDOW)
    return bucket.astype(np.int32), band


def kernel(x, c, w_ada, b_ada, g_norm_mix, g_norm_ffn, w_in, sinks, b_forget, w_proj_swa, w_proj_fox,
           w_out, rel_bias_table, w_router_group, b_router_group, w_router_expert, b_router_expert,
           w_gate_exp, w_up_exp, w_down_exp, g_final):
    l = 0
    x2 = x.reshape(N_TOK, D_MODEL)

    c16 = jnp.concatenate([c, jnp.zeros_like(c)], axis=0)
    mod = _ada(c16, w_ada[l], b_ada[l][None, :])[:BATCH]
    shift_m, scale_m, gate_m, shift_f, scale_f, gate_f = [
        m.reshape(BATCH, 1, D_MODEL) for m in jnp.split(mod, 6, axis=-1)]

    w = w_in[l]
    o_ka, o_va, o_qb = Q_A, Q_A + KV_A, Q_A + 2 * KV_A
    o_kb, o_vb, o_f = o_qb + W_B, o_qb + 2 * W_B, o_qb + 3 * W_B
    o_g = o_f + N_HEADS_FOX

    def dup(cols):
        heads = [cols[:, h * HEAD_DIM:(h + 1) * HEAD_DIM] for h in range(N_KV_HEADS_SWA)]
        return jnp.concatenate([hd for hd in heads for _ in range(2)], axis=1)

    head_order = jnp.argsort(b_forget[l])

    def reorder_heads(cols):
        return jnp.take(cols.reshape(D_MODEL, N_HEADS_FOX, HEAD_DIM), head_order, axis=1).reshape(D_MODEL, W_B)

    w_fox = jnp.concatenate([reorder_heads(w[:, o_qb:o_kb]), reorder_heads(w[:, o_kb:o_vb]),
                             reorder_heads(w[:, o_vb:o_f])], axis=1)
    b_fox = jnp.take(b_forget[l], head_order)
    w_proj_b = jnp.take(w_proj_fox[l].reshape(N_HEADS_FOX, HEAD_DIM, D_MODEL), head_order, axis=0).reshape(W_B, D_MODEL)
    carrier = DECAY_LANES * N_HEADS_FOX
    w_f = jnp.pad(jnp.repeat(jnp.take(w[:, o_f:o_g], head_order, axis=1), DECAY_LANES, axis=1),
                  ((0, 0), (0, LANES - carrier)))
    w_main = jnp.concatenate([w[:, :Q_A], dup(w[:, o_ka:o_va]), w[:, o_va:o_qb], w_f, w_fox], axis=1).astype(BF16)
    w_g = w[:, o_g:].astype(BF16)
    qa, kdup, va, qb, kb, vb, f_pad, gates, nrm = _inproj(
        x2, scale_m, shift_m, g_norm_mix[l][None, :], w_main, w_g)

    b_pad = jnp.pad(jnp.repeat(b_fox, DECAY_LANES), (0, LANES - carrier))[None, :]
    lanes = np.arange(LANES)
    jmod = jnp.asarray(np.where(lanes < DECAY_LANES * N_HEADS_FOX, lanes % DECAY_LANES, 7)[None, :].astype(np.int32))
    dq, dk, fb = _cum(f_pad, b_pad, jmod)

    bucket, band = _t5_bucket_np()
    onehot = jnp.asarray(bucket[None] == np.arange(NUM_BUCKETS)[:, None, None], dtype=F32)
    bias = jnp.einsum("bh,bqk->hqk", rel_bias_table.astype(F32), onehot, precision=HIGHEST)
    bias = jnp.where(band[None], bias * LOG2E, NEG_INF)
    first = np.arange(2 * BLOCK)[None, None, :] < BLOCK
    bias = jnp.stack([jnp.where(first, NEG_INF, bias), bias]).reshape(2, N_KV_HEADS_SWA, -1, 2 * BLOCK)
    o_a = _swa(sinks[l].astype(F32) * LOG2E, qa, kdup, va, bias)

    o_b = _fox(_fox_first_tiles(nrm, fb), qb, kb, vb, dq, dk)

    w_r = jnp.concatenate([w_router_group[l]] + [w_router_expert[l][g] for g in range(N_GROUPS)], axis=1)
    w_r = jnp.pad(w_r, ((0, 0), (0, LANES - w_r.shape[1])))
    wr_hi = w_r.astype(BF16)
    wr_lo = (w_r - wr_hi.astype(F32)).astype(BF16)
    wr2 = jnp.concatenate([wr_hi, wr_lo], axis=1)
    b_r = jnp.concatenate([b_router_group[l], b_router_expert[l].reshape(-1)])
    b_r = jnp.pad(b_r, (0, LANES - b_r.shape[0]))[None, :]
    x1, xy, rcol, cu = _post(x2, o_a, o_b, gates, gate_m, scale_f, shift_f, g_norm_ffn[l][None, :],
                             w_proj_swa[l].astype(BF16), w_proj_b.astype(BF16), w_out[l].astype(BF16),
                             wr2, b_r)

    i32 = jnp.int32
    n_tok_tiles = N_TOK // TM_POST
    cu = cu.reshape(n_tok_tiles, 8, LANES)[:, 0, :N_EXPERTS].astype(i32)
    loc_u = jnp.cumsum(cu, axis=1) - cu
    cend = jnp.cumsum(cu, axis=0)
    cstart = cend - cu
    tot_u = cend[-1]
    tiles_e = (tot_u + UNITS_PER_TILE - 1) // UNITS_PER_TILE
    tile_end = jnp.cumsum(tiles_e)
    tile_start = tile_end - tiles_e
    r = jnp.arange(N_EXP_TILES, dtype=i32)
    tile_expert = jnp.minimum(jnp.sum((tile_end[None, :] <= r[:, None]).astype(i32), axis=1), N_EXPERTS - 1)
    sel_e = tile_expert[:, None] == jnp.arange(N_EXPERTS, dtype=i32)[None, :]
    tw = r - jnp.sum(jnp.where(sel_e, tile_start[None, :], 0), axis=1)
    tot_r = jnp.sum(jnp.where(sel_e, tot_u[None, :], 0), axis=1)
    n_used = tile_end[-1:].astype(i32)
    q = tw[:, None] * UNITS_PER_TILE + jnp.arange(UNITS_PER_TILE, dtype=i32)[None, :]

    def of_expert(tab):
        return jnp.sum(jnp.where(sel_e[:, None, :], tab[None, :, :], 0), axis=2)

    cend_r, cstart_r, loc_r = of_expert(cend), of_expert(cstart), of_expert(loc_u)
    src_tile = jnp.minimum(jnp.sum((cend_r[:, None, :] <= q[:, :, None]).astype(i32), axis=2), n_tok_tiles - 1)
    sel_t = src_tile[:, :, None] == jnp.arange(n_tok_tiles, dtype=i32)[None, None, :]
    k = (q - jnp.sum(jnp.where(sel_t, cstart_r[:, None, :], 0), axis=2)
         + jnp.sum(jnp.where(sel_t, loc_r[:, None, :], 0), axis=2))
    real_rows = src_tile * XY_ROWS + k * UNIT
    pad_rows = PAD_BASE_ROW + (tile_expert[:, None] * PAD_UNITS_PER_EXPERT + (q - tot_r[:, None])) * UNIT
    idle_row = PAD_BASE_ROW + N_EXPERTS * PAD_UNITS_PER_EXPERT * UNIT
    unit_rows = jnp.where(q < tot_r[:, None], real_rows, pad_rows)
    unit_rows = jnp.where((r < n_used)[:, None], unit_rows, idle_row).reshape(-1).astype(i32)

    eid = jnp.arange(N_EXPERTS, dtype=i32)
    used = tiles_e > 0
    later_used = (eid[None, :] > eid[:, None]) & used[None, :]
    next_e = jnp.min(jnp.where(later_used, eid[None, :], N_EXPERTS), axis=1)
    next_e = jnp.where(next_e == N_EXPERTS, -1, next_e)
    parity_e = (jnp.cumsum(used.astype(i32)) - used.astype(i32)) % 2
    next_expert = jnp.sum(jnp.where(sel_e, next_e[None, :], 0), axis=1).astype(i32)
    expert_parity = jnp.sum(jnp.where(sel_e, parity_e[None, :], 0), axis=1).astype(i32)

    xy = _experts(tile_expert.astype(i32), n_used, unit_rows, next_expert, expert_parity, xy,
                  w_gate_exp[l].reshape(N_EXPERTS, D_MODEL, D_FF_EXPERT),
                  w_up_exp[l].reshape(N_EXPERTS, D_MODEL, D_FF_EXPERT),
                  w_down_exp[l].reshape(N_EXPERTS, D_FF_EXPERT, D_MODEL))
    out = _combine(x1, rcol, gate_f, g_final[None, :], xy)
    return out.reshape(BATCH, SEQ, D_MODEL)
```

```python
import math

import numpy as np
import jax
import jax.numpy as jnp
from jax import lax
from jax.experimental import pallas as pl
from jax.experimental.pallas import tpu as pltpu

F32 = jnp.float32
BF16 = jnp.bfloat16
HIGHEST = lax.Precision.HIGHEST

D_MODEL = 1024
BATCH = 8
SEQ = 4096
N_TOK = BATCH * SEQ
N_HEADS_SWA = 8
N_KV_HEADS_SWA = 2
N_HEADS_FOX = 8
HEAD_DIM = 64
WINDOW = 128
BLOCK = 128
NUM_BUCKETS = 32
MAX_DISTANCE = 128
N_GROUPS = 4
EXPERTS_PER_GROUP = 8
N_EXPERTS = N_GROUPS * EXPERTS_PER_GROUP
D_FF_EXPERT = 256
EPS = 1e-6
NEG_INF = -1e30

Q_A = N_HEADS_SWA * HEAD_DIM
KV_A = N_KV_HEADS_SWA * HEAD_DIM
W_B = N_HEADS_FOX * HEAD_DIM
LANES = 128
QK_SCALE = HEAD_DIM ** -0.5

TM_IN = 512
TM_POST = 512
TQ_FOX = 512
TK_FOX = TQ_FOX
SWA_BLOCKS = 4
TM_EXP = 512
TM_ROW = 512
UNIT = 16
XY_UNITS = 2 * TM_POST // UNIT + N_EXPERTS
XY_ROWS = XY_UNITS * UNIT
XY_COLS = D_MODEL + LANES
UNITS_PER_TILE = TM_EXP // UNIT
N_TOK_TILES = N_TOK // TM_POST
N_EXP_TILES = N_TOK_TILES * XY_UNITS // UNITS_PER_TILE + N_EXPERTS
PAD_UNITS_PER_EXPERT = UNITS_PER_TILE - 1
PAD_BLOCKS = -(-(N_EXPERTS * PAD_UNITS_PER_EXPERT * UNIT) // XY_ROWS)
PAD_BASE_ROW = N_TOK_TILES * XY_ROWS
W1_LANES, W2_LANES, E1_LANE, E2_LANE = (4, 6), (5, 7), 8, 9
VMEM_LIMIT = 56 * 1024 * 1024

DECAY_LANES = 6
LOG2E = math.log2(math.e)
Q_SCALE_LOG2 = QK_SCALE * LOG2E
PRUNE_MARGIN = 160.0


def _cparams(n_axes):
    return pltpu.CompilerParams(dimension_semantics=("arbitrary",) * n_axes,
                                vmem_limit_bytes=VMEM_LIMIT)


def _ada_kernel(c_ref, w_ref, b_ref, o_ref):
    c = c_ref[...]
    ca = c * jax.nn.sigmoid(c)
    o_ref[...] = jnp.dot(ca.astype(BF16), w_ref[...].astype(BF16),
                         preferred_element_type=F32) + b_ref[...]


def _ada(c16, w_ada, b_ada):
    n_out = w_ada.shape[1]
    blk = 1024
    return pl.pallas_call(
        _ada_kernel,
        out_shape=jax.ShapeDtypeStruct((16, n_out), F32),
        grid=(n_out // blk,),
        in_specs=[pl.BlockSpec((16, D_MODEL), lambda j: (0, 0)),
                  pl.BlockSpec((D_MODEL, blk), lambda j: (0, j)),
                  pl.BlockSpec((1, blk), lambda j: (0, j))],
        out_specs=pl.BlockSpec((16, blk), lambda j: (0, j)),
        compiler_params=_cparams(1),
        name="ada",
    )(c16, w_ada, b_ada)


def _inproj_kernel(x_ref, sc_ref, sh_ref, g_ref, wm_ref, wg_ref, ind_ref,
                   qa_ref, kd_ref, va_ref, qb_ref, kb_ref, vb_ref, f_ref, gt_ref, nrm_ref):
    x = x_ref[...]
    rs = lax.rsqrt(jnp.mean(x * x, axis=-1, keepdims=True) + EPS)
    a = g_ref[...] * (1.0 + sc_ref[...])
    h = (x * rs * a + sh_ref[...]).astype(BF16)

    def mm(w):
        return jnp.dot(h, w, preferred_element_type=F32)

    qa_ref[...] = (mm(wm_ref[:, 0:512]) * Q_SCALE_LOG2).astype(BF16)
    kd_ref[...] = mm(wm_ref[:, 512:768]).astype(BF16)
    vf = mm(wm_ref[:, 768:1024])
    f_ref[...] = vf[:, LANES:2 * LANES]
    v = vf[:, 0:LANES]
    vr = pltpu.roll(v, HEAD_DIM, 1)
    lo = lax.broadcasted_iota(jnp.int32, v.shape, 1) < HEAD_DIM
    va_ref[:, 0:LANES] = jnp.where(lo, v, vr).astype(BF16)
    va_ref[:, LANES:2 * LANES] = jnp.where(lo, vr, v).astype(BF16)
    qb = (mm(wm_ref[:, 1024:1536]) * Q_SCALE_LOG2).astype(BF16)
    kb = mm(wm_ref[:, 1536:2048]).astype(BF16)
    qb_ref[...] = qb
    kb_ref[...] = kb
    vb_ref[...] = mm(wm_ref[:, 2048:2560]).astype(BF16)
    sq = jnp.concatenate([qb, kb], axis=1).astype(F32)
    seg = jnp.dot((sq * sq).astype(BF16), ind_ref[...], preferred_element_type=F32)
    nrm_ref[...] = jnp.broadcast_to(jnp.max(seg, axis=0, keepdims=True), nrm_ref.shape)
    gt_ref[...] = mm(wg_ref[...]).astype(BF16)


def _inproj(x2, scale_m, shift_m, g_mix, w_main, w_g):
    tm = TM_IN
    tpb = SEQ // tm
    row = lambda i: (i, 0)
    per_b = lambda i: (i // tpb, 0, 0)
    const = lambda i: (0, 0)
    outs = [(Q_A, BF16), (2 * KV_A, BF16), (2 * KV_A, BF16), (W_B, BF16), (W_B, BF16), (W_B, BF16),
            (LANES, F32), (2 * D_MODEL, BF16)]
    ind_np = np.zeros((2 * W_B, LANES), np.float32)
    ind_np[np.arange(2 * W_B), np.arange(2 * W_B) // HEAD_DIM] = 1.0
    ind = jnp.asarray(ind_np, dtype=BF16)
    n_steps = N_TOK // tm
    return pl.pallas_call(
        _inproj_kernel,
        out_shape=[jax.ShapeDtypeStruct((N_TOK, w), dt) for w, dt in outs]
        + [jax.ShapeDtypeStruct((n_steps * 8, LANES), F32)],
        grid=(n_steps,),
        in_specs=[pl.BlockSpec((tm, D_MODEL), row),
                  pl.BlockSpec((None, 1, D_MODEL), per_b),
                  pl.BlockSpec((None, 1, D_MODEL), per_b),
                  pl.BlockSpec((1, D_MODEL), const),
                  pl.BlockSpec(w_main.shape, const),
                  pl.BlockSpec(w_g.shape, const),
                  pl.BlockSpec(ind.shape, const)],
        out_specs=[pl.BlockSpec((tm, w), row) for w, _ in outs] + [pl.BlockSpec((8, LANES), row)],
        compiler_params=_cparams(1),
        name="inproj",
    )(x2, scale_m, shift_m, g_mix, w_main, w_g, ind)


def _log_sigmoid(x):
    return jnp.minimum(x, 0.0) - jnp.log1p(jnp.exp(-jnp.abs(x)))


def _cum_kernel(f_ref, b_ref, jm_ref, qa_ref, ka_ref, fb_ref):
    cum = _log_sigmoid(f_ref[...] + b_ref[...]) * LOG2E
    row = lax.broadcasted_iota(jnp.int32, cum.shape, 0)
    k = 1
    while k < SEQ:
        if k < 8:
            shifted = jnp.where(row >= k, pltpu.roll(cum, k, 0), 0.0)
        else:
            shifted = jnp.concatenate([jnp.zeros((k, LANES), F32), cum[:SEQ - k]], axis=0)
        cum = cum + shifted
        k *= 2
    jm = jm_ref[...]
    for blk in range(SEQ // LANES):
        rows = slice(blk * LANES, (blk + 1) * LANES)
        cb = cum[rows]
        carry = cb[LANES - 1:LANES]
        hi = cb.astype(BF16).astype(F32)
        r1 = cb - hi
        mid = r1.astype(BF16).astype(F32)
        lo = (r1 - mid).astype(BF16).astype(F32)
        one = jnp.ones_like(cb)
        zero = jnp.zeros_like(cb)
        qa = jnp.where(jm == 0, hi, jnp.where(jm == 1, mid, jnp.where(jm == 2, lo,
                       jnp.where(jm < DECAY_LANES, one, zero))))
        ka = jnp.where(jm == 3, -hi, jnp.where(jm == 4, -mid, jnp.where(jm == 5, -lo,
                       jnp.where(jm < 3, one, zero))))
        qa_ref[rows, :] = qa.astype(BF16)
        ka_ref[rows, :] = ka.astype(BF16)
        blocks_per_tile = TQ_FOX // LANES
        tile = blk // blocks_per_tile
        if blk % blocks_per_tile == 0:
            fb_ref[2 * tile:2 * tile + 1, :] = cb[0:1]
        if blk % blocks_per_tile == blocks_per_tile - 1:
            fb_ref[2 * tile + 1:2 * tile + 2, :] = carry


def _cum(f_pad, b_pad, jmod):
    n_tiles = SEQ // TQ_FOX
    return pl.pallas_call(
        _cum_kernel,
        out_shape=[jax.ShapeDtypeStruct((BATCH, SEQ, LANES), BF16)] * 2
        + [jax.ShapeDtypeStruct((BATCH, 2 * n_tiles, LANES), F32)],
        grid=(BATCH,),
        in_specs=[pl.BlockSpec((SEQ, LANES), lambda b: (b, 0)),
                  pl.BlockSpec((1, LANES), lambda b: (0, 0)),
                  pl.BlockSpec((1, LANES), lambda b: (0, 0))],
        out_specs=[pl.BlockSpec((None, SEQ, LANES), lambda b: (b, 0, 0))] * 2
        + [pl.BlockSpec((None, 2 * n_tiles, LANES), lambda b: (b, 0, 0))],
        compiler_params=_cparams(1),
        name="cum",
    )(f_pad, b_pad, jmod)


def _swa_block(sink_cols, q, kk, vv, bias_ref, lo):
    tiles = []
    for g in range(N_KV_HEADS_SWA):
        parts = []
        for t in range(2):
            qt = q[:, (2 * g + t) * LANES:(2 * g + t + 1) * LANES]
            zero = jnp.zeros_like(qt)
            parts.append(jnp.where(lo, qt, zero))
            parts.append(jnp.where(lo, zero, qt))
        q4 = jnp.concatenate(parts, axis=0)
        s = lax.dot_general(q4, kk[:, g * LANES:(g + 1) * LANES], (((1,), (1,)), ((), ())),
                            preferred_element_type=F32)
        s = s + bias_ref[g]
        sink = sink_cols[g]
        m = jnp.maximum(jnp.max(s, axis=-1, keepdims=True), sink)
        p = jnp.exp2(s - m)
        den = jnp.sum(p, axis=-1, keepdims=True) + jnp.exp2(sink - m)
        o = jnp.dot(p.astype(BF16), vv[:, g * LANES:(g + 1) * LANES],
                    preferred_element_type=F32) / den
        tiles.append(jnp.where(lo, o[0:BLOCK], o[BLOCK:2 * BLOCK]))
        tiles.append(jnp.where(lo, o[2 * BLOCK:3 * BLOCK], o[3 * BLOCK:4 * BLOCK]))
    return tiles


def _swa_kernel(sink_ref, q_ref, kc_ref, kp_ref, vc_ref, vp_ref, bias_first_ref, bias_ref, o_ref):
    lane = lax.broadcasted_iota(jnp.int32, (BLOCK, LANES), 1)
    lo = lane < HEAD_DIM
    grp = N_HEADS_SWA // N_KV_HEADS_SWA
    row = lax.broadcasted_iota(jnp.int32, (grp * BLOCK, 1), 0)
    sink_cols = []
    for g in range(N_KV_HEADS_SWA):
        col = jnp.full((grp * BLOCK, 1), sink_ref[g * grp + grp - 1], F32)
        for hh in range(grp - 2, -1, -1):
            col = jnp.where(row < (hh + 1) * BLOCK, sink_ref[g * grp + hh], col)
        sink_cols.append(col)
    for blk in range(SWA_BLOCKS):
        rows = slice(blk * BLOCK, (blk + 1) * BLOCK)
        if blk == 0:
            kk = jnp.concatenate([kp_ref[...], kc_ref[rows, :]], axis=0)
            vv = jnp.concatenate([vp_ref[...], vc_ref[rows, :]], axis=0)
            bias = bias_first_ref
        else:
            prev_rows = slice((blk - 1) * BLOCK, (blk + 1) * BLOCK)
            kk = kc_ref[prev_rows, :]
            vv = vc_ref[prev_rows, :]
            bias = bias_ref
        tiles = _swa_block(sink_cols, q_ref[rows, :], kk, vv, bias, lo)
        for c, tile in enumerate(tiles):
            o_ref[rows, c * LANES:(c + 1) * LANES] = tile.astype(BF16)


def _swa(sinks, qa, kdup, va, bias):
    nb = SEQ // BLOCK
    ns = nb // SWA_BLOCKS
    cur = lambda b, i, s: (b * ns + i, 0)
    prev = lambda b, i, s: (b * nb + jnp.maximum(SWA_BLOCKS * i - 1, 0), 0)
    grid_spec = pltpu.PrefetchScalarGridSpec(
        num_scalar_prefetch=1,
        grid=(BATCH, ns),
        in_specs=[pl.BlockSpec((SWA_BLOCKS * BLOCK, Q_A), cur),
                  pl.BlockSpec((SWA_BLOCKS * BLOCK, 2 * KV_A), cur),
                  pl.BlockSpec((BLOCK, 2 * KV_A), prev),
                  pl.BlockSpec((SWA_BLOCKS * BLOCK, 2 * KV_A), cur),
                  pl.BlockSpec((BLOCK, 2 * KV_A), prev),
                  pl.BlockSpec((None,) + bias.shape[1:], lambda b, i, s: (jnp.minimum(i, 1), 0, 0, 0)),
                  pl.BlockSpec((None,) + bias.shape[1:], lambda b, i, s: (1, 0, 0, 0))],
        out_specs=pl.BlockSpec((SWA_BLOCKS * BLOCK, Q_A), cur))
    return pl.pallas_call(
        _swa_kernel,
        out_shape=jax.ShapeDtypeStruct((N_TOK, Q_A), BF16),
        grid_spec=grid_spec,
        compiler_params=_cparams(2),
        name="swa",
    )(sinks, qa, kdup, kdup, va, va, bias, bias)


def _fox_kernel(js_ref, q_ref, k_ref, v_ref, qa_ref, ka_ref, o_ref,
                kaug, vaug, q2, m_sc, acc_sc, s_a, s_b):
    tq, tk = TQ_FOX, TK_FOX
    b = pl.program_id(0)
    t = pl.program_id(1)
    i = pl.program_id(2)
    j_start = js_ref[(b * pl.num_programs(1) + t) * pl.num_programs(2) + i]

    @pl.when(i == 0)
    def _():
        kaug[:, 0:LANES] = k_ref[...]
        kaug[:, LANES:2 * LANES] = ka_ref[...]
        vaug[:, 0:LANES] = v_ref[...]
        vaug[:, LANES:2 * LANES] = jnp.ones((SEQ, LANES), BF16)

    lane = lax.broadcasted_iota(jnp.int32, (tq, LANES), 1)
    lo = lane < HEAD_DIM
    base = 2 * DECAY_LANES * t
    own = [(lane >= base + h * DECAY_LANES) & (lane < base + (h + 1) * DECAY_LANES) for h in range(2)]
    q = q_ref[...]
    qa = qa_ref[...]
    zero = jnp.zeros_like(q)
    q2[0, :, 0:LANES] = jnp.where(lo, q, zero)
    q2[1, :, 0:LANES] = jnp.where(lo, zero, q)
    for h in range(2):
        q2[h, :, LANES:2 * LANES] = jnp.where(own[h], qa, zero)
    m_sc[...] = jnp.full(m_sc.shape, NEG_INF, F32)
    acc_sc[...] = jnp.zeros(acc_sc.shape, F32)

    def scores(h, ks):
        return lax.dot_general(q2[h], kaug[pl.ds(ks, tk), :], (((1,), (1,)), ((), ())),
                               preferred_element_type=F32)

    def consume(h, s, ks, mask):
        if mask is not None:
            s = jnp.where(mask, s, NEG_INF)
        m_prev = m_sc[h]
        m_new = jnp.maximum(m_prev, jnp.max(s, axis=-1, keepdims=True))
        alpha = jnp.exp2(m_prev - m_new)
        p = jnp.exp2(s - jnp.concatenate([m_new] * (tk // LANES), axis=1))
        pv = jnp.dot(p.astype(BF16), vaug[pl.ds(ks, tk), :], preferred_element_type=F32)
        acc_sc[h] = jnp.concatenate([alpha, alpha], axis=1) * acc_sc[h] + pv
        m_sc[h] = m_new

    def key_start(j):
        return pl.multiple_of(j * tk, tk)

    def scores_into(buf, j):
        for h in range(2):
            buf[h] = scores(h, key_start(j))

    def consume_from(buf, j, mask):
        for h in range(2):
            consume(h, buf[h], key_start(j), mask)

    rr = lax.broadcasted_iota(jnp.int32, (tq, tk), 0)
    cc = lax.broadcasted_iota(jnp.int32, (tq, tk), 1)
    causal = cc <= rr
    n_full = i - j_start

    scores_into(s_a, j_start)

    def pair(p, carry):
        j = j_start + 2 * p
        scores_into(s_b, j + 1)
        consume_from(s_a, j, None)
        scores_into(s_a, j + 2)
        consume_from(s_b, j + 1, None)
        return carry

    lax.fori_loop(0, n_full // 2, pair, 0)
    odd = lax.rem(n_full, 2) == 1

    @pl.when(odd)
    def _():
        scores_into(s_b, i)
        consume_from(s_a, i - 1, None)
        consume_from(s_b, i, causal)

    @pl.when(jnp.logical_not(odd))
    def _():
        consume_from(s_a, i, causal)

    outs = [acc_sc[h, :, 0:LANES] / acc_sc[h, :, LANES:2 * LANES] for h in range(2)]
    o_ref[...] = jnp.where(lo, outs[0], outs[1]).astype(BF16)


def _fox(j_start, qb, kb, vb, qa, ka):
    tq = TQ_FOX
    nq = SEQ // tq
    n_pairs = N_HEADS_FOX // 2
    qmap = lambda b, t, i, js: (b * nq + i, t)
    kmap = lambda b, t, i, js: (b, t)
    grid_spec = pltpu.PrefetchScalarGridSpec(
        num_scalar_prefetch=1,
        grid=(BATCH, n_pairs, nq),
        in_specs=[pl.BlockSpec((tq, LANES), qmap),
                  pl.BlockSpec((SEQ, LANES), kmap),
                  pl.BlockSpec((SEQ, LANES), kmap),
                  pl.BlockSpec((None, tq, LANES), lambda b, t, i, js: (b, i, 0)),
                  pl.BlockSpec((None, SEQ, LANES), lambda b, t, i, js: (b, 0, 0))],
        out_specs=pl.BlockSpec((tq, LANES), qmap),
        scratch_shapes=[pltpu.VMEM((SEQ, 2 * LANES), BF16),
                        pltpu.VMEM((SEQ, 2 * LANES), BF16),
                        pltpu.VMEM((2, tq, 2 * LANES), BF16),
                        pltpu.VMEM((2, tq, LANES), F32),
                        pltpu.VMEM((2, tq, 2 * LANES), F32),
                        pltpu.VMEM((2, tq, TK_FOX), F32),
                        pltpu.VMEM((2, tq, TK_FOX), F32)])
    return pl.pallas_call(
        _fox_kernel,
        out_shape=jax.ShapeDtypeStruct((N_TOK, W_B), BF16),
        grid_spec=grid_spec,
        compiler_params=_cparams(3),
        name="fox",
    )(j_start, qb, kb, vb, qa, ka)


def _fox_first_tiles(nrm, fb):
    n_tiles = SEQ // TQ_FOX
    nr = nrm.reshape(BATCH, SEQ // TM_IN, 8, LANES)[:, :, 0, :] * 1.02
    nr = jnp.repeat(nr, TM_IN // TQ_FOX, axis=1)
    qn = jnp.sqrt(nr[..., 0:N_HEADS_FOX])
    kn = jnp.sqrt(nr[..., N_HEADS_FOX:2 * N_HEADS_FOX])
    f_first = fb[:, 0::2, 0:DECAY_LANES * N_HEADS_FOX:DECAY_LANES]
    f_last = fb[:, 1::2, 0:DECAY_LANES * N_HEADS_FOX:DECAY_LANES]
    kn_prefix = lax.cummax(kn, axis=1)
    upper = qn[:, :, None, :] * kn_prefix[:, None, :, :] + f_first[:, :, None, :] - f_last[:, None, :, :]
    row_max_low = -(qn * kn)[:, :, None, :]
    ii = jnp.arange(n_tiles)[None, :, None, None]
    jj = jnp.arange(n_tiles)[None, None, :, None]
    skip = (upper < row_max_low - PRUNE_MARGIN) & (jj < ii)
    skip = jnp.all(skip.reshape(BATCH, n_tiles, n_tiles, N_HEADS_FOX // 2, 2), axis=-1)
    first = jnp.sum(jnp.cumprod(skip.astype(jnp.int32), axis=2), axis=2)
    return jnp.transpose(first, (0, 2, 1)).reshape(-1).astype(jnp.int32)


def _post_kernel(x_ref, oa_ref, ob_ref, gt_ref, gm_ref, sc_ref, sh_ref, g_ref,
                 wa_ref, wb_ref, wo_ref, wr2_ref, br_ref,
                 x1_ref, xy_ref, rc_ref, cu_ref, hh_prev, lg_prev):
    step = pl.program_id(0)

    @pl.when(step == 0)
    def _():
        hh_prev[...] = jnp.zeros(hh_prev.shape, BF16)
        lg_prev[...] = jnp.zeros(lg_prev.shape, F32)

    @pl.when(step <= N_TOK_TILES)
    def _():
        hh_p = hh_prev[...]
        lg_p = lg_prev[...]
        hh, logits = _post_mix(x_ref, oa_ref, ob_ref, gt_ref, gm_ref, sc_ref, sh_ref, g_ref,
                               wa_ref, wb_ref, wo_ref, wr2_ref, br_ref, x1_ref)
        _post_route(hh_p, lg_p, xy_ref, rc_ref, cu_ref)
        hh_prev[...] = hh
        lg_prev[...] = logits

    @pl.when(step > N_TOK_TILES)
    def _():
        xy_ref[...] = jnp.zeros(xy_ref.shape, BF16)


def _post_mix(x_ref, oa_ref, ob_ref, gt_ref, gm_ref, sc_ref, sh_ref, g_ref,
              wa_ref, wb_ref, wo_ref, wr2_ref, br_ref, x1_ref):
    pa = jnp.dot(oa_ref[...], wa_ref[...], preferred_element_type=F32)
    pb = jnp.dot(ob_ref[...], wb_ref[...], preferred_element_type=F32)
    ga = jax.nn.sigmoid(gt_ref[:, 0:D_MODEL].astype(F32))
    gb = jax.nn.sigmoid(gt_ref[:, D_MODEL:2 * D_MODEL].astype(F32))
    merged = (ga * pa + gb * pb).astype(BF16)
    y = jnp.dot(merged, wo_ref[...], preferred_element_type=F32)
    x1 = x_ref[...] + gm_ref[...] * y
    x1_ref[...] = x1

    rs = lax.rsqrt(jnp.mean(x1 * x1, axis=-1, keepdims=True) + EPS)
    a = g_ref[...] * (1.0 + sc_ref[...])
    h2 = x1 * rs * a + sh_ref[...]

    hh = h2.astype(BF16)
    hl = (h2 - hh.astype(F32)).astype(BF16)
    hi_both = jnp.dot(hh, wr2_ref[...], preferred_element_type=F32)
    logits = (hi_both[:, 0:LANES] + hi_both[:, LANES:2 * LANES]
              + jnp.dot(hl, wr2_ref[:, 0:LANES], preferred_element_type=F32)
              + br_ref[...])
    return hh, logits


def _post_route(hh, logits, xy_ref, rc_ref, cu_ref):
    tm = TM_POST
    lane = lax.broadcasted_iota(jnp.int32, (tm, LANES), 1).astype(F32)
    big = float(LANES)
    gl = jnp.where(lane < N_GROUPS, logits, -jnp.inf)
    gmax = jnp.max(gl, axis=-1, keepdims=True)
    gi = jnp.min(jnp.where(gl == gmax, lane, big), axis=-1, keepdims=True)
    gsum = jnp.sum(jnp.exp(gl - gmax), axis=-1, keepdims=True)
    gp = 1.0 / gsum
    e_lo = N_GROUPS + EXPERTS_PER_GROUP * gi
    el = jnp.where((lane >= e_lo) & (lane < e_lo + EXPERTS_PER_GROUP), logits, -jnp.inf)
    v1 = jnp.max(el, axis=-1, keepdims=True)
    i1 = jnp.min(jnp.where(el == v1, lane, big), axis=-1, keepdims=True)
    el2 = jnp.where(lane == i1, -jnp.inf, el)
    v2 = jnp.max(el2, axis=-1, keepdims=True)
    i2 = jnp.min(jnp.where(el2 == v2, lane, big), axis=-1, keepdims=True)
    e21 = jnp.exp(v2 - v1)
    w1 = gp / (1.0 + e21)
    w2 = gp * e21 / (1.0 + e21)
    e1 = i1 - N_GROUPS
    e2 = i2 - N_GROUPS

    oh = jnp.where((lane == e1) | (lane == e2), 1.0, 0.0)
    cnt_u = jnp.floor((jnp.sum(oh, axis=0, keepdims=True) + (UNIT - 1)) * (1.0 / UNIT))
    r128 = lax.broadcasted_iota(jnp.int32, (LANES, LANES), 0)
    c128 = lax.broadcasted_iota(jnp.int32, (LANES, LANES), 1)
    before_lane = jnp.where(r128 < c128, 1.0, 0.0).astype(BF16)
    loc_u = jnp.dot(jnp.broadcast_to(cnt_u, (8, LANES)).astype(BF16), before_lane,
                    preferred_element_type=F32)
    trow = lax.broadcasted_iota(jnp.int32, (tm, LANES), 0)
    seen = oh
    k = 1
    while k < tm:
        if k < 8:
            shifted = jnp.where(trow >= k, pltpu.roll(seen, k, 0), 0.0)
        else:
            shifted = jnp.concatenate([jnp.zeros((k, LANES), F32), seen[:tm - k]], axis=0)
        seen = seen + shifted
        k *= 2
    pos_e = (seen - oh) + loc_u[0:1] * UNIT
    lp1 = jnp.sum(jnp.where(lane == e1, pos_e, 0.0), axis=-1, keepdims=True)
    lp2 = jnp.sum(jnp.where(lane == e2, pos_e, 0.0), axis=-1, keepdims=True)

    def to_row(col):
        return jnp.transpose(jnp.broadcast_to(col, (tm, LANES)))[0:1]

    srow = lax.broadcasted_iota(jnp.int32, (XY_ROWS, tm), 0).astype(F32)
    pm1 = jnp.where(srow == to_row(lp1), 1.0, 0.0).astype(BF16)
    pm2 = jnp.where(srow == to_row(lp2), 1.0, 0.0).astype(BF16)
    w1h = w1.astype(BF16).astype(F32)
    w2h = w2.astype(BF16).astype(F32)
    side = jnp.where(lane == W1_LANES[0], w1h, jnp.where(lane == W1_LANES[1], w1 - w1h,
           jnp.where(lane == W2_LANES[0], w2h, jnp.where(lane == W2_LANES[1], w2 - w2h,
           jnp.where(lane == E1_LANE, e1, jnp.where(lane == E2_LANE, e2, 0.0))))))
    tok = jnp.concatenate([hh, side.astype(BF16)], axis=1)
    xy_ref[...] = jnp.dot(pm1 + pm2, tok, preferred_element_type=F32).astype(BF16)

    cu_ref[...] = jnp.broadcast_to(cnt_u, cu_ref.shape)
    rc_ref[...] = jnp.where(lane == 0, lp1, jnp.where(lane == 1, lp2, 0.0))


def _post(x2, oa, ob, gates, gate_m, scale_f, shift_f, g_ffn, wa, wb, wo, wr2, b_r):
    tm = TM_POST
    tpb = SEQ // tm
    n_steps = N_TOK_TILES
    row = lambda i: (jnp.minimum(i, n_steps - 1), 0)
    per_b = lambda i: (jnp.minimum(i, n_steps - 1) // tpb, 0, 0)
    routed = lambda i: (jnp.clip(i - 1, 0, n_steps - 1), 0)
    const = lambda i: (0, 0)
    return pl.pallas_call(
        _post_kernel,
        out_shape=[jax.ShapeDtypeStruct((N_TOK, D_MODEL), F32),
                   jax.ShapeDtypeStruct(((n_steps + PAD_BLOCKS) * XY_ROWS, XY_COLS), BF16),
                   jax.ShapeDtypeStruct((N_TOK, LANES), F32),
                   jax.ShapeDtypeStruct((n_steps * 8, LANES), F32)],
        grid=(n_steps + 1 + PAD_BLOCKS,),
        in_specs=[pl.BlockSpec((tm, D_MODEL), row),
                  pl.BlockSpec((tm, Q_A), row),
                  pl.BlockSpec((tm, W_B), row),
                  pl.BlockSpec((tm, 2 * D_MODEL), row),
                  pl.BlockSpec((None, 1, D_MODEL), per_b),
                  pl.BlockSpec((None, 1, D_MODEL), per_b),
                  pl.BlockSpec((None, 1, D_MODEL), per_b),
                  pl.BlockSpec((1, D_MODEL), const),
                  pl.BlockSpec(wa.shape, const),
                  pl.BlockSpec(wb.shape, const),
                  pl.BlockSpec(wo.shape, const),
                  pl.BlockSpec(wr2.shape, const),
                  pl.BlockSpec((1, LANES), const)],
        out_specs=[pl.BlockSpec((tm, D_MODEL), row),
                   pl.BlockSpec((XY_ROWS, XY_COLS), lambda i: (jnp.maximum(i - 1, 0), 0)),
                   pl.BlockSpec((tm, LANES), routed),
                   pl.BlockSpec((8, LANES), routed)],
        scratch_shapes=[pltpu.VMEM((tm, D_MODEL), BF16),
                        pltpu.VMEM((tm, LANES), F32)],
        compiler_params=_cparams(1),
        name="post",
    )(x2, oa, ob, gates, gate_m, scale_f, shift_f, g_ffn, wa, wb, wo, wr2, b_r)


def _experts_kernel(te_ref, nu_ref, ur_ref, ne_ref, ep_ref, xy_in, wg_hbm, wu_hbm, wd_hbm, xy_out,
                    xbuf, ybuf, wg_s, wu_s, wd_s, wg_f, wu_f, wd_f, gsem, ssem, wsem):
    del xy_in
    r = pl.program_id(0)
    last = pl.num_programs(0) - 1
    n_used = nu_ref[0]
    slot = lax.rem(r, 2)

    def unit_row(step, s):
        return pl.multiple_of(ur_ref[step * UNITS_PER_TILE + s], UNIT)

    def start_gathers(step, sl):
        for s in range(UNITS_PER_TILE):
            pltpu.make_async_copy(xy_out.at[pl.ds(unit_row(step, s), UNIT), :],
                                  xbuf.at[sl, pl.ds(s * UNIT, UNIT), :], gsem.at[sl]).start()

    def wait_gathers(sl):
        pltpu.make_async_copy(xy_out.at[pl.ds(0, TM_EXP), :], xbuf.at[sl], gsem.at[sl]).wait()

    def start_scatters(step, sl):
        for s in range(UNITS_PER_TILE):
            pltpu.make_async_copy(ybuf.at[sl, pl.ds(s * UNIT, UNIT), :],
                                  xy_out.at[pl.ds(unit_row(step, s), UNIT), pl.ds(0, D_MODEL)],
                                  ssem.at[sl]).start()

    def wait_scatters(sl):
        pltpu.make_async_copy(ybuf.at[sl], xy_out.at[pl.ds(0, TM_EXP), pl.ds(0, D_MODEL)], ssem.at[sl]).wait()

    @pl.when(r == 0)
    def _():
        start_gathers(0, 0)

    @pl.when(jnp.logical_and(r < n_used, r >= 2))
    def _():
        wait_scatters(slot)

    def weight_copies(e, p):
        return [pltpu.make_async_copy(src.at[e], dst.at[p], wsem.at[p])
                for src, dst in ((wg_hbm, wg_f), (wu_hbm, wu_f), (wd_hbm, wd_f))]

    @pl.when(jnp.logical_and(r < n_used,
                             jnp.logical_or(r == 0, te_ref[r] != te_ref[jnp.maximum(r - 1, 0)])))
    def _():
        e = te_ref[r]
        p = ep_ref[r]

        @pl.when(r == 0)
        def _():
            for cp in weight_copies(e, p):
                cp.start()

        for cp in weight_copies(e, p):
            cp.wait()
        wg_s[...] = wg_f[p].astype(BF16)
        wu_s[...] = wu_f[p].astype(BF16)
        wd_s[...] = wd_f[p].astype(BF16)

        @pl.when(ne_ref[r] >= 0)
        def _():
            for cp in weight_copies(ne_ref[r], 1 - p):
                cp.start()

    @pl.when(r < n_used)
    def _():
        wait_gathers(slot)
        start_gathers(jnp.minimum(r + 1, last), 1 - slot)
        x = xbuf[slot, :, 0:D_MODEL]
        side = xbuf[slot, :, D_MODEL:XY_COLS].astype(F32)
        lane = lax.broadcasted_iota(jnp.int32, side.shape, 1)

        def lanes_sum(a, b):
            return jnp.sum(jnp.where((lane == a) | (lane == b), side, 0.0), axis=-1, keepdims=True)

        is_slot1 = lanes_sum(E1_LANE, E1_LANE) == te_ref[r].astype(F32)
        wrow = jnp.where(is_slot1, lanes_sum(*W1_LANES), lanes_sum(*W2_LANES))
        a = jnp.dot(x, wg_s[...], preferred_element_type=F32)
        u = jnp.dot(x, wu_s[...], preferred_element_type=F32)
        hid = (a * jax.nn.sigmoid(a) * u * wrow).astype(BF16)
        ybuf[slot] = jnp.dot(hid, wd_s[...], preferred_element_type=F32).astype(BF16)
        start_scatters(r, slot)

    @pl.when(r == n_used - 1)
    def _():
        wait_gathers(1 - slot)
        wait_scatters(slot)

        @pl.when(r >= 1)
        def _():
            wait_scatters(1 - slot)


def _experts(tile_expert, n_used, unit_rows, next_expert, expert_parity, xy, wg, wu, wd):
    grid_spec = pltpu.PrefetchScalarGridSpec(
        num_scalar_prefetch=5,
        grid=(N_EXP_TILES,),
        in_specs=[pl.BlockSpec(memory_space=pl.ANY),
                  pl.BlockSpec(memory_space=pl.ANY),
                  pl.BlockSpec(memory_space=pl.ANY),
                  pl.BlockSpec(memory_space=pl.ANY)],
        out_specs=pl.BlockSpec(memory_space=pl.ANY),
        scratch_shapes=[pltpu.VMEM((2, TM_EXP, XY_COLS), BF16),
                        pltpu.VMEM((2, TM_EXP, D_MODEL), BF16),
                        pltpu.VMEM((D_MODEL, D_FF_EXPERT), BF16),
                        pltpu.VMEM((D_MODEL, D_FF_EXPERT), BF16),
                        pltpu.VMEM((D_FF_EXPERT, D_MODEL), BF16),
                        pltpu.VMEM((2, D_MODEL, D_FF_EXPERT), F32),
                        pltpu.VMEM((2, D_MODEL, D_FF_EXPERT), F32),
                        pltpu.VMEM((2, D_FF_EXPERT, D_MODEL), F32),
                        pltpu.SemaphoreType.DMA((2,)),
                        pltpu.SemaphoreType.DMA((2,)),
                        pltpu.SemaphoreType.DMA((2,))])
    return pl.pallas_call(
        _experts_kernel,
        out_shape=jax.ShapeDtypeStruct(xy.shape, xy.dtype),
        grid_spec=grid_spec,
        input_output_aliases={5: 0},
        compiler_params=_cparams(1),
        name="experts",
    )(tile_expert, n_used, unit_rows, next_expert, expert_parity, xy, wg, wu, wd)


def _combine_kernel(x1_ref, rc_ref, gf_ref, gfin_ref, y_ref, o_ref):
    lp1 = rc_ref[:, 0:1]
    lp2 = rc_ref[:, 1:2]
    scol = lax.broadcasted_iota(jnp.int32, (TM_ROW, XY_ROWS), 1).astype(F32)
    pick = jnp.where((scol == lp1) | (scol == lp2), 1.0, 0.0).astype(BF16)
    y = jnp.dot(pick, y_ref[...], preferred_element_type=F32)
    xf = x1_ref[...] + gf_ref[...] * y
    rs = lax.rsqrt(jnp.mean(xf * xf, axis=-1, keepdims=True) + EPS)
    o_ref[...] = xf * rs * gfin_ref[...]


def _combine(x1, rcol, gate_f, g_final, xy):
    tm = TM_ROW
    tpb = SEQ // tm
    row = lambda i: (i, 0)
    return pl.pallas_call(
        _combine_kernel,
        out_shape=jax.ShapeDtypeStruct((N_TOK, D_MODEL), F32),
        grid=(N_TOK // tm,),
        in_specs=[pl.BlockSpec((tm, D_MODEL), row),
                  pl.BlockSpec((tm, LANES), row),
                  pl.BlockSpec((None, 1, D_MODEL), lambda i: (i // tpb, 0, 0)),
                  pl.BlockSpec((1, D_MODEL), lambda i: (0, 0)),
                  pl.BlockSpec((XY_ROWS, D_MODEL), row)],
        out_specs=pl.BlockSpec((tm, D_MODEL), row),
        compiler_params=_cparams(1),
        name="combine",
    )(x1, rcol, gate_f, g_final, xy)


def _t5_bucket_np():
    qi = np.arange(BLOCK)[:, None]
    kj = np.arange(2 * BLOCK)[None, :]
    dist = qi - kj + BLOCK
    n = np.maximum(dist, 0)
    max_exact = NUM_BUCKETS // 2
    nf = np.maximum(n, 1).astype(np.float32)
    large = max_exact + (np.log(nf / np.float32(max_exact)) / np.float32(math.log(MAX_DISTANCE / max_exact))
                         * np.float32(NUM_BUCKETS - max_exact)).astype(np.int32)
    large = np.minimum(large, NUM_BUCKETS - 1)
    bucket = np.where(n < max_exact, n, large)
    band = (dist >= 0) & (dist < WINDOW)
    return bucket.astype(np.int32), band


def kernel(x, c, w_ada, b_ada, g_norm_mix, g_norm_ffn, w_in, sinks, b_forget, w_proj_swa, w_proj_fox,
           w_out, rel_bias_table, w_router_group, b_router_group, w_router_expert, b_router_expert,
           w_gate_exp, w_up_exp, w_down_exp, g_final):
    l = 0
    x2 = x.reshape(N_TOK, D_MODEL)

    c16 = jnp.concatenate([c, jnp.zeros_like(c)], axis=0)
    mod = _ada(c16, w_ada[l], b_ada[l][None, :])[:BATCH]
    shift_m, scale_m, gate_m, shift_f, scale_f, gate_f = [
        m.reshape(BATCH, 1, D_MODEL) for m in jnp.split(mod, 6, axis=-1)]

    w = w_in[l]
    o_ka, o_va, o_qb = Q_A, Q_A + KV_A, Q_A + 2 * KV_A
    o_kb, o_vb, o_f = o_qb + W_B, o_qb + 2 * W_B, o_qb + 3 * W_B
    o_g = o_f + N_HEADS_FOX

    def dup(cols):
        heads = [cols[:, h * HEAD_DIM:(h + 1) * HEAD_DIM] for h in range(N_KV_HEADS_SWA)]
        return jnp.concatenate([hd for hd in heads for _ in range(2)], axis=1)

    head_order = jnp.argsort(b_forget[l])
    pick = (head_order[:, None] == jnp.arange(N_HEADS_FOX)[None, :]).astype(F32)
    w_fox = jnp.einsum("gh,dthe->dtge", pick, w[:, o_qb:o_f].reshape(D_MODEL, 3, N_HEADS_FOX, HEAD_DIM),
                       precision=HIGHEST).reshape(D_MODEL, 3 * W_B)
    b_fox = jnp.einsum("gh,h->g", pick, b_forget[l], precision=HIGHEST)
    w_proj_b = jnp.einsum("gh,hed->ged", pick, w_proj_fox[l].reshape(N_HEADS_FOX, HEAD_DIM, D_MODEL),
                          precision=HIGHEST).reshape(W_B, D_MODEL)
    carrier = DECAY_LANES * N_HEADS_FOX
    w_f = jnp.pad(jnp.repeat(jnp.einsum("gh,dh->dg", pick, w[:, o_f:o_g], precision=HIGHEST), DECAY_LANES, axis=1),
                  ((0, 0), (0, LANES - carrier)))
    w_main = jnp.concatenate([w[:, :Q_A], dup(w[:, o_ka:o_va]), w[:, o_va:o_qb], w_f, w_fox], axis=1).astype(BF16)
    w_g = w[:, o_g:].astype(BF16)
    qa, kdup, va, qb, kb, vb, f_pad, gates, nrm = _inproj(
        x2, scale_m, shift_m, g_norm_mix[l][None, :], w_main, w_g)

    b_pad = jnp.pad(jnp.repeat(b_fox, DECAY_LANES), (0, LANES - carrier))[None, :]
    lanes = np.arange(LANES)
    jmod = jnp.asarray(np.where(lanes < DECAY_LANES * N_HEADS_FOX, lanes % DECAY_LANES, 7)[None, :].astype(np.int32))
    dq, dk, fb = _cum(f_pad, b_pad, jmod)

    bucket, band = _t5_bucket_np()
    onehot = jnp.asarray(bucket[None] == np.arange(NUM_BUCKETS)[:, None, None], dtype=F32)
    bias = jnp.einsum("bh,bqk->hqk", rel_bias_table.astype(F32), onehot, precision=HIGHEST)
    bias = jnp.where(band[None], bias * LOG2E, NEG_INF)
    first = np.arange(2 * BLOCK)[None, None, :] < BLOCK
    bias = jnp.stack([jnp.where(first, NEG_INF, bias), bias]).reshape(2, N_KV_HEADS_SWA, -1, 2 * BLOCK)
    o_a = _swa(sinks[l].astype(F32) * LOG2E, qa, kdup, va, bias)

    o_b = _fox(_fox_first_tiles(nrm, fb), qb, kb, vb, dq, dk)

    w_r = jnp.concatenate([w_router_group[l]] + [w_router_expert[l][g] for g in range(N_GROUPS)], axis=1)
    w_r = jnp.pad(w_r, ((0, 0), (0, LANES - w_r.shape[1])))
    wr_hi = w_r.astype(BF16)
    wr_lo = (w_r - wr_hi.astype(F32)).astype(BF16)
    wr2 = jnp.concatenate([wr_hi, wr_lo], axis=1)
    b_r = jnp.concatenate([b_router_group[l], b_router_expert[l].reshape(-1)])
    b_r = jnp.pad(b_r, (0, LANES - b_r.shape[0]))[None, :]
    x1, xy, rcol, cu = _post(x2, o_a, o_b, gates, gate_m, scale_f, shift_f, g_norm_ffn[l][None, :],
                             w_proj_swa[l].astype(BF16), w_proj_b.astype(BF16), w_out[l].astype(BF16),
                             wr2, b_r)

    i32 = jnp.int32
    n_tok_tiles = N_TOK // TM_POST
    cu = cu.reshape(n_tok_tiles, 8, LANES)[:, 0, :N_EXPERTS].astype(i32)
    loc_u = jnp.cumsum(cu, axis=1) - cu
    cend = jnp.cumsum(cu, axis=0)
    cstart = cend - cu
    tot_u = cend[-1]
    tiles_e = (tot_u + UNITS_PER_TILE - 1) // UNITS_PER_TILE
    tile_end = jnp.cumsum(tiles_e)
    tile_start = tile_end - tiles_e
    r = jnp.arange(N_EXP_TILES, dtype=i32)
    tile_expert = jnp.minimum(jnp.sum((tile_end[None, :] <= r[:, None]).astype(i32), axis=1), N_EXPERTS - 1)
    sel_e = tile_expert[:, None] == jnp.arange(N_EXPERTS, dtype=i32)[None, :]
    tw = r - jnp.sum(jnp.where(sel_e, tile_start[None, :], 0), axis=1)
    tot_r = jnp.sum(jnp.where(sel_e, tot_u[None, :], 0), axis=1)
    n_used = tile_end[-1:].astype(i32)
    q = tw[:, None] * UNITS_PER_TILE + jnp.arange(UNITS_PER_TILE, dtype=i32)[None, :]

    def of_expert(tab):
        return jnp.sum(jnp.where(sel_e[:, None, :], tab[None, :, :], 0), axis=2)

    cend_r, cstart_r, loc_r = of_expert(cend), of_expert(cstart), of_expert(loc_u)
    src_tile = jnp.minimum(jnp.sum((cend_r[:, None, :] <= q[:, :, None]).astype(i32), axis=2), n_tok_tiles - 1)
    sel_t = src_tile[:, :, None] == jnp.arange(n_tok_tiles, dtype=i32)[None, None, :]
    k = (q - jnp.sum(jnp.where(sel_t, cstart_r[:, None, :], 0), axis=2)
         + jnp.sum(jnp.where(sel_t, loc_r[:, None, :], 0), axis=2))
    real_rows = src_tile * XY_ROWS + k * UNIT
    pad_rows = PAD_BASE_ROW + (tile_expert[:, None] * PAD_UNITS_PER_EXPERT + (q - tot_r[:, None])) * UNIT
    idle_row = PAD_BASE_ROW + N_EXPERTS * PAD_UNITS_PER_EXPERT * UNIT
    unit_rows = jnp.where(q < tot_r[:, None], real_rows, pad_rows)
    unit_rows = jnp.where((r < n_used)[:, None], unit_rows, idle_row).reshape(-1).astype(i32)

    eid = jnp.arange(N_EXPERTS, dtype=i32)
    used = tiles_e > 0
    later_used = (eid[None, :] > eid[:, None]) & used[None, :]
    next_e = jnp.min(jnp.where(later_used, eid[None, :], N_EXPERTS), axis=1)
    next_e = jnp.where(next_e == N_EXPERTS, -1, next_e)
    parity_e = (jnp.cumsum(used.astype(i32)) - used.astype(i32)) % 2
    next_expert = jnp.sum(jnp.where(sel_e, next_e[None, :], 0), axis=1).astype(i32)
    expert_parity = jnp.sum(jnp.where(sel_e, parity_e[None, :], 0), axis=1).astype(i32)

    xy = _experts(tile_expert.astype(i32), n_used, unit_rows, next_expert, expert_parity, xy,
                  w_gate_exp[l].reshape(N_EXPERTS, D_MODEL, D_FF_EXPERT),
                  w_up_exp[l].reshape(N_EXPERTS, D_MODEL, D_FF_EXPERT),
                  w_down_exp[l].reshape(N_EXPERTS, D_FF_EXPERT, D_MODEL))
    out = _combine(x1, rcol, gate_f, g_final[None, :], xy)
    return out.reshape(BATCH, SEQ, D_MODEL)
```

```python
import math

import numpy as np
import jax
import jax.numpy as jnp
from jax import lax
from jax.experimental import pallas as pl
from jax.experimental.pallas import tpu as pltpu

F32 = jnp.float32
BF16 = jnp.bfloat16
HIGHEST = lax.Precision.HIGHEST

D_MODEL = 1024
BATCH = 8
SEQ = 4096
N_TOK = BATCH * SEQ
N_HEADS_SWA = 8
N_KV_HEADS_SWA = 2
N_HEADS_FOX = 8
HEAD_DIM = 64
WINDOW = 128
BLOCK = 128
NUM_BUCKETS = 32
MAX_DISTANCE = 128
N_GROUPS = 4
EXPERTS_PER_GROUP = 8
N_EXPERTS = N_GROUPS * EXPERTS_PER_GROUP
D_FF_EXPERT = 256
EPS = 1e-6
NEG_INF = -1e30

Q_A = N_HEADS_SWA * HEAD_DIM
KV_A = N_KV_HEADS_SWA * HEAD_DIM
W_B = N_HEADS_FOX * HEAD_DIM
LANES = 128
QK_SCALE = HEAD_DIM ** -0.5

TM_IN = 512
TM_POST = 512
TQ_FOX = 512
TK_FOX = TQ_FOX
SWA_BLOCKS = 4
TM_EXP = 512
TM_ROW = 512
UNIT = 16
XY_UNITS = 2 * TM_POST // UNIT + N_EXPERTS
XY_ROWS = XY_UNITS * UNIT
XY_COLS = D_MODEL + LANES
UNITS_PER_TILE = TM_EXP // UNIT
N_TOK_TILES = N_TOK // TM_POST
N_EXP_TILES = N_TOK_TILES * XY_UNITS // UNITS_PER_TILE + N_EXPERTS
PAD_UNITS_PER_EXPERT = UNITS_PER_TILE - 1
PAD_BLOCKS = -(-(N_EXPERTS * PAD_UNITS_PER_EXPERT * UNIT) // XY_ROWS)
PAD_BASE_ROW = N_TOK_TILES * XY_ROWS
W1_LANES, W2_LANES, E1_LANE, E2_LANE = (4, 6), (5, 7), 8, 9
VMEM_LIMIT = 56 * 1024 * 1024

DECAY_LANES = 6
LOG2E = math.log2(math.e)
Q_SCALE_LOG2 = QK_SCALE * LOG2E
PRUNE_MARGIN = 160.0


def _cparams(n_axes):
    return pltpu.CompilerParams(dimension_semantics=("arbitrary",) * n_axes,
                                vmem_limit_bytes=VMEM_LIMIT)


def _ada_kernel(c_ref, w_ref, b_ref, o_ref):
    c = c_ref[...]
    ca = c * jax.nn.sigmoid(c)
    o_ref[...] = jnp.dot(ca.astype(BF16), w_ref[...].astype(BF16),
                         preferred_element_type=F32) + b_ref[...]


def _ada(c16, w_ada, b_ada):
    n_out = w_ada.shape[1]
    blk = 1024
    return pl.pallas_call(
        _ada_kernel,
        out_shape=jax.ShapeDtypeStruct((16, n_out), F32),
        grid=(n_out // blk,),
        in_specs=[pl.BlockSpec((16, D_MODEL), lambda j: (0, 0)),
                  pl.BlockSpec((D_MODEL, blk), lambda j: (0, j)),
                  pl.BlockSpec((1, blk), lambda j: (0, j))],
        out_specs=pl.BlockSpec((16, blk), lambda j: (0, j)),
        compiler_params=_cparams(1),
        name="ada",
    )(c16, w_ada, b_ada)


def _inproj_kernel(x_ref, sc_ref, sh_ref, g_ref, wm_ref, wg_ref, ind_ref,
                   qa_ref, kd_ref, va_ref, qb_ref, kb_ref, vb_ref, f_ref, gt_ref, nrm_ref):
    x = x_ref[...]
    rs = lax.rsqrt(jnp.mean(x * x, axis=-1, keepdims=True) + EPS)
    a = g_ref[...] * (1.0 + sc_ref[...])
    h = (x * rs * a + sh_ref[...]).astype(BF16)

    def mm(w):
        return jnp.dot(h, w, preferred_element_type=F32)

    qa_ref[...] = (mm(wm_ref[:, 0:512]) * Q_SCALE_LOG2).astype(BF16)
    kd_ref[...] = mm(wm_ref[:, 512:768]).astype(BF16)
    vf = mm(wm_ref[:, 768:1024])
    f_ref[...] = vf[:, LANES:2 * LANES]
    v = vf[:, 0:LANES]
    vr = pltpu.roll(v, HEAD_DIM, 1)
    lo = lax.broadcasted_iota(jnp.int32, v.shape, 1) < HEAD_DIM
    va_ref[:, 0:LANES] = jnp.where(lo, v, vr).astype(BF16)
    va_ref[:, LANES:2 * LANES] = jnp.where(lo, vr, v).astype(BF16)
    qb = (mm(wm_ref[:, 1024:1536]) * Q_SCALE_LOG2).astype(BF16)
    kb = mm(wm_ref[:, 1536:2048]).astype(BF16)
    qb_ref[...] = qb
    kb_ref[...] = kb
    vb_ref[...] = mm(wm_ref[:, 2048:2560]).astype(BF16)
    sq = jnp.concatenate([qb, kb], axis=1).astype(F32)
    seg = jnp.dot((sq * sq).astype(BF16), ind_ref[...], preferred_element_type=F32)
    nrm_ref[...] = jnp.broadcast_to(jnp.max(seg, axis=0, keepdims=True), nrm_ref.shape)
    gt_ref[...] = mm(wg_ref[...]).astype(BF16)


def _inproj(x2, scale_m, shift_m, g_mix, w_main, w_g):
    tm = TM_IN
    tpb = SEQ // tm
    row = lambda i: (i, 0)
    per_b = lambda i: (i // tpb, 0, 0)
    const = lambda i: (0, 0)
    outs = [(Q_A, BF16), (2 * KV_A, BF16), (2 * KV_A, BF16), (W_B, BF16), (W_B, BF16), (W_B, BF16),
            (LANES, F32), (2 * D_MODEL, BF16)]
    ind_np = np.zeros((2 * W_B, LANES), np.float32)
    ind_np[np.arange(2 * W_B), np.arange(2 * W_B) // HEAD_DIM] = 1.0
    ind = jnp.asarray(ind_np, dtype=BF16)
    n_steps = N_TOK // tm
    return pl.pallas_call(
        _inproj_kernel,
        out_shape=[jax.ShapeDtypeStruct((N_TOK, w), dt) for w, dt in outs]
        + [jax.ShapeDtypeStruct((n_steps * 8, LANES), F32)],
        grid=(n_steps,),
        in_specs=[pl.BlockSpec((tm, D_MODEL), row),
                  pl.BlockSpec((None, 1, D_MODEL), per_b),
                  pl.BlockSpec((None, 1, D_MODEL), per_b),
                  pl.BlockSpec((1, D_MODEL), const),
                  pl.BlockSpec(w_main.shape, const),
                  pl.BlockSpec(w_g.shape, const),
                  pl.BlockSpec(ind.shape, const)],
        out_specs=[pl.BlockSpec((tm, w), row) for w, _ in outs] + [pl.BlockSpec((8, LANES), row)],
        compiler_params=_cparams(1),
        name="inproj",
    )(x2, scale_m, shift_m, g_mix, w_main, w_g, ind)


def _log_sigmoid(x):
    return jnp.minimum(x, 0.0) - jnp.log1p(jnp.exp(-jnp.abs(x)))


def _cum_kernel(f_ref, b_ref, jm_ref, qa_ref, ka_ref, fb_ref):
    cum = _log_sigmoid(f_ref[...] + b_ref[...]) * LOG2E
    row = lax.broadcasted_iota(jnp.int32, cum.shape, 0)
    k = 1
    while k < SEQ:
        if k < 8:
            shifted = jnp.where(row >= k, pltpu.roll(cum, k, 0), 0.0)
        else:
            shifted = jnp.concatenate([jnp.zeros((k, LANES), F32), cum[:SEQ - k]], axis=0)
        cum = cum + shifted
        k *= 2
    jm = jm_ref[...]
    for blk in range(SEQ // LANES):
        rows = slice(blk * LANES, (blk + 1) * LANES)
        cb = cum[rows]
        carry = cb[LANES - 1:LANES]
        hi = cb.astype(BF16).astype(F32)
        r1 = cb - hi
        mid = r1.astype(BF16).astype(F32)
        lo = (r1 - mid).astype(BF16).astype(F32)
        one = jnp.ones_like(cb)
        zero = jnp.zeros_like(cb)
        qa = jnp.where(jm == 0, hi, jnp.where(jm == 1, mid, jnp.where(jm == 2, lo,
                       jnp.where(jm < DECAY_LANES, one, zero))))
        ka = jnp.where(jm == 3, -hi, jnp.where(jm == 4, -mid, jnp.where(jm == 5, -lo,
                       jnp.where(jm < 3, one, zero))))
        qa_ref[rows, :] = qa.astype(BF16)
        ka_ref[rows, :] = ka.astype(BF16)
        blocks_per_tile = TQ_FOX // LANES
        tile = blk // blocks_per_tile
        if blk % blocks_per_tile == 0:
            fb_ref[2 * tile:2 * tile + 1, :] = cb[0:1]
        if blk % blocks_per_tile == blocks_per_tile - 1:
            fb_ref[2 * tile + 1:2 * tile + 2, :] = carry


def _cum(f_pad, b_pad, jmod):
    n_tiles = SEQ // TQ_FOX
    return pl.pallas_call(
        _cum_kernel,
        out_shape=[jax.ShapeDtypeStruct((BATCH, SEQ, LANES), BF16)] * 2
        + [jax.ShapeDtypeStruct((BATCH, 2 * n_tiles, LANES), F32)],
        grid=(BATCH,),
        in_specs=[pl.BlockSpec((SEQ, LANES), lambda b: (b, 0)),
                  pl.BlockSpec((1, LANES), lambda b: (0, 0)),
                  pl.BlockSpec((1, LANES), lambda b: (0, 0))],
        out_specs=[pl.BlockSpec((None, SEQ, LANES), lambda b: (b, 0, 0))] * 2
        + [pl.BlockSpec((None, 2 * n_tiles, LANES), lambda b: (b, 0, 0))],
        compiler_params=_cparams(1),
        name="cum",
    )(f_pad, b_pad, jmod)


def _swa_block(sink_cols, q, kk, vv, bias_ref, lo):
    tiles = []
    for g in range(N_KV_HEADS_SWA):
        parts = []
        for t in range(2):
            qt = q[:, (2 * g + t) * LANES:(2 * g + t + 1) * LANES]
            zero = jnp.zeros_like(qt)
            parts.append(jnp.where(lo, qt, zero))
            parts.append(jnp.where(lo, zero, qt))
        q4 = jnp.concatenate(parts, axis=0)
        s = lax.dot_general(q4, kk[:, g * LANES:(g + 1) * LANES], (((1,), (1,)), ((), ())),
                            preferred_element_type=F32)
        s = s + bias_ref[g]
        sink = sink_cols[g]
        m = jnp.maximum(jnp.max(s, axis=-1, keepdims=True), sink)
        p = jnp.exp2(s - m)
        den = jnp.sum(p, axis=-1, keepdims=True) + jnp.exp2(sink - m)
        o = jnp.dot(p.astype(BF16), vv[:, g * LANES:(g + 1) * LANES],
                    preferred_element_type=F32) / den
        tiles.append(jnp.where(lo, o[0:BLOCK], o[BLOCK:2 * BLOCK]))
        tiles.append(jnp.where(lo, o[2 * BLOCK:3 * BLOCK], o[3 * BLOCK:4 * BLOCK]))
    return tiles


def _swa_kernel(sink_ref, q_ref, kc_ref, kp_ref, vc_ref, vp_ref, bias_first_ref, bias_ref, o_ref):
    lane = lax.broadcasted_iota(jnp.int32, (BLOCK, LANES), 1)
    lo = lane < HEAD_DIM
    grp = N_HEADS_SWA // N_KV_HEADS_SWA
    row = lax.broadcasted_iota(jnp.int32, (grp * BLOCK, 1), 0)
    sink_cols = []
    for g in range(N_KV_HEADS_SWA):
        col = jnp.full((grp * BLOCK, 1), sink_ref[g * grp + grp - 1], F32)
        for hh in range(grp - 2, -1, -1):
            col = jnp.where(row < (hh + 1) * BLOCK, sink_ref[g * grp + hh], col)
        sink_cols.append(col)
    for blk in range(SWA_BLOCKS):
        rows = slice(blk * BLOCK, (blk + 1) * BLOCK)
        if blk == 0:
            kk = jnp.concatenate([kp_ref[...], kc_ref[rows, :]], axis=0)
            vv = jnp.concatenate([vp_ref[...], vc_ref[rows, :]], axis=0)
            bias = bias_first_ref
        else:
            prev_rows = slice((blk - 1) * BLOCK, (blk + 1) * BLOCK)
            kk = kc_ref[prev_rows, :]
            vv = vc_ref[prev_rows, :]
            bias = bias_ref
        tiles = _swa_block(sink_cols, q_ref[rows, :], kk, vv, bias, lo)
        for c, tile in enumerate(tiles):
            o_ref[rows, c * LANES:(c + 1) * LANES] = tile.astype(BF16)


def _swa(sinks, qa, kdup, va, bias):
    nb = SEQ // BLOCK
    ns = nb // SWA_BLOCKS
    cur = lambda b, i, s: (b * ns + i, 0)
    prev = lambda b, i, s: (b * nb + jnp.maximum(SWA_BLOCKS * i - 1, 0), 0)
    grid_spec = pltpu.PrefetchScalarGridSpec(
        num_scalar_prefetch=1,
        grid=(BATCH, ns),
        in_specs=[pl.BlockSpec((SWA_BLOCKS * BLOCK, Q_A), cur),
                  pl.BlockSpec((SWA_BLOCKS * BLOCK, 2 * KV_A), cur),
                  pl.BlockSpec((BLOCK, 2 * KV_A), prev),
                  pl.BlockSpec((SWA_BLOCKS * BLOCK, 2 * KV_A), cur),
                  pl.BlockSpec((BLOCK, 2 * KV_A), prev),
                  pl.BlockSpec((None,) + bias.shape[1:], lambda b, i, s: (jnp.minimum(i, 1), 0, 0, 0)),
                  pl.BlockSpec((None,) + bias.shape[1:], lambda b, i, s: (1, 0, 0, 0))],
        out_specs=pl.BlockSpec((SWA_BLOCKS * BLOCK, Q_A), cur))
    return pl.pallas_call(
        _swa_kernel,
        out_shape=jax.ShapeDtypeStruct((N_TOK, Q_A), BF16),
        grid_spec=grid_spec,
        compiler_params=_cparams(2),
        name="swa",
    )(sinks, qa, kdup, kdup, va, va, bias, bias)


def _fox_kernel(js_ref, q_ref, k_ref, v_ref, qa_ref, ka_ref, o_ref,
                kaug, vaug, q2, m_sc, acc_sc, s_a, s_b):
    tq, tk = TQ_FOX, TK_FOX
    b = pl.program_id(0)
    t = pl.program_id(1)
    i = pl.program_id(2)
    j_start = js_ref[(b * pl.num_programs(1) + t) * pl.num_programs(2) + i]

    @pl.when(i == 0)
    def _():
        kaug[:, 0:LANES] = k_ref[...]
        kaug[:, LANES:2 * LANES] = ka_ref[...]
        vaug[:, 0:LANES] = v_ref[...]
        vaug[:, LANES:2 * LANES] = jnp.ones((SEQ, LANES), BF16)

    lane = lax.broadcasted_iota(jnp.int32, (tq, LANES), 1)
    lo = lane < HEAD_DIM
    base = 2 * DECAY_LANES * t
    own = [(lane >= base + h * DECAY_LANES) & (lane < base + (h + 1) * DECAY_LANES) for h in range(2)]
    q = q_ref[...]
    qa = qa_ref[...]
    zero = jnp.zeros_like(q)
    q2[0, :, 0:LANES] = jnp.where(lo, q, zero)
    q2[1, :, 0:LANES] = jnp.where(lo, zero, q)
    for h in range(2):
        q2[h, :, LANES:2 * LANES] = jnp.where(own[h], qa, zero)
    m_sc[...] = jnp.full(m_sc.shape, NEG_INF, F32)
    acc_sc[...] = jnp.zeros(acc_sc.shape, F32)

    def scores(h, ks):
        return lax.dot_general(q2[h], kaug[pl.ds(ks, tk), :], (((1,), (1,)), ((), ())),
                               preferred_element_type=F32)

    def consume(h, s, ks, mask):
        if mask is not None:
            s = jnp.where(mask, s, NEG_INF)
        m_prev = m_sc[h]
        m_new = jnp.maximum(m_prev, jnp.max(s, axis=-1, keepdims=True))
        alpha = jnp.exp2(m_prev - m_new)
        p = jnp.exp2(s - jnp.concatenate([m_new] * (tk // LANES), axis=1))
        pv = jnp.dot(p.astype(BF16), vaug[pl.ds(ks, tk), :], preferred_element_type=F32)
        acc_sc[h] = jnp.concatenate([alpha, alpha], axis=1) * acc_sc[h] + pv
        m_sc[h] = m_new

    def key_start(j):
        return pl.multiple_of(j * tk, tk)

    def scores_into(buf, j):
        for h in range(2):
            buf[h] = scores(h, key_start(j))

    def consume_from(buf, j, mask):
        for h in range(2):
            consume(h, buf[h], key_start(j), mask)

    rr = lax.broadcasted_iota(jnp.int32, (tq, tk), 0)
    cc = lax.broadcasted_iota(jnp.int32, (tq, tk), 1)
    causal = cc <= rr
    n_full = i - j_start

    scores_into(s_a, j_start)

    def pair(p, carry):
        j = j_start + 2 * p
        scores_into(s_b, j + 1)
        consume_from(s_a, j, None)
        scores_into(s_a, j + 2)
        consume_from(s_b, j + 1, None)
        return carry

    lax.fori_loop(0, n_full // 2, pair, 0)
    odd = lax.rem(n_full, 2) == 1

    @pl.when(odd)
    def _():
        scores_into(s_b, i)
        consume_from(s_a, i - 1, None)
        consume_from(s_b, i, causal)

    @pl.when(jnp.logical_not(odd))
    def _():
        consume_from(s_a, i, causal)

    outs = [acc_sc[h, :, 0:LANES] / acc_sc[h, :, LANES:2 * LANES] for h in range(2)]
    o_ref[...] = jnp.where(lo, outs[0], outs[1]).astype(BF16)


def _fox(j_start, qb, kb, vb, qa, ka):
    tq = TQ_FOX
    nq = SEQ // tq
    n_pairs = N_HEADS_FOX // 2
    qmap = lambda b, t, i, js: (b * nq + i, t)
    kmap = lambda b, t, i, js: (b, t)
    grid_spec = pltpu.PrefetchScalarGridSpec(
        num_scalar_prefetch=1,
        grid=(BATCH, n_pairs, nq),
        in_specs=[pl.BlockSpec((tq, LANES), qmap),
                  pl.BlockSpec((SEQ, LANES), kmap),
                  pl.BlockSpec((SEQ, LANES), kmap),
                  pl.BlockSpec((None, tq, LANES), lambda b, t, i, js: (b, i, 0)),
                  pl.BlockSpec((None, SEQ, LANES), lambda b, t, i, js: (b, 0, 0))],
        out_specs=pl.BlockSpec((tq, LANES), qmap),
        scratch_shapes=[pltpu.VMEM((SEQ, 2 * LANES), BF16),
                        pltpu.VMEM((SEQ, 2 * LANES), BF16),
                        pltpu.VMEM((2, tq, 2 * LANES), BF16),
                        pltpu.VMEM((2, tq, LANES), F32),
                        pltpu.VMEM((2, tq, 2 * LANES), F32),
                        pltpu.VMEM((2, tq, TK_FOX), F32),
                        pltpu.VMEM((2, tq, TK_FOX), F32)])
    return pl.pallas_call(
        _fox_kernel,
        out_shape=jax.ShapeDtypeStruct((N_TOK, W_B), BF16),
        grid_spec=grid_spec,
        compiler_params=_cparams(3),
        name="fox",
    )(j_start, qb, kb, vb, qa, ka)


def _fox_first_tiles(nrm, fb):
    n_tiles = SEQ // TQ_FOX
    nr = nrm.reshape(BATCH, SEQ // TM_IN, 8, LANES)[:, :, 0, :] * 1.02
    nr = jnp.repeat(nr, TM_IN // TQ_FOX, axis=1)
    qn = jnp.sqrt(nr[..., 0:N_HEADS_FOX])
    kn = jnp.sqrt(nr[..., N_HEADS_FOX:2 * N_HEADS_FOX])
    f_first = fb[:, 0::2, 0:DECAY_LANES * N_HEADS_FOX:DECAY_LANES]
    f_last = fb[:, 1::2, 0:DECAY_LANES * N_HEADS_FOX:DECAY_LANES]
    kn_prefix = lax.cummax(kn, axis=1)
    upper = qn[:, :, None, :] * kn_prefix[:, None, :, :] + f_first[:, :, None, :] - f_last[:, None, :, :]
    row_max_low = -(qn * kn)[:, :, None, :]
    ii = jnp.arange(n_tiles)[None, :, None, None]
    jj = jnp.arange(n_tiles)[None, None, :, None]
    skip = (upper < row_max_low - PRUNE_MARGIN) & (jj < ii)
    skip = jnp.all(skip.reshape(BATCH, n_tiles, n_tiles, N_HEADS_FOX // 2, 2), axis=-1)
    first = jnp.sum(jnp.cumprod(skip.astype(jnp.int32), axis=2), axis=2)
    return jnp.transpose(first, (0, 2, 1)).reshape(-1).astype(jnp.int32)


def _post_kernel(x_ref, oa_ref, ob_ref, gt_ref, gm_ref, sc_ref, sh_ref, g_ref,
                 wa_ref, wb_ref, wo_ref, wr2_ref, br_ref,
                 x1_ref, xy_ref, rc_ref, cu_ref, hh_prev, rt_prev):
    step = pl.program_id(0)

    @pl.when(step == 0)
    def _():
        hh_prev[...] = jnp.zeros(hh_prev.shape, BF16)
        rt_prev[...] = jnp.zeros(rt_prev.shape, F32)

    @pl.when(step <= N_TOK_TILES)
    def _():
        hh_p = hh_prev[...]
        rt_p = rt_prev[...]
        hh, logits = _post_mix(x_ref, oa_ref, ob_ref, gt_ref, gm_ref, sc_ref, sh_ref, g_ref,
                               wa_ref, wb_ref, wo_ref, wr2_ref, br_ref, x1_ref)
        rt = _post_topk(logits)
        _post_sort(hh_p, rt_p, xy_ref, rc_ref, cu_ref)
        hh_prev[...] = hh
        rt_prev[...] = rt

    @pl.when(step > N_TOK_TILES)
    def _():
        xy_ref[...] = jnp.zeros(xy_ref.shape, BF16)


def _post_mix(x_ref, oa_ref, ob_ref, gt_ref, gm_ref, sc_ref, sh_ref, g_ref,
              wa_ref, wb_ref, wo_ref, wr2_ref, br_ref, x1_ref):
    pa = jnp.dot(oa_ref[...], wa_ref[...], preferred_element_type=F32)
    pb = jnp.dot(ob_ref[...], wb_ref[...], preferred_element_type=F32)
    ga = jax.nn.sigmoid(gt_ref[:, 0:D_MODEL].astype(F32))
    gb = jax.nn.sigmoid(gt_ref[:, D_MODEL:2 * D_MODEL].astype(F32))
    merged = (ga * pa + gb * pb).astype(BF16)
    y = jnp.dot(merged, wo_ref[...], preferred_element_type=F32)
    x1 = x_ref[...] + gm_ref[...] * y
    x1_ref[...] = x1

    rs = lax.rsqrt(jnp.mean(x1 * x1, axis=-1, keepdims=True) + EPS)
    a = g_ref[...] * (1.0 + sc_ref[...])
    h2 = x1 * rs * a + sh_ref[...]

    hh = h2.astype(BF16)
    hl = (h2 - hh.astype(F32)).astype(BF16)
    hi_both = jnp.dot(hh, wr2_ref[...], preferred_element_type=F32)
    logits = (hi_both[:, 0:LANES] + hi_both[:, LANES:2 * LANES]
              + jnp.dot(hl, wr2_ref[:, 0:LANES], preferred_element_type=F32)
              + br_ref[...])
    return hh, logits


def _post_topk(logits):
    tm = TM_POST
    lane = lax.broadcasted_iota(jnp.int32, (tm, LANES), 1).astype(F32)
    big = float(LANES)
    gl = jnp.where(lane < N_GROUPS, logits, -jnp.inf)
    gmax = jnp.max(gl, axis=-1, keepdims=True)
    gi = jnp.min(jnp.where(gl == gmax, lane, big), axis=-1, keepdims=True)
    gsum = jnp.sum(jnp.exp(gl - gmax), axis=-1, keepdims=True)
    gp = 1.0 / gsum
    e_lo = N_GROUPS + EXPERTS_PER_GROUP * gi
    el = jnp.where((lane >= e_lo) & (lane < e_lo + EXPERTS_PER_GROUP), logits, -jnp.inf)
    v1 = jnp.max(el, axis=-1, keepdims=True)
    i1 = jnp.min(jnp.where(el == v1, lane, big), axis=-1, keepdims=True)
    el2 = jnp.where(lane == i1, -jnp.inf, el)
    v2 = jnp.max(el2, axis=-1, keepdims=True)
    i2 = jnp.min(jnp.where(el2 == v2, lane, big), axis=-1, keepdims=True)
    e21 = jnp.exp(v2 - v1)
    w1 = gp / (1.0 + e21)
    w2 = gp * e21 / (1.0 + e21)
    e1 = i1 - N_GROUPS
    e2 = i2 - N_GROUPS
    return jnp.where(lane == 0, e1, jnp.where(lane == 1, e2, jnp.where(lane == 2, w1, jnp.where(lane == 3, w2, 0.0))))


def _post_sort(hh, rt, xy_ref, rc_ref, cu_ref):
    tm = TM_POST
    lane = lax.broadcasted_iota(jnp.int32, (tm, LANES), 1).astype(F32)
    e1, e2, w1, w2 = rt[:, 0:1], rt[:, 1:2], rt[:, 2:3], rt[:, 3:4]
    oh = jnp.where((lane == e1) | (lane == e2), 1.0, 0.0)
    cnt_u = jnp.floor((jnp.sum(oh, axis=0, keepdims=True) + (UNIT - 1)) * (1.0 / UNIT))
    r128 = lax.broadcasted_iota(jnp.int32, (LANES, LANES), 0)
    c128 = lax.broadcasted_iota(jnp.int32, (LANES, LANES), 1)
    before_lane = jnp.where(r128 < c128, 1.0, 0.0).astype(BF16)
    loc_u = jnp.dot(jnp.broadcast_to(cnt_u, (8, LANES)).astype(BF16), before_lane,
                    preferred_element_type=F32)
    trow = lax.broadcasted_iota(jnp.int32, (tm, LANES), 0)
    seen = oh
    k = 1
    while k < tm:
        if k < 8:
            shifted = jnp.where(trow >= k, pltpu.roll(seen, k, 0), 0.0)
        else:
            shifted = jnp.concatenate([jnp.zeros((k, LANES), F32), seen[:tm - k]], axis=0)
        seen = seen + shifted
        k *= 2
    pos_e = (seen - oh) + loc_u[0:1] * UNIT
    lp1 = jnp.sum(jnp.where(lane == e1, pos_e, 0.0), axis=-1, keepdims=True)
    lp2 = jnp.sum(jnp.where(lane == e2, pos_e, 0.0), axis=-1, keepdims=True)

    def to_row(col):
        return jnp.transpose(jnp.broadcast_to(col, (tm, LANES)))[0:1]

    srow = lax.broadcasted_iota(jnp.int32, (XY_ROWS, tm), 0).astype(F32)
    pm1 = jnp.where(srow == to_row(lp1), 1.0, 0.0).astype(BF16)
    pm2 = jnp.where(srow == to_row(lp2), 1.0, 0.0).astype(BF16)
    w1h = w1.astype(BF16).astype(F32)
    w2h = w2.astype(BF16).astype(F32)
    side = jnp.where(lane == W1_LANES[0], w1h, jnp.where(lane == W1_LANES[1], w1 - w1h,
           jnp.where(lane == W2_LANES[0], w2h, jnp.where(lane == W2_LANES[1], w2 - w2h,
           jnp.where(lane == E1_LANE, e1, jnp.where(lane == E2_LANE, e2, 0.0))))))
    tok = jnp.concatenate([hh, side.astype(BF16)], axis=1)
    xy_ref[...] = jnp.dot(pm1 + pm2, tok, preferred_element_type=F32).astype(BF16)

    cu_ref[...] = jnp.broadcast_to(cnt_u, cu_ref.shape)
    rc_ref[...] = jnp.where(lane == 0, lp1, jnp.where(lane == 1, lp2, 0.0))


def _post(x2, oa, ob, gates, gate_m, scale_f, shift_f, g_ffn, wa, wb, wo, wr2, b_r):
    tm = TM_POST
    tpb = SEQ // tm
    n_steps = N_TOK_TILES
    row = lambda i: (jnp.minimum(i, n_steps - 1), 0)
    per_b = lambda i: (jnp.minimum(i, n_steps - 1) // tpb, 0, 0)
    routed = lambda i: (jnp.clip(i - 1, 0, n_steps - 1), 0)
    const = lambda i: (0, 0)
    return pl.pallas_call(
        _post_kernel,
        out_shape=[jax.ShapeDtypeStruct((N_TOK, D_MODEL), F32),
                   jax.ShapeDtypeStruct(((n_steps + PAD_BLOCKS) * XY_ROWS, XY_COLS), BF16),
                   jax.ShapeDtypeStruct((N_TOK, LANES), F32),
                   jax.ShapeDtypeStruct((n_steps * 8, LANES), F32)],
        grid=(n_steps + 1 + PAD_BLOCKS,),
        in_specs=[pl.BlockSpec((tm, D_MODEL), row),
                  pl.BlockSpec((tm, Q_A), row),
                  pl.BlockSpec((tm, W_B), row),
                  pl.BlockSpec((tm, 2 * D_MODEL), row),
                  pl.BlockSpec((None, 1, D_MODEL), per_b),
                  pl.BlockSpec((None, 1, D_MODEL), per_b),
                  pl.BlockSpec((None, 1, D_MODEL), per_b),
                  pl.BlockSpec((1, D_MODEL), const),
                  pl.BlockSpec(wa.shape, const),
                  pl.BlockSpec(wb.shape, const),
                  pl.BlockSpec(wo.shape, const),
                  pl.BlockSpec(wr2.shape, const),
                  pl.BlockSpec((1, LANES), const)],
        out_specs=[pl.BlockSpec((tm, D_MODEL), row),
                   pl.BlockSpec((XY_ROWS, XY_COLS), lambda i: (jnp.maximum(i - 1, 0), 0)),
                   pl.BlockSpec((tm, LANES), routed),
                   pl.BlockSpec((8, LANES), routed)],
        scratch_shapes=[pltpu.VMEM((tm, D_MODEL), BF16),
                        pltpu.VMEM((tm, LANES), F32)],
        compiler_params=_cparams(1),
        name="post",
    )(x2, oa, ob, gates, gate_m, scale_f, shift_f, g_ffn, wa, wb, wo, wr2, b_r)


def _experts_kernel(te_ref, nu_ref, ur_ref, ne_ref, ep_ref, xy_in, wg_hbm, wu_hbm, wd_hbm, xy_out,
                    xbuf, ybuf, wg_s, wu_s, wd_s, wg_f, wu_f, wd_f, gsem, ssem, wsem):
    del xy_in
    r = pl.program_id(0)
    last = pl.num_programs(0) - 1
    n_used = nu_ref[0]
    slot = lax.rem(r, 2)

    def unit_row(step, s):
        return pl.multiple_of(ur_ref[step * UNITS_PER_TILE + s], UNIT)

    def start_gathers(step, sl):
        for s in range(UNITS_PER_TILE):
            pltpu.make_async_copy(xy_out.at[pl.ds(unit_row(step, s), UNIT), :],
                                  xbuf.at[sl, pl.ds(s * UNIT, UNIT), :], gsem.at[sl]).start()

    def wait_gathers(sl):
        pltpu.make_async_copy(xy_out.at[pl.ds(0, TM_EXP), :], xbuf.at[sl], gsem.at[sl]).wait()

    def start_scatters(step, sl):
        for s in range(UNITS_PER_TILE):
            pltpu.make_async_copy(ybuf.at[sl, pl.ds(s * UNIT, UNIT), :],
                                  xy_out.at[pl.ds(unit_row(step, s), UNIT), pl.ds(0, D_MODEL)],
                                  ssem.at[sl]).start()

    def wait_scatters(sl):
        pltpu.make_async_copy(ybuf.at[sl], xy_out.at[pl.ds(0, TM_EXP), pl.ds(0, D_MODEL)], ssem.at[sl]).wait()

    @pl.when(r == 0)
    def _():
        start_gathers(0, 0)

    @pl.when(jnp.logical_and(r < n_used, r >= 2))
    def _():
        wait_scatters(slot)

    def weight_copies(e, p):
        return [pltpu.make_async_copy(src.at[e], dst.at[p], wsem.at[p])
                for src, dst in ((wg_hbm, wg_f), (wu_hbm, wu_f), (wd_hbm, wd_f))]

    @pl.when(jnp.logical_and(r < n_used,
                             jnp.logical_or(r == 0, te_ref[r] != te_ref[jnp.maximum(r - 1, 0)])))
    def _():
        e = te_ref[r]
        p = ep_ref[r]

        @pl.when(r == 0)
        def _():
            for cp in weight_copies(e, p):
                cp.start()

        for cp in weight_copies(e, p):
            cp.wait()
        wg_s[...] = wg_f[p].astype(BF16)
        wu_s[...] = wu_f[p].astype(BF16)
        wd_s[...] = wd_f[p].astype(BF16)

        @pl.when(ne_ref[r] >= 0)
        def _():
            for cp in weight_copies(ne_ref[r], 1 - p):
                cp.start()

    @pl.when(r < n_used)
    def _():
        wait_gathers(slot)
        start_gathers(jnp.minimum(r + 1, last), 1 - slot)
        x = xbuf[slot, :, 0:D_MODEL]
        side = xbuf[slot, :, D_MODEL:XY_COLS].astype(F32)
        lane = lax.broadcasted_iota(jnp.int32, side.shape, 1)

        def lanes_sum(a, b):
            return jnp.sum(jnp.where((lane == a) | (lane == b), side, 0.0), axis=-1, keepdims=True)

        is_slot1 = lanes_sum(E1_LANE, E1_LANE) == te_ref[r].astype(F32)
        wrow = jnp.where(is_slot1, lanes_sum(*W1_LANES), lanes_sum(*W2_LANES))
        a = jnp.dot(x, wg_s[...], preferred_element_type=F32)
        u = jnp.dot(x, wu_s[...], preferred_element_type=F32)
        hid = (a * jax.nn.sigmoid(a) * u * wrow).astype(BF16)
        ybuf[slot] = jnp.dot(hid, wd_s[...], preferred_element_type=F32).astype(BF16)
        start_scatters(r, slot)

    @pl.when(r == n_used - 1)
    def _():
        wait_gathers(1 - slot)
        wait_scatters(slot)

        @pl.when(r >= 1)
        def _():
            wait_scatters(1 - slot)


def _experts(tile_expert, n_used, unit_rows, next_expert, expert_parity, xy, wg, wu, wd):
    grid_spec = pltpu.PrefetchScalarGridSpec(
        num_scalar_prefetch=5,
        grid=(N_EXP_TILES,),
        in_specs=[pl.BlockSpec(memory_space=pl.ANY),
                  pl.BlockSpec(memory_space=pl.ANY),
                  pl.BlockSpec(memory_space=pl.ANY),
                  pl.BlockSpec(memory_space=pl.ANY)],
        out_specs=pl.BlockSpec(memory_space=pl.ANY),
        scratch_shapes=[pltpu.VMEM((2, TM_EXP, XY_COLS), BF16),
                        pltpu.VMEM((2, TM_EXP, D_MODEL), BF16),
                        pltpu.VMEM((D_MODEL, D_FF_EXPERT), BF16),
                        pltpu.VMEM((D_MODEL, D_FF_EXPERT), BF16),
                        pltpu.VMEM((D_FF_EXPERT, D_MODEL), BF16),
                        pltpu.VMEM((2, D_MODEL, D_FF_EXPERT), F32),
                        pltpu.VMEM((2, D_MODEL, D_FF_EXPERT), F32),
                        pltpu.VMEM((2, D_FF_EXPERT, D_MODEL), F32),
                        pltpu.SemaphoreType.DMA((2,)),
                        pltpu.SemaphoreType.DMA((2,)),
                        pltpu.SemaphoreType.DMA((2,))])
    return pl.pallas_call(
        _experts_kernel,
        out_shape=jax.ShapeDtypeStruct(xy.shape, xy.dtype),
        grid_spec=grid_spec,
        input_output_aliases={5: 0},
        compiler_params=_cparams(1),
        name="experts",
    )(tile_expert, n_used, unit_rows, next_expert, expert_parity, xy, wg, wu, wd)


def _combine_kernel(x1_ref, rc_ref, gf_ref, gfin_ref, y_ref, o_ref):
    lp1 = rc_ref[:, 0:1]
    lp2 = rc_ref[:, 1:2]
    scol = lax.broadcasted_iota(jnp.int32, (TM_ROW, XY_ROWS), 1).astype(F32)
    pick = jnp.where((scol == lp1) | (scol == lp2), 1.0, 0.0).astype(BF16)
    y = jnp.dot(pick, y_ref[...], preferred_element_type=F32)
    xf = x1_ref[...] + gf_ref[...] * y
    rs = lax.rsqrt(jnp.mean(xf * xf, axis=-1, keepdims=True) + EPS)
    o_ref[...] = xf * rs * gfin_ref[...]


def _combine(x1, rcol, gate_f, g_final, xy):
    tm = TM_ROW
    tpb = SEQ // tm
    row = lambda i: (i, 0)
    return pl.pallas_call(
        _combine_kernel,
        out_shape=jax.ShapeDtypeStruct((N_TOK, D_MODEL), F32),
        grid=(N_TOK // tm,),
        in_specs=[pl.BlockSpec((tm, D_MODEL), row),
                  pl.BlockSpec((tm, LANES), row),
                  pl.BlockSpec((None, 1, D_MODEL), lambda i: (i // tpb, 0, 0)),
                  pl.BlockSpec((1, D_MODEL), lambda i: (0, 0)),
                  pl.BlockSpec((XY_ROWS, D_MODEL), row)],
        out_specs=pl.BlockSpec((tm, D_MODEL), row),
        compiler_params=_cparams(1),
        name="combine",
    )(x1, rcol, gate_f, g_final, xy)


def _t5_bucket_np():
    qi = np.arange(BLOCK)[:, None]
    kj = np.arange(2 * BLOCK)[None, :]
    dist = qi - kj + BLOCK
    n = np.maximum(dist, 0)
    max_exact = NUM_BUCKETS // 2
    nf = np.maximum(n, 1).astype(np.float32)
    large = max_exact + (np.log(nf / np.float32(max_exact)) / np.float32(math.log(MAX_DISTANCE / max_exact))
                         * np.float32(NUM_BUCKETS - max_exact)).astype(np.int32)
    large = np.minimum(large, NUM_BUCKETS - 1)
    bucket = np.where(n < max_exact, n, large)
    band = (dist >= 0) & (dist < WINDOW)
    return bucket.astype(np.int32), band


def kernel(x, c, w_ada, b_ada, g_norm_mix, g_norm_ffn, w_in, sinks, b_forget, w_proj_swa, w_proj_fox,
           w_out, rel_bias_table, w_router_group, b_router_group, w_router_expert, b_router_expert,
           w_gate_exp, w_up_exp, w_down_exp, g_final):
    l = 0
    x2 = x.reshape(N_TOK, D_MODEL)

    c16 = jnp.concatenate([c, jnp.zeros_like(c)], axis=0)
    mod = _ada(c16, w_ada[l], b_ada[l][None, :])[:BATCH]
    shift_m, scale_m, gate_m, shift_f, scale_f, gate_f = [
        m.reshape(BATCH, 1, D_MODEL) for m in jnp.split(mod, 6, axis=-1)]

    w = w_in[l]
    o_ka, o_va, o_qb = Q_A, Q_A + KV_A, Q_A + 2 * KV_A
    o_kb, o_vb, o_f = o_qb + W_B, o_qb + 2 * W_B, o_qb + 3 * W_B
    o_g = o_f + N_HEADS_FOX

    def dup(cols):
        heads = [cols[:, h * HEAD_DIM:(h + 1) * HEAD_DIM] for h in range(N_KV_HEADS_SWA)]
        return jnp.concatenate([hd for hd in heads for _ in range(2)], axis=1)

    head_order = jnp.argsort(b_forget[l])

    w_fox = jnp.take(w[:, o_qb:o_f].reshape(D_MODEL, 3, N_HEADS_FOX, HEAD_DIM), head_order,
                     axis=2).reshape(D_MODEL, 3 * W_B)
    b_fox = jnp.take(b_forget[l], head_order)
    w_proj_b = jnp.take(w_proj_fox[l].reshape(N_HEADS_FOX, HEAD_DIM, D_MODEL), head_order, axis=0).reshape(W_B, D_MODEL)
    carrier = DECAY_LANES * N_HEADS_FOX
    w_f = jnp.pad(jnp.repeat(jnp.take(w[:, o_f:o_g], head_order, axis=1), DECAY_LANES, axis=1),
                  ((0, 0), (0, LANES - carrier)))
    w_main = jnp.concatenate([w[:, :Q_A], dup(w[:, o_ka:o_va]), w[:, o_va:o_qb], w_f, w_fox], axis=1).astype(BF16)
    w_g = w[:, o_g:].astype(BF16)
    qa, kdup, va, qb, kb, vb, f_pad, gates, nrm = _inproj(
        x2, scale_m, shift_m, g_norm_mix[l][None, :], w_main, w_g)

    b_pad = jnp.pad(jnp.repeat(b_fox, DECAY_LANES), (0, LANES - carrier))[None, :]
    lanes = np.arange(LANES)
    jmod = jnp.asarray(np.where(lanes < DECAY_LANES * N_HEADS_FOX, lanes % DECAY_LANES, 7)[None, :].astype(np.int32))
    dq, dk, fb = _cum(f_pad, b_pad, jmod)

    bucket, band = _t5_bucket_np()
    onehot = jnp.asarray(bucket[None] == np.arange(NUM_BUCKETS)[:, None, None], dtype=F32)
    bias = jnp.einsum("bh,bqk->hqk", rel_bias_table.astype(F32), onehot, precision=HIGHEST)
    bias = jnp.where(band[None], bias * LOG2E, NEG_INF)
    first = np.arange(2 * BLOCK)[None, None, :] < BLOCK
    bias = jnp.stack([jnp.where(first, NEG_INF, bias), bias]).reshape(2, N_KV_HEADS_SWA, -1, 2 * BLOCK)
    o_a = _swa(sinks[l].astype(F32) * LOG2E, qa, kdup, va, bias)

    o_b = _fox(_fox_first_tiles(nrm, fb), qb, kb, vb, dq, dk)

    w_r = jnp.concatenate([w_router_group[l]] + [w_router_expert[l][g] for g in range(N_GROUPS)], axis=1)
    w_r = jnp.pad(w_r, ((0, 0), (0, LANES - w_r.shape[1])))
    wr_hi = w_r.astype(BF16)
    wr_lo = (w_r - wr_hi.astype(F32)).astype(BF16)
    wr2 = jnp.concatenate([wr_hi, wr_lo], axis=1)
    b_r = jnp.concatenate([b_router_group[l], b_router_expert[l].reshape(-1)])
    b_r = jnp.pad(b_r, (0, LANES - b_r.shape[0]))[None, :]
    x1, xy, rcol, cu = _post(x2, o_a, o_b, gates, gate_m, scale_f, shift_f, g_norm_ffn[l][None, :],
                             w_proj_swa[l].astype(BF16), w_proj_b.astype(BF16), w_out[l].astype(BF16),
                             wr2, b_r)

    i32 = jnp.int32
    n_tok_tiles = N_TOK // TM_POST
    cu = cu.reshape(n_tok_tiles, 8, LANES)[:, 0, :N_EXPERTS].astype(i32)
    loc_u = jnp.cumsum(cu, axis=1) - cu
    cend = jnp.cumsum(cu, axis=0)
    cstart = cend - cu
    tot_u = cend[-1]
    tiles_e = (tot_u + UNITS_PER_TILE - 1) // UNITS_PER_TILE
    tile_end = jnp.cumsum(tiles_e)
    tile_start = tile_end - tiles_e
    r = jnp.arange(N_EXP_TILES, dtype=i32)
    tile_expert = jnp.minimum(jnp.sum((tile_end[None, :] <= r[:, None]).astype(i32), axis=1), N_EXPERTS - 1)
    sel_e = tile_expert[:, None] == jnp.arange(N_EXPERTS, dtype=i32)[None, :]
    tw = r - jnp.sum(jnp.where(sel_e, tile_start[None, :], 0), axis=1)
    tot_r = jnp.sum(jnp.where(sel_e, tot_u[None, :], 0), axis=1)
    n_used = tile_end[-1:].astype(i32)
    q = tw[:, None] * UNITS_PER_TILE + jnp.arange(UNITS_PER_TILE, dtype=i32)[None, :]

    def of_expert(tab):
        return jnp.sum(jnp.where(sel_e[:, None, :], tab[None, :, :], 0), axis=2)

    cend_r, cstart_r, loc_r = of_expert(cend), of_expert(cstart), of_expert(loc_u)
    src_tile = jnp.minimum(jnp.sum((cend_r[:, None, :] <= q[:, :, None]).astype(i32), axis=2), n_tok_tiles - 1)
    sel_t = src_tile[:, :, None] == jnp.arange(n_tok_tiles, dtype=i32)[None, None, :]
    k = (q - jnp.sum(jnp.where(sel_t, cstart_r[:, None, :], 0), axis=2)
         + jnp.sum(jnp.where(sel_t, loc_r[:, None, :], 0), axis=2))
    real_rows = src_tile * XY_ROWS + k * UNIT
    pad_rows = PAD_BASE_ROW + (tile_expert[:, None] * PAD_UNITS_PER_EXPERT + (q - tot_r[:, None])) * UNIT
    idle_row = PAD_BASE_ROW + N_EXPERTS * PAD_UNITS_PER_EXPERT * UNIT
    unit_rows = jnp.where(q < tot_r[:, None], real_rows, pad_rows)
    unit_rows = jnp.where((r < n_used)[:, None], unit_rows, idle_row).reshape(-1).astype(i32)

    eid = jnp.arange(N_EXPERTS, dtype=i32)
    used = tiles_e > 0
    later_used = (eid[None, :] > eid[:, None]) & used[None, :]
    next_e = jnp.min(jnp.where(later_used, eid[None, :], N_EXPERTS), axis=1)
    next_e = jnp.where(next_e == N_EXPERTS, -1, next_e)
    parity_e = (jnp.cumsum(used.astype(i32)) - used.astype(i32)) % 2
    next_expert = jnp.sum(jnp.where(sel_e, next_e[None, :], 0), axis=1).astype(i32)
    expert_parity = jnp.sum(jnp.where(sel_e, parity_e[None, :], 0), axis=1).astype(i32)

    xy = _experts(tile_expert.astype(i32), n_used, unit_rows, next_expert, expert_parity, xy,
                  w_gate_exp[l].reshape(N_EXPERTS, D_MODEL, D_FF_EXPERT),
                  w_up_exp[l].reshape(N_EXPERTS, D_MODEL, D_FF_EXPERT),
                  w_down_exp[l].reshape(N_EXPERTS, D_FF_EXPERT, D_MODEL))
    out = _combine(x1, rcol, gate_f, g_final[None, :], xy)
    return out.reshape(BATCH, SEQ, D_MODEL)
```

```python
import math

import numpy as np
import jax
import jax.numpy as jnp
from jax import lax
from jax.experimental import pallas as pl
from jax.experimental.pallas import tpu as pltpu

F32 = jnp.float32
BF16 = jnp.bfloat16
HIGHEST = lax.Precision.HIGHEST

D_MODEL = 1024
BATCH = 8
SEQ = 4096
N_TOK = BATCH * SEQ
N_HEADS_SWA = 8
N_KV_HEADS_SWA = 2
N_HEADS_FOX = 8
HEAD_DIM = 64
WINDOW = 128
BLOCK = 128
NUM_BUCKETS = 32
MAX_DISTANCE = 128
N_GROUPS = 4
EXPERTS_PER_GROUP = 8
N_EXPERTS = N_GROUPS * EXPERTS_PER_GROUP
D_FF_EXPERT = 256
EPS = 1e-6
NEG_INF = -1e30

Q_A = N_HEADS_SWA * HEAD_DIM
KV_A = N_KV_HEADS_SWA * HEAD_DIM
W_B = N_HEADS_FOX * HEAD_DIM
LANES = 128
QK_SCALE = HEAD_DIM ** -0.5

TM_IN = 512
TM_POST = 512
TQ_FOX = 512
TK_FOX = TQ_FOX
SWA_BLOCKS = 4
TM_EXP = 512
TM_ROW = 512
UNIT = 16
XY_UNITS = 2 * TM_POST // UNIT + N_EXPERTS
XY_ROWS = XY_UNITS * UNIT
XY_COLS = D_MODEL + LANES
UNITS_PER_TILE = TM_EXP // UNIT
N_TOK_TILES = N_TOK // TM_POST
N_EXP_TILES = N_TOK_TILES * XY_UNITS // UNITS_PER_TILE + N_EXPERTS
PAD_UNITS_PER_EXPERT = UNITS_PER_TILE - 1
PAD_BLOCKS = -(-(N_EXPERTS * PAD_UNITS_PER_EXPERT * UNIT) // XY_ROWS)
PAD_BASE_ROW = N_TOK_TILES * XY_ROWS
W1_LANES, W2_LANES, E1_LANE, E2_LANE = (4, 6), (5, 7), 8, 9
VMEM_LIMIT = 56 * 1024 * 1024

DECAY_LANES = 6
LOG2E = math.log2(math.e)
Q_SCALE_LOG2 = QK_SCALE * LOG2E
PRUNE_MARGIN = 160.0


def _cparams(n_axes):
    return pltpu.CompilerParams(dimension_semantics=("arbitrary",) * n_axes,
                                vmem_limit_bytes=VMEM_LIMIT)


def _ada_kernel(c_ref, w_ref, b_ref, o_ref):
    c = c_ref[...]
    ca = c * jax.nn.sigmoid(c)
    o_ref[...] = jnp.dot(ca.astype(BF16), w_ref[...].astype(BF16),
                         preferred_element_type=F32) + b_ref[...]


def _ada(c16, w_ada, b_ada):
    n_out = w_ada.shape[1]
    blk = 1024
    return pl.pallas_call(
        _ada_kernel,
        out_shape=jax.ShapeDtypeStruct((16, n_out), F32),
        grid=(n_out // blk,),
        in_specs=[pl.BlockSpec((16, D_MODEL), lambda j: (0, 0)),
                  pl.BlockSpec((D_MODEL, blk), lambda j: (0, j)),
                  pl.BlockSpec((1, blk), lambda j: (0, j))],
        out_specs=pl.BlockSpec((16, blk), lambda j: (0, j)),
        compiler_params=_cparams(1),
        name="ada",
    )(c16, w_ada, b_ada)


def _log_sigmoid(x):
    return jnp.minimum(x, 0.0) - jnp.log1p(jnp.exp(-jnp.abs(x)))


def _prefix_sum_rows(x):
    n = x.shape[0]
    row = lax.broadcasted_iota(jnp.int32, x.shape, 0)
    k = 1
    while k < n:
        if k < 8:
            shifted = jnp.where(row >= k, pltpu.roll(x, k, 0), 0.0)
        else:
            shifted = jnp.concatenate([jnp.zeros((k,) + x.shape[1:], x.dtype), x[:n - k]], axis=0)
        x = x + shifted
        k *= 2
    return x


def _inproj_kernel(x_ref, sc_ref, sh_ref, g_ref, wm_ref, wg_ref, ind_ref, bf_ref, jm_ref,
                   qa_ref, kd_ref, va_ref, qb_ref, kb_ref, vb_ref, gt_ref, nrm_ref, dq_ref, dk_ref, fb_ref,
                   carry):
    step = pl.program_id(0)

    @pl.when(step % (SEQ // TM_IN) == 0)
    def _():
        carry[...] = jnp.zeros(carry.shape, F32)

    x = x_ref[...]
    rs = lax.rsqrt(jnp.mean(x * x, axis=-1, keepdims=True) + EPS)
    a = g_ref[...] * (1.0 + sc_ref[...])
    h = (x * rs * a + sh_ref[...]).astype(BF16)

    def mm(w):
        return jnp.dot(h, w, preferred_element_type=F32)

    qa_ref[...] = (mm(wm_ref[:, 0:512]) * Q_SCALE_LOG2).astype(BF16)
    kd_ref[...] = mm(wm_ref[:, 512:768]).astype(BF16)
    vf = mm(wm_ref[:, 768:1024])
    cum = _prefix_sum_rows(_log_sigmoid(vf[:, LANES:2 * LANES] + bf_ref[...]) * LOG2E) + carry[...]
    carry[...] = cum[TM_IN - 1:TM_IN]
    hi = cum.astype(BF16).astype(F32)
    r1 = cum - hi
    mid = r1.astype(BF16).astype(F32)
    lo_piece = (r1 - mid).astype(BF16).astype(F32)
    jm = jm_ref[...]
    dq_ref[...] = jnp.where(jm == 0, hi, jnp.where(jm == 1, mid, jnp.where(jm == 2, lo_piece,
                            jnp.where(jm < DECAY_LANES, 1.0, 0.0)))).astype(BF16)
    dk_ref[...] = jnp.where(jm == 3, -hi, jnp.where(jm == 4, -mid, jnp.where(jm == 5, -lo_piece,
                            jnp.where(jm < 3, 1.0, 0.0)))).astype(BF16)
    r8 = lax.broadcasted_iota(jnp.int32, fb_ref.shape, 0)
    fb_ref[...] = jnp.where(r8 == 0, cum[0:1], jnp.where(r8 == 1, cum[TM_IN - 1:TM_IN], 0.0))
    v = vf[:, 0:LANES]
    vr = pltpu.roll(v, HEAD_DIM, 1)
    lo = lax.broadcasted_iota(jnp.int32, v.shape, 1) < HEAD_DIM
    va_ref[:, 0:LANES] = jnp.where(lo, v, vr).astype(BF16)
    va_ref[:, LANES:2 * LANES] = jnp.where(lo, vr, v).astype(BF16)
    qb = (mm(wm_ref[:, 1024:1536]) * Q_SCALE_LOG2).astype(BF16)
    kb = mm(wm_ref[:, 1536:2048]).astype(BF16)
    qb_ref[...] = qb
    kb_ref[...] = kb
    vb_ref[...] = mm(wm_ref[:, 2048:2560]).astype(BF16)
    sq = jnp.concatenate([qb, kb], axis=1).astype(F32)
    seg = jnp.dot((sq * sq).astype(BF16), ind_ref[...], preferred_element_type=F32)
    nrm_ref[...] = jnp.broadcast_to(jnp.max(seg, axis=0, keepdims=True), nrm_ref.shape)
    gt_ref[...] = mm(wg_ref[...]).astype(BF16)


def _inproj(x2, scale_m, shift_m, g_mix, w_main, w_g, b_pad, jmod):
    assert TM_IN == TQ_FOX
    tm = TM_IN
    tpb = SEQ // tm
    row = lambda i: (i, 0)
    per_b = lambda i: (i // tpb, 0, 0)
    const = lambda i: (0, 0)
    outs = [(Q_A, BF16), (2 * KV_A, BF16), (2 * KV_A, BF16), (W_B, BF16), (W_B, BF16), (W_B, BF16),
            (2 * D_MODEL, BF16)]
    tail = [(8, LANES, F32), (tm, LANES, BF16), (tm, LANES, BF16), (8, LANES, F32)]
    ind_np = np.zeros((2 * W_B, LANES), np.float32)
    ind_np[np.arange(2 * W_B), np.arange(2 * W_B) // HEAD_DIM] = 1.0
    ind = jnp.asarray(ind_np, dtype=BF16)
    n_steps = N_TOK // tm
    return pl.pallas_call(
        _inproj_kernel,
        out_shape=[jax.ShapeDtypeStruct((N_TOK, w), dt) for w, dt in outs]
        + [jax.ShapeDtypeStruct((n_steps * r, w), dt) for r, w, dt in tail],
        grid=(n_steps,),
        in_specs=[pl.BlockSpec((tm, D_MODEL), row),
                  pl.BlockSpec((None, 1, D_MODEL), per_b),
                  pl.BlockSpec((None, 1, D_MODEL), per_b),
                  pl.BlockSpec((1, D_MODEL), const),
                  pl.BlockSpec(w_main.shape, const),
                  pl.BlockSpec(w_g.shape, const),
                  pl.BlockSpec(ind.shape, const),
                  pl.BlockSpec((1, LANES), const),
                  pl.BlockSpec((1, LANES), const)],
        out_specs=[pl.BlockSpec((tm, w), row) for w, _ in outs]
        + [pl.BlockSpec((r, w), row) for r, w, _ in tail],
        scratch_shapes=[pltpu.VMEM((1, LANES), F32)],
        compiler_params=_cparams(1),
        name="inproj",
    )(x2, scale_m, shift_m, g_mix, w_main, w_g, ind, b_pad, jmod)


def _swa_block(sink_cols, q, kk, vv, bias_ref, lo):
    tiles = []
    for g in range(N_KV_HEADS_SWA):
        parts = []
        for t in range(2):
            qt = q[:, (2 * g + t) * LANES:(2 * g + t + 1) * LANES]
            zero = jnp.zeros_like(qt)
            parts.append(jnp.where(lo, qt, zero))
            parts.append(jnp.where(lo, zero, qt))
        q4 = jnp.concatenate(parts, axis=0)
        s = lax.dot_general(q4, kk[:, g * LANES:(g + 1) * LANES], (((1,), (1,)), ((), ())),
                            preferred_element_type=F32)
        s = s + bias_ref[g]
        sink = sink_cols[g]
        m = jnp.maximum(jnp.max(s, axis=-1, keepdims=True), sink)
        p = jnp.exp2(s - m)
        den = jnp.sum(p, axis=-1, keepdims=True) + jnp.exp2(sink - m)
        o = jnp.dot(p.astype(BF16), vv[:, g * LANES:(g + 1) * LANES],
                    preferred_element_type=F32) / den
        tiles.append(jnp.where(lo, o[0:BLOCK], o[BLOCK:2 * BLOCK]))
        tiles.append(jnp.where(lo, o[2 * BLOCK:3 * BLOCK], o[3 * BLOCK:4 * BLOCK]))
    return tiles


def _swa_kernel(sink_ref, q_ref, kc_ref, kp_ref, vc_ref, vp_ref, bias_first_ref, bias_ref, o_ref):
    lane = lax.broadcasted_iota(jnp.int32, (BLOCK, LANES), 1)
    lo = lane < HEAD_DIM
    grp = N_HEADS_SWA // N_KV_HEADS_SWA
    row = lax.broadcasted_iota(jnp.int32, (grp * BLOCK, 1), 0)
    sink_cols = []
    for g in range(N_KV_HEADS_SWA):
        col = jnp.full((grp * BLOCK, 1), sink_ref[g * grp + grp - 1], F32)
        for hh in range(grp - 2, -1, -1):
            col = jnp.where(row < (hh + 1) * BLOCK, sink_ref[g * grp + hh], col)
        sink_cols.append(col)
    for blk in range(SWA_BLOCKS):
        rows = slice(blk * BLOCK, (blk + 1) * BLOCK)
        if blk == 0:
            kk = jnp.concatenate([kp_ref[...], kc_ref[rows, :]], axis=0)
            vv = jnp.concatenate([vp_ref[...], vc_ref[rows, :]], axis=0)
            bias = bias_first_ref
        else:
            prev_rows = slice((blk - 1) * BLOCK, (blk + 1) * BLOCK)
            kk = kc_ref[prev_rows, :]
            vv = vc_ref[prev_rows, :]
            bias = bias_ref
        tiles = _swa_block(sink_cols, q_ref[rows, :], kk, vv, bias, lo)
        for c, tile in enumerate(tiles):
            o_ref[rows, c * LANES:(c + 1) * LANES] = tile.astype(BF16)


def _swa(sinks, qa, kdup, va, bias):
    nb = SEQ // BLOCK
    ns = nb // SWA_BLOCKS
    cur = lambda b, i, s: (b * ns + i, 0)
    prev = lambda b, i, s: (b * nb + jnp.maximum(SWA_BLOCKS * i - 1, 0), 0)
    grid_spec = pltpu.PrefetchScalarGridSpec(
        num_scalar_prefetch=1,
        grid=(BATCH, ns),
        in_specs=[pl.BlockSpec((SWA_BLOCKS * BLOCK, Q_A), cur),
                  pl.BlockSpec((SWA_BLOCKS * BLOCK, 2 * KV_A), cur),
                  pl.BlockSpec((BLOCK, 2 * KV_A), prev),
                  pl.BlockSpec((SWA_BLOCKS * BLOCK, 2 * KV_A), cur),
                  pl.BlockSpec((BLOCK, 2 * KV_A), prev),
                  pl.BlockSpec((None,) + bias.shape[1:], lambda b, i, s: (jnp.minimum(i, 1), 0, 0, 0)),
                  pl.BlockSpec((None,) + bias.shape[1:], lambda b, i, s: (1, 0, 0, 0))],
        out_specs=pl.BlockSpec((SWA_BLOCKS * BLOCK, Q_A), cur))
    return pl.pallas_call(
        _swa_kernel,
        out_shape=jax.ShapeDtypeStruct((N_TOK, Q_A), BF16),
        grid_spec=grid_spec,
        compiler_params=_cparams(2),
        name="swa",
    )(sinks, qa, kdup, kdup, va, va, bias, bias)


def _fox_kernel(js_ref, q_ref, k_ref, v_ref, qa_ref, ka_ref, o_ref,
                kaug, vaug, q2, m_sc, acc_sc, s_a, s_b):
    tq, tk = TQ_FOX, TK_FOX
    b = pl.program_id(0)
    t = pl.program_id(1)
    i = pl.program_id(2)
    j_start = js_ref[(b * pl.num_programs(1) + t) * pl.num_programs(2) + i]

    @pl.when(i == 0)
    def _():
        kaug[:, 0:LANES] = k_ref[...]
        kaug[:, LANES:2 * LANES] = ka_ref[...]
        vaug[:, 0:LANES] = v_ref[...]
        vaug[:, LANES:2 * LANES] = jnp.ones((SEQ, LANES), BF16)

    lane = lax.broadcasted_iota(jnp.int32, (tq, LANES), 1)
    lo = lane < HEAD_DIM
    base = 2 * DECAY_LANES * t
    own = [(lane >= base + h * DECAY_LANES) & (lane < base + (h + 1) * DECAY_LANES) for h in range(2)]
    q = q_ref[...]
    qa = qa_ref[...]
    zero = jnp.zeros_like(q)
    q2[0, :, 0:LANES] = jnp.where(lo, q, zero)
    q2[1, :, 0:LANES] = jnp.where(lo, zero, q)
    for h in range(2):
        q2[h, :, LANES:2 * LANES] = jnp.where(own[h], qa, zero)
    m_sc[...] = jnp.full(m_sc.shape, NEG_INF, F32)
    acc_sc[...] = jnp.zeros(acc_sc.shape, F32)

    def scores(h, ks):
        return lax.dot_general(q2[h], kaug[pl.ds(ks, tk), :], (((1,), (1,)), ((), ())),
                               preferred_element_type=F32)

    def consume(h, s, ks, mask):
        if mask is not None:
            s = jnp.where(mask, s, NEG_INF)
        m_prev = m_sc[h]
        m_new = jnp.maximum(m_prev, jnp.max(s, axis=-1, keepdims=True))
        alpha = jnp.exp2(m_prev - m_new)
        p = jnp.exp2(s - jnp.concatenate([m_new] * (tk // LANES), axis=1))
        pv = jnp.dot(p.astype(BF16), vaug[pl.ds(ks, tk), :], preferred_element_type=F32)
        acc_sc[h] = jnp.concatenate([alpha, alpha], axis=1) * acc_sc[h] + pv
        m_sc[h] = m_new

    def key_start(j):
        return pl.multiple_of(j * tk, tk)

    def scores_into(buf, j):
        for h in range(2):
            buf[h] = scores(h, key_start(j))

    def consume_from(buf, j, mask):
        for h in range(2):
            consume(h, buf[h], key_start(j), mask)

    rr = lax.broadcasted_iota(jnp.int32, (tq, tk), 0)
    cc = lax.broadcasted_iota(jnp.int32, (tq, tk), 1)
    causal = cc <= rr
    n_full = i - j_start

    scores_into(s_a, j_start)

    def pair(p, carry):
        j = j_start + 2 * p
        scores_into(s_b, j + 1)
        consume_from(s_a, j, None)
        scores_into(s_a, j + 2)
        consume_from(s_b, j + 1, None)
        return carry

    lax.fori_loop(0, n_full // 2, pair, 0)
    odd = lax.rem(n_full, 2) == 1

    @pl.when(odd)
    def _():
        scores_into(s_b, i)
        consume_from(s_a, i - 1, None)
        consume_from(s_b, i, causal)

    @pl.when(jnp.logical_not(odd))
    def _():
        consume_from(s_a, i, causal)

    outs = [acc_sc[h, :, 0:LANES] / acc_sc[h, :, LANES:2 * LANES] for h in range(2)]
    o_ref[...] = jnp.where(lo, outs[0], outs[1]).astype(BF16)


def _fox(j_start, qb, kb, vb, qa, ka):
    tq = TQ_FOX
    nq = SEQ // tq
    n_pairs = N_HEADS_FOX // 2
    qmap = lambda b, t, i, js: (b * nq + i, t)
    kmap = lambda b, t, i, js: (b, t)
    grid_spec = pltpu.PrefetchScalarGridSpec(
        num_scalar_prefetch=1,
        grid=(BATCH, n_pairs, nq),
        in_specs=[pl.BlockSpec((tq, LANES), qmap),
                  pl.BlockSpec((SEQ, LANES), kmap),
                  pl.BlockSpec((SEQ, LANES), kmap),
                  pl.BlockSpec((None, tq, LANES), lambda b, t, i, js: (b, i, 0)),
                  pl.BlockSpec((None, SEQ, LANES), lambda b, t, i, js: (b, 0, 0))],
        out_specs=pl.BlockSpec((tq, LANES), qmap),
        scratch_shapes=[pltpu.VMEM((SEQ, 2 * LANES), BF16),
                        pltpu.VMEM((SEQ, 2 * LANES), BF16),
                        pltpu.VMEM((2, tq, 2 * LANES), BF16),
                        pltpu.VMEM((2, tq, LANES), F32),
                        pltpu.VMEM((2, tq, 2 * LANES), F32),
                        pltpu.VMEM((2, tq, TK_FOX), F32),
                        pltpu.VMEM((2, tq, TK_FOX), F32)])
    return pl.pallas_call(
        _fox_kernel,
        out_shape=jax.ShapeDtypeStruct((N_TOK, W_B), BF16),
        grid_spec=grid_spec,
        compiler_params=_cparams(3),
        name="fox",
    )(j_start, qb, kb, vb, qa, ka)


def _fox_first_tiles(nrm, fb):
    n_tiles = SEQ // TQ_FOX
    nr = nrm.reshape(BATCH, SEQ // TM_IN, 8, LANES)[:, :, 0, :] * 1.02
    nr = jnp.repeat(nr, TM_IN // TQ_FOX, axis=1)
    qn = jnp.sqrt(nr[..., 0:N_HEADS_FOX])
    kn = jnp.sqrt(nr[..., N_HEADS_FOX:2 * N_HEADS_FOX])
    fbt = fb.reshape(BATCH, n_tiles, 8, LANES)
    f_first = fbt[:, :, 0, 0:DECAY_LANES * N_HEADS_FOX:DECAY_LANES]
    f_last = fbt[:, :, 1, 0:DECAY_LANES * N_HEADS_FOX:DECAY_LANES]
    kn_prefix = lax.cummax(kn, axis=1)
    upper = qn[:, :, None, :] * kn_prefix[:, None, :, :] + f_first[:, :, None, :] - f_last[:, None, :, :]
    row_max_low = -(qn * kn)[:, :, None, :]
    ii = jnp.arange(n_tiles)[None, :, None, None]
    jj = jnp.arange(n_tiles)[None, None, :, None]
    skip = (upper < row_max_low - PRUNE_MARGIN) & (jj < ii)
    skip = jnp.all(skip.reshape(BATCH, n_tiles, n_tiles, N_HEADS_FOX // 2, 2), axis=-1)
    first = jnp.sum(jnp.cumprod(skip.astype(jnp.int32), axis=2), axis=2)
    return jnp.transpose(first, (0, 2, 1)).reshape(-1).astype(jnp.int32)


def _post_kernel(x_ref, oa_ref, ob_ref, gt_ref, gm_ref, sc_ref, sh_ref, g_ref,
                 wa_ref, wb_ref, wo_ref, wr2_ref, br_ref,
                 x1_ref, xy_ref, rc_ref, cu_ref, hh_prev, lg_prev):
    step = pl.program_id(0)

    @pl.when(step == 0)
    def _():
        hh_prev[...] = jnp.zeros(hh_prev.shape, BF16)
        lg_prev[...] = jnp.zeros(lg_prev.shape, F32)

    @pl.when(step <= N_TOK_TILES)
    def _():
        hh_p = hh_prev[...]
        lg_p = lg_prev[...]
        hh, logits = _post_mix(x_ref, oa_ref, ob_ref, gt_ref, gm_ref, sc_ref, sh_ref, g_ref,
                               wa_ref, wb_ref, wo_ref, wr2_ref, br_ref, x1_ref)
        _post_route(hh_p, lg_p, xy_ref, rc_ref, cu_ref)
        hh_prev[...] = hh
        lg_prev[...] = logits

    @pl.when(step > N_TOK_TILES)
    def _():
        xy_ref[...] = jnp.zeros(xy_ref.shape, BF16)


def _post_mix(x_ref, oa_ref, ob_ref, gt_ref, gm_ref, sc_ref, sh_ref, g_ref,
              wa_ref, wb_ref, wo_ref, wr2_ref, br_ref, x1_ref):
    pa = jnp.dot(oa_ref[...], wa_ref[...], preferred_element_type=F32)
    pb = jnp.dot(ob_ref[...], wb_ref[...], preferred_element_type=F32)
    ga = jax.nn.sigmoid(gt_ref[:, 0:D_MODEL].astype(F32))
    gb = jax.nn.sigmoid(gt_ref[:, D_MODEL:2 * D_MODEL].astype(F32))
    merged = (ga * pa + gb * pb).astype(BF16)
    y = jnp.dot(merged, wo_ref[...], preferred_element_type=F32)
    x1 = x_ref[...] + gm_ref[...] * y
    x1_ref[...] = x1

    rs = lax.rsqrt(jnp.mean(x1 * x1, axis=-1, keepdims=True) + EPS)
    a = g_ref[...] * (1.0 + sc_ref[...])
    h2 = x1 * rs * a + sh_ref[...]

    hh = h2.astype(BF16)
    hl = (h2 - hh.astype(F32)).astype(BF16)
    hi_both = jnp.dot(hh, wr2_ref[...], preferred_element_type=F32)
    logits = (hi_both[:, 0:LANES] + hi_both[:, LANES:2 * LANES]
              + jnp.dot(hl, wr2_ref[:, 0:LANES], preferred_element_type=F32)
              + br_ref[...])
    return hh, logits


def _post_route(hh, logits, xy_ref, rc_ref, cu_ref):
    tm = TM_POST
    lane = lax.broadcasted_iota(jnp.int32, (tm, LANES), 1).astype(F32)
    big = float(LANES)
    gl = jnp.where(lane < N_GROUPS, logits, -jnp.inf)
    gmax = jnp.max(gl, axis=-1, keepdims=True)
    gi = jnp.min(jnp.where(gl == gmax, lane, big), axis=-1, keepdims=True)
    gsum = jnp.sum(jnp.exp(gl - gmax), axis=-1, keepdims=True)
    gp = 1.0 / gsum
    e_lo = N_GROUPS + EXPERTS_PER_GROUP * gi
    el = jnp.where((lane >= e_lo) & (lane < e_lo + EXPERTS_PER_GROUP), logits, -jnp.inf)
    v1 = jnp.max(el, axis=-1, keepdims=True)
    i1 = jnp.min(jnp.where(el == v1, lane, big), axis=-1, keepdims=True)
    el2 = jnp.where(lane == i1, -jnp.inf, el)
    v2 = jnp.max(el2, axis=-1, keepdims=True)
    i2 = jnp.min(jnp.where(el2 == v2, lane, big), axis=-1, keepdims=True)
    e21 = jnp.exp(v2 - v1)
    w1 = gp / (1.0 + e21)
    w2 = gp * e21 / (1.0 + e21)
    e1 = i1 - N_GROUPS
    e2 = i2 - N_GROUPS

    oh = jnp.where((lane == e1) | (lane == e2), 1.0, 0.0)
    cnt_u = jnp.floor((jnp.sum(oh, axis=0, keepdims=True) + (UNIT - 1)) * (1.0 / UNIT))
    r128 = lax.broadcasted_iota(jnp.int32, (LANES, LANES), 0)
    c128 = lax.broadcasted_iota(jnp.int32, (LANES, LANES), 1)
    before_lane = jnp.where(r128 < c128, 1.0, 0.0).astype(BF16)
    loc_u = jnp.dot(jnp.broadcast_to(cnt_u, (8, LANES)).astype(BF16), before_lane,
                    preferred_element_type=F32)
    trow = lax.broadcasted_iota(jnp.int32, (tm, LANES), 0)
    seen = oh
    k = 1
    while k < tm:
        if k < 8:
            shifted = jnp.where(trow >= k, pltpu.roll(seen, k, 0), 0.0)
        else:
            shifted = jnp.concatenate([jnp.zeros((k, LANES), F32), seen[:tm - k]], axis=0)
        seen = seen + shifted
        k *= 2
    pos_e = (seen - oh) + loc_u[0:1] * UNIT
    lp1 = jnp.sum(jnp.where(lane == e1, pos_e, 0.0), axis=-1, keepdims=True)
    lp2 = jnp.sum(jnp.where(lane == e2, pos_e, 0.0), axis=-1, keepdims=True)

    def to_row(col):
        return jnp.transpose(jnp.broadcast_to(col, (tm, LANES)))[0:1]

    srow = lax.broadcasted_iota(jnp.int32, (XY_ROWS, tm), 0).astype(F32)
    pm1 = jnp.where(srow == to_row(lp1), 1.0, 0.0).astype(BF16)
    pm2 = jnp.where(srow == to_row(lp2), 1.0, 0.0).astype(BF16)
    w1h = w1.astype(BF16).astype(F32)
    w2h = w2.astype(BF16).astype(F32)
    side = jnp.where(lane == W1_LANES[0], w1h, jnp.where(lane == W1_LANES[1], w1 - w1h,
           jnp.where(lane == W2_LANES[0], w2h, jnp.where(lane == W2_LANES[1], w2 - w2h,
           jnp.where(lane == E1_LANE, e1, jnp.where(lane == E2_LANE, e2, 0.0))))))
    tok = jnp.concatenate([hh, side.astype(BF16)], axis=1)
    xy_ref[...] = jnp.dot(pm1 + pm2, tok, preferred_element_type=F32).astype(BF16)

    cu_ref[...] = jnp.broadcast_to(cnt_u, cu_ref.shape)
    rc_ref[...] = jnp.where(lane == 0, lp1, jnp.where(lane == 1, lp2, 0.0))


def _post(x2, oa, ob, gates, gate_m, scale_f, shift_f, g_ffn, wa, wb, wo, wr2, b_r):
    tm = TM_POST
    tpb = SEQ // tm
    n_steps = N_TOK_TILES
    row = lambda i: (jnp.minimum(i, n_steps - 1), 0)
    per_b = lambda i: (jnp.minimum(i, n_steps - 1) // tpb, 0, 0)
    routed = lambda i: (jnp.clip(i - 1, 0, n_steps - 1), 0)
    const = lambda i: (0, 0)
    return pl.pallas_call(
        _post_kernel,
        out_shape=[jax.ShapeDtypeStruct((N_TOK, D_MODEL), F32),
                   jax.ShapeDtypeStruct(((n_steps + PAD_BLOCKS) * XY_ROWS, XY_COLS), BF16),
                   jax.ShapeDtypeStruct((N_TOK, LANES), F32),
                   jax.ShapeDtypeStruct((n_steps * 8, LANES), F32)],
        grid=(n_steps + 1 + PAD_BLOCKS,),
        in_specs=[pl.BlockSpec((tm, D_MODEL), row),
                  pl.BlockSpec((tm, Q_A), row),
                  pl.BlockSpec((tm, W_B), row),
                  pl.BlockSpec((tm, 2 * D_MODEL), row),
                  pl.BlockSpec((None, 1, D_MODEL), per_b),
                  pl.BlockSpec((None, 1, D_MODEL), per_b),
                  pl.BlockSpec((None, 1, D_MODEL), per_b),
                  pl.BlockSpec((1, D_MODEL), const),
                  pl.BlockSpec(wa.shape, const),
                  pl.BlockSpec(wb.shape, const),
                  pl.BlockSpec(wo.shape, const),
                  pl.BlockSpec(wr2.shape, const),
                  pl.BlockSpec((1, LANES), const)],
        out_specs=[pl.BlockSpec((tm, D_MODEL), row),
                   pl.BlockSpec((XY_ROWS, XY_COLS), lambda i: (jnp.maximum(i - 1, 0), 0)),
                   pl.BlockSpec((tm, LANES), routed),
                   pl.BlockSpec((8, LANES), routed)],
        scratch_shapes=[pltpu.VMEM((tm, D_MODEL), BF16),
                        pltpu.VMEM((tm, LANES), F32)],
        compiler_params=_cparams(1),
        name="post",
    )(x2, oa, ob, gates, gate_m, scale_f, shift_f, g_ffn, wa, wb, wo, wr2, b_r)


def _experts_kernel(te_ref, nu_ref, ur_ref, ne_ref, ep_ref, xy_in, wg_hbm, wu_hbm, wd_hbm, xy_out,
                    xbuf, ybuf, wg_s, wu_s, wd_s, wg_f, wu_f, wd_f, gsem, ssem, wsem):
    del xy_in
    r = pl.program_id(0)
    last = pl.num_programs(0) - 1
    n_used = nu_ref[0]
    slot = lax.rem(r, 2)

    def unit_row(step, s):
        return pl.multiple_of(ur_ref[step * UNITS_PER_TILE + s], UNIT)

    def start_gathers(step, sl):
        for s in range(UNITS_PER_TILE):
            pltpu.make_async_copy(xy_out.at[pl.ds(unit_row(step, s), UNIT), :],
                                  xbuf.at[sl, pl.ds(s * UNIT, UNIT), :], gsem.at[sl]).start()

    def wait_gathers(sl):
        pltpu.make_async_copy(xy_out.at[pl.ds(0, TM_EXP), :], xbuf.at[sl], gsem.at[sl]).wait()

    def start_scatters(step, sl):
        for s in range(UNITS_PER_TILE):
            pltpu.make_async_copy(ybuf.at[sl, pl.ds(s * UNIT, UNIT), :],
                                  xy_out.at[pl.ds(unit_row(step, s), UNIT), pl.ds(0, D_MODEL)],
                                  ssem.at[sl]).start()

    def wait_scatters(sl):
        pltpu.make_async_copy(ybuf.at[sl], xy_out.at[pl.ds(0, TM_EXP), pl.ds(0, D_MODEL)], ssem.at[sl]).wait()

    @pl.when(r == 0)
    def _():
        start_gathers(0, 0)

    @pl.when(jnp.logical_and(r < n_used, r >= 2))
    def _():
        wait_scatters(slot)

    def weight_copies(e, p):
        return [pltpu.make_async_copy(src.at[e], dst.at[p], wsem.at[p])
                for src, dst in ((wg_hbm, wg_f), (wu_hbm, wu_f), (wd_hbm, wd_f))]

    @pl.when(jnp.logical_and(r < n_used,
                             jnp.logical_or(r == 0, te_ref[r] != te_ref[jnp.maximum(r - 1, 0)])))
    def _():
        e = te_ref[r]
        p = ep_ref[r]

        @pl.when(r == 0)
        def _():
            for cp in weight_copies(e, p):
                cp.start()

        for cp in weight_copies(e, p):
            cp.wait()
        wg_s[...] = wg_f[p].astype(BF16)
        wu_s[...] = wu_f[p].astype(BF16)
        wd_s[...] = wd_f[p].astype(BF16)

        @pl.when(ne_ref[r] >= 0)
        def _():
            for cp in weight_copies(ne_ref[r], 1 - p):
                cp.start()

    @pl.when(r < n_used)
    def _():
        wait_gathers(slot)
        start_gathers(jnp.minimum(r + 1, last), 1 - slot)
        x = xbuf[slot, :, 0:D_MODEL]
        side = xbuf[slot, :, D_MODEL:XY_COLS].astype(F32)
        lane = lax.broadcasted_iota(jnp.int32, side.shape, 1)

        def lanes_sum(a, b):
            return jnp.sum(jnp.where((lane == a) | (lane == b), side, 0.0), axis=-1, keepdims=True)

        is_slot1 = lanes_sum(E1_LANE, E1_LANE) == te_ref[r].astype(F32)
        wrow = jnp.where(is_slot1, lanes_sum(*W1_LANES), lanes_sum(*W2_LANES))
        a = jnp.dot(x, wg_s[...], preferred_element_type=F32)
        u = jnp.dot(x, wu_s[...], preferred_element_type=F32)
        hid = (a * jax.nn.sigmoid(a) * u * wrow).astype(BF16)
        ybuf[slot] = jnp.dot(hid, wd_s[...], preferred_element_type=F32).astype(BF16)
        start_scatters(r, slot)

    @pl.when(r == n_used - 1)
    def _():
        wait_gathers(1 - slot)
        wait_scatters(slot)

        @pl.when(r >= 1)
        def _():
            wait_scatters(1 - slot)


def _experts(tile_expert, n_used, unit_rows, next_expert, expert_parity, xy, wg, wu, wd):
    grid_spec = pltpu.PrefetchScalarGridSpec(
        num_scalar_prefetch=5,
        grid=(N_EXP_TILES,),
        in_specs=[pl.BlockSpec(memory_space=pl.ANY),
                  pl.BlockSpec(memory_space=pl.ANY),
                  pl.BlockSpec(memory_space=pl.ANY),
                  pl.BlockSpec(memory_space=pl.ANY)],
        out_specs=pl.BlockSpec(memory_space=pl.ANY),
        scratch_shapes=[pltpu.VMEM((2, TM_EXP, XY_COLS), BF16),
                        pltpu.VMEM((2, TM_EXP, D_MODEL), BF16),
                        pltpu.VMEM((D_MODEL, D_FF_EXPERT), BF16),
                        pltpu.VMEM((D_MODEL, D_FF_EXPERT), BF16),
                        pltpu.VMEM((D_FF_EXPERT, D_MODEL), BF16),
                        pltpu.VMEM((2, D_MODEL, D_FF_EXPERT), F32),
                        pltpu.VMEM((2, D_MODEL, D_FF_EXPERT), F32),
                        pltpu.VMEM((2, D_FF_EXPERT, D_MODEL), F32),
                        pltpu.SemaphoreType.DMA((2,)),
                        pltpu.SemaphoreType.DMA((2,)),
                        pltpu.SemaphoreType.DMA((2,))])
    return pl.pallas_call(
        _experts_kernel,
        out_shape=jax.ShapeDtypeStruct(xy.shape, xy.dtype),
        grid_spec=grid_spec,
        input_output_aliases={5: 0},
        compiler_params=_cparams(1),
        name="experts",
    )(tile_expert, n_used, unit_rows, next_expert, expert_parity, xy, wg, wu, wd)


def _combine_kernel(x1_ref, rc_ref, gf_ref, gfin_ref, y_ref, o_ref):
    lp1 = rc_ref[:, 0:1]
    lp2 = rc_ref[:, 1:2]
    scol = lax.broadcasted_iota(jnp.int32, (TM_ROW, XY_ROWS), 1).astype(F32)
    pick = jnp.where((scol == lp1) | (scol == lp2), 1.0, 0.0).astype(BF16)
    y = jnp.dot(pick, y_ref[...], preferred_element_type=F32)
    xf = x1_ref[...] + gf_ref[...] * y
    rs = lax.rsqrt(jnp.mean(xf * xf, axis=-1, keepdims=True) + EPS)
    o_ref[...] = xf * rs * gfin_ref[...]


def _combine(x1, rcol, gate_f, g_final, xy):
    tm = TM_ROW
    tpb = SEQ // tm
    row = lambda i: (i, 0)
    return pl.pallas_call(
        _combine_kernel,
        out_shape=jax.ShapeDtypeStruct((N_TOK, D_MODEL), F32),
        grid=(N_TOK // tm,),
        in_specs=[pl.BlockSpec((tm, D_MODEL), row),
                  pl.BlockSpec((tm, LANES), row),
                  pl.BlockSpec((None, 1, D_MODEL), lambda i: (i // tpb, 0, 0)),
                  pl.BlockSpec((1, D_MODEL), lambda i: (0, 0)),
                  pl.BlockSpec((XY_ROWS, D_MODEL), row)],
        out_specs=pl.BlockSpec((tm, D_MODEL), row),
        compiler_params=_cparams(1),
        name="combine",
    )(x1, rcol, gate_f, g_final, xy)


def _t5_bucket_np():
    qi = np.arange(BLOCK)[:, None]
    kj = np.arange(2 * BLOCK)[None, :]
    dist = qi - kj + BLOCK
    n = np.maximum(dist, 0)
    max_exact = NUM_BUCKETS // 2
    nf = np.maximum(n, 1).astype(np.float32)
    large = max_exact + (np.log(nf / np.float32(max_exact)) / np.float32(math.log(MAX_DISTANCE / max_exact))
                         * np.float32(NUM_BUCKETS - max_exact)).astype(np.int32)
    large = np.minimum(large, NUM_BUCKETS - 1)
    bucket = np.where(n < max_exact, n, large)
    band = (dist >= 0) & (dist < WINDOW)
    return bucket.astype(np.int32), band


def kernel(x, c, w_ada, b_ada, g_norm_mix, g_norm_ffn, w_in, sinks, b_forget, w_proj_swa, w_proj_fox,
           w_out, rel_bias_table, w_router_group, b_router_group, w_router_expert, b_router_expert,
           w_gate_exp, w_up_exp, w_down_exp, g_final):
    l = 0
    x2 = x.reshape(N_TOK, D_MODEL)

    c16 = jnp.concatenate([c, jnp.zeros_like(c)], axis=0)
    mod = _ada(c16, w_ada[l], b_ada[l][None, :])[:BATCH]
    shift_m, scale_m, gate_m, shift_f, scale_f, gate_f = [
        m.reshape(BATCH, 1, D_MODEL) for m in jnp.split(mod, 6, axis=-1)]

    w = w_in[l]
    o_ka, o_va, o_qb = Q_A, Q_A + KV_A, Q_A + 2 * KV_A
    o_kb, o_vb, o_f = o_qb + W_B, o_qb + 2 * W_B, o_qb + 3 * W_B
    o_g = o_f + N_HEADS_FOX

    def dup(cols):
        heads = [cols[:, h * HEAD_DIM:(h + 1) * HEAD_DIM] for h in range(N_KV_HEADS_SWA)]
        return jnp.concatenate([hd for hd in heads for _ in range(2)], axis=1)

    head_order = jnp.argsort(b_forget[l])

    def reorder_heads(cols):
        return jnp.take(cols.reshape(D_MODEL, N_HEADS_FOX, HEAD_DIM), head_order, axis=1).reshape(D_MODEL, W_B)

    w_fox = jnp.concatenate([reorder_heads(w[:, o_qb:o_kb]), reorder_heads(w[:, o_kb:o_vb]),
                             reorder_heads(w[:, o_vb:o_f])], axis=1)
    b_fox = jnp.take(b_forget[l], head_order)
    w_proj_b = jnp.take(w_proj_fox[l].reshape(N_HEADS_FOX, HEAD_DIM, D_MODEL), head_order, axis=0).reshape(W_B, D_MODEL)
    carrier = DECAY_LANES * N_HEADS_FOX
    w_f = jnp.pad(jnp.repeat(jnp.take(w[:, o_f:o_g], head_order, axis=1), DECAY_LANES, axis=1),
                  ((0, 0), (0, LANES - carrier)))
    w_main = jnp.concatenate([w[:, :Q_A], dup(w[:, o_ka:o_va]), w[:, o_va:o_qb], w_f, w_fox], axis=1).astype(BF16)
    w_g = w[:, o_g:].astype(BF16)
    b_pad = jnp.pad(jnp.repeat(b_fox, DECAY_LANES), (0, LANES - carrier))[None, :]
    lanes = np.arange(LANES)
    jmod = jnp.asarray(np.where(lanes < DECAY_LANES * N_HEADS_FOX, lanes % DECAY_LANES, 7)[None, :].astype(np.int32))
    qa, kdup, va, qb, kb, vb, gates, nrm, dq, dk, fb = _inproj(
        x2, scale_m, shift_m, g_norm_mix[l][None, :], w_main, w_g, b_pad, jmod)
    dq = dq.reshape(BATCH, SEQ, LANES)
    dk = dk.reshape(BATCH, SEQ, LANES)

    bucket, band = _t5_bucket_np()
    onehot = jnp.asarray(bucket[None] == np.arange(NUM_BUCKETS)[:, None, None], dtype=F32)
    bias = jnp.einsum("bh,bqk->hqk", rel_bias_table.astype(F32), onehot, precision=HIGHEST)
    bias = jnp.where(band[None], bias * LOG2E, NEG_INF)
    first = np.arange(2 * BLOCK)[None, None, :] < BLOCK
    bias = jnp.stack([jnp.where(first, NEG_INF, bias), bias]).reshape(2, N_KV_HEADS_SWA, -1, 2 * BLOCK)
    o_a = _swa(sinks[l].astype(F32) * LOG2E, qa, kdup, va, bias)

    o_b = _fox(_fox_first_tiles(nrm, fb), qb, kb, vb, dq, dk)

    w_r = jnp.concatenate([w_router_group[l]] + [w_router_expert[l][g] for g in range(N_GROUPS)], axis=1)
    w_r = jnp.pad(w_r, ((0, 0), (0, LANES - w_r.shape[1])))
    wr_hi = w_r.astype(BF16)
    wr_lo = (w_r - wr_hi.astype(F32)).astype(BF16)
    wr2 = jnp.concatenate([wr_hi, wr_lo], axis=1)
    b_r = jnp.concatenate([b_router_group[l], b_router_expert[l].reshape(-1)])
    b_r = jnp.pad(b_r, (0, LANES - b_r.shape[0]))[None, :]
    x1, xy, rcol, cu = _post(x2, o_a, o_b, gates, gate_m, scale_f, shift_f, g_norm_ffn[l][None, :],
                             w_proj_swa[l].astype(BF16), w_proj_b.astype(BF16), w_out[l].astype(BF16),
                             wr2, b_r)

    i32 = jnp.int32
    n_tok_tiles = N_TOK // TM_POST
    cu = cu.reshape(n_tok_tiles, 8, LANES)[:, 0, :N_EXPERTS].astype(i32)
    loc_u = jnp.cumsum(cu, axis=1) - cu
    cend = jnp.cumsum(cu, axis=0)
    cstart = cend - cu
    tot_u = cend[-1]
    tiles_e = (tot_u + UNITS_PER_TILE - 1) // UNITS_PER_TILE
    tile_end = jnp.cumsum(tiles_e)
    tile_start = tile_end - tiles_e
    r = jnp.arange(N_EXP_TILES, dtype=i32)
    tile_expert = jnp.minimum(jnp.sum((tile_end[None, :] <= r[:, None]).astype(i32), axis=1), N_EXPERTS - 1)
    sel_e = tile_expert[:, None] == jnp.arange(N_EXPERTS, dtype=i32)[None, :]
    tw = r - jnp.sum(jnp.where(sel_e, tile_start[None, :], 0), axis=1)
    tot_r = jnp.sum(jnp.where(sel_e, tot_u[None, :], 0), axis=1)
    n_used = tile_end[-1:].astype(i32)
    q = tw[:, None] * UNITS_PER_TILE + jnp.arange(UNITS_PER_TILE, dtype=i32)[None, :]

    def of_expert(tab):
        return jnp.sum(jnp.where(sel_e[:, None, :], tab[None, :, :], 0), axis=2)

    cend_r, cstart_r, loc_r = of_expert(cend), of_expert(cstart), of_expert(loc_u)
    src_tile = jnp.minimum(jnp.sum((cend_r[:, None, :] <= q[:, :, None]).astype(i32), axis=2), n_tok_tiles - 1)
    sel_t = src_tile[:, :, None] == jnp.arange(n_tok_tiles, dtype=i32)[None, None, :]
    k = (q - jnp.sum(jnp.where(sel_t, cstart_r[:, None, :], 0), axis=2)
         + jnp.sum(jnp.where(sel_t, loc_r[:, None, :], 0), axis=2))
    real_rows = src_tile * XY_ROWS + k * UNIT
    pad_rows = PAD_BASE_ROW + (tile_expert[:, None] * PAD_UNITS_PER_EXPERT + (q - tot_r[:, None])) * UNIT
    idle_row = PAD_BASE_ROW + N_EXPERTS * PAD_UNITS_PER_EXPERT * UNIT
    unit_rows = jnp.where(q < tot_r[:, None], real_rows, pad_rows)
    unit_rows = jnp.where((r < n_used)[:, None], unit_rows, idle_row).reshape(-1).astype(i32)

    eid = jnp.arange(N_EXPERTS, dtype=i32)
    used = tiles_e > 0
    later_used = (eid[None, :] > eid[:, None]) & used[None, :]
    next_e = jnp.min(jnp.where(later_used, eid[None, :], N_EXPERTS), axis=1)
    next_e = jnp.where(next_e == N_EXPERTS, -1, next_e)
    parity_e = (jnp.cumsum(used.astype(i32)) - used.astype(i32)) % 2
    next_expert = jnp.sum(jnp.where(sel_e, next_e[None, :], 0), axis=1).astype(i32)
    expert_parity = jnp.sum(jnp.where(sel_e, parity_e[None, :], 0), axis=1).astype(i32)

    xy = _experts(tile_expert.astype(i32), n_used, unit_rows, next_expert, expert_parity, xy,
                  w_gate_exp[l].reshape(N_EXPERTS, D_MODEL, D_FF_EXPERT),
                  w_up_exp[l].reshape(N_EXPERTS, D_MODEL, D_FF_EXPERT),
                  w_down_exp[l].reshape(N_EXPERTS, D_FF_EXPERT, D_MODEL))
    out = _combine(x1, rcol, gate_f, g_final[None, :], xy)
    return out.reshape(BATCH, SEQ, D_MODEL)
```

```python
import math

import numpy as np
import jax
import jax.numpy as jnp
from jax import lax
from jax.experimental import pallas as pl
from jax.experimental.pallas import tpu as pltpu

F32 = jnp.float32
BF16 = jnp.bfloat16
HIGHEST = lax.Precision.HIGHEST

D_MODEL = 1024
BATCH = 8
SEQ = 4096
N_TOK = BATCH * SEQ
N_HEADS_SWA = 8
N_KV_HEADS_SWA = 2
N_HEADS_FOX = 8
HEAD_DIM = 64
WINDOW = 128
BLOCK = 128
NUM_BUCKETS = 32
MAX_DISTANCE = 128
N_GROUPS = 4
EXPERTS_PER_GROUP = 8
N_EXPERTS = N_GROUPS * EXPERTS_PER_GROUP
D_FF_EXPERT = 256
EPS = 1e-6
NEG_INF = -1e30

Q_A = N_HEADS_SWA * HEAD_DIM
KV_A = N_KV_HEADS_SWA * HEAD_DIM
W_B = N_HEADS_FOX * HEAD_DIM
LANES = 128
QK_SCALE = HEAD_DIM ** -0.5

TM_IN = 512
TM_POST = 512
TQ_FOX = 512
TK_FOX = TQ_FOX
FOX_ROWS = 2
SWA_BLOCKS = 4
TM_EXP = 512
TM_ROW = 512
UNIT = 16
XY_UNITS = 2 * TM_POST // UNIT + N_EXPERTS
XY_ROWS = XY_UNITS * UNIT
XY_COLS = D_MODEL + LANES
UNITS_PER_TILE = TM_EXP // UNIT
N_TOK_TILES = N_TOK // TM_POST
N_EXP_TILES = N_TOK_TILES * XY_UNITS // UNITS_PER_TILE + N_EXPERTS
PAD_UNITS_PER_EXPERT = UNITS_PER_TILE - 1
PAD_BLOCKS = -(-(N_EXPERTS * PAD_UNITS_PER_EXPERT * UNIT) // XY_ROWS)
PAD_BASE_ROW = N_TOK_TILES * XY_ROWS
W1_LANES, W2_LANES, E1_LANE, E2_LANE = (4, 6), (5, 7), 8, 9
VMEM_LIMIT = 56 * 1024 * 1024

DECAY_LANES = 6
LOG2E = math.log2(math.e)
Q_SCALE_LOG2 = QK_SCALE * LOG2E
PRUNE_MARGIN = 160.0


def _cparams(n_axes):
    return pltpu.CompilerParams(dimension_semantics=("arbitrary",) * n_axes,
                                vmem_limit_bytes=VMEM_LIMIT)


def _ada_kernel(c_ref, w_ref, b_ref, o_ref):
    c = c_ref[...]
    ca = c * jax.nn.sigmoid(c)
    o_ref[...] = jnp.dot(ca.astype(BF16), w_ref[...].astype(BF16),
                         preferred_element_type=F32) + b_ref[...]


def _ada(c16, w_ada, b_ada):
    n_out = w_ada.shape[1]
    blk = 1024
    return pl.pallas_call(
        _ada_kernel,
        out_shape=jax.ShapeDtypeStruct((16, n_out), F32),
        grid=(n_out // blk,),
        in_specs=[pl.BlockSpec((16, D_MODEL), lambda j: (0, 0)),
                  pl.BlockSpec((D_MODEL, blk), lambda j: (0, j)),
                  pl.BlockSpec((1, blk), lambda j: (0, j))],
        out_specs=pl.BlockSpec((16, blk), lambda j: (0, j)),
        compiler_params=_cparams(1),
        name="ada",
    )(c16, w_ada, b_ada)


def _inproj_kernel(x_ref, sc_ref, sh_ref, g_ref, wm_ref, wg_ref, ind_ref,
                   qa_ref, kd_ref, va_ref, qb_ref, kb_ref, vb_ref, f_ref, gt_ref, nrm_ref):
    x = x_ref[...]
    rs = lax.rsqrt(jnp.mean(x * x, axis=-1, keepdims=True) + EPS)
    a = g_ref[...] * (1.0 + sc_ref[...])
    h = (x * rs * a + sh_ref[...]).astype(BF16)

    def mm(w):
        return jnp.dot(h, w, preferred_element_type=F32)

    qa_ref[...] = (mm(wm_ref[:, 0:512]) * Q_SCALE_LOG2).astype(BF16)
    kd_ref[...] = mm(wm_ref[:, 512:768]).astype(BF16)
    vf = mm(wm_ref[:, 768:1024])
    f_ref[...] = vf[:, LANES:2 * LANES]
    v = vf[:, 0:LANES]
    vr = pltpu.roll(v, HEAD_DIM, 1)
    lo = lax.broadcasted_iota(jnp.int32, v.shape, 1) < HEAD_DIM
    va_ref[:, 0:LANES] = jnp.where(lo, v, vr).astype(BF16)
    va_ref[:, LANES:2 * LANES] = jnp.where(lo, vr, v).astype(BF16)
    qb = (mm(wm_ref[:, 1024:1536]) * Q_SCALE_LOG2).astype(BF16)
    kb = mm(wm_ref[:, 1536:2048]).astype(BF16)
    qb_ref[...] = qb
    kb_ref[...] = kb
    vb_ref[...] = mm(wm_ref[:, 2048:2560]).astype(BF16)
    sq = jnp.concatenate([qb, kb], axis=1).astype(F32)
    seg = jnp.dot((sq * sq).astype(BF16), ind_ref[...], preferred_element_type=F32)
    nrm_ref[...] = jnp.broadcast_to(jnp.max(seg, axis=0, keepdims=True), nrm_ref.shape)
    gt_ref[...] = mm(wg_ref[...]).astype(BF16)


def _inproj(x2, scale_m, shift_m, g_mix, w_main, w_g):
    tm = TM_IN
    tpb = SEQ // tm
    row = lambda i: (i, 0)
    per_b = lambda i: (i // tpb, 0, 0)
    const = lambda i: (0, 0)
    outs = [(Q_A, BF16), (2 * KV_A, BF16), (2 * KV_A, BF16), (W_B, BF16), (W_B, BF16), (W_B, BF16),
            (LANES, F32), (2 * D_MODEL, BF16)]
    ind_np = np.zeros((2 * W_B, LANES), np.float32)
    ind_np[np.arange(2 * W_B), np.arange(2 * W_B) // HEAD_DIM] = 1.0
    ind = jnp.asarray(ind_np, dtype=BF16)
    n_steps = N_TOK // tm
    return pl.pallas_call(
        _inproj_kernel,
        out_shape=[jax.ShapeDtypeStruct((N_TOK, w), dt) for w, dt in outs]
        + [jax.ShapeDtypeStruct((n_steps * 8, LANES), F32)],
        grid=(n_steps,),
        in_specs=[pl.BlockSpec((tm, D_MODEL), row),
                  pl.BlockSpec((None, 1, D_MODEL), per_b),
                  pl.BlockSpec((None, 1, D_MODEL), per_b),
                  pl.BlockSpec((1, D_MODEL), const),
                  pl.BlockSpec(w_main.shape, const),
                  pl.BlockSpec(w_g.shape, const),
                  pl.BlockSpec(ind.shape, const)],
        out_specs=[pl.BlockSpec((tm, w), row) for w, _ in outs] + [pl.BlockSpec((8, LANES), row)],
        compiler_params=_cparams(1),
        name="inproj",
    )(x2, scale_m, shift_m, g_mix, w_main, w_g, ind)


def _log_sigmoid(x):
    return jnp.minimum(x, 0.0) - jnp.log1p(jnp.exp(-jnp.abs(x)))


def _cum_kernel(f_ref, b_ref, jm_ref, qa_ref, ka_ref, fb_ref):
    cum = _log_sigmoid(f_ref[...] + b_ref[...]) * LOG2E
    row = lax.broadcasted_iota(jnp.int32, cum.shape, 0)
    k = 1
    while k < SEQ:
        if k < 8:
            shifted = jnp.where(row >= k, pltpu.roll(cum, k, 0), 0.0)
        else:
            shifted = jnp.concatenate([jnp.zeros((k, LANES), F32), cum[:SEQ - k]], axis=0)
        cum = cum + shifted
        k *= 2
    jm = jm_ref[...]
    for blk in range(SEQ // LANES):
        rows = slice(blk * LANES, (blk + 1) * LANES)
        cb = cum[rows]
        carry = cb[LANES - 1:LANES]
        hi = cb.astype(BF16).astype(F32)
        r1 = cb - hi
        mid = r1.astype(BF16).astype(F32)
        lo = (r1 - mid).astype(BF16).astype(F32)
        one = jnp.ones_like(cb)
        zero = jnp.zeros_like(cb)
        qa = jnp.where(jm == 0, hi, jnp.where(jm == 1, mid, jnp.where(jm == 2, lo,
                       jnp.where(jm < DECAY_LANES, one, zero))))
        ka = jnp.where(jm == 3, -hi, jnp.where(jm == 4, -mid, jnp.where(jm == 5, -lo,
                       jnp.where(jm < 3, one, zero))))
        qa_ref[rows, :] = qa.astype(BF16)
        ka_ref[rows, :] = ka.astype(BF16)
        blocks_per_tile = TQ_FOX // LANES
        tile = blk // blocks_per_tile
        if blk % blocks_per_tile == 0:
            fb_ref[2 * tile:2 * tile + 1, :] = cb[0:1]
        if blk % blocks_per_tile == blocks_per_tile - 1:
            fb_ref[2 * tile + 1:2 * tile + 2, :] = carry


def _cum(f_pad, b_pad, jmod):
    n_tiles = SEQ // TQ_FOX
    return pl.pallas_call(
        _cum_kernel,
        out_shape=[jax.ShapeDtypeStruct((BATCH, SEQ, LANES), BF16)] * 2
        + [jax.ShapeDtypeStruct((BATCH, 2 * n_tiles, LANES), F32)],
        grid=(BATCH,),
        in_specs=[pl.BlockSpec((SEQ, LANES), lambda b: (b, 0)),
                  pl.BlockSpec((1, LANES), lambda b: (0, 0)),
                  pl.BlockSpec((1, LANES), lambda b: (0, 0))],
        out_specs=[pl.BlockSpec((None, SEQ, LANES), lambda b: (b, 0, 0))] * 2
        + [pl.BlockSpec((None, 2 * n_tiles, LANES), lambda b: (b, 0, 0))],
        compiler_params=_cparams(1),
        name="cum",
    )(f_pad, b_pad, jmod)


def _swa_block(sink_cols, q, kk, vv, bias_ref, lo):
    tiles = []
    for g in range(N_KV_HEADS_SWA):
        parts = []
        for t in range(2):
            qt = q[:, (2 * g + t) * LANES:(2 * g + t + 1) * LANES]
            zero = jnp.zeros_like(qt)
            parts.append(jnp.where(lo, qt, zero))
            parts.append(jnp.where(lo, zero, qt))
        q4 = jnp.concatenate(parts, axis=0)
        s = lax.dot_general(q4, kk[:, g * LANES:(g + 1) * LANES], (((1,), (1,)), ((), ())),
                            preferred_element_type=F32)
        s = s + bias_ref[g]
        sink = sink_cols[g]
        m = jnp.maximum(jnp.max(s, axis=-1, keepdims=True), sink)
        p = jnp.exp2(s - m)
        den = jnp.sum(p, axis=-1, keepdims=True) + jnp.exp2(sink - m)
        o = jnp.dot(p.astype(BF16), vv[:, g * LANES:(g + 1) * LANES],
                    preferred_element_type=F32) / den
        tiles.append(jnp.where(lo, o[0:BLOCK], o[BLOCK:2 * BLOCK]))
        tiles.append(jnp.where(lo, o[2 * BLOCK:3 * BLOCK], o[3 * BLOCK:4 * BLOCK]))
    return tiles


def _swa_kernel(sink_ref, q_ref, kc_ref, kp_ref, vc_ref, vp_ref, bias_first_ref, bias_ref, o_ref):
    lane = lax.broadcasted_iota(jnp.int32, (BLOCK, LANES), 1)
    lo = lane < HEAD_DIM
    grp = N_HEADS_SWA // N_KV_HEADS_SWA
    row = lax.broadcasted_iota(jnp.int32, (grp * BLOCK, 1), 0)
    sink_cols = []
    for g in range(N_KV_HEADS_SWA):
        col = jnp.full((grp * BLOCK, 1), sink_ref[g * grp + grp - 1], F32)
        for hh in range(grp - 2, -1, -1):
            col = jnp.where(row < (hh + 1) * BLOCK, sink_ref[g * grp + hh], col)
        sink_cols.append(col)
    for blk in range(SWA_BLOCKS):
        rows = slice(blk * BLOCK, (blk + 1) * BLOCK)
        if blk == 0:
            kk = jnp.concatenate([kp_ref[...], kc_ref[rows, :]], axis=0)
            vv = jnp.concatenate([vp_ref[...], vc_ref[rows, :]], axis=0)
            bias = bias_first_ref
        else:
            prev_rows = slice((blk - 1) * BLOCK, (blk + 1) * BLOCK)
            kk = kc_ref[prev_rows, :]
            vv = vc_ref[prev_rows, :]
            bias = bias_ref
        tiles = _swa_block(sink_cols, q_ref[rows, :], kk, vv, bias, lo)
        for c, tile in enumerate(tiles):
            o_ref[rows, c * LANES:(c + 1) * LANES] = tile.astype(BF16)


def _swa(sinks, qa, kdup, va, bias):
    nb = SEQ // BLOCK
    ns = nb // SWA_BLOCKS
    cur = lambda b, i, s: (b * ns + i, 0)
    prev = lambda b, i, s: (b * nb + jnp.maximum(SWA_BLOCKS * i - 1, 0), 0)
    grid_spec = pltpu.PrefetchScalarGridSpec(
        num_scalar_prefetch=1,
        grid=(BATCH, ns),
        in_specs=[pl.BlockSpec((SWA_BLOCKS * BLOCK, Q_A), cur),
                  pl.BlockSpec((SWA_BLOCKS * BLOCK, 2 * KV_A), cur),
                  pl.BlockSpec((BLOCK, 2 * KV_A), prev),
                  pl.BlockSpec((SWA_BLOCKS * BLOCK, 2 * KV_A), cur),
                  pl.BlockSpec((BLOCK, 2 * KV_A), prev),
                  pl.BlockSpec((None,) + bias.shape[1:], lambda b, i, s: (jnp.minimum(i, 1), 0, 0, 0)),
                  pl.BlockSpec((None,) + bias.shape[1:], lambda b, i, s: (1, 0, 0, 0))],
        out_specs=pl.BlockSpec((SWA_BLOCKS * BLOCK, Q_A), cur))
    return pl.pallas_call(
        _swa_kernel,
        out_shape=jax.ShapeDtypeStruct((N_TOK, Q_A), BF16),
        grid_spec=grid_spec,
        compiler_params=_cparams(2),
        name="swa",
    )(sinks, qa, kdup, kdup, va, va, bias, bias)


def _fox_kernel(js_ref, q_ref, k_ref, v_ref, qa_ref, ka_ref, o_ref,
                kaug, vaug, q2, m_sc, acc_sc, s_0, s_1, s_2, s_3):
    tq, tk = TQ_FOX, TK_FOX
    b = pl.program_id(0)
    t = pl.program_id(1)
    g = pl.program_id(2)
    first_tile = (b * pl.num_programs(1) + t) * (SEQ // tq) + g * FOX_ROWS

    @pl.when(g == 0)
    def _():
        kaug[:, 0:LANES] = k_ref[...]
        kaug[:, LANES:2 * LANES] = ka_ref[...]
        vaug[:, 0:LANES] = v_ref[...]
        vaug[:, LANES:2 * LANES] = jnp.ones((SEQ, LANES), BF16)

    lane = lax.broadcasted_iota(jnp.int32, (tq, LANES), 1)
    lo = lane < HEAD_DIM
    base = 2 * DECAY_LANES * t
    own = [(lane >= base + h * DECAY_LANES) & (lane < base + (h + 1) * DECAY_LANES) for h in range(2)]
    for rw in range(FOX_ROWS):
        rows = slice(rw * tq, (rw + 1) * tq)
        q = q_ref[rows, :]
        qa = qa_ref[rows, :]
        zero = jnp.zeros_like(q)
        q2[rw, 0, :, 0:LANES] = jnp.where(lo, q, zero)
        q2[rw, 1, :, 0:LANES] = jnp.where(lo, zero, q)
        for h in range(2):
            q2[rw, h, :, LANES:2 * LANES] = jnp.where(own[h], qa, zero)

    rr = lax.broadcasted_iota(jnp.int32, (tq, tk), 0)
    cc = lax.broadcasted_iota(jnp.int32, (tq, tk), 1)
    causal = cc <= rr
    bufs = ((s_0, s_1), (s_2, s_3))
    j_starts = [js_ref[first_tile + rw] for rw in range(FOX_ROWS)]
    for rw in range(FOX_ROWS):
        nxt = rw + 1 if rw + 1 < FOX_ROWS else None
        _fox_query_tile(rw, g * FOX_ROWS + rw, j_starts[rw], bufs[rw % 2], q2, kaug, vaug, m_sc, acc_sc,
                        causal, lo, o_ref,
                        first_scores_done=rw > 0,
                        next_first=None if nxt is None else (nxt, j_starts[nxt], bufs[nxt % 2][0]))


def _fox_query_tile(rw, i, j_start, buf_pair, q2, kaug, vaug, m_sc, acc_sc, causal, lo, o_ref,
                    first_scores_done, next_first):
    tq, tk = TQ_FOX, TK_FOX
    s_a, s_b = buf_pair
    m_sc[...] = jnp.full(m_sc.shape, NEG_INF, F32)
    acc_sc[...] = jnp.zeros(acc_sc.shape, F32)

    def scores_of(row, h, ks):
        return lax.dot_general(q2[row, h], kaug[pl.ds(ks, tk), :], (((1,), (1,)), ((), ())),
                               preferred_element_type=F32)

    def scores(h, ks):
        return scores_of(rw, h, ks)

    def consume(h, s, ks, mask):
        if mask is not None:
            s = jnp.where(mask, s, NEG_INF)
        m_prev = m_sc[h]
        m_new = jnp.maximum(m_prev, jnp.max(s, axis=-1, keepdims=True))
        alpha = jnp.exp2(m_prev - m_new)
        p = jnp.exp2(s - jnp.concatenate([m_new] * (tk // LANES), axis=1))
        pv = jnp.dot(p.astype(BF16), vaug[pl.ds(ks, tk), :], preferred_element_type=F32)
        acc_sc[h] = jnp.concatenate([alpha, alpha], axis=1) * acc_sc[h] + pv
        m_sc[h] = m_new

    def key_start(j):
        return pl.multiple_of(j * tk, tk)

    def scores_into(buf, j):
        for h in range(2):
            buf[h] = scores(h, key_start(j))

    def consume_from(buf, j, mask):
        for h in range(2):
            consume(h, buf[h], key_start(j), mask)

    n_full = i - j_start

    if not first_scores_done:
        scores_into(s_a, j_start)

    def pair(p, carry):
        j = j_start + 2 * p
        scores_into(s_b, j + 1)
        consume_from(s_a, j, None)
        scores_into(s_a, j + 2)
        consume_from(s_b, j + 1, None)
        return carry

    lax.fori_loop(0, n_full // 2, pair, 0)
    odd = lax.rem(n_full, 2) == 1

    def start_next():
        if next_first is not None:
            nrow, nj, nbuf = next_first
            for h in range(2):
                nbuf[h] = scores_of(nrow, h, key_start(nj))

    @pl.when(odd)
    def _():
        scores_into(s_b, i)
        consume_from(s_a, i - 1, None)
        start_next()
        consume_from(s_b, i, causal)

    @pl.when(jnp.logical_not(odd))
    def _():
        start_next()
        consume_from(s_a, i, causal)

    outs = [acc_sc[h, :, 0:LANES] / acc_sc[h, :, LANES:2 * LANES] for h in range(2)]
    o_ref[rw * tq:(rw + 1) * tq, :] = jnp.where(lo, outs[0], outs[1]).astype(BF16)


def _fox(j_start, qb, kb, vb, qa, ka):
    tq = TQ_FOX
    ns = SEQ // (tq * FOX_ROWS)
    n_pairs = N_HEADS_FOX // 2
    qmap = lambda b, t, i, js: (b * ns + i, t)
    kmap = lambda b, t, i, js: (b, t)
    grid_spec = pltpu.PrefetchScalarGridSpec(
        num_scalar_prefetch=1,
        grid=(BATCH, n_pairs, ns),
        in_specs=[pl.BlockSpec((FOX_ROWS * tq, LANES), qmap),
                  pl.BlockSpec((SEQ, LANES), kmap),
                  pl.BlockSpec((SEQ, LANES), kmap),
                  pl.BlockSpec((None, FOX_ROWS * tq, LANES), lambda b, t, i, js: (b, i, 0)),
                  pl.BlockSpec((None, SEQ, LANES), lambda b, t, i, js: (b, 0, 0))],
        out_specs=pl.BlockSpec((FOX_ROWS * tq, LANES), qmap),
        scratch_shapes=[pltpu.VMEM((SEQ, 2 * LANES), BF16),
                        pltpu.VMEM((SEQ, 2 * LANES), BF16),
                        pltpu.VMEM((FOX_ROWS, 2, tq, 2 * LANES), BF16),
                        pltpu.VMEM((2, tq, LANES), F32),
                        pltpu.VMEM((2, tq, 2 * LANES), F32)]
        + [pltpu.VMEM((2, tq, TK_FOX), F32)] * 4)
    return pl.pallas_call(
        _fox_kernel,
        out_shape=jax.ShapeDtypeStruct((N_TOK, W_B), BF16),
        grid_spec=grid_spec,
        compiler_params=_cparams(3),
        name="fox",
    )(j_start, qb, kb, vb, qa, ka)


def _fox_first_tiles(nrm, fb):
    n_tiles = SEQ // TQ_FOX
    nr = nrm.reshape(BATCH, SEQ // TM_IN, 8, LANES)[:, :, 0, :] * 1.02
    nr = jnp.repeat(nr, TM_IN // TQ_FOX, axis=1)
    qn = jnp.sqrt(nr[..., 0:N_HEADS_FOX])
    kn = jnp.sqrt(nr[..., N_HEADS_FOX:2 * N_HEADS_FOX])
    f_first = fb[:, 0::2, 0:DECAY_LANES * N_HEADS_FOX:DECAY_LANES]
    f_last = fb[:, 1::2, 0:DECAY_LANES * N_HEADS_FOX:DECAY_LANES]
    kn_prefix = lax.cummax(kn, axis=1)
    upper = qn[:, :, None, :] * kn_prefix[:, None, :, :] + f_first[:, :, None, :] - f_last[:, None, :, :]
    row_max_low = -(qn * kn)[:, :, None, :]
    ii = jnp.arange(n_tiles)[None, :, None, None]
    jj = jnp.arange(n_tiles)[None, None, :, None]
    skip = (upper < row_max_low - PRUNE_MARGIN) & (jj < ii)
    skip = jnp.all(skip.reshape(BATCH, n_tiles, n_tiles, N_HEADS_FOX // 2, 2), axis=-1)
    first = jnp.sum(jnp.cumprod(skip.astype(jnp.int32), axis=2), axis=2)
    return jnp.transpose(first, (0, 2, 1)).reshape(-1).astype(jnp.int32)


def _post_kernel(x_ref, oa_ref, ob_ref, gt_ref, gm_ref, sc_ref, sh_ref, g_ref,
                 wa_ref, wb_ref, wo_ref, wr2_ref, br_ref,
                 x1_ref, xy_ref, rc_ref, cu_ref, hh_prev, lg_prev):
    step = pl.program_id(0)

    @pl.when(step == 0)
    def _():
        hh_prev[...] = jnp.zeros(hh_prev.shape, BF16)
        lg_prev[...] = jnp.zeros(lg_prev.shape, F32)

    @pl.when(step <= N_TOK_TILES)
    def _():
        hh_p = hh_prev[...]
        lg_p = lg_prev[...]
        hh, logits = _post_mix(x_ref, oa_ref, ob_ref, gt_ref, gm_ref, sc_ref, sh_ref, g_ref,
                               wa_ref, wb_ref, wo_ref, wr2_ref, br_ref, x1_ref)
        _post_route(hh_p, lg_p, xy_ref, rc_ref, cu_ref)
        hh_prev[...] = hh
        lg_prev[...] = logits

    @pl.when(step > N_TOK_TILES)
    def _():
        xy_ref[...] = jnp.zeros(xy_ref.shape, BF16)


def _post_mix(x_ref, oa_ref, ob_ref, gt_ref, gm_ref, sc_ref, sh_ref, g_ref,
              wa_ref, wb_ref, wo_ref, wr2_ref, br_ref, x1_ref):
    pa = jnp.dot(oa_ref[...], wa_ref[...], preferred_element_type=F32)
    pb = jnp.dot(ob_ref[...], wb_ref[...], preferred_element_type=F32)
    ga = jax.nn.sigmoid(gt_ref[:, 0:D_MODEL].astype(F32))
    gb = jax.nn.sigmoid(gt_ref[:, D_MODEL:2 * D_MODEL].astype(F32))
    merged = (ga * pa + gb * pb).astype(BF16)
    y = jnp.dot(merged, wo_ref[...], preferred_element_type=F32)
    x1 = x_ref[...] + gm_ref[...] * y
    x1_ref[...] = x1

    rs = lax.rsqrt(jnp.mean(x1 * x1, axis=-1, keepdims=True) + EPS)
    a = g_ref[...] * (1.0 + sc_ref[...])
    h2 = x1 * rs * a + sh_ref[...]

    hh = h2.astype(BF16)
    hl = (h2 - hh.astype(F32)).astype(BF16)
    hi_both = jnp.dot(hh, wr2_ref[...], preferred_element_type=F32)
    logits = (hi_both[:, 0:LANES] + hi_both[:, LANES:2 * LANES]
              + jnp.dot(hl, wr2_ref[:, 0:LANES], preferred_element_type=F32)
              + br_ref[...])
    return hh, logits


def _post_route(hh, logits, xy_ref, rc_ref, cu_ref):
    tm = TM_POST
    lane = lax.broadcasted_iota(jnp.int32, (tm, LANES), 1).astype(F32)
    big = float(LANES)
    gl = jnp.where(lane < N_GROUPS, logits, -jnp.inf)
    gmax = jnp.max(gl, axis=-1, keepdims=True)
    gi = jnp.min(jnp.where(gl == gmax, lane, big), axis=-1, keepdims=True)
    gsum = jnp.sum(jnp.exp(gl - gmax), axis=-1, keepdims=True)
    gp = 1.0 / gsum
    e_lo = N_GROUPS + EXPERTS_PER_GROUP * gi
    el = jnp.where((lane >= e_lo) & (lane < e_lo + EXPERTS_PER_GROUP), logits, -jnp.inf)
    v1 = jnp.max(el, axis=-1, keepdims=True)
    i1 = jnp.min(jnp.where(el == v1, lane, big), axis=-1, keepdims=True)
    el2 = jnp.where(lane == i1, -jnp.inf, el)
    v2 = jnp.max(el2, axis=-1, keepdims=True)
    i2 = jnp.min(jnp.where(el2 == v2, lane, big), axis=-1, keepdims=True)
    e21 = jnp.exp(v2 - v1)
    w1 = gp / (1.0 + e21)
    w2 = gp * e21 / (1.0 + e21)
    e1 = i1 - N_GROUPS
    e2 = i2 - N_GROUPS

    oh = jnp.where((lane == e1) | (lane == e2), 1.0, 0.0)
    cnt_u = jnp.floor((jnp.sum(oh, axis=0, keepdims=True) + (UNIT - 1)) * (1.0 / UNIT))
    r128 = lax.broadcasted_iota(jnp.int32, (LANES, LANES), 0)
    c128 = lax.broadcasted_iota(jnp.int32, (LANES, LANES), 1)
    before_lane = jnp.where(r128 < c128, 1.0, 0.0).astype(BF16)
    loc_u = jnp.dot(jnp.broadcast_to(cnt_u, (8, LANES)).astype(BF16), before_lane,
                    preferred_element_type=F32)
    trow = lax.broadcasted_iota(jnp.int32, (tm, LANES), 0)
    seen = oh
    k = 1
    while k < tm:
        if k < 8:
            shifted = jnp.where(trow >= k, pltpu.roll(seen, k, 0), 0.0)
        else:
            shifted = jnp.concatenate([jnp.zeros((k, LANES), F32), seen[:tm - k]], axis=0)
        seen = seen + shifted
        k *= 2
    pos_e = (seen - oh) + loc_u[0:1] * UNIT
    lp1 = jnp.sum(jnp.where(lane == e1, pos_e, 0.0), axis=-1, keepdims=True)
    lp2 = jnp.sum(jnp.where(lane == e2, pos_e, 0.0), axis=-1, keepdims=True)

    def to_row(col):
        return jnp.transpose(jnp.broadcast_to(col, (tm, LANES)))[0:1]

    srow = lax.broadcasted_iota(jnp.int32, (XY_ROWS, tm), 0).astype(F32)
    pm1 = jnp.where(srow == to_row(lp1), 1.0, 0.0).astype(BF16)
    pm2 = jnp.where(srow == to_row(lp2), 1.0, 0.0).astype(BF16)
    w1h = w1.astype(BF16).astype(F32)
    w2h = w2.astype(BF16).astype(F32)
    side = jnp.where(lane == W1_LANES[0], w1h, jnp.where(lane == W1_LANES[1], w1 - w1h,
           jnp.where(lane == W2_LANES[0], w2h, jnp.where(lane == W2_LANES[1], w2 - w2h,
           jnp.where(lane == E1_LANE, e1, jnp.where(lane == E2_LANE, e2, 0.0))))))
    tok = jnp.concatenate([hh, side.astype(BF16)], axis=1)
    xy_ref[...] = jnp.dot(pm1 + pm2, tok, preferred_element_type=F32).astype(BF16)

    cu_ref[...] = jnp.broadcast_to(cnt_u, cu_ref.shape)
    rc_ref[...] = jnp.where(lane == 0, lp1, jnp.where(lane == 1, lp2, 0.0))


def _post(x2, oa, ob, gates, gate_m, scale_f, shift_f, g_ffn, wa, wb, wo, wr2, b_r):
    tm = TM_POST
    tpb = SEQ // tm
    n_steps = N_TOK_TILES
    row = lambda i: (jnp.minimum(i, n_steps - 1), 0)
    per_b = lambda i: (jnp.minimum(i, n_steps - 1) // tpb, 0, 0)
    routed = lambda i: (jnp.clip(i - 1, 0, n_steps - 1), 0)
    const = lambda i: (0, 0)
    return pl.pallas_call(
        _post_kernel,
        out_shape=[jax.ShapeDtypeStruct((N_TOK, D_MODEL), F32),
                   jax.ShapeDtypeStruct(((n_steps + PAD_BLOCKS) * XY_ROWS, XY_COLS), BF16),
                   jax.ShapeDtypeStruct((N_TOK, LANES), F32),
                   jax.ShapeDtypeStruct((n_steps * 8, LANES), F32)],
        grid=(n_steps + 1 + PAD_BLOCKS,),
        in_specs=[pl.BlockSpec((tm, D_MODEL), row),
                  pl.BlockSpec((tm, Q_A), row),
                  pl.BlockSpec((tm, W_B), row),
                  pl.BlockSpec((tm, 2 * D_MODEL), row),
                  pl.BlockSpec((None, 1, D_MODEL), per_b),
                  pl.BlockSpec((None, 1, D_MODEL), per_b),
                  pl.BlockSpec((None, 1, D_MODEL), per_b),
                  pl.BlockSpec((1, D_MODEL), const),
                  pl.BlockSpec(wa.shape, const),
                  pl.BlockSpec(wb.shape, const),
                  pl.BlockSpec(wo.shape, const),
                  pl.BlockSpec(wr2.shape, const),
                  pl.BlockSpec((1, LANES), const)],
        out_specs=[pl.BlockSpec((tm, D_MODEL), row),
                   pl.BlockSpec((XY_ROWS, XY_COLS), lambda i: (jnp.maximum(i - 1, 0), 0)),
                   pl.BlockSpec((tm, LANES), routed),
                   pl.BlockSpec((8, LANES), routed)],
        scratch_shapes=[pltpu.VMEM((tm, D_MODEL), BF16),
                        pltpu.VMEM((tm, LANES), F32)],
        compiler_params=_cparams(1),
        name="post",
    )(x2, oa, ob, gates, gate_m, scale_f, shift_f, g_ffn, wa, wb, wo, wr2, b_r)


def _experts_kernel(te_ref, nu_ref, ur_ref, ne_ref, ep_ref, xy_in, wg_hbm, wu_hbm, wd_hbm, xy_out,
                    xbuf, ybuf, wg_s, wu_s, wd_s, wg_f, wu_f, wd_f, gsem, ssem, wsem):
    del xy_in
    r = pl.program_id(0)
    last = pl.num_programs(0) - 1
    n_used = nu_ref[0]
    slot = lax.rem(r, 2)

    def unit_row(step, s):
        return pl.multiple_of(ur_ref[step * UNITS_PER_TILE + s], UNIT)

    def start_gathers(step, sl):
        for s in range(UNITS_PER_TILE):
            pltpu.make_async_copy(xy_out.at[pl.ds(unit_row(step, s), UNIT), :],
                                  xbuf.at[sl, pl.ds(s * UNIT, UNIT), :], gsem.at[sl]).start()

    def wait_gathers(sl):
        pltpu.make_async_copy(xy_out.at[pl.ds(0, TM_EXP), :], xbuf.at[sl], gsem.at[sl]).wait()

    def start_scatters(step, sl):
        for s in range(UNITS_PER_TILE):
            pltpu.make_async_copy(ybuf.at[sl, pl.ds(s * UNIT, UNIT), :],
                                  xy_out.at[pl.ds(unit_row(step, s), UNIT), pl.ds(0, D_MODEL)],
                                  ssem.at[sl]).start()

    def wait_scatters(sl):
        pltpu.make_async_copy(ybuf.at[sl], xy_out.at[pl.ds(0, TM_EXP), pl.ds(0, D_MODEL)], ssem.at[sl]).wait()

    @pl.when(r == 0)
    def _():
        start_gathers(0, 0)

    @pl.when(jnp.logical_and(r < n_used, r >= 2))
    def _():
        wait_scatters(slot)

    def weight_copies(e, p):
        return [pltpu.make_async_copy(src.at[e], dst.at[p], wsem.at[p])
                for src, dst in ((wg_hbm, wg_f), (wu_hbm, wu_f), (wd_hbm, wd_f))]

    @pl.when(jnp.logical_and(r < n_used,
                             jnp.logical_or(r == 0, te_ref[r] != te_ref[jnp.maximum(r - 1, 0)])))
    def _():
        e = te_ref[r]
        p = ep_ref[r]

        @pl.when(r == 0)
        def _():
            for cp in weight_copies(e, p):
                cp.start()

        for cp in weight_copies(e, p):
            cp.wait()
        wg_s[...] = wg_f[p].astype(BF16)
        wu_s[...] = wu_f[p].astype(BF16)
        wd_s[...] = wd_f[p].astype(BF16)

        @pl.when(ne_ref[r] >= 0)
        def _():
            for cp in weight_copies(ne_ref[r], 1 - p):
                cp.start()

    @pl.when(r < n_used)
    def _():
        wait_gathers(slot)
        start_gathers(jnp.minimum(r + 1, last), 1 - slot)
        x = xbuf[slot, :, 0:D_MODEL]
        side = xbuf[slot, :, D_MODEL:XY_COLS].astype(F32)
        lane = lax.broadcasted_iota(jnp.int32, side.shape, 1)

        def lanes_sum(a, b):
            return jnp.sum(jnp.where((lane == a) | (lane == b), side, 0.0), axis=-1, keepdims=True)

        is_slot1 = lanes_sum(E1_LANE, E1_LANE) == te_ref[r].astype(F32)
        wrow = jnp.where(is_slot1, lanes_sum(*W1_LANES), lanes_sum(*W2_LANES))
        a = jnp.dot(x, wg_s[...], preferred_element_type=F32)
        u = jnp.dot(x, wu_s[...], preferred_element_type=F32)
        hid = (a * jax.nn.sigmoid(a) * u * wrow).astype(BF16)
        ybuf[slot] = jnp.dot(hid, wd_s[...], preferred_element_type=F32).astype(BF16)
        start_scatters(r, slot)

    @pl.when(r == n_used - 1)
    def _():
        wait_gathers(1 - slot)
        wait_scatters(slot)

        @pl.when(r >= 1)
        def _():
            wait_scatters(1 - slot)


def _experts(tile_expert, n_used, unit_rows, next_expert, expert_parity, xy, wg, wu, wd):
    grid_spec = pltpu.PrefetchScalarGridSpec(
        num_scalar_prefetch=5,
        grid=(N_EXP_TILES,),
        in_specs=[pl.BlockSpec(memory_space=pl.ANY),
                  pl.BlockSpec(memory_space=pl.ANY),
                  pl.BlockSpec(memory_space=pl.ANY),
                  pl.BlockSpec(memory_space=pl.ANY)],
        out_specs=pl.BlockSpec(memory_space=pl.ANY),
        scratch_shapes=[pltpu.VMEM((2, TM_EXP, XY_COLS), BF16),
                        pltpu.VMEM((2, TM_EXP, D_MODEL), BF16),
                        pltpu.VMEM((D_MODEL, D_FF_EXPERT), BF16),
                        pltpu.VMEM((D_MODEL, D_FF_EXPERT), BF16),
                        pltpu.VMEM((D_FF_EXPERT, D_MODEL), BF16),
                        pltpu.VMEM((2, D_MODEL, D_FF_EXPERT), F32),
                        pltpu.VMEM((2, D_MODEL, D_FF_EXPERT), F32),
                        pltpu.VMEM((2, D_FF_EXPERT, D_MODEL), F32),
                        pltpu.SemaphoreType.DMA((2,)),
                        pltpu.SemaphoreType.DMA((2,)),
                        pltpu.SemaphoreType.DMA((2,))])
    return pl.pallas_call(
        _experts_kernel,
        out_shape=jax.ShapeDtypeStruct(xy.shape, xy.dtype),
        grid_spec=grid_spec,
        input_output_aliases={5: 0},
        compiler_params=_cparams(1),
        name="experts",
    )(tile_expert, n_used, unit_rows, next_expert, expert_parity, xy, wg, wu, wd)


def _combine_kernel(x1_ref, rc_ref, gf_ref, gfin_ref, y_ref, o_ref):
    lp1 = rc_ref[:, 0:1]
    lp2 = rc_ref[:, 1:2]
    scol = lax.broadcasted_iota(jnp.int32, (TM_ROW, XY_ROWS), 1).astype(F32)
    pick = jnp.where((scol == lp1) | (scol == lp2), 1.0, 0.0).astype(BF16)
    y = jnp.dot(pick, y_ref[...], preferred_element_type=F32)
    xf = x1_ref[...] + gf_ref[...] * y
    rs = lax.rsqrt(jnp.mean(xf * xf, axis=-1, keepdims=True) + EPS)
    o_ref[...] = xf * rs * gfin_ref[...]


def _combine(x1, rcol, gate_f, g_final, xy):
    tm = TM_ROW
    tpb = SEQ // tm
    row = lambda i: (i, 0)
    return pl.pallas_call(
        _combine_kernel,
        out_shape=jax.ShapeDtypeStruct((N_TOK, D_MODEL), F32),
        grid=(N_TOK // tm,),
        in_specs=[pl.BlockSpec((tm, D_MODEL), row),
                  pl.BlockSpec((tm, LANES), row),
                  pl.BlockSpec((None, 1, D_MODEL), lambda i: (i // tpb, 0, 0)),
                  pl.BlockSpec((1, D_MODEL), lambda i: (0, 0)),
                  pl.BlockSpec((XY_ROWS, D_MODEL), row)],
        out_specs=pl.BlockSpec((tm, D_MODEL), row),
        compiler_params=_cparams(1),
        name="combine",
    )(x1, rcol, gate_f, g_final, xy)


def _t5_bucket_np():
    qi = np.arange(BLOCK)[:, None]
    kj = np.arange(2 * BLOCK)[None, :]
    dist = qi - kj + BLOCK
    n = np.maximum(dist, 0)
    max_exact = NUM_BUCKETS // 2
    nf = np.maximum(n, 1).astype(np.float32)
    large = max_exact + (np.log(nf / np.float32(max_exact)) / np.float32(math.log(MAX_DISTANCE / max_exact))
                         * np.float32(NUM_BUCKETS - max_exact)).astype(np.int32)
    large = np.minimum(large, NUM_BUCKETS - 1)
    bucket = np.where(n < max_exact, n, large)
    band = (dist >= 0) & (dist < WINDOW)
    return bucket.astype(np.int32), band


def kernel(x, c, w_ada, b_ada, g_norm_mix, g_norm_ffn, w_in, sinks, b_forget, w_proj_swa, w_proj_fox,
           w_out, rel_bias_table, w_router_group, b_router_group, w_router_expert, b_router_expert,
           w_gate_exp, w_up_exp, w_down_exp, g_final):
    l = 0
    x2 = x.reshape(N_TOK, D_MODEL)

    c16 = jnp.concatenate([c, jnp.zeros_like(c)], axis=0)
    mod = _ada(c16, w_ada[l], b_ada[l][None, :])[:BATCH]
    shift_m, scale_m, gate_m, shift_f, scale_f, gate_f = [
        m.reshape(BATCH, 1, D_MODEL) for m in jnp.split(mod, 6, axis=-1)]

    w = w_in[l]
    o_ka, o_va, o_qb = Q_A, Q_A + KV_A, Q_A + 2 * KV_A
    o_kb, o_vb, o_f = o_qb + W_B, o_qb + 2 * W_B, o_qb + 3 * W_B
    o_g = o_f + N_HEADS_FOX

    def dup(cols):
        heads = [cols[:, h * HEAD_DIM:(h + 1) * HEAD_DIM] for h in range(N_KV_HEADS_SWA)]
        return jnp.concatenate([hd for hd in heads for _ in range(2)], axis=1)

    head_order = jnp.argsort(b_forget[l])

    def reorder_heads(cols):
        return jnp.take(cols.reshape(D_MODEL, N_HEADS_FOX, HEAD_DIM), head_order, axis=1).reshape(D_MODEL, W_B)

    w_fox = jnp.concatenate([reorder_heads(w[:, o_qb:o_kb]), reorder_heads(w[:, o_kb:o_vb]),
                             reorder_heads(w[:, o_vb:o_f])], axis=1)
    b_fox = jnp.take(b_forget[l], head_order)
    w_proj_b = jnp.take(w_proj_fox[l].reshape(N_HEADS_FOX, HEAD_DIM, D_MODEL), head_order, axis=0).reshape(W_B, D_MODEL)
    carrier = DECAY_LANES * N_HEADS_FOX
    w_f = jnp.pad(jnp.repeat(jnp.take(w[:, o_f:o_g], head_order, axis=1), DECAY_LANES, axis=1),
                  ((0, 0), (0, LANES - carrier)))
    w_main = jnp.concatenate([w[:, :Q_A], dup(w[:, o_ka:o_va]), w[:, o_va:o_qb], w_f, w_fox], axis=1).astype(BF16)
    w_g = w[:, o_g:].astype(BF16)
    qa, kdup, va, qb, kb, vb, f_pad, gates, nrm = _inproj(
        x2, scale_m, shift_m, g_norm_mix[l][None, :], w_main, w_g)

    b_pad = jnp.pad(jnp.repeat(b_fox, DECAY_LANES), (0, LANES - carrier))[None, :]
    lanes = np.arange(LANES)
    jmod = jnp.asarray(np.where(lanes < DECAY_LANES * N_HEADS_FOX, lanes % DECAY_LANES, 7)[None, :].astype(np.int32))
    dq, dk, fb = _cum(f_pad, b_pad, jmod)

    bucket, band = _t5_bucket_np()
    onehot = jnp.asarray(bucket[None] == np.arange(NUM_BUCKETS)[:, None, None], dtype=F32)
    bias = jnp.einsum("bh,bqk->hqk", rel_bias_table.astype(F32), onehot, precision=HIGHEST)
    bias = jnp.where(band[None], bias * LOG2E, NEG_INF)
    first = np.arange(2 * BLOCK)[None, None, :] < BLOCK
    bias = jnp.stack([jnp.where(first, NEG_INF, bias), bias]).reshape(2, N_KV_HEADS_SWA, -1, 2 * BLOCK)
    o_a = _swa(sinks[l].astype(F32) * LOG2E, qa, kdup, va, bias)

    o_b = _fox(_fox_first_tiles(nrm, fb), qb, kb, vb, dq, dk)

    w_r = jnp.concatenate([w_router_group[l]] + [w_router_expert[l][g] for g in range(N_GROUPS)], axis=1)
    w_r = jnp.pad(w_r, ((0, 0), (0, LANES - w_r.shape[1])))
    wr_hi = w_r.astype(BF16)
    wr_lo = (w_r - wr_hi.astype(F32)).astype(BF16)
    wr2 = jnp.concatenate([wr_hi, wr_lo], axis=1)
    b_r = jnp.concatenate([b_router_group[l], b_router_expert[l].reshape(-1)])
    b_r = jnp.pad(b_r, (0, LANES - b_r.shape[0]))[None, :]
    x1, xy, rcol, cu = _post(x2, o_a, o_b, gates, gate_m, scale_f, shift_f, g_norm_ffn[l][None, :],
                             w_proj_swa[l].astype(BF16), w_proj_b.astype(BF16), w_out[l].astype(BF16),
                             wr2, b_r)

    i32 = jnp.int32
    n_tok_tiles = N_TOK // TM_POST
    cu = cu.reshape(n_tok_tiles, 8, LANES)[:, 0, :N_EXPERTS].astype(i32)
    loc_u = jnp.cumsum(cu, axis=1) - cu
    cend = jnp.cumsum(cu, axis=0)
    cstart = cend - cu
    tot_u = cend[-1]
    tiles_e = (tot_u + UNITS_PER_TILE - 1) // UNITS_PER_TILE
    tile_end = jnp.cumsum(tiles_e)
    tile_start = tile_end - tiles_e
    r = jnp.arange(N_EXP_TILES, dtype=i32)
    tile_expert = jnp.minimum(jnp.sum((tile_end[None, :] <= r[:, None]).astype(i32), axis=1), N_EXPERTS - 1)
    sel_e = tile_expert[:, None] == jnp.arange(N_EXPERTS, dtype=i32)[None, :]
    tw = r - jnp.sum(jnp.where(sel_e, tile_start[None, :], 0), axis=1)
    tot_r = jnp.sum(jnp.where(sel_e, tot_u[None, :], 0), axis=1)
    n_used = tile_end[-1:].astype(i32)
    q = tw[:, None] * UNITS_PER_TILE + jnp.arange(UNITS_PER_TILE, dtype=i32)[None, :]

    def of_expert(tab):
        return jnp.sum(jnp.where(sel_e[:, None, :], tab[None, :, :], 0), axis=2)

    cend_r, cstart_r, loc_r = of_expert(cend), of_expert(cstart), of_expert(loc_u)
    src_tile = jnp.minimum(jnp.sum((cend_r[:, None, :] <= q[:, :, None]).astype(i32), axis=2), n_tok_tiles - 1)
    sel_t = src_tile[:, :, None] == jnp.arange(n_tok_tiles, dtype=i32)[None, None, :]
    k = (q - jnp.sum(jnp.where(sel_t, cstart_r[:, None, :], 0), axis=2)
         + jnp.sum(jnp.where(sel_t, loc_r[:, None, :], 0), axis=2))
    real_rows = src_tile * XY_ROWS + k * UNIT
    pad_rows = PAD_BASE_ROW + (tile_expert[:, None] * PAD_UNITS_PER_EXPERT + (q - tot_r[:, None])) * UNIT
    idle_row = PAD_BASE_ROW + N_EXPERTS * PAD_UNITS_PER_EXPERT * UNIT
    unit_rows = jnp.where(q < tot_r[:, None], real_rows, pad_rows)
    unit_rows = jnp.where((r < n_used)[:, None], unit_rows, idle_row).reshape(-1).astype(i32)

    eid = jnp.arange(N_EXPERTS, dtype=i32)
    used = tiles_e > 0
    later_used = (eid[None, :] > eid[:, None]) & used[None, :]
    next_e = jnp.min(jnp.where(later_used, eid[None, :], N_EXPERTS), axis=1)
    next_e = jnp.where(next_e == N_EXPERTS, -1, next_e)
    parity_e = (jnp.cumsum(used.astype(i32)) - used.astype(i32)) % 2
    next_expert = jnp.sum(jnp.where(sel_e, next_e[None, :], 0), axis=1).astype(i32)
    expert_parity = jnp.sum(jnp.where(sel_e, parity_e[None, :], 0), axis=1).astype(i32)

    xy = _experts(tile_expert.astype(i32), n_used, unit_rows, next_expert, expert_parity, xy,
                  w_gate_exp[l].reshape(N_EXPERTS, D_MODEL, D_FF_EXPERT),
                  w_up_exp[l].reshape(N_EXPERTS, D_MODEL, D_FF_EXPERT),
                  w_down_exp[l].reshape(N_EXPERTS, D_FF_EXPERT, D_MODEL))
    out = _combine(x1, rcol, gate_f, g_final[None, :], xy)
    return out.reshape(BATCH, SEQ, D_MODEL)
```

```python
import math

import numpy as np
import jax
import jax.numpy as jnp
from jax import lax
from jax.experimental import pallas as pl
from jax.experimental.pallas import tpu as pltpu

F32 = jnp.float32
BF16 = jnp.bfloat16
HIGHEST = lax.Precision.HIGHEST

D_MODEL = 1024
BATCH = 8
SEQ = 4096
N_TOK = BATCH * SEQ
N_HEADS_SWA = 8
N_KV_HEADS_SWA = 2
N_HEADS_FOX = 8
HEAD_DIM = 64
WINDOW = 128
BLOCK = 128
NUM_BUCKETS = 32
MAX_DISTANCE = 128
N_GROUPS = 4
EXPERTS_PER_GROUP = 8
N_EXPERTS = N_GROUPS * EXPERTS_PER_GROUP
D_FF_EXPERT = 256
EPS = 1e-6
NEG_INF = -1e30

Q_A = N_HEADS_SWA * HEAD_DIM
KV_A = N_KV_HEADS_SWA * HEAD_DIM
W_B = N_HEADS_FOX * HEAD_DIM
LANES = 128
QK_SCALE = HEAD_DIM ** -0.5

TM_IN = 512
TM_POST = 512
TQ_FOX = 512
TK_FOX = TQ_FOX
FOX_ROWS = 2
SWA_BLOCKS = 4
TM_EXP = 512
TM_ROW = 512
COMBINE_TILES = 2
UNIT = 16
XY_UNITS = 2 * TM_POST // UNIT + N_EXPERTS
XY_ROWS = XY_UNITS * UNIT
XY_COLS = D_MODEL + LANES
UNITS_PER_TILE = TM_EXP // UNIT
N_TOK_TILES = N_TOK // TM_POST
N_EXP_TILES = N_TOK_TILES * XY_UNITS // UNITS_PER_TILE + N_EXPERTS
PAD_UNITS_PER_EXPERT = UNITS_PER_TILE - 1
PAD_BLOCKS = -(-(N_EXPERTS * PAD_UNITS_PER_EXPERT * UNIT) // XY_ROWS)
PAD_BASE_ROW = N_TOK_TILES * XY_ROWS
W1_LANES, W2_LANES, E1_LANE, E2_LANE = (4, 6), (5, 7), 8, 9
VMEM_LIMIT = 56 * 1024 * 1024

DECAY_LANES = 6
LOG2E = math.log2(math.e)
Q_SCALE_LOG2 = QK_SCALE * LOG2E
PRUNE_MARGIN = 160.0


def _cparams(n_axes):
    return pltpu.CompilerParams(dimension_semantics=("arbitrary",) * n_axes,
                                vmem_limit_bytes=VMEM_LIMIT)


def _ada_kernel(c_ref, w_ref, b_ref, o_ref):
    c = c_ref[...]
    ca = c * jax.nn.sigmoid(c)
    o_ref[...] = jnp.dot(ca.astype(BF16), w_ref[...].astype(BF16),
                         preferred_element_type=F32) + b_ref[...]


def _ada(c16, w_ada, b_ada):
    n_out = w_ada.shape[1]
    blk = 1024
    return pl.pallas_call(
        _ada_kernel,
        out_shape=jax.ShapeDtypeStruct((16, n_out), F32),
        grid=(n_out // blk,),
        in_specs=[pl.BlockSpec((16, D_MODEL), lambda j: (0, 0)),
                  pl.BlockSpec((D_MODEL, blk), lambda j: (0, j)),
                  pl.BlockSpec((1, blk), lambda j: (0, j))],
        out_specs=pl.BlockSpec((16, blk), lambda j: (0, j)),
        compiler_params=_cparams(1),
        name="ada",
    )(c16, w_ada, b_ada)


def _inproj_kernel(x_ref, sc_ref, sh_ref, g_ref, wm_ref, wg_ref, ind_ref,
                   qa_ref, kd_ref, va_ref, qb_ref, kb_ref, vb_ref, f_ref, gt_ref, nrm_ref):
    x = x_ref[...]
    rs = lax.rsqrt(jnp.mean(x * x, axis=-1, keepdims=True) + EPS)
    a = g_ref[...] * (1.0 + sc_ref[...])
    h = (x * rs * a + sh_ref[...]).astype(BF16)

    def mm(w):
        return jnp.dot(h, w, preferred_element_type=F32)

    qa_ref[...] = (mm(wm_ref[:, 0:512]) * Q_SCALE_LOG2).astype(BF16)
    kd_ref[...] = mm(wm_ref[:, 512:768]).astype(BF16)
    vf = mm(wm_ref[:, 768:1024])
    f_ref[...] = vf[:, LANES:2 * LANES]
    v = vf[:, 0:LANES]
    vr = pltpu.roll(v, HEAD_DIM, 1)
    lo = lax.broadcasted_iota(jnp.int32, v.shape, 1) < HEAD_DIM
    va_ref[:, 0:LANES] = jnp.where(lo, v, vr).astype(BF16)
    va_ref[:, LANES:2 * LANES] = jnp.where(lo, vr, v).astype(BF16)
    qb = (mm(wm_ref[:, 1024:1536]) * Q_SCALE_LOG2).astype(BF16)
    kb = mm(wm_ref[:, 1536:2048]).astype(BF16)
    qb_ref[...] = qb
    kb_ref[...] = kb
    vb_ref[...] = mm(wm_ref[:, 2048:2560]).astype(BF16)
    sq = jnp.concatenate([qb, kb], axis=1).astype(F32)
    seg = jnp.dot((sq * sq).astype(BF16), ind_ref[...], preferred_element_type=F32)
    nrm_ref[...] = jnp.broadcast_to(jnp.max(seg, axis=0, keepdims=True), nrm_ref.shape)
    gt_ref[...] = mm(wg_ref[...]).astype(BF16)


def _inproj(x2, scale_m, shift_m, g_mix, w_main, w_g):
    tm = TM_IN
    tpb = SEQ // tm
    row = lambda i: (i, 0)
    per_b = lambda i: (i // tpb, 0, 0)
    const = lambda i: (0, 0)
    outs = [(Q_A, BF16), (2 * KV_A, BF16), (2 * KV_A, BF16), (W_B, BF16), (W_B, BF16), (W_B, BF16),
            (LANES, F32), (2 * D_MODEL, BF16)]
    ind_np = np.zeros((2 * W_B, LANES), np.float32)
    ind_np[np.arange(2 * W_B), np.arange(2 * W_B) // HEAD_DIM] = 1.0
    ind = jnp.asarray(ind_np, dtype=BF16)
    n_steps = N_TOK // tm
    return pl.pallas_call(
        _inproj_kernel,
        out_shape=[jax.ShapeDtypeStruct((N_TOK, w), dt) for w, dt in outs]
        + [jax.ShapeDtypeStruct((n_steps * 8, LANES), F32)],
        grid=(n_steps,),
        in_specs=[pl.BlockSpec((tm, D_MODEL), row),
                  pl.BlockSpec((None, 1, D_MODEL), per_b),
                  pl.BlockSpec((None, 1, D_MODEL), per_b),
                  pl.BlockSpec((1, D_MODEL), const),
                  pl.BlockSpec(w_main.shape, const),
                  pl.BlockSpec(w_g.shape, const),
                  pl.BlockSpec(ind.shape, const)],
        out_specs=[pl.BlockSpec((tm, w), row) for w, _ in outs] + [pl.BlockSpec((8, LANES), row)],
        compiler_params=_cparams(1),
        name="inproj",
    )(x2, scale_m, shift_m, g_mix, w_main, w_g, ind)


def _log_sigmoid(x):
    return jnp.minimum(x, 0.0) - jnp.log1p(jnp.exp(-jnp.abs(x)))


def _cum_kernel(f_ref, b_ref, jm_ref, qa_ref, ka_ref, fb_ref):
    cum = _log_sigmoid(f_ref[...] + b_ref[...]) * LOG2E
    row = lax.broadcasted_iota(jnp.int32, cum.shape, 0)
    k = 1
    while k < SEQ:
        if k < 8:
            shifted = jnp.where(row >= k, pltpu.roll(cum, k, 0), 0.0)
        else:
            shifted = jnp.concatenate([jnp.zeros((k, LANES), F32), cum[:SEQ - k]], axis=0)
        cum = cum + shifted
        k *= 2
    jm = jm_ref[...]
    for blk in range(SEQ // LANES):
        rows = slice(blk * LANES, (blk + 1) * LANES)
        cb = cum[rows]
        carry = cb[LANES - 1:LANES]
        hi = cb.astype(BF16).astype(F32)
        r1 = cb - hi
        mid = r1.astype(BF16).astype(F32)
        lo = (r1 - mid).astype(BF16).astype(F32)
        one = jnp.ones_like(cb)
        zero = jnp.zeros_like(cb)
        qa = jnp.where(jm == 0, hi, jnp.where(jm == 1, mid, jnp.where(jm == 2, lo,
                       jnp.where(jm < DECAY_LANES, one, zero))))
        ka = jnp.where(jm == 3, -hi, jnp.where(jm == 4, -mid, jnp.where(jm == 5, -lo,
                       jnp.where(jm < 3, one, zero))))
        qa_ref[rows, :] = qa.astype(BF16)
        ka_ref[rows, :] = ka.astype(BF16)
        blocks_per_tile = TQ_FOX // LANES
        tile = blk // blocks_per_tile
        if blk % blocks_per_tile == 0:
            fb_ref[2 * tile:2 * tile + 1, :] = cb[0:1]
        if blk % blocks_per_tile == blocks_per_tile - 1:
            fb_ref[2 * tile + 1:2 * tile + 2, :] = carry


def _cum(f_pad, b_pad, jmod):
    n_tiles = SEQ // TQ_FOX
    return pl.pallas_call(
        _cum_kernel,
        out_shape=[jax.ShapeDtypeStruct((BATCH, SEQ, LANES), BF16)] * 2
        + [jax.ShapeDtypeStruct((BATCH, 2 * n_tiles, LANES), F32)],
        grid=(BATCH,),
        in_specs=[pl.BlockSpec((SEQ, LANES), lambda b: (b, 0)),
                  pl.BlockSpec((1, LANES), lambda b: (0, 0)),
                  pl.BlockSpec((1, LANES), lambda b: (0, 0))],
        out_specs=[pl.BlockSpec((None, SEQ, LANES), lambda b: (b, 0, 0))] * 2
        + [pl.BlockSpec((None, 2 * n_tiles, LANES), lambda b: (b, 0, 0))],
        compiler_params=_cparams(1),
        name="cum",
    )(f_pad, b_pad, jmod)


def _swa_block(sink_cols, q, kk, vv, bias_ref, lo):
    tiles = []
    for g in range(N_KV_HEADS_SWA):
        parts = []
        for t in range(2):
            qt = q[:, (2 * g + t) * LANES:(2 * g + t + 1) * LANES]
            zero = jnp.zeros_like(qt)
            parts.append(jnp.where(lo, qt, zero))
            parts.append(jnp.where(lo, zero, qt))
        q4 = jnp.concatenate(parts, axis=0)
        s = lax.dot_general(q4, kk[:, g * LANES:(g + 1) * LANES], (((1,), (1,)), ((), ())),
                            preferred_element_type=F32)
        s = s + bias_ref[g]
        sink = sink_cols[g]
        m = jnp.maximum(jnp.max(s, axis=-1, keepdims=True), sink)
        p = jnp.exp2(s - m)
        den = jnp.sum(p, axis=-1, keepdims=True) + jnp.exp2(sink - m)
        o = jnp.dot(p.astype(BF16), vv[:, g * LANES:(g + 1) * LANES],
                    preferred_element_type=F32) / den
        tiles.append(jnp.where(lo, o[0:BLOCK], o[BLOCK:2 * BLOCK]))
        tiles.append(jnp.where(lo, o[2 * BLOCK:3 * BLOCK], o[3 * BLOCK:4 * BLOCK]))
    return tiles


def _swa_kernel(sink_ref, q_ref, kc_ref, kp_ref, vc_ref, vp_ref, bias_first_ref, bias_ref, o_ref):
    lane = lax.broadcasted_iota(jnp.int32, (BLOCK, LANES), 1)
    lo = lane < HEAD_DIM
    grp = N_HEADS_SWA // N_KV_HEADS_SWA
    row = lax.broadcasted_iota(jnp.int32, (grp * BLOCK, 1), 0)
    sink_cols = []
    for g in range(N_KV_HEADS_SWA):
        col = jnp.full((grp * BLOCK, 1), sink_ref[g * grp + grp - 1], F32)
        for hh in range(grp - 2, -1, -1):
            col = jnp.where(row < (hh + 1) * BLOCK, sink_ref[g * grp + hh], col)
        sink_cols.append(col)
    for blk in range(SWA_BLOCKS):
        rows = slice(blk * BLOCK, (blk + 1) * BLOCK)
        if blk == 0:
            kk = jnp.concatenate([kp_ref[...], kc_ref[rows, :]], axis=0)
            vv = jnp.concatenate([vp_ref[...], vc_ref[rows, :]], axis=0)
            bias = bias_first_ref
        else:
            prev_rows = slice((blk - 1) * BLOCK, (blk + 1) * BLOCK)
            kk = kc_ref[prev_rows, :]
            vv = vc_ref[prev_rows, :]
            bias = bias_ref
        tiles = _swa_block(sink_cols, q_ref[rows, :], kk, vv, bias, lo)
        for c, tile in enumerate(tiles):
            o_ref[rows, c * LANES:(c + 1) * LANES] = tile.astype(BF16)


def _swa(sinks, qa, kdup, va, bias):
    nb = SEQ // BLOCK
    ns = nb // SWA_BLOCKS
    cur = lambda b, i, s: (b * ns + i, 0)
    prev = lambda b, i, s: (b * nb + jnp.maximum(SWA_BLOCKS * i - 1, 0), 0)
    grid_spec = pltpu.PrefetchScalarGridSpec(
        num_scalar_prefetch=1,
        grid=(BATCH, ns),
        in_specs=[pl.BlockSpec((SWA_BLOCKS * BLOCK, Q_A), cur),
                  pl.BlockSpec((SWA_BLOCKS * BLOCK, 2 * KV_A), cur),
                  pl.BlockSpec((BLOCK, 2 * KV_A), prev),
                  pl.BlockSpec((SWA_BLOCKS * BLOCK, 2 * KV_A), cur),
                  pl.BlockSpec((BLOCK, 2 * KV_A), prev),
                  pl.BlockSpec((None,) + bias.shape[1:], lambda b, i, s: (jnp.minimum(i, 1), 0, 0, 0)),
                  pl.BlockSpec((None,) + bias.shape[1:], lambda b, i, s: (1, 0, 0, 0))],
        out_specs=pl.BlockSpec((SWA_BLOCKS * BLOCK, Q_A), cur))
    return pl.pallas_call(
        _swa_kernel,
        out_shape=jax.ShapeDtypeStruct((N_TOK, Q_A), BF16),
        grid_spec=grid_spec,
        compiler_params=_cparams(2),
        name="swa",
    )(sinks, qa, kdup, kdup, va, va, bias, bias)


def _fox_kernel(js_ref, q_ref, k_ref, v_ref, qa_ref, ka_ref, o_ref,
                kaug, vaug, q2, m_sc, acc_sc, s_0, s_1, s_2, s_3):
    tq, tk = TQ_FOX, TK_FOX
    b = pl.program_id(0)
    t = pl.program_id(1)
    g = pl.program_id(2)
    first_tile = (b * pl.num_programs(1) + t) * (SEQ // tq) + g * FOX_ROWS

    @pl.when(g == 0)
    def _():
        kaug[:, 0:LANES] = k_ref[...]
        kaug[:, LANES:2 * LANES] = ka_ref[...]
        vaug[:, 0:LANES] = v_ref[...]
        vaug[:, LANES:2 * LANES] = jnp.ones((SEQ, LANES), BF16)

    lane = lax.broadcasted_iota(jnp.int32, (tq, LANES), 1)
    lo = lane < HEAD_DIM
    base = 2 * DECAY_LANES * t
    own = [(lane >= base + h * DECAY_LANES) & (lane < base + (h + 1) * DECAY_LANES) for h in range(2)]
    for rw in range(FOX_ROWS):
        rows = slice(rw * tq, (rw + 1) * tq)
        q = q_ref[rows, :]
        qa = qa_ref[rows, :]
        zero = jnp.zeros_like(q)
        q2[rw, 0, :, 0:LANES] = jnp.where(lo, q, zero)
        q2[rw, 1, :, 0:LANES] = jnp.where(lo, zero, q)
        for h in range(2):
            q2[rw, h, :, LANES:2 * LANES] = jnp.where(own[h], qa, zero)

    rr = lax.broadcasted_iota(jnp.int32, (tq, tk), 0)
    cc = lax.broadcasted_iota(jnp.int32, (tq, tk), 1)
    causal = cc <= rr
    bufs = ((s_0, s_1), (s_2, s_3))
    j_starts = [js_ref[first_tile + rw] for rw in range(FOX_ROWS)]
    for rw in range(FOX_ROWS):
        nxt = rw + 1 if rw + 1 < FOX_ROWS else None
        _fox_query_tile(rw, g * FOX_ROWS + rw, j_starts[rw], bufs[rw % 2], q2, kaug, vaug, m_sc, acc_sc,
                        causal, lo, o_ref,
                        first_scores_done=rw > 0,
                        next_first=None if nxt is None else (nxt, j_starts[nxt], bufs[nxt % 2][0]))


def _fox_query_tile(rw, i, j_start, buf_pair, q2, kaug, vaug, m_sc, acc_sc, causal, lo, o_ref,
                    first_scores_done, next_first):
    tq, tk = TQ_FOX, TK_FOX
    s_a, s_b = buf_pair
    m_sc[...] = jnp.full(m_sc.shape, NEG_INF, F32)
    acc_sc[...] = jnp.zeros(acc_sc.shape, F32)

    def scores_of(row, h, ks):
        return lax.dot_general(q2[row, h], kaug[pl.ds(ks, tk), :], (((1,), (1,)), ((), ())),
                               preferred_element_type=F32)

    def scores(h, ks):
        return scores_of(rw, h, ks)

    def consume(h, s, ks, mask):
        if mask is not None:
            s = jnp.where(mask, s, NEG_INF)
        m_prev = m_sc[h]
        m_new = jnp.maximum(m_prev, jnp.max(s, axis=-1, keepdims=True))
        alpha = jnp.exp2(m_prev - m_new)
        p = jnp.exp2(s - jnp.concatenate([m_new] * (tk // LANES), axis=1))
        pv = jnp.dot(p.astype(BF16), vaug[pl.ds(ks, tk), :], preferred_element_type=F32)
        acc_sc[h] = jnp.concatenate([alpha, alpha], axis=1) * acc_sc[h] + pv
        m_sc[h] = m_new

    def key_start(j):
        return pl.multiple_of(j * tk, tk)

    def scores_into(buf, j):
        for h in range(2):
            buf[h] = scores(h, key_start(j))

    def consume_from(buf, j, mask):
        for h in range(2):
            consume(h, buf[h], key_start(j), mask)

    n_full = i - j_start

    if not first_scores_done:
        scores_into(s_a, j_start)

    def pair(p, carry):
        j = j_start + 2 * p
        scores_into(s_b, j + 1)
        consume_from(s_a, j, None)
        scores_into(s_a, j + 2)
        consume_from(s_b, j + 1, None)
        return carry

    lax.fori_loop(0, n_full // 2, pair, 0)
    odd = lax.rem(n_full, 2) == 1

    def start_next():
        if next_first is not None:
            nrow, nj, nbuf = next_first
            for h in range(2):
                nbuf[h] = scores_of(nrow, h, key_start(nj))

    @pl.when(odd)
    def _():
        scores_into(s_b, i)
        consume_from(s_a, i - 1, None)
        start_next()
        consume_from(s_b, i, causal)

    @pl.when(jnp.logical_not(odd))
    def _():
        start_next()
        consume_from(s_a, i, causal)

    outs = [acc_sc[h, :, 0:LANES] / acc_sc[h, :, LANES:2 * LANES] for h in range(2)]
    o_ref[rw * tq:(rw + 1) * tq, :] = jnp.where(lo, outs[0], outs[1]).astype(BF16)


def _fox(j_start, qb, kb, vb, qa, ka):
    tq = TQ_FOX
    ns = SEQ // (tq * FOX_ROWS)
    n_pairs = N_HEADS_FOX // 2
    qmap = lambda b, t, i, js: (b * ns + i, t)
    kmap = lambda b, t, i, js: (b, t)
    grid_spec = pltpu.PrefetchScalarGridSpec(
        num_scalar_prefetch=1,
        grid=(BATCH, n_pairs, ns),
        in_specs=[pl.BlockSpec((FOX_ROWS * tq, LANES), qmap),
                  pl.BlockSpec((SEQ, LANES), kmap),
                  pl.BlockSpec((SEQ, LANES), kmap),
                  pl.BlockSpec((None, FOX_ROWS * tq, LANES), lambda b, t, i, js: (b, i, 0)),
                  pl.BlockSpec((None, SEQ, LANES), lambda b, t, i, js: (b, 0, 0))],
        out_specs=pl.BlockSpec((FOX_ROWS * tq, LANES), qmap),
        scratch_shapes=[pltpu.VMEM((SEQ, 2 * LANES), BF16),
                        pltpu.VMEM((SEQ, 2 * LANES), BF16),
                        pltpu.VMEM((FOX_ROWS, 2, tq, 2 * LANES), BF16),
                        pltpu.VMEM((2, tq, LANES), F32),
                        pltpu.VMEM((2, tq, 2 * LANES), F32)]
        + [pltpu.VMEM((2, tq, TK_FOX), F32)] * 4)
    return pl.pallas_call(
        _fox_kernel,
        out_shape=jax.ShapeDtypeStruct((N_TOK, W_B), BF16),
        grid_spec=grid_spec,
        compiler_params=_cparams(3),
        name="fox",
    )(j_start, qb, kb, vb, qa, ka)


def _fox_first_tiles(nrm, fb):
    n_tiles = SEQ // TQ_FOX
    nr = nrm.reshape(BATCH, SEQ // TM_IN, 8, LANES)[:, :, 0, :] * 1.02
    nr = jnp.repeat(nr, TM_IN // TQ_FOX, axis=1)
    qn = jnp.sqrt(nr[..., 0:N_HEADS_FOX])
    kn = jnp.sqrt(nr[..., N_HEADS_FOX:2 * N_HEADS_FOX])
    f_first = fb[:, 0::2, 0:DECAY_LANES * N_HEADS_FOX:DECAY_LANES]
    f_last = fb[:, 1::2, 0:DECAY_LANES * N_HEADS_FOX:DECAY_LANES]
    kn_prefix = lax.cummax(kn, axis=1)
    upper = qn[:, :, None, :] * kn_prefix[:, None, :, :] + f_first[:, :, None, :] - f_last[:, None, :, :]
    row_max_low = -(qn * kn)[:, :, None, :]
    ii = jnp.arange(n_tiles)[None, :, None, None]
    jj = jnp.arange(n_tiles)[None, None, :, None]
    skip = (upper < row_max_low - PRUNE_MARGIN) & (jj < ii)
    skip = jnp.all(skip.reshape(BATCH, n_tiles, n_tiles, N_HEADS_FOX // 2, 2), axis=-1)
    first = jnp.sum(jnp.cumprod(skip.astype(jnp.int32), axis=2), axis=2)
    return jnp.transpose(first, (0, 2, 1)).reshape(-1).astype(jnp.int32)


def _post_kernel(x_ref, oa_ref, ob_ref, gt_ref, gm_ref, sc_ref, sh_ref, g_ref,
                 wa_ref, wb_ref, wo_ref, wr2_ref, br_ref,
                 x1_ref, xy_ref, rc_ref, cu_ref, hh_prev, lg_prev):
    step = pl.program_id(0)

    @pl.when(step == 0)
    def _():
        hh_prev[...] = jnp.zeros(hh_prev.shape, BF16)
        lg_prev[...] = jnp.zeros(lg_prev.shape, F32)

    @pl.when(step <= N_TOK_TILES)
    def _():
        hh_p = hh_prev[...]
        lg_p = lg_prev[...]
        hh, logits = _post_mix(x_ref, oa_ref, ob_ref, gt_ref, gm_ref, sc_ref, sh_ref, g_ref,
                               wa_ref, wb_ref, wo_ref, wr2_ref, br_ref, x1_ref)
        _post_route(hh_p, lg_p, xy_ref, rc_ref, cu_ref)
        hh_prev[...] = hh
        lg_prev[...] = logits

    @pl.when(step > N_TOK_TILES)
    def _():
        xy_ref[...] = jnp.zeros(xy_ref.shape, BF16)


def _post_mix(x_ref, oa_ref, ob_ref, gt_ref, gm_ref, sc_ref, sh_ref, g_ref,
              wa_ref, wb_ref, wo_ref, wr2_ref, br_ref, x1_ref):
    pa = jnp.dot(oa_ref[...], wa_ref[...], preferred_element_type=F32)
    pb = jnp.dot(ob_ref[...], wb_ref[...], preferred_element_type=F32)
    ga = jax.nn.sigmoid(gt_ref[:, 0:D_MODEL].astype(F32))
    gb = jax.nn.sigmoid(gt_ref[:, D_MODEL:2 * D_MODEL].astype(F32))
    merged = (ga * pa + gb * pb).astype(BF16)
    y = jnp.dot(merged, wo_ref[...], preferred_element_type=F32)
    x1 = x_ref[...] + gm_ref[...] * y
    x1_ref[...] = x1

    rs = lax.rsqrt(jnp.mean(x1 * x1, axis=-1, keepdims=True) + EPS)
    a = g_ref[...] * (1.0 + sc_ref[...])
    h2 = x1 * rs * a + sh_ref[...]

    hh = h2.astype(BF16)
    hl = (h2 - hh.astype(F32)).astype(BF16)
    hi_both = jnp.dot(hh, wr2_ref[...], preferred_element_type=F32)
    logits = (hi_both[:, 0:LANES] + hi_both[:, LANES:2 * LANES]
              + jnp.dot(hl, wr2_ref[:, 0:LANES], preferred_element_type=F32)
              + br_ref[...])
    return hh, logits


def _post_route(hh, logits, xy_ref, rc_ref, cu_ref):
    tm = TM_POST
    lane = lax.broadcasted_iota(jnp.int32, (tm, LANES), 1).astype(F32)
    big = float(LANES)
    gl = jnp.where(lane < N_GROUPS, logits, -jnp.inf)
    gmax = jnp.max(gl, axis=-1, keepdims=True)
    gi = jnp.min(jnp.where(gl == gmax, lane, big), axis=-1, keepdims=True)
    gsum = jnp.sum(jnp.exp(gl - gmax), axis=-1, keepdims=True)
    gp = 1.0 / gsum
    e_lo = N_GROUPS + EXPERTS_PER_GROUP * gi
    el = jnp.where((lane >= e_lo) & (lane < e_lo + EXPERTS_PER_GROUP), logits, -jnp.inf)
    v1 = jnp.max(el, axis=-1, keepdims=True)
    i1 = jnp.min(jnp.where(el == v1, lane, big), axis=-1, keepdims=True)
    el2 = jnp.where(lane == i1, -jnp.inf, el)
    v2 = jnp.max(el2, axis=-1, keepdims=True)
    i2 = jnp.min(jnp.where(el2 == v2, lane, big), axis=-1, keepdims=True)
    e21 = jnp.exp(v2 - v1)
    w1 = gp / (1.0 + e21)
    w2 = gp * e21 / (1.0 + e21)
    e1 = i1 - N_GROUPS
    e2 = i2 - N_GROUPS

    oh = jnp.where((lane == e1) | (lane == e2), 1.0, 0.0)
    cnt_u = jnp.floor((jnp.sum(oh, axis=0, keepdims=True) + (UNIT - 1)) * (1.0 / UNIT))
    r128 = lax.broadcasted_iota(jnp.int32, (LANES, LANES), 0)
    c128 = lax.broadcasted_iota(jnp.int32, (LANES, LANES), 1)
    before_lane = jnp.where(r128 < c128, 1.0, 0.0).astype(BF16)
    loc_u = jnp.dot(jnp.broadcast_to(cnt_u, (8, LANES)).astype(BF16), before_lane,
                    preferred_element_type=F32)
    trow = lax.broadcasted_iota(jnp.int32, (tm, LANES), 0)
    seen = oh
    k = 1
    while k < tm:
        if k < 8:
            shifted = jnp.where(trow >= k, pltpu.roll(seen, k, 0), 0.0)
        else:
            shifted = jnp.concatenate([jnp.zeros((k, LANES), F32), seen[:tm - k]], axis=0)
        seen = seen + shifted
        k *= 2
    pos_e = (seen - oh) + loc_u[0:1] * UNIT
    lp1 = jnp.sum(jnp.where(lane == e1, pos_e, 0.0), axis=-1, keepdims=True)
    lp2 = jnp.sum(jnp.where(lane == e2, pos_e, 0.0), axis=-1, keepdims=True)

    def to_row(col):
        return jnp.transpose(jnp.broadcast_to(col, (tm, LANES)))[0:1]

    srow = lax.broadcasted_iota(jnp.int32, (XY_ROWS, tm), 0).astype(F32)
    pm1 = jnp.where(srow == to_row(lp1), 1.0, 0.0).astype(BF16)
    pm2 = jnp.where(srow == to_row(lp2), 1.0, 0.0).astype(BF16)
    w1h = w1.astype(BF16).astype(F32)
    w2h = w2.astype(BF16).astype(F32)
    side = jnp.where(lane == W1_LANES[0], w1h, jnp.where(lane == W1_LANES[1], w1 - w1h,
           jnp.where(lane == W2_LANES[0], w2h, jnp.where(lane == W2_LANES[1], w2 - w2h,
           jnp.where(lane == E1_LANE, e1, jnp.where(lane == E2_LANE, e2, 0.0))))))
    tok = jnp.concatenate([hh, side.astype(BF16)], axis=1)
    xy_ref[...] = jnp.dot(pm1 + pm2, tok, preferred_element_type=F32).astype(BF16)

    cu_ref[...] = jnp.broadcast_to(cnt_u, cu_ref.shape)
    rc_ref[...] = jnp.where(lane == 0, lp1, jnp.where(lane == 1, lp2, 0.0))


def _post(x2, oa, ob, gates, gate_m, scale_f, shift_f, g_ffn, wa, wb, wo, wr2, b_r):
    tm = TM_POST
    tpb = SEQ // tm
    n_steps = N_TOK_TILES
    row = lambda i: (jnp.minimum(i, n_steps - 1), 0)
    per_b = lambda i: (jnp.minimum(i, n_steps - 1) // tpb, 0, 0)
    routed = lambda i: (jnp.clip(i - 1, 0, n_steps - 1), 0)
    const = lambda i: (0, 0)
    return pl.pallas_call(
        _post_kernel,
        out_shape=[jax.ShapeDtypeStruct((N_TOK, D_MODEL), F32),
                   jax.ShapeDtypeStruct(((n_steps + PAD_BLOCKS) * XY_ROWS, XY_COLS), BF16),
                   jax.ShapeDtypeStruct((N_TOK, LANES), F32),
                   jax.ShapeDtypeStruct((n_steps * 8, LANES), F32)],
        grid=(n_steps + 1 + PAD_BLOCKS,),
        in_specs=[pl.BlockSpec((tm, D_MODEL), row),
                  pl.BlockSpec((tm, Q_A), row),
                  pl.BlockSpec((tm, W_B), row),
                  pl.BlockSpec((tm, 2 * D_MODEL), row),
                  pl.BlockSpec((None, 1, D_MODEL), per_b),
                  pl.BlockSpec((None, 1, D_MODEL), per_b),
                  pl.BlockSpec((None, 1, D_MODEL), per_b),
                  pl.BlockSpec((1, D_MODEL), const),
                  pl.BlockSpec(wa.shape, const),
                  pl.BlockSpec(wb.shape, const),
                  pl.BlockSpec(wo.shape, const),
                  pl.BlockSpec(wr2.shape, const),
                  pl.BlockSpec((1, LANES), const)],
        out_specs=[pl.BlockSpec((tm, D_MODEL), row),
                   pl.BlockSpec((XY_ROWS, XY_COLS), lambda i: (jnp.maximum(i - 1, 0), 0)),
                   pl.BlockSpec((tm, LANES), routed),
                   pl.BlockSpec((8, LANES), routed)],
        scratch_shapes=[pltpu.VMEM((tm, D_MODEL), BF16),
                        pltpu.VMEM((tm, LANES), F32)],
        compiler_params=_cparams(1),
        name="post",
    )(x2, oa, ob, gates, gate_m, scale_f, shift_f, g_ffn, wa, wb, wo, wr2, b_r)


def _experts_kernel(te_ref, nu_ref, ur_ref, ne_ref, ep_ref, xy_in, wg_hbm, wu_hbm, wd_hbm, xy_out,
                    xbuf, ybuf, wg_s, wu_s, wd_s, wg_f, wu_f, wd_f, gsem, ssem, wsem):
    del xy_in
    r = pl.program_id(0)
    last = pl.num_programs(0) - 1
    n_used = nu_ref[0]
    slot = lax.rem(r, 2)

    def unit_row(step, s):
        return pl.multiple_of(ur_ref[step * UNITS_PER_TILE + s], UNIT)

    def start_gathers(step, sl):
        for s in range(UNITS_PER_TILE):
            pltpu.make_async_copy(xy_out.at[pl.ds(unit_row(step, s), UNIT), :],
                                  xbuf.at[sl, pl.ds(s * UNIT, UNIT), :], gsem.at[sl]).start()

    def wait_gathers(sl):
        pltpu.make_async_copy(xy_out.at[pl.ds(0, TM_EXP), :], xbuf.at[sl], gsem.at[sl]).wait()

    def start_scatters(step, sl):
        for s in range(UNITS_PER_TILE):
            pltpu.make_async_copy(ybuf.at[sl, pl.ds(s * UNIT, UNIT), :],
                                  xy_out.at[pl.ds(unit_row(step, s), UNIT), pl.ds(0, D_MODEL)],
                                  ssem.at[sl]).start()

    def wait_scatters(sl):
        pltpu.make_async_copy(ybuf.at[sl], xy_out.at[pl.ds(0, TM_EXP), pl.ds(0, D_MODEL)], ssem.at[sl]).wait()

    @pl.when(r == 0)
    def _():
        start_gathers(0, 0)

    @pl.when(jnp.logical_and(r < n_used, r >= 2))
    def _():
        wait_scatters(slot)

    def weight_copies(e, p):
        return [pltpu.make_async_copy(src.at[e], dst.at[p], wsem.at[p])
                for src, dst in ((wg_hbm, wg_f), (wu_hbm, wu_f), (wd_hbm, wd_f))]

    @pl.when(jnp.logical_and(r < n_used,
                             jnp.logical_or(r == 0, te_ref[r] != te_ref[jnp.maximum(r - 1, 0)])))
    def _():
        e = te_ref[r]
        p = ep_ref[r]

        @pl.when(r == 0)
        def _():
            for cp in weight_copies(e, p):
                cp.start()

        for cp in weight_copies(e, p):
            cp.wait()
        wg_s[...] = wg_f[p].astype(BF16)
        wu_s[...] = wu_f[p].astype(BF16)
        wd_s[...] = wd_f[p].astype(BF16)

        @pl.when(ne_ref[r] >= 0)
        def _():
            for cp in weight_copies(ne_ref[r], 1 - p):
                cp.start()

    @pl.when(r < n_used)
    def _():
        wait_gathers(slot)
        start_gathers(jnp.minimum(r + 1, last), 1 - slot)
        x = xbuf[slot, :, 0:D_MODEL]
        side = xbuf[slot, :, D_MODEL:XY_COLS].astype(F32)
        lane = lax.broadcasted_iota(jnp.int32, side.shape, 1)

        def lanes_sum(a, b):
            return jnp.sum(jnp.where((lane == a) | (lane == b), side, 0.0), axis=-1, keepdims=True)

        is_slot1 = lanes_sum(E1_LANE, E1_LANE) == te_ref[r].astype(F32)
        wrow = jnp.where(is_slot1, lanes_sum(*W1_LANES), lanes_sum(*W2_LANES))
        a = jnp.dot(x, wg_s[...], preferred_element_type=F32)
        u = jnp.dot(x, wu_s[...], preferred_element_type=F32)
        hid = (a * jax.nn.sigmoid(a) * u * wrow).astype(BF16)
        ybuf[slot] = jnp.dot(hid, wd_s[...], preferred_element_type=F32).astype(BF16)
        start_scatters(r, slot)

    @pl.when(r == n_used - 1)
    def _():
        wait_gathers(1 - slot)
        wait_scatters(slot)

        @pl.when(r >= 1)
        def _():
            wait_scatters(1 - slot)


def _experts(tile_expert, n_used, unit_rows, next_expert, expert_parity, xy, wg, wu, wd):
    grid_spec = pltpu.PrefetchScalarGridSpec(
        num_scalar_prefetch=5,
        grid=(N_EXP_TILES,),
        in_specs=[pl.BlockSpec(memory_space=pl.ANY),
                  pl.BlockSpec(memory_space=pl.ANY),
                  pl.BlockSpec(memory_space=pl.ANY),
                  pl.BlockSpec(memory_space=pl.ANY)],
        out_specs=pl.BlockSpec(memory_space=pl.ANY),
        scratch_shapes=[pltpu.VMEM((2, TM_EXP, XY_COLS), BF16),
                        pltpu.VMEM((2, TM_EXP, D_MODEL), BF16),
                        pltpu.VMEM((D_MODEL, D_FF_EXPERT), BF16),
                        pltpu.VMEM((D_MODEL, D_FF_EXPERT), BF16),
                        pltpu.VMEM((D_FF_EXPERT, D_MODEL), BF16),
                        pltpu.VMEM((2, D_MODEL, D_FF_EXPERT), F32),
                        pltpu.VMEM((2, D_MODEL, D_FF_EXPERT), F32),
                        pltpu.VMEM((2, D_FF_EXPERT, D_MODEL), F32),
                        pltpu.SemaphoreType.DMA((2,)),
                        pltpu.SemaphoreType.DMA((2,)),
                        pltpu.SemaphoreType.DMA((2,))])
    return pl.pallas_call(
        _experts_kernel,
        out_shape=jax.ShapeDtypeStruct(xy.shape, xy.dtype),
        grid_spec=grid_spec,
        input_output_aliases={5: 0},
        compiler_params=_cparams(1),
        name="experts",
    )(tile_expert, n_used, unit_rows, next_expert, expert_parity, xy, wg, wu, wd)


def _combine_kernel(x1_ref, rc_ref, gf_ref, gfin_ref, y_ref, o_ref):
    for tl in range(COMBINE_TILES):
        rows = slice(tl * TM_ROW, (tl + 1) * TM_ROW)
        lp1 = rc_ref[rows, 0:1]
        lp2 = rc_ref[rows, 1:2]
        scol = lax.broadcasted_iota(jnp.int32, (TM_ROW, XY_ROWS), 1).astype(F32)
        pick = jnp.where((scol == lp1) | (scol == lp2), 1.0, 0.0).astype(BF16)
        y = jnp.dot(pick, y_ref[tl * XY_ROWS:(tl + 1) * XY_ROWS, :], preferred_element_type=F32)
        xf = x1_ref[rows, :] + gf_ref[...] * y
        rs = lax.rsqrt(jnp.mean(xf * xf, axis=-1, keepdims=True) + EPS)
        o_ref[rows, :] = xf * rs * gfin_ref[...]


def _combine(x1, rcol, gate_f, g_final, xy):
    tm = TM_ROW * COMBINE_TILES
    tpb = SEQ // tm
    row = lambda i: (i, 0)
    return pl.pallas_call(
        _combine_kernel,
        out_shape=jax.ShapeDtypeStruct((N_TOK, D_MODEL), F32),
        grid=(N_TOK // tm,),
        in_specs=[pl.BlockSpec((tm, D_MODEL), row),
                  pl.BlockSpec((tm, LANES), row),
                  pl.BlockSpec((None, 1, D_MODEL), lambda i: (i // tpb, 0, 0)),
                  pl.BlockSpec((1, D_MODEL), lambda i: (0, 0)),
                  pl.BlockSpec((COMBINE_TILES * XY_ROWS, D_MODEL), row)],
        out_specs=pl.BlockSpec((tm, D_MODEL), row),
        compiler_params=_cparams(1),
        name="combine",
    )(x1, rcol, gate_f, g_final, xy)


def _t5_bucket_np():
    qi = np.arange(BLOCK)[:, None]
    kj = np.arange(2 * BLOCK)[None, :]
    dist = qi - kj + BLOCK
    n = np.maximum(dist, 0)
    max_exact = NUM_BUCKETS // 2
    nf = np.maximum(n, 1).astype(np.float32)
    large = max_exact + (np.log(nf / np.float32(max_exact)) / np.float32(math.log(MAX_DISTANCE / max_exact))
                         * np.float32(NUM_BUCKETS - max_exact)).astype(np.int32)
    large = np.minimum(large, NUM_BUCKETS - 1)
    bucket = np.where(n < max_exact, n, large)
    band = (dist >= 0) & (dist < WINDOW)
    return bucket.astype(np.int32), band


def kernel(x, c, w_ada, b_ada, g_norm_mix, g_norm_ffn, w_in, sinks, b_forget, w_proj_swa, w_proj_fox,
           w_out, rel_bias_table, w_router_group, b_router_group, w_router_expert, b_router_expert,
           w_gate_exp, w_up_exp, w_down_exp, g_final):
    l = 0
    x2 = x.reshape(N_TOK, D_MODEL)

    c16 = jnp.concatenate([c, jnp.zeros_like(c)], axis=0)
    mod = _ada(c16, w_ada[l], b_ada[l][None, :])[:BATCH]
    shift_m, scale_m, gate_m, shift_f, scale_f, gate_f = [
        m.reshape(BATCH, 1, D_MODEL) for m in jnp.split(mod, 6, axis=-1)]

    w = w_in[l]
    o_ka, o_va, o_qb = Q_A, Q_A + KV_A, Q_A + 2 * KV_A
    o_kb, o_vb, o_f = o_qb + W_B, o_qb + 2 * W_B, o_qb + 3 * W_B
    o_g = o_f + N_HEADS_FOX

    def dup(cols):
        heads = [cols[:, h * HEAD_DIM:(h + 1) * HEAD_DIM] for h in range(N_KV_HEADS_SWA)]
        return jnp.concatenate([hd for hd in heads for _ in range(2)], axis=1)

    head_order = jnp.argsort(b_forget[l])

    def reorder_heads(cols):
        return jnp.take(cols.reshape(D_MODEL, N_HEADS_FOX, HEAD_DIM), head_order, axis=1).reshape(D_MODEL, W_B)

    w_fox = jnp.concatenate([reorder_heads(w[:, o_qb:o_kb]), reorder_heads(w[:, o_kb:o_vb]),
                             reorder_heads(w[:, o_vb:o_f])], axis=1)
    b_fox = jnp.take(b_forget[l], head_order)
    w_proj_b = jnp.take(w_proj_fox[l].reshape(N_HEADS_FOX, HEAD_DIM, D_MODEL), head_order, axis=0).reshape(W_B, D_MODEL)
    carrier = DECAY_LANES * N_HEADS_FOX
    w_f = jnp.pad(jnp.repeat(jnp.take(w[:, o_f:o_g], head_order, axis=1), DECAY_LANES, axis=1),
                  ((0, 0), (0, LANES - carrier)))
    w_main = jnp.concatenate([w[:, :Q_A], dup(w[:, o_ka:o_va]), w[:, o_va:o_qb], w_f, w_fox], axis=1).astype(BF16)
    w_g = w[:, o_g:].astype(BF16)
    qa, kdup, va, qb, kb, vb, f_pad, gates, nrm = _inproj(
        x2, scale_m, shift_m, g_norm_mix[l][None, :], w_main, w_g)

    b_pad = jnp.pad(jnp.repeat(b_fox, DECAY_LANES), (0, LANES - carrier))[None, :]
    lanes = np.arange(LANES)
    jmod = jnp.asarray(np.where(lanes < DECAY_LANES * N_HEADS_FOX, lanes % DECAY_LANES, 7)[None, :].astype(np.int32))
    dq, dk, fb = _cum(f_pad, b_pad, jmod)

    bucket, band = _t5_bucket_np()
    onehot = jnp.asarray(bucket[None] == np.arange(NUM_BUCKETS)[:, None, None], dtype=F32)
    bias = jnp.einsum("bh,bqk->hqk", rel_bias_table.astype(F32), onehot, precision=HIGHEST)
    bias = jnp.where(band[None], bias * LOG2E, NEG_INF)
    first = np.arange(2 * BLOCK)[None, None, :] < BLOCK
    bias = jnp.stack([jnp.where(first, NEG_INF, bias), bias]).reshape(2, N_KV_HEADS_SWA, -1, 2 * BLOCK)
    o_a = _swa(sinks[l].astype(F32) * LOG2E, qa, kdup, va, bias)

    o_b = _fox(_fox_first_tiles(nrm, fb), qb, kb, vb, dq, dk)

    w_r = jnp.concatenate([w_router_group[l]] + [w_router_expert[l][g] for g in range(N_GROUPS)], axis=1)
    w_r = jnp.pad(w_r, ((0, 0), (0, LANES - w_r.shape[1])))
    wr_hi = w_r.astype(BF16)
    wr_lo = (w_r - wr_hi.astype(F32)).astype(BF16)
    wr2 = jnp.concatenate([wr_hi, wr_lo], axis=1)
    b_r = jnp.concatenate([b_router_group[l], b_router_expert[l].reshape(-1)])
    b_r = jnp.pad(b_r, (0, LANES - b_r.shape[0]))[None, :]
    x1, xy, rcol, cu = _post(x2, o_a, o_b, gates, gate_m, scale_f, shift_f, g_norm_ffn[l][None, :],
                             w_proj_swa[l].astype(BF16), w_proj_b.astype(BF16), w_out[l].astype(BF16),
                             wr2, b_r)

    i32 = jnp.int32
    n_tok_tiles = N_TOK // TM_POST
    cu = cu.reshape(n_tok_tiles, 8, LANES)[:, 0, :N_EXPERTS].astype(i32)
    loc_u = jnp.cumsum(cu, axis=1) - cu
    cend = jnp.cumsum(cu, axis=0)
    cstart = cend - cu
    tot_u = cend[-1]
    tiles_e = (tot_u + UNITS_PER_TILE - 1) // UNITS_PER_TILE
    tile_end = jnp.cumsum(tiles_e)
    tile_start = tile_end - tiles_e
    r = jnp.arange(N_EXP_TILES, dtype=i32)
    tile_expert = jnp.minimum(jnp.sum((tile_end[None, :] <= r[:, None]).astype(i32), axis=1), N_EXPERTS - 1)
    sel_e = tile_expert[:, None] == jnp.arange(N_EXPERTS, dtype=i32)[None, :]
    tw = r - jnp.sum(jnp.where(sel_e, tile_start[None, :], 0), axis=1)
    tot_r = jnp.sum(jnp.where(sel_e, tot_u[None, :], 0), axis=1)
    n_used = tile_end[-1:].astype(i32)
    q = tw[:, None] * UNITS_PER_TILE + jnp.arange(UNITS_PER_TILE, dtype=i32)[None, :]

    def of_expert(tab):
        return jnp.sum(jnp.where(sel_e[:, None, :], tab[None, :, :], 0), axis=2)

    cend_r, cstart_r, loc_r = of_expert(cend), of_expert(cstart), of_expert(loc_u)
    src_tile = jnp.minimum(jnp.sum((cend_r[:, None, :] <= q[:, :, None]).astype(i32), axis=2), n_tok_tiles - 1)
    sel_t = src_tile[:, :, None] == jnp.arange(n_tok_tiles, dtype=i32)[None, None, :]
    k = (q - jnp.sum(jnp.where(sel_t, cstart_r[:, None, :], 0), axis=2)
         + jnp.sum(jnp.where(sel_t, loc_r[:, None, :], 0), axis=2))
    real_rows = src_tile * XY_ROWS + k * UNIT
    pad_rows = PAD_BASE_ROW + (tile_expert[:, None] * PAD_UNITS_PER_EXPERT + (q - tot_r[:, None])) * UNIT
    idle_row = PAD_BASE_ROW + N_EXPERTS * PAD_UNITS_PER_EXPERT * UNIT
    unit_rows = jnp.where(q < tot_r[:, None], real_rows, pad_rows)
    unit_rows = jnp.where((r < n_used)[:, None], unit_rows, idle_row).reshape(-1).astype(i32)

    eid = jnp.arange(N_EXPERTS, dtype=i32)
    used = tiles_e > 0
    later_used = (eid[None, :] > eid[:, None]) & used[None, :]
    next_e = jnp.min(jnp.where(later_used, eid[None, :], N_EXPERTS), axis=1)
    next_e = jnp.where(next_e == N_EXPERTS, -1, next_e)
    parity_e = (jnp.cumsum(used.astype(i32)) - used.astype(i32)) % 2
    next_expert = jnp.sum(jnp.where(sel_e, next_e[None, :], 0), axis=1).astype(i32)
    expert_parity = jnp.sum(jnp.where(sel_e, parity_e[None, :], 0), axis=1).astype(i32)

    xy = _experts(tile_expert.astype(i32), n_used, unit_rows, next_expert, expert_parity, xy,
                  w_gate_exp[l].reshape(N_EXPERTS, D_MODEL, D_FF_EXPERT),
                  w_up_exp[l].reshape(N_EXPERTS, D_MODEL, D_FF_EXPERT),
                  w_down_exp[l].reshape(N_EXPERTS, D_FF_EXPERT, D_MODEL))
    out = _combine(x1, rcol, gate_f, g_final[None, :], xy)
    return out.reshape(BATCH, SEQ, D_MODEL)
```

```python
import math

import numpy as np
import jax
import jax.numpy as jnp
from jax import lax
from jax.experimental import pallas as pl
from jax.experimental.pallas import tpu as pltpu

F32 = jnp.float32
BF16 = jnp.bfloat16
HIGHEST = lax.Precision.HIGHEST

D_MODEL = 1024
BATCH = 8
SEQ = 4096
N_TOK = BATCH * SEQ
N_HEADS_SWA = 8
N_KV_HEADS_SWA = 2
N_HEADS_FOX = 8
HEAD_DIM = 64
WINDOW = 128
BLOCK = 128
NUM_BUCKETS = 32
MAX_DISTANCE = 128
N_GROUPS = 4
EXPERTS_PER_GROUP = 8
N_EXPERTS = N_GROUPS * EXPERTS_PER_GROUP
D_FF_EXPERT = 256
EPS = 1e-6
NEG_INF = -1e30

Q_A = N_HEADS_SWA * HEAD_DIM
KV_A = N_KV_HEADS_SWA * HEAD_DIM
W_B = N_HEADS_FOX * HEAD_DIM
LANES = 128
QK_SCALE = HEAD_DIM ** -0.5

TM_IN = 1024
TM_POST = 512
TQ_FOX = 512
TK_FOX = TQ_FOX
FOX_ROWS = 2
SWA_BLOCKS = 4
TM_EXP = 512
TM_ROW = 512
COMBINE_TILES = 2
UNIT = 16
XY_UNITS = 2 * TM_POST // UNIT + N_EXPERTS
XY_ROWS = XY_UNITS * UNIT
XY_COLS = D_MODEL + LANES
UNITS_PER_TILE = TM_EXP // UNIT
N_TOK_TILES = N_TOK // TM_POST
N_EXP_TILES = N_TOK_TILES * XY_UNITS // UNITS_PER_TILE + N_EXPERTS
PAD_UNITS_PER_EXPERT = UNITS_PER_TILE - 1
PAD_BLOCKS = -(-(N_EXPERTS * PAD_UNITS_PER_EXPERT * UNIT) // XY_ROWS)
PAD_BASE_ROW = N_TOK_TILES * XY_ROWS
W1_LANES, W2_LANES, E1_LANE, E2_LANE = (4, 6), (5, 7), 8, 9
VMEM_LIMIT = 56 * 1024 * 1024

DECAY_LANES = 6
LOG2E = math.log2(math.e)
Q_SCALE_LOG2 = QK_SCALE * LOG2E
PRUNE_MARGIN = 160.0


def _cparams(n_axes):
    return pltpu.CompilerParams(dimension_semantics=("arbitrary",) * n_axes,
                                vmem_limit_bytes=VMEM_LIMIT)


def _ada_kernel(c_ref, w_ref, b_ref, o_ref):
    c = c_ref[...]
    ca = c * jax.nn.sigmoid(c)
    o_ref[...] = jnp.dot(ca.astype(BF16), w_ref[...].astype(BF16),
                         preferred_element_type=F32) + b_ref[...]


def _ada(c16, w_ada, b_ada):
    n_out = w_ada.shape[1]
    blk = 1024
    return pl.pallas_call(
        _ada_kernel,
        out_shape=jax.ShapeDtypeStruct((16, n_out), F32),
        grid=(n_out // blk,),
        in_specs=[pl.BlockSpec((16, D_MODEL), lambda j: (0, 0)),
                  pl.BlockSpec((D_MODEL, blk), lambda j: (0, j)),
                  pl.BlockSpec((1, blk), lambda j: (0, j))],
        out_specs=pl.BlockSpec((16, blk), lambda j: (0, j)),
        compiler_params=_cparams(1),
        name="ada",
    )(c16, w_ada, b_ada)


def _inproj_kernel(x_ref, sc_ref, sh_ref, g_ref, wm_ref, wg_ref, ind_ref,
                   qa_ref, kd_ref, va_ref, qb_ref, kb_ref, vb_ref, f_ref, gt_ref, nrm_ref):
    x = x_ref[...]
    rs = lax.rsqrt(jnp.mean(x * x, axis=-1, keepdims=True) + EPS)
    a = g_ref[...] * (1.0 + sc_ref[...])
    h = (x * rs * a + sh_ref[...]).astype(BF16)

    def mm(w):
        return jnp.dot(h, w, preferred_element_type=F32)

    qa_ref[...] = (mm(wm_ref[:, 0:512]) * Q_SCALE_LOG2).astype(BF16)
    kd_ref[...] = mm(wm_ref[:, 512:768]).astype(BF16)
    vf = mm(wm_ref[:, 768:1024])
    f_ref[...] = vf[:, LANES:2 * LANES]
    v = vf[:, 0:LANES]
    vr = pltpu.roll(v, HEAD_DIM, 1)
    lo = lax.broadcasted_iota(jnp.int32, v.shape, 1) < HEAD_DIM
    va_ref[:, 0:LANES] = jnp.where(lo, v, vr).astype(BF16)
    va_ref[:, LANES:2 * LANES] = jnp.where(lo, vr, v).astype(BF16)
    qb = (mm(wm_ref[:, 1024:1536]) * Q_SCALE_LOG2).astype(BF16)
    kb = mm(wm_ref[:, 1536:2048]).astype(BF16)
    qb_ref[...] = qb
    kb_ref[...] = kb
    vb_ref[...] = mm(wm_ref[:, 2048:2560]).astype(BF16)
    sq = jnp.concatenate([qb, kb], axis=1).astype(F32)
    seg = jnp.dot((sq * sq).astype(BF16), ind_ref[...], preferred_element_type=F32)
    for sub in range(TM_IN // TQ_FOX):
        tile_max = jnp.max(seg[sub * TQ_FOX:(sub + 1) * TQ_FOX], axis=0, keepdims=True)
        nrm_ref[8 * sub:8 * sub + 8, :] = jnp.broadcast_to(tile_max, (8, LANES))
    gt_ref[...] = mm(wg_ref[...]).astype(BF16)


def _inproj(x2, scale_m, shift_m, g_mix, w_main, w_g):
    tm = TM_IN
    tpb = SEQ // tm
    row = lambda i: (i, 0)
    per_b = lambda i: (i // tpb, 0, 0)
    const = lambda i: (0, 0)
    outs = [(Q_A, BF16), (2 * KV_A, BF16), (2 * KV_A, BF16), (W_B, BF16), (W_B, BF16), (W_B, BF16),
            (LANES, F32), (2 * D_MODEL, BF16)]
    ind_np = np.zeros((2 * W_B, LANES), np.float32)
    ind_np[np.arange(2 * W_B), np.arange(2 * W_B) // HEAD_DIM] = 1.0
    ind = jnp.asarray(ind_np, dtype=BF16)
    n_steps = N_TOK // tm
    nrm_rows = 8 * (tm // TQ_FOX)
    once = pl.Buffered(1)
    return pl.pallas_call(
        _inproj_kernel,
        out_shape=[jax.ShapeDtypeStruct((N_TOK, w), dt) for w, dt in outs]
        + [jax.ShapeDtypeStruct((n_steps * nrm_rows, LANES), F32)],
        grid=(n_steps,),
        in_specs=[pl.BlockSpec((tm, D_MODEL), row),
                  pl.BlockSpec((None, 1, D_MODEL), per_b),
                  pl.BlockSpec((None, 1, D_MODEL), per_b),
                  pl.BlockSpec((1, D_MODEL), const),
                  pl.BlockSpec(w_main.shape, const, pipeline_mode=once),
                  pl.BlockSpec(w_g.shape, const, pipeline_mode=once),
                  pl.BlockSpec(ind.shape, const, pipeline_mode=once)],
        out_specs=[pl.BlockSpec((tm, w), row) for w, _ in outs] + [pl.BlockSpec((nrm_rows, LANES), row)],
        compiler_params=_cparams(1),
        name="inproj",
    )(x2, scale_m, shift_m, g_mix, w_main, w_g, ind)


def _log_sigmoid(x):
    return jnp.minimum(x, 0.0) - jnp.log1p(jnp.exp(-jnp.abs(x)))


def _cum_kernel(f_ref, b_ref, jm_ref, qa_ref, ka_ref, fb_ref):
    cum = _log_sigmoid(f_ref[...] + b_ref[...]) * LOG2E
    row = lax.broadcasted_iota(jnp.int32, cum.shape, 0)
    k = 1
    while k < SEQ:
        if k < 8:
            shifted = jnp.where(row >= k, pltpu.roll(cum, k, 0), 0.0)
        else:
            shifted = jnp.concatenate([jnp.zeros((k, LANES), F32), cum[:SEQ - k]], axis=0)
        cum = cum + shifted
        k *= 2
    jm = jm_ref[...]
    for blk in range(SEQ // LANES):
        rows = slice(blk * LANES, (blk + 1) * LANES)
        cb = cum[rows]
        carry = cb[LANES - 1:LANES]
        hi = cb.astype(BF16).astype(F32)
        r1 = cb - hi
        mid = r1.astype(BF16).astype(F32)
        lo = (r1 - mid).astype(BF16).astype(F32)
        one = jnp.ones_like(cb)
        zero = jnp.zeros_like(cb)
        qa = jnp.where(jm == 0, hi, jnp.where(jm == 1, mid, jnp.where(jm == 2, lo,
                       jnp.where(jm < DECAY_LANES, one, zero))))
        ka = jnp.where(jm == 3, -hi, jnp.where(jm == 4, -mid, jnp.where(jm == 5, -lo,
                       jnp.where(jm < 3, one, zero))))
        qa_ref[rows, :] = qa.astype(BF16)
        ka_ref[rows, :] = ka.astype(BF16)
        blocks_per_tile = TQ_FOX // LANES
        tile = blk // blocks_per_tile
        if blk % blocks_per_tile == 0:
            fb_ref[2 * tile:2 * tile + 1, :] = cb[0:1]
        if blk % blocks_per_tile == blocks_per_tile - 1:
            fb_ref[2 * tile + 1:2 * tile + 2, :] = carry


def _cum(f_pad, b_pad, jmod):
    n_tiles = SEQ // TQ_FOX
    return pl.pallas_call(
        _cum_kernel,
        out_shape=[jax.ShapeDtypeStruct((BATCH, SEQ, LANES), BF16)] * 2
        + [jax.ShapeDtypeStruct((BATCH, 2 * n_tiles, LANES), F32)],
        grid=(BATCH,),
        in_specs=[pl.BlockSpec((SEQ, LANES), lambda b: (b, 0)),
                  pl.BlockSpec((1, LANES), lambda b: (0, 0)),
                  pl.BlockSpec((1, LANES), lambda b: (0, 0))],
        out_specs=[pl.BlockSpec((None, SEQ, LANES), lambda b: (b, 0, 0))] * 2
        + [pl.BlockSpec((None, 2 * n_tiles, LANES), lambda b: (b, 0, 0))],
        compiler_params=_cparams(1),
        name="cum",
    )(f_pad, b_pad, jmod)


def _swa_block(sink_cols, q, kk, vv, bias_ref, lo):
    tiles = []
    for g in range(N_KV_HEADS_SWA):
        parts = []
        for t in range(2):
            qt = q[:, (2 * g + t) * LANES:(2 * g + t + 1) * LANES]
            zero = jnp.zeros_like(qt)
            parts.append(jnp.where(lo, qt, zero))
            parts.append(jnp.where(lo, zero, qt))
        q4 = jnp.concatenate(parts, axis=0)
        s = lax.dot_general(q4, kk[:, g * LANES:(g + 1) * LANES], (((1,), (1,)), ((), ())),
                            preferred_element_type=F32)
        s = s + bias_ref[g]
        sink = sink_cols[g]
        m = jnp.maximum(jnp.max(s, axis=-1, keepdims=True), sink)
        p = jnp.exp2(s - m)
        den = jnp.sum(p, axis=-1, keepdims=True) + jnp.exp2(sink - m)
        o = jnp.dot(p.astype(BF16), vv[:, g * LANES:(g + 1) * LANES],
                    preferred_element_type=F32) / den
        tiles.append(jnp.where(lo, o[0:BLOCK], o[BLOCK:2 * BLOCK]))
        tiles.append(jnp.where(lo, o[2 * BLOCK:3 * BLOCK], o[3 * BLOCK:4 * BLOCK]))
    return tiles


def _swa_kernel(sink_ref, q_ref, kc_ref, kp_ref, vc_ref, vp_ref, bias_first_ref, bias_ref, o_ref):
    lane = lax.broadcasted_iota(jnp.int32, (BLOCK, LANES), 1)
    lo = lane < HEAD_DIM
    grp = N_HEADS_SWA // N_KV_HEADS_SWA
    row = lax.broadcasted_iota(jnp.int32, (grp * BLOCK, 1), 0)
    sink_cols = []
    for g in range(N_KV_HEADS_SWA):
        col = jnp.full((grp * BLOCK, 1), sink_ref[g * grp + grp - 1], F32)
        for hh in range(grp - 2, -1, -1):
            col = jnp.where(row < (hh + 1) * BLOCK, sink_ref[g * grp + hh], col)
        sink_cols.append(col)
    for blk in range(SWA_BLOCKS):
        rows = slice(blk * BLOCK, (blk + 1) * BLOCK)
        if blk == 0:
            kk = jnp.concatenate([kp_ref[...], kc_ref[rows, :]], axis=0)
            vv = jnp.concatenate([vp_ref[...], vc_ref[rows, :]], axis=0)
            bias = bias_first_ref
        else:
            prev_rows = slice((blk - 1) * BLOCK, (blk + 1) * BLOCK)
            kk = kc_ref[prev_rows, :]
            vv = vc_ref[prev_rows, :]
            bias = bias_ref
        tiles = _swa_block(sink_cols, q_ref[rows, :], kk, vv, bias, lo)
        for c, tile in enumerate(tiles):
            o_ref[rows, c * LANES:(c + 1) * LANES] = tile.astype(BF16)


def _swa(sinks, qa, kdup, va, bias):
    nb = SEQ // BLOCK
    ns = nb // SWA_BLOCKS
    cur = lambda b, i, s: (b * ns + i, 0)
    prev = lambda b, i, s: (b * nb + jnp.maximum(SWA_BLOCKS * i - 1, 0), 0)
    grid_spec = pltpu.PrefetchScalarGridSpec(
        num_scalar_prefetch=1,
        grid=(BATCH, ns),
        in_specs=[pl.BlockSpec((SWA_BLOCKS * BLOCK, Q_A), cur),
                  pl.BlockSpec((SWA_BLOCKS * BLOCK, 2 * KV_A), cur),
                  pl.BlockSpec((BLOCK, 2 * KV_A), prev),
                  pl.BlockSpec((SWA_BLOCKS * BLOCK, 2 * KV_A), cur),
                  pl.BlockSpec((BLOCK, 2 * KV_A), prev),
                  pl.BlockSpec((None,) + bias.shape[1:], lambda b, i, s: (jnp.minimum(i, 1), 0, 0, 0)),
                  pl.BlockSpec((None,) + bias.shape[1:], lambda b, i, s: (1, 0, 0, 0))],
        out_specs=pl.BlockSpec((SWA_BLOCKS * BLOCK, Q_A), cur))
    return pl.pallas_call(
        _swa_kernel,
        out_shape=jax.ShapeDtypeStruct((N_TOK, Q_A), BF16),
        grid_spec=grid_spec,
        compiler_params=_cparams(2),
        name="swa",
    )(sinks, qa, kdup, kdup, va, va, bias, bias)


def _fox_kernel(js_ref, q_ref, k_ref, v_ref, qa_ref, ka_ref, o_ref,
                kaug, vaug, q2, m_sc, acc_sc, s_0, s_1, s_2, s_3):
    tq, tk = TQ_FOX, TK_FOX
    b = pl.program_id(0)
    t = pl.program_id(1)
    g = pl.program_id(2)
    first_tile = (b * pl.num_programs(1) + t) * (SEQ // tq) + g * FOX_ROWS

    @pl.when(g == 0)
    def _():
        kaug[:, 0:LANES] = k_ref[...]
        kaug[:, LANES:2 * LANES] = ka_ref[...]
        vaug[:, 0:LANES] = v_ref[...]
        vaug[:, LANES:2 * LANES] = jnp.ones((SEQ, LANES), BF16)

    lane = lax.broadcasted_iota(jnp.int32, (tq, LANES), 1)
    lo = lane < HEAD_DIM
    base = 2 * DECAY_LANES * t
    own = [(lane >= base + h * DECAY_LANES) & (lane < base + (h + 1) * DECAY_LANES) for h in range(2)]
    for rw in range(FOX_ROWS):
        rows = slice(rw * tq, (rw + 1) * tq)
        q = q_ref[rows, :]
        qa = qa_ref[rows, :]
        zero = jnp.zeros_like(q)
        q2[rw, 0, :, 0:LANES] = jnp.where(lo, q, zero)
        q2[rw, 1, :, 0:LANES] = jnp.where(lo, zero, q)
        for h in range(2):
            q2[rw, h, :, LANES:2 * LANES] = jnp.where(own[h], qa, zero)

    rr = lax.broadcasted_iota(jnp.int32, (tq, tk), 0)
    cc = lax.broadcasted_iota(jnp.int32, (tq, tk), 1)
    causal = cc <= rr
    bufs = ((s_0, s_1), (s_2, s_3))
    j_starts = [js_ref[first_tile + rw] for rw in range(FOX_ROWS)]
    for rw in range(FOX_ROWS):
        nxt = rw + 1 if rw + 1 < FOX_ROWS else None
        _fox_query_tile(rw, g * FOX_ROWS + rw, j_starts[rw], bufs[rw % 2], q2, kaug, vaug, m_sc, acc_sc,
                        causal, lo, o_ref,
                        first_scores_done=rw > 0,
                        next_first=None if nxt is None else (nxt, j_starts[nxt], bufs[nxt % 2][0]))


def _fox_query_tile(rw, i, j_start, buf_pair, q2, kaug, vaug, m_sc, acc_sc, causal, lo, o_ref,
                    first_scores_done, next_first):
    tq, tk = TQ_FOX, TK_FOX
    s_a, s_b = buf_pair
    m_sc[...] = jnp.full(m_sc.shape, NEG_INF, F32)
    acc_sc[...] = jnp.zeros(acc_sc.shape, F32)

    def scores_of(row, h, ks):
        return lax.dot_general(q2[row, h], kaug[pl.ds(ks, tk), :], (((1,), (1,)), ((), ())),
                               preferred_element_type=F32)

    def scores(h, ks):
        return scores_of(rw, h, ks)

    def consume(h, s, ks, mask):
        if mask is not None:
            s = jnp.where(mask, s, NEG_INF)
        m_prev = m_sc[h]
        m_new = jnp.maximum(m_prev, jnp.max(s, axis=-1, keepdims=True))
        alpha = jnp.exp2(m_prev - m_new)
        p = jnp.exp2(s - jnp.concatenate([m_new] * (tk // LANES), axis=1))
        pv = jnp.dot(p.astype(BF16), vaug[pl.ds(ks, tk), :], preferred_element_type=F32)
        acc_sc[h] = jnp.concatenate([alpha, alpha], axis=1) * acc_sc[h] + pv
        m_sc[h] = m_new

    def key_start(j):
        return pl.multiple_of(j * tk, tk)

    def scores_into(buf, j):
        for h in range(2):
            buf[h] = scores(h, key_start(j))

    def consume_from(buf, j, mask):
        for h in range(2):
            consume(h, buf[h], key_start(j), mask)

    n_full = i - j_start

    if not first_scores_done:
        scores_into(s_a, j_start)

    def pair(p, carry):
        j = j_start + 2 * p
        scores_into(s_b, j + 1)
        consume_from(s_a, j, None)
        scores_into(s_a, j + 2)
        consume_from(s_b, j + 1, None)
        return carry

    lax.fori_loop(0, n_full // 2, pair, 0)
    odd = lax.rem(n_full, 2) == 1

    def start_next():
        if next_first is not None:
            nrow, nj, nbuf = next_first
            for h in range(2):
                nbuf[h] = scores_of(nrow, h, key_start(nj))

    @pl.when(odd)
    def _():
        scores_into(s_b, i)
        consume_from(s_a, i - 1, None)
        start_next()
        consume_from(s_b, i, causal)

    @pl.when(jnp.logical_not(odd))
    def _():
        start_next()
        consume_from(s_a, i, causal)

    outs = [acc_sc[h, :, 0:LANES] / acc_sc[h, :, LANES:2 * LANES] for h in range(2)]
    o_ref[rw * tq:(rw + 1) * tq, :] = jnp.where(lo, outs[0], outs[1]).astype(BF16)


def _fox(j_start, qb, kb, vb, qa, ka):
    tq = TQ_FOX
    ns = SEQ // (tq * FOX_ROWS)
    n_pairs = N_HEADS_FOX // 2
    qmap = lambda b, t, i, js: (b * ns + i, t)
    kmap = lambda b, t, i, js: (b, t)
    grid_spec = pltpu.PrefetchScalarGridSpec(
        num_scalar_prefetch=1,
        grid=(BATCH, n_pairs, ns),
        in_specs=[pl.BlockSpec((FOX_ROWS * tq, LANES), qmap),
                  pl.BlockSpec((SEQ, LANES), kmap),
                  pl.BlockSpec((SEQ, LANES), kmap),
                  pl.BlockSpec((None, FOX_ROWS * tq, LANES), lambda b, t, i, js: (b, i, 0)),
                  pl.BlockSpec((None, SEQ, LANES), lambda b, t, i, js: (b, 0, 0))],
        out_specs=pl.BlockSpec((FOX_ROWS * tq, LANES), qmap),
        scratch_shapes=[pltpu.VMEM((SEQ, 2 * LANES), BF16),
                        pltpu.VMEM((SEQ, 2 * LANES), BF16),
                        pltpu.VMEM((FOX_ROWS, 2, tq, 2 * LANES), BF16),
                        pltpu.VMEM((2, tq, LANES), F32),
                        pltpu.VMEM((2, tq, 2 * LANES), F32)]
        + [pltpu.VMEM((2, tq, TK_FOX), F32)] * 4)
    return pl.pallas_call(
        _fox_kernel,
        out_shape=jax.ShapeDtypeStruct((N_TOK, W_B), BF16),
        grid_spec=grid_spec,
        compiler_params=_cparams(3),
        name="fox",
    )(j_start, qb, kb, vb, qa, ka)


def _fox_first_tiles(nrm, fb):
    n_tiles = SEQ // TQ_FOX
    nr = nrm.reshape(BATCH, n_tiles, 8, LANES)[:, :, 0, :] * 1.02
    qn = jnp.sqrt(nr[..., 0:N_HEADS_FOX])
    kn = jnp.sqrt(nr[..., N_HEADS_FOX:2 * N_HEADS_FOX])
    f_first = fb[:, 0::2, 0:DECAY_LANES * N_HEADS_FOX:DECAY_LANES]
    f_last = fb[:, 1::2, 0:DECAY_LANES * N_HEADS_FOX:DECAY_LANES]
    kn_prefix = lax.cummax(kn, axis=1)
    upper = qn[:, :, None, :] * kn_prefix[:, None, :, :] + f_first[:, :, None, :] - f_last[:, None, :, :]
    row_max_low = -(qn * kn)[:, :, None, :]
    ii = jnp.arange(n_tiles)[None, :, None, None]
    jj = jnp.arange(n_tiles)[None, None, :, None]
    skip = (upper < row_max_low - PRUNE_MARGIN) & (jj < ii)
    skip = jnp.all(skip.reshape(BATCH, n_tiles, n_tiles, N_HEADS_FOX // 2, 2), axis=-1)
    first = jnp.sum(jnp.cumprod(skip.astype(jnp.int32), axis=2), axis=2)
    return jnp.transpose(first, (0, 2, 1)).reshape(-1).astype(jnp.int32)


def _post_kernel(x_ref, oa_ref, ob_ref, gt_ref, gm_ref, sc_ref, sh_ref, g_ref,
                 wa_ref, wb_ref, wo_ref, wr2_ref, br_ref,
                 x1_ref, xy_ref, rc_ref, cu_ref, hh_prev, lg_prev):
    step = pl.program_id(0)

    @pl.when(step == 0)
    def _():
        hh_prev[...] = jnp.zeros(hh_prev.shape, BF16)
        lg_prev[...] = jnp.zeros(lg_prev.shape, F32)

    @pl.when(step <= N_TOK_TILES)
    def _():
        hh_p = hh_prev[...]
        lg_p = lg_prev[...]
        hh, logits = _post_mix(x_ref, oa_ref, ob_ref, gt_ref, gm_ref, sc_ref, sh_ref, g_ref,
                               wa_ref, wb_ref, wo_ref, wr2_ref, br_ref, x1_ref)
        _post_route(hh_p, lg_p, xy_ref, rc_ref, cu_ref)
        hh_prev[...] = hh
        lg_prev[...] = logits

    @pl.when(step > N_TOK_TILES)
    def _():
        xy_ref[...] = jnp.zeros(xy_ref.shape, BF16)


def _post_mix(x_ref, oa_ref, ob_ref, gt_ref, gm_ref, sc_ref, sh_ref, g_ref,
              wa_ref, wb_ref, wo_ref, wr2_ref, br_ref, x1_ref):
    pa = jnp.dot(oa_ref[...], wa_ref[...], preferred_element_type=F32)
    pb = jnp.dot(ob_ref[...], wb_ref[...], preferred_element_type=F32)
    ga = jax.nn.sigmoid(gt_ref[:, 0:D_MODEL].astype(F32))
    gb = jax.nn.sigmoid(gt_ref[:, D_MODEL:2 * D_MODEL].astype(F32))
    merged = (ga * pa + gb * pb).astype(BF16)
    y = jnp.dot(merged, wo_ref[...], preferred_element_type=F32)
    x1 = x_ref[...] + gm_ref[...] * y
    x1_ref[...] = x1

    rs = lax.rsqrt(jnp.mean(x1 * x1, axis=-1, keepdims=True) + EPS)
    a = g_ref[...] * (1.0 + sc_ref[...])
    h2 = x1 * rs * a + sh_ref[...]

    hh = h2.astype(BF16)
    hl = (h2 - hh.astype(F32)).astype(BF16)
    hi_both = jnp.dot(hh, wr2_ref[...], preferred_element_type=F32)
    logits = (hi_both[:, 0:LANES] + hi_both[:, LANES:2 * LANES]
              + jnp.dot(hl, wr2_ref[:, 0:LANES], preferred_element_type=F32)
              + br_ref[...])
    return hh, logits


def _post_route(hh, logits, xy_ref, rc_ref, cu_ref):
    tm = TM_POST
    lane = lax.broadcasted_iota(jnp.int32, (tm, LANES), 1).astype(F32)
    big = float(LANES)
    gl = jnp.where(lane < N_GROUPS, logits, -jnp.inf)
    gmax = jnp.max(gl, axis=-1, keepdims=True)
    gi = jnp.min(jnp.where(gl == gmax, lane, big), axis=-1, keepdims=True)
    gsum = jnp.sum(jnp.exp(gl - gmax), axis=-1, keepdims=True)
    gp = 1.0 / gsum
    e_lo = N_GROUPS + EXPERTS_PER_GROUP * gi
    el = jnp.where((lane >= e_lo) & (lane < e_lo + EXPERTS_PER_GROUP), logits, -jnp.inf)
    v1 = jnp.max(el, axis=-1, keepdims=True)
    i1 = jnp.min(jnp.where(el == v1, lane, big), axis=-1, keepdims=True)
    el2 = jnp.where(lane == i1, -jnp.inf, el)
    v2 = jnp.max(el2, axis=-1, keepdims=True)
    i2 = jnp.min(jnp.where(el2 == v2, lane, big), axis=-1, keepdims=True)
    e21 = jnp.exp(v2 - v1)
    w1 = gp / (1.0 + e21)
    w2 = gp * e21 / (1.0 + e21)
    e1 = i1 - N_GROUPS
    e2 = i2 - N_GROUPS

    oh = jnp.where((lane == e1) | (lane == e2), 1.0, 0.0)
    cnt_u = jnp.floor((jnp.sum(oh, axis=0, keepdims=True) + (UNIT - 1)) * (1.0 / UNIT))
    r128 = lax.broadcasted_iota(jnp.int32, (LANES, LANES), 0)
    c128 = lax.broadcasted_iota(jnp.int32, (LANES, LANES), 1)
    before_lane = jnp.where(r128 < c128, 1.0, 0.0).astype(BF16)
    loc_u = jnp.dot(jnp.broadcast_to(cnt_u, (8, LANES)).astype(BF16), before_lane,
                    preferred_element_type=F32)
    trow = lax.broadcasted_iota(jnp.int32, (tm, LANES), 0)
    seen = oh
    k = 1
    while k < tm:
        if k < 8:
            shifted = jnp.where(trow >= k, pltpu.roll(seen, k, 0), 0.0)
        else:
            shifted = jnp.concatenate([jnp.zeros((k, LANES), F32), seen[:tm - k]], axis=0)
        seen = seen + shifted
        k *= 2
    pos_e = (seen - oh) + loc_u[0:1] * UNIT
    lp1 = jnp.sum(jnp.where(lane == e1, pos_e, 0.0), axis=-1, keepdims=True)
    lp2 = jnp.sum(jnp.where(lane == e2, pos_e, 0.0), axis=-1, keepdims=True)

    def to_row(col):
        return jnp.transpose(jnp.broadcast_to(col, (tm, LANES)))[0:1]

    srow = lax.broadcasted_iota(jnp.int32, (XY_ROWS, tm), 0).astype(F32)
    pm1 = jnp.where(srow == to_row(lp1), 1.0, 0.0).astype(BF16)
    pm2 = jnp.where(srow == to_row(lp2), 1.0, 0.0).astype(BF16)
    w1h = w1.astype(BF16).astype(F32)
    w2h = w2.astype(BF16).astype(F32)
    side = jnp.where(lane == W1_LANES[0], w1h, jnp.where(lane == W1_LANES[1], w1 - w1h,
           jnp.where(lane == W2_LANES[0], w2h, jnp.where(lane == W2_LANES[1], w2 - w2h,
           jnp.where(lane == E1_LANE, e1, jnp.where(lane == E2_LANE, e2, 0.0))))))
    tok = jnp.concatenate([hh, side.astype(BF16)], axis=1)
    xy_ref[...] = jnp.dot(pm1 + pm2, tok, preferred_element_type=F32).astype(BF16)

    cu_ref[...] = jnp.broadcast_to(cnt_u, cu_ref.shape)
    rc_ref[...] = jnp.where(lane == 0, lp1, jnp.where(lane == 1, lp2, 0.0))


def _post(x2, oa, ob, gates, gate_m, scale_f, shift_f, g_ffn, wa, wb, wo, wr2, b_r):
    tm = TM_POST
    tpb = SEQ // tm
    n_steps = N_TOK_TILES
    row = lambda i: (jnp.minimum(i, n_steps - 1), 0)
    per_b = lambda i: (jnp.minimum(i, n_steps - 1) // tpb, 0, 0)
    routed = lambda i: (jnp.clip(i - 1, 0, n_steps - 1), 0)
    const = lambda i: (0, 0)
    return pl.pallas_call(
        _post_kernel,
        out_shape=[jax.ShapeDtypeStruct((N_TOK, D_MODEL), F32),
                   jax.ShapeDtypeStruct(((n_steps + PAD_BLOCKS) * XY_ROWS, XY_COLS), BF16),
                   jax.ShapeDtypeStruct((N_TOK, LANES), F32),
                   jax.ShapeDtypeStruct((n_steps * 8, LANES), F32)],
        grid=(n_steps + 1 + PAD_BLOCKS,),
        in_specs=[pl.BlockSpec((tm, D_MODEL), row),
                  pl.BlockSpec((tm, Q_A), row),
                  pl.BlockSpec((tm, W_B), row),
                  pl.BlockSpec((tm, 2 * D_MODEL), row),
                  pl.BlockSpec((None, 1, D_MODEL), per_b),
                  pl.BlockSpec((None, 1, D_MODEL), per_b),
                  pl.BlockSpec((None, 1, D_MODEL), per_b),
                  pl.BlockSpec((1, D_MODEL), const),
                  pl.BlockSpec(wa.shape, const),
                  pl.BlockSpec(wb.shape, const),
                  pl.BlockSpec(wo.shape, const),
                  pl.BlockSpec(wr2.shape, const),
                  pl.BlockSpec((1, LANES), const)],
        out_specs=[pl.BlockSpec((tm, D_MODEL), row),
                   pl.BlockSpec((XY_ROWS, XY_COLS), lambda i: (jnp.maximum(i - 1, 0), 0)),
                   pl.BlockSpec((tm, LANES), routed),
                   pl.BlockSpec((8, LANES), routed)],
        scratch_shapes=[pltpu.VMEM((tm, D_MODEL), BF16),
                        pltpu.VMEM((tm, LANES), F32)],
        compiler_params=_cparams(1),
        name="post",
    )(x2, oa, ob, gates, gate_m, scale_f, shift_f, g_ffn, wa, wb, wo, wr2, b_r)


def _experts_kernel(te_ref, nu_ref, ur_ref, ne_ref, ep_ref, xy_in, wg_hbm, wu_hbm, wd_hbm, xy_out,
                    xbuf, ybuf, wg_s, wu_s, wd_s, wg_f, wu_f, wd_f, gsem, ssem, wsem):
    del xy_in
    r = pl.program_id(0)
    last = pl.num_programs(0) - 1
    n_used = nu_ref[0]
    slot = lax.rem(r, 2)

    def unit_row(step, s):
        return pl.multiple_of(ur_ref[step * UNITS_PER_TILE + s], UNIT)

    def start_gathers(step, sl):
        for s in range(UNITS_PER_TILE):
            pltpu.make_async_copy(xy_out.at[pl.ds(unit_row(step, s), UNIT), :],
                                  xbuf.at[sl, pl.ds(s * UNIT, UNIT), :], gsem.at[sl]).start()

    def wait_gathers(sl):
        pltpu.make_async_copy(xy_out.at[pl.ds(0, TM_EXP), :], xbuf.at[sl], gsem.at[sl]).wait()

    def start_scatters(step, sl):
        for s in range(UNITS_PER_TILE):
            pltpu.make_async_copy(ybuf.at[sl, pl.ds(s * UNIT, UNIT), :],
                                  xy_out.at[pl.ds(unit_row(step, s), UNIT), pl.ds(0, D_MODEL)],
                                  ssem.at[sl]).start()

    def wait_scatters(sl):
        pltpu.make_async_copy(ybuf.at[sl], xy_out.at[pl.ds(0, TM_EXP), pl.ds(0, D_MODEL)], ssem.at[sl]).wait()

    @pl.when(r == 0)
    def _():
        start_gathers(0, 0)

    @pl.when(jnp.logical_and(r < n_used, r >= 2))
    def _():
        wait_scatters(slot)

    def weight_copies(e, p):
        return [pltpu.make_async_copy(src.at[e], dst.at[p], wsem.at[p])
                for src, dst in ((wg_hbm, wg_f), (wu_hbm, wu_f), (wd_hbm, wd_f))]

    @pl.when(jnp.logical_and(r < n_used,
                             jnp.logical_or(r == 0, te_ref[r] != te_ref[jnp.maximum(r - 1, 0)])))
    def _():
        e = te_ref[r]
        p = ep_ref[r]

        @pl.when(r == 0)
        def _():
            for cp in weight_copies(e, p):
                cp.start()

        for cp in weight_copies(e, p):
            cp.wait()
        wg_s[...] = wg_f[p].astype(BF16)
        wu_s[...] = wu_f[p].astype(BF16)
        wd_s[...] = wd_f[p].astype(BF16)

        @pl.when(ne_ref[r] >= 0)
        def _():
            for cp in weight_copies(ne_ref[r], 1 - p):
                cp.start()

    @pl.when(r < n_used)
    def _():
        wait_gathers(slot)
        start_gathers(jnp.minimum(r + 1, last), 1 - slot)
        x = xbuf[slot, :, 0:D_MODEL]
        side = xbuf[slot, :, D_MODEL:XY_COLS].astype(F32)
        lane = lax.broadcasted_iota(jnp.int32, side.shape, 1)

        def lanes_sum(a, b):
            return jnp.sum(jnp.where((lane == a) | (lane == b), side, 0.0), axis=-1, keepdims=True)

        is_slot1 = lanes_sum(E1_LANE, E1_LANE) == te_ref[r].astype(F32)
        wrow = jnp.where(is_slot1, lanes_sum(*W1_LANES), lanes_sum(*W2_LANES))
        a = jnp.dot(x, wg_s[...], preferred_element_type=F32)
        u = jnp.dot(x, wu_s[...], preferred_element_type=F32)
        hid = (a * jax.nn.sigmoid(a) * u * wrow).astype(BF16)
        ybuf[slot] = jnp.dot(hid, wd_s[...], preferred_element_type=F32).astype(BF16)
        start_scatters(r, slot)

    @pl.when(r == n_used - 1)
    def _():
        wait_gathers(1 - slot)
        wait_scatters(slot)

        @pl.when(r >= 1)
        def _():
            wait_scatters(1 - slot)


def _experts(tile_expert, n_used, unit_rows, next_expert, expert_parity, xy, wg, wu, wd):
    grid_spec = pltpu.PrefetchScalarGridSpec(
        num_scalar_prefetch=5,
        grid=(N_EXP_TILES,),
        in_specs=[pl.BlockSpec(memory_space=pl.ANY),
                  pl.BlockSpec(memory_space=pl.ANY),
                  pl.BlockSpec(memory_space=pl.ANY),
                  pl.BlockSpec(memory_space=pl.ANY)],
        out_specs=pl.BlockSpec(memory_space=pl.ANY),
        scratch_shapes=[pltpu.VMEM((2, TM_EXP, XY_COLS), BF16),
                        pltpu.VMEM((2, TM_EXP, D_MODEL), BF16),
                        pltpu.VMEM((D_MODEL, D_FF_EXPERT), BF16),
                        pltpu.VMEM((D_MODEL, D_FF_EXPERT), BF16),
                        pltpu.VMEM((D_FF_EXPERT, D_MODEL), BF16),
                        pltpu.VMEM((2, D_MODEL, D_FF_EXPERT), F32),
                        pltpu.VMEM((2, D_MODEL, D_FF_EXPERT), F32),
                        pltpu.VMEM((2, D_FF_EXPERT, D_MODEL), F32),
                        pltpu.SemaphoreType.DMA((2,)),
                        pltpu.SemaphoreType.DMA((2,)),
                        pltpu.SemaphoreType.DMA((2,))])
    return pl.pallas_call(
        _experts_kernel,
        out_shape=jax.ShapeDtypeStruct(xy.shape, xy.dtype),
        grid_spec=grid_spec,
        input_output_aliases={5: 0},
        compiler_params=_cparams(1),
        name="experts",
    )(tile_expert, n_used, unit_rows, next_expert, expert_parity, xy, wg, wu, wd)


def _combine_kernel(x1_ref, rc_ref, gf_ref, gfin_ref, y_ref, o_ref):
    for tl in range(COMBINE_TILES):
        rows = slice(tl * TM_ROW, (tl + 1) * TM_ROW)
        lp1 = rc_ref[rows, 0:1]
        lp2 = rc_ref[rows, 1:2]
        scol = lax.broadcasted_iota(jnp.int32, (TM_ROW, XY_ROWS), 1).astype(F32)
        pick = jnp.where((scol == lp1) | (scol == lp2), 1.0, 0.0).astype(BF16)
        y = jnp.dot(pick, y_ref[tl * XY_ROWS:(tl + 1) * XY_ROWS, :], preferred_element_type=F32)
        xf = x1_ref[rows, :] + gf_ref[...] * y
        rs = lax.rsqrt(jnp.mean(xf * xf, axis=-1, keepdims=True) + EPS)
        o_ref[rows, :] = xf * rs * gfin_ref[...]


def _combine(x1, rcol, gate_f, g_final, xy):
    tm = TM_ROW * COMBINE_TILES
    tpb = SEQ // tm
    row = lambda i: (i, 0)
    return pl.pallas_call(
        _combine_kernel,
        out_shape=jax.ShapeDtypeStruct((N_TOK, D_MODEL), F32),
        grid=(N_TOK // tm,),
        in_specs=[pl.BlockSpec((tm, D_MODEL), row),
                  pl.BlockSpec((tm, LANES), row),
                  pl.BlockSpec((None, 1, D_MODEL), lambda i: (i // tpb, 0, 0)),
                  pl.BlockSpec((1, D_MODEL), lambda i: (0, 0)),
                  pl.BlockSpec((COMBINE_TILES * XY_ROWS, D_MODEL), row)],
        out_specs=pl.BlockSpec((tm, D_MODEL), row),
        compiler_params=_cparams(1),
        name="combine",
    )(x1, rcol, gate_f, g_final, xy)


def _t5_bucket_np():
    qi = np.arange(BLOCK)[:, None]
    kj = np.arange(2 * BLOCK)[None, :]
    dist = qi - kj + BLOCK
    n = np.maximum(dist, 0)
    max_exact = NUM_BUCKETS // 2
    nf = np.maximum(n, 1).astype(np.float32)
    large = max_exact + (np.log(nf / np.float32(max_exact)) / np.float32(math.log(MAX_DISTANCE / max_exact))
                         * np.float32(NUM_BUCKETS - max_exact)).astype(np.int32)
    large = np.minimum(large, NUM_BUCKETS - 1)
    bucket = np.where(n < max_exact, n, large)
    band = (dist >= 0) & (dist < WINDOW)
    return bucket.astype(np.int32), band


def kernel(x, c, w_ada, b_ada, g_norm_mix, g_norm_ffn, w_in, sinks, b_forget, w_proj_swa, w_proj_fox,
           w_out, rel_bias_table, w_router_group, b_router_group, w_router_expert, b_router_expert,
           w_gate_exp, w_up_exp, w_down_exp, g_final):
    l = 0
    x2 = x.reshape(N_TOK, D_MODEL)

    c16 = jnp.concatenate([c, jnp.zeros_like(c)], axis=0)
    mod = _ada(c16, w_ada[l], b_ada[l][None, :])[:BATCH]
    shift_m, scale_m, gate_m, shift_f, scale_f, gate_f = [
        m.reshape(BATCH, 1, D_MODEL) for m in jnp.split(mod, 6, axis=-1)]

    w = w_in[l]
    o_ka, o_va, o_qb = Q_A, Q_A + KV_A, Q_A + 2 * KV_A
    o_kb, o_vb, o_f = o_qb + W_B, o_qb + 2 * W_B, o_qb + 3 * W_B
    o_g = o_f + N_HEADS_FOX

    def dup(cols):
        heads = [cols[:, h * HEAD_DIM:(h + 1) * HEAD_DIM] for h in range(N_KV_HEADS_SWA)]
        return jnp.concatenate([hd for hd in heads for _ in range(2)], axis=1)

    head_order = jnp.argsort(b_forget[l])

    def reorder_heads(cols):
        return jnp.take(cols.reshape(D_MODEL, N_HEADS_FOX, HEAD_DIM), head_order, axis=1).reshape(D_MODEL, W_B)

    w_fox = jnp.concatenate([reorder_heads(w[:, o_qb:o_kb]), reorder_heads(w[:, o_kb:o_vb]),
                             reorder_heads(w[:, o_vb:o_f])], axis=1)
    b_fox = jnp.take(b_forget[l], head_order)
    w_proj_b = jnp.take(w_proj_fox[l].reshape(N_HEADS_FOX, HEAD_DIM, D_MODEL), head_order, axis=0).reshape(W_B, D_MODEL)
    carrier = DECAY_LANES * N_HEADS_FOX
    w_f = jnp.pad(jnp.repeat(jnp.take(w[:, o_f:o_g], head_order, axis=1), DECAY_LANES, axis=1),
                  ((0, 0), (0, LANES - carrier)))
    w_main = jnp.concatenate([w[:, :Q_A], dup(w[:, o_ka:o_va]), w[:, o_va:o_qb], w_f, w_fox], axis=1).astype(BF16)
    w_g = w[:, o_g:].astype(BF16)
    qa, kdup, va, qb, kb, vb, f_pad, gates, nrm = _inproj(
        x2, scale_m, shift_m, g_norm_mix[l][None, :], w_main, w_g)

    b_pad = jnp.pad(jnp.repeat(b_fox, DECAY_LANES), (0, LANES - carrier))[None, :]
    lanes = np.arange(LANES)
    jmod = jnp.asarray(np.where(lanes < DECAY_LANES * N_HEADS_FOX, lanes % DECAY_LANES, 7)[None, :].astype(np.int32))
    dq, dk, fb = _cum(f_pad, b_pad, jmod)

    bucket, band = _t5_bucket_np()
    onehot = jnp.asarray(bucket[None] == np.arange(NUM_BUCKETS)[:, None, None], dtype=F32)
    bias = jnp.einsum("bh,bqk->hqk", rel_bias_table.astype(F32), onehot, precision=HIGHEST)
    bias = jnp.where(band[None], bias * LOG2E, NEG_INF)
    first = np.arange(2 * BLOCK)[None, None, :] < BLOCK
    bias = jnp.stack([jnp.where(first, NEG_INF, bias), bias]).reshape(2, N_KV_HEADS_SWA, -1, 2 * BLOCK)
    o_a = _swa(sinks[l].astype(F32) * LOG2E, qa, kdup, va, bias)

    o_b = _fox(_fox_first_tiles(nrm, fb), qb, kb, vb, dq, dk)

    w_r = jnp.concatenate([w_router_group[l]] + [w_router_expert[l][g] for g in range(N_GROUPS)], axis=1)
    w_r = jnp.pad(w_r, ((0, 0), (0, LANES - w_r.shape[1])))
    wr_hi = w_r.astype(BF16)
    wr_lo = (w_r - wr_hi.astype(F32)).astype(BF16)
    wr2 = jnp.concatenate([wr_hi, wr_lo], axis=1)
    b_r = jnp.concatenate([b_router_group[l], b_router_expert[l].reshape(-1)])
    b_r = jnp.pad(b_r, (0, LANES - b_r.shape[0]))[None, :]
    x1, xy, rcol, cu = _post(x2, o_a, o_b, gates, gate_m, scale_f, shift_f, g_norm_ffn[l][None, :],
                             w_proj_swa[l].astype(BF16), w_proj_b.astype(BF16), w_out[l].astype(BF16),
                             wr2, b_r)

    i32 = jnp.int32
    n_tok_tiles = N_TOK // TM_POST
    cu = cu.reshape(n_tok_tiles, 8, LANES)[:, 0, :N_EXPERTS].astype(i32)
    loc_u = jnp.cumsum(cu, axis=1) - cu
    cend = jnp.cumsum(cu, axis=0)
    cstart = cend - cu
    tot_u = cend[-1]
    tiles_e = (tot_u + UNITS_PER_TILE - 1) // UNITS_PER_TILE
    tile_end = jnp.cumsum(tiles_e)
    tile_start = tile_end - tiles_e
    r = jnp.arange(N_EXP_TILES, dtype=i32)
    tile_expert = jnp.minimum(jnp.sum((tile_end[None, :] <= r[:, None]).astype(i32), axis=1), N_EXPERTS - 1)
    sel_e = tile_expert[:, None] == jnp.arange(N_EXPERTS, dtype=i32)[None, :]
    tw = r - jnp.sum(jnp.where(sel_e, tile_start[None, :], 0), axis=1)
    tot_r = jnp.sum(jnp.where(sel_e, tot_u[None, :], 0), axis=1)
    n_used = tile_end[-1:].astype(i32)
    q = tw[:, None] * UNITS_PER_TILE + jnp.arange(UNITS_PER_TILE, dtype=i32)[None, :]

    def of_expert(tab):
        return jnp.sum(jnp.where(sel_e[:, None, :], tab[None, :, :], 0), axis=2)

    cend_r, cstart_r, loc_r = of_expert(cend), of_expert(cstart), of_expert(loc_u)
    src_tile = jnp.minimum(jnp.sum((cend_r[:, None, :] <= q[:, :, None]).astype(i32), axis=2), n_tok_tiles - 1)
    sel_t = src_tile[:, :, None] == jnp.arange(n_tok_tiles, dtype=i32)[None, None, :]
    k = (q - jnp.sum(jnp.where(sel_t, cstart_r[:, None, :], 0), axis=2)
         + jnp.sum(jnp.where(sel_t, loc_r[:, None, :], 0), axis=2))
    real_rows = src_tile * XY_ROWS + k * UNIT
    pad_rows = PAD_BASE_ROW + (tile_expert[:, None] * PAD_UNITS_PER_EXPERT + (q - tot_r[:, None])) * UNIT
    idle_row = PAD_BASE_ROW + N_EXPERTS * PAD_UNITS_PER_EXPERT * UNIT
    unit_rows = jnp.where(q < tot_r[:, None], real_rows, pad_rows)
    unit_rows = jnp.where((r < n_used)[:, None], unit_rows, idle_row).reshape(-1).astype(i32)

    eid = jnp.arange(N_EXPERTS, dtype=i32)
    used = tiles_e > 0
    later_used = (eid[None, :] > eid[:, None]) & used[None, :]
    next_e = jnp.min(jnp.where(later_used, eid[None, :], N_EXPERTS), axis=1)
    next_e = jnp.where(next_e == N_EXPERTS, -1, next_e)
    parity_e = (jnp.cumsum(used.astype(i32)) - used.astype(i32)) % 2
    next_expert = jnp.sum(jnp.where(sel_e, next_e[None, :], 0), axis=1).astype(i32)
    expert_parity = jnp.sum(jnp.where(sel_e, parity_e[None, :], 0), axis=1).astype(i32)

    xy = _experts(tile_expert.astype(i32), n_used, unit_rows, next_expert, expert_parity, xy,
                  w_gate_exp[l].reshape(N_EXPERTS, D_MODEL, D_FF_EXPERT),
                  w_up_exp[l].reshape(N_EXPERTS, D_MODEL, D_FF_EXPERT),
                  w_down_exp[l].reshape(N_EXPERTS, D_FF_EXPERT, D_MODEL))
    out = _combine(x1, rcol, gate_f, g_final[None, :], xy)
    return out.reshape(BATCH, SEQ, D_MODEL)
```

```python
import math

import numpy as np
import jax
import jax.numpy as jnp
from jax import lax
from jax.experimental import pallas as pl
from jax.experimental.pallas import tpu as pltpu

F32 = jnp.float32
BF16 = jnp.bfloat16
HIGHEST = lax.Precision.HIGHEST

D_MODEL = 1024
BATCH = 8
SEQ = 4096
N_TOK = BATCH * SEQ
N_HEADS_SWA = 8
N_KV_HEADS_SWA = 2
N_HEADS_FOX = 8
HEAD_DIM = 64
WINDOW = 128
BLOCK = 128
NUM_BUCKETS = 32
MAX_DISTANCE = 128
N_GROUPS = 4
EXPERTS_PER_GROUP = 8
N_EXPERTS = N_GROUPS * EXPERTS_PER_GROUP
D_FF_EXPERT = 256
EPS = 1e-6
NEG_INF = -1e30

Q_A = N_HEADS_SWA * HEAD_DIM
KV_A = N_KV_HEADS_SWA * HEAD_DIM
W_B = N_HEADS_FOX * HEAD_DIM
LANES = 128
QK_SCALE = HEAD_DIM ** -0.5

TM_IN = 1024
TM_POST = 512
TQ_FOX = 512
TK_FOX = TQ_FOX
FOX_ROWS = 2
SWA_BLOCKS = 8
TM_EXP = 512
TM_ROW = 512
COMBINE_TILES = 2
UNIT = 16
XY_UNITS = 2 * TM_POST // UNIT + N_EXPERTS
XY_ROWS = XY_UNITS * UNIT
XY_COLS = D_MODEL + LANES
UNITS_PER_TILE = TM_EXP // UNIT
N_TOK_TILES = N_TOK // TM_POST
N_EXP_TILES = N_TOK_TILES * XY_UNITS // UNITS_PER_TILE + N_EXPERTS
PAD_UNITS_PER_EXPERT = UNITS_PER_TILE - 1
PAD_BLOCKS = -(-(N_EXPERTS * PAD_UNITS_PER_EXPERT * UNIT) // XY_ROWS)
PAD_BASE_ROW = N_TOK_TILES * XY_ROWS
W1_LANES, W2_LANES, E1_LANE, E2_LANE = (4, 6), (5, 7), 8, 9
VMEM_LIMIT = 56 * 1024 * 1024

DECAY_LANES = 6
LOG2E = math.log2(math.e)
Q_SCALE_LOG2 = QK_SCALE * LOG2E
PRUNE_MARGIN = 160.0


def _cparams(n_axes):
    return pltpu.CompilerParams(dimension_semantics=("arbitrary",) * n_axes,
                                vmem_limit_bytes=VMEM_LIMIT)


def _ada_kernel(c_ref, w_ref, b_ref, o_ref):
    c = c_ref[...]
    ca = c * jax.nn.sigmoid(c)
    o_ref[...] = jnp.dot(ca.astype(BF16), w_ref[...].astype(BF16),
                         preferred_element_type=F32) + b_ref[...]


def _ada(c16, w_ada, b_ada):
    n_out = w_ada.shape[1]
    blk = 1024
    return pl.pallas_call(
        _ada_kernel,
        out_shape=jax.ShapeDtypeStruct((16, n_out), F32),
        grid=(n_out // blk,),
        in_specs=[pl.BlockSpec((16, D_MODEL), lambda j: (0, 0)),
                  pl.BlockSpec((D_MODEL, blk), lambda j: (0, j)),
                  pl.BlockSpec((1, blk), lambda j: (0, j))],
        out_specs=pl.BlockSpec((16, blk), lambda j: (0, j)),
        compiler_params=_cparams(1),
        name="ada",
    )(c16, w_ada, b_ada)


def _inproj_kernel(x_ref, sc_ref, sh_ref, g_ref, wm_ref, wg_ref, ind_ref,
                   qa_ref, kd_ref, va_ref, qb_ref, kb_ref, vb_ref, f_ref, gt_ref, nrm_ref):
    x = x_ref[...]
    rs = lax.rsqrt(jnp.mean(x * x, axis=-1, keepdims=True) + EPS)
    a = g_ref[...] * (1.0 + sc_ref[...])
    h = (x * rs * a + sh_ref[...]).astype(BF16)

    def mm(w):
        return jnp.dot(h, w, preferred_element_type=F32)

    qa_ref[...] = (mm(wm_ref[:, 0:512]) * Q_SCALE_LOG2).astype(BF16)
    kd_ref[...] = mm(wm_ref[:, 512:768]).astype(BF16)
    vf = mm(wm_ref[:, 768:1024])
    f_ref[...] = vf[:, LANES:2 * LANES]
    v = vf[:, 0:LANES]
    vr = pltpu.roll(v, HEAD_DIM, 1)
    lo = lax.broadcasted_iota(jnp.int32, v.shape, 1) < HEAD_DIM
    va_ref[:, 0:LANES] = jnp.where(lo, v, vr).astype(BF16)
    va_ref[:, LANES:2 * LANES] = jnp.where(lo, vr, v).astype(BF16)
    qb = (mm(wm_ref[:, 1024:1536]) * Q_SCALE_LOG2).astype(BF16)
    kb = mm(wm_ref[:, 1536:2048]).astype(BF16)
    qb_ref[...] = qb
    kb_ref[...] = kb
    vb_ref[...] = mm(wm_ref[:, 2048:2560]).astype(BF16)
    sq = jnp.concatenate([qb, kb], axis=1).astype(F32)
    seg = jnp.dot((sq * sq).astype(BF16), ind_ref[...], preferred_element_type=F32)
    for sub in range(TM_IN // TQ_FOX):
        tile_max = jnp.max(seg[sub * TQ_FOX:(sub + 1) * TQ_FOX], axis=0, keepdims=True)
        nrm_ref[8 * sub:8 * sub + 8, :] = jnp.broadcast_to(tile_max, (8, LANES))
    gt_ref[...] = mm(wg_ref[...]).astype(BF16)


def _inproj(x2, scale_m, shift_m, g_mix, w_main, w_g):
    tm = TM_IN
    tpb = SEQ // tm
    row = lambda i: (i, 0)
    per_b = lambda i: (i // tpb, 0, 0)
    const = lambda i: (0, 0)
    outs = [(Q_A, BF16), (2 * KV_A, BF16), (2 * KV_A, BF16), (W_B, BF16), (W_B, BF16), (W_B, BF16),
            (LANES, F32), (2 * D_MODEL, BF16)]
    ind_np = np.zeros((2 * W_B, LANES), np.float32)
    ind_np[np.arange(2 * W_B), np.arange(2 * W_B) // HEAD_DIM] = 1.0
    ind = jnp.asarray(ind_np, dtype=BF16)
    n_steps = N_TOK // tm
    nrm_rows = 8 * (tm // TQ_FOX)
    once = pl.Buffered(1)
    return pl.pallas_call(
        _inproj_kernel,
        out_shape=[jax.ShapeDtypeStruct((N_TOK, w), dt) for w, dt in outs]
        + [jax.ShapeDtypeStruct((n_steps * nrm_rows, LANES), F32)],
        grid=(n_steps,),
        in_specs=[pl.BlockSpec((tm, D_MODEL), row),
                  pl.BlockSpec((None, 1, D_MODEL), per_b),
                  pl.BlockSpec((None, 1, D_MODEL), per_b),
                  pl.BlockSpec((1, D_MODEL), const),
                  pl.BlockSpec(w_main.shape, const, pipeline_mode=once),
                  pl.BlockSpec(w_g.shape, const, pipeline_mode=once),
                  pl.BlockSpec(ind.shape, const, pipeline_mode=once)],
        out_specs=[pl.BlockSpec((tm, w), row) for w, _ in outs] + [pl.BlockSpec((nrm_rows, LANES), row)],
        compiler_params=_cparams(1),
        name="inproj",
    )(x2, scale_m, shift_m, g_mix, w_main, w_g, ind)


def _log_sigmoid(x):
    return jnp.minimum(x, 0.0) - jnp.log1p(jnp.exp(-jnp.abs(x)))


def _cum_kernel(f_ref, b_ref, jm_ref, qa_ref, ka_ref, fb_ref):
    cum = _log_sigmoid(f_ref[...] + b_ref[...]) * LOG2E
    row = lax.broadcasted_iota(jnp.int32, cum.shape, 0)
    k = 1
    while k < SEQ:
        if k < 8:
            shifted = jnp.where(row >= k, pltpu.roll(cum, k, 0), 0.0)
        else:
            shifted = jnp.concatenate([jnp.zeros((k, LANES), F32), cum[:SEQ - k]], axis=0)
        cum = cum + shifted
        k *= 2
    jm = jm_ref[...]
    for blk in range(SEQ // LANES):
        rows = slice(blk * LANES, (blk + 1) * LANES)
        cb = cum[rows]
        carry = cb[LANES - 1:LANES]
        hi = cb.astype(BF16).astype(F32)
        r1 = cb - hi
        mid = r1.astype(BF16).astype(F32)
        lo = (r1 - mid).astype(BF16).astype(F32)
        one = jnp.ones_like(cb)
        zero = jnp.zeros_like(cb)
        qa = jnp.where(jm == 0, hi, jnp.where(jm == 1, mid, jnp.where(jm == 2, lo,
                       jnp.where(jm < DECAY_LANES, one, zero))))
        ka = jnp.where(jm == 3, -hi, jnp.where(jm == 4, -mid, jnp.where(jm == 5, -lo,
                       jnp.where(jm < 3, one, zero))))
        qa_ref[rows, :] = qa.astype(BF16)
        ka_ref[rows, :] = ka.astype(BF16)
        blocks_per_tile = TQ_FOX // LANES
        tile = blk // blocks_per_tile
        if blk % blocks_per_tile == 0:
            fb_ref[2 * tile:2 * tile + 1, :] = cb[0:1]
        if blk % blocks_per_tile == blocks_per_tile - 1:
            fb_ref[2 * tile + 1:2 * tile + 2, :] = carry


def _cum(f_pad, b_pad, jmod):
    n_tiles = SEQ // TQ_FOX
    return pl.pallas_call(
        _cum_kernel,
        out_shape=[jax.ShapeDtypeStruct((BATCH, SEQ, LANES), BF16)] * 2
        + [jax.ShapeDtypeStruct((BATCH, 2 * n_tiles, LANES), F32)],
        grid=(BATCH,),
        in_specs=[pl.BlockSpec((SEQ, LANES), lambda b: (b, 0)),
                  pl.BlockSpec((1, LANES), lambda b: (0, 0)),
                  pl.BlockSpec((1, LANES), lambda b: (0, 0))],
        out_specs=[pl.BlockSpec((None, SEQ, LANES), lambda b: (b, 0, 0))] * 2
        + [pl.BlockSpec((None, 2 * n_tiles, LANES), lambda b: (b, 0, 0))],
        compiler_params=_cparams(1),
        name="cum",
    )(f_pad, b_pad, jmod)


def _swa_block(sink_cols, q, kk, vv, bias_ref, lo):
    tiles = []
    for g in range(N_KV_HEADS_SWA):
        parts = []
        for t in range(2):
            qt = q[:, (2 * g + t) * LANES:(2 * g + t + 1) * LANES]
            zero = jnp.zeros_like(qt)
            parts.append(jnp.where(lo, qt, zero))
            parts.append(jnp.where(lo, zero, qt))
        q4 = jnp.concatenate(parts, axis=0)
        s = lax.dot_general(q4, kk[:, g * LANES:(g + 1) * LANES], (((1,), (1,)), ((), ())),
                            preferred_element_type=F32)
        s = s + bias_ref[g]
        sink = sink_cols[g]
        m = jnp.maximum(jnp.max(s, axis=-1, keepdims=True), sink)
        p = jnp.exp2(s - m)
        den = jnp.sum(p, axis=-1, keepdims=True) + jnp.exp2(sink - m)
        o = jnp.dot(p.astype(BF16), vv[:, g * LANES:(g + 1) * LANES],
                    preferred_element_type=F32) / den
        tiles.append(jnp.where(lo, o[0:BLOCK], o[BLOCK:2 * BLOCK]))
        tiles.append(jnp.where(lo, o[2 * BLOCK:3 * BLOCK], o[3 * BLOCK:4 * BLOCK]))
    return tiles


def _swa_kernel(sink_ref, q_ref, kc_ref, kp_ref, vc_ref, vp_ref, bias_first_ref, bias_ref, o_ref):
    lane = lax.broadcasted_iota(jnp.int32, (BLOCK, LANES), 1)
    lo = lane < HEAD_DIM
    grp = N_HEADS_SWA // N_KV_HEADS_SWA
    row = lax.broadcasted_iota(jnp.int32, (grp * BLOCK, 1), 0)
    sink_cols = []
    for g in range(N_KV_HEADS_SWA):
        col = jnp.full((grp * BLOCK, 1), sink_ref[g * grp + grp - 1], F32)
        for hh in range(grp - 2, -1, -1):
            col = jnp.where(row < (hh + 1) * BLOCK, sink_ref[g * grp + hh], col)
        sink_cols.append(col)
    for blk in range(SWA_BLOCKS):
        rows = slice(blk * BLOCK, (blk + 1) * BLOCK)
        if blk == 0:
            kk = jnp.concatenate([kp_ref[...], kc_ref[rows, :]], axis=0)
            vv = jnp.concatenate([vp_ref[...], vc_ref[rows, :]], axis=0)
            bias = bias_first_ref
        else:
            prev_rows = slice((blk - 1) * BLOCK, (blk + 1) * BLOCK)
            kk = kc_ref[prev_rows, :]
            vv = vc_ref[prev_rows, :]
            bias = bias_ref
        tiles = _swa_block(sink_cols, q_ref[rows, :], kk, vv, bias, lo)
        for c, tile in enumerate(tiles):
            o_ref[rows, c * LANES:(c + 1) * LANES] = tile.astype(BF16)


def _swa(sinks, qa, kdup, va, bias):
    nb = SEQ // BLOCK
    ns = nb // SWA_BLOCKS
    cur = lambda b, i, s: (b * ns + i, 0)
    prev = lambda b, i, s: (b * nb + jnp.maximum(SWA_BLOCKS * i - 1, 0), 0)
    grid_spec = pltpu.PrefetchScalarGridSpec(
        num_scalar_prefetch=1,
        grid=(BATCH, ns),
        in_specs=[pl.BlockSpec((SWA_BLOCKS * BLOCK, Q_A), cur),
                  pl.BlockSpec((SWA_BLOCKS * BLOCK, 2 * KV_A), cur),
                  pl.BlockSpec((BLOCK, 2 * KV_A), prev),
                  pl.BlockSpec((SWA_BLOCKS * BLOCK, 2 * KV_A), cur),
                  pl.BlockSpec((BLOCK, 2 * KV_A), prev),
                  pl.BlockSpec((None,) + bias.shape[1:], lambda b, i, s: (jnp.minimum(i, 1), 0, 0, 0)),
                  pl.BlockSpec((None,) + bias.shape[1:], lambda b, i, s: (1, 0, 0, 0))],
        out_specs=pl.BlockSpec((SWA_BLOCKS * BLOCK, Q_A), cur))
    return pl.pallas_call(
        _swa_kernel,
        out_shape=jax.ShapeDtypeStruct((N_TOK, Q_A), BF16),
        grid_spec=grid_spec,
        compiler_params=_cparams(2),
        name="swa",
    )(sinks, qa, kdup, kdup, va, va, bias, bias)


def _fox_kernel(js_ref, q_ref, k_ref, v_ref, qa_ref, ka_ref, o_ref,
                kaug, vaug, q2, m_sc, acc_sc, s_0, s_1, s_2, s_3):
    tq, tk = TQ_FOX, TK_FOX
    b = pl.program_id(0)
    t = pl.program_id(1)
    g = pl.program_id(2)
    first_tile = (b * pl.num_programs(1) + t) * (SEQ // tq) + g * FOX_ROWS

    @pl.when(g == 0)
    def _():
        kaug[:, 0:LANES] = k_ref[...]
        kaug[:, LANES:2 * LANES] = ka_ref[...]
        vaug[:, 0:LANES] = v_ref[...]
        vaug[:, LANES:2 * LANES] = jnp.ones((SEQ, LANES), BF16)

    lane = lax.broadcasted_iota(jnp.int32, (tq, LANES), 1)
    lo = lane < HEAD_DIM
    base = 2 * DECAY_LANES * t
    own = [(lane >= base + h * DECAY_LANES) & (lane < base + (h + 1) * DECAY_LANES) for h in range(2)]
    for rw in range(FOX_ROWS):
        rows = slice(rw * tq, (rw + 1) * tq)
        q = q_ref[rows, :]
        qa = qa_ref[rows, :]
        zero = jnp.zeros_like(q)
        q2[rw, 0, :, 0:LANES] = jnp.where(lo, q, zero)
        q2[rw, 1, :, 0:LANES] = jnp.where(lo, zero, q)
        for h in range(2):
            q2[rw, h, :, LANES:2 * LANES] = jnp.where(own[h], qa, zero)

    rr = lax.broadcasted_iota(jnp.int32, (tq, tk), 0)
    cc = lax.broadcasted_iota(jnp.int32, (tq, tk), 1)
    causal = cc <= rr
    bufs = ((s_0, s_1), (s_2, s_3))
    j_starts = [js_ref[first_tile + rw] for rw in range(FOX_ROWS)]
    for rw in range(FOX_ROWS):
        nxt = rw + 1 if rw + 1 < FOX_ROWS else None
        _fox_query_tile(rw, g * FOX_ROWS + rw, j_starts[rw], bufs[rw % 2], q2, kaug, vaug, m_sc, acc_sc,
                        causal, lo, o_ref,
                        first_scores_done=rw > 0,
                        next_first=None if nxt is None else (nxt, j_starts[nxt], bufs[nxt % 2][0]))


def _fox_query_tile(rw, i, j_start, buf_pair, q2, kaug, vaug, m_sc, acc_sc, causal, lo, o_ref,
                    first_scores_done, next_first):
    tq, tk = TQ_FOX, TK_FOX
    s_a, s_b = buf_pair
    m_sc[...] = jnp.full(m_sc.shape, NEG_INF, F32)
    acc_sc[...] = jnp.zeros(acc_sc.shape, F32)

    def scores_of(row, h, ks):
        return lax.dot_general(q2[row, h], kaug[pl.ds(ks, tk), :], (((1,), (1,)), ((), ())),
                               preferred_element_type=F32)

    def scores(h, ks):
        return scores_of(rw, h, ks)

    def consume(h, s, ks, mask):
        if mask is not None:
            s = jnp.where(mask, s, NEG_INF)
        m_prev = m_sc[h]
        m_new = jnp.maximum(m_prev, jnp.max(s, axis=-1, keepdims=True))
        alpha = jnp.exp2(m_prev - m_new)
        p = jnp.exp2(s - jnp.concatenate([m_new] * (tk // LANES), axis=1))
        pv = jnp.dot(p.astype(BF16), vaug[pl.ds(ks, tk), :], preferred_element_type=F32)
        acc_sc[h] = jnp.concatenate([alpha, alpha], axis=1) * acc_sc[h] + pv
        m_sc[h] = m_new

    def key_start(j):
        return pl.multiple_of(j * tk, tk)

    def scores_into(buf, j):
        for h in range(2):
            buf[h] = scores(h, key_start(j))

    def consume_from(buf, j, mask):
        for h in range(2):
            consume(h, buf[h], key_start(j), mask)

    n_full = i - j_start

    if not first_scores_done:
        scores_into(s_a, j_start)

    def pair(p, carry):
        j = j_start + 2 * p
        scores_into(s_b, j + 1)
        consume_from(s_a, j, None)
        scores_into(s_a, j + 2)
        consume_from(s_b, j + 1, None)
        return carry

    lax.fori_loop(0, n_full // 2, pair, 0)
    odd = lax.rem(n_full, 2) == 1

    def start_next():
        if next_first is not None:
            nrow, nj, nbuf = next_first
            for h in range(2):
                nbuf[h] = scores_of(nrow, h, key_start(nj))

    @pl.when(odd)
    def _():
        scores_into(s_b, i)
        consume_from(s_a, i - 1, None)
        start_next()
        consume_from(s_b, i, causal)

    @pl.when(jnp.logical_not(odd))
    def _():
        start_next()
        consume_from(s_a, i, causal)

    outs = [acc_sc[h, :, 0:LANES] / acc_sc[h, :, LANES:2 * LANES] for h in range(2)]
    o_ref[rw * tq:(rw + 1) * tq, :] = jnp.where(lo, outs[0], outs[1]).astype(BF16)


def _fox(j_start, qb, kb, vb, qa, ka):
    tq = TQ_FOX
    ns = SEQ // (tq * FOX_ROWS)
    n_pairs = N_HEADS_FOX // 2
    qmap = lambda b, t, i, js: (b * ns + i, t)
    kmap = lambda b, t, i, js: (b, t)
    grid_spec = pltpu.PrefetchScalarGridSpec(
        num_scalar_prefetch=1,
        grid=(BATCH, n_pairs, ns),
        in_specs=[pl.BlockSpec((FOX_ROWS * tq, LANES), qmap),
                  pl.BlockSpec((SEQ, LANES), kmap),
                  pl.BlockSpec((SEQ, LANES), kmap),
                  pl.BlockSpec((None, FOX_ROWS * tq, LANES), lambda b, t, i, js: (b, i, 0)),
                  pl.BlockSpec((None, SEQ, LANES), lambda b, t, i, js: (b, 0, 0))],
        out_specs=pl.BlockSpec((FOX_ROWS * tq, LANES), qmap),
        scratch_shapes=[pltpu.VMEM((SEQ, 2 * LANES), BF16),
                        pltpu.VMEM((SEQ, 2 * LANES), BF16),
                        pltpu.VMEM((FOX_ROWS, 2, tq, 2 * LANES), BF16),
                        pltpu.VMEM((2, tq, LANES), F32),
                        pltpu.VMEM((2, tq, 2 * LANES), F32)]
        + [pltpu.VMEM((2, tq, TK_FOX), F32)] * 4)
    return pl.pallas_call(
        _fox_kernel,
        out_shape=jax.ShapeDtypeStruct((N_TOK, W_B), BF16),
        grid_spec=grid_spec,
        compiler_params=_cparams(3),
        name="fox",
    )(j_start, qb, kb, vb, qa, ka)


def _fox_first_tiles(nrm, fb):
    n_tiles = SEQ // TQ_FOX
    nr = nrm.reshape(BATCH, n_tiles, 8, LANES)[:, :, 0, :] * 1.02
    qn = jnp.sqrt(nr[..., 0:N_HEADS_FOX])
    kn = jnp.sqrt(nr[..., N_HEADS_FOX:2 * N_HEADS_FOX])
    f_first = fb[:, 0::2, 0:DECAY_LANES * N_HEADS_FOX:DECAY_LANES]
    f_last = fb[:, 1::2, 0:DECAY_LANES * N_HEADS_FOX:DECAY_LANES]
    kn_prefix = lax.cummax(kn, axis=1)
    upper = qn[:, :, None, :] * kn_prefix[:, None, :, :] + f_first[:, :, None, :] - f_last[:, None, :, :]
    row_max_low = -(qn * kn)[:, :, None, :]
    ii = jnp.arange(n_tiles)[None, :, None, None]
    jj = jnp.arange(n_tiles)[None, None, :, None]
    skip = (upper < row_max_low - PRUNE_MARGIN) & (jj < ii)
    skip = jnp.all(skip.reshape(BATCH, n_tiles, n_tiles, N_HEADS_FOX // 2, 2), axis=-1)
    first = jnp.sum(jnp.cumprod(skip.astype(jnp.int32), axis=2), axis=2)
    return jnp.transpose(first, (0, 2, 1)).reshape(-1).astype(jnp.int32)


def _post_kernel(x_ref, oa_ref, ob_ref, gt_ref, gm_ref, sc_ref, sh_ref, g_ref,
                 wa_ref, wb_ref, wo_ref, wr2_ref, br_ref,
                 x1_ref, xy_ref, rc_ref, cu_ref, hh_prev, lg_prev):
    step = pl.program_id(0)

    @pl.when(step == 0)
    def _():
        hh_prev[...] = jnp.zeros(hh_prev.shape, BF16)
        lg_prev[...] = jnp.zeros(lg_prev.shape, F32)

    @pl.when(step <= N_TOK_TILES)
    def _():
        hh_p = hh_prev[...]
        lg_p = lg_prev[...]
        hh, logits = _post_mix(x_ref, oa_ref, ob_ref, gt_ref, gm_ref, sc_ref, sh_ref, g_ref,
                               wa_ref, wb_ref, wo_ref, wr2_ref, br_ref, x1_ref)
        _post_route(hh_p, lg_p, xy_ref, rc_ref, cu_ref)
        hh_prev[...] = hh
        lg_prev[...] = logits

    @pl.when(step > N_TOK_TILES)
    def _():
        xy_ref[...] = jnp.zeros(xy_ref.shape, BF16)


def _post_mix(x_ref, oa_ref, ob_ref, gt_ref, gm_ref, sc_ref, sh_ref, g_ref,
              wa_ref, wb_ref, wo_ref, wr2_ref, br_ref, x1_ref):
    pa = jnp.dot(oa_ref[...], wa_ref[...], preferred_element_type=F32)
    pb = jnp.dot(ob_ref[...], wb_ref[...], preferred_element_type=F32)
    ga = jax.nn.sigmoid(gt_ref[:, 0:D_MODEL].astype(F32))
    gb = jax.nn.sigmoid(gt_ref[:, D_MODEL:2 * D_MODEL].astype(F32))
    merged = (ga * pa + gb * pb).astype(BF16)
    y = jnp.dot(merged, wo_ref[...], preferred_element_type=F32)
    x1 = x_ref[...] + gm_ref[...] * y
    x1_ref[...] = x1

    rs = lax.rsqrt(jnp.mean(x1 * x1, axis=-1, keepdims=True) + EPS)
    a = g_ref[...] * (1.0 + sc_ref[...])
    h2 = x1 * rs * a + sh_ref[...]

    hh = h2.astype(BF16)
    hl = (h2 - hh.astype(F32)).astype(BF16)
    hi_both = jnp.dot(hh, wr2_ref[...], preferred_element_type=F32)
    logits = (hi_both[:, 0:LANES] + hi_both[:, LANES:2 * LANES]
              + jnp.dot(hl, wr2_ref[:, 0:LANES], preferred_element_type=F32)
              + br_ref[...])
    return hh, logits


def _post_route(hh, logits, xy_ref, rc_ref, cu_ref):
    tm = TM_POST
    lane = lax.broadcasted_iota(jnp.int32, (tm, LANES), 1).astype(F32)
    big = float(LANES)
    gl = jnp.where(lane < N_GROUPS, logits, -jnp.inf)
    gmax = jnp.max(gl, axis=-1, keepdims=True)
    gi = jnp.min(jnp.where(gl == gmax, lane, big), axis=-1, keepdims=True)
    gsum = jnp.sum(jnp.exp(gl - gmax), axis=-1, keepdims=True)
    gp = 1.0 / gsum
    e_lo = N_GROUPS + EXPERTS_PER_GROUP * gi
    el = jnp.where((lane >= e_lo) & (lane < e_lo + EXPERTS_PER_GROUP), logits, -jnp.inf)
    v1 = jnp.max(el, axis=-1, keepdims=True)
    i1 = jnp.min(jnp.where(el == v1, lane, big), axis=-1, keepdims=True)
    el2 = jnp.where(lane == i1, -jnp.inf, el)
    v2 = jnp.max(el2, axis=-1, keepdims=True)
    i2 = jnp.min(jnp.where(el2 == v2, lane, big), axis=-1, keepdims=True)
    e21 = jnp.exp(v2 - v1)
    w1 = gp / (1.0 + e21)
    w2 = gp * e21 / (1.0 + e21)
    e1 = i1 - N_GROUPS
    e2 = i2 - N_GROUPS

    oh = jnp.where((lane == e1) | (lane == e2), 1.0, 0.0)
    cnt_u = jnp.floor((jnp.sum(oh, axis=0, keepdims=True) + (UNIT - 1)) * (1.0 / UNIT))
    r128 = lax.broadcasted_iota(jnp.int32, (LANES, LANES), 0)
    c128 = lax.broadcasted_iota(jnp.int32, (LANES, LANES), 1)
    before_lane = jnp.where(r128 < c128, 1.0, 0.0).astype(BF16)
    loc_u = jnp.dot(jnp.broadcast_to(cnt_u, (8, LANES)).astype(BF16), before_lane,
                    preferred_element_type=F32)
    trow = lax.broadcasted_iota(jnp.int32, (tm, LANES), 0)
    seen = oh
    k = 1
    while k < tm:
        if k < 8:
            shifted = jnp.where(trow >= k, pltpu.roll(seen, k, 0), 0.0)
        else:
            shifted = jnp.concatenate([jnp.zeros((k, LANES), F32), seen[:tm - k]], axis=0)
        seen = seen + shifted
        k *= 2
    pos_e = (seen - oh) + loc_u[0:1] * UNIT
    lp1 = jnp.sum(jnp.where(lane == e1, pos_e, 0.0), axis=-1, keepdims=True)
    lp2 = jnp.sum(jnp.where(lane == e2, pos_e, 0.0), axis=-1, keepdims=True)

    def to_row(col):
        return jnp.transpose(jnp.broadcast_to(col, (tm, LANES)))[0:1]

    srow = lax.broadcasted_iota(jnp.int32, (XY_ROWS, tm), 0).astype(F32)
    pm1 = jnp.where(srow == to_row(lp1), 1.0, 0.0).astype(BF16)
    pm2 = jnp.where(srow == to_row(lp2), 1.0, 0.0).astype(BF16)
    w1h = w1.astype(BF16).astype(F32)
    w2h = w2.astype(BF16).astype(F32)
    side = jnp.where(lane == W1_LANES[0], w1h, jnp.where(lane == W1_LANES[1], w1 - w1h,
           jnp.where(lane == W2_LANES[0], w2h, jnp.where(lane == W2_LANES[1], w2 - w2h,
           jnp.where(lane == E1_LANE, e1, jnp.where(lane == E2_LANE, e2, 0.0))))))
    tok = jnp.concatenate([hh, side.astype(BF16)], axis=1)
    xy_ref[...] = jnp.dot(pm1 + pm2, tok, preferred_element_type=F32).astype(BF16)

    cu_ref[...] = jnp.broadcast_to(cnt_u, cu_ref.shape)
    rc_ref[...] = jnp.where(lane == 0, lp1, jnp.where(lane == 1, lp2, 0.0))


def _post(x2, oa, ob, gates, gate_m, scale_f, shift_f, g_ffn, wa, wb, wo, wr2, b_r):
    tm = TM_POST
    tpb = SEQ // tm
    n_steps = N_TOK_TILES
    row = lambda i: (jnp.minimum(i, n_steps - 1), 0)
    per_b = lambda i: (jnp.minimum(i, n_steps - 1) // tpb, 0, 0)
    routed = lambda i: (jnp.clip(i - 1, 0, n_steps - 1), 0)
    const = lambda i: (0, 0)
    return pl.pallas_call(
        _post_kernel,
        out_shape=[jax.ShapeDtypeStruct((N_TOK, D_MODEL), F32),
                   jax.ShapeDtypeStruct(((n_steps + PAD_BLOCKS) * XY_ROWS, XY_COLS), BF16),
                   jax.ShapeDtypeStruct((N_TOK, LANES), F32),
                   jax.ShapeDtypeStruct((n_steps * 8, LANES), F32)],
        grid=(n_steps + 1 + PAD_BLOCKS,),
        in_specs=[pl.BlockSpec((tm, D_MODEL), row),
                  pl.BlockSpec((tm, Q_A), row),
                  pl.BlockSpec((tm, W_B), row),
                  pl.BlockSpec((tm, 2 * D_MODEL), row),
                  pl.BlockSpec((None, 1, D_MODEL), per_b),
                  pl.BlockSpec((None, 1, D_MODEL), per_b),
                  pl.BlockSpec((None, 1, D_MODEL), per_b),
                  pl.BlockSpec((1, D_MODEL), const),
                  pl.BlockSpec(wa.shape, const),
                  pl.BlockSpec(wb.shape, const),
                  pl.BlockSpec(wo.shape, const),
                  pl.BlockSpec(wr2.shape, const),
                  pl.BlockSpec((1, LANES), const)],
        out_specs=[pl.BlockSpec((tm, D_MODEL), row),
                   pl.BlockSpec((XY_ROWS, XY_COLS), lambda i: (jnp.maximum(i - 1, 0), 0)),
                   pl.BlockSpec((tm, LANES), routed),
                   pl.BlockSpec((8, LANES), routed)],
        scratch_shapes=[pltpu.VMEM((tm, D_MODEL), BF16),
                        pltpu.VMEM((tm, LANES), F32)],
        compiler_params=_cparams(1),
        name="post",
    )(x2, oa, ob, gates, gate_m, scale_f, shift_f, g_ffn, wa, wb, wo, wr2, b_r)


def _experts_kernel(te_ref, nu_ref, ur_ref, ne_ref, ep_ref, xy_in, wg_hbm, wu_hbm, wd_hbm, xy_out,
                    xbuf, ybuf, wg_s, wu_s, wd_s, wg_f, wu_f, wd_f, gsem, ssem, wsem):
    del xy_in
    r = pl.program_id(0)
    last = pl.num_programs(0) - 1
    n_used = nu_ref[0]
    slot = lax.rem(r, 2)

    def unit_row(step, s):
        return pl.multiple_of(ur_ref[step * UNITS_PER_TILE + s], UNIT)

    def start_gathers(step, sl):
        for s in range(UNITS_PER_TILE):
            pltpu.make_async_copy(xy_out.at[pl.ds(unit_row(step, s), UNIT), :],
                                  xbuf.at[sl, pl.ds(s * UNIT, UNIT), :], gsem.at[sl]).start()

    def wait_gathers(sl):
        pltpu.make_async_copy(xy_out.at[pl.ds(0, TM_EXP), :], xbuf.at[sl], gsem.at[sl]).wait()

    def start_scatters(step, sl):
        for s in range(UNITS_PER_TILE):
            pltpu.make_async_copy(ybuf.at[sl, pl.ds(s * UNIT, UNIT), :],
                                  xy_out.at[pl.ds(unit_row(step, s), UNIT), pl.ds(0, D_MODEL)],
                                  ssem.at[sl]).start()

    def wait_scatters(sl):
        pltpu.make_async_copy(ybuf.at[sl], xy_out.at[pl.ds(0, TM_EXP), pl.ds(0, D_MODEL)], ssem.at[sl]).wait()

    @pl.when(r == 0)
    def _():
        start_gathers(0, 0)

    @pl.when(jnp.logical_and(r < n_used, r >= 2))
    def _():
        wait_scatters(slot)

    def weight_copies(e, p):
        return [pltpu.make_async_copy(src.at[e], dst.at[p], wsem.at[p])
                for src, dst in ((wg_hbm, wg_f), (wu_hbm, wu_f), (wd_hbm, wd_f))]

    @pl.when(jnp.logical_and(r < n_used,
                             jnp.logical_or(r == 0, te_ref[r] != te_ref[jnp.maximum(r - 1, 0)])))
    def _():
        e = te_ref[r]
        p = ep_ref[r]

        @pl.when(r == 0)
        def _():
            for cp in weight_copies(e, p):
                cp.start()

        for cp in weight_copies(e, p):
            cp.wait()
        wg_s[...] = wg_f[p].astype(BF16)
        wu_s[...] = wu_f[p].astype(BF16)
        wd_s[...] = wd_f[p].astype(BF16)

        @pl.when(ne_ref[r] >= 0)
        def _():
            for cp in weight_copies(ne_ref[r], 1 - p):
                cp.start()

    @pl.when(r < n_used)
    def _():
        wait_gathers(slot)
        start_gathers(jnp.minimum(r + 1, last), 1 - slot)
        x = xbuf[slot, :, 0:D_MODEL]
        side = xbuf[slot, :, D_MODEL:XY_COLS].astype(F32)
        lane = lax.broadcasted_iota(jnp.int32, side.shape, 1)

        def lanes_sum(a, b):
            return jnp.sum(jnp.where((lane == a) | (lane == b), side, 0.0), axis=-1, keepdims=True)

        is_slot1 = lanes_sum(E1_LANE, E1_LANE) == te_ref[r].astype(F32)
        wrow = jnp.where(is_slot1, lanes_sum(*W1_LANES), lanes_sum(*W2_LANES))
        a = jnp.dot(x, wg_s[...], preferred_element_type=F32)
        u = jnp.dot(x, wu_s[...], preferred_element_type=F32)
        hid = (a * jax.nn.sigmoid(a) * u * wrow).astype(BF16)
        ybuf[slot] = jnp.dot(hid, wd_s[...], preferred_element_type=F32).astype(BF16)
        start_scatters(r, slot)

    @pl.when(r == n_used - 1)
    def _():
        wait_gathers(1 - slot)
        wait_scatters(slot)

        @pl.when(r >= 1)
        def _():
            wait_scatters(1 - slot)


def _experts(tile_expert, n_used, unit_rows, next_expert, expert_parity, xy, wg, wu, wd):
    grid_spec = pltpu.PrefetchScalarGridSpec(
        num_scalar_prefetch=5,
        grid=(N_EXP_TILES,),
        in_specs=[pl.BlockSpec(memory_space=pl.ANY),
                  pl.BlockSpec(memory_space=pl.ANY),
                  pl.BlockSpec(memory_space=pl.ANY),
                  pl.BlockSpec(memory_space=pl.ANY)],
        out_specs=pl.BlockSpec(memory_space=pl.ANY),
        scratch_shapes=[pltpu.VMEM((2, TM_EXP, XY_COLS), BF16),
                        pltpu.VMEM((2, TM_EXP, D_MODEL), BF16),
                        pltpu.VMEM((D_MODEL, D_FF_EXPERT), BF16),
                        pltpu.VMEM((D_MODEL, D_FF_EXPERT), BF16),
                        pltpu.VMEM((D_FF_EXPERT, D_MODEL), BF16),
                        pltpu.VMEM((2, D_MODEL, D_FF_EXPERT), F32),
                        pltpu.VMEM((2, D_MODEL, D_FF_EXPERT), F32),
                        pltpu.VMEM((2, D_FF_EXPERT, D_MODEL), F32),
                        pltpu.SemaphoreType.DMA((2,)),
                        pltpu.SemaphoreType.DMA((2,)),
                        pltpu.SemaphoreType.DMA((2,))])
    return pl.pallas_call(
        _experts_kernel,
        out_shape=jax.ShapeDtypeStruct(xy.shape, xy.dtype),
        grid_spec=grid_spec,
        input_output_aliases={5: 0},
        compiler_params=_cparams(1),
        name="experts",
    )(tile_expert, n_used, unit_rows, next_expert, expert_parity, xy, wg, wu, wd)


def _combine_kernel(x1_ref, rc_ref, gf_ref, gfin_ref, y_ref, o_ref):
    for tl in range(COMBINE_TILES):
        rows = slice(tl * TM_ROW, (tl + 1) * TM_ROW)
        lp1 = rc_ref[rows, 0:1]
        lp2 = rc_ref[rows, 1:2]
        scol = lax.broadcasted_iota(jnp.int32, (TM_ROW, XY_ROWS), 1).astype(F32)
        pick = jnp.where((scol == lp1) | (scol == lp2), 1.0, 0.0).astype(BF16)
        y = jnp.dot(pick, y_ref[tl * XY_ROWS:(tl + 1) * XY_ROWS, :], preferred_element_type=F32)
        xf = x1_ref[rows, :] + gf_ref[...] * y
        rs = lax.rsqrt(jnp.mean(xf * xf, axis=-1, keepdims=True) + EPS)
        o_ref[rows, :] = xf * rs * gfin_ref[...]


def _combine(x1, rcol, gate_f, g_final, xy):
    tm = TM_ROW * COMBINE_TILES
    tpb = SEQ // tm
    row = lambda i: (i, 0)
    return pl.pallas_call(
        _combine_kernel,
        out_shape=jax.ShapeDtypeStruct((N_TOK, D_MODEL), F32),
        grid=(N_TOK // tm,),
        in_specs=[pl.BlockSpec((tm, D_MODEL), row),
                  pl.BlockSpec((tm, LANES), row),
                  pl.BlockSpec((None, 1, D_MODEL), lambda i: (i // tpb, 0, 0)),
                  pl.BlockSpec((1, D_MODEL), lambda i: (0, 0)),
                  pl.BlockSpec((COMBINE_TILES * XY_ROWS, D_MODEL), row)],
        out_specs=pl.BlockSpec((tm, D_MODEL), row),
        compiler_params=_cparams(1),
        name="combine",
    )(x1, rcol, gate_f, g_final, xy)


def _t5_bucket_np():
    qi = np.arange(BLOCK)[:, None]
    kj = np.arange(2 * BLOCK)[None, :]
    dist = qi - kj + BLOCK
    n = np.maximum(dist, 0)
    max_exact = NUM_BUCKETS // 2
    nf = np.maximum(n, 1).astype(np.float32)
    large = max_exact + (np.log(nf / np.float32(max_exact)) / np.float32(math.log(MAX_DISTANCE / max_exact))
                         * np.float32(NUM_BUCKETS - max_exact)).astype(np.int32)
    large = np.minimum(large, NUM_BUCKETS - 1)
    bucket = np.where(n < max_exact, n, large)
    band = (dist >= 0) & (dist < WINDOW)
    return bucket.astype(np.int32), band


def kernel(x, c, w_ada, b_ada, g_norm_mix, g_norm_ffn, w_in, sinks, b_forget, w_proj_swa, w_proj_fox,
           w_out, rel_bias_table, w_router_group, b_router_group, w_router_expert, b_router_expert,
           w_gate_exp, w_up_exp, w_down_exp, g_final):
    l = 0
    x2 = x.reshape(N_TOK, D_MODEL)

    c16 = jnp.concatenate([c, jnp.zeros_like(c)], axis=0)
    mod = _ada(c16, w_ada[l], b_ada[l][None, :])[:BATCH]
    shift_m, scale_m, gate_m, shift_f, scale_f, gate_f = [
        m.reshape(BATCH, 1, D_MODEL) for m in jnp.split(mod, 6, axis=-1)]

    w = w_in[l]
    o_ka, o_va, o_qb = Q_A, Q_A + KV_A, Q_A + 2 * KV_A
    o_kb, o_vb, o_f = o_qb + W_B, o_qb + 2 * W_B, o_qb + 3 * W_B
    o_g = o_f + N_HEADS_FOX

    def dup(cols):
        heads = [cols[:, h * HEAD_DIM:(h + 1) * HEAD_DIM] for h in range(N_KV_HEADS_SWA)]
        return jnp.concatenate([hd for hd in heads for _ in range(2)], axis=1)

    head_order = jnp.argsort(b_forget[l])

    def reorder_heads(cols):
        return jnp.take(cols.reshape(D_MODEL, N_HEADS_FOX, HEAD_DIM), head_order, axis=1).reshape(D_MODEL, W_B)

    w_fox = jnp.concatenate([reorder_heads(w[:, o_qb:o_kb]), reorder_heads(w[:, o_kb:o_vb]),
                             reorder_heads(w[:, o_vb:o_f])], axis=1)
    b_fox = jnp.take(b_forget[l], head_order)
    w_proj_b = jnp.take(w_proj_fox[l].reshape(N_HEADS_FOX, HEAD_DIM, D_MODEL), head_order, axis=0).reshape(W_B, D_MODEL)
    carrier = DECAY_LANES * N_HEADS_FOX
    w_f = jnp.pad(jnp.repeat(jnp.take(w[:, o_f:o_g], head_order, axis=1), DECAY_LANES, axis=1),
                  ((0, 0), (0, LANES - carrier)))
    w_main = jnp.concatenate([w[:, :Q_A], dup(w[:, o_ka:o_va]), w[:, o_va:o_qb], w_f, w_fox], axis=1).astype(BF16)
    w_g = w[:, o_g:].astype(BF16)
    qa, kdup, va, qb, kb, vb, f_pad, gates, nrm = _inproj(
        x2, scale_m, shift_m, g_norm_mix[l][None, :], w_main, w_g)

    b_pad = jnp.pad(jnp.repeat(b_fox, DECAY_LANES), (0, LANES - carrier))[None, :]
    lanes = np.arange(LANES)
    jmod = jnp.asarray(np.where(lanes < DECAY_LANES * N_HEADS_FOX, lanes % DECAY_LANES, 7)[None, :].astype(np.int32))
    dq, dk, fb = _cum(f_pad, b_pad, jmod)

    bucket, band = _t5_bucket_np()
    onehot = jnp.asarray(bucket[None] == np.arange(NUM_BUCKETS)[:, None, None], dtype=F32)
    bias = jnp.einsum("bh,bqk->hqk", rel_bias_table.astype(F32), onehot, precision=HIGHEST)
    bias = jnp.where(band[None], bias * LOG2E, NEG_INF)
    first = np.arange(2 * BLOCK)[None, None, :] < BLOCK
    bias = jnp.stack([jnp.where(first, NEG_INF, bias), bias]).reshape(2, N_KV_HEADS_SWA, -1, 2 * BLOCK)
    o_a = _swa(sinks[l].astype(F32) * LOG2E, qa, kdup, va, bias)

    o_b = _fox(_fox_first_tiles(nrm, fb), qb, kb, vb, dq, dk)

    w_r = jnp.concatenate([w_router_group[l]] + [w_router_expert[l][g] for g in range(N_GROUPS)], axis=1)
    w_r = jnp.pad(w_r, ((0, 0), (0, LANES - w_r.shape[1])))
    wr_hi = w_r.astype(BF16)
    wr_lo = (w_r - wr_hi.astype(F32)).astype(BF16)
    wr2 = jnp.concatenate([wr_hi, wr_lo], axis=1)
    b_r = jnp.concatenate([b_router_group[l], b_router_expert[l].reshape(-1)])
    b_r = jnp.pad(b_r, (0, LANES - b_r.shape[0]))[None, :]
    x1, xy, rcol, cu = _post(x2, o_a, o_b, gates, gate_m, scale_f, shift_f, g_norm_ffn[l][None, :],
                             w_proj_swa[l].astype(BF16), w_proj_b.astype(BF16), w_out[l].astype(BF16),
                             wr2, b_r)

    i32 = jnp.int32
    n_tok_tiles = N_TOK // TM_POST
    cu = cu.reshape(n_tok_tiles, 8, LANES)[:, 0, :N_EXPERTS].astype(i32)
    loc_u = jnp.cumsum(cu, axis=1) - cu
    cend = jnp.cumsum(cu, axis=0)
    cstart = cend - cu
    tot_u = cend[-1]
    tiles_e = (tot_u + UNITS_PER_TILE - 1) // UNITS_PER_TILE
    tile_end = jnp.cumsum(tiles_e)
    tile_start = tile_end - tiles_e
    r = jnp.arange(N_EXP_TILES, dtype=i32)
    tile_expert = jnp.minimum(jnp.sum((tile_end[None, :] <= r[:, None]).astype(i32), axis=1), N_EXPERTS - 1)
    sel_e = tile_expert[:, None] == jnp.arange(N_EXPERTS, dtype=i32)[None, :]
    tw = r - jnp.sum(jnp.where(sel_e, tile_start[None, :], 0), axis=1)
    tot_r = jnp.sum(jnp.where(sel_e, tot_u[None, :], 0), axis=1)
    n_used = tile_end[-1:].astype(i32)
    q = tw[:, None] * UNITS_PER_TILE + jnp.arange(UNITS_PER_TILE, dtype=i32)[None, :]

    def of_expert(tab):
        return jnp.sum(jnp.where(sel_e[:, None, :], tab[None, :, :], 0), axis=2)

    cend_r, cstart_r, loc_r = of_expert(cend), of_expert(cstart), of_expert(loc_u)
    src_tile = jnp.minimum(jnp.sum((cend_r[:, None, :] <= q[:, :, None]).astype(i32), axis=2), n_tok_tiles - 1)
    sel_t = src_tile[:, :, None] == jnp.arange(n_tok_tiles, dtype=i32)[None, None, :]
    k = (q - jnp.sum(jnp.where(sel_t, cstart_r[:, None, :], 0), axis=2)
         + jnp.sum(jnp.where(sel_t, loc_r[:, None, :], 0), axis=2))
    real_rows = src_tile * XY_ROWS + k * UNIT
    pad_rows = PAD_BASE_ROW + (tile_expert[:, None] * PAD_UNITS_PER_EXPERT + (q - tot_r[:, None])) * UNIT
    idle_row = PAD_BASE_ROW + N_EXPERTS * PAD_UNITS_PER_EXPERT * UNIT
    unit_rows = jnp.where(q < tot_r[:, None], real_rows, pad_rows)
    unit_rows = jnp.where((r < n_used)[:, None], unit_rows, idle_row).reshape(-1).astype(i32)

    eid = jnp.arange(N_EXPERTS, dtype=i32)
    used = tiles_e > 0
    later_used = (eid[None, :] > eid[:, None]) & used[None, :]
    next_e = jnp.min(jnp.where(later_used, eid[None, :], N_EXPERTS), axis=1)
    next_e = jnp.where(next_e == N_EXPERTS, -1, next_e)
    parity_e = (jnp.cumsum(used.astype(i32)) - used.astype(i32)) % 2
    next_expert = jnp.sum(jnp.where(sel_e, next_e[None, :], 0), axis=1).astype(i32)
    expert_parity = jnp.sum(jnp.where(sel_e, parity_e[None, :], 0), axis=1).astype(i32)

    xy = _experts(tile_expert.astype(i32), n_used, unit_rows, next_expert, expert_parity, xy,
                  w_gate_exp[l].reshape(N_EXPERTS, D_MODEL, D_FF_EXPERT),
                  w_up_exp[l].reshape(N_EXPERTS, D_MODEL, D_FF_EXPERT),
                  w_down_exp[l].reshape(N_EXPERTS, D_FF_EXPERT, D_MODEL))
    out = _combine(x1, rcol, gate_f, g_final[None, :], xy)
    return out.reshape(BATCH, SEQ, D_MODEL)
```

```python
import math

import numpy as np
import jax
import jax.numpy as jnp
from jax import lax
from jax.experimental import pallas as pl
from jax.experimental.pallas import tpu as pltpu

F32 = jnp.float32
BF16 = jnp.bfloat16
HIGHEST = lax.Precision.HIGHEST

D_MODEL = 1024
BATCH = 8
SEQ = 4096
N_TOK = BATCH * SEQ
N_HEADS_SWA = 8
N_KV_HEADS_SWA = 2
N_HEADS_FOX = 8
HEAD_DIM = 64
WINDOW = 128
BLOCK = 128
NUM_BUCKETS = 32
MAX_DISTANCE = 128
N_GROUPS = 4
EXPERTS_PER_GROUP = 8
N_EXPERTS = N_GROUPS * EXPERTS_PER_GROUP
D_FF_EXPERT = 256
EPS = 1e-6
NEG_INF = -1e30

Q_A = N_HEADS_SWA * HEAD_DIM
KV_A = N_KV_HEADS_SWA * HEAD_DIM
W_B = N_HEADS_FOX * HEAD_DIM
LANES = 128
QK_SCALE = HEAD_DIM ** -0.5

TM_IN = 1024
TM_POST = 512
TQ_FOX = 512
TK_FOX = TQ_FOX
FOX_ROWS = 2
SWA_BLOCKS = 8
TM_EXP = 512
TM_ROW = 512
COMBINE_TILES = 2
UNIT = 16
XY_UNITS = 2 * TM_POST // UNIT + N_EXPERTS
XY_ROWS = XY_UNITS * UNIT
XY_COLS = D_MODEL + LANES
UNITS_PER_TILE = TM_EXP // UNIT
N_TOK_TILES = N_TOK // TM_POST
N_EXP_TILES = N_TOK_TILES * XY_UNITS // UNITS_PER_TILE + N_EXPERTS
PAD_UNITS_PER_EXPERT = UNITS_PER_TILE - 1
PAD_BLOCKS = -(-(N_EXPERTS * PAD_UNITS_PER_EXPERT * UNIT) // XY_ROWS)
PAD_BASE_ROW = N_TOK_TILES * XY_ROWS
W1_LANES, W2_LANES, E1_LANE, E2_LANE = (4, 6), (5, 7), 8, 9
VMEM_LIMIT = 56 * 1024 * 1024

DECAY_LANES = 6
LOG2E = math.log2(math.e)
Q_SCALE_LOG2 = QK_SCALE * LOG2E
PRUNE_MARGIN = 160.0


def _cparams(n_axes):
    return pltpu.CompilerParams(dimension_semantics=("arbitrary",) * n_axes,
                                vmem_limit_bytes=VMEM_LIMIT)


def _ada_kernel(c_ref, w_ref, b_ref, o_ref):
    c = c_ref[...]
    ca = c * jax.nn.sigmoid(c)
    o_ref[...] = jnp.dot(ca.astype(BF16), w_ref[...].astype(BF16),
                         preferred_element_type=F32) + b_ref[...]


def _ada(c16, w_ada, b_ada):
    n_out = w_ada.shape[1]
    blk = 1024
    return pl.pallas_call(
        _ada_kernel,
        out_shape=jax.ShapeDtypeStruct((16, n_out), F32),
        grid=(n_out // blk,),
        in_specs=[pl.BlockSpec((16, D_MODEL), lambda j: (0, 0)),
                  pl.BlockSpec((D_MODEL, blk), lambda j: (0, j)),
                  pl.BlockSpec((1, blk), lambda j: (0, j))],
        out_specs=pl.BlockSpec((16, blk), lambda j: (0, j)),
        compiler_params=_cparams(1),
        name="ada",
    )(c16, w_ada, b_ada)


def _inproj_kernel(x_ref, sc_ref, sh_ref, g_ref, wm_ref, wg_ref, ind_ref,
                   qa_ref, kd_ref, va_ref, qb_ref, kb_ref, vb_ref, f_ref, gt_ref, nrm_ref):
    x = x_ref[...]
    rs = lax.rsqrt(jnp.mean(x * x, axis=-1, keepdims=True) + EPS)
    a = g_ref[...] * (1.0 + sc_ref[...])
    h = (x * rs * a + sh_ref[...]).astype(BF16)

    def mm(w):
        return jnp.dot(h, w, preferred_element_type=F32)

    qa_ref[...] = (mm(wm_ref[:, 0:512]) * Q_SCALE_LOG2).astype(BF16)
    kd_ref[...] = mm(wm_ref[:, 512:768]).astype(BF16)
    vf = mm(wm_ref[:, 768:1024])
    f_ref[...] = vf[:, LANES:2 * LANES]
    v = vf[:, 0:LANES]
    vr = pltpu.roll(v, HEAD_DIM, 1)
    lo = lax.broadcasted_iota(jnp.int32, v.shape, 1) < HEAD_DIM
    va_ref[:, 0:LANES] = jnp.where(lo, v, vr).astype(BF16)
    va_ref[:, LANES:2 * LANES] = jnp.where(lo, vr, v).astype(BF16)
    qb = (mm(wm_ref[:, 1024:1536]) * Q_SCALE_LOG2).astype(BF16)
    kb = mm(wm_ref[:, 1536:2048]).astype(BF16)
    qb_ref[...] = qb
    kb_ref[...] = kb
    vb_ref[...] = mm(wm_ref[:, 2048:2560]).astype(BF16)
    sq = jnp.concatenate([qb, kb], axis=1).astype(F32)
    seg = jnp.dot((sq * sq).astype(BF16), ind_ref[...], preferred_element_type=F32)
    for sub in range(TM_IN // TQ_FOX):
        tile_max = jnp.max(seg[sub * TQ_FOX:(sub + 1) * TQ_FOX], axis=0, keepdims=True)
        nrm_ref[8 * sub:8 * sub + 8, :] = jnp.broadcast_to(tile_max, (8, LANES))
    gt_ref[...] = mm(wg_ref[...]).astype(BF16)


def _inproj(x2, scale_m, shift_m, g_mix, w_main, w_g):
    tm = TM_IN
    tpb = SEQ // tm
    row = lambda i: (i, 0)
    per_b = lambda i: (i // tpb, 0, 0)
    const = lambda i: (0, 0)
    outs = [(Q_A, BF16), (2 * KV_A, BF16), (2 * KV_A, BF16), (W_B, BF16), (W_B, BF16), (W_B, BF16),
            (LANES, F32), (2 * D_MODEL, BF16)]
    ind_np = np.zeros((2 * W_B, LANES), np.float32)
    ind_np[np.arange(2 * W_B), np.arange(2 * W_B) // HEAD_DIM] = 1.0
    ind = jnp.asarray(ind_np, dtype=BF16)
    n_steps = N_TOK // tm
    nrm_rows = 8 * (tm // TQ_FOX)
    once = pl.Buffered(1)
    return pl.pallas_call(
        _inproj_kernel,
        out_shape=[jax.ShapeDtypeStruct((N_TOK, w), dt) for w, dt in outs]
        + [jax.ShapeDtypeStruct((n_steps * nrm_rows, LANES), F32)],
        grid=(n_steps,),
        in_specs=[pl.BlockSpec((tm, D_MODEL), row),
                  pl.BlockSpec((None, 1, D_MODEL), per_b),
                  pl.BlockSpec((None, 1, D_MODEL), per_b),
                  pl.BlockSpec((1, D_MODEL), const),
                  pl.BlockSpec(w_main.shape, const, pipeline_mode=once),
                  pl.BlockSpec(w_g.shape, const, pipeline_mode=once),
                  pl.BlockSpec(ind.shape, const, pipeline_mode=once)],
        out_specs=[pl.BlockSpec((tm, w), row) for w, _ in outs] + [pl.BlockSpec((nrm_rows, LANES), row)],
        compiler_params=_cparams(1),
        name="inproj",
    )(x2, scale_m, shift_m, g_mix, w_main, w_g, ind)


def _log_sigmoid(x):
    return jnp.minimum(x, 0.0) - jnp.log1p(jnp.exp(-jnp.abs(x)))


def _cum_kernel(f_ref, b_ref, jm_ref, qa_ref, ka_ref, fb_ref):
    cum = _log_sigmoid(f_ref[...] + b_ref[...]) * LOG2E
    row = lax.broadcasted_iota(jnp.int32, cum.shape, 0)
    k = 1
    while k < SEQ:
        if k < 8:
            shifted = jnp.where(row >= k, pltpu.roll(cum, k, 0), 0.0)
        else:
            shifted = jnp.concatenate([jnp.zeros((k, LANES), F32), cum[:SEQ - k]], axis=0)
        cum = cum + shifted
        k *= 2
    jm = jm_ref[...]
    for blk in range(SEQ // LANES):
        rows = slice(blk * LANES, (blk + 1) * LANES)
        cb = cum[rows]
        carry = cb[LANES - 1:LANES]
        hi = cb.astype(BF16).astype(F32)
        r1 = cb - hi
        mid = r1.astype(BF16).astype(F32)
        lo = (r1 - mid).astype(BF16).astype(F32)
        one = jnp.ones_like(cb)
        zero = jnp.zeros_like(cb)
        qa = jnp.where(jm == 0, hi, jnp.where(jm == 1, mid, jnp.where(jm == 2, lo,
                       jnp.where(jm < DECAY_LANES, one, zero))))
        ka = jnp.where(jm == 3, -hi, jnp.where(jm == 4, -mid, jnp.where(jm == 5, -lo,
                       jnp.where(jm < 3, one, zero))))
        qa_ref[rows, :] = qa.astype(BF16)
        ka_ref[rows, :] = ka.astype(BF16)
        blocks_per_tile = TQ_FOX // LANES
        tile = blk // blocks_per_tile
        if blk % blocks_per_tile == 0:
            fb_ref[2 * tile:2 * tile + 1, :] = cb[0:1]
        if blk % blocks_per_tile == blocks_per_tile - 1:
            fb_ref[2 * tile + 1:2 * tile + 2, :] = carry


def _cum(f_pad, b_pad, jmod):
    n_tiles = SEQ // TQ_FOX
    return pl.pallas_call(
        _cum_kernel,
        out_shape=[jax.ShapeDtypeStruct((BATCH, SEQ, LANES), BF16)] * 2
        + [jax.ShapeDtypeStruct((BATCH, 2 * n_tiles, LANES), F32)],
        grid=(BATCH,),
        in_specs=[pl.BlockSpec((SEQ, LANES), lambda b: (b, 0)),
                  pl.BlockSpec((1, LANES), lambda b: (0, 0)),
                  pl.BlockSpec((1, LANES), lambda b: (0, 0))],
        out_specs=[pl.BlockSpec((None, SEQ, LANES), lambda b: (b, 0, 0))] * 2
        + [pl.BlockSpec((None, 2 * n_tiles, LANES), lambda b: (b, 0, 0))],
        compiler_params=_cparams(1),
        name="cum",
    )(f_pad, b_pad, jmod)


def _swa_block(sink_cols, q, kk, vv, bias_ref, lo):
    tiles = []
    for g in range(N_KV_HEADS_SWA):
        parts = []
        for t in range(2):
            qt = q[:, (2 * g + t) * LANES:(2 * g + t + 1) * LANES]
            zero = jnp.zeros_like(qt)
            parts.append(jnp.where(lo, qt, zero))
            parts.append(jnp.where(lo, zero, qt))
        q4 = jnp.concatenate(parts, axis=0)
        s = lax.dot_general(q4, kk[:, g * LANES:(g + 1) * LANES], (((1,), (1,)), ((), ())),
                            preferred_element_type=F32)
        s = s + bias_ref[g]
        sink = sink_cols[g]
        m = jnp.maximum(jnp.max(s, axis=-1, keepdims=True), sink)
        p = jnp.exp2(s - m)
        den = jnp.sum(p, axis=-1, keepdims=True) + jnp.exp2(sink - m)
        o = jnp.dot(p.astype(BF16), vv[:, g * LANES:(g + 1) * LANES],
                    preferred_element_type=F32) / den
        tiles.append(jnp.where(lo, o[0:BLOCK], o[BLOCK:2 * BLOCK]))
        tiles.append(jnp.where(lo, o[2 * BLOCK:3 * BLOCK], o[3 * BLOCK:4 * BLOCK]))
    return tiles


def _swa_kernel(sink_ref, q_ref, kc_ref, kp_ref, vc_ref, vp_ref, bias_first_ref, bias_ref, o_ref):
    lane = lax.broadcasted_iota(jnp.int32, (BLOCK, LANES), 1)
    lo = lane < HEAD_DIM
    grp = N_HEADS_SWA // N_KV_HEADS_SWA
    row = lax.broadcasted_iota(jnp.int32, (grp * BLOCK, 1), 0)
    sink_cols = []
    for g in range(N_KV_HEADS_SWA):
        col = jnp.full((grp * BLOCK, 1), sink_ref[g * grp + grp - 1], F32)
        for hh in range(grp - 2, -1, -1):
            col = jnp.where(row < (hh + 1) * BLOCK, sink_ref[g * grp + hh], col)
        sink_cols.append(col)
    for blk in range(SWA_BLOCKS):
        rows = slice(blk * BLOCK, (blk + 1) * BLOCK)
        if blk == 0:
            kk = jnp.concatenate([kp_ref[...], kc_ref[rows, :]], axis=0)
            vv = jnp.concatenate([vp_ref[...], vc_ref[rows, :]], axis=0)
            bias = bias_first_ref
        else:
            prev_rows = slice((blk - 1) * BLOCK, (blk + 1) * BLOCK)
            kk = kc_ref[prev_rows, :]
            vv = vc_ref[prev_rows, :]
            bias = bias_ref
        tiles = _swa_block(sink_cols, q_ref[rows, :], kk, vv, bias, lo)
        for c, tile in enumerate(tiles):
            o_ref[rows, c * LANES:(c + 1) * LANES] = tile.astype(BF16)


def _swa(sinks, qa, kdup, va, bias):
    nb = SEQ // BLOCK
    ns = nb // SWA_BLOCKS
    cur = lambda b, i, s: (b * ns + i, 0)
    prev = lambda b, i, s: (b * nb + jnp.maximum(SWA_BLOCKS * i - 1, 0), 0)
    grid_spec = pltpu.PrefetchScalarGridSpec(
        num_scalar_prefetch=1,
        grid=(BATCH, ns),
        in_specs=[pl.BlockSpec((SWA_BLOCKS * BLOCK, Q_A), cur),
                  pl.BlockSpec((SWA_BLOCKS * BLOCK, 2 * KV_A), cur),
                  pl.BlockSpec((BLOCK, 2 * KV_A), prev),
                  pl.BlockSpec((SWA_BLOCKS * BLOCK, 2 * KV_A), cur),
                  pl.BlockSpec((BLOCK, 2 * KV_A), prev),
                  pl.BlockSpec((None,) + bias.shape[1:], lambda b, i, s: (jnp.minimum(i, 1), 0, 0, 0)),
                  pl.BlockSpec((None,) + bias.shape[1:], lambda b, i, s: (1, 0, 0, 0))],
        out_specs=pl.BlockSpec((SWA_BLOCKS * BLOCK, Q_A), cur))
    return pl.pallas_call(
        _swa_kernel,
        out_shape=jax.ShapeDtypeStruct((N_TOK, Q_A), BF16),
        grid_spec=grid_spec,
        compiler_params=_cparams(2),
        name="swa",
    )(sinks, qa, kdup, kdup, va, va, bias, bias)


def _fox_kernel(js_ref, q_ref, k_ref, v_ref, qa_ref, ka_ref, o_ref,
                kaug, vaug, q2, m_sc, acc_sc, s_0, s_1, s_2, s_3):
    tq, tk = TQ_FOX, TK_FOX
    b = pl.program_id(0)
    t = pl.program_id(1)
    g = pl.program_id(2)
    first_tile = (b * pl.num_programs(1) + t) * (SEQ // tq) + g * FOX_ROWS

    @pl.when(g == 0)
    def _():
        kaug[:, 0:LANES] = k_ref[...]
        kaug[:, LANES:2 * LANES] = ka_ref[...]
        vaug[:, 0:LANES] = v_ref[...]
        vaug[:, LANES:2 * LANES] = jnp.ones((SEQ, LANES), BF16)

    lane = lax.broadcasted_iota(jnp.int32, (tq, LANES), 1)
    lo = lane < HEAD_DIM
    base = 2 * DECAY_LANES * t
    own = [(lane >= base + h * DECAY_LANES) & (lane < base + (h + 1) * DECAY_LANES) for h in range(2)]
    for rw in range(FOX_ROWS):
        rows = slice(rw * tq, (rw + 1) * tq)
        q = q_ref[rows, :]
        qa = qa_ref[rows, :]
        zero = jnp.zeros_like(q)
        q2[rw, 0, :, 0:LANES] = jnp.where(lo, q, zero)
        q2[rw, 1, :, 0:LANES] = jnp.where(lo, zero, q)
        for h in range(2):
            q2[rw, h, :, LANES:2 * LANES] = jnp.where(own[h], qa, zero)

    rr = lax.broadcasted_iota(jnp.int32, (tq, tk), 0)
    cc = lax.broadcasted_iota(jnp.int32, (tq, tk), 1)
    causal = cc <= rr
    bufs = ((s_0, s_1), (s_2, s_3))
    j_starts = [js_ref[first_tile + rw] for rw in range(FOX_ROWS)]
    for rw in range(FOX_ROWS):
        nxt = rw + 1 if rw + 1 < FOX_ROWS else None
        _fox_query_tile(rw, g * FOX_ROWS + rw, j_starts[rw], bufs[rw % 2], q2, kaug, vaug, m_sc, acc_sc,
                        causal, lo, o_ref,
                        first_scores_done=rw > 0,
                        next_first=None if nxt is None else (nxt, j_starts[nxt], bufs[nxt % 2][0]))


def _fox_query_tile(rw, i, j_start, buf_pair, q2, kaug, vaug, m_sc, acc_sc, causal, lo, o_ref,
                    first_scores_done, next_first):
    tq, tk = TQ_FOX, TK_FOX
    s_a, s_b = buf_pair
    m_sc[...] = jnp.full(m_sc.shape, NEG_INF, F32)
    acc_sc[...] = jnp.zeros(acc_sc.shape, F32)

    def scores_of(row, h, ks):
        return lax.dot_general(q2[row, h], kaug[pl.ds(ks, tk), :], (((1,), (1,)), ((), ())),
                               preferred_element_type=F32)

    def scores(h, ks):
        return scores_of(rw, h, ks)

    def consume(h, s, ks, mask):
        if mask is not None:
            s = jnp.where(mask, s, NEG_INF)
        m_prev = m_sc[h]
        m_new = jnp.maximum(m_prev, jnp.max(s, axis=-1, keepdims=True))
        alpha = jnp.exp2(m_prev - m_new)
        p = jnp.exp2(s - jnp.concatenate([m_new] * (tk // LANES), axis=1))
        pv = jnp.dot(p.astype(BF16), vaug[pl.ds(ks, tk), :], preferred_element_type=F32)
        acc_sc[h] = jnp.concatenate([alpha, alpha], axis=1) * acc_sc[h] + pv
        m_sc[h] = m_new

    def key_start(j):
        return pl.multiple_of(j * tk, tk)

    def scores_into(buf, j):
        for h in range(2):
            buf[h] = scores(h, key_start(j))

    def consume_from(buf, j, mask):
        for h in range(2):
            consume(h, buf[h], key_start(j), mask)

    n_full = i - j_start

    if not first_scores_done:
        scores_into(s_a, j_start)

    def pair(p, carry):
        j = j_start + 2 * p
        scores_into(s_b, j + 1)
        consume_from(s_a, j, None)
        scores_into(s_a, j + 2)
        consume_from(s_b, j + 1, None)
        return carry

    lax.fori_loop(0, n_full // 2, pair, 0)
    odd = lax.rem(n_full, 2) == 1

    def start_next():
        if next_first is not None:
            nrow, nj, nbuf = next_first
            for h in range(2):
                nbuf[h] = scores_of(nrow, h, key_start(nj))

    @pl.when(odd)
    def _():
        scores_into(s_b, i)
        consume_from(s_a, i - 1, None)
        start_next()
        consume_from(s_b, i, causal)

    @pl.when(jnp.logical_not(odd))
    def _():
        start_next()
        consume_from(s_a, i, causal)

    outs = [acc_sc[h, :, 0:LANES] / acc_sc[h, :, LANES:2 * LANES] for h in range(2)]
    o_ref[rw * tq:(rw + 1) * tq, :] = jnp.where(lo, outs[0], outs[1]).astype(BF16)


def _fox(j_start, qb, kb, vb, qa, ka):
    tq = TQ_FOX
    ns = SEQ // (tq * FOX_ROWS)
    n_pairs = N_HEADS_FOX // 2
    qmap = lambda b, t, i, js: (b * ns + i, t)
    kmap = lambda b, t, i, js: (b, t)
    grid_spec = pltpu.PrefetchScalarGridSpec(
        num_scalar_prefetch=1,
        grid=(BATCH, n_pairs, ns),
        in_specs=[pl.BlockSpec((FOX_ROWS * tq, LANES), qmap),
                  pl.BlockSpec((SEQ, LANES), kmap),
                  pl.BlockSpec((SEQ, LANES), kmap),
                  pl.BlockSpec((None, FOX_ROWS * tq, LANES), lambda b, t, i, js: (b, i, 0)),
                  pl.BlockSpec((None, SEQ, LANES), lambda b, t, i, js: (b, 0, 0))],
        out_specs=pl.BlockSpec((FOX_ROWS * tq, LANES), qmap),
        scratch_shapes=[pltpu.VMEM((SEQ, 2 * LANES), BF16),
                        pltpu.VMEM((SEQ, 2 * LANES), BF16),
                        pltpu.VMEM((FOX_ROWS, 2, tq, 2 * LANES), BF16),
                        pltpu.VMEM((2, tq, LANES), F32),
                        pltpu.VMEM((2, tq, 2 * LANES), F32)]
        + [pltpu.VMEM((2, tq, TK_FOX), F32)] * 4)
    return pl.pallas_call(
        _fox_kernel,
        out_shape=jax.ShapeDtypeStruct((N_TOK, W_B), BF16),
        grid_spec=grid_spec,
        compiler_params=_cparams(3),
        name="fox",
    )(j_start, qb, kb, vb, qa, ka)


def _fox_first_tiles(nrm, fb):
    n_tiles = SEQ // TQ_FOX
    nr = nrm.reshape(BATCH, n_tiles, 8, LANES)[:, :, 0, :] * 1.02
    qn = jnp.sqrt(nr[..., 0:N_HEADS_FOX])
    kn = jnp.sqrt(nr[..., N_HEADS_FOX:2 * N_HEADS_FOX])
    f_first = fb[:, 0::2, 0:DECAY_LANES * N_HEADS_FOX:DECAY_LANES]
    f_last = fb[:, 1::2, 0:DECAY_LANES * N_HEADS_FOX:DECAY_LANES]
    kn_prefix = lax.cummax(kn, axis=1)
    upper = qn[:, :, None, :] * kn_prefix[:, None, :, :] + f_first[:, :, None, :] - f_last[:, None, :, :]
    row_max_low = -(qn * kn)[:, :, None, :]
    ii = jnp.arange(n_tiles)[None, :, None, None]
    jj = jnp.arange(n_tiles)[None, None, :, None]
    skip = (upper < row_max_low - PRUNE_MARGIN) & (jj < ii)
    skip = jnp.all(skip.reshape(BATCH, n_tiles, n_tiles, N_HEADS_FOX // 2, 2), axis=-1)
    first = jnp.min(jnp.where(skip, n_tiles, jj), axis=2)
    return jnp.transpose(first, (0, 2, 1)).reshape(-1).astype(jnp.int32)


def _post_kernel(x_ref, oa_ref, ob_ref, gt_ref, gm_ref, sc_ref, sh_ref, g_ref,
                 wa_ref, wb_ref, wo_ref, wr2_ref, br_ref,
                 x1_ref, xy_ref, rc_ref, cu_ref, hh_prev, lg_prev):
    step = pl.program_id(0)

    @pl.when(step == 0)
    def _():
        hh_prev[...] = jnp.zeros(hh_prev.shape, BF16)
        lg_prev[...] = jnp.zeros(lg_prev.shape, F32)

    @pl.when(step <= N_TOK_TILES)
    def _():
        hh_p = hh_prev[...]
        lg_p = lg_prev[...]
        hh, logits = _post_mix(x_ref, oa_ref, ob_ref, gt_ref, gm_ref, sc_ref, sh_ref, g_ref,
                               wa_ref, wb_ref, wo_ref, wr2_ref, br_ref, x1_ref)
        _post_route(hh_p, lg_p, xy_ref, rc_ref, cu_ref)
        hh_prev[...] = hh
        lg_prev[...] = logits

    @pl.when(step > N_TOK_TILES)
    def _():
        xy_ref[...] = jnp.zeros(xy_ref.shape, BF16)


def _post_mix(x_ref, oa_ref, ob_ref, gt_ref, gm_ref, sc_ref, sh_ref, g_ref,
              wa_ref, wb_ref, wo_ref, wr2_ref, br_ref, x1_ref):
    pa = jnp.dot(oa_ref[...], wa_ref[...], preferred_element_type=F32)
    pb = jnp.dot(ob_ref[...], wb_ref[...], preferred_element_type=F32)
    ga = jax.nn.sigmoid(gt_ref[:, 0:D_MODEL].astype(F32))
    gb = jax.nn.sigmoid(gt_ref[:, D_MODEL:2 * D_MODEL].astype(F32))
    merged = (ga * pa + gb * pb).astype(BF16)
    y = jnp.dot(merged, wo_ref[...], preferred_element_type=F32)
    x1 = x_ref[...] + gm_ref[...] * y
    x1_ref[...] = x1

    rs = lax.rsqrt(jnp.mean(x1 * x1, axis=-1, keepdims=True) + EPS)
    a = g_ref[...] * (1.0 + sc_ref[...])
    h2 = x1 * rs * a + sh_ref[...]

    hh = h2.astype(BF16)
    hl = (h2 - hh.astype(F32)).astype(BF16)
    hi_both = jnp.dot(hh, wr2_ref[...], preferred_element_type=F32)
    logits = (hi_both[:, 0:LANES] + hi_both[:, LANES:2 * LANES]
              + jnp.dot(hl, wr2_ref[:, 0:LANES], preferred_element_type=F32)
              + br_ref[...])
    return hh, logits


def _post_route(hh, logits, xy_ref, rc_ref, cu_ref):
    tm = TM_POST
    lane = lax.broadcasted_iota(jnp.int32, (tm, LANES), 1).astype(F32)
    big = float(LANES)
    gl = jnp.where(lane < N_GROUPS, logits, -jnp.inf)
    gmax = jnp.max(gl, axis=-1, keepdims=True)
    gi = jnp.min(jnp.where(gl == gmax, lane, big), axis=-1, keepdims=True)
    gsum = jnp.sum(jnp.exp(gl - gmax), axis=-1, keepdims=True)
    gp = 1.0 / gsum
    e_lo = N_GROUPS + EXPERTS_PER_GROUP * gi
    el = jnp.where((lane >= e_lo) & (lane < e_lo + EXPERTS_PER_GROUP), logits, -jnp.inf)
    v1 = jnp.max(el, axis=-1, keepdims=True)
    i1 = jnp.min(jnp.where(el == v1, lane, big), axis=-1, keepdims=True)
    el2 = jnp.where(lane == i1, -jnp.inf, el)
    v2 = jnp.max(el2, axis=-1, keepdims=True)
    i2 = jnp.min(jnp.where(el2 == v2, lane, big), axis=-1, keepdims=True)
    e21 = jnp.exp(v2 - v1)
    w1 = gp / (1.0 + e21)
    w2 = gp * e21 / (1.0 + e21)
    e1 = i1 - N_GROUPS
    e2 = i2 - N_GROUPS

    oh = jnp.where((lane == e1) | (lane == e2), 1.0, 0.0)
    cnt_u = jnp.floor((jnp.sum(oh, axis=0, keepdims=True) + (UNIT - 1)) * (1.0 / UNIT))
    r128 = lax.broadcasted_iota(jnp.int32, (LANES, LANES), 0)
    c128 = lax.broadcasted_iota(jnp.int32, (LANES, LANES), 1)
    before_lane = jnp.where(r128 < c128, 1.0, 0.0).astype(BF16)
    loc_u = jnp.dot(jnp.broadcast_to(cnt_u, (8, LANES)).astype(BF16), before_lane,
                    preferred_element_type=F32)
    trow = lax.broadcasted_iota(jnp.int32, (tm, LANES), 0)
    seen = oh
    k = 1
    while k < tm:
        if k < 8:
            shifted = jnp.where(trow >= k, pltpu.roll(seen, k, 0), 0.0)
        else:
            shifted = jnp.concatenate([jnp.zeros((k, LANES), F32), seen[:tm - k]], axis=0)
        seen = seen + shifted
        k *= 2
    pos_e = (seen - oh) + loc_u[0:1] * UNIT
    lp1 = jnp.sum(jnp.where(lane == e1, pos_e, 0.0), axis=-1, keepdims=True)
    lp2 = jnp.sum(jnp.where(lane == e2, pos_e, 0.0), axis=-1, keepdims=True)

    def to_row(col):
        return jnp.transpose(jnp.broadcast_to(col, (tm, LANES)))[0:1]

    srow = lax.broadcasted_iota(jnp.int32, (XY_ROWS, tm), 0).astype(F32)
    pm1 = jnp.where(srow == to_row(lp1), 1.0, 0.0).astype(BF16)
    pm2 = jnp.where(srow == to_row(lp2), 1.0, 0.0).astype(BF16)
    w1h = w1.astype(BF16).astype(F32)
    w2h = w2.astype(BF16).astype(F32)
    side = jnp.where(lane == W1_LANES[0], w1h, jnp.where(lane == W1_LANES[1], w1 - w1h,
           jnp.where(lane == W2_LANES[0], w2h, jnp.where(lane == W2_LANES[1], w2 - w2h,
           jnp.where(lane == E1_LANE, e1, jnp.where(lane == E2_LANE, e2, 0.0))))))
    tok = jnp.concatenate([hh, side.astype(BF16)], axis=1)
    xy_ref[...] = jnp.dot(pm1 + pm2, tok, preferred_element_type=F32).astype(BF16)

    cu_ref[...] = jnp.broadcast_to(cnt_u, cu_ref.shape)
    rc_ref[...] = jnp.where(lane == 0, lp1, jnp.where(lane == 1, lp2, 0.0))


def _post(x2, oa, ob, gates, gate_m, scale_f, shift_f, g_ffn, wa, wb, wo, wr2, b_r):
    tm = TM_POST
    tpb = SEQ // tm
    n_steps = N_TOK_TILES
    row = lambda i: (jnp.minimum(i, n_steps - 1), 0)
    per_b = lambda i: (jnp.minimum(i, n_steps - 1) // tpb, 0, 0)
    routed = lambda i: (jnp.clip(i - 1, 0, n_steps - 1), 0)
    const = lambda i: (0, 0)
    return pl.pallas_call(
        _post_kernel,
        out_shape=[jax.ShapeDtypeStruct((N_TOK, D_MODEL), F32),
                   jax.ShapeDtypeStruct(((n_steps + PAD_BLOCKS) * XY_ROWS, XY_COLS), BF16),
                   jax.ShapeDtypeStruct((N_TOK, LANES), F32),
                   jax.ShapeDtypeStruct((n_steps * 8, LANES), F32)],
        grid=(n_steps + 1 + PAD_BLOCKS,),
        in_specs=[pl.BlockSpec((tm, D_MODEL), row),
                  pl.BlockSpec((tm, Q_A), row),
                  pl.BlockSpec((tm, W_B), row),
                  pl.BlockSpec((tm, 2 * D_MODEL), row),
                  pl.BlockSpec((None, 1, D_MODEL), per_b),
                  pl.BlockSpec((None, 1, D_MODEL), per_b),
                  pl.BlockSpec((None, 1, D_MODEL), per_b),
                  pl.BlockSpec((1, D_MODEL), const),
                  pl.BlockSpec(wa.shape, const),
                  pl.BlockSpec(wb.shape, const),
                  pl.BlockSpec(wo.shape, const),
                  pl.BlockSpec(wr2.shape, const),
                  pl.BlockSpec((1, LANES), const)],
        out_specs=[pl.BlockSpec((tm, D_MODEL), row),
                   pl.BlockSpec((XY_ROWS, XY_COLS), lambda i: (jnp.maximum(i - 1, 0), 0)),
                   pl.BlockSpec((tm, LANES), routed),
                   pl.BlockSpec((8, LANES), routed)],
        scratch_shapes=[pltpu.VMEM((tm, D_MODEL), BF16),
                        pltpu.VMEM((tm, LANES), F32)],
        compiler_params=_cparams(1),
        name="post",
    )(x2, oa, ob, gates, gate_m, scale_f, shift_f, g_ffn, wa, wb, wo, wr2, b_r)


def _experts_kernel(te_ref, nu_ref, ur_ref, ne_ref, ep_ref, xy_in, wg_hbm, wu_hbm, wd_hbm, xy_out,
                    xbuf, ybuf, wg_s, wu_s, wd_s, wg_f, wu_f, wd_f, gsem, ssem, wsem):
    del xy_in
    r = pl.program_id(0)
    last = pl.num_programs(0) - 1
    n_used = nu_ref[0]
    slot = lax.rem(r, 2)

    def unit_row(step, s):
        return pl.multiple_of(ur_ref[step * UNITS_PER_TILE + s], UNIT)

    def start_gathers(step, sl):
        for s in range(UNITS_PER_TILE):
            pltpu.make_async_copy(xy_out.at[pl.ds(unit_row(step, s), UNIT), :],
                                  xbuf.at[sl, pl.ds(s * UNIT, UNIT), :], gsem.at[sl]).start()

    def wait_gathers(sl):
        pltpu.make_async_copy(xy_out.at[pl.ds(0, TM_EXP), :], xbuf.at[sl], gsem.at[sl]).wait()

    def start_scatters(step, sl):
        for s in range(UNITS_PER_TILE):
            pltpu.make_async_copy(ybuf.at[sl, pl.ds(s * UNIT, UNIT), :],
                                  xy_out.at[pl.ds(unit_row(step, s), UNIT), pl.ds(0, D_MODEL)],
                                  ssem.at[sl]).start()

    def wait_scatters(sl):
        pltpu.make_async_copy(ybuf.at[sl], xy_out.at[pl.ds(0, TM_EXP), pl.ds(0, D_MODEL)], ssem.at[sl]).wait()

    @pl.when(r == 0)
    def _():
        start_gathers(0, 0)

    @pl.when(jnp.logical_and(r < n_used, r >= 2))
    def _():
        wait_scatters(slot)

    def weight_copies(e, p):
        return [pltpu.make_async_copy(src.at[e], dst.at[p], wsem.at[p])
                for src, dst in ((wg_hbm, wg_f), (wu_hbm, wu_f), (wd_hbm, wd_f))]

    @pl.when(jnp.logical_and(r < n_used,
                             jnp.logical_or(r == 0, te_ref[r] != te_ref[jnp.maximum(r - 1, 0)])))
    def _():
        e = te_ref[r]
        p = ep_ref[r]

        @pl.when(r == 0)
        def _():
            for cp in weight_copies(e, p):
                cp.start()

        for cp in weight_copies(e, p):
            cp.wait()
        wg_s[...] = wg_f[p].astype(BF16)
        wu_s[...] = wu_f[p].astype(BF16)
        wd_s[...] = wd_f[p].astype(BF16)

        @pl.when(ne_ref[r] >= 0)
        def _():
            for cp in weight_copies(ne_ref[r], 1 - p):
                cp.start()

    @pl.when(r < n_used)
    def _():
        wait_gathers(slot)
        start_gathers(jnp.minimum(r + 1, last), 1 - slot)
        x = xbuf[slot, :, 0:D_MODEL]
        side = xbuf[slot, :, D_MODEL:XY_COLS].astype(F32)
        lane = lax.broadcasted_iota(jnp.int32, side.shape, 1)

        def lanes_sum(a, b):
            return jnp.sum(jnp.where((lane == a) | (lane == b), side, 0.0), axis=-1, keepdims=True)

        is_slot1 = lanes_sum(E1_LANE, E1_LANE) == te_ref[r].astype(F32)
        wrow = jnp.where(is_slot1, lanes_sum(*W1_LANES), lanes_sum(*W2_LANES))
        a = jnp.dot(x, wg_s[...], preferred_element_type=F32)
        u = jnp.dot(x, wu_s[...], preferred_element_type=F32)
        hid = (a * jax.nn.sigmoid(a) * u * wrow).astype(BF16)
        ybuf[slot] = jnp.dot(hid, wd_s[...], preferred_element_type=F32).astype(BF16)
        start_scatters(r, slot)

    @pl.when(r == n_used - 1)
    def _():
        wait_gathers(1 - slot)
        wait_scatters(slot)

        @pl.when(r >= 1)
        def _():
            wait_scatters(1 - slot)


def _experts(tile_expert, n_used, unit_rows, next_expert, expert_parity, xy, wg, wu, wd):
    grid_spec = pltpu.PrefetchScalarGridSpec(
        num_scalar_prefetch=5,
        grid=(N_EXP_TILES,),
        in_specs=[pl.BlockSpec(memory_space=pl.ANY),
                  pl.BlockSpec(memory_space=pl.ANY),
                  pl.BlockSpec(memory_space=pl.ANY),
                  pl.BlockSpec(memory_space=pl.ANY)],
        out_specs=pl.BlockSpec(memory_space=pl.ANY),
        scratch_shapes=[pltpu.VMEM((2, TM_EXP, XY_COLS), BF16),
                        pltpu.VMEM((2, TM_EXP, D_MODEL), BF16),
                        pltpu.VMEM((D_MODEL, D_FF_EXPERT), BF16),
                        pltpu.VMEM((D_MODEL, D_FF_EXPERT), BF16),
                        pltpu.VMEM((D_FF_EXPERT, D_MODEL), BF16),
                        pltpu.VMEM((2, D_MODEL, D_FF_EXPERT), F32),
                        pltpu.VMEM((2, D_MODEL, D_FF_EXPERT), F32),
                        pltpu.VMEM((2, D_FF_EXPERT, D_MODEL), F32),
                        pltpu.SemaphoreType.DMA((2,)),
                        pltpu.SemaphoreType.DMA((2,)),
                        pltpu.SemaphoreType.DMA((2,))])
    return pl.pallas_call(
        _experts_kernel,
        out_shape=jax.ShapeDtypeStruct(xy.shape, xy.dtype),
        grid_spec=grid_spec,
        input_output_aliases={5: 0},
        compiler_params=_cparams(1),
        name="experts",
    )(tile_expert, n_used, unit_rows, next_expert, expert_parity, xy, wg, wu, wd)


def _combine_kernel(x1_ref, rc_ref, gf_ref, gfin_ref, y_ref, o_ref):
    for tl in range(COMBINE_TILES):
        rows = slice(tl * TM_ROW, (tl + 1) * TM_ROW)
        lp1 = rc_ref[rows, 0:1]
        lp2 = rc_ref[rows, 1:2]
        scol = lax.broadcasted_iota(jnp.int32, (TM_ROW, XY_ROWS), 1).astype(F32)
        pick = jnp.where((scol == lp1) | (scol == lp2), 1.0, 0.0).astype(BF16)
        y = jnp.dot(pick, y_ref[tl * XY_ROWS:(tl + 1) * XY_ROWS, :], preferred_element_type=F32)
        xf = x1_ref[rows, :] + gf_ref[...] * y
        rs = lax.rsqrt(jnp.mean(xf * xf, axis=-1, keepdims=True) + EPS)
        o_ref[rows, :] = xf * rs * gfin_ref[...]


def _combine(x1, rcol, gate_f, g_final, xy):
    tm = TM_ROW * COMBINE_TILES
    tpb = SEQ // tm
    row = lambda i: (i, 0)
    return pl.pallas_call(
        _combine_kernel,
        out_shape=jax.ShapeDtypeStruct((N_TOK, D_MODEL), F32),
        grid=(N_TOK // tm,),
        in_specs=[pl.BlockSpec((tm, D_MODEL), row),
                  pl.BlockSpec((tm, LANES), row),
                  pl.BlockSpec((None, 1, D_MODEL), lambda i: (i // tpb, 0, 0)),
                  pl.BlockSpec((1, D_MODEL), lambda i: (0, 0)),
                  pl.BlockSpec((COMBINE_TILES * XY_ROWS, D_MODEL), row)],
        out_specs=pl.BlockSpec((tm, D_MODEL), row),
        compiler_params=_cparams(1),
        name="combine",
    )(x1, rcol, gate_f, g_final, xy)


def _t5_bucket_np():
    qi = np.arange(BLOCK)[:, None]
    kj = np.arange(2 * BLOCK)[None, :]
    dist = qi - kj + BLOCK
    n = np.maximum(dist, 0)
    max_exact = NUM_BUCKETS // 2
    nf = np.maximum(n, 1).astype(np.float32)
    large = max_exact + (np.log(nf / np.float32(max_exact)) / np.float32(math.log(MAX_DISTANCE / max_exact))
                         * np.float32(NUM_BUCKETS - max_exact)).astype(np.int32)
    large = np.minimum(large, NUM_BUCKETS - 1)
    bucket = np.where(n < max_exact, n, large)
    band = (dist >= 0) & (dist < WINDOW)
    return bucket.astype(np.int32), band


def kernel(x, c, w_ada, b_ada, g_norm_mix, g_norm_ffn, w_in, sinks, b_forget, w_proj_swa, w_proj_fox,
           w_out, rel_bias_table, w_router_group, b_router_group, w_router_expert, b_router_expert,
           w_gate_exp, w_up_exp, w_down_exp, g_final):
    l = 0
    x2 = x.reshape(N_TOK, D_MODEL)

    c16 = jnp.concatenate([c, jnp.zeros_like(c)], axis=0)
    mod = _ada(c16, w_ada[l], b_ada[l][None, :])[:BATCH]
    shift_m, scale_m, gate_m, shift_f, scale_f, gate_f = [
        m.reshape(BATCH, 1, D_MODEL) for m in jnp.split(mod, 6, axis=-1)]

    w = w_in[l]
    o_ka, o_va, o_qb = Q_A, Q_A + KV_A, Q_A + 2 * KV_A
    o_kb, o_vb, o_f = o_qb + W_B, o_qb + 2 * W_B, o_qb + 3 * W_B
    o_g = o_f + N_HEADS_FOX

    def dup(cols):
        heads = [cols[:, h * HEAD_DIM:(h + 1) * HEAD_DIM] for h in range(N_KV_HEADS_SWA)]
        return jnp.concatenate([hd for hd in heads for _ in range(2)], axis=1)

    head_order = jnp.argsort(b_forget[l])

    def reorder_heads(cols):
        return jnp.take(cols.reshape(D_MODEL, N_HEADS_FOX, HEAD_DIM), head_order, axis=1).reshape(D_MODEL, W_B)

    w_fox = jnp.concatenate([reorder_heads(w[:, o_qb:o_kb]), reorder_heads(w[:, o_kb:o_vb]),
                             reorder_heads(w[:, o_vb:o_f])], axis=1)
    b_fox = jnp.take(b_forget[l], head_order)
    w_proj_b = jnp.take(w_proj_fox[l].reshape(N_HEADS_FOX, HEAD_DIM, D_MODEL), head_order, axis=0).reshape(W_B, D_MODEL)
    carrier = DECAY_LANES * N_HEADS_FOX
    w_f = jnp.pad(jnp.repeat(jnp.take(w[:, o_f:o_g], head_order, axis=1), DECAY_LANES, axis=1),
                  ((0, 0), (0, LANES - carrier)))
    w_main = jnp.concatenate([w[:, :Q_A], dup(w[:, o_ka:o_va]), w[:, o_va:o_qb], w_f, w_fox], axis=1).astype(BF16)
    w_g = w[:, o_g:].astype(BF16)
    qa, kdup, va, qb, kb, vb, f_pad, gates, nrm = _inproj(
        x2, scale_m, shift_m, g_norm_mix[l][None, :], w_main, w_g)

    b_pad = jnp.pad(jnp.repeat(b_fox, DECAY_LANES), (0, LANES - carrier))[None, :]
    lanes = np.arange(LANES)
    jmod = jnp.asarray(np.where(lanes < DECAY_LANES * N_HEADS_FOX, lanes % DECAY_LANES, 7)[None, :].astype(np.int32))
    dq, dk, fb = _cum(f_pad, b_pad, jmod)

    bucket, band = _t5_bucket_np()
    onehot = jnp.asarray(bucket[None] == np.arange(NUM_BUCKETS)[:, None, None], dtype=F32)
    bias = jnp.einsum("bh,bqk->hqk", rel_bias_table.astype(F32), onehot, precision=HIGHEST)
    bias = jnp.where(band[None], bias * LOG2E, NEG_INF)
    first = np.arange(2 * BLOCK)[None, None, :] < BLOCK
    bias = jnp.stack([jnp.where(first, NEG_INF, bias), bias]).reshape(2, N_KV_HEADS_SWA, -1, 2 * BLOCK)
    o_a = _swa(sinks[l].astype(F32) * LOG2E, qa, kdup, va, bias)

    o_b = _fox(_fox_first_tiles(nrm, fb), qb, kb, vb, dq, dk)

    w_r = jnp.concatenate([w_router_group[l]] + [w_router_expert[l][g] for g in range(N_GROUPS)], axis=1)
    w_r = jnp.pad(w_r, ((0, 0), (0, LANES - w_r.shape[1])))
    wr_hi = w_r.astype(BF16)
    wr_lo = (w_r - wr_hi.astype(F32)).astype(BF16)
    wr2 = jnp.concatenate([wr_hi, wr_lo], axis=1)
    b_r = jnp.concatenate([b_router_group[l], b_router_expert[l].reshape(-1)])
    b_r = jnp.pad(b_r, (0, LANES - b_r.shape[0]))[None, :]
    x1, xy, rcol, cu = _post(x2, o_a, o_b, gates, gate_m, scale_f, shift_f, g_norm_ffn[l][None, :],
                             w_proj_swa[l].astype(BF16), w_proj_b.astype(BF16), w_out[l].astype(BF16),
                             wr2, b_r)

    i32 = jnp.int32
    n_tok_tiles = N_TOK // TM_POST
    cu = cu.reshape(n_tok_tiles, 8, LANES)[:, 0, :N_EXPERTS].astype(i32)
    loc_u = jnp.cumsum(cu, axis=1) - cu
    cend = jnp.cumsum(cu, axis=0)
    cstart = cend - cu
    tot_u = cend[-1]
    tiles_e = (tot_u + UNITS_PER_TILE - 1) // UNITS_PER_TILE
    tile_end = jnp.cumsum(tiles_e)
    tile_start = tile_end - tiles_e
    r = jnp.arange(N_EXP_TILES, dtype=i32)
    tile_expert = jnp.minimum(jnp.sum((tile_end[None, :] <= r[:, None]).astype(i32), axis=1), N_EXPERTS - 1)
    sel_e = tile_expert[:, None] == jnp.arange(N_EXPERTS, dtype=i32)[None, :]
    tw = r - jnp.sum(jnp.where(sel_e, tile_start[None, :], 0), axis=1)
    tot_r = jnp.sum(jnp.where(sel_e, tot_u[None, :], 0), axis=1)
    n_used = tile_end[-1:].astype(i32)
    q = tw[:, None] * UNITS_PER_TILE + jnp.arange(UNITS_PER_TILE, dtype=i32)[None, :]

    def of_expert(tab):
        return jnp.sum(jnp.where(sel_e[:, None, :], tab[None, :, :], 0), axis=2)

    cend_r, cstart_r, loc_r = of_expert(cend), of_expert(cstart), of_expert(loc_u)
    src_tile = jnp.minimum(jnp.sum((cend_r[:, None, :] <= q[:, :, None]).astype(i32), axis=2), n_tok_tiles - 1)
    sel_t = src_tile[:, :, None] == jnp.arange(n_tok_tiles, dtype=i32)[None, None, :]
    k = (q - jnp.sum(jnp.where(sel_t, cstart_r[:, None, :], 0), axis=2)
         + jnp.sum(jnp.where(sel_t, loc_r[:, None, :], 0), axis=2))
    real_rows = src_tile * XY_ROWS + k * UNIT
    pad_rows = PAD_BASE_ROW + (tile_expert[:, None] * PAD_UNITS_PER_EXPERT + (q - tot_r[:, None])) * UNIT
    idle_row = PAD_BASE_ROW + N_EXPERTS * PAD_UNITS_PER_EXPERT * UNIT
    unit_rows = jnp.where(q < tot_r[:, None], real_rows, pad_rows)
    unit_rows = jnp.where((r < n_used)[:, None], unit_rows, idle_row).reshape(-1).astype(i32)

    eid = jnp.arange(N_EXPERTS, dtype=i32)
    used = tiles_e > 0
    later_used = (eid[None, :] > eid[:, None]) & used[None, :]
    next_e = jnp.min(jnp.where(later_used, eid[None, :], N_EXPERTS), axis=1)
    next_e = jnp.where(next_e == N_EXPERTS, -1, next_e)
    parity_e = (jnp.cumsum(used.astype(i32)) - used.astype(i32)) % 2
    next_expert = jnp.sum(jnp.where(sel_e, next_e[None, :], 0), axis=1).astype(i32)
    expert_parity = jnp.sum(jnp.where(sel_e, parity_e[None, :], 0), axis=1).astype(i32)

    xy = _experts(tile_expert.astype(i32), n_used, unit_rows, next_expert, expert_parity, xy,
                  w_gate_exp[l].reshape(N_EXPERTS, D_MODEL, D_FF_EXPERT),
                  w_up_exp[l].reshape(N_EXPERTS, D_MODEL, D_FF_EXPERT),
                  w_down_exp[l].reshape(N_EXPERTS, D_FF_EXPERT, D_MODEL))
    out = _combine(x1, rcol, gate_f, g_final[None, :], xy)
    return out.reshape(BATCH, SEQ, D_MODEL)
```

```python
import math

import numpy as np
import jax
import jax.numpy as jnp
from jax import lax
from jax.experimental import pallas as pl
from jax.experimental.pallas import tpu as pltpu

F32 = jnp.float32
BF16 = jnp.bfloat16
HIGHEST = lax.Precision.HIGHEST

D_MODEL = 1024
BATCH = 8
SEQ = 4096
N_TOK = BATCH * SEQ
N_HEADS_SWA = 8
N_KV_HEADS_SWA = 2
N_HEADS_FOX = 8
HEAD_DIM = 64
WINDOW = 128
BLOCK = 128
NUM_BUCKETS = 32
MAX_DISTANCE = 128
N_GROUPS = 4
EXPERTS_PER_GROUP = 8
N_EXPERTS = N_GROUPS * EXPERTS_PER_GROUP
D_FF_EXPERT = 256
EPS = 1e-6
NEG_INF = -1e30

Q_A = N_HEADS_SWA * HEAD_DIM
KV_A = N_KV_HEADS_SWA * HEAD_DIM
W_B = N_HEADS_FOX * HEAD_DIM
LANES = 128
QK_SCALE = HEAD_DIM ** -0.5

TM_IN = 1024
TM_POST = 512
TQ_FOX = 512
TK_FOX = TQ_FOX
FOX_ROWS = 2
SWA_BLOCKS = 8
TM_EXP = 512
TM_ROW = 512
COMBINE_TILES = 2
UNIT = 16
XY_UNITS = 2 * TM_POST // UNIT + N_EXPERTS
XY_ROWS = XY_UNITS * UNIT
XY_COLS = D_MODEL + LANES
UNITS_PER_TILE = TM_EXP // UNIT
N_TOK_TILES = N_TOK // TM_POST
N_EXP_TILES = N_TOK_TILES * XY_UNITS // UNITS_PER_TILE + N_EXPERTS
PAD_UNITS_PER_EXPERT = UNITS_PER_TILE - 1
PAD_BLOCKS = -(-(N_EXPERTS * PAD_UNITS_PER_EXPERT * UNIT) // XY_ROWS)
PAD_BASE_ROW = N_TOK_TILES * XY_ROWS
W1_LANES, W2_LANES, E1_LANE, E2_LANE = (4, 6), (5, 7), 8, 9
VMEM_LIMIT = 56 * 1024 * 1024

DECAY_LANES = 6
LOG2E = math.log2(math.e)
Q_SCALE_LOG2 = QK_SCALE * LOG2E
PRUNE_MARGIN = 160.0


def _cparams(n_axes):
    return pltpu.CompilerParams(dimension_semantics=("arbitrary",) * n_axes,
                                vmem_limit_bytes=VMEM_LIMIT)


def _ada_kernel(c_ref, w_ref, b_ref, o_ref):
    c = c_ref[...]
    ca = c * jax.nn.sigmoid(c)
    o_ref[...] = jnp.dot(ca.astype(BF16), w_ref[...].astype(BF16),
                         preferred_element_type=F32) + b_ref[...]


def _ada(c16, w_ada, b_ada):
    n_out = w_ada.shape[1]
    blk = n_out // 2
    return pl.pallas_call(
        _ada_kernel,
        out_shape=jax.ShapeDtypeStruct((16, n_out), F32),
        grid=(n_out // blk,),
        in_specs=[pl.BlockSpec((16, D_MODEL), lambda j: (0, 0)),
                  pl.BlockSpec((D_MODEL, blk), lambda j: (0, j)),
                  pl.BlockSpec((1, blk), lambda j: (0, j))],
        out_specs=pl.BlockSpec((16, blk), lambda j: (0, j)),
        compiler_params=_cparams(1),
        name="ada",
    )(c16, w_ada, b_ada)


def _inproj_kernel(x_ref, sc_ref, sh_ref, g_ref, wm_ref, wg_ref, ind_ref,
                   qa_ref, kd_ref, va_ref, qb_ref, kb_ref, vb_ref, f_ref, gt_ref, nrm_ref):
    x = x_ref[...]
    rs = lax.rsqrt(jnp.mean(x * x, axis=-1, keepdims=True) + EPS)
    a = g_ref[...] * (1.0 + sc_ref[...])
    h = (x * rs * a + sh_ref[...]).astype(BF16)

    def mm(w):
        return jnp.dot(h, w, preferred_element_type=F32)

    qa_ref[...] = (mm(wm_ref[:, 0:512]) * Q_SCALE_LOG2).astype(BF16)
    kd_ref[...] = mm(wm_ref[:, 512:768]).astype(BF16)
    vf = mm(wm_ref[:, 768:1024])
    f_ref[...] = vf[:, LANES:2 * LANES]
    v = vf[:, 0:LANES]
    vr = pltpu.roll(v, HEAD_DIM, 1)
    lo = lax.broadcasted_iota(jnp.int32, v.shape, 1) < HEAD_DIM
    va_ref[:, 0:LANES] = jnp.where(lo, v, vr).astype(BF16)
    va_ref[:, LANES:2 * LANES] = jnp.where(lo, vr, v).astype(BF16)
    qb = (mm(wm_ref[:, 1024:1536]) * Q_SCALE_LOG2).astype(BF16)
    kb = mm(wm_ref[:, 1536:2048]).astype(BF16)
    qb_ref[...] = qb
    kb_ref[...] = kb
    vb_ref[...] = mm(wm_ref[:, 2048:2560]).astype(BF16)
    sq = jnp.concatenate([qb, kb], axis=1).astype(F32)
    seg = jnp.dot((sq * sq).astype(BF16), ind_ref[...], preferred_element_type=F32)
    for sub in range(TM_IN // TQ_FOX):
        tile_max = jnp.max(seg[sub * TQ_FOX:(sub + 1) * TQ_FOX], axis=0, keepdims=True)
        nrm_ref[8 * sub:8 * sub + 8, :] = jnp.broadcast_to(tile_max, (8, LANES))
    gt_ref[...] = mm(wg_ref[...]).astype(BF16)


def _inproj(x2, scale_m, shift_m, g_mix, w_main, w_g):
    tm = TM_IN
    tpb = SEQ // tm
    row = lambda i: (i, 0)
    per_b = lambda i: (i // tpb, 0, 0)
    const = lambda i: (0, 0)
    outs = [(Q_A, BF16), (2 * KV_A, BF16), (2 * KV_A, BF16), (W_B, BF16), (W_B, BF16), (W_B, BF16),
            (LANES, F32), (2 * D_MODEL, BF16)]
    ind_np = np.zeros((2 * W_B, LANES), np.float32)
    ind_np[np.arange(2 * W_B), np.arange(2 * W_B) // HEAD_DIM] = 1.0
    ind = jnp.asarray(ind_np, dtype=BF16)
    n_steps = N_TOK // tm
    nrm_rows = 8 * (tm // TQ_FOX)
    once = pl.Buffered(1)
    return pl.pallas_call(
        _inproj_kernel,
        out_shape=[jax.ShapeDtypeStruct((N_TOK, w), dt) for w, dt in outs]
        + [jax.ShapeDtypeStruct((n_steps * nrm_rows, LANES), F32)],
        grid=(n_steps,),
        in_specs=[pl.BlockSpec((tm, D_MODEL), row),
                  pl.BlockSpec((None, 1, D_MODEL), per_b),
                  pl.BlockSpec((None, 1, D_MODEL), per_b),
                  pl.BlockSpec((1, D_MODEL), const),
                  pl.BlockSpec(w_main.shape, const, pipeline_mode=once),
                  pl.BlockSpec(w_g.shape, const, pipeline_mode=once),
                  pl.BlockSpec(ind.shape, const, pipeline_mode=once)],
        out_specs=[pl.BlockSpec((tm, w), row) for w, _ in outs] + [pl.BlockSpec((nrm_rows, LANES), row)],
        compiler_params=_cparams(1),
        name="inproj",
    )(x2, scale_m, shift_m, g_mix, w_main, w_g, ind)


def _log_sigmoid(x):
    return jnp.minimum(x, 0.0) - jnp.log1p(jnp.exp(-jnp.abs(x)))


def _cum_kernel(f_ref, b_ref, jm_ref, qa_ref, ka_ref, fb_ref):
    cum = _log_sigmoid(f_ref[...] + b_ref[...]) * LOG2E
    row = lax.broadcasted_iota(jnp.int32, cum.shape, 0)
    k = 1
    while k < SEQ:
        if k < 8:
            shifted = jnp.where(row >= k, pltpu.roll(cum, k, 0), 0.0)
        else:
            shifted = jnp.concatenate([jnp.zeros((k, LANES), F32), cum[:SEQ - k]], axis=0)
        cum = cum + shifted
        k *= 2
    jm = jm_ref[...]
    for blk in range(SEQ // LANES):
        rows = slice(blk * LANES, (blk + 1) * LANES)
        cb = cum[rows]
        carry = cb[LANES - 1:LANES]
        hi = cb.astype(BF16).astype(F32)
        r1 = cb - hi
        mid = r1.astype(BF16).astype(F32)
        lo = (r1 - mid).astype(BF16).astype(F32)
        piece = jnp.where((jm == 0) | (jm == 3), hi, jnp.where((jm == 1) | (jm == 4), mid, lo))
        q_const = jnp.where(jm < DECAY_LANES, 1.0, 0.0)
        k_const = jnp.where(jm < 3, 1.0, 0.0)
        qa_ref[rows, :] = jnp.where(jm < 3, piece, q_const).astype(BF16)
        ka_ref[rows, :] = jnp.where((jm >= 3) & (jm < DECAY_LANES), -piece, k_const).astype(BF16)
        blocks_per_tile = TQ_FOX // LANES
        tile = blk // blocks_per_tile
        if blk % blocks_per_tile == 0:
            fb_ref[2 * tile:2 * tile + 1, :] = cb[0:1]
        if blk % blocks_per_tile == blocks_per_tile - 1:
            fb_ref[2 * tile + 1:2 * tile + 2, :] = carry


def _cum(f_pad, b_pad, jmod):
    n_tiles = SEQ // TQ_FOX
    return pl.pallas_call(
        _cum_kernel,
        out_shape=[jax.ShapeDtypeStruct((BATCH, SEQ, LANES), BF16)] * 2
        + [jax.ShapeDtypeStruct((BATCH, 2 * n_tiles, LANES), F32)],
        grid=(BATCH,),
        in_specs=[pl.BlockSpec((SEQ, LANES), lambda b: (b, 0)),
                  pl.BlockSpec((1, LANES), lambda b: (0, 0)),
                  pl.BlockSpec((1, LANES), lambda b: (0, 0))],
        out_specs=[pl.BlockSpec((None, SEQ, LANES), lambda b: (b, 0, 0))] * 2
        + [pl.BlockSpec((None, 2 * n_tiles, LANES), lambda b: (b, 0, 0))],
        compiler_params=_cparams(1),
        name="cum",
    )(f_pad, b_pad, jmod)


def _swa_block(sink_cols, q, kk, vv, bias_ref, lo):
    tiles = []
    for g in range(N_KV_HEADS_SWA):
        parts = []
        for t in range(2):
            qt = q[:, (2 * g + t) * LANES:(2 * g + t + 1) * LANES]
            zero = jnp.zeros_like(qt)
            parts.append(jnp.where(lo, qt, zero))
            parts.append(jnp.where(lo, zero, qt))
        q4 = jnp.concatenate(parts, axis=0)
        s = lax.dot_general(q4, kk[:, g * LANES:(g + 1) * LANES], (((1,), (1,)), ((), ())),
                            preferred_element_type=F32)
        s = s + bias_ref[g]
        sink = sink_cols[g]
        m = jnp.maximum(jnp.max(s, axis=-1, keepdims=True), sink)
        p = jnp.exp2(s - m)
        den = jnp.sum(p, axis=-1, keepdims=True) + jnp.exp2(sink - m)
        o = jnp.dot(p.astype(BF16), vv[:, g * LANES:(g + 1) * LANES],
                    preferred_element_type=F32) / den
        tiles.append(jnp.where(lo, o[0:BLOCK], o[BLOCK:2 * BLOCK]))
        tiles.append(jnp.where(lo, o[2 * BLOCK:3 * BLOCK], o[3 * BLOCK:4 * BLOCK]))
    return tiles


def _swa_kernel(sink_ref, q_ref, kc_ref, kp_ref, vc_ref, vp_ref, bias_first_ref, bias_ref, o_ref):
    lane = lax.broadcasted_iota(jnp.int32, (BLOCK, LANES), 1)
    lo = lane < HEAD_DIM
    grp = N_HEADS_SWA // N_KV_HEADS_SWA
    row = lax.broadcasted_iota(jnp.int32, (grp * BLOCK, 1), 0)
    sink_cols = []
    for g in range(N_KV_HEADS_SWA):
        col = jnp.full((grp * BLOCK, 1), sink_ref[g * grp + grp - 1], F32)
        for hh in range(grp - 2, -1, -1):
            col = jnp.where(row < (hh + 1) * BLOCK, sink_ref[g * grp + hh], col)
        sink_cols.append(col)
    for blk in range(SWA_BLOCKS):
        rows = slice(blk * BLOCK, (blk + 1) * BLOCK)
        if blk == 0:
            kk = jnp.concatenate([kp_ref[...], kc_ref[rows, :]], axis=0)
            vv = jnp.concatenate([vp_ref[...], vc_ref[rows, :]], axis=0)
            bias = bias_first_ref
        else:
            prev_rows = slice((blk - 1) * BLOCK, (blk + 1) * BLOCK)
            kk = kc_ref[prev_rows, :]
            vv = vc_ref[prev_rows, :]
            bias = bias_ref
        tiles = _swa_block(sink_cols, q_ref[rows, :], kk, vv, bias, lo)
        for c, tile in enumerate(tiles):
            o_ref[rows, c * LANES:(c + 1) * LANES] = tile.astype(BF16)


def _swa(sinks, qa, kdup, va, bias):
    nb = SEQ // BLOCK
    ns = nb // SWA_BLOCKS
    cur = lambda b, i, s: (b * ns + i, 0)
    prev = lambda b, i, s: (b * nb + jnp.maximum(SWA_BLOCKS * i - 1, 0), 0)
    grid_spec = pltpu.PrefetchScalarGridSpec(
        num_scalar_prefetch=1,
        grid=(BATCH, ns),
        in_specs=[pl.BlockSpec((SWA_BLOCKS * BLOCK, Q_A), cur),
                  pl.BlockSpec((SWA_BLOCKS * BLOCK, 2 * KV_A), cur),
                  pl.BlockSpec((BLOCK, 2 * KV_A), prev),
                  pl.BlockSpec((SWA_BLOCKS * BLOCK, 2 * KV_A), cur),
                  pl.BlockSpec((BLOCK, 2 * KV_A), prev),
                  pl.BlockSpec((None,) + bias.shape[1:], lambda b, i, s: (jnp.minimum(i, 1), 0, 0, 0)),
                  pl.BlockSpec((None,) + bias.shape[1:], lambda b, i, s: (1, 0, 0, 0))],
        out_specs=pl.BlockSpec((SWA_BLOCKS * BLOCK, Q_A), cur))
    return pl.pallas_call(
        _swa_kernel,
        out_shape=jax.ShapeDtypeStruct((N_TOK, Q_A), BF16),
        grid_spec=grid_spec,
        compiler_params=_cparams(2),
        name="swa",
    )(sinks, qa, kdup, kdup, va, va, bias, bias)


def _fox_kernel(js_ref, q_ref, k_ref, v_ref, qa_ref, ka_ref, o_ref,
                kaug, vaug, q2, m_sc, acc_sc, s_0, s_1, s_2, s_3):
    tq, tk = TQ_FOX, TK_FOX
    b = pl.program_id(0)
    t = pl.program_id(1)
    g = pl.program_id(2)
    first_tile = (b * pl.num_programs(1) + t) * (SEQ // tq) + g * FOX_ROWS

    @pl.when(g == 0)
    def _():
        kaug[:, 0:LANES] = k_ref[...]
        kaug[:, LANES:2 * LANES] = ka_ref[...]
        vaug[:, 0:LANES] = v_ref[...]
        vaug[:, LANES:2 * LANES] = jnp.ones((SEQ, LANES), BF16)

    lane = lax.broadcasted_iota(jnp.int32, (tq, LANES), 1)
    lo = lane < HEAD_DIM
    base = 2 * DECAY_LANES * t
    own = [(lane >= base + h * DECAY_LANES) & (lane < base + (h + 1) * DECAY_LANES) for h in range(2)]
    for rw in range(FOX_ROWS):
        rows = slice(rw * tq, (rw + 1) * tq)
        q = q_ref[rows, :]
        qa = qa_ref[rows, :]
        zero = jnp.zeros_like(q)
        q2[rw, 0, :, 0:LANES] = jnp.where(lo, q, zero)
        q2[rw, 1, :, 0:LANES] = jnp.where(lo, zero, q)
        for h in range(2):
            q2[rw, h, :, LANES:2 * LANES] = jnp.where(own[h], qa, zero)

    rr = lax.broadcasted_iota(jnp.int32, (tq, tk), 0)
    cc = lax.broadcasted_iota(jnp.int32, (tq, tk), 1)
    causal = cc <= rr
    bufs = ((s_0, s_1), (s_2, s_3))
    j_starts = [js_ref[first_tile + rw] for rw in range(FOX_ROWS)]
    for rw in range(FOX_ROWS):
        nxt = rw + 1 if rw + 1 < FOX_ROWS else None
        _fox_query_tile(rw, g * FOX_ROWS + rw, j_starts[rw], bufs[rw % 2], q2, kaug, vaug, m_sc, acc_sc,
                        causal, lo, o_ref,
                        first_scores_done=rw > 0,
                        next_first=None if nxt is None else (nxt, j_starts[nxt], bufs[nxt % 2][0]))


def _fox_query_tile(rw, i, j_start, buf_pair, q2, kaug, vaug, m_sc, acc_sc, causal, lo, o_ref,
                    first_scores_done, next_first):
    tq, tk = TQ_FOX, TK_FOX
    s_a, s_b = buf_pair
    m_sc[...] = jnp.full(m_sc.shape, NEG_INF, F32)
    acc_sc[...] = jnp.zeros(acc_sc.shape, F32)

    def scores_of(row, h, ks):
        return lax.dot_general(q2[row, h], kaug[pl.ds(ks, tk), :], (((1,), (1,)), ((), ())),
                               preferred_element_type=F32)

    def scores(h, ks):
        return scores_of(rw, h, ks)

    def consume(h, s, ks, mask):
        if mask is not None:
            s = jnp.where(mask, s, NEG_INF)
        m_prev = m_sc[h]
        m_new = jnp.maximum(m_prev, jnp.max(s, axis=-1, keepdims=True))
        alpha = jnp.exp2(m_prev - m_new)
        p = jnp.exp2(s - jnp.concatenate([m_new] * (tk // LANES), axis=1))
        pv = jnp.dot(p.astype(BF16), vaug[pl.ds(ks, tk), :], preferred_element_type=F32)
        acc_sc[h] = jnp.concatenate([alpha, alpha], axis=1) * acc_sc[h] + pv
        m_sc[h] = m_new

    def key_start(j):
        return pl.multiple_of(j * tk, tk)

    def scores_into(buf, j):
        for h in range(2):
            buf[h] = scores(h, key_start(j))

    def consume_from(buf, j, mask):
        for h in range(2):
            consume(h, buf[h], key_start(j), mask)

    n_full = i - j_start

    if not first_scores_done:
        scores_into(s_a, j_start)

    def pair(p, carry):
        j = j_start + 2 * p
        scores_into(s_b, j + 1)
        consume_from(s_a, j, None)
        scores_into(s_a, j + 2)
        consume_from(s_b, j + 1, None)
        return carry

    lax.fori_loop(0, n_full // 2, pair, 0)
    odd = lax.rem(n_full, 2) == 1

    def start_next():
        if next_first is not None:
            nrow, nj, nbuf = next_first
            for h in range(2):
                nbuf[h] = scores_of(nrow, h, key_start(nj))

    @pl.when(odd)
    def _():
        scores_into(s_b, i)
        consume_from(s_a, i - 1, None)
        start_next()
        consume_from(s_b, i, causal)

    @pl.when(jnp.logical_not(odd))
    def _():
        start_next()
        consume_from(s_a, i, causal)

    outs = [acc_sc[h, :, 0:LANES] / acc_sc[h, :, LANES:2 * LANES] for h in range(2)]
    o_ref[rw * tq:(rw + 1) * tq, :] = jnp.where(lo, outs[0], outs[1]).astype(BF16)


def _fox(j_start, qb, kb, vb, qa, ka):
    tq = TQ_FOX
    ns = SEQ // (tq * FOX_ROWS)
    n_pairs = N_HEADS_FOX // 2
    qmap = lambda b, t, i, js: (b * ns + i, t)
    kmap = lambda b, t, i, js: (b, t)
    grid_spec = pltpu.PrefetchScalarGridSpec(
        num_scalar_prefetch=1,
        grid=(BATCH, n_pairs, ns),
        in_specs=[pl.BlockSpec((FOX_ROWS * tq, LANES), qmap),
                  pl.BlockSpec((SEQ, LANES), kmap),
                  pl.BlockSpec((SEQ, LANES), kmap),
                  pl.BlockSpec((None, FOX_ROWS * tq, LANES), lambda b, t, i, js: (b, i, 0)),
                  pl.BlockSpec((None, SEQ, LANES), lambda b, t, i, js: (b, 0, 0))],
        out_specs=pl.BlockSpec((FOX_ROWS * tq, LANES), qmap),
        scratch_shapes=[pltpu.VMEM((SEQ, 2 * LANES), BF16),
                        pltpu.VMEM((SEQ, 2 * LANES), BF16),
                        pltpu.VMEM((FOX_ROWS, 2, tq, 2 * LANES), BF16),
                        pltpu.VMEM((2, tq, LANES), F32),
                        pltpu.VMEM((2, tq, 2 * LANES), F32)]
        + [pltpu.VMEM((2, tq, TK_FOX), F32)] * 4)
    return pl.pallas_call(
        _fox_kernel,
        out_shape=jax.ShapeDtypeStruct((N_TOK, W_B), BF16),
        grid_spec=grid_spec,
        compiler_params=_cparams(3),
        name="fox",
    )(j_start, qb, kb, vb, qa, ka)


def _fox_first_tiles(nrm, fb):
    n_tiles = SEQ // TQ_FOX
    nr = nrm.reshape(BATCH, n_tiles, 8, LANES)[:, :, 0, :] * 1.02
    qn = jnp.sqrt(nr[..., 0:N_HEADS_FOX])
    kn = jnp.sqrt(nr[..., N_HEADS_FOX:2 * N_HEADS_FOX])
    f_first = fb[:, 0::2, 0:DECAY_LANES * N_HEADS_FOX:DECAY_LANES]
    f_last = fb[:, 1::2, 0:DECAY_LANES * N_HEADS_FOX:DECAY_LANES]
    kn_prefix = lax.cummax(kn, axis=1)
    upper = qn[:, :, None, :] * kn_prefix[:, None, :, :] + f_first[:, :, None, :] - f_last[:, None, :, :]
    row_max_low = -(qn * kn)[:, :, None, :]
    ii = jnp.arange(n_tiles)[None, :, None, None]
    jj = jnp.arange(n_tiles)[None, None, :, None]
    skip = (upper < row_max_low - PRUNE_MARGIN) & (jj < ii)
    skip = jnp.all(skip.reshape(BATCH, n_tiles, n_tiles, N_HEADS_FOX // 2, 2), axis=-1)
    first = jnp.min(jnp.where(skip, n_tiles, jj), axis=2)
    return jnp.transpose(first, (0, 2, 1)).reshape(-1).astype(jnp.int32)


def _post_kernel(x_ref, oa_ref, ob_ref, gt_ref, gm_ref, sc_ref, sh_ref, g_ref,
                 wa_ref, wb_ref, wo_ref, wr2_ref, br_ref,
                 x1_ref, xy_ref, rc_ref, cu_ref, hh_prev, lg_prev):
    step = pl.program_id(0)

    @pl.when(step == 0)
    def _():
        hh_prev[...] = jnp.zeros(hh_prev.shape, BF16)
        lg_prev[...] = jnp.zeros(lg_prev.shape, F32)

    @pl.when(step <= N_TOK_TILES)
    def _():
        hh_p = hh_prev[...]
        lg_p = lg_prev[...]
        hh, logits = _post_mix(x_ref, oa_ref, ob_ref, gt_ref, gm_ref, sc_ref, sh_ref, g_ref,
                               wa_ref, wb_ref, wo_ref, wr2_ref, br_ref, x1_ref)
        _post_route(hh_p, lg_p, xy_ref, rc_ref, cu_ref)
        hh_prev[...] = hh
        lg_prev[...] = logits

    @pl.when(step > N_TOK_TILES)
    def _():
        xy_ref[...] = jnp.zeros(xy_ref.shape, BF16)


def _post_mix(x_ref, oa_ref, ob_ref, gt_ref, gm_ref, sc_ref, sh_ref, g_ref,
              wa_ref, wb_ref, wo_ref, wr2_ref, br_ref, x1_ref):
    pa = jnp.dot(oa_ref[...], wa_ref[...], preferred_element_type=F32)
    pb = jnp.dot(ob_ref[...], wb_ref[...], preferred_element_type=F32)
    ga = jax.nn.sigmoid(gt_ref[:, 0:D_MODEL].astype(F32))
    gb = jax.nn.sigmoid(gt_ref[:, D_MODEL:2 * D_MODEL].astype(F32))
    merged = (ga * pa + gb * pb).astype(BF16)
    y = jnp.dot(merged, wo_ref[...], preferred_element_type=F32)
    x1 = x_ref[...] + gm_ref[...] * y
    x1_ref[...] = x1

    rs = lax.rsqrt(jnp.mean(x1 * x1, axis=-1, keepdims=True) + EPS)
    a = g_ref[...] * (1.0 + sc_ref[...])
    h2 = x1 * rs * a + sh_ref[...]

    hh = h2.astype(BF16)
    hl = (h2 - hh.astype(F32)).astype(BF16)
    hi_both = jnp.dot(hh, wr2_ref[...], preferred_element_type=F32)
    logits = (hi_both[:, 0:LANES] + hi_both[:, LANES:2 * LANES]
              + jnp.dot(hl, wr2_ref[:, 0:LANES], preferred_element_type=F32)
              + br_ref[...])
    return hh, logits


def _post_route(hh, logits, xy_ref, rc_ref, cu_ref):
    tm = TM_POST
    lane = lax.broadcasted_iota(jnp.int32, (tm, LANES), 1).astype(F32)
    big = float(LANES)
    gl = jnp.where(lane < N_GROUPS, logits, -jnp.inf)
    gmax = jnp.max(gl, axis=-1, keepdims=True)
    gi = jnp.min(jnp.where(gl == gmax, lane, big), axis=-1, keepdims=True)
    gsum = jnp.sum(jnp.exp(gl - gmax), axis=-1, keepdims=True)
    gp = 1.0 / gsum
    e_lo = N_GROUPS + EXPERTS_PER_GROUP * gi
    el = jnp.where((lane >= e_lo) & (lane < e_lo + EXPERTS_PER_GROUP), logits, -jnp.inf)
    v1 = jnp.max(el, axis=-1, keepdims=True)
    i1 = jnp.min(jnp.where(el == v1, lane, big), axis=-1, keepdims=True)
    el2 = jnp.where(lane == i1, -jnp.inf, el)
    v2 = jnp.max(el2, axis=-1, keepdims=True)
    i2 = jnp.min(jnp.where(el2 == v2, lane, big), axis=-1, keepdims=True)
    e21 = jnp.exp(v2 - v1)
    w1 = gp / (1.0 + e21)
    w2 = gp * e21 / (1.0 + e21)
    e1 = i1 - N_GROUPS
    e2 = i2 - N_GROUPS

    oh = jnp.where((lane == e1) | (lane == e2), 1.0, 0.0)
    cnt_u = jnp.floor((jnp.sum(oh, axis=0, keepdims=True) + (UNIT - 1)) * (1.0 / UNIT))
    r128 = lax.broadcasted_iota(jnp.int32, (LANES, LANES), 0)
    c128 = lax.broadcasted_iota(jnp.int32, (LANES, LANES), 1)
    before_lane = jnp.where(r128 < c128, 1.0, 0.0).astype(BF16)
    loc_u = jnp.dot(jnp.broadcast_to(cnt_u, (8, LANES)).astype(BF16), before_lane,
                    preferred_element_type=F32)
    trow = lax.broadcasted_iota(jnp.int32, (tm, LANES), 0)
    seen = oh
    k = 1
    while k < tm:
        if k < 8:
            shifted = jnp.where(trow >= k, pltpu.roll(seen, k, 0), 0.0)
        else:
            shifted = jnp.concatenate([jnp.zeros((k, LANES), F32), seen[:tm - k]], axis=0)
        seen = seen + shifted
        k *= 2
    pos_e = (seen - oh) + loc_u[0:1] * UNIT
    lp1 = jnp.sum(jnp.where(lane == e1, pos_e, 0.0), axis=-1, keepdims=True)
    lp2 = jnp.sum(jnp.where(lane == e2, pos_e, 0.0), axis=-1, keepdims=True)

    def to_row(col):
        return jnp.transpose(jnp.broadcast_to(col, (tm, LANES)))[0:1]

    srow = lax.broadcasted_iota(jnp.int32, (XY_ROWS, tm), 0).astype(F32)
    pm1 = jnp.where(srow == to_row(lp1), 1.0, 0.0).astype(BF16)
    pm2 = jnp.where(srow == to_row(lp2), 1.0, 0.0).astype(BF16)
    w1h = w1.astype(BF16).astype(F32)
    w2h = w2.astype(BF16).astype(F32)
    side = jnp.where(lane == W1_LANES[0], w1h, jnp.where(lane == W1_LANES[1], w1 - w1h,
           jnp.where(lane == W2_LANES[0], w2h, jnp.where(lane == W2_LANES[1], w2 - w2h,
           jnp.where(lane == E1_LANE, e1, jnp.where(lane == E2_LANE, e2, 0.0))))))
    tok = jnp.concatenate([hh, side.astype(BF16)], axis=1)
    xy_ref[...] = jnp.dot(pm1 + pm2, tok, preferred_element_type=F32).astype(BF16)

    cu_ref[...] = jnp.broadcast_to(cnt_u, cu_ref.shape)
    rc_ref[...] = jnp.where(lane == 0, lp1, jnp.where(lane == 1, lp2, 0.0))


def _post(x2, oa, ob, gates, gate_m, scale_f, shift_f, g_ffn, wa, wb, wo, wr2, b_r):
    tm = TM_POST
    tpb = SEQ // tm
    n_steps = N_TOK_TILES
    row = lambda i: (jnp.minimum(i, n_steps - 1), 0)
    per_b = lambda i: (jnp.minimum(i, n_steps - 1) // tpb, 0, 0)
    routed = lambda i: (jnp.clip(i - 1, 0, n_steps - 1), 0)
    const = lambda i: (0, 0)
    return pl.pallas_call(
        _post_kernel,
        out_shape=[jax.ShapeDtypeStruct((N_TOK, D_MODEL), F32),
                   jax.ShapeDtypeStruct(((n_steps + PAD_BLOCKS) * XY_ROWS, XY_COLS), BF16),
                   jax.ShapeDtypeStruct((N_TOK, LANES), F32),
                   jax.ShapeDtypeStruct((n_steps * 8, LANES), F32)],
        grid=(n_steps + 1 + PAD_BLOCKS,),
        in_specs=[pl.BlockSpec((tm, D_MODEL), row),
                  pl.BlockSpec((tm, Q_A), row),
                  pl.BlockSpec((tm, W_B), row),
                  pl.BlockSpec((tm, 2 * D_MODEL), row),
                  pl.BlockSpec((None, 1, D_MODEL), per_b),
                  pl.BlockSpec((None, 1, D_MODEL), per_b),
                  pl.BlockSpec((None, 1, D_MODEL), per_b),
                  pl.BlockSpec((1, D_MODEL), const),
                  pl.BlockSpec(wa.shape, const),
                  pl.BlockSpec(wb.shape, const),
                  pl.BlockSpec(wo.shape, const),
                  pl.BlockSpec(wr2.shape, const),
                  pl.BlockSpec((1, LANES), const)],
        out_specs=[pl.BlockSpec((tm, D_MODEL), row),
                   pl.BlockSpec((XY_ROWS, XY_COLS), lambda i: (jnp.maximum(i - 1, 0), 0)),
                   pl.BlockSpec((tm, LANES), routed),
                   pl.BlockSpec((8, LANES), routed)],
        scratch_shapes=[pltpu.VMEM((tm, D_MODEL), BF16),
                        pltpu.VMEM((tm, LANES), F32)],
        compiler_params=_cparams(1),
        name="post",
    )(x2, oa, ob, gates, gate_m, scale_f, shift_f, g_ffn, wa, wb, wo, wr2, b_r)


def _experts_kernel(te_ref, nu_ref, ur_ref, ne_ref, ep_ref, xy_in, wg_hbm, wu_hbm, wd_hbm, xy_out,
                    xbuf, ybuf, wg_s, wu_s, wd_s, wg_f, wu_f, wd_f, gsem, ssem, wsem):
    del xy_in
    r = pl.program_id(0)
    last = pl.num_programs(0) - 1
    n_used = nu_ref[0]
    slot = lax.rem(r, 2)

    def unit_row(step, s):
        return pl.multiple_of(ur_ref[step * UNITS_PER_TILE + s], UNIT)

    def start_gathers(step, sl):
        for s in range(UNITS_PER_TILE):
            pltpu.make_async_copy(xy_out.at[pl.ds(unit_row(step, s), UNIT), :],
                                  xbuf.at[sl, pl.ds(s * UNIT, UNIT), :], gsem.at[sl]).start()

    def wait_gathers(sl):
        pltpu.make_async_copy(xy_out.at[pl.ds(0, TM_EXP), :], xbuf.at[sl], gsem.at[sl]).wait()

    def start_scatters(step, sl):
        for s in range(UNITS_PER_TILE):
            pltpu.make_async_copy(ybuf.at[sl, pl.ds(s * UNIT, UNIT), :],
                                  xy_out.at[pl.ds(unit_row(step, s), UNIT), pl.ds(0, D_MODEL)],
                                  ssem.at[sl]).start()

    def wait_scatters(sl):
        pltpu.make_async_copy(ybuf.at[sl], xy_out.at[pl.ds(0, TM_EXP), pl.ds(0, D_MODEL)], ssem.at[sl]).wait()

    @pl.when(r == 0)
    def _():
        start_gathers(0, 0)

    @pl.when(jnp.logical_and(r < n_used, r >= 2))
    def _():
        wait_scatters(slot)

    def weight_copies(e, p):
        return [pltpu.make_async_copy(src.at[e], dst.at[p], wsem.at[p])
                for src, dst in ((wg_hbm, wg_f), (wu_hbm, wu_f), (wd_hbm, wd_f))]

    @pl.when(jnp.logical_and(r < n_used,
                             jnp.logical_or(r == 0, te_ref[r] != te_ref[jnp.maximum(r - 1, 0)])))
    def _():
        e = te_ref[r]
        p = ep_ref[r]

        @pl.when(r == 0)
        def _():
            for cp in weight_copies(e, p):
                cp.start()

        for cp in weight_copies(e, p):
            cp.wait()
        wg_s[...] = wg_f[p].astype(BF16)
        wu_s[...] = wu_f[p].astype(BF16)
        wd_s[...] = wd_f[p].astype(BF16)

        @pl.when(ne_ref[r] >= 0)
        def _():
            for cp in weight_copies(ne_ref[r], 1 - p):
                cp.start()

    @pl.when(r < n_used)
    def _():
        wait_gathers(slot)
        start_gathers(jnp.minimum(r + 1, last), 1 - slot)
        x = xbuf[slot, :, 0:D_MODEL]
        side = xbuf[slot, :, D_MODEL:XY_COLS].astype(F32)
        lane = lax.broadcasted_iota(jnp.int32, side.shape, 1)

        def lanes_sum(a, b):
            return jnp.sum(jnp.where((lane == a) | (lane == b), side, 0.0), axis=-1, keepdims=True)

        is_slot1 = lanes_sum(E1_LANE, E1_LANE) == te_ref[r].astype(F32)
        wrow = jnp.where(is_slot1, lanes_sum(*W1_LANES), lanes_sum(*W2_LANES))
        a = jnp.dot(x, wg_s[...], preferred_element_type=F32)
        u = jnp.dot(x, wu_s[...], preferred_element_type=F32)
        hid = (a * jax.nn.sigmoid(a) * u * wrow).astype(BF16)
        ybuf[slot] = jnp.dot(hid, wd_s[...], preferred_element_type=F32).astype(BF16)
        start_scatters(r, slot)

    @pl.when(r == n_used - 1)
    def _():
        wait_gathers(1 - slot)
        wait_scatters(slot)

        @pl.when(r >= 1)
        def _():
            wait_scatters(1 - slot)


def _experts(tile_expert, n_used, unit_rows, next_expert, expert_parity, xy, wg, wu, wd):
    grid_spec = pltpu.PrefetchScalarGridSpec(
        num_scalar_prefetch=5,
        grid=(N_EXP_TILES,),
        in_specs=[pl.BlockSpec(memory_space=pl.ANY),
                  pl.BlockSpec(memory_space=pl.ANY),
                  pl.BlockSpec(memory_space=pl.ANY),
                  pl.BlockSpec(memory_space=pl.ANY)],
        out_specs=pl.BlockSpec(memory_space=pl.ANY),
        scratch_shapes=[pltpu.VMEM((2, TM_EXP, XY_COLS), BF16),
                        pltpu.VMEM((2, TM_EXP, D_MODEL), BF16),
                        pltpu.VMEM((D_MODEL, D_FF_EXPERT), BF16),
                        pltpu.VMEM((D_MODEL, D_FF_EXPERT), BF16),
                        pltpu.VMEM((D_FF_EXPERT, D_MODEL), BF16),
                        pltpu.VMEM((2, D_MODEL, D_FF_EXPERT), F32),
                        pltpu.VMEM((2, D_MODEL, D_FF_EXPERT), F32),
                        pltpu.VMEM((2, D_FF_EXPERT, D_MODEL), F32),
                        pltpu.SemaphoreType.DMA((2,)),
                        pltpu.SemaphoreType.DMA((2,)),
                        pltpu.SemaphoreType.DMA((2,))])
    return pl.pallas_call(
        _experts_kernel,
        out_shape=jax.ShapeDtypeStruct(xy.shape, xy.dtype),
        grid_spec=grid_spec,
        input_output_aliases={5: 0},
        compiler_params=_cparams(1),
        name="experts",
    )(tile_expert, n_used, unit_rows, next_expert, expert_parity, xy, wg, wu, wd)


def _combine_kernel(x1_ref, rc_ref, gf_ref, gfin_ref, y_ref, o_ref):
    for tl in range(COMBINE_TILES):
        rows = slice(tl * TM_ROW, (tl + 1) * TM_ROW)
        lp1 = rc_ref[rows, 0:1]
        lp2 = rc_ref[rows, 1:2]
        scol = lax.broadcasted_iota(jnp.int32, (TM_ROW, XY_ROWS), 1).astype(F32)
        pick = jnp.where((scol == lp1) | (scol == lp2), 1.0, 0.0).astype(BF16)
        y = jnp.dot(pick, y_ref[tl * XY_ROWS:(tl + 1) * XY_ROWS, :], preferred_element_type=F32)
        xf = x1_ref[rows, :] + gf_ref[...] * y
        rs = lax.rsqrt(jnp.mean(xf * xf, axis=-1, keepdims=True) + EPS)
        o_ref[rows, :] = xf * rs * gfin_ref[...]


def _combine(x1, rcol, gate_f, g_final, xy):
    tm = TM_ROW * COMBINE_TILES
    tpb = SEQ // tm
    row = lambda i: (i, 0)
    return pl.pallas_call(
        _combine_kernel,
        out_shape=jax.ShapeDtypeStruct((N_TOK, D_MODEL), F32),
        grid=(N_TOK // tm,),
        in_specs=[pl.BlockSpec((tm, D_MODEL), row),
                  pl.BlockSpec((tm, LANES), row),
                  pl.BlockSpec((None, 1, D_MODEL), lambda i: (i // tpb, 0, 0)),
                  pl.BlockSpec((1, D_MODEL), lambda i: (0, 0)),
                  pl.BlockSpec((COMBINE_TILES * XY_ROWS, D_MODEL), row)],
        out_specs=pl.BlockSpec((tm, D_MODEL), row),
        compiler_params=_cparams(1),
        name="combine",
    )(x1, rcol, gate_f, g_final, xy)


def _t5_bucket_np():
    qi = np.arange(BLOCK)[:, None]
    kj = np.arange(2 * BLOCK)[None, :]
    dist = qi - kj + BLOCK
    n = np.maximum(dist, 0)
    max_exact = NUM_BUCKETS // 2
    nf = np.maximum(n, 1).astype(np.float32)
    large = max_exact + (np.log(nf / np.float32(max_exact)) / np.float32(math.log(MAX_DISTANCE / max_exact))
                         * np.float32(NUM_BUCKETS - max_exact)).astype(np.int32)
    large = np.minimum(large, NUM_BUCKETS - 1)
    bucket = np.where(n < max_exact, n, large)
    band = (dist >= 0) & (dist < WINDOW)
    return bucket.astype(np.int32), band


def kernel(x, c, w_ada, b_ada, g_norm_mix, g_norm_ffn, w_in, sinks, b_forget, w_proj_swa, w_proj_fox,
           w_out, rel_bias_table, w_router_group, b_router_group, w_router_expert, b_router_expert,
           w_gate_exp, w_up_exp, w_down_exp, g_final):
    l = 0
    x2 = x.reshape(N_TOK, D_MODEL)

    c16 = jnp.concatenate([c, jnp.zeros_like(c)], axis=0)
    mod = _ada(c16, w_ada[l], b_ada[l][None, :])[:BATCH]
    shift_m, scale_m, gate_m, shift_f, scale_f, gate_f = [
        m.reshape(BATCH, 1, D_MODEL) for m in jnp.split(mod, 6, axis=-1)]

    w = w_in[l]
    o_ka, o_va, o_qb = Q_A, Q_A + KV_A, Q_A + 2 * KV_A
    o_kb, o_vb, o_f = o_qb + W_B, o_qb + 2 * W_B, o_qb + 3 * W_B
    o_g = o_f + N_HEADS_FOX

    def dup(cols):
        heads = [cols[:, h * HEAD_DIM:(h + 1) * HEAD_DIM] for h in range(N_KV_HEADS_SWA)]
        return jnp.concatenate([hd for hd in heads for _ in range(2)], axis=1)

    head_order = jnp.argsort(b_forget[l])

    def reorder_heads(cols):
        return jnp.take(cols.reshape(D_MODEL, N_HEADS_FOX, HEAD_DIM), head_order, axis=1).reshape(D_MODEL, W_B)

    w_fox = jnp.concatenate([reorder_heads(w[:, o_qb:o_kb]), reorder_heads(w[:, o_kb:o_vb]),
                             reorder_heads(w[:, o_vb:o_f])], axis=1)
    b_fox = jnp.take(b_forget[l], head_order)
    w_proj_b = jnp.take(w_proj_fox[l].reshape(N_HEADS_FOX, HEAD_DIM, D_MODEL), head_order, axis=0).reshape(W_B, D_MODEL)
    carrier = DECAY_LANES * N_HEADS_FOX
    w_f = jnp.pad(jnp.repeat(jnp.take(w[:, o_f:o_g], head_order, axis=1), DECAY_LANES, axis=1),
                  ((0, 0), (0, LANES - carrier)))
    w_main = jnp.concatenate([w[:, :Q_A], dup(w[:, o_ka:o_va]), w[:, o_va:o_qb], w_f, w_fox], axis=1).astype(BF16)
    w_g = w[:, o_g:].astype(BF16)
    qa, kdup, va, qb, kb, vb, f_pad, gates, nrm = _inproj(
        x2, scale_m, shift_m, g_norm_mix[l][None, :], w_main, w_g)

    b_pad = jnp.pad(jnp.repeat(b_fox, DECAY_LANES), (0, LANES - carrier))[None, :]
    lanes = np.arange(LANES)
    jmod = jnp.asarray(np.where(lanes < DECAY_LANES * N_HEADS_FOX, lanes % DECAY_LANES, 7)[None, :].astype(np.int32))
    dq, dk, fb = _cum(f_pad, b_pad, jmod)

    bucket, band = _t5_bucket_np()
    onehot = jnp.asarray(bucket[None] == np.arange(NUM_BUCKETS)[:, None, None], dtype=F32)
    bias = jnp.einsum("bh,bqk->hqk", rel_bias_table.astype(F32), onehot, precision=HIGHEST)
    bias = jnp.where(band[None], bias * LOG2E, NEG_INF)
    first = np.arange(2 * BLOCK)[None, None, :] < BLOCK
    bias = jnp.stack([jnp.where(first, NEG_INF, bias), bias]).reshape(2, N_KV_HEADS_SWA, -1, 2 * BLOCK)
    o_a = _swa(sinks[l].astype(F32) * LOG2E, qa, kdup, va, bias)

    o_b = _fox(_fox_first_tiles(nrm, fb), qb, kb, vb, dq, dk)

    w_r = jnp.concatenate([w_router_group[l]] + [w_router_expert[l][g] for g in range(N_GROUPS)], axis=1)
    w_r = jnp.pad(w_r, ((0, 0), (0, LANES - w_r.shape[1])))
    wr_hi = w_r.astype(BF16)
    wr_lo = (w_r - wr_hi.astype(F32)).astype(BF16)
    wr2 = jnp.concatenate([wr_hi, wr_lo], axis=1)
    b_r = jnp.concatenate([b_router_group[l], b_router_expert[l].reshape(-1)])
    b_r = jnp.pad(b_r, (0, LANES - b_r.shape[0]))[None, :]
    x1, xy, rcol, cu = _post(x2, o_a, o_b, gates, gate_m, scale_f, shift_f, g_norm_ffn[l][None, :],
                             w_proj_swa[l].astype(BF16), w_proj_b.astype(BF16), w_out[l].astype(BF16),
                             wr2, b_r)

    i32 = jnp.int32
    n_tok_tiles = N_TOK // TM_POST
    cu = cu.reshape(n_tok_tiles, 8, LANES)[:, 0, :N_EXPERTS].astype(i32)
    loc_u = jnp.cumsum(cu, axis=1) - cu
    cend = jnp.cumsum(cu, axis=0)
    cstart = cend - cu
    tot_u = cend[-1]
    tiles_e = (tot_u + UNITS_PER_TILE - 1) // UNITS_PER_TILE
    tile_end = jnp.cumsum(tiles_e)
    tile_start = tile_end - tiles_e
    r = jnp.arange(N_EXP_TILES, dtype=i32)
    tile_expert = jnp.minimum(jnp.sum((tile_end[None, :] <= r[:, None]).astype(i32), axis=1), N_EXPERTS - 1)
    sel_e = tile_expert[:, None] == jnp.arange(N_EXPERTS, dtype=i32)[None, :]
    tw = r - jnp.sum(jnp.where(sel_e, tile_start[None, :], 0), axis=1)
    tot_r = jnp.sum(jnp.where(sel_e, tot_u[None, :], 0), axis=1)
    n_used = tile_end[-1:].astype(i32)
    q = tw[:, None] * UNITS_PER_TILE + jnp.arange(UNITS_PER_TILE, dtype=i32)[None, :]

    def of_expert(tab):
        return jnp.sum(jnp.where(sel_e[:, None, :], tab[None, :, :], 0), axis=2)

    cend_r, cstart_r, loc_r = of_expert(cend), of_expert(cstart), of_expert(loc_u)
    src_tile = jnp.minimum(jnp.sum((cend_r[:, None, :] <= q[:, :, None]).astype(i32), axis=2), n_tok_tiles - 1)
    sel_t = src_tile[:, :, None] == jnp.arange(n_tok_tiles, dtype=i32)[None, None, :]
    k = (q - jnp.sum(jnp.where(sel_t, cstart_r[:, None, :], 0), axis=2)
         + jnp.sum(jnp.where(sel_t, loc_r[:, None, :], 0), axis=2))
    real_rows = src_tile * XY_ROWS + k * UNIT
    pad_rows = PAD_BASE_ROW + (tile_expert[:, None] * PAD_UNITS_PER_EXPERT + (q - tot_r[:, None])) * UNIT
    idle_row = PAD_BASE_ROW + N_EXPERTS * PAD_UNITS_PER_EXPERT * UNIT
    unit_rows = jnp.where(q < tot_r[:, None], real_rows, pad_rows)
    unit_rows = jnp.where((r < n_used)[:, None], unit_rows, idle_row).reshape(-1).astype(i32)

    eid = jnp.arange(N_EXPERTS, dtype=i32)
    used = tiles_e > 0
    later_used = (eid[None, :] > eid[:, None]) & used[None, :]
    next_e = jnp.min(jnp.where(later_used, eid[None, :], N_EXPERTS), axis=1)
    next_e = jnp.where(next_e == N_EXPERTS, -1, next_e)
    parity_e = (jnp.cumsum(used.astype(i32)) - used.astype(i32)) % 2
    next_expert = jnp.sum(jnp.where(sel_e, next_e[None, :], 0), axis=1).astype(i32)
    expert_parity = jnp.sum(jnp.where(sel_e, parity_e[None, :], 0), axis=1).astype(i32)

    xy = _experts(tile_expert.astype(i32), n_used, unit_rows, next_expert, expert_parity, xy,
                  w_gate_exp[l].reshape(N_EXPERTS, D_MODEL, D_FF_EXPERT),
                  w_up_exp[l].reshape(N_EXPERTS, D_MODEL, D_FF_EXPERT),
                  w_down_exp[l].reshape(N_EXPERTS, D_FF_EXPERT, D_MODEL))
    out = _combine(x1, rcol, gate_f, g_final[None, :], xy)
    return out.reshape(BATCH, SEQ, D_MODEL)
```

```python
import math

import numpy as np
import jax
import jax.numpy as jnp
from jax import lax
from jax.experimental import pallas as pl
from jax.experimental.pallas import tpu as pltpu

F32 = jnp.float32
BF16 = jnp.bfloat16
HIGHEST = lax.Precision.HIGHEST

D_MODEL = 1024
BATCH = 8
SEQ = 4096
N_TOK = BATCH * SEQ
N_HEADS_SWA = 8
N_KV_HEADS_SWA = 2
N_HEADS_FOX = 8
HEAD_DIM = 64
WINDOW = 128
BLOCK = 128
NUM_BUCKETS = 32
MAX_DISTANCE = 128
N_GROUPS = 4
EXPERTS_PER_GROUP = 8
N_EXPERTS = N_GROUPS * EXPERTS_PER_GROUP
D_FF_EXPERT = 256
EPS = 1e-6
NEG_INF = -1e30

Q_A = N_HEADS_SWA * HEAD_DIM
KV_A = N_KV_HEADS_SWA * HEAD_DIM
W_B = N_HEADS_FOX * HEAD_DIM
LANES = 128
QK_SCALE = HEAD_DIM ** -0.5

TM_IN = 1024
TM_POST = 512
TQ_FOX = 512
TK_FOX = TQ_FOX
FOX_ROWS = 4
SWA_BLOCKS = 8
TM_EXP = 512
TM_ROW = 512
COMBINE_TILES = 2
UNIT = 16
XY_UNITS = 2 * TM_POST // UNIT + N_EXPERTS
XY_ROWS = XY_UNITS * UNIT
XY_COLS = D_MODEL + LANES
UNITS_PER_TILE = TM_EXP // UNIT
N_TOK_TILES = N_TOK // TM_POST
N_EXP_TILES = N_TOK_TILES * XY_UNITS // UNITS_PER_TILE + N_EXPERTS
PAD_UNITS_PER_EXPERT = UNITS_PER_TILE - 1
PAD_BLOCKS = -(-(N_EXPERTS * PAD_UNITS_PER_EXPERT * UNIT) // XY_ROWS)
PAD_BASE_ROW = N_TOK_TILES * XY_ROWS
W1_LANES, W2_LANES, E1_LANE, E2_LANE = (4, 6), (5, 7), 8, 9
VMEM_LIMIT = 56 * 1024 * 1024

DECAY_LANES = 6
LOG2E = math.log2(math.e)
Q_SCALE_LOG2 = QK_SCALE * LOG2E
PRUNE_MARGIN = 160.0


def _cparams(n_axes):
    return pltpu.CompilerParams(dimension_semantics=("arbitrary",) * n_axes,
                                vmem_limit_bytes=VMEM_LIMIT)


def _ada_kernel(c_ref, w_ref, b_ref, o_ref):
    c = c_ref[...]
    ca = c * jax.nn.sigmoid(c)
    o_ref[...] = jnp.dot(ca.astype(BF16), w_ref[...].astype(BF16),
                         preferred_element_type=F32) + b_ref[...]


def _ada(c16, w_ada, b_ada):
    n_out = w_ada.shape[1]
    blk = n_out // 2
    return pl.pallas_call(
        _ada_kernel,
        out_shape=jax.ShapeDtypeStruct((16, n_out), F32),
        grid=(n_out // blk,),
        in_specs=[pl.BlockSpec((16, D_MODEL), lambda j: (0, 0)),
                  pl.BlockSpec((D_MODEL, blk), lambda j: (0, j)),
                  pl.BlockSpec((1, blk), lambda j: (0, j))],
        out_specs=pl.BlockSpec((16, blk), lambda j: (0, j)),
        compiler_params=_cparams(1),
        name="ada",
    )(c16, w_ada, b_ada)


def _inproj_kernel(x_ref, sc_ref, sh_ref, g_ref, wm_ref, wg_ref, ind_ref,
                   qa_ref, kd_ref, va_ref, qb_ref, kb_ref, vb_ref, f_ref, gt_ref, nrm_ref):
    x = x_ref[...]
    rs = lax.rsqrt(jnp.mean(x * x, axis=-1, keepdims=True) + EPS)
    a = g_ref[...] * (1.0 + sc_ref[...])
    h = (x * rs * a + sh_ref[...]).astype(BF16)

    def mm(w):
        return jnp.dot(h, w, preferred_element_type=F32)

    qa_ref[...] = (mm(wm_ref[:, 0:512]) * Q_SCALE_LOG2).astype(BF16)
    kd_ref[...] = mm(wm_ref[:, 512:768]).astype(BF16)
    vf = mm(wm_ref[:, 768:1024])
    f_ref[...] = vf[:, LANES:2 * LANES]
    v = vf[:, 0:LANES]
    vr = pltpu.roll(v, HEAD_DIM, 1)
    lo = lax.broadcasted_iota(jnp.int32, v.shape, 1) < HEAD_DIM
    va_ref[:, 0:LANES] = jnp.where(lo, v, vr).astype(BF16)
    va_ref[:, LANES:2 * LANES] = jnp.where(lo, vr, v).astype(BF16)
    qb = (mm(wm_ref[:, 1024:1536]) * Q_SCALE_LOG2).astype(BF16)
    kb = mm(wm_ref[:, 1536:2048]).astype(BF16)
    qb_ref[...] = qb
    kb_ref[...] = kb
    vb_ref[...] = mm(wm_ref[:, 2048:2560]).astype(BF16)
    sq = jnp.concatenate([qb, kb], axis=1).astype(F32)
    seg = jnp.dot((sq * sq).astype(BF16), ind_ref[...], preferred_element_type=F32)
    for sub in range(TM_IN // TQ_FOX):
        tile_max = jnp.max(seg[sub * TQ_FOX:(sub + 1) * TQ_FOX], axis=0, keepdims=True)
        nrm_ref[8 * sub:8 * sub + 8, :] = jnp.broadcast_to(tile_max, (8, LANES))
    gt_ref[...] = mm(wg_ref[...]).astype(BF16)


def _inproj(x2, scale_m, shift_m, g_mix, w_main, w_g):
    tm = TM_IN
    tpb = SEQ // tm
    row = lambda i: (i, 0)
    per_b = lambda i: (i // tpb, 0, 0)
    const = lambda i: (0, 0)
    outs = [(Q_A, BF16), (2 * KV_A, BF16), (2 * KV_A, BF16), (W_B, BF16), (W_B, BF16), (W_B, BF16),
            (LANES, F32), (2 * D_MODEL, BF16)]
    ind_np = np.zeros((2 * W_B, LANES), np.float32)
    ind_np[np.arange(2 * W_B), np.arange(2 * W_B) // HEAD_DIM] = 1.0
    ind = jnp.asarray(ind_np, dtype=BF16)
    n_steps = N_TOK // tm
    nrm_rows = 8 * (tm // TQ_FOX)
    once = pl.Buffered(1)
    return pl.pallas_call(
        _inproj_kernel,
        out_shape=[jax.ShapeDtypeStruct((N_TOK, w), dt) for w, dt in outs]
        + [jax.ShapeDtypeStruct((n_steps * nrm_rows, LANES), F32)],
        grid=(n_steps,),
        in_specs=[pl.BlockSpec((tm, D_MODEL), row),
                  pl.BlockSpec((None, 1, D_MODEL), per_b),
                  pl.BlockSpec((None, 1, D_MODEL), per_b),
                  pl.BlockSpec((1, D_MODEL), const),
                  pl.BlockSpec(w_main.shape, const, pipeline_mode=once),
                  pl.BlockSpec(w_g.shape, const, pipeline_mode=once),
                  pl.BlockSpec(ind.shape, const, pipeline_mode=once)],
        out_specs=[pl.BlockSpec((tm, w), row) for w, _ in outs] + [pl.BlockSpec((nrm_rows, LANES), row)],
        compiler_params=_cparams(1),
        name="inproj",
    )(x2, scale_m, shift_m, g_mix, w_main, w_g, ind)


def _log_sigmoid(x):
    return jnp.minimum(x, 0.0) - jnp.log1p(jnp.exp(-jnp.abs(x)))


def _cum_kernel(f_ref, b_ref, jm_ref, qa_ref, ka_ref, fb_ref):
    cum = _log_sigmoid(f_ref[...] + b_ref[...]) * LOG2E
    row = lax.broadcasted_iota(jnp.int32, cum.shape, 0)
    k = 1
    while k < SEQ:
        if k < 8:
            shifted = jnp.where(row >= k, pltpu.roll(cum, k, 0), 0.0)
        else:
            shifted = jnp.concatenate([jnp.zeros((k, LANES), F32), cum[:SEQ - k]], axis=0)
        cum = cum + shifted
        k *= 2
    jm = jm_ref[...]
    for blk in range(SEQ // LANES):
        rows = slice(blk * LANES, (blk + 1) * LANES)
        cb = cum[rows]
        carry = cb[LANES - 1:LANES]
        hi = cb.astype(BF16).astype(F32)
        r1 = cb - hi
        mid = r1.astype(BF16).astype(F32)
        lo = (r1 - mid).astype(BF16).astype(F32)
        piece = jnp.where((jm == 0) | (jm == 3), hi, jnp.where((jm == 1) | (jm == 4), mid, lo))
        q_const = jnp.where(jm < DECAY_LANES, 1.0, 0.0)
        k_const = jnp.where(jm < 3, 1.0, 0.0)
        qa_ref[rows, :] = jnp.where(jm < 3, piece, q_const).astype(BF16)
        ka_ref[rows, :] = jnp.where((jm >= 3) & (jm < DECAY_LANES), -piece, k_const).astype(BF16)
        blocks_per_tile = TQ_FOX // LANES
        tile = blk // blocks_per_tile
        if blk % blocks_per_tile == 0:
            fb_ref[2 * tile:2 * tile + 1, :] = cb[0:1]
        if blk % blocks_per_tile == blocks_per_tile - 1:
            fb_ref[2 * tile + 1:2 * tile + 2, :] = carry


def _cum(f_pad, b_pad, jmod):
    n_tiles = SEQ // TQ_FOX
    return pl.pallas_call(
        _cum_kernel,
        out_shape=[jax.ShapeDtypeStruct((BATCH, SEQ, LANES), BF16)] * 2
        + [jax.ShapeDtypeStruct((BATCH, 2 * n_tiles, LANES), F32)],
        grid=(BATCH,),
        in_specs=[pl.BlockSpec((SEQ, LANES), lambda b: (b, 0)),
                  pl.BlockSpec((1, LANES), lambda b: (0, 0)),
                  pl.BlockSpec((1, LANES), lambda b: (0, 0))],
        out_specs=[pl.BlockSpec((None, SEQ, LANES), lambda b: (b, 0, 0))] * 2
        + [pl.BlockSpec((None, 2 * n_tiles, LANES), lambda b: (b, 0, 0))],
        compiler_params=_cparams(1),
        name="cum",
    )(f_pad, b_pad, jmod)


def _swa_block(sink_cols, q, kk, vv, bias_ref, lo):
    tiles = []
    for g in range(N_KV_HEADS_SWA):
        parts = []
        for t in range(2):
            qt = q[:, (2 * g + t) * LANES:(2 * g + t + 1) * LANES]
            zero = jnp.zeros_like(qt)
            parts.append(jnp.where(lo, qt, zero))
            parts.append(jnp.where(lo, zero, qt))
        q4 = jnp.concatenate(parts, axis=0)
        s = lax.dot_general(q4, kk[:, g * LANES:(g + 1) * LANES], (((1,), (1,)), ((), ())),
                            preferred_element_type=F32)
        s = s + bias_ref[g]
        sink = sink_cols[g]
        m = jnp.maximum(jnp.max(s, axis=-1, keepdims=True), sink)
        p = jnp.exp2(s - m)
        den = jnp.sum(p, axis=-1, keepdims=True) + jnp.exp2(sink - m)
        o = jnp.dot(p.astype(BF16), vv[:, g * LANES:(g + 1) * LANES],
                    preferred_element_type=F32) / den
        tiles.append(jnp.where(lo, o[0:BLOCK], o[BLOCK:2 * BLOCK]))
        tiles.append(jnp.where(lo, o[2 * BLOCK:3 * BLOCK], o[3 * BLOCK:4 * BLOCK]))
    return tiles


def _swa_kernel(sink_ref, q_ref, kc_ref, kp_ref, vc_ref, vp_ref, bias_first_ref, bias_ref, o_ref):
    lane = lax.broadcasted_iota(jnp.int32, (BLOCK, LANES), 1)
    lo = lane < HEAD_DIM
    grp = N_HEADS_SWA // N_KV_HEADS_SWA
    row = lax.broadcasted_iota(jnp.int32, (grp * BLOCK, 1), 0)
    sink_cols = []
    for g in range(N_KV_HEADS_SWA):
        col = jnp.full((grp * BLOCK, 1), sink_ref[g * grp + grp - 1], F32)
        for hh in range(grp - 2, -1, -1):
            col = jnp.where(row < (hh + 1) * BLOCK, sink_ref[g * grp + hh], col)
        sink_cols.append(col)
    for blk in range(SWA_BLOCKS):
        rows = slice(blk * BLOCK, (blk + 1) * BLOCK)
        if blk == 0:
            kk = jnp.concatenate([kp_ref[...], kc_ref[rows, :]], axis=0)
            vv = jnp.concatenate([vp_ref[...], vc_ref[rows, :]], axis=0)
            bias = bias_first_ref
        else:
            prev_rows = slice((blk - 1) * BLOCK, (blk + 1) * BLOCK)
            kk = kc_ref[prev_rows, :]
            vv = vc_ref[prev_rows, :]
            bias = bias_ref
        tiles = _swa_block(sink_cols, q_ref[rows, :], kk, vv, bias, lo)
        for c, tile in enumerate(tiles):
            o_ref[rows, c * LANES:(c + 1) * LANES] = tile.astype(BF16)


def _swa(sinks, qa, kdup, va, bias):
    nb = SEQ // BLOCK
    ns = nb // SWA_BLOCKS
    cur = lambda b, i, s: (b * ns + i, 0)
    prev = lambda b, i, s: (b * nb + jnp.maximum(SWA_BLOCKS * i - 1, 0), 0)
    grid_spec = pltpu.PrefetchScalarGridSpec(
        num_scalar_prefetch=1,
        grid=(BATCH, ns),
        in_specs=[pl.BlockSpec((SWA_BLOCKS * BLOCK, Q_A), cur),
                  pl.BlockSpec((SWA_BLOCKS * BLOCK, 2 * KV_A), cur),
                  pl.BlockSpec((BLOCK, 2 * KV_A), prev),
                  pl.BlockSpec((SWA_BLOCKS * BLOCK, 2 * KV_A), cur),
                  pl.BlockSpec((BLOCK, 2 * KV_A), prev),
                  pl.BlockSpec((None,) + bias.shape[1:], lambda b, i, s: (jnp.minimum(i, 1), 0, 0, 0)),
                  pl.BlockSpec((None,) + bias.shape[1:], lambda b, i, s: (1, 0, 0, 0))],
        out_specs=pl.BlockSpec((SWA_BLOCKS * BLOCK, Q_A), cur))
    return pl.pallas_call(
        _swa_kernel,
        out_shape=jax.ShapeDtypeStruct((N_TOK, Q_A), BF16),
        grid_spec=grid_spec,
        compiler_params=_cparams(2),
        name="swa",
    )(sinks, qa, kdup, kdup, va, va, bias, bias)


def _fox_kernel(js_ref, q_ref, k_ref, v_ref, qa_ref, ka_ref, o_ref,
                kaug, vaug, q2, m_sc, acc_sc, s_0, s_1, s_2, s_3):
    tq, tk = TQ_FOX, TK_FOX
    b = pl.program_id(0)
    t = pl.program_id(1)
    g = pl.program_id(2)
    first_tile = (b * pl.num_programs(1) + t) * (SEQ // tq) + g * FOX_ROWS

    @pl.when(g == 0)
    def _():
        kaug[:, 0:LANES] = k_ref[...]
        kaug[:, LANES:2 * LANES] = ka_ref[...]
        vaug[:, 0:LANES] = v_ref[...]
        vaug[:, LANES:2 * LANES] = jnp.ones((SEQ, LANES), BF16)

    lane = lax.broadcasted_iota(jnp.int32, (tq, LANES), 1)
    lo = lane < HEAD_DIM
    base = 2 * DECAY_LANES * t
    own = [(lane >= base + h * DECAY_LANES) & (lane < base + (h + 1) * DECAY_LANES) for h in range(2)]
    for rw in range(FOX_ROWS):
        rows = slice(rw * tq, (rw + 1) * tq)
        q = q_ref[rows, :]
        qa = qa_ref[rows, :]
        zero = jnp.zeros_like(q)
        q2[rw, 0, :, 0:LANES] = jnp.where(lo, q, zero)
        q2[rw, 1, :, 0:LANES] = jnp.where(lo, zero, q)
        for h in range(2):
            q2[rw, h, :, LANES:2 * LANES] = jnp.where(own[h], qa, zero)

    rr = lax.broadcasted_iota(jnp.int32, (tq, tk), 0)
    cc = lax.broadcasted_iota(jnp.int32, (tq, tk), 1)
    causal = cc <= rr
    bufs = ((s_0, s_1), (s_2, s_3))
    j_starts = [js_ref[first_tile + rw] for rw in range(FOX_ROWS)]
    for rw in range(FOX_ROWS):
        nxt = rw + 1 if rw + 1 < FOX_ROWS else None
        _fox_query_tile(rw, g * FOX_ROWS + rw, j_starts[rw], bufs[rw % 2], q2, kaug, vaug, m_sc, acc_sc,
                        causal, lo, o_ref,
                        first_scores_done=rw > 0,
                        next_first=None if nxt is None else (nxt, j_starts[nxt], bufs[nxt % 2][0]))


def _fox_query_tile(rw, i, j_start, buf_pair, q2, kaug, vaug, m_sc, acc_sc, causal, lo, o_ref,
                    first_scores_done, next_first):
    tq, tk = TQ_FOX, TK_FOX
    s_a, s_b = buf_pair
    m_sc[...] = jnp.full(m_sc.shape, NEG_INF, F32)
    acc_sc[...] = jnp.zeros(acc_sc.shape, F32)

    def scores_of(row, h, ks):
        return lax.dot_general(q2[row, h], kaug[pl.ds(ks, tk), :], (((1,), (1,)), ((), ())),
                               preferred_element_type=F32)

    def scores(h, ks):
        return scores_of(rw, h, ks)

    def consume(h, s, ks, mask):
        if mask is not None:
            s = jnp.where(mask, s, NEG_INF)
        m_prev = m_sc[h]
        m_new = jnp.maximum(m_prev, jnp.max(s, axis=-1, keepdims=True))
        alpha = jnp.exp2(m_prev - m_new)
        p = jnp.exp2(s - jnp.concatenate([m_new] * (tk // LANES), axis=1))
        pv = jnp.dot(p.astype(BF16), vaug[pl.ds(ks, tk), :], preferred_element_type=F32)
        acc_sc[h] = jnp.concatenate([alpha, alpha], axis=1) * acc_sc[h] + pv
        m_sc[h] = m_new

    def key_start(j):
        return pl.multiple_of(j * tk, tk)

    def scores_into(buf, j):
        for h in range(2):
            buf[h] = scores(h, key_start(j))

    def consume_from(buf, j, mask):
        for h in range(2):
            consume(h, buf[h], key_start(j), mask)

    n_full = i - j_start

    if not first_scores_done:
        scores_into(s_a, j_start)

    def pair(p, carry):
        j = j_start + 2 * p
        scores_into(s_b, j + 1)
        consume_from(s_a, j, None)
        scores_into(s_a, j + 2)
        consume_from(s_b, j + 1, None)
        return carry

    lax.fori_loop(0, n_full // 2, pair, 0)
    odd = lax.rem(n_full, 2) == 1

    def start_next():
        if next_first is not None:
            nrow, nj, nbuf = next_first
            for h in range(2):
                nbuf[h] = scores_of(nrow, h, key_start(nj))

    @pl.when(odd)
    def _():
        scores_into(s_b, i)
        consume_from(s_a, i - 1, None)
        start_next()
        consume_from(s_b, i, causal)

    @pl.when(jnp.logical_not(odd))
    def _():
        start_next()
        consume_from(s_a, i, causal)

    outs = [acc_sc[h, :, 0:LANES] / acc_sc[h, :, LANES:2 * LANES] for h in range(2)]
    o_ref[rw * tq:(rw + 1) * tq, :] = jnp.where(lo, outs[0], outs[1]).astype(BF16)


def _fox(j_start, qb, kb, vb, qa, ka):
    tq = TQ_FOX
    ns = SEQ // (tq * FOX_ROWS)
    n_pairs = N_HEADS_FOX // 2
    qmap = lambda b, t, i, js: (b * ns + i, t)
    kmap = lambda b, t, i, js: (b, t)
    grid_spec = pltpu.PrefetchScalarGridSpec(
        num_scalar_prefetch=1,
        grid=(BATCH, n_pairs, ns),
        in_specs=[pl.BlockSpec((FOX_ROWS * tq, LANES), qmap),
                  pl.BlockSpec((SEQ, LANES), kmap),
                  pl.BlockSpec((SEQ, LANES), kmap),
                  pl.BlockSpec((None, FOX_ROWS * tq, LANES), lambda b, t, i, js: (b, i, 0)),
                  pl.BlockSpec((None, SEQ, LANES), lambda b, t, i, js: (b, 0, 0))],
        out_specs=pl.BlockSpec((FOX_ROWS * tq, LANES), qmap),
        scratch_shapes=[pltpu.VMEM((SEQ, 2 * LANES), BF16),
                        pltpu.VMEM((SEQ, 2 * LANES), BF16),
                        pltpu.VMEM((FOX_ROWS, 2, tq, 2 * LANES), BF16),
                        pltpu.VMEM((2, tq, LANES), F32),
                        pltpu.VMEM((2, tq, 2 * LANES), F32)]
        + [pltpu.VMEM((2, tq, TK_FOX), F32)] * 4)
    return pl.pallas_call(
        _fox_kernel,
        out_shape=jax.ShapeDtypeStruct((N_TOK, W_B), BF16),
        grid_spec=grid_spec,
        compiler_params=_cparams(3),
        name="fox",
    )(j_start, qb, kb, vb, qa, ka)


def _fox_first_tiles(nrm, fb):
    n_tiles = SEQ // TQ_FOX
    nr = nrm.reshape(BATCH, n_tiles, 8, LANES)[:, :, 0, :] * 1.02
    qn = jnp.sqrt(nr[..., 0:N_HEADS_FOX])
    kn = jnp.sqrt(nr[..., N_HEADS_FOX:2 * N_HEADS_FOX])
    f_first = fb[:, 0::2, 0:DECAY_LANES * N_HEADS_FOX:DECAY_LANES]
    f_last = fb[:, 1::2, 0:DECAY_LANES * N_HEADS_FOX:DECAY_LANES]
    kn_prefix = lax.cummax(kn, axis=1)
    upper = qn[:, :, None, :] * kn_prefix[:, None, :, :] + f_first[:, :, None, :] - f_last[:, None, :, :]
    row_max_low = -(qn * kn)[:, :, None, :]
    ii = jnp.arange(n_tiles)[None, :, None, None]
    jj = jnp.arange(n_tiles)[None, None, :, None]
    skip = (upper < row_max_low - PRUNE_MARGIN) & (jj < ii)
    skip = jnp.all(skip.reshape(BATCH, n_tiles, n_tiles, N_HEADS_FOX // 2, 2), axis=-1)
    first = jnp.min(jnp.where(skip, n_tiles, jj), axis=2)
    return jnp.transpose(first, (0, 2, 1)).reshape(-1).astype(jnp.int32)


def _post_kernel(x_ref, oa_ref, ob_ref, gt_ref, gm_ref, sc_ref, sh_ref, g_ref,
                 wa_ref, wb_ref, wo_ref, wr2_ref, br_ref,
                 x1_ref, xy_ref, rc_ref, cu_ref, hh_prev, lg_prev):
    step = pl.program_id(0)

    @pl.when(step == 0)
    def _():
        hh_prev[...] = jnp.zeros(hh_prev.shape, BF16)
        lg_prev[...] = jnp.zeros(lg_prev.shape, F32)

    @pl.when(step <= N_TOK_TILES)
    def _():
        hh_p = hh_prev[...]
        lg_p = lg_prev[...]
        hh, logits = _post_mix(x_ref, oa_ref, ob_ref, gt_ref, gm_ref, sc_ref, sh_ref, g_ref,
                               wa_ref, wb_ref, wo_ref, wr2_ref, br_ref, x1_ref)
        _post_route(hh_p, lg_p, xy_ref, rc_ref, cu_ref)
        hh_prev[...] = hh
        lg_prev[...] = logits

    @pl.when(step > N_TOK_TILES)
    def _():
        xy_ref[...] = jnp.zeros(xy_ref.shape, BF16)


def _post_mix(x_ref, oa_ref, ob_ref, gt_ref, gm_ref, sc_ref, sh_ref, g_ref,
              wa_ref, wb_ref, wo_ref, wr2_ref, br_ref, x1_ref):
    pa = jnp.dot(oa_ref[...], wa_ref[...], preferred_element_type=F32)
    pb = jnp.dot(ob_ref[...], wb_ref[...], preferred_element_type=F32)
    ga = jax.nn.sigmoid(gt_ref[:, 0:D_MODEL].astype(F32))
    gb = jax.nn.sigmoid(gt_ref[:, D_MODEL:2 * D_MODEL].astype(F32))
    merged = (ga * pa + gb * pb).astype(BF16)
    y = jnp.dot(merged, wo_ref[...], preferred_element_type=F32)
    x1 = x_ref[...] + gm_ref[...] * y
    x1_ref[...] = x1

    rs = lax.rsqrt(jnp.mean(x1 * x1, axis=-1, keepdims=True) + EPS)
    a = g_ref[...] * (1.0 + sc_ref[...])
    h2 = x1 * rs * a + sh_ref[...]

    hh = h2.astype(BF16)
    hl = (h2 - hh.astype(F32)).astype(BF16)
    hi_both = jnp.dot(hh, wr2_ref[...], preferred_element_type=F32)
    logits = (hi_both[:, 0:LANES] + hi_both[:, LANES:2 * LANES]
              + jnp.dot(hl, wr2_ref[:, 0:LANES], preferred_element_type=F32)
              + br_ref[...])
    return hh, logits


def _post_route(hh, logits, xy_ref, rc_ref, cu_ref):
    tm = TM_POST
    lane = lax.broadcasted_iota(jnp.int32, (tm, LANES), 1).astype(F32)
    big = float(LANES)
    gl = jnp.where(lane < N_GROUPS, logits, -jnp.inf)
    gmax = jnp.max(gl, axis=-1, keepdims=True)
    gi = jnp.min(jnp.where(gl == gmax, lane, big), axis=-1, keepdims=True)
    gsum = jnp.sum(jnp.exp(gl - gmax), axis=-1, keepdims=True)
    gp = 1.0 / gsum
    e_lo = N_GROUPS + EXPERTS_PER_GROUP * gi
    el = jnp.where((lane >= e_lo) & (lane < e_lo + EXPERTS_PER_GROUP), logits, -jnp.inf)
    v1 = jnp.max(el, axis=-1, keepdims=True)
    i1 = jnp.min(jnp.where(el == v1, lane, big), axis=-1, keepdims=True)
    el2 = jnp.where(lane == i1, -jnp.inf, el)
    v2 = jnp.max(el2, axis=-1, keepdims=True)
    i2 = jnp.min(jnp.where(el2 == v2, lane, big), axis=-1, keepdims=True)
    e21 = jnp.exp(v2 - v1)
    w1 = gp / (1.0 + e21)
    w2 = gp * e21 / (1.0 + e21)
    e1 = i1 - N_GROUPS
    e2 = i2 - N_GROUPS

    oh = jnp.where((lane == e1) | (lane == e2), 1.0, 0.0)
    cnt_u = jnp.floor((jnp.sum(oh, axis=0, keepdims=True) + (UNIT - 1)) * (1.0 / UNIT))
    r128 = lax.broadcasted_iota(jnp.int32, (LANES, LANES), 0)
    c128 = lax.broadcasted_iota(jnp.int32, (LANES, LANES), 1)
    before_lane = jnp.where(r128 < c128, 1.0, 0.0).astype(BF16)
    loc_u = jnp.dot(jnp.broadcast_to(cnt_u, (8, LANES)).astype(BF16), before_lane,
                    preferred_element_type=F32)
    trow = lax.broadcasted_iota(jnp.int32, (tm, LANES), 0)
    seen = oh
    k = 1
    while k < tm:
        if k < 8:
            shifted = jnp.where(trow >= k, pltpu.roll(seen, k, 0), 0.0)
        else:
            shifted = jnp.concatenate([jnp.zeros((k, LANES), F32), seen[:tm - k]], axis=0)
        seen = seen + shifted
        k *= 2
    pos_e = (seen - oh) + loc_u[0:1] * UNIT
    lp1 = jnp.sum(jnp.where(lane == e1, pos_e, 0.0), axis=-1, keepdims=True)
    lp2 = jnp.sum(jnp.where(lane == e2, pos_e, 0.0), axis=-1, keepdims=True)

    def to_row(col):
        return jnp.transpose(jnp.broadcast_to(col, (tm, LANES)))[0:1]

    srow = lax.broadcasted_iota(jnp.int32, (XY_ROWS, tm), 0).astype(F32)
    pm1 = jnp.where(srow == to_row(lp1), 1.0, 0.0).astype(BF16)
    pm2 = jnp.where(srow == to_row(lp2), 1.0, 0.0).astype(BF16)
    w1h = w1.astype(BF16).astype(F32)
    w2h = w2.astype(BF16).astype(F32)
    side = jnp.where(lane == W1_LANES[0], w1h, jnp.where(lane == W1_LANES[1], w1 - w1h,
           jnp.where(lane == W2_LANES[0], w2h, jnp.where(lane == W2_LANES[1], w2 - w2h,
           jnp.where(lane == E1_LANE, e1, jnp.where(lane == E2_LANE, e2, 0.0))))))
    tok = jnp.concatenate([hh, side.astype(BF16)], axis=1)
    xy_ref[...] = jnp.dot(pm1 + pm2, tok, preferred_element_type=F32).astype(BF16)

    cu_ref[...] = jnp.broadcast_to(cnt_u, cu_ref.shape)
    rc_ref[...] = jnp.where(lane == 0, lp1, jnp.where(lane == 1, lp2, 0.0))


def _post(x2, oa, ob, gates, gate_m, scale_f, shift_f, g_ffn, wa, wb, wo, wr2, b_r):
    tm = TM_POST
    tpb = SEQ // tm
    n_steps = N_TOK_TILES
    row = lambda i: (jnp.minimum(i, n_steps - 1), 0)
    per_b = lambda i: (jnp.minimum(i, n_steps - 1) // tpb, 0, 0)
    routed = lambda i: (jnp.clip(i - 1, 0, n_steps - 1), 0)
    const = lambda i: (0, 0)
    return pl.pallas_call(
        _post_kernel,
        out_shape=[jax.ShapeDtypeStruct((N_TOK, D_MODEL), F32),
                   jax.ShapeDtypeStruct(((n_steps + PAD_BLOCKS) * XY_ROWS, XY_COLS), BF16),
                   jax.ShapeDtypeStruct((N_TOK, LANES), F32),
                   jax.ShapeDtypeStruct((n_steps * 8, LANES), F32)],
        grid=(n_steps + 1 + PAD_BLOCKS,),
        in_specs=[pl.BlockSpec((tm, D_MODEL), row),
                  pl.BlockSpec((tm, Q_A), row),
                  pl.BlockSpec((tm, W_B), row),
                  pl.BlockSpec((tm, 2 * D_MODEL), row),
                  pl.BlockSpec((None, 1, D_MODEL), per_b),
                  pl.BlockSpec((None, 1, D_MODEL), per_b),
                  pl.BlockSpec((None, 1, D_MODEL), per_b),
                  pl.BlockSpec((1, D_MODEL), const),
                  pl.BlockSpec(wa.shape, const),
                  pl.BlockSpec(wb.shape, const),
                  pl.BlockSpec(wo.shape, const),
                  pl.BlockSpec(wr2.shape, const),
                  pl.BlockSpec((1, LANES), const)],
        out_specs=[pl.BlockSpec((tm, D_MODEL), row),
                   pl.BlockSpec((XY_ROWS, XY_COLS), lambda i: (jnp.maximum(i - 1, 0), 0)),
                   pl.BlockSpec((tm, LANES), routed),
                   pl.BlockSpec((8, LANES), routed)],
        scratch_shapes=[pltpu.VMEM((tm, D_MODEL), BF16),
                        pltpu.VMEM((tm, LANES), F32)],
        compiler_params=_cparams(1),
        name="post",
    )(x2, oa, ob, gates, gate_m, scale_f, shift_f, g_ffn, wa, wb, wo, wr2, b_r)


def _experts_kernel(te_ref, nu_ref, ur_ref, ne_ref, ep_ref, xy_in, wg_hbm, wu_hbm, wd_hbm, xy_out,
                    xbuf, ybuf, wg_s, wu_s, wd_s, wg_f, wu_f, wd_f, gsem, ssem, wsem):
    del xy_in
    r = pl.program_id(0)
    last = pl.num_programs(0) - 1
    n_used = nu_ref[0]
    slot = lax.rem(r, 2)

    def unit_row(step, s):
        return pl.multiple_of(ur_ref[step * UNITS_PER_TILE + s], UNIT)

    def start_gathers(step, sl):
        for s in range(UNITS_PER_TILE):
            pltpu.make_async_copy(xy_out.at[pl.ds(unit_row(step, s), UNIT), :],
                                  xbuf.at[sl, pl.ds(s * UNIT, UNIT), :], gsem.at[sl]).start()

    def wait_gathers(sl):
        pltpu.make_async_copy(xy_out.at[pl.ds(0, TM_EXP), :], xbuf.at[sl], gsem.at[sl]).wait()

    def start_scatters(step, sl):
        for s in range(UNITS_PER_TILE):
            pltpu.make_async_copy(ybuf.at[sl, pl.ds(s * UNIT, UNIT), :],
                                  xy_out.at[pl.ds(unit_row(step, s), UNIT), pl.ds(0, D_MODEL)],
                                  ssem.at[sl]).start()

    def wait_scatters(sl):
        pltpu.make_async_copy(ybuf.at[sl], xy_out.at[pl.ds(0, TM_EXP), pl.ds(0, D_MODEL)], ssem.at[sl]).wait()

    @pl.when(r == 0)
    def _():
        start_gathers(0, 0)

    @pl.when(jnp.logical_and(r < n_used, r >= 2))
    def _():
        wait_scatters(slot)

    def weight_copies(e, p):
        return [pltpu.make_async_copy(src.at[e], dst.at[p], wsem.at[p])
                for src, dst in ((wg_hbm, wg_f), (wu_hbm, wu_f), (wd_hbm, wd_f))]

    @pl.when(jnp.logical_and(r < n_used,
                             jnp.logical_or(r == 0, te_ref[r] != te_ref[jnp.maximum(r - 1, 0)])))
    def _():
        e = te_ref[r]
        p = ep_ref[r]

        @pl.when(r == 0)
        def _():
            for cp in weight_copies(e, p):
                cp.start()

        for cp in weight_copies(e, p):
            cp.wait()
        wg_s[...] = wg_f[p].astype(BF16)
        wu_s[...] = wu_f[p].astype(BF16)
        wd_s[...] = wd_f[p].astype(BF16)

        @pl.when(ne_ref[r] >= 0)
        def _():
            for cp in weight_copies(ne_ref[r], 1 - p):
                cp.start()

    @pl.when(r < n_used)
    def _():
        wait_gathers(slot)
        start_gathers(jnp.minimum(r + 1, last), 1 - slot)
        x = xbuf[slot, :, 0:D_MODEL]
        side = xbuf[slot, :, D_MODEL:XY_COLS].astype(F32)
        lane = lax.broadcasted_iota(jnp.int32, side.shape, 1)

        def lanes_sum(a, b):
            return jnp.sum(jnp.where((lane == a) | (lane == b), side, 0.0), axis=-1, keepdims=True)

        is_slot1 = lanes_sum(E1_LANE, E1_LANE) == te_ref[r].astype(F32)
        wrow = jnp.where(is_slot1, lanes_sum(*W1_LANES), lanes_sum(*W2_LANES))
        a = jnp.dot(x, wg_s[...], preferred_element_type=F32)
        u = jnp.dot(x, wu_s[...], preferred_element_type=F32)
        hid = (a * jax.nn.sigmoid(a) * u * wrow).astype(BF16)
        ybuf[slot] = jnp.dot(hid, wd_s[...], preferred_element_type=F32).astype(BF16)
        start_scatters(r, slot)

    @pl.when(r == n_used - 1)
    def _():
        wait_gathers(1 - slot)
        wait_scatters(slot)

        @pl.when(r >= 1)
        def _():
            wait_scatters(1 - slot)


def _experts(tile_expert, n_used, unit_rows, next_expert, expert_parity, xy, wg, wu, wd):
    grid_spec = pltpu.PrefetchScalarGridSpec(
        num_scalar_prefetch=5,
        grid=(N_EXP_TILES,),
        in_specs=[pl.BlockSpec(memory_space=pl.ANY),
                  pl.BlockSpec(memory_space=pl.ANY),
                  pl.BlockSpec(memory_space=pl.ANY),
                  pl.BlockSpec(memory_space=pl.ANY)],
        out_specs=pl.BlockSpec(memory_space=pl.ANY),
        scratch_shapes=[pltpu.VMEM((2, TM_EXP, XY_COLS), BF16),
                        pltpu.VMEM((2, TM_EXP, D_MODEL), BF16),
                        pltpu.VMEM((D_MODEL, D_FF_EXPERT), BF16),
                        pltpu.VMEM((D_MODEL, D_FF_EXPERT), BF16),
                        pltpu.VMEM((D_FF_EXPERT, D_MODEL), BF16),
                        pltpu.VMEM((2, D_MODEL, D_FF_EXPERT), F32),
                        pltpu.VMEM((2, D_MODEL, D_FF_EXPERT), F32),
                        pltpu.VMEM((2, D_FF_EXPERT, D_MODEL), F32),
                        pltpu.SemaphoreType.DMA((2,)),
                        pltpu.SemaphoreType.DMA((2,)),
                        pltpu.SemaphoreType.DMA((2,))])
    return pl.pallas_call(
        _experts_kernel,
        out_shape=jax.ShapeDtypeStruct(xy.shape, xy.dtype),
        grid_spec=grid_spec,
        input_output_aliases={5: 0},
        compiler_params=_cparams(1),
        name="experts",
    )(tile_expert, n_used, unit_rows, next_expert, expert_parity, xy, wg, wu, wd)


def _combine_kernel(x1_ref, rc_ref, gf_ref, gfin_ref, y_ref, o_ref):
    for tl in range(COMBINE_TILES):
        rows = slice(tl * TM_ROW, (tl + 1) * TM_ROW)
        lp1 = rc_ref[rows, 0:1]
        lp2 = rc_ref[rows, 1:2]
        scol = lax.broadcasted_iota(jnp.int32, (TM_ROW, XY_ROWS), 1).astype(F32)
        pick = jnp.where((scol == lp1) | (scol == lp2), 1.0, 0.0).astype(BF16)
        y = jnp.dot(pick, y_ref[tl * XY_ROWS:(tl + 1) * XY_ROWS, :], preferred_element_type=F32)
        xf = x1_ref[rows, :] + gf_ref[...] * y
        rs = lax.rsqrt(jnp.mean(xf * xf, axis=-1, keepdims=True) + EPS)
        o_ref[rows, :] = xf * rs * gfin_ref[...]


def _combine(x1, rcol, gate_f, g_final, xy):
    tm = TM_ROW * COMBINE_TILES
    tpb = SEQ // tm
    row = lambda i: (i, 0)
    return pl.pallas_call(
        _combine_kernel,
        out_shape=jax.ShapeDtypeStruct((N_TOK, D_MODEL), F32),
        grid=(N_TOK // tm,),
        in_specs=[pl.BlockSpec((tm, D_MODEL), row),
                  pl.BlockSpec((tm, LANES), row),
                  pl.BlockSpec((None, 1, D_MODEL), lambda i: (i // tpb, 0, 0)),
                  pl.BlockSpec((1, D_MODEL), lambda i: (0, 0)),
                  pl.BlockSpec((COMBINE_TILES * XY_ROWS, D_MODEL), row)],
        out_specs=pl.BlockSpec((tm, D_MODEL), row),
        compiler_params=_cparams(1),
        name="combine",
    )(x1, rcol, gate_f, g_final, xy)


def _t5_bucket_np():
    qi = np.arange(BLOCK)[:, None]
    kj = np.arange(2 * BLOCK)[None, :]
    dist = qi - kj + BLOCK
    n = np.maximum(dist, 0)
    max_exact = NUM_BUCKETS // 2
    nf = np.maximum(n, 1).astype(np.float32)
    large = max_exact + (np.log(nf / np.float32(max_exact)) / np.float32(math.log(MAX_DISTANCE / max_exact))
                         * np.float32(NUM_BUCKETS - max_exact)).astype(np.int32)
    large = np.minimum(large, NUM_BUCKETS - 1)
    bucket = np.where(n < max_exact, n, large)
    band = (dist >= 0) & (dist < WINDOW)
    return bucket.astype(np.int32), band


def kernel(x, c, w_ada, b_ada, g_norm_mix, g_norm_ffn, w_in, sinks, b_forget, w_proj_swa, w_proj_fox,
           w_out, rel_bias_table, w_router_group, b_router_group, w_router_expert, b_router_expert,
           w_gate_exp, w_up_exp, w_down_exp, g_final):
    l = 0
    x2 = x.reshape(N_TOK, D_MODEL)

    c16 = jnp.concatenate([c, jnp.zeros_like(c)], axis=0)
    mod = _ada(c16, w_ada[l], b_ada[l][None, :])[:BATCH]
    shift_m, scale_m, gate_m, shift_f, scale_f, gate_f = [
        m.reshape(BATCH, 1, D_MODEL) for m in jnp.split(mod, 6, axis=-1)]

    w = w_in[l]
    o_ka, o_va, o_qb = Q_A, Q_A + KV_A, Q_A + 2 * KV_A
    o_kb, o_vb, o_f = o_qb + W_B, o_qb + 2 * W_B, o_qb + 3 * W_B
    o_g = o_f + N_HEADS_FOX

    def dup(cols):
        heads = [cols[:, h * HEAD_DIM:(h + 1) * HEAD_DIM] for h in range(N_KV_HEADS_SWA)]
        return jnp.concatenate([hd for hd in heads for _ in range(2)], axis=1)

    head_order = jnp.argsort(b_forget[l])

    def reorder_heads(cols):
        return jnp.take(cols.reshape(D_MODEL, N_HEADS_FOX, HEAD_DIM), head_order, axis=1).reshape(D_MODEL, W_B)

    w_fox = jnp.concatenate([reorder_heads(w[:, o_qb:o_kb]), reorder_heads(w[:, o_kb:o_vb]),
                             reorder_heads(w[:, o_vb:o_f])], axis=1)
    b_fox = jnp.take(b_forget[l], head_order)
    w_proj_b = jnp.take(w_proj_fox[l].reshape(N_HEADS_FOX, HEAD_DIM, D_MODEL), head_order, axis=0).reshape(W_B, D_MODEL)
    carrier = DECAY_LANES * N_HEADS_FOX
    w_f = jnp.pad(jnp.repeat(jnp.take(w[:, o_f:o_g], head_order, axis=1), DECAY_LANES, axis=1),
                  ((0, 0), (0, LANES - carrier)))
    w_main = jnp.concatenate([w[:, :Q_A], dup(w[:, o_ka:o_va]), w[:, o_va:o_qb], w_f, w_fox], axis=1).astype(BF16)
    w_g = w[:, o_g:].astype(BF16)
    qa, kdup, va, qb, kb, vb, f_pad, gates, nrm = _inproj(
        x2, scale_m, shift_m, g_norm_mix[l][None, :], w_main, w_g)

    b_pad = jnp.pad(jnp.repeat(b_fox, DECAY_LANES), (0, LANES - carrier))[None, :]
    lanes = np.arange(LANES)
    jmod = jnp.asarray(np.where(lanes < DECAY_LANES * N_HEADS_FOX, lanes % DECAY_LANES, 7)[None, :].astype(np.int32))
    dq, dk, fb = _cum(f_pad, b_pad, jmod)

    bucket, band = _t5_bucket_np()
    onehot = jnp.asarray(bucket[None] == np.arange(NUM_BUCKETS)[:, None, None], dtype=F32)
    bias = jnp.einsum("bh,bqk->hqk", rel_bias_table.astype(F32), onehot, precision=HIGHEST)
    bias = jnp.where(band[None], bias * LOG2E, NEG_INF)
    first = np.arange(2 * BLOCK)[None, None, :] < BLOCK
    bias = jnp.stack([jnp.where(first, NEG_INF, bias), bias]).reshape(2, N_KV_HEADS_SWA, -1, 2 * BLOCK)
    o_a = _swa(sinks[l].astype(F32) * LOG2E, qa, kdup, va, bias)

    o_b = _fox(_fox_first_tiles(nrm, fb), qb, kb, vb, dq, dk)

    w_r = jnp.concatenate([w_router_group[l]] + [w_router_expert[l][g] for g in range(N_GROUPS)], axis=1)
    w_r = jnp.pad(w_r, ((0, 0), (0, LANES - w_r.shape[1])))
    wr_hi = w_r.astype(BF16)
    wr_lo = (w_r - wr_hi.astype(F32)).astype(BF16)
    wr2 = jnp.concatenate([wr_hi, wr_lo], axis=1)
    b_r = jnp.concatenate([b_router_group[l], b_router_expert[l].reshape(-1)])
    b_r = jnp.pad(b_r, (0, LANES - b_r.shape[0]))[None, :]
    x1, xy, rcol, cu = _post(x2, o_a, o_b, gates, gate_m, scale_f, shift_f, g_norm_ffn[l][None, :],
                             w_proj_swa[l].astype(BF16), w_proj_b.astype(BF16), w_out[l].astype(BF16),
                             wr2, b_r)

    i32 = jnp.int32
    n_tok_tiles = N_TOK // TM_POST
    cu = cu.reshape(n_tok_tiles, 8, LANES)[:, 0, :N_EXPERTS].astype(i32)
    loc_u = jnp.cumsum(cu, axis=1) - cu
    cend = jnp.cumsum(cu, axis=0)
    cstart = cend - cu
    tot_u = cend[-1]
    tiles_e = (tot_u + UNITS_PER_TILE - 1) // UNITS_PER_TILE
    tile_end = jnp.cumsum(tiles_e)
    tile_start = tile_end - tiles_e
    r = jnp.arange(N_EXP_TILES, dtype=i32)
    tile_expert = jnp.minimum(jnp.sum((tile_end[None, :] <= r[:, None]).astype(i32), axis=1), N_EXPERTS - 1)
    sel_e = tile_expert[:, None] == jnp.arange(N_EXPERTS, dtype=i32)[None, :]
    tw = r - jnp.sum(jnp.where(sel_e, tile_start[None, :], 0), axis=1)
    tot_r = jnp.sum(jnp.where(sel_e, tot_u[None, :], 0), axis=1)
    n_used = tile_end[-1:].astype(i32)
    q = tw[:, None] * UNITS_PER_TILE + jnp.arange(UNITS_PER_TILE, dtype=i32)[None, :]

    def of_expert(tab):
        return jnp.sum(jnp.where(sel_e[:, None, :], tab[None, :, :], 0), axis=2)

    cend_r, cstart_r, loc_r = of_expert(cend), of_expert(cstart), of_expert(loc_u)
    src_tile = jnp.minimum(jnp.sum((cend_r[:, None, :] <= q[:, :, None]).astype(i32), axis=2), n_tok_tiles - 1)
    sel_t = src_tile[:, :, None] == jnp.arange(n_tok_tiles, dtype=i32)[None, None, :]
    k = (q - jnp.sum(jnp.where(sel_t, cstart_r[:, None, :], 0), axis=2)
         + jnp.sum(jnp.where(sel_t, loc_r[:, None, :], 0), axis=2))
    real_rows = src_tile * XY_ROWS + k * UNIT
    pad_rows = PAD_BASE_ROW + (tile_expert[:, None] * PAD_UNITS_PER_EXPERT + (q - tot_r[:, None])) * UNIT
    idle_row = PAD_BASE_ROW + N_EXPERTS * PAD_UNITS_PER_EXPERT * UNIT
    unit_rows = jnp.where(q < tot_r[:, None], real_rows, pad_rows)
    unit_rows = jnp.where((r < n_used)[:, None], unit_rows, idle_row).reshape(-1).astype(i32)

    eid = jnp.arange(N_EXPERTS, dtype=i32)
    used = tiles_e > 0
    later_used = (eid[None, :] > eid[:, None]) & used[None, :]
    next_e = jnp.min(jnp.where(later_used, eid[None, :], N_EXPERTS), axis=1)
    next_e = jnp.where(next_e == N_EXPERTS, -1, next_e)
    parity_e = (jnp.cumsum(used.astype(i32)) - used.astype(i32)) % 2
    next_expert = jnp.sum(jnp.where(sel_e, next_e[None, :], 0), axis=1).astype(i32)
    expert_parity = jnp.sum(jnp.where(sel_e, parity_e[None, :], 0), axis=1).astype(i32)

    xy = _experts(tile_expert.astype(i32), n_used, unit_rows, next_expert, expert_parity, xy,
                  w_gate_exp[l].reshape(N_EXPERTS, D_MODEL, D_FF_EXPERT),
                  w_up_exp[l].reshape(N_EXPERTS, D_MODEL, D_FF_EXPERT),
                  w_down_exp[l].reshape(N_EXPERTS, D_FF_EXPERT, D_MODEL))
    out = _combine(x1, rcol, gate_f, g_final[None, :], xy)
    return out.reshape(BATCH, SEQ, D_MODEL)
```

```python
import math

import numpy as np
import jax
import jax.numpy as jnp
from jax import lax
from jax.experimental import pallas as pl
from jax.experimental.pallas import tpu as pltpu

F32 = jnp.float32
BF16 = jnp.bfloat16
HIGHEST = lax.Precision.HIGHEST

D_MODEL = 1024
BATCH = 8
SEQ = 4096
N_TOK = BATCH * SEQ
N_HEADS_SWA = 8
N_KV_HEADS_SWA = 2
N_HEADS_FOX = 8
HEAD_DIM = 64
WINDOW = 128
BLOCK = 128
NUM_BUCKETS = 32
MAX_DISTANCE = 128
N_GROUPS = 4
EXPERTS_PER_GROUP = 8
N_EXPERTS = N_GROUPS * EXPERTS_PER_GROUP
D_FF_EXPERT = 256
EPS = 1e-6
NEG_INF = -1e30

Q_A = N_HEADS_SWA * HEAD_DIM
KV_A = N_KV_HEADS_SWA * HEAD_DIM
W_B = N_HEADS_FOX * HEAD_DIM
LANES = 128
QK_SCALE = HEAD_DIM ** -0.5

TM_IN = 1024
TM_POST = 512
TQ_FOX = 512
TK_FOX = TQ_FOX
FOX_ROWS = 2
SWA_BLOCKS = 8
TM_EXP = 512
TM_ROW = 512
COMBINE_TILES = 2
UNIT = 16
XY_UNITS = 2 * TM_POST // UNIT + N_EXPERTS
XY_ROWS = XY_UNITS * UNIT
XY_COLS = D_MODEL + LANES
UNITS_PER_TILE = TM_EXP // UNIT
N_TOK_TILES = N_TOK // TM_POST
N_EXP_TILES = N_TOK_TILES * XY_UNITS // UNITS_PER_TILE + N_EXPERTS
PAD_UNITS_PER_EXPERT = UNITS_PER_TILE - 1
PAD_POOLS = 3
PAD_BLOCKS = -(-((PAD_POOLS * PAD_UNITS_PER_EXPERT + 1) * UNIT) // XY_ROWS)
PAD_BASE_ROW = N_TOK_TILES * XY_ROWS
W1_LANES, W2_LANES, E1_LANE, E2_LANE = (4, 6), (5, 7), 8, 9
VMEM_LIMIT = 56 * 1024 * 1024

DECAY_LANES = 6
LOG2E = math.log2(math.e)
Q_SCALE_LOG2 = QK_SCALE * LOG2E
PRUNE_MARGIN = 160.0


def _cparams(n_axes):
    return pltpu.CompilerParams(dimension_semantics=("arbitrary",) * n_axes,
                                vmem_limit_bytes=VMEM_LIMIT)


def _ada_kernel(c_ref, w_ref, b_ref, o_ref):
    c = c_ref[...]
    ca = c * jax.nn.sigmoid(c)
    o_ref[...] = jnp.dot(ca.astype(BF16), w_ref[...].astype(BF16),
                         preferred_element_type=F32) + b_ref[...]


def _ada(c16, w_ada, b_ada):
    n_out = w_ada.shape[1]
    blk = n_out // 2
    return pl.pallas_call(
        _ada_kernel,
        out_shape=jax.ShapeDtypeStruct((16, n_out), F32),
        grid=(n_out // blk,),
        in_specs=[pl.BlockSpec((16, D_MODEL), lambda j: (0, 0)),
                  pl.BlockSpec((D_MODEL, blk), lambda j: (0, j)),
                  pl.BlockSpec((1, blk), lambda j: (0, j))],
        out_specs=pl.BlockSpec((16, blk), lambda j: (0, j)),
        compiler_params=_cparams(1),
        name="ada",
    )(c16, w_ada, b_ada)


def _inproj_kernel(x_ref, sc_ref, sh_ref, g_ref, wm_ref, wg_ref, ind_ref,
                   qa_ref, kd_ref, va_ref, qb_ref, kb_ref, vb_ref, f_ref, gt_ref, nrm_ref):
    x = x_ref[...]
    rs = lax.rsqrt(jnp.mean(x * x, axis=-1, keepdims=True) + EPS)
    a = g_ref[...] * (1.0 + sc_ref[...])
    h = (x * rs * a + sh_ref[...]).astype(BF16)

    def mm(w):
        return jnp.dot(h, w, preferred_element_type=F32)

    qa_ref[...] = (mm(wm_ref[:, 0:512]) * Q_SCALE_LOG2).astype(BF16)
    kd_ref[...] = mm(wm_ref[:, 512:768]).astype(BF16)
    vf = mm(wm_ref[:, 768:1024])
    f_ref[...] = vf[:, LANES:2 * LANES]
    v = vf[:, 0:LANES]
    vr = pltpu.roll(v, HEAD_DIM, 1)
    lo = lax.broadcasted_iota(jnp.int32, v.shape, 1) < HEAD_DIM
    va_ref[:, 0:LANES] = jnp.where(lo, v, vr).astype(BF16)
    va_ref[:, LANES:2 * LANES] = jnp.where(lo, vr, v).astype(BF16)
    qb = (mm(wm_ref[:, 1024:1536]) * Q_SCALE_LOG2).astype(BF16)
    kb = mm(wm_ref[:, 1536:2048]).astype(BF16)
    qb_ref[...] = qb
    kb_ref[...] = kb
    vb_ref[...] = mm(wm_ref[:, 2048:2560]).astype(BF16)
    sq = jnp.concatenate([qb, kb], axis=1).astype(F32)
    seg = jnp.dot((sq * sq).astype(BF16), ind_ref[...], preferred_element_type=F32)
    for sub in range(TM_IN // TQ_FOX):
        tile_max = jnp.max(seg[sub * TQ_FOX:(sub + 1) * TQ_FOX], axis=0, keepdims=True)
        nrm_ref[8 * sub:8 * sub + 8, :] = jnp.broadcast_to(tile_max, (8, LANES))
    gt_ref[...] = mm(wg_ref[...]).astype(BF16)


def _inproj(x2, scale_m, shift_m, g_mix, w_main, w_g):
    tm = TM_IN
    tpb = SEQ // tm
    row = lambda i: (i, 0)
    per_b = lambda i: (i // tpb, 0, 0)
    const = lambda i: (0, 0)
    outs = [(Q_A, BF16), (2 * KV_A, BF16), (2 * KV_A, BF16), (W_B, BF16), (W_B, BF16), (W_B, BF16),
            (LANES, F32), (2 * D_MODEL, BF16)]
    ind_np = np.zeros((2 * W_B, LANES), np.float32)
    ind_np[np.arange(2 * W_B), np.arange(2 * W_B) // HEAD_DIM] = 1.0
    ind = jnp.asarray(ind_np, dtype=BF16)
    n_steps = N_TOK // tm
    nrm_rows = 8 * (tm // TQ_FOX)
    once = pl.Buffered(1)
    return pl.pallas_call(
        _inproj_kernel,
        out_shape=[jax.ShapeDtypeStruct((N_TOK, w), dt) for w, dt in outs]
        + [jax.ShapeDtypeStruct((n_steps * nrm_rows, LANES), F32)],
        grid=(n_steps,),
        in_specs=[pl.BlockSpec((tm, D_MODEL), row),
                  pl.BlockSpec((None, 1, D_MODEL), per_b),
                  pl.BlockSpec((None, 1, D_MODEL), per_b),
                  pl.BlockSpec((1, D_MODEL), const),
                  pl.BlockSpec(w_main.shape, const, pipeline_mode=once),
                  pl.BlockSpec(w_g.shape, const, pipeline_mode=once),
                  pl.BlockSpec(ind.shape, const, pipeline_mode=once)],
        out_specs=[pl.BlockSpec((tm, w), row) for w, _ in outs] + [pl.BlockSpec((nrm_rows, LANES), row)],
        compiler_params=_cparams(1),
        name="inproj",
    )(x2, scale_m, shift_m, g_mix, w_main, w_g, ind)


def _log_sigmoid(x):
    return jnp.minimum(x, 0.0) - jnp.log1p(jnp.exp(-jnp.abs(x)))


def _cum_kernel(f_ref, b_ref, jm_ref, qa_ref, ka_ref, fb_ref):
    cum = _log_sigmoid(f_ref[...] + b_ref[...]) * LOG2E
    row = lax.broadcasted_iota(jnp.int32, cum.shape, 0)
    k = 1
    while k < SEQ:
        if k < 8:
            shifted = jnp.where(row >= k, pltpu.roll(cum, k, 0), 0.0)
        else:
            shifted = jnp.concatenate([jnp.zeros((k, LANES), F32), cum[:SEQ - k]], axis=0)
        cum = cum + shifted
        k *= 2
    jm = jm_ref[...]
    for blk in range(SEQ // LANES):
        rows = slice(blk * LANES, (blk + 1) * LANES)
        cb = cum[rows]
        carry = cb[LANES - 1:LANES]
        hi = cb.astype(BF16).astype(F32)
        r1 = cb - hi
        mid = r1.astype(BF16).astype(F32)
        lo = (r1 - mid).astype(BF16).astype(F32)
        piece = jnp.where((jm == 0) | (jm == 3), hi, jnp.where((jm == 1) | (jm == 4), mid, lo))
        q_const = jnp.where(jm < DECAY_LANES, 1.0, 0.0)
        k_const = jnp.where(jm < 3, 1.0, 0.0)
        qa_ref[rows, :] = jnp.where(jm < 3, piece, q_const).astype(BF16)
        ka_ref[rows, :] = jnp.where((jm >= 3) & (jm < DECAY_LANES), -piece, k_const).astype(BF16)
        blocks_per_tile = TQ_FOX // LANES
        tile = blk // blocks_per_tile
        if blk % blocks_per_tile == 0:
            fb_ref[2 * tile:2 * tile + 1, :] = cb[0:1]
        if blk % blocks_per_tile == blocks_per_tile - 1:
            fb_ref[2 * tile + 1:2 * tile + 2, :] = carry


def _cum(f_pad, b_pad, jmod):
    n_tiles = SEQ // TQ_FOX
    return pl.pallas_call(
        _cum_kernel,
        out_shape=[jax.ShapeDtypeStruct((BATCH, SEQ, LANES), BF16)] * 2
        + [jax.ShapeDtypeStruct((BATCH, 2 * n_tiles, LANES), F32)],
        grid=(BATCH,),
        in_specs=[pl.BlockSpec((SEQ, LANES), lambda b: (b, 0)),
                  pl.BlockSpec((1, LANES), lambda b: (0, 0)),
                  pl.BlockSpec((1, LANES), lambda b: (0, 0))],
        out_specs=[pl.BlockSpec((None, SEQ, LANES), lambda b: (b, 0, 0))] * 2
        + [pl.BlockSpec((None, 2 * n_tiles, LANES), lambda b: (b, 0, 0))],
        compiler_params=_cparams(1),
        name="cum",
    )(f_pad, b_pad, jmod)


def _swa_block(sink_cols, q, kk, vv, bias_ref, lo):
    tiles = []
    for g in range(N_KV_HEADS_SWA):
        parts = []
        for t in range(2):
            qt = q[:, (2 * g + t) * LANES:(2 * g + t + 1) * LANES]
            zero = jnp.zeros_like(qt)
            parts.append(jnp.where(lo, qt, zero))
            parts.append(jnp.where(lo, zero, qt))
        q4 = jnp.concatenate(parts, axis=0)
        s = lax.dot_general(q4, kk[:, g * LANES:(g + 1) * LANES], (((1,), (1,)), ((), ())),
                            preferred_element_type=F32)
        s = s + bias_ref[g]
        sink = sink_cols[g]
        m = jnp.maximum(jnp.max(s, axis=-1, keepdims=True), sink)
        p = jnp.exp2(s - m)
        den = jnp.sum(p, axis=-1, keepdims=True) + jnp.exp2(sink - m)
        o = jnp.dot(p.astype(BF16), vv[:, g * LANES:(g + 1) * LANES],
                    preferred_element_type=F32) / den
        tiles.append(jnp.where(lo, o[0:BLOCK], o[BLOCK:2 * BLOCK]))
        tiles.append(jnp.where(lo, o[2 * BLOCK:3 * BLOCK], o[3 * BLOCK:4 * BLOCK]))
    return tiles


def _swa_kernel(sink_ref, q_ref, kc_ref, kp_ref, vc_ref, vp_ref, bias_first_ref, bias_ref, o_ref):
    lane = lax.broadcasted_iota(jnp.int32, (BLOCK, LANES), 1)
    lo = lane < HEAD_DIM
    grp = N_HEADS_SWA // N_KV_HEADS_SWA
    row = lax.broadcasted_iota(jnp.int32, (grp * BLOCK, 1), 0)
    sink_cols = []
    for g in range(N_KV_HEADS_SWA):
        col = jnp.full((grp * BLOCK, 1), sink_ref[g * grp + grp - 1], F32)
        for hh in range(grp - 2, -1, -1):
            col = jnp.where(row < (hh + 1) * BLOCK, sink_ref[g * grp + hh], col)
        sink_cols.append(col)
    for blk in range(SWA_BLOCKS):
        rows = slice(blk * BLOCK, (blk + 1) * BLOCK)
        if blk == 0:
            kk = jnp.concatenate([kp_ref[...], kc_ref[rows, :]], axis=0)
            vv = jnp.concatenate([vp_ref[...], vc_ref[rows, :]], axis=0)
            bias = bias_first_ref
        else:
            prev_rows = slice((blk - 1) * BLOCK, (blk + 1) * BLOCK)
            kk = kc_ref[prev_rows, :]
            vv = vc_ref[prev_rows, :]
            bias = bias_ref
        tiles = _swa_block(sink_cols, q_ref[rows, :], kk, vv, bias, lo)
        for c, tile in enumerate(tiles):
            o_ref[rows, c * LANES:(c + 1) * LANES] = tile.astype(BF16)


def _swa(sinks, qa, kdup, va, bias):
    nb = SEQ // BLOCK
    ns = nb // SWA_BLOCKS
    cur = lambda b, i, s: (b * ns + i, 0)
    prev = lambda b, i, s: (b * nb + jnp.maximum(SWA_BLOCKS * i - 1, 0), 0)
    grid_spec = pltpu.PrefetchScalarGridSpec(
        num_scalar_prefetch=1,
        grid=(BATCH, ns),
        in_specs=[pl.BlockSpec((SWA_BLOCKS * BLOCK, Q_A), cur),
                  pl.BlockSpec((SWA_BLOCKS * BLOCK, 2 * KV_A), cur),
                  pl.BlockSpec((BLOCK, 2 * KV_A), prev),
                  pl.BlockSpec((SWA_BLOCKS * BLOCK, 2 * KV_A), cur),
                  pl.BlockSpec((BLOCK, 2 * KV_A), prev),
                  pl.BlockSpec((None,) + bias.shape[1:], lambda b, i, s: (jnp.minimum(i, 1), 0, 0, 0)),
                  pl.BlockSpec((None,) + bias.shape[1:], lambda b, i, s: (1, 0, 0, 0))],
        out_specs=pl.BlockSpec((SWA_BLOCKS * BLOCK, Q_A), cur))
    return pl.pallas_call(
        _swa_kernel,
        out_shape=jax.ShapeDtypeStruct((N_TOK, Q_A), BF16),
        grid_spec=grid_spec,
        compiler_params=_cparams(2),
        name="swa",
    )(sinks, qa, kdup, kdup, va, va, bias, bias)


def _fox_kernel(js_ref, q_ref, k_ref, v_ref, qa_ref, ka_ref, o_ref,
                kaug, vaug, q2, m_sc, acc_sc, s_0, s_1, s_2, s_3):
    tq, tk = TQ_FOX, TK_FOX
    b = pl.program_id(0)
    t = pl.program_id(1)
    g = pl.program_id(2)
    first_tile = (b * pl.num_programs(1) + t) * (SEQ // tq) + g * FOX_ROWS

    @pl.when(g == 0)
    def _():
        kaug[:, 0:LANES] = k_ref[...]
        kaug[:, LANES:2 * LANES] = ka_ref[...]
        vaug[:, 0:LANES] = v_ref[...]
        vaug[:, LANES:2 * LANES] = jnp.ones((SEQ, LANES), BF16)

    lane = lax.broadcasted_iota(jnp.int32, (tq, LANES), 1)
    lo = lane < HEAD_DIM
    base = 2 * DECAY_LANES * t
    own = [(lane >= base + h * DECAY_LANES) & (lane < base + (h + 1) * DECAY_LANES) for h in range(2)]
    for rw in range(FOX_ROWS):
        rows = slice(rw * tq, (rw + 1) * tq)
        q = q_ref[rows, :]
        qa = qa_ref[rows, :]
        zero = jnp.zeros_like(q)
        q2[rw, 0, :, 0:LANES] = jnp.where(lo, q, zero)
        q2[rw, 1, :, 0:LANES] = jnp.where(lo, zero, q)
        for h in range(2):
            q2[rw, h, :, LANES:2 * LANES] = jnp.where(own[h], qa, zero)

    rr = lax.broadcasted_iota(jnp.int32, (tq, tk), 0)
    cc = lax.broadcasted_iota(jnp.int32, (tq, tk), 1)
    causal = cc <= rr
    bufs = ((s_0, s_1), (s_2, s_3))
    j_starts = [js_ref[first_tile + rw] for rw in range(FOX_ROWS)]
    for rw in range(FOX_ROWS):
        nxt = rw + 1 if rw + 1 < FOX_ROWS else None
        _fox_query_tile(rw, g * FOX_ROWS + rw, j_starts[rw], bufs[rw % 2], q2, kaug, vaug, m_sc, acc_sc,
                        causal, lo, o_ref,
                        first_scores_done=rw > 0,
                        next_first=None if nxt is None else (nxt, j_starts[nxt], bufs[nxt % 2][0]))


def _fox_query_tile(rw, i, j_start, buf_pair, q2, kaug, vaug, m_sc, acc_sc, causal, lo, o_ref,
                    first_scores_done, next_first):
    tq, tk = TQ_FOX, TK_FOX
    s_a, s_b = buf_pair
    m_sc[...] = jnp.full(m_sc.shape, NEG_INF, F32)
    acc_sc[...] = jnp.zeros(acc_sc.shape, F32)

    def scores_of(row, h, ks):
        return lax.dot_general(q2[row, h], kaug[pl.ds(ks, tk), :], (((1,), (1,)), ((), ())),
                               preferred_element_type=F32)

    def scores(h, ks):
        return scores_of(rw, h, ks)

    def consume(h, s, ks, mask):
        if mask is not None:
            s = jnp.where(mask, s, NEG_INF)
        m_prev = m_sc[h]
        m_new = jnp.maximum(m_prev, jnp.max(s, axis=-1, keepdims=True))
        alpha = jnp.exp2(m_prev - m_new)
        p = jnp.exp2(s - jnp.concatenate([m_new] * (tk // LANES), axis=1))
        pv = jnp.dot(p.astype(BF16), vaug[pl.ds(ks, tk), :], preferred_element_type=F32)
        acc_sc[h] = jnp.concatenate([alpha, alpha], axis=1) * acc_sc[h] + pv
        m_sc[h] = m_new

    def key_start(j):
        return pl.multiple_of(j * tk, tk)

    def scores_into(buf, j):
        for h in range(2):
            buf[h] = scores(h, key_start(j))

    def consume_from(buf, j, mask):
        for h in range(2):
            consume(h, buf[h], key_start(j), mask)

    n_full = i - j_start

    if not first_scores_done:
        scores_into(s_a, j_start)

    def pair(p, carry):
        j = j_start + 2 * p
        scores_into(s_b, j + 1)
        consume_from(s_a, j, None)
        scores_into(s_a, j + 2)
        consume_from(s_b, j + 1, None)
        return carry

    lax.fori_loop(0, n_full // 2, pair, 0)
    odd = lax.rem(n_full, 2) == 1

    def start_next():
        if next_first is not None:
            nrow, nj, nbuf = next_first
            for h in range(2):
                nbuf[h] = scores_of(nrow, h, key_start(nj))

    @pl.when(odd)
    def _():
        scores_into(s_b, i)
        consume_from(s_a, i - 1, None)
        start_next()
        consume_from(s_b, i, causal)

    @pl.when(jnp.logical_not(odd))
    def _():
        start_next()
        consume_from(s_a, i, causal)

    outs = [acc_sc[h, :, 0:LANES] / acc_sc[h, :, LANES:2 * LANES] for h in range(2)]
    o_ref[rw * tq:(rw + 1) * tq, :] = jnp.where(lo, outs[0], outs[1]).astype(BF16)


def _fox(j_start, qb, kb, vb, qa, ka):
    tq = TQ_FOX
    ns = SEQ // (tq * FOX_ROWS)
    n_pairs = N_HEADS_FOX // 2
    qmap = lambda b, t, i, js: (b * ns + i, t)
    kmap = lambda b, t, i, js: (b, t)
    grid_spec = pltpu.PrefetchScalarGridSpec(
        num_scalar_prefetch=1,
        grid=(BATCH, n_pairs, ns),
        in_specs=[pl.BlockSpec((FOX_ROWS * tq, LANES), qmap),
                  pl.BlockSpec((SEQ, LANES), kmap),
                  pl.BlockSpec((SEQ, LANES), kmap),
                  pl.BlockSpec((None, FOX_ROWS * tq, LANES), lambda b, t, i, js: (b, i, 0)),
                  pl.BlockSpec((None, SEQ, LANES), lambda b, t, i, js: (b, 0, 0))],
        out_specs=pl.BlockSpec((FOX_ROWS * tq, LANES), qmap),
        scratch_shapes=[pltpu.VMEM((SEQ, 2 * LANES), BF16),
                        pltpu.VMEM((SEQ, 2 * LANES), BF16),
                        pltpu.VMEM((FOX_ROWS, 2, tq, 2 * LANES), BF16),
                        pltpu.VMEM((2, tq, LANES), F32),
                        pltpu.VMEM((2, tq, 2 * LANES), F32)]
        + [pltpu.VMEM((2, tq, TK_FOX), F32)] * 4)
    return pl.pallas_call(
        _fox_kernel,
        out_shape=jax.ShapeDtypeStruct((N_TOK, W_B), BF16),
        grid_spec=grid_spec,
        compiler_params=_cparams(3),
        name="fox",
    )(j_start, qb, kb, vb, qa, ka)


def _fox_first_tiles(nrm, fb):
    n_tiles = SEQ // TQ_FOX
    nr = nrm.reshape(BATCH, n_tiles, 8, LANES)[:, :, 0, :] * 1.02
    qn = jnp.sqrt(nr[..., 0:N_HEADS_FOX])
    kn = jnp.sqrt(nr[..., N_HEADS_FOX:2 * N_HEADS_FOX])
    f_first = fb[:, 0::2, 0:DECAY_LANES * N_HEADS_FOX:DECAY_LANES]
    f_last = fb[:, 1::2, 0:DECAY_LANES * N_HEADS_FOX:DECAY_LANES]
    kn_prefix = lax.cummax(kn, axis=1)
    upper = qn[:, :, None, :] * kn_prefix[:, None, :, :] + f_first[:, :, None, :] - f_last[:, None, :, :]
    row_max_low = -(qn * kn)[:, :, None, :]
    ii = jnp.arange(n_tiles)[None, :, None, None]
    jj = jnp.arange(n_tiles)[None, None, :, None]
    skip = (upper < row_max_low - PRUNE_MARGIN) & (jj < ii)
    skip = jnp.all(skip.reshape(BATCH, n_tiles, n_tiles, N_HEADS_FOX // 2, 2), axis=-1)
    first = jnp.min(jnp.where(skip, n_tiles, jj), axis=2)
    return jnp.transpose(first, (0, 2, 1)).reshape(-1).astype(jnp.int32)


def _post_kernel(x_ref, oa_ref, ob_ref, gt_ref, gm_ref, sc_ref, sh_ref, g_ref,
                 wa_ref, wb_ref, wo_ref, wr2_ref, br_ref,
                 x1_ref, xy_ref, rc_ref, cu_ref, hh_prev, lg_prev):
    step = pl.program_id(0)

    @pl.when(step == 0)
    def _():
        hh_prev[...] = jnp.zeros(hh_prev.shape, BF16)
        lg_prev[...] = jnp.zeros(lg_prev.shape, F32)

    @pl.when(step <= N_TOK_TILES)
    def _():
        hh_p = hh_prev[...]
        lg_p = lg_prev[...]
        hh, logits = _post_mix(x_ref, oa_ref, ob_ref, gt_ref, gm_ref, sc_ref, sh_ref, g_ref,
                               wa_ref, wb_ref, wo_ref, wr2_ref, br_ref, x1_ref)
        _post_route(hh_p, lg_p, xy_ref, rc_ref, cu_ref)
        hh_prev[...] = hh
        lg_prev[...] = logits

    @pl.when(step > N_TOK_TILES)
    def _():
        xy_ref[...] = jnp.zeros(xy_ref.shape, BF16)


def _post_mix(x_ref, oa_ref, ob_ref, gt_ref, gm_ref, sc_ref, sh_ref, g_ref,
              wa_ref, wb_ref, wo_ref, wr2_ref, br_ref, x1_ref):
    pa = jnp.dot(oa_ref[...], wa_ref[...], preferred_element_type=F32)
    pb = jnp.dot(ob_ref[...], wb_ref[...], preferred_element_type=F32)
    ga = jax.nn.sigmoid(gt_ref[:, 0:D_MODEL].astype(F32))
    gb = jax.nn.sigmoid(gt_ref[:, D_MODEL:2 * D_MODEL].astype(F32))
    merged = (ga * pa + gb * pb).astype(BF16)
    y = jnp.dot(merged, wo_ref[...], preferred_element_type=F32)
    x1 = x_ref[...] + gm_ref[...] * y
    x1_ref[...] = x1

    rs = lax.rsqrt(jnp.mean(x1 * x1, axis=-1, keepdims=True) + EPS)
    a = g_ref[...] * (1.0 + sc_ref[...])
    h2 = x1 * rs * a + sh_ref[...]

    hh = h2.astype(BF16)
    hl = (h2 - hh.astype(F32)).astype(BF16)
    hi_both = jnp.dot(hh, wr2_ref[...], preferred_element_type=F32)
    logits = (hi_both[:, 0:LANES] + hi_both[:, LANES:2 * LANES]
              + jnp.dot(hl, wr2_ref[:, 0:LANES], preferred_element_type=F32)
              + br_ref[...])
    return hh, logits


def _post_route(hh, logits, xy_ref, rc_ref, cu_ref):
    tm = TM_POST
    lane = lax.broadcasted_iota(jnp.int32, (tm, LANES), 1).astype(F32)
    big = float(LANES)
    gl = jnp.where(lane < N_GROUPS, logits, -jnp.inf)
    gmax = jnp.max(gl, axis=-1, keepdims=True)
    gi = jnp.min(jnp.where(gl == gmax, lane, big), axis=-1, keepdims=True)
    gsum = jnp.sum(jnp.exp(gl - gmax), axis=-1, keepdims=True)
    gp = 1.0 / gsum
    e_lo = N_GROUPS + EXPERTS_PER_GROUP * gi
    el = jnp.where((lane >= e_lo) & (lane < e_lo + EXPERTS_PER_GROUP), logits, -jnp.inf)
    v1 = jnp.max(el, axis=-1, keepdims=True)
    i1 = jnp.min(jnp.where(el == v1, lane, big), axis=-1, keepdims=True)
    el2 = jnp.where(lane == i1, -jnp.inf, el)
    v2 = jnp.max(el2, axis=-1, keepdims=True)
    i2 = jnp.min(jnp.where(el2 == v2, lane, big), axis=-1, keepdims=True)
    e21 = jnp.exp(v2 - v1)
    w1 = gp / (1.0 + e21)
    w2 = gp * e21 / (1.0 + e21)
    e1 = i1 - N_GROUPS
    e2 = i2 - N_GROUPS

    oh = jnp.where((lane == e1) | (lane == e2), 1.0, 0.0)
    cnt_u = jnp.floor((jnp.sum(oh, axis=0, keepdims=True) + (UNIT - 1)) * (1.0 / UNIT))
    r128 = lax.broadcasted_iota(jnp.int32, (LANES, LANES), 0)
    c128 = lax.broadcasted_iota(jnp.int32, (LANES, LANES), 1)
    before_lane = jnp.where(r128 < c128, 1.0, 0.0).astype(BF16)
    loc_u = jnp.dot(jnp.broadcast_to(cnt_u, (8, LANES)).astype(BF16), before_lane,
                    preferred_element_type=F32)
    trow = lax.broadcasted_iota(jnp.int32, (tm, LANES), 0)
    seen = oh
    k = 1
    while k < tm:
        if k < 8:
            shifted = jnp.where(trow >= k, pltpu.roll(seen, k, 0), 0.0)
        else:
            shifted = jnp.concatenate([jnp.zeros((k, LANES), F32), seen[:tm - k]], axis=0)
        seen = seen + shifted
        k *= 2
    pos_e = (seen - oh) + loc_u[0:1] * UNIT
    lp1 = jnp.sum(jnp.where(lane == e1, pos_e, 0.0), axis=-1, keepdims=True)
    lp2 = jnp.sum(jnp.where(lane == e2, pos_e, 0.0), axis=-1, keepdims=True)

    def to_row(col):
        return jnp.transpose(jnp.broadcast_to(col, (tm, LANES)))[0:1]

    srow = lax.broadcasted_iota(jnp.int32, (XY_ROWS, tm), 0).astype(F32)
    pm1 = jnp.where(srow == to_row(lp1), 1.0, 0.0).astype(BF16)
    pm2 = jnp.where(srow == to_row(lp2), 1.0, 0.0).astype(BF16)
    w1h = w1.astype(BF16).astype(F32)
    w2h = w2.astype(BF16).astype(F32)
    side = jnp.where(lane == W1_LANES[0], w1h, jnp.where(lane == W1_LANES[1], w1 - w1h,
           jnp.where(lane == W2_LANES[0], w2h, jnp.where(lane == W2_LANES[1], w2 - w2h,
           jnp.where(lane == E1_LANE, e1, jnp.where(lane == E2_LANE, e2, 0.0))))))
    tok = jnp.concatenate([hh, side.astype(BF16)], axis=1)
    xy_ref[...] = jnp.dot(pm1 + pm2, tok, preferred_element_type=F32).astype(BF16)

    cu_ref[...] = jnp.broadcast_to(cnt_u, cu_ref.shape)
    rc_ref[...] = jnp.where(lane == 0, lp1, jnp.where(lane == 1, lp2, 0.0))


def _post(x2, oa, ob, gates, gate_m, scale_f, shift_f, g_ffn, wa, wb, wo, wr2, b_r):
    tm = TM_POST
    tpb = SEQ // tm
    n_steps = N_TOK_TILES
    row = lambda i: (jnp.minimum(i, n_steps - 1), 0)
    per_b = lambda i: (jnp.minimum(i, n_steps - 1) // tpb, 0, 0)
    routed = lambda i: (jnp.clip(i - 1, 0, n_steps - 1), 0)
    const = lambda i: (0, 0)
    return pl.pallas_call(
        _post_kernel,
        out_shape=[jax.ShapeDtypeStruct((N_TOK, D_MODEL), F32),
                   jax.ShapeDtypeStruct(((n_steps + PAD_BLOCKS) * XY_ROWS, XY_COLS), BF16),
                   jax.ShapeDtypeStruct((N_TOK, LANES), F32),
                   jax.ShapeDtypeStruct((n_steps * 8, LANES), F32)],
        grid=(n_steps + 1 + PAD_BLOCKS,),
        in_specs=[pl.BlockSpec((tm, D_MODEL), row),
                  pl.BlockSpec((tm, Q_A), row),
                  pl.BlockSpec((tm, W_B), row),
                  pl.BlockSpec((tm, 2 * D_MODEL), row),
                  pl.BlockSpec((None, 1, D_MODEL), per_b),
                  pl.BlockSpec((None, 1, D_MODEL), per_b),
                  pl.BlockSpec((None, 1, D_MODEL), per_b),
                  pl.BlockSpec((1, D_MODEL), const),
                  pl.BlockSpec(wa.shape, const),
                  pl.BlockSpec(wb.shape, const),
                  pl.BlockSpec(wo.shape, const),
                  pl.BlockSpec(wr2.shape, const),
                  pl.BlockSpec((1, LANES), const)],
        out_specs=[pl.BlockSpec((tm, D_MODEL), row),
                   pl.BlockSpec((XY_ROWS, XY_COLS), lambda i: (jnp.maximum(i - 1, 0), 0)),
                   pl.BlockSpec((tm, LANES), routed),
                   pl.BlockSpec((8, LANES), routed)],
        scratch_shapes=[pltpu.VMEM((tm, D_MODEL), BF16),
                        pltpu.VMEM((tm, LANES), F32)],
        compiler_params=_cparams(1),
        name="post",
    )(x2, oa, ob, gates, gate_m, scale_f, shift_f, g_ffn, wa, wb, wo, wr2, b_r)


def _experts_kernel(te_ref, nu_ref, ur_ref, ne_ref, ep_ref, xy_in, wg_hbm, wu_hbm, wd_hbm, xy_out,
                    xbuf, ybuf, wg_s, wu_s, wd_s, wg_f, wu_f, wd_f, gsem, ssem, wsem):
    del xy_in
    r = pl.program_id(0)
    last = pl.num_programs(0) - 1
    n_used = nu_ref[0]
    slot = lax.rem(r, 2)

    def unit_row(step, s):
        return pl.multiple_of(ur_ref[step * UNITS_PER_TILE + s], UNIT)

    def start_gathers(step, sl):
        for s in range(UNITS_PER_TILE):
            pltpu.make_async_copy(xy_out.at[pl.ds(unit_row(step, s), UNIT), :],
                                  xbuf.at[sl, pl.ds(s * UNIT, UNIT), :], gsem.at[sl]).start()

    def wait_gathers(sl):
        pltpu.make_async_copy(xy_out.at[pl.ds(0, TM_EXP), :], xbuf.at[sl], gsem.at[sl]).wait()

    def start_scatters(step, sl):
        for s in range(UNITS_PER_TILE):
            pltpu.make_async_copy(ybuf.at[sl, pl.ds(s * UNIT, UNIT), :],
                                  xy_out.at[pl.ds(unit_row(step, s), UNIT), pl.ds(0, D_MODEL)],
                                  ssem.at[sl]).start()

    def wait_scatters(sl):
        pltpu.make_async_copy(ybuf.at[sl], xy_out.at[pl.ds(0, TM_EXP), pl.ds(0, D_MODEL)], ssem.at[sl]).wait()

    @pl.when(r == 0)
    def _():
        start_gathers(0, 0)

    @pl.when(jnp.logical_and(r < n_used, r >= 2))
    def _():
        wait_scatters(slot)

    def weight_copies(e, p):
        return [pltpu.make_async_copy(src.at[e], dst.at[p], wsem.at[p])
                for src, dst in ((wg_hbm, wg_f), (wu_hbm, wu_f), (wd_hbm, wd_f))]

    @pl.when(jnp.logical_and(r < n_used,
                             jnp.logical_or(r == 0, te_ref[r] != te_ref[jnp.maximum(r - 1, 0)])))
    def _():
        e = te_ref[r]
        p = ep_ref[r]

        @pl.when(r == 0)
        def _():
            for cp in weight_copies(e, p):
                cp.start()

        for cp in weight_copies(e, p):
            cp.wait()
        wg_s[...] = wg_f[p].astype(BF16)
        wu_s[...] = wu_f[p].astype(BF16)
        wd_s[...] = wd_f[p].astype(BF16)

        @pl.when(ne_ref[r] >= 0)
        def _():
            for cp in weight_copies(ne_ref[r], 1 - p):
                cp.start()

    @pl.when(r < n_used)
    def _():
        wait_gathers(slot)
        start_gathers(jnp.minimum(r + 1, last), 1 - slot)
        x = xbuf[slot, :, 0:D_MODEL]
        side = xbuf[slot, :, D_MODEL:XY_COLS].astype(F32)
        lane = lax.broadcasted_iota(jnp.int32, side.shape, 1)

        def lanes_sum(a, b):
            return jnp.sum(jnp.where((lane == a) | (lane == b), side, 0.0), axis=-1, keepdims=True)

        is_slot1 = lanes_sum(E1_LANE, E1_LANE) == te_ref[r].astype(F32)
        wrow = jnp.where(is_slot1, lanes_sum(*W1_LANES), lanes_sum(*W2_LANES))
        a = jnp.dot(x, wg_s[...], preferred_element_type=F32)
        u = jnp.dot(x, wu_s[...], preferred_element_type=F32)
        hid = (a * jax.nn.sigmoid(a) * u * wrow).astype(BF16)
        ybuf[slot] = jnp.dot(hid, wd_s[...], preferred_element_type=F32).astype(BF16)
        start_scatters(r, slot)

    @pl.when(r == n_used - 1)
    def _():
        wait_gathers(1 - slot)
        wait_scatters(slot)

        @pl.when(r >= 1)
        def _():
            wait_scatters(1 - slot)


def _experts(tile_expert, n_used, unit_rows, next_expert, expert_parity, xy, wg, wu, wd):
    grid_spec = pltpu.PrefetchScalarGridSpec(
        num_scalar_prefetch=5,
        grid=(N_EXP_TILES,),
        in_specs=[pl.BlockSpec(memory_space=pl.ANY),
                  pl.BlockSpec(memory_space=pl.ANY),
                  pl.BlockSpec(memory_space=pl.ANY),
                  pl.BlockSpec(memory_space=pl.ANY)],
        out_specs=pl.BlockSpec(memory_space=pl.ANY),
        scratch_shapes=[pltpu.VMEM((2, TM_EXP, XY_COLS), BF16),
                        pltpu.VMEM((2, TM_EXP, D_MODEL), BF16),
                        pltpu.VMEM((D_MODEL, D_FF_EXPERT), BF16),
                        pltpu.VMEM((D_MODEL, D_FF_EXPERT), BF16),
                        pltpu.VMEM((D_FF_EXPERT, D_MODEL), BF16),
                        pltpu.VMEM((2, D_MODEL, D_FF_EXPERT), F32),
                        pltpu.VMEM((2, D_MODEL, D_FF_EXPERT), F32),
                        pltpu.VMEM((2, D_FF_EXPERT, D_MODEL), F32),
                        pltpu.SemaphoreType.DMA((2,)),
                        pltpu.SemaphoreType.DMA((2,)),
                        pltpu.SemaphoreType.DMA((2,))])
    return pl.pallas_call(
        _experts_kernel,
        out_shape=jax.ShapeDtypeStruct(xy.shape, xy.dtype),
        grid_spec=grid_spec,
        input_output_aliases={5: 0},
        compiler_params=_cparams(1),
        name="experts",
    )(tile_expert, n_used, unit_rows, next_expert, expert_parity, xy, wg, wu, wd)


def _combine_kernel(x1_ref, rc_ref, gf_ref, gfin_ref, y_ref, o_ref):
    for tl in range(COMBINE_TILES):
        rows = slice(tl * TM_ROW, (tl + 1) * TM_ROW)
        lp1 = rc_ref[rows, 0:1]
        lp2 = rc_ref[rows, 1:2]
        scol = lax.broadcasted_iota(jnp.int32, (TM_ROW, XY_ROWS), 1).astype(F32)
        pick = jnp.where((scol == lp1) | (scol == lp2), 1.0, 0.0).astype(BF16)
        y = jnp.dot(pick, y_ref[tl * XY_ROWS:(tl + 1) * XY_ROWS, :], preferred_element_type=F32)
        xf = x1_ref[rows, :] + gf_ref[...] * y
        rs = lax.rsqrt(jnp.mean(xf * xf, axis=-1, keepdims=True) + EPS)
        o_ref[rows, :] = xf * rs * gfin_ref[...]


def _combine(x1, rcol, gate_f, g_final, xy):
    tm = TM_ROW * COMBINE_TILES
    tpb = SEQ // tm
    row = lambda i: (i, 0)
    return pl.pallas_call(
        _combine_kernel,
        out_shape=jax.ShapeDtypeStruct((N_TOK, D_MODEL), F32),
        grid=(N_TOK // tm,),
        in_specs=[pl.BlockSpec((tm, D_MODEL), row),
                  pl.BlockSpec((tm, LANES), row),
                  pl.BlockSpec((None, 1, D_MODEL), lambda i: (i // tpb, 0, 0)),
                  pl.BlockSpec((1, D_MODEL), lambda i: (0, 0)),
                  pl.BlockSpec((COMBINE_TILES * XY_ROWS, D_MODEL), row)],
        out_specs=pl.BlockSpec((tm, D_MODEL), row),
        compiler_params=_cparams(1),
        name="combine",
    )(x1, rcol, gate_f, g_final, xy)


def _t5_bucket_np():
    qi = np.arange(BLOCK)[:, None]
    kj = np.arange(2 * BLOCK)[None, :]
    dist = qi - kj + BLOCK
    n = np.maximum(dist, 0)
    max_exact = NUM_BUCKETS // 2
    nf = np.maximum(n, 1).astype(np.float32)
    large = max_exact + (np.log(nf / np.float32(max_exact)) / np.float32(math.log(MAX_DISTANCE / max_exact))
                         * np.float32(NUM_BUCKETS - max_exact)).astype(np.int32)
    large = np.minimum(large, NUM_BUCKETS - 1)
    bucket = np.where(n < max_exact, n, large)
    band = (dist >= 0) & (dist < WINDOW)
    return bucket.astype(np.int32), band


def kernel(x, c, w_ada, b_ada, g_norm_mix, g_norm_ffn, w_in, sinks, b_forget, w_proj_swa, w_proj_fox,
           w_out, rel_bias_table, w_router_group, b_router_group, w_router_expert, b_router_expert,
           w_gate_exp, w_up_exp, w_down_exp, g_final):
    l = 0
    x2 = x.reshape(N_TOK, D_MODEL)

    c16 = jnp.concatenate([c, jnp.zeros_like(c)], axis=0)
    mod = _ada(c16, w_ada[l], b_ada[l][None, :])[:BATCH]
    shift_m, scale_m, gate_m, shift_f, scale_f, gate_f = [
        m.reshape(BATCH, 1, D_MODEL) for m in jnp.split(mod, 6, axis=-1)]

    w = w_in[l]
    o_ka, o_va, o_qb = Q_A, Q_A + KV_A, Q_A + 2 * KV_A
    o_kb, o_vb, o_f = o_qb + W_B, o_qb + 2 * W_B, o_qb + 3 * W_B
    o_g = o_f + N_HEADS_FOX

    def dup(cols):
        heads = [cols[:, h * HEAD_DIM:(h + 1) * HEAD_DIM] for h in range(N_KV_HEADS_SWA)]
        return jnp.concatenate([hd for hd in heads for _ in range(2)], axis=1)

    head_order = jnp.argsort(b_forget[l])

    def reorder_heads(cols):
        return jnp.take(cols.reshape(D_MODEL, N_HEADS_FOX, HEAD_DIM), head_order, axis=1).reshape(D_MODEL, W_B)

    w_fox = jnp.concatenate([reorder_heads(w[:, o_qb:o_kb]), reorder_heads(w[:, o_kb:o_vb]),
                             reorder_heads(w[:, o_vb:o_f])], axis=1)
    b_fox = jnp.take(b_forget[l], head_order)
    w_proj_b = jnp.take(w_proj_fox[l].reshape(N_HEADS_FOX, HEAD_DIM, D_MODEL), head_order, axis=0).reshape(W_B, D_MODEL)
    carrier = DECAY_LANES * N_HEADS_FOX
    w_f = jnp.pad(jnp.repeat(jnp.take(w[:, o_f:o_g], head_order, axis=1), DECAY_LANES, axis=1),
                  ((0, 0), (0, LANES - carrier)))
    w_main = jnp.concatenate([w[:, :Q_A], dup(w[:, o_ka:o_va]), w[:, o_va:o_qb], w_f, w_fox], axis=1).astype(BF16)
    w_g = w[:, o_g:].astype(BF16)
    qa, kdup, va, qb, kb, vb, f_pad, gates, nrm = _inproj(
        x2, scale_m, shift_m, g_norm_mix[l][None, :], w_main, w_g)

    b_pad = jnp.pad(jnp.repeat(b_fox, DECAY_LANES), (0, LANES - carrier))[None, :]
    lanes = np.arange(LANES)
    jmod = jnp.asarray(np.where(lanes < DECAY_LANES * N_HEADS_FOX, lanes % DECAY_LANES, 7)[None, :].astype(np.int32))
    dq, dk, fb = _cum(f_pad, b_pad, jmod)

    bucket, band = _t5_bucket_np()
    onehot = jnp.asarray(bucket[None] == np.arange(NUM_BUCKETS)[:, None, None], dtype=F32)
    bias = jnp.einsum("bh,bqk->hqk", rel_bias_table.astype(F32), onehot, precision=HIGHEST)
    bias = jnp.where(band[None], bias * LOG2E, NEG_INF)
    first = np.arange(2 * BLOCK)[None, None, :] < BLOCK
    bias = jnp.stack([jnp.where(first, NEG_INF, bias), bias]).reshape(2, N_KV_HEADS_SWA, -1, 2 * BLOCK)
    o_a = _swa(sinks[l].astype(F32) * LOG2E, qa, kdup, va, bias)

    o_b = _fox(_fox_first_tiles(nrm, fb), qb, kb, vb, dq, dk)

    w_r = jnp.concatenate([w_router_group[l]] + [w_router_expert[l][g] for g in range(N_GROUPS)], axis=1)
    w_r = jnp.pad(w_r, ((0, 0), (0, LANES - w_r.shape[1])))
    wr_hi = w_r.astype(BF16)
    wr_lo = (w_r - wr_hi.astype(F32)).astype(BF16)
    wr2 = jnp.concatenate([wr_hi, wr_lo], axis=1)
    b_r = jnp.concatenate([b_router_group[l], b_router_expert[l].reshape(-1)])
    b_r = jnp.pad(b_r, (0, LANES - b_r.shape[0]))[None, :]
    x1, xy, rcol, cu = _post(x2, o_a, o_b, gates, gate_m, scale_f, shift_f, g_norm_ffn[l][None, :],
                             w_proj_swa[l].astype(BF16), w_proj_b.astype(BF16), w_out[l].astype(BF16),
                             wr2, b_r)

    i32 = jnp.int32
    n_tok_tiles = N_TOK // TM_POST
    cu = cu.reshape(n_tok_tiles, 8, LANES)[:, 0, :N_EXPERTS].astype(i32)
    loc_u = jnp.cumsum(cu, axis=1) - cu
    cend = jnp.cumsum(cu, axis=0)
    cstart = cend - cu
    tot_u = cend[-1]
    tiles_e = (tot_u + UNITS_PER_TILE - 1) // UNITS_PER_TILE
    tile_end = jnp.cumsum(tiles_e)
    tile_start = tile_end - tiles_e
    r = jnp.arange(N_EXP_TILES, dtype=i32)
    tile_expert = jnp.minimum(jnp.sum((tile_end[None, :] <= r[:, None]).astype(i32), axis=1), N_EXPERTS - 1)
    sel_e = tile_expert[:, None] == jnp.arange(N_EXPERTS, dtype=i32)[None, :]
    tw = r - jnp.sum(jnp.where(sel_e, tile_start[None, :], 0), axis=1)
    tot_r = jnp.sum(jnp.where(sel_e, tot_u[None, :], 0), axis=1)
    n_used = tile_end[-1:].astype(i32)
    q = tw[:, None] * UNITS_PER_TILE + jnp.arange(UNITS_PER_TILE, dtype=i32)[None, :]

    def of_expert(tab):
        return jnp.sum(jnp.where(sel_e[:, None, :], tab[None, :, :], 0), axis=2)

    cend_r, cstart_r, loc_r = of_expert(cend), of_expert(cstart), of_expert(loc_u)
    src_tile = jnp.minimum(jnp.sum((cend_r[:, None, :] <= q[:, :, None]).astype(i32), axis=2), n_tok_tiles - 1)
    sel_t = src_tile[:, :, None] == jnp.arange(n_tok_tiles, dtype=i32)[None, None, :]
    k = (q - jnp.sum(jnp.where(sel_t, cstart_r[:, None, :], 0), axis=2)
         + jnp.sum(jnp.where(sel_t, loc_r[:, None, :], 0), axis=2))
    real_rows = src_tile * XY_ROWS + k * UNIT
    used = tiles_e > 0
    used_index = jnp.cumsum(used.astype(i32)) - used.astype(i32)
    pool_r = jnp.sum(jnp.where(sel_e, (used_index % PAD_POOLS)[None, :], 0), axis=1)
    pad_rows = PAD_BASE_ROW + (pool_r[:, None] * PAD_UNITS_PER_EXPERT + (q - tot_r[:, None])) * UNIT
    idle_row = PAD_BASE_ROW + PAD_POOLS * PAD_UNITS_PER_EXPERT * UNIT
    unit_rows = jnp.where(q < tot_r[:, None], real_rows, pad_rows)
    unit_rows = jnp.where((r < n_used)[:, None], unit_rows, idle_row).reshape(-1).astype(i32)

    eid = jnp.arange(N_EXPERTS, dtype=i32)
    later_used = (eid[None, :] > eid[:, None]) & used[None, :]
    next_e = jnp.min(jnp.where(later_used, eid[None, :], N_EXPERTS), axis=1)
    next_e = jnp.where(next_e == N_EXPERTS, -1, next_e)
    parity_e = used_index % 2
    next_expert = jnp.sum(jnp.where(sel_e, next_e[None, :], 0), axis=1).astype(i32)
    expert_parity = jnp.sum(jnp.where(sel_e, parity_e[None, :], 0), axis=1).astype(i32)

    xy = _experts(tile_expert.astype(i32), n_used, unit_rows, next_expert, expert_parity, xy,
                  w_gate_exp[l].reshape(N_EXPERTS, D_MODEL, D_FF_EXPERT),
                  w_up_exp[l].reshape(N_EXPERTS, D_MODEL, D_FF_EXPERT),
                  w_down_exp[l].reshape(N_EXPERTS, D_FF_EXPERT, D_MODEL))
    out = _combine(x1, rcol, gate_f, g_final[None, :], xy)
    return out.reshape(BATCH, SEQ, D_MODEL)
```
